```python
import jax, jax.numpy as jnp
from jax import lax
import numpy as np

D_MODEL = 2048
BATCH = 8
SEQ = 4096
DEPTH = 2

MEM_LEN = 256
EPS = 1e-6
HEAD_DIM = 64
N_Q_HEADS = 16
N_KV_HEADS = 2
Q_PER_KV = N_Q_HEADS // N_KV_HEADS
ATTN_WIDTH = N_Q_HEADS * HEAD_DIM
KV_WIDTH = N_KV_HEADS * HEAD_DIM
WINDOW = 128
BLOCK = 128
ROPE_DIM = HEAD_DIM // 4
ROPE_THETA = 500000.0
SGU_GROUPS = 8
SGU_WIDTH = D_MODEL // 2
SGU_GROUP_DIM = SGU_WIDTH // SGU_GROUPS
CHUNK = 128
IN_WIDTH = ATTN_WIDTH + 2 * KV_WIDTH + 2 * SGU_WIDTH
MIX_WIDTH = ATTN_WIDTH + SGU_WIDTH
POOL_WINDOWS = (2, 4, 8, 16)
N_POOL_GROUPS = len(POOL_WINDOWS)
POOL_GROUP_DIM = D_MODEL // N_POOL_GROUPS
X_HEADS = 4
X_HEAD_DIM = 128
X_WIDTH = X_HEADS * X_HEAD_DIM
D_FF = 5632
N_NORMS = 8
N_EVEN = (DEPTH + 1) // 2
N_ODD = DEPTH // 2

kernel_name = "hybrid_swa_sgu_pool_macaron"


def rms_norm(x, g):
    xf = x.astype(jnp.float32)
    y = xf * lax.rsqrt(jnp.mean(xf * xf, axis=-1, keepdims=True) + EPS)
    return (y * g.astype(jnp.float32)).astype(x.dtype)


def swiglu(h, wg, wu, wd):
    return (jax.nn.silu(h @ wg) * (h @ wu)) @ wd


def rope_tables(seq):
    half = ROPE_DIM // 2
    inv = ROPE_THETA ** (-jnp.arange(half, dtype=jnp.float32) * 2.0 / ROPE_DIM)
    ang = jnp.arange(seq, dtype=jnp.float32)[:, None] * inv[None, :]
    return jnp.cos(ang)[:, None, :], jnp.sin(ang)[:, None, :]


def partial_rope(x, cos, sin):
    xf = x.astype(jnp.float32)
    half = ROPE_DIM // 2
    x1 = xf[..., :half]
    x2 = xf[..., half:ROPE_DIM]
    rot = jnp.concatenate([x1 * cos - x2 * sin, x2 * cos + x1 * sin, xf[..., ROPE_DIM:]], axis=-1)
    return rot.astype(x.dtype)


def swa_sink_attention(q, k, v, sinks):
    b, s = q.shape[0], q.shape[1]
    nb = s // BLOCK
    qb = q.reshape(b, nb, BLOCK, N_KV_HEADS, Q_PER_KV, HEAD_DIM)
    pad = ((0, 0), (BLOCK, 0), (0, 0), (0, 0))
    kp = jnp.pad(k, pad).reshape(b, nb + 1, BLOCK, N_KV_HEADS, HEAD_DIM)
    vp = jnp.pad(v, pad).reshape(b, nb + 1, BLOCK, N_KV_HEADS, HEAD_DIM)
    kb = jnp.concatenate([kp[:, :-1], kp[:, 1:]], axis=2)
    vb = jnp.concatenate([vp[:, :-1], vp[:, 1:]], axis=2)
    scores = jnp.einsum('bnqhgd,bnkhd->bnhgqk', qb, kb,
                        preferred_element_type=jnp.float32) * (HEAD_DIM ** -0.5)
    qi = jnp.arange(BLOCK)[:, None]
    kj = jnp.arange(2 * BLOCK)[None, :]
    rel = qi + BLOCK - kj
    band = (rel >= 0) & (rel < WINDOW)
    not_pad = (jnp.arange(nb)[:, None, None] > 0) | (kj >= BLOCK)[None]
    valid = band[None] & not_pad
    scores = jnp.where(valid[None, :, None, None], scores, jnp.float32(-1e30))
    sink = jnp.broadcast_to(
        sinks.astype(jnp.float32).reshape(N_KV_HEADS, Q_PER_KV)[None, None, :, :, None, None],
        scores.shape[:-1] + (1,))
    probs = jax.nn.softmax(jnp.concatenate([scores, sink], axis=-1), axis=-1)[..., :-1]
    out = jnp.einsum('bnhgqk,bnkhd->bnqhgd', probs.astype(v.dtype), vb)
    return out.reshape(b, s, ATTN_WIDTH)


def chunked_spatial_gating(u, v, ln_g, ln_b, w_s, b_s):
    b, s = u.shape[0], u.shape[1]
    nc = s // CHUNK
    vf = v.astype(jnp.float32)
    mu = jnp.mean(vf, axis=-1, keepdims=True)
    var = jnp.mean(jnp.square(vf - mu), axis=-1, keepdims=True)
    vn = ((vf - mu) * lax.rsqrt(var + EPS) * ln_g.astype(jnp.float32) + ln_b.astype(jnp.float32)).astype(v.dtype)
    vc = vn.reshape(b, nc, CHUNK, SGU_GROUPS, SGU_GROUP_DIM)
    causal = jnp.tril(jnp.ones((CHUNK, CHUNK), dtype=w_s.dtype))
    mixed = jnp.einsum('gij,bnjgc->bnigc', w_s * causal[None], vc) \
        + jnp.transpose(b_s)[None, None, :, :, None]
    return u * mixed.reshape(b, s, SGU_WIDTH).astype(u.dtype)


def attn_sgu_mixer(h, w_in, w_out, sinks, ln_g, ln_b, w_s, b_s, cos, sin):
    b, s = h.shape[0], h.shape[1]
    z = h @ w_in
    o1 = ATTN_WIDTH
    o2 = o1 + KV_WIDTH
    o3 = o2 + KV_WIDTH
    o4 = o3 + SGU_WIDTH
    q = partial_rope(z[..., :o1].reshape(b, s, N_Q_HEADS, HEAD_DIM), cos, sin)
    k = partial_rope(z[..., o1:o2].reshape(b, s, N_KV_HEADS, HEAD_DIM), cos, sin)
    v = z[..., o2:o3].reshape(b, s, N_KV_HEADS, HEAD_DIM)
    attn = swa_sink_attention(q, k, v, sinks)
    gate = chunked_spatial_gating(jax.nn.gelu(z[..., o3:o4]), jax.nn.gelu(z[..., o4:]), ln_g, ln_b, w_s, b_s)
    return jnp.concatenate([attn, gate], axis=-1) @ w_out


def multiscale_pool_mixer(h, pool_w, pool_scale):
    b, s = h.shape[0], h.shape[1]
    hf = h.astype(jnp.float32).reshape(b, s, N_POOL_GROUPS, POOL_GROUP_DIM)
    cs = jnp.cumsum(hf, axis=1)
    count = jnp.arange(1, s + 1, dtype=jnp.float32)
    outs = []
    for gi, w in enumerate(POOL_WINDOWS):
        c = cs[:, :, gi]
        prev = jnp.pad(c, ((0, 0), (w, 0), (0, 0)))[:, :s]
        mean = (c - prev) / jnp.minimum(count, jnp.float32(w))[None, :, None]
        outs.append(mean - hf[:, :, gi])
    pooled = jnp.stack(outs, axis=2).astype(h.dtype)
    y = jnp.einsum('bsgc,gcd->bsgd', pooled, pool_w).reshape(b, s, D_MODEL)
    return y * pool_scale


def memory_cross_attention(h, mem_n, wq, wk, wv, wo):
    b, s = h.shape[0], h.shape[1]
    m = mem_n.shape[1]
    q = (h @ wq).reshape(b, s, X_HEADS, X_HEAD_DIM)
    k = (mem_n @ wk).reshape(b, m, X_HEADS, X_HEAD_DIM)
    v = (mem_n @ wv).reshape(b, m, X_HEADS, X_HEAD_DIM)
    sc = jnp.einsum('bshd,bmhd->bhsm', q, k, preferred_element_type=jnp.float32) * (X_HEAD_DIM ** -0.5)
    p = jax.nn.softmax(sc, axis=-1)
    o = jnp.einsum('bhsm,bmhd->bshd', p.astype(v.dtype), v).reshape(b, s, X_WIDTH)
    return o @ wo


def _fwd_setup_inputs(seed: int = 0) -> dict:
    key = jax.random.key(seed)
    ks = jax.random.split(key, 24)
    f32 = jnp.float32

    def w(k, shape, fan_in):
        return jax.random.normal(k, shape, f32) * (fan_in ** -0.5)

    return {
        "x": jax.random.normal(ks[0], (BATCH, SEQ, D_MODEL), f32),
        "mem": jax.random.normal(ks[1], (BATCH, MEM_LEN, D_MODEL), f32),
        "norms": 1.0 + 0.1 * jax.random.normal(ks[2], (DEPTH, N_NORMS, D_MODEL), f32),
        "mem_norm": 1.0 + 0.1 * jax.random.normal(ks[3], (DEPTH, D_MODEL), f32),
        "ffn1_wg": w(ks[4], (DEPTH, D_MODEL, D_FF), D_MODEL),
        "ffn1_wu": w(ks[5], (DEPTH, D_MODEL, D_FF), D_MODEL),
        "ffn1_wd": w(ks[6], (DEPTH, D_FF, D_MODEL), D_FF),
        "ffn2_wg": w(ks[7], (DEPTH, D_MODEL, D_FF), D_MODEL),
        "ffn2_wu": w(ks[8], (DEPTH, D_MODEL, D_FF), D_MODEL),
        "ffn2_wd": w(ks[9], (DEPTH, D_FF, D_MODEL), D_FF),
        "x_wq": w(ks[10], (DEPTH, D_MODEL, X_WIDTH), D_MODEL),
        "x_wk": w(ks[11], (DEPTH, D_MODEL, X_WIDTH), D_MODEL),
        "x_wv": w(ks[12], (DEPTH, D_MODEL, X_WIDTH), D_MODEL),
        "x_wo": w(ks[13], (DEPTH, X_WIDTH, D_MODEL), X_WIDTH),
        "mix_w_in": w(ks[14], (N_EVEN, D_MODEL, IN_WIDTH), D_MODEL),
        "mix_w_out": w(ks[15], (N_EVEN, MIX_WIDTH, D_MODEL), MIX_WIDTH),
        "attn_sinks": 0.5 * jax.random.normal(ks[16], (N_EVEN, N_Q_HEADS), f32),
        "sgu_ln_g": 1.0 + 0.1 * jax.random.normal(ks[17], (N_EVEN, SGU_WIDTH), f32),
        "sgu_ln_b": 0.02 * jax.random.normal(ks[18], (N_EVEN, SGU_WIDTH), f32),
        "sgu_w": w(ks[19], (N_EVEN, SGU_GROUPS, CHUNK, CHUNK), CHUNK),
        "sgu_b": 1.0 + 0.1 * jax.random.normal(ks[20], (N_EVEN, SGU_GROUPS, CHUNK), f32),
        "pool_w": w(ks[21], (N_ODD, N_POOL_GROUPS, POOL_GROUP_DIM, POOL_GROUP_DIM), POOL_GROUP_DIM),
        "pool_scale": 1.0 + 0.2 * jax.random.normal(ks[22], (N_ODD, D_MODEL), f32),
    }


def _fwd_reference(x, mem, norms, mem_norm, ffn1_wg, ffn1_wu, ffn1_wd, ffn2_wg, ffn2_wu, ffn2_wd,
              x_wq, x_wk, x_wv, x_wo, mix_w_in, mix_w_out, attn_sinks, sgu_ln_g, sgu_ln_b,
              sgu_w, sgu_b, pool_w, pool_scale):
    cos, sin = rope_tables(x.shape[1])
    for layer in range(DEPTH):
        g = norms[layer]
        h = rms_norm(x, g[0])
        x = x + 0.5 * rms_norm(swiglu(h, ffn1_wg[layer], ffn1_wu[layer], ffn1_wd[layer]), g[1])
        h = rms_norm(x, g[2])
        i = layer // 2
        if layer % 2 == 0:
            m = attn_sgu_mixer(h, mix_w_in[i], mix_w_out[i], attn_sinks[i], sgu_ln_g[i], sgu_ln_b[i],
                               sgu_w[i], sgu_b[i], cos, sin)
        else:
            m = multiscale_pool_mixer(h, pool_w[i], pool_scale[i])
        x = x + rms_norm(m, g[3])
        h = rms_norm(x, g[4])
        mem_n = rms_norm(mem, mem_norm[layer])
        x = x + rms_norm(memory_cross_attention(h, mem_n, x_wq[layer], x_wk[layer], x_wv[layer], x_wo[layer]), g[5])
        h = rms_norm(x, g[6])
        x = x + 0.5 * rms_norm(swiglu(h, ffn2_wg[layer], ffn2_wu[layer], ffn2_wd[layer]), g[7])
    return x


import jax as _jax
import jax.numpy as _jnp

TWIN_FORMAT = 'train_step'
FWD_PARAMS = ['x', 'mem', 'norms', 'mem_norm', 'ffn1_wg', 'ffn1_wu', 'ffn1_wd', 'ffn2_wg', 'ffn2_wu', 'ffn2_wd', 'x_wq', 'x_wk', 'x_wv', 'x_wo', 'mix_w_in', 'mix_w_out', 'attn_sinks', 'sgu_ln_g', 'sgu_ln_b', 'sgu_w', 'sgu_b', 'pool_w', 'pool_scale']
TWIN_WEIGHTS = ['norms', 'mem_norm', 'ffn1_wg', 'ffn1_wu', 'ffn1_wd', 'ffn2_wg', 'ffn2_wu', 'ffn2_wd', 'x_wq', 'x_wk', 'x_wv', 'x_wo', 'mix_w_in', 'mix_w_out', 'attn_sinks', 'sgu_ln_g', 'sgu_ln_b', 'sgu_w', 'sgu_b', 'pool_w', 'pool_scale']
TWIN_DIFF_INPUT = 'x'
TWIN_INPUTS = ['x', 'mem', 'norms', 'mem_norm', 'ffn1_wg', 'ffn1_wu', 'ffn1_wd', 'ffn2_wg', 'ffn2_wu', 'ffn2_wd', 'x_wq', 'x_wk', 'x_wv', 'x_wo', 'mix_w_in', 'mix_w_out', 'attn_sinks', 'sgu_ln_g', 'sgu_ln_b', 'sgu_w', 'sgu_b', 'pool_w', 'pool_scale', 'loss_target', 'm_norms', 'm_mem_norm', 'm_ffn1_wg', 'm_ffn1_wu', 'm_ffn1_wd', 'm_ffn2_wg', 'm_ffn2_wu', 'm_ffn2_wd', 'm_x_wq', 'm_x_wk', 'm_x_wv', 'm_x_wo', 'm_mix_w_in', 'm_mix_w_out', 'm_attn_sinks', 'm_sgu_ln_g', 'm_sgu_ln_b', 'm_sgu_w', 'm_sgu_b', 'm_pool_w', 'm_pool_scale', 'v_norms', 'v_mem_norm', 'v_ffn1_wg', 'v_ffn1_wu', 'v_ffn1_wd', 'v_ffn2_wg', 'v_ffn2_wu', 'v_ffn2_wd', 'v_x_wq', 'v_x_wk', 'v_x_wv', 'v_x_wo', 'v_mix_w_in', 'v_mix_w_out', 'v_attn_sinks', 'v_sgu_ln_g', 'v_sgu_ln_b', 'v_sgu_w', 'v_sgu_b', 'v_pool_w', 'v_pool_scale']
TWIN_OUTPUTS = ['loss', 'grad_x', 'grad_norms', 'grad_mem_norm', 'grad_ffn1_wg', 'grad_ffn1_wu', 'grad_ffn1_wd', 'grad_ffn2_wg', 'grad_ffn2_wu', 'grad_ffn2_wd', 'grad_x_wq', 'grad_x_wk', 'grad_x_wv', 'grad_x_wo', 'grad_mix_w_in', 'grad_mix_w_out', 'grad_attn_sinks', 'grad_sgu_ln_g', 'grad_sgu_ln_b', 'grad_sgu_w', 'grad_sgu_b', 'grad_pool_w', 'grad_pool_scale', 'delta_norms', 'delta_mem_norm', 'delta_ffn1_wg', 'delta_ffn1_wu', 'delta_ffn1_wd', 'delta_ffn2_wg', 'delta_ffn2_wu', 'delta_ffn2_wd', 'delta_x_wq', 'delta_x_wk', 'delta_x_wv', 'delta_x_wo', 'delta_mix_w_in', 'delta_mix_w_out', 'delta_attn_sinks', 'delta_sgu_ln_g', 'delta_sgu_ln_b', 'delta_sgu_w', 'delta_sgu_b', 'delta_pool_w', 'delta_pool_scale', 'new_m_norms', 'new_m_mem_norm', 'new_m_ffn1_wg', 'new_m_ffn1_wu', 'new_m_ffn1_wd', 'new_m_ffn2_wg', 'new_m_ffn2_wu', 'new_m_ffn2_wd', 'new_m_x_wq', 'new_m_x_wk', 'new_m_x_wv', 'new_m_x_wo', 'new_m_mix_w_in', 'new_m_mix_w_out', 'new_m_attn_sinks', 'new_m_sgu_ln_g', 'new_m_sgu_ln_b', 'new_m_sgu_w', 'new_m_sgu_b', 'new_m_pool_w', 'new_m_pool_scale', 'new_v_norms', 'new_v_mem_norm', 'new_v_ffn1_wg', 'new_v_ffn1_wu', 'new_v_ffn1_wd', 'new_v_ffn2_wg', 'new_v_ffn2_wu', 'new_v_ffn2_wd', 'new_v_x_wq', 'new_v_x_wk', 'new_v_x_wv', 'new_v_x_wo', 'new_v_mix_w_in', 'new_v_mix_w_out', 'new_v_attn_sinks', 'new_v_sgu_ln_g', 'new_v_sgu_ln_b', 'new_v_sgu_w', 'new_v_sgu_b', 'new_v_pool_w', 'new_v_pool_scale']
TWIN_LEAF_KINDS = {'loss': 'loss', 'grad_x': 'grad_x', 'grad_norms': 'grad_w', 'grad_mem_norm': 'grad_w', 'grad_ffn1_wg': 'grad_w', 'grad_ffn1_wu': 'grad_w', 'grad_ffn1_wd': 'grad_w', 'grad_ffn2_wg': 'grad_w', 'grad_ffn2_wu': 'grad_w', 'grad_ffn2_wd': 'grad_w', 'grad_x_wq': 'grad_w', 'grad_x_wk': 'grad_w', 'grad_x_wv': 'grad_w', 'grad_x_wo': 'grad_w', 'grad_mix_w_in': 'grad_w', 'grad_mix_w_out': 'grad_w', 'grad_attn_sinks': 'grad_w', 'grad_sgu_ln_g': 'grad_w', 'grad_sgu_ln_b': 'grad_w', 'grad_sgu_w': 'grad_w', 'grad_sgu_b': 'grad_w', 'grad_pool_w': 'grad_w', 'grad_pool_scale': 'grad_w', 'delta_norms': 'delta_w', 'delta_mem_norm': 'delta_w', 'delta_ffn1_wg': 'delta_w', 'delta_ffn1_wu': 'delta_w', 'delta_ffn1_wd': 'delta_w', 'delta_ffn2_wg': 'delta_w', 'delta_ffn2_wu': 'delta_w', 'delta_ffn2_wd': 'delta_w', 'delta_x_wq': 'delta_w', 'delta_x_wk': 'delta_w', 'delta_x_wv': 'delta_w', 'delta_x_wo': 'delta_w', 'delta_mix_w_in': 'delta_w', 'delta_mix_w_out': 'delta_w', 'delta_attn_sinks': 'delta_w', 'delta_sgu_ln_g': 'delta_w', 'delta_sgu_ln_b': 'delta_w', 'delta_sgu_w': 'delta_w', 'delta_sgu_b': 'delta_w', 'delta_pool_w': 'delta_w', 'delta_pool_scale': 'delta_w', 'new_m_norms': 'new_m', 'new_m_mem_norm': 'new_m', 'new_m_ffn1_wg': 'new_m', 'new_m_ffn1_wu': 'new_m', 'new_m_ffn1_wd': 'new_m', 'new_m_ffn2_wg': 'new_m', 'new_m_ffn2_wu': 'new_m', 'new_m_ffn2_wd': 'new_m', 'new_m_x_wq': 'new_m', 'new_m_x_wk': 'new_m', 'new_m_x_wv': 'new_m', 'new_m_x_wo': 'new_m', 'new_m_mix_w_in': 'new_m', 'new_m_mix_w_out': 'new_m', 'new_m_attn_sinks': 'new_m', 'new_m_sgu_ln_g': 'new_m', 'new_m_sgu_ln_b': 'new_m', 'new_m_sgu_w': 'new_m', 'new_m_sgu_b': 'new_m', 'new_m_pool_w': 'new_m', 'new_m_pool_scale': 'new_m', 'new_v_norms': 'new_v', 'new_v_mem_norm': 'new_v', 'new_v_ffn1_wg': 'new_v', 'new_v_ffn1_wu': 'new_v', 'new_v_ffn1_wd': 'new_v', 'new_v_ffn2_wg': 'new_v', 'new_v_ffn2_wu': 'new_v', 'new_v_ffn2_wd': 'new_v', 'new_v_x_wq': 'new_v', 'new_v_x_wk': 'new_v', 'new_v_x_wv': 'new_v', 'new_v_x_wo': 'new_v', 'new_v_mix_w_in': 'new_v', 'new_v_mix_w_out': 'new_v', 'new_v_attn_sinks': 'new_v', 'new_v_sgu_ln_g': 'new_v', 'new_v_sgu_ln_b': 'new_v', 'new_v_sgu_w': 'new_v', 'new_v_sgu_b': 'new_v', 'new_v_pool_w': 'new_v', 'new_v_pool_scale': 'new_v'}


def _forward(args):
    return _fwd_reference(*[args[k] for k in FWD_PARAMS])


def _output_shape():
    def fwd():
        inp = _fwd_setup_inputs(0)
        return _fwd_reference(*[inp[k] for k in FWD_PARAMS])
    out = _jax.eval_shape(fwd)
    return out.shape, out.dtype

N_MICROBATCH = 1
ADAM_LR = 0.001
ADAM_B1 = 0.9
ADAM_B2 = 0.999
ADAM_EPS = 1e-08
ADAM_WD = 0.01
ADAM_STEP = 10
PER_EXAMPLE_BATCH_AXIS = {'x': 0, 'mem': 0, 'loss_target': 0}
SHARED_INPUTS = []
_WEIGHT_DTYPES = {'norms': _jnp.float32, 'mem_norm': _jnp.float32, 'ffn1_wg': _jnp.float32, 'ffn1_wu': _jnp.float32, 'ffn1_wd': _jnp.float32, 'ffn2_wg': _jnp.float32, 'ffn2_wu': _jnp.float32, 'ffn2_wd': _jnp.float32, 'x_wq': _jnp.float32, 'x_wk': _jnp.float32, 'x_wv': _jnp.float32, 'x_wo': _jnp.float32, 'mix_w_in': _jnp.float32, 'mix_w_out': _jnp.float32, 'attn_sinks': _jnp.float32, 'sgu_ln_g': _jnp.float32, 'sgu_ln_b': _jnp.float32, 'sgu_w': _jnp.float32, 'sgu_b': _jnp.float32, 'pool_w': _jnp.float32, 'pool_scale': _jnp.float32}
MOMENT_SCALE = {'norms': 8.648623e+00, 'mem_norm': 2.854933e+00, 'ffn1_wg': 1.944292e-01, 'ffn1_wu': 2.415039e-01, 'ffn1_wd': 4.001084e-01, 'ffn2_wg': 2.228883e-01, 'ffn2_wu': 3.351732e-01, 'ffn2_wd': 5.565828e-01, 'x_wq': 1.579040e+00, 'x_wk': 1.656316e+00, 'x_wv': 5.340425e+00, 'x_wo': 2.781386e+00, 'mix_w_in': 4.393830e-01, 'mix_w_out': 2.028517e+00, 'attn_sinks': 1.011269e-01, 'sgu_ln_g': 3.546471e-01, 'sgu_ln_b': 3.439332e-01, 'sgu_w': 3.057548e-01, 'sgu_b': 5.104594e-01, 'pool_w': 1.150841e+00, 'pool_scale': 3.421641e+00}


def _to_microbatches(a, axis):
    t = _jnp.moveaxis(a, axis, 0)
    t = t.reshape((N_MICROBATCH, t.shape[0] // N_MICROBATCH) + t.shape[1:])
    return _jnp.moveaxis(t, 1, axis + 1)


def setup_inputs(seed: int = 0) -> dict:
    inp = _fwd_setup_inputs(seed)
    key = _jax.random.fold_in(_jax.random.key(seed), 7919)
    shape, _ = _output_shape()
    out = dict(inp)
    out["loss_target"] = _jax.random.normal(_jax.random.fold_in(key, 0), shape, _jnp.float32)
    for i, name in enumerate(TWIN_WEIGHTS):
        w = inp[name].astype(_jnp.float32)
        if MOMENT_SCALE is None:
            s = _jnp.sqrt(_jnp.mean(_jnp.square(w)) + 1e-30)
        else:
            s = MOMENT_SCALE[name]
        km, kv = _jax.random.split(_jax.random.fold_in(key, i + 1))
        out[name] = w
        out["m_" + name] = s * _jax.random.normal(km, w.shape, _jnp.float32)
        out["v_" + name] = (s * s) * _jax.random.uniform(kv, w.shape, _jnp.float32, 0.5, 1.5)
    if N_MICROBATCH > 1:
        for name, axis in PER_EXAMPLE_BATCH_AXIS.items():
            out[name] = _to_microbatches(out[name], axis)
    return {'x': out['x'], 'mem': out['mem'], 'norms': out['norms'], 'mem_norm': out['mem_norm'], 'ffn1_wg': out['ffn1_wg'], 'ffn1_wu': out['ffn1_wu'], 'ffn1_wd': out['ffn1_wd'], 'ffn2_wg': out['ffn2_wg'], 'ffn2_wu': out['ffn2_wu'], 'ffn2_wd': out['ffn2_wd'], 'x_wq': out['x_wq'], 'x_wk': out['x_wk'], 'x_wv': out['x_wv'], 'x_wo': out['x_wo'], 'mix_w_in': out['mix_w_in'], 'mix_w_out': out['mix_w_out'], 'attn_sinks': out['attn_sinks'], 'sgu_ln_g': out['sgu_ln_g'], 'sgu_ln_b': out['sgu_ln_b'], 'sgu_w': out['sgu_w'], 'sgu_b': out['sgu_b'], 'pool_w': out['pool_w'], 'pool_scale': out['pool_scale'], 'loss_target': out['loss_target'], 'm_norms': out['m_norms'], 'm_mem_norm': out['m_mem_norm'], 'm_ffn1_wg': out['m_ffn1_wg'], 'm_ffn1_wu': out['m_ffn1_wu'], 'm_ffn1_wd': out['m_ffn1_wd'], 'm_ffn2_wg': out['m_ffn2_wg'], 'm_ffn2_wu': out['m_ffn2_wu'], 'm_ffn2_wd': out['m_ffn2_wd'], 'm_x_wq': out['m_x_wq'], 'm_x_wk': out['m_x_wk'], 'm_x_wv': out['m_x_wv'], 'm_x_wo': out['m_x_wo'], 'm_mix_w_in': out['m_mix_w_in'], 'm_mix_w_out': out['m_mix_w_out'], 'm_attn_sinks': out['m_attn_sinks'], 'm_sgu_ln_g': out['m_sgu_ln_g'], 'm_sgu_ln_b': out['m_sgu_ln_b'], 'm_sgu_w': out['m_sgu_w'], 'm_sgu_b': out['m_sgu_b'], 'm_pool_w': out['m_pool_w'], 'm_pool_scale': out['m_pool_scale'], 'v_norms': out['v_norms'], 'v_mem_norm': out['v_mem_norm'], 'v_ffn1_wg': out['v_ffn1_wg'], 'v_ffn1_wu': out['v_ffn1_wu'], 'v_ffn1_wd': out['v_ffn1_wd'], 'v_ffn2_wg': out['v_ffn2_wg'], 'v_ffn2_wu': out['v_ffn2_wu'], 'v_ffn2_wd': out['v_ffn2_wd'], 'v_x_wq': out['v_x_wq'], 'v_x_wk': out['v_x_wk'], 'v_x_wv': out['v_x_wv'], 'v_x_wo': out['v_x_wo'], 'v_mix_w_in': out['v_mix_w_in'], 'v_mix_w_out': out['v_mix_w_out'], 'v_attn_sinks': out['v_attn_sinks'], 'v_sgu_ln_g': out['v_sgu_ln_g'], 'v_sgu_ln_b': out['v_sgu_ln_b'], 'v_sgu_w': out['v_sgu_w'], 'v_sgu_b': out['v_sgu_b'], 'v_pool_w': out['v_pool_w'], 'v_pool_scale': out['v_pool_scale']}


def _loss(weights, diff, rest, loss_target):
    with _jax.named_scope("forward"):
        args = {**rest, TWIN_DIFF_INPUT: diff, **{k: w.astype(_WEIGHT_DTYPES[k]) for k, w in weights.items()}}
        y = _forward(args)
    with _jax.named_scope("loss_head"):
        err = _jnp.square(y.astype(_jnp.float32) - loss_target)
        return 0.5 * _jnp.sum(_jnp.mean(err, axis=-1)) if err.ndim else 0.5 * err


def _adamw(w, g, m, v):
    m = ADAM_B1 * m + (1.0 - ADAM_B1) * g
    v = ADAM_B2 * v + (1.0 - ADAM_B2) * _jnp.square(g)
    m_hat = m / (1.0 - ADAM_B1 ** ADAM_STEP)
    v_hat = v / (1.0 - ADAM_B2 ** ADAM_STEP)
    delta = -ADAM_LR * (m_hat / (_jnp.sqrt(v_hat) + ADAM_EPS) + ADAM_WD * w)
    return delta, m, v


def reference(x, mem, norms, mem_norm, ffn1_wg, ffn1_wu, ffn1_wd, ffn2_wg, ffn2_wu, ffn2_wd, x_wq, x_wk, x_wv, x_wo, mix_w_in, mix_w_out, attn_sinks, sgu_ln_g, sgu_ln_b, sgu_w, sgu_b, pool_w, pool_scale, loss_target, m_norms, m_mem_norm, m_ffn1_wg, m_ffn1_wu, m_ffn1_wd, m_ffn2_wg, m_ffn2_wu, m_ffn2_wd, m_x_wq, m_x_wk, m_x_wv, m_x_wo, m_mix_w_in, m_mix_w_out, m_attn_sinks, m_sgu_ln_g, m_sgu_ln_b, m_sgu_w, m_sgu_b, m_pool_w, m_pool_scale, v_norms, v_mem_norm, v_ffn1_wg, v_ffn1_wu, v_ffn1_wd, v_ffn2_wg, v_ffn2_wu, v_ffn2_wd, v_x_wq, v_x_wk, v_x_wv, v_x_wo, v_mix_w_in, v_mix_w_out, v_attn_sinks, v_sgu_ln_g, v_sgu_ln_b, v_sgu_w, v_sgu_b, v_pool_w, v_pool_scale):
    given = dict(x=x, mem=mem, norms=norms, mem_norm=mem_norm, ffn1_wg=ffn1_wg, ffn1_wu=ffn1_wu, ffn1_wd=ffn1_wd, ffn2_wg=ffn2_wg, ffn2_wu=ffn2_wu, ffn2_wd=ffn2_wd, x_wq=x_wq, x_wk=x_wk, x_wv=x_wv, x_wo=x_wo, mix_w_in=mix_w_in, mix_w_out=mix_w_out, attn_sinks=attn_sinks, sgu_ln_g=sgu_ln_g, sgu_ln_b=sgu_ln_b, sgu_w=sgu_w, sgu_b=sgu_b, pool_w=pool_w, pool_scale=pool_scale, loss_target=loss_target, m_norms=m_norms, m_mem_norm=m_mem_norm, m_ffn1_wg=m_ffn1_wg, m_ffn1_wu=m_ffn1_wu, m_ffn1_wd=m_ffn1_wd, m_ffn2_wg=m_ffn2_wg, m_ffn2_wu=m_ffn2_wu, m_ffn2_wd=m_ffn2_wd, m_x_wq=m_x_wq, m_x_wk=m_x_wk, m_x_wv=m_x_wv, m_x_wo=m_x_wo, m_mix_w_in=m_mix_w_in, m_mix_w_out=m_mix_w_out, m_attn_sinks=m_attn_sinks, m_sgu_ln_g=m_sgu_ln_g, m_sgu_ln_b=m_sgu_ln_b, m_sgu_w=m_sgu_w, m_sgu_b=m_sgu_b, m_pool_w=m_pool_w, m_pool_scale=m_pool_scale, v_norms=v_norms, v_mem_norm=v_mem_norm, v_ffn1_wg=v_ffn1_wg, v_ffn1_wu=v_ffn1_wu, v_ffn1_wd=v_ffn1_wd, v_ffn2_wg=v_ffn2_wg, v_ffn2_wu=v_ffn2_wu, v_ffn2_wd=v_ffn2_wd, v_x_wq=v_x_wq, v_x_wk=v_x_wk, v_x_wv=v_x_wv, v_x_wo=v_x_wo, v_mix_w_in=v_mix_w_in, v_mix_w_out=v_mix_w_out, v_attn_sinks=v_attn_sinks, v_sgu_ln_g=v_sgu_ln_g, v_sgu_ln_b=v_sgu_ln_b, v_sgu_w=v_sgu_w, v_sgu_b=v_sgu_b, v_pool_w=v_pool_w, v_pool_scale=v_pool_scale)
    weights = {n: given[n] for n in TWIN_WEIGHTS}
    shared = {n: given[n] for n in SHARED_INPUTS}
    per_example = {n: given[n] for n in ['x', 'mem']}
    grad_fn = _jax.value_and_grad(_loss, argnums=(0, 1))

    def one_microbatch(ex, loss_target):
        ex = dict(ex)
        diff = ex.pop(TWIN_DIFF_INPUT)
        return grad_fn(weights, diff, {**shared, **ex}, loss_target)

    if N_MICROBATCH == 1:
        loss, (grad_w, grad_x) = one_microbatch(per_example, given["loss_target"])
    else:
        def body(carry, xs):
            loss_sum, grad_sum = carry
            l_k, (gw_k, gx_k) = one_microbatch(xs[0], xs[1])
            with _jax.named_scope("update"):
                return (loss_sum + l_k, _jax.tree.map(_jnp.add, grad_sum, gw_k)), gx_k

        init = (_jnp.zeros((), _jnp.float32), _jax.tree.map(_jnp.zeros_like, weights))
        (loss, grad_w), grad_x = _jax.lax.scan(body, init, (per_example, given["loss_target"]))
    with _jax.named_scope("update"):
        delta_w, new_m, new_v = {}, {}, {}
        for n in TWIN_WEIGHTS:
            delta_w[n], new_m[n], new_v[n] = _adamw(weights[n], grad_w[n], given["m_" + n], given["v_" + n])
    return (loss, grad_x, *[grad_w[n] for n in TWIN_WEIGHTS], *[delta_w[n] for n in TWIN_WEIGHTS],
            *[new_m[n] for n in TWIN_WEIGHTS], *[new_v[n] for n in TWIN_WEIGHTS])
```

```python
import functools

import jax
import jax.numpy as jnp
from jax import lax
from jax.experimental import pallas as pl
from jax.experimental.pallas import tpu as pltpu

F32 = jnp.float32
BF16 = jnp.bfloat16
MESH = pl.DeviceIdType.MESH

EPS = 1e-6
ROPE_THETA = 500000.0
ROPE_HALF = 8
HEAD_DIM = 64
N_Q_HEADS = 16
N_KV_HEADS = 2
Q_PER_KV = 8
BLOCK = 128
ATTN_WIDTH = 1024
KV_WIDTH = 128
QK_WIDTH = ATTN_WIDTH + KV_WIDTH
SGU_WIDTH = 1024
SGU_GROUPS = 8
POOL_WINDOWS = (2, 4, 8, 16)
POOL_HALO = 16
X_HEADS = 4
X_HEAD_DIM = 128
N_CHIPS = 4
N_DEV = 8

ADAM_LR = 0.001
ADAM_B1 = 0.9
ADAM_B2 = 0.999
ADAM_EPS = 1e-08
ADAM_WD = 0.01
ADAM_STEP = 10

VMEM_LIMIT_V7X = 48 * 1024 * 1024
LANES = 128
ROW_TILE_BYTES = 2 * 1024 * 1024


def _params(sem):
    return pltpu.CompilerParams(dimension_semantics=sem, vmem_limit_bytes=VMEM_LIMIT_V7X)


def _pick(dim, pref, align):
    cands = [t for t in range(align, dim + 1, align) if dim % t == 0]
    small = [t for t in cands if t <= pref]
    if small and small[-1] * 2 >= min(pref, dim):
        return small[-1]
    return dim


def _rows(n_rows, bytes_per_row):
    want = max(16, min(1024, ROW_TILE_BYTES // max(1, bytes_per_row)))
    cands = [t for t in range(16, n_rows + 1, 16) if n_rows % t == 0 and t <= want]
    return cands[-1] if cands else n_rows


def _rsum8(v):
    r, c = v.shape
    return v.reshape(r // 8, 8, c).sum(axis=0)


def _mm(a, b, *, name, ta=False, tb=False, batch="none", groups=0, a_cb=False, b_cb=False, o_cb=False,
        out_dtype=F32, pm=512, pn=512, pk=512):
    J = groups or (a.shape[0] if a.ndim == 3 else (b.shape[0] if b.ndim == 3 else 1))
    a2, b2 = a.shape[-2:], b.shape[-2:]
    M, K = (a2[1], a2[0]) if ta else a2
    N, Kb = b2 if tb else (b2[1], b2[0])
    if a_cb:
        if ta:
            M //= J
        else:
            K //= J
    if b_cb:
        if tb:
            Kb //= J
        else:
            N //= J
    assert K == Kb, (name, a.shape, b.shape)
    tm = _pick(M, pm, LANES if ta else 16)
    tn = _pick(N, pn, LANES)
    tk = _pick(K, pk, LANES if (not ta or tb) else 16)
    nm, nn, nk = M // tm, N // tn, K // tk
    reduce = batch == "reduce"
    if reduce:
        grid = (nm, nn, J, nk)
        unpack = lambda m, n, j, k: (j, m, n, k)
        sem = ("parallel", "parallel", "arbitrary", "arbitrary")
    else:
        grid = (J, nm, nn, nk)
        unpack = lambda j, m, n, k: (j, m, n, k)
        sem = ("parallel", "parallel", "parallel", "arbitrary")

    def a_map(*g):
        j, m, n, k = unpack(*g)
        r, c = (k, m) if ta else (m, k)
        if a_cb:
            c = c + j * (nm if ta else nk)
        return (j, r, c) if a.ndim == 3 else (r, c)

    def b_map(*g):
        j, m, n, k = unpack(*g)
        r, c = (n, k) if tb else (k, n)
        if b_cb:
            c = c + j * (nk if tb else nn)
        return (j, r, c) if b.ndim == 3 else (r, c)

    def o_map(*g):
        j, m, n, k = unpack(*g)
        if o_cb:
            return (m, n + j * nn)
        return (j, m, n) if batch == "map" else (m, n)

    a_blk = (tk, tm) if ta else (tm, tk)
    b_blk = (tn, tk) if tb else (tk, tn)
    a_spec = pl.BlockSpec(((None,) + a_blk) if a.ndim == 3 else a_blk, a_map)
    b_spec = pl.BlockSpec(((None,) + b_blk) if b.ndim == 3 else b_blk, b_map)
    if o_cb:
        out_shape, o_blk = (M, N * J), (tm, tn)
    elif batch == "map":
        out_shape, o_blk = (J, M, N), (None, tm, tn)
    else:
        out_shape, o_blk = (M, N), (tm, tn)
    dims = (((0 if ta else 1,), (1 if tb else 0,)), ((), ()))
    red_axes = (2, 3) if reduce else (3,)

    def body(a_ref, b_ref, o_ref, acc_ref):
        first = functools.reduce(jnp.logical_and, [pl.program_id(ax) == 0 for ax in red_axes])
        last = functools.reduce(jnp.logical_and, [pl.program_id(ax) == grid[ax] - 1 for ax in red_axes])

        @pl.when(first)
        def _():
            acc_ref[...] = jnp.zeros_like(acc_ref)

        acc_ref[...] += lax.dot_general(a_ref[...], b_ref[...], dims, preferred_element_type=F32)

        @pl.when(last)
        def _():
            o_ref[...] = acc_ref[...].astype(o_ref.dtype)

    return pl.pallas_call(
        body, name=name, grid=grid, in_specs=[a_spec, b_spec], out_specs=pl.BlockSpec(o_blk, o_map),
        out_shape=jax.ShapeDtypeStruct(out_shape, out_dtype), scratch_shapes=[pltpu.VMEM((tm, tn), F32)],
        compiler_params=_params(sem),
    )(a, b)


def _rowwise(fn, tiled, whole, outs, accs, *, name):
    n_rows = tiled[0].shape[0]
    row_bytes = sum(t.shape[1] * t.dtype.itemsize for t in tiled) + sum(c * jnp.dtype(d).itemsize for c, d in outs)
    tr = _rows(n_rows, row_bytes)
    n_t, n_w, n_o = len(tiled), len(whole), len(outs)

    def body(*refs):
        i = pl.program_id(0)
        t_refs, w_refs = refs[:n_t], refs[n_t:n_t + n_w]
        o_refs, a_refs = refs[n_t + n_w:n_t + n_w + n_o], refs[n_t + n_w + n_o:]
        o_vals, a_vals = fn(i, *[r[...] for r in t_refs], *[r[...] for r in w_refs])
        for r, v in zip(o_refs, o_vals):
            r[...] = v.astype(r.dtype)
        if a_refs:
            @pl.when(i == 0)
            def _():
                for r in a_refs:
                    r[...] = jnp.zeros_like(r)
            for r, v in zip(a_refs, a_vals):
                r[...] += v

    in_specs = [pl.BlockSpec((tr, t.shape[1]), lambda i: (i, 0)) for t in tiled]
    in_specs += [pl.BlockSpec(w.shape, lambda i, nd=w.ndim: (0,) * nd) for w in whole]
    out_specs = [pl.BlockSpec((tr, c), lambda i: (i, 0)) for c, _ in outs]
    out_specs += [pl.BlockSpec(s, lambda i, nd=len(s): (0,) * nd) for s, _ in accs]
    out_shape = [jax.ShapeDtypeStruct((n_rows, c), d) for c, d in outs]
    out_shape += [jax.ShapeDtypeStruct(s, d) for s, d in accs]
    res = pl.pallas_call(
        body, name=name, grid=(n_rows // tr,), in_specs=in_specs, out_specs=out_specs, out_shape=out_shape,
        compiler_params=_params(("arbitrary",) if accs else ("parallel",)),
    )(*tiled, *whole)
    return res


def _rms_stats(x):
    r = lax.rsqrt(jnp.mean(x * x, axis=-1, keepdims=True) + EPS)
    return x * r, r


def rms_fwd(x, g, out_dtype, name):
    def fn(i, x, g):
        xhat, _ = _rms_stats(x)
        return [xhat * g], []
    return _rowwise(fn, [x], [g], [(x.shape[1], out_dtype)], [], name=name)[0]


def postnorm_res(x, y, g, s, name):
    def fn(i, x, y, g):
        yhat, _ = _rms_stats(y)
        return [x + s * (yhat * g)], []
    return _rowwise(fn, [x, y], [g], [(x.shape[1], F32)], [], name=name)[0]


def rms_bwd(xin, g, douts, scale, add, out_dtype, name):
    n_d = len(douts)

    def fn(i, x, *rest):
        ds, rest = rest[:n_d], rest[n_d:]
        ad = rest[0] if add is not None else None
        g = rest[-1]
        xhat, r = _rms_stats(x)
        d = ds[0].astype(F32)
        for e in ds[1:]:
            d = d + e.astype(F32)
        if scale != 1.0:
            d = d * scale
        dg = _rsum8(d * xhat)
        dxhat = d * g
        dx = r * (dxhat - xhat * jnp.mean(dxhat * xhat, axis=-1, keepdims=True))
        if ad is not None:
            dx = dx + ad
        return [dx], [dg]

    tiled = [xin, *douts] + ([add] if add is not None else [])
    dx, dg = _rowwise(fn, tiled, [g], [(xin.shape[1], out_dtype)], [((8, xin.shape[1]), F32)], name=name)
    return dx, dg


def _silu_parts(g):
    sg = 1.0 / (1.0 + jnp.exp(-g))
    return g * sg, sg


def swiglu_fwd(G, U, name):
    shp = G.shape

    def fn(i, g, u):
        s, _ = _silu_parts(g.astype(F32))
        return [s * u.astype(F32)], []
    a = _rowwise(fn, [G.reshape(-1, shp[-1]), U.reshape(-1, shp[-1])], [], [(shp[-1], BF16)], [], name=name)[0]
    return a.reshape(shp)


def swiglu_bwd(G, U, dA, name):
    shp = G.shape

    def fn(i, g, u, da):
        g, u, da = g.astype(F32), u.astype(F32), da.astype(F32)
        s, sg = _silu_parts(g)
        dsilu = sg * (1.0 + g * (1.0 - sg))
        return [da * u * dsilu, da * s], []
    flat = lambda t: t.reshape(-1, shp[-1])
    dG, dU = _rowwise(fn, [flat(G), flat(U), flat(dA)], [], [(shp[-1], BF16), (shp[-1], BF16)], [], name=name)
    return dG.reshape(shp), dU.reshape(shp)


def scale_cols(y, s, name):
    def fn(i, y, s):
        return [y * s], []
    return _rowwise(fn, [y], [s], [(y.shape[1], F32)], [], name=name)[0]


def scale_cols_bwd(dm, y, s, name):
    def fn(i, dm, y, s):
        return [dm * s], [_rsum8(dm * y)]
    return _rowwise(fn, [dm, y], [s], [(y.shape[1], BF16)], [((8, y.shape[1]), F32)], name=name)


def loss_and_grad(y, target, name):
    n_feat = y.shape[1]

    def fn(i, y, t):
        e = y - t
        return [e * (1.0 / n_feat)], [_rsum8(e * e)]
    dy, part = _rowwise(fn, [y, target], [], [(n_feat, F32)], [((8, n_feat), F32)], name=name)
    return part, dy


def cast_layer(w3, layer, name):
    _, n_rows, n_cols = w3.shape
    tr = _rows(n_rows, n_cols * 6)

    def body(w_ref, o_ref):
        o_ref[...] = w_ref[...].astype(BF16)

    return pl.pallas_call(
        body, name=name, grid=(n_rows // tr,),
        in_specs=[pl.BlockSpec((None, tr, n_cols), lambda i: (layer, i, 0))],
        out_specs=pl.BlockSpec((tr, n_cols), lambda i: (i, 0)),
        out_shape=jax.ShapeDtypeStruct((n_rows, n_cols), BF16), compiler_params=_params(("parallel",)),
    )(w3)


def adamw(w, g, m, v, name):
    n_l, n_rows, n_cols = w.shape
    tr = _rows(n_rows, n_cols * 4 * 7)
    c1 = 1.0 / (1.0 - ADAM_B1 ** ADAM_STEP)
    c2 = 1.0 / (1.0 - ADAM_B2 ** ADAM_STEP)

    def body(w_ref, g_ref, m_ref, v_ref, d_ref, mo_ref, vo_ref):
        g = g_ref[...]
        m = ADAM_B1 * m_ref[...] + (1.0 - ADAM_B1) * g
        v = ADAM_B2 * v_ref[...] + (1.0 - ADAM_B2) * (g * g)
        d_ref[...] = -ADAM_LR * ((m * c1) / (jnp.sqrt(v * c2) + ADAM_EPS) + ADAM_WD * w_ref[...])
        mo_ref[...] = m
        vo_ref[...] = v

    spec = pl.BlockSpec((None, tr, n_cols), lambda l, i: (l, i, 0))
    shp = jax.ShapeDtypeStruct(w.shape, F32)
    return pl.pallas_call(
        body, name=name, grid=(n_l, n_rows // tr), in_specs=[spec] * 4, out_specs=[spec] * 3, out_shape=[shp] * 3,
        compiler_params=_params(("parallel", "parallel")),
    )(w, g, m, v)


def rope_tables(n_tok):
    inv = ROPE_THETA ** (-jnp.arange(ROPE_HALF, dtype=F32) * 2.0 / (2 * ROPE_HALF))
    ang = jnp.arange(n_tok, dtype=F32)[:, None] * inv[None, :]
    cos, sin = jnp.cos(ang), jnp.sin(ang)
    rest = HEAD_DIM - 2 * ROPE_HALF
    one, zero, z8 = jnp.ones((n_tok, rest), F32), jnp.zeros((n_tok, rest), F32), jnp.zeros((n_tok, ROPE_HALF), F32)
    c = jnp.concatenate([cos, cos, one], axis=1)
    s1 = jnp.concatenate([-sin, z8, zero], axis=1)
    s2 = jnp.concatenate([z8, sin, zero], axis=1)
    two = lambda t: jnp.concatenate([t, t], axis=1)
    return two(c), two(s1), two(s2)


def rope_apply(x, tabs, n_col_blocks, inverse, out_dtype, name):
    n_tok = x.shape[0]
    tr = _rows(n_tok, LANES * 4 * 6)

    def body(x_ref, c_ref, s1_ref, s2_ref, o_ref):
        x = x_ref[...].astype(F32)
        if inverse:
            out = x * c_ref[...] + pltpu.roll(x * s1_ref[...], ROPE_HALF, 1) + pltpu.roll(x * s2_ref[...], LANES - ROPE_HALF, 1)
        else:
            out = x * c_ref[...] + pltpu.roll(x, LANES - ROPE_HALF, 1) * s1_ref[...] + pltpu.roll(x, ROPE_HALF, 1) * s2_ref[...]
        o_ref[...] = out.astype(o_ref.dtype)

    tab_spec = pl.BlockSpec((tr, LANES), lambda i, c: (i, 0))
    blk = pl.BlockSpec((tr, LANES), lambda i, c: (i, c))
    return pl.pallas_call(
        body, name=name, grid=(n_tok // tr, n_col_blocks), in_specs=[blk, tab_spec, tab_spec, tab_spec], out_specs=blk,
        out_shape=jax.ShapeDtypeStruct((n_tok, n_col_blocks * LANES), out_dtype), compiler_params=_params(("parallel", "parallel")),
    )(x, *tabs)


def _swa_probs(q, k, sink, n):
    rows = Q_PER_KV * BLOCK
    s = lax.dot_general(q, k, (((1,), (1,)), ((), ())), preferred_element_type=F32) * (HEAD_DIM ** -0.5)
    qi = lax.broadcasted_iota(jnp.int32, (rows, 2 * BLOCK), 0) & (BLOCK - 1)
    kj = lax.broadcasted_iota(jnp.int32, (rows, 2 * BLOCK), 1)
    rel = qi + BLOCK - kj
    valid = (rel >= 0) & (rel < BLOCK) & ((n > 0) | (kj >= BLOCK))
    s = jnp.where(valid, s, -1e30)
    m = jnp.maximum(jnp.max(s, axis=-1, keepdims=True), sink)
    e = jnp.exp(s - m)
    es = jnp.exp(sink - m)
    inv = 1.0 / (jnp.sum(e, axis=-1, keepdims=True) + es)
    return e * inv, es * inv


def _swa_specs(n_blocks):
    q_spec = pl.BlockSpec((Q_PER_KV, BLOCK, HEAD_DIM), lambda h, n: (h, n, 0))
    prev = pl.BlockSpec((None, BLOCK, HEAD_DIM), lambda h, n: (h, jnp.maximum(n - 1, 0), 0))
    cur = pl.BlockSpec((None, BLOCK, HEAD_DIM), lambda h, n: (h, n, 0))
    sink = pl.BlockSpec((None, Q_PER_KV * BLOCK, 1), lambda h, n: (h, 0, 0))
    return q_spec, prev, cur, sink


def swa_fwd(q, k, v, sink_rows, name):
    n_tok = q.shape[1]
    q_spec, prev, cur, sink = _swa_specs(n_tok // BLOCK)

    def body(q_ref, kp_ref, kc_ref, vp_ref, vc_ref, s_ref, o_ref):
        n = pl.program_id(1)
        qq = q_ref[...].reshape(Q_PER_KV * BLOCK, HEAD_DIM)
        kk = jnp.concatenate([kp_ref[...], kc_ref[...]], axis=0)
        vv = jnp.concatenate([vp_ref[...], vc_ref[...]], axis=0)
        p, _ = _swa_probs(qq, kk, s_ref[...], n)
        o = jnp.dot(p.astype(BF16), vv, preferred_element_type=F32)
        o_ref[...] = o.reshape(Q_PER_KV, BLOCK, HEAD_DIM).astype(o_ref.dtype)

    return pl.pallas_call(
        body, name=name, grid=(N_KV_HEADS, n_tok // BLOCK), in_specs=[q_spec, prev, cur, prev, cur, sink], out_specs=q_spec,
        out_shape=jax.ShapeDtypeStruct(q.shape, BF16), compiler_params=_params(("parallel", "parallel")),
    )(q, k, k, v, v, sink_rows)


def swa_bwd(q, k, v, sink_rows, do, name):
    n_tok = q.shape[1]
    nb = n_tok // BLOCK
    q_spec, prev, cur, sink = _swa_specs(nb)
    rows = Q_PER_KV * BLOCK

    def body(q_ref, kp_ref, kc_ref, vp_ref, vc_ref, s_ref, do_ref, dq_ref, dkp_ref, dkc_ref, dvp_ref, dvc_ref, ds_ref):
        n = pl.program_id(1)
        qq = q_ref[...].reshape(rows, HEAD_DIM)
        dd = do_ref[...].reshape(rows, HEAD_DIM)
        kk = jnp.concatenate([kp_ref[...], kc_ref[...]], axis=0)
        vv = jnp.concatenate([vp_ref[...], vc_ref[...]], axis=0)
        p, ps = _swa_probs(qq, kk, s_ref[...], n)
        dp = lax.dot_general(dd, vv, (((1,), (1,)), ((), ())), preferred_element_type=F32)
        delta = jnp.sum(p * dp, axis=-1, keepdims=True)
        ds = (p * (dp - delta) * (HEAD_DIM ** -0.5)).astype(BF16)
        dq = jnp.dot(ds, kk, preferred_element_type=F32)
        dk = lax.dot_general(ds, qq, (((0,), (0,)), ((), ())), preferred_element_type=F32)
        dv = lax.dot_general(p.astype(BF16), dd, (((0,), (0,)), ((), ())), preferred_element_type=F32)
        dq_ref[...] = dq.reshape(Q_PER_KV, BLOCK, HEAD_DIM).astype(dq_ref.dtype)
        dkp_ref[...] = dk[:BLOCK]
        dkc_ref[...] = dk[BLOCK:]
        dvp_ref[...] = dv[:BLOCK]
        dvc_ref[...] = dv[BLOCK:]
        dsink = jnp.broadcast_to(-ps * delta, (rows, LANES)).reshape(Q_PER_KV, BLOCK, LANES)
        ds_ref[...] = jnp.sum(dsink, axis=1)

    part = pl.BlockSpec((None, None, BLOCK, HEAD_DIM), lambda h, n: (h, n, 0, 0))
    part_shape = jax.ShapeDtypeStruct((N_KV_HEADS, nb, BLOCK, HEAD_DIM), F32)
    return pl.pallas_call(
        body, name=name, grid=(N_KV_HEADS, nb), in_specs=[q_spec, prev, cur, prev, cur, sink, q_spec],
        out_specs=[q_spec, part, part, part, part, pl.BlockSpec((None, None, Q_PER_KV, LANES), lambda h, n: (h, n, 0, 0))],
        out_shape=[jax.ShapeDtypeStruct(q.shape, BF16), part_shape, part_shape, part_shape, part_shape,
                   jax.ShapeDtypeStruct((N_KV_HEADS, nb, Q_PER_KV, LANES), F32)],
        compiler_params=_params(("parallel", "parallel")),
    )(q, k, k, v, v, sink_rows, do)


def _to_heads(t, n_heads):
    return t.reshape(t.shape[0], n_heads, HEAD_DIM).transpose(1, 0, 2)


def _from_heads(t):
    return t.transpose(1, 0, 2).reshape(t.shape[1], -1)


def _fold_kv_grad(prev, cur):
    shifted = jnp.concatenate([prev[:, 1:], jnp.zeros_like(prev[:, :1])], axis=1)
    tot = (cur + shifted).reshape(N_KV_HEADS, -1, HEAD_DIM)
    return _from_heads(tot)


def _x_probs(qh, kh):
    s = lax.dot_general(qh, kh, (((1,), (1,)), ((), ())), preferred_element_type=F32) * (X_HEAD_DIM ** -0.5)
    e = jnp.exp(s - jnp.max(s, axis=-1, keepdims=True))
    return e * (1.0 / jnp.sum(e, axis=-1, keepdims=True))


def xattn_fwd(q, k, v, name):
    n_tok, width = q.shape
    n_mem = k.shape[0]
    tq = _pick(n_tok, 512, 16)

    def body(q_ref, k_ref, v_ref, o_ref):
        for h in range(X_HEADS):
            cols = slice(h * X_HEAD_DIM, (h + 1) * X_HEAD_DIM)
            p = _x_probs(q_ref[:, cols], k_ref[:, cols])
            o_ref[:, cols] = jnp.dot(p.astype(BF16), v_ref[:, cols], preferred_element_type=F32).astype(o_ref.dtype)

    row = pl.BlockSpec((tq, width), lambda i: (i, 0))
    mem = pl.BlockSpec((n_mem, width), lambda i: (0, 0))
    return pl.pallas_call(
        body, name=name, grid=(n_tok // tq,), in_specs=[row, mem, mem], out_specs=row,
        out_shape=jax.ShapeDtypeStruct(q.shape, BF16), compiler_params=_params(("parallel",)),
    )(q, k, v)


def xattn_bwd(q, k, v, do, name):
    n_tok, width = q.shape
    n_mem = k.shape[0]
    tq = _pick(n_tok, 512, 16)

    def body(q_ref, k_ref, v_ref, do_ref, dq_ref, dk_ref, dv_ref):
        @pl.when(pl.program_id(0) == 0)
        def _():
            dk_ref[...] = jnp.zeros_like(dk_ref)
            dv_ref[...] = jnp.zeros_like(dv_ref)

        for h in range(X_HEADS):
            cols = slice(h * X_HEAD_DIM, (h + 1) * X_HEAD_DIM)
            qh, kh, vh, dh = q_ref[:, cols], k_ref[:, cols], v_ref[:, cols], do_ref[:, cols]
            p = _x_probs(qh, kh)
            dp = lax.dot_general(dh, vh, (((1,), (1,)), ((), ())), preferred_element_type=F32)
            delta = jnp.sum(p * dp, axis=-1, keepdims=True)
            ds = (p * (dp - delta) * (X_HEAD_DIM ** -0.5)).astype(BF16)
            dq_ref[:, cols] = jnp.dot(ds, kh, preferred_element_type=F32).astype(dq_ref.dtype)
            dk_ref[:, cols] += lax.dot_general(ds, qh, (((0,), (0,)), ((), ())), preferred_element_type=F32)
            dv_ref[:, cols] += lax.dot_general(p.astype(BF16), dh, (((0,), (0,)), ((), ())), preferred_element_type=F32)

    row = pl.BlockSpec((tq, width), lambda i: (i, 0))
    mem = pl.BlockSpec((n_mem, width), lambda i: (0, 0))
    return pl.pallas_call(
        body, name=name, grid=(n_tok // tq,), in_specs=[row, mem, mem, row], out_specs=[row, mem, mem],
        out_shape=[jax.ShapeDtypeStruct(q.shape, BF16), jax.ShapeDtypeStruct(k.shape, F32), jax.ShapeDtypeStruct(k.shape, F32)],
        compiler_params=_params(("arbitrary",)),
    )(q, k, v, do)


GELU_C = 0.7978845608028654
GELU_A = 0.044715


def _gelu_parts(x):
    x2 = x * x
    t = jnp.tanh(GELU_C * x * (1.0 + GELU_A * x2))
    y = 0.5 * x * (1.0 + t)
    dy = 0.5 * (1.0 + t) + 0.5 * x * (1.0 - t * t) * GELU_C * (1.0 + 3.0 * GELU_A * x2)
    return y, dy


def _sgu_norm(v, ln_g, ln_b):
    mu = jnp.mean(v, axis=-1, keepdims=True)
    vc = v - mu
    r = lax.rsqrt(jnp.mean(vc * vc, axis=-1, keepdims=True) + EPS)
    xhat = vc * r
    return xhat * ln_g + ln_b, xhat, r


def _causal_weights(w_ref):
    i = lax.broadcasted_iota(jnp.int32, (BLOCK, BLOCK), 0)
    j = lax.broadcasted_iota(jnp.int32, (BLOCK, BLOCK), 1)
    return [jnp.where(i >= j, w_ref[g], 0.0).astype(BF16) for g in range(SGU_GROUPS)]


def sgu_fwd(u_pre, v_pre, ln_g, ln_b, w_s, bias_rows, name):
    n_tok = u_pre.shape[0]
    tm = _pick(n_tok, 512, BLOCK)

    def body(u_ref, v_ref, g_ref, b_ref, w_ref, bb_ref, o_ref):
        vn, _, _ = _sgu_norm(_gelu_parts(v_ref[...])[0], g_ref[...], b_ref[...])
        vn = vn.astype(BF16)
        wc = _causal_weights(w_ref)
        for c in range(tm // BLOCK):
            rows = slice(c * BLOCK, (c + 1) * BLOCK)
            for g in range(SGU_GROUPS):
                cols = slice(g * LANES, (g + 1) * LANES)
                mixed = jnp.dot(wc[g], vn[rows, cols], preferred_element_type=F32) + bb_ref[g]
                u = _gelu_parts(u_ref[rows, cols])[0]
                o_ref[rows, cols] = (u * mixed).astype(o_ref.dtype)

    row = pl.BlockSpec((tm, SGU_WIDTH), lambda i: (i, 0))
    vec = pl.BlockSpec((1, SGU_WIDTH), lambda i: (0, 0))
    mat = pl.BlockSpec((SGU_GROUPS, BLOCK, LANES), lambda i: (0, 0, 0))
    return pl.pallas_call(
        body, name=name, grid=(n_tok // tm,), in_specs=[row, row, vec, vec, mat, mat], out_specs=row,
        out_shape=jax.ShapeDtypeStruct((n_tok, SGU_WIDTH), BF16), compiler_params=_params(("parallel",)),
    )(u_pre, v_pre, ln_g, ln_b, w_s, bias_rows)


def sgu_bwd(u_pre, v_pre, ln_g, ln_b, w_s, bias_rows, dgate, name):
    n_tok = u_pre.shape[0]
    tm = _pick(n_tok, 512, BLOCK)

    def body(u_ref, v_ref, g_ref, b_ref, w_ref, bb_ref, dg_ref, du_ref, dv_ref, dw_ref, db_ref, dlg_ref, dlb_ref, dvn_ref):
        @pl.when(pl.program_id(0) == 0)
        def _():
            dw_ref[...] = jnp.zeros_like(dw_ref)
            db_ref[...] = jnp.zeros_like(db_ref)
            dlg_ref[...] = jnp.zeros_like(dlg_ref)
            dlb_ref[...] = jnp.zeros_like(dlb_ref)

        gv, dgv = _gelu_parts(v_ref[...])
        vn, xhat, r = _sgu_norm(gv, g_ref[...], b_ref[...])
        vn = vn.astype(BF16)
        wc = _causal_weights(w_ref)
        for c in range(tm // BLOCK):
            rows = slice(c * BLOCK, (c + 1) * BLOCK)
            for g in range(SGU_GROUPS):
                cols = slice(g * LANES, (g + 1) * LANES)
                vt = vn[rows, cols]
                mixed = jnp.dot(wc[g], vt, preferred_element_type=F32) + bb_ref[g]
                u, du_dpre = _gelu_parts(u_ref[rows, cols])
                dgate_t = dg_ref[rows, cols].astype(F32)
                du_ref[rows, cols] = (dgate_t * mixed * du_dpre).astype(du_ref.dtype)
                dmix = dgate_t * u
                dmix_b = dmix.astype(BF16)
                db_ref[g] += dmix
                dw_ref[g] += lax.dot_general(dmix_b, vt, (((1,), (1,)), ((), ())), preferred_element_type=F32)
                dvn_ref[rows, cols] = lax.dot_general(wc[g], dmix_b, (((0,), (0,)), ((), ())), preferred_element_type=F32)
        dvn = dvn_ref[...]
        dlg_ref[...] += _rsum8(dvn * xhat)
        dlb_ref[...] += _rsum8(dvn)
        dxhat = dvn * g_ref[...]
        dgv_in = r * (dxhat - jnp.mean(dxhat, axis=-1, keepdims=True) - xhat * jnp.mean(dxhat * xhat, axis=-1, keepdims=True))
        dv_ref[...] = (dgv_in * dgv).astype(dv_ref.dtype)

    row = pl.BlockSpec((tm, SGU_WIDTH), lambda i: (i, 0))
    vec = pl.BlockSpec((1, SGU_WIDTH), lambda i: (0, 0))
    mat = pl.BlockSpec((SGU_GROUPS, BLOCK, LANES), lambda i: (0, 0, 0))
    part = pl.BlockSpec((8, SGU_WIDTH), lambda i: (0, 0))
    mat_shape = jax.ShapeDtypeStruct((SGU_GROUPS, BLOCK, LANES), F32)
    part_shape = jax.ShapeDtypeStruct((8, SGU_WIDTH), F32)
    act_shape = jax.ShapeDtypeStruct((n_tok, SGU_WIDTH), BF16)
    return pl.pallas_call(
        body, name=name, grid=(n_tok // tm,), in_specs=[row, row, vec, vec, mat, mat, row],
        out_specs=[row, row, mat, mat, part, part], out_shape=[act_shape, act_shape, mat_shape, mat_shape, part_shape, part_shape],
        scratch_shapes=[pltpu.VMEM((tm, SGU_WIDTH), F32)], compiler_params=_params(("arbitrary",)),
    )(u_pre, v_pre, ln_g, ln_b, w_s, bias_rows, dgate)


def _pool_tile(n_tok):
    return _pick(n_tok, 256, POOL_HALO)


def pool_fwd(h, name):
    n_tok, width = h.shape
    gw = width // len(POOL_WINDOWS)
    tm = _pool_tile(n_tok)
    per = tm // POOL_HALO

    def body(cur_ref, halo_ref, o_ref, buf_ref):
        i = pl.program_id(0)
        buf_ref[0:POOL_HALO, :] = jnp.where(i > 0, halo_ref[...], 0.0)
        buf_ref[POOL_HALO:, :] = cur_ref[...]
        tok = i * tm + lax.broadcasted_iota(jnp.int32, (tm, 1), 0)
        for g, w in enumerate(POOL_WINDOWS):
            cols = slice(g * gw, (g + 1) * gw)
            acc = buf_ref[POOL_HALO:, cols]
            for j in range(1, w):
                acc = acc + buf_ref[POOL_HALO - j:POOL_HALO - j + tm, cols]
            cnt = jnp.minimum(tok + 1, w).astype(F32)
            o_ref[:, cols] = (acc / cnt - cur_ref[:, cols]).astype(o_ref.dtype)

    return pl.pallas_call(
        body, name=name, grid=(n_tok // tm,),
        in_specs=[pl.BlockSpec((tm, width), lambda i: (i, 0)),
                  pl.BlockSpec((POOL_HALO, width), lambda i: (jnp.maximum(i * per - 1, 0), 0))],
        out_specs=pl.BlockSpec((tm, width), lambda i: (i, 0)), out_shape=jax.ShapeDtypeStruct(h.shape, BF16),
        scratch_shapes=[pltpu.VMEM((tm + POOL_HALO, width), F32)], compiler_params=_params(("parallel",)),
    )(h, h)


def pool_bwd(dp, name):
    n_tok, width = dp.shape
    gw = width // len(POOL_WINDOWS)
    tm = _pool_tile(n_tok)
    per = tm // POOL_HALO
    n_steps = n_tok // tm

    def body(cur_ref, halo_ref, o_ref, buf_ref):
        i = pl.program_id(0)
        tok = i * tm + lax.broadcasted_iota(jnp.int32, (tm, 1), 0)
        for g, w in enumerate(POOL_WINDOWS):
            cols = slice(g * gw, (g + 1) * gw)
            cnt = jnp.minimum(tok + 1, w).astype(F32)
            buf_ref[0:tm, cols] = cur_ref[:, cols] / cnt
            buf_ref[tm:, cols] = jnp.where(i < n_steps - 1, halo_ref[:, cols] / float(w), 0.0)
        for g, w in enumerate(POOL_WINDOWS):
            cols = slice(g * gw, (g + 1) * gw)
            acc = buf_ref[0:tm, cols]
            for j in range(1, w):
                acc = acc + buf_ref[j:j + tm, cols]
            o_ref[:, cols] = acc - cur_ref[:, cols]

    return pl.pallas_call(
        body, name=name, grid=(n_steps,),
        in_specs=[pl.BlockSpec((tm, width), lambda i: (i, 0)),
                  pl.BlockSpec((POOL_HALO, width), lambda i: (jnp.minimum((i + 1) * per, n_tok // POOL_HALO - 1), 0))],
        out_specs=pl.BlockSpec((tm, width), lambda i: (i, 0)), out_shape=jax.ShapeDtypeStruct(dp.shape, F32),
        scratch_shapes=[pltpu.VMEM((tm + POOL_HALO, width), F32)], compiler_params=_params(("parallel",)),
    )(dp, dp)


def _ffn_fwd(x, ga, gb, wg, wu, wd, tag):
    h = rms_fwd(x, ga, BF16, f"{tag}_prenorm")
    G = _mm(h, wg, batch="map", out_dtype=BF16, name=f"{tag}_gate")
    U = _mm(h, wu, batch="map", out_dtype=BF16, name=f"{tag}_up")
    A = swiglu_fwd(G, U, f"{tag}_swiglu")
    y = _mm(A, wd, batch="reduce", name=f"{tag}_down")
    return postnorm_res(x, y, gb, 0.5, f"{tag}_postnorm"), (x, h, G, U, A, y)


def _ffn_bwd(res, ga, gb, wg, wu, wd, dx2, tag):
    x, h, G, U, A, y = res
    dy, dgb = rms_bwd(y, gb, [dx2], 0.5, None, BF16, f"{tag}_postnorm_bwd")
    dA = _mm(dy, wd, tb=True, batch="map", out_dtype=BF16, name=f"{tag}_down_dx")
    dwd = _mm(A, dy, ta=True, batch="map", name=f"{tag}_down_dw")
    dG, dU = swiglu_bwd(G, U, dA, f"{tag}_swiglu_bwd")
    dwg = _mm(h, dG, ta=True, batch="map", name=f"{tag}_gate_dw")
    dwu = _mm(h, dU, ta=True, batch="map", name=f"{tag}_up_dw")
    dh1 = _mm(dG, wg, tb=True, batch="reduce", name=f"{tag}_gate_dx")
    dh2 = _mm(dU, wu, tb=True, batch="reduce", name=f"{tag}_up_dx")
    dx, dga = rms_bwd(x, ga, [dh1, dh2], 1.0, dx2, F32, f"{tag}_prenorm_bwd")
    return dx, dga, dgb, dwg, dwu, dwd


def _sink_rows(sinks):
    return jnp.repeat(sinks.reshape(N_KV_HEADS, Q_PER_KV), BLOCK, axis=1)[..., None]


def _attn_sgu_fwd(x, g_pre, g_post, w_in, w_out, sinks, ln_g, ln_b, sgu_w, bias_rows, tabs, tag):
    h = rms_fwd(x, g_pre, BF16, f"{tag}_prenorm")
    z = _mm(h, w_in, name=f"{tag}_in")
    qk = rope_apply(z, tabs, QK_WIDTH // LANES, False, BF16, f"{tag}_rope")
    q = _to_heads(qk[:, :ATTN_WIDTH], N_Q_HEADS)
    k = _to_heads(qk[:, ATTN_WIDTH:], N_KV_HEADS)
    v = _to_heads(z[:, QK_WIDTH:QK_WIDTH + KV_WIDTH].astype(BF16), N_KV_HEADS)
    o = swa_fwd(q, k, v, _sink_rows(sinks), f"{tag}_swa")
    u_pre = z[:, QK_WIDTH + KV_WIDTH:QK_WIDTH + KV_WIDTH + SGU_WIDTH]
    v_pre = z[:, QK_WIDTH + KV_WIDTH + SGU_WIDTH:]
    gate = sgu_fwd(u_pre, v_pre, ln_g, ln_b, sgu_w, bias_rows, f"{tag}_sgu")
    cat = jnp.concatenate([_from_heads(o), gate], axis=1)
    m = _mm(cat, w_out, name=f"{tag}_out")
    return postnorm_res(x, m, g_post, 1.0, f"{tag}_postnorm"), (x, h, q, k, v, u_pre, v_pre, cat, m)


def _attn_sgu_bwd(res, g_pre, g_post, w_in, w_out, sinks, ln_g, ln_b, sgu_w, bias_rows, tabs, dx2, tag):
    x, h, q, k, v, u_pre, v_pre, cat, m = res
    dm, dg_post = rms_bwd(m, g_post, [dx2], 1.0, None, BF16, f"{tag}_postnorm_bwd")
    dcat = _mm(dm, w_out, tb=True, out_dtype=BF16, name=f"{tag}_out_dx")
    dw_out = _mm(cat, dm, ta=True, name=f"{tag}_out_dw")
    do = _to_heads(dcat[:, :ATTN_WIDTH], N_Q_HEADS)
    dq, dkp, dkc, dvp, dvc, dsink = swa_bwd(q, k, v, _sink_rows(sinks), do, f"{tag}_swa_bwd")
    d_sinks = jnp.sum(dsink[..., 0], axis=1).reshape(1, N_Q_HEADS)
    dqk_rot = jnp.concatenate([_from_heads(dq).astype(F32), _fold_kv_grad(dkp, dkc)], axis=1)
    dqk = rope_apply(dqk_rot, tabs, QK_WIDTH // LANES, True, BF16, f"{tag}_rope_bwd")
    dv = _fold_kv_grad(dvp, dvc).astype(BF16)
    du_pre, dv_pre, dw_s, dbias, dlg, dlb = sgu_bwd(u_pre, v_pre, ln_g, ln_b, sgu_w, bias_rows, dcat[:, ATTN_WIDTH:], f"{tag}_sgu_bwd")
    dz = jnp.concatenate([dqk, dv, du_pre, dv_pre], axis=1)
    dw_in = _mm(h, dz, ta=True, name=f"{tag}_in_dw")
    dh = _mm(dz, w_in, tb=True, name=f"{tag}_in_dx")
    dx, dg_pre = rms_bwd(x, g_pre, [dh], 1.0, dx2, F32, f"{tag}_prenorm_bwd")
    causal = jnp.tril(jnp.ones((BLOCK, BLOCK), F32))
    small = dict(attn_sinks=d_sinks, sgu_ln_g=jnp.sum(dlg, axis=0, keepdims=True), sgu_ln_b=jnp.sum(dlb, axis=0, keepdims=True),
                 sgu_w=(dw_s * causal[None])[None], sgu_b=jnp.sum(dbias, axis=-1)[None])
    return dx, dg_pre, dg_post, dw_in, dw_out, small


def _pool_mix_fwd(x, g_pre, g_post, pool_w, pool_scale, tag):
    hf = rms_fwd(x, g_pre, F32, f"{tag}_prenorm")
    pooled = pool_fwd(hf, f"{tag}_pool")
    n_g = len(POOL_WINDOWS)
    ypre = _mm(pooled, pool_w, batch="map", groups=n_g, a_cb=True, o_cb=True, name=f"{tag}_proj")
    m = scale_cols(ypre, pool_scale, f"{tag}_scale")
    return postnorm_res(x, m, g_post, 1.0, f"{tag}_postnorm"), (x, pooled, ypre, m)


def _pool_mix_bwd(res, g_pre, g_post, pool_w, pool_scale, dx2, tag):
    x, pooled, ypre, m = res
    n_g = len(POOL_WINDOWS)
    dm, dg_post = rms_bwd(m, g_post, [dx2], 1.0, None, F32, f"{tag}_postnorm_bwd")
    dypre, dscale = scale_cols_bwd(dm, ypre, pool_scale, f"{tag}_scale_bwd")
    dpooled = _mm(dypre, pool_w, tb=True, batch="map", groups=n_g, a_cb=True, o_cb=True, name=f"{tag}_proj_dx")
    dpw = _mm(pooled, dypre, ta=True, batch="map", groups=n_g, a_cb=True, b_cb=True, name=f"{tag}_proj_dw")
    dhf = pool_bwd(dpooled, f"{tag}_pool_bwd")
    dx, dg_pre = rms_bwd(x, g_pre, [dhf], 1.0, dx2, F32, f"{tag}_prenorm_bwd")
    return dx, dg_pre, dg_post, dpw, jnp.sum(dscale, axis=0, keepdims=True)


def _xattn_fwd(x, mem, g_pre, g_post, g_mem, wq, wk, wv, wo, tag):
    h = rms_fwd(x, g_pre, BF16, f"{tag}_prenorm")
    mem_n = rms_fwd(mem, g_mem, BF16, f"{tag}_memnorm")
    q = _mm(h, wq, out_dtype=BF16, name=f"{tag}_q")
    k = _mm(mem_n, wk, out_dtype=BF16, name=f"{tag}_k")
    v = _mm(mem_n, wv, out_dtype=BF16, name=f"{tag}_v")
    o = xattn_fwd(q, k, v, f"{tag}_core")
    r = _mm(o, wo, name=f"{tag}_o")
    return postnorm_res(x, r, g_post, 1.0, f"{tag}_postnorm"), (x, h, mem_n, q, k, v, o, r)


def _xattn_bwd(res, mem, g_pre, g_post, g_mem, wq, wk, wv, wo, dx2, tag):
    x, h, mem_n, q, k, v, o, r = res
    dr, dg_post = rms_bwd(r, g_post, [dx2], 1.0, None, BF16, f"{tag}_postnorm_bwd")
    do = _mm(dr, wo, tb=True, out_dtype=BF16, name=f"{tag}_o_dx")
    dwo = _mm(o, dr, ta=True, name=f"{tag}_o_dw")
    dq, dk, dv = xattn_bwd(q, k, v, do, f"{tag}_core_bwd")
    dk, dv = dk.astype(BF16), dv.astype(BF16)
    dwq = _mm(h, dq, ta=True, name=f"{tag}_q_dw")
    dwk = _mm(mem_n, dk, ta=True, name=f"{tag}_k_dw")
    dwv = _mm(mem_n, dv, ta=True, name=f"{tag}_v_dw")
    dh = _mm(dq, wq, tb=True, name=f"{tag}_q_dx")
    dmem1 = _mm(dk, wk, tb=True, name=f"{tag}_k_dx")
    dmem2 = _mm(dv, wv, tb=True, name=f"{tag}_v_dx")
    _, dg_mem = rms_bwd(mem, g_mem, [dmem1, dmem2], 1.0, None, BF16, f"{tag}_memnorm_bwd")
    dx, dg_pre = rms_bwd(x, g_pre, [dh], 1.0, dx2, F32, f"{tag}_prenorm_bwd")
    return dx, dg_pre, dg_post, dg_mem, dwq, dwk, dwv, dwo


def _rowsum8(part):
    return jnp.sum(part, axis=0, keepdims=True)


def device_step(x, mem, target, norms, mem_norm, big, small):
    n_tok = x.shape[0]
    tabs = rope_tables(n_tok)
    bias_rows = jnp.broadcast_to(small["sgu_b"][:, :, None], (SGU_GROUPS, BLOCK, LANES))
    gn = lambda l, i: norms[l, i][None, :]
    gm = lambda l: mem_norm[l][None, :]
    ffn1 = lambda l: (big["ffn1_wg"][l], big["ffn1_wu"][l], big["ffn1_wd"][l])
    ffn2 = lambda l: (big["ffn2_wg"][l], big["ffn2_wu"][l], big["ffn2_wd"][l])
    xw = lambda l: (big["x_wq"][l], big["x_wk"][l], big["x_wv"][l], big["x_wo"][l])
    mix0 = (big["mix_w_in"][0], big["mix_w_out"][0], small["attn_sinks"], small["sgu_ln_g"], small["sgu_ln_b"], small["sgu_w"], bias_rows, tabs)
    mix1 = (big["pool_w"][0], small["pool_scale"])

    saved = []
    for l in range(2):
        x, r1 = _ffn_fwd(x, gn(l, 0), gn(l, 1), *ffn1(l), f"l{l}_ffn1")
        if l == 0:
            x, r2 = _attn_sgu_fwd(x, gn(l, 2), gn(l, 3), *mix0, f"l{l}_mix")
        else:
            x, r2 = _pool_mix_fwd(x, gn(l, 2), gn(l, 3), *mix1, f"l{l}_mix")
        x, r3 = _xattn_fwd(x, mem, gn(l, 4), gn(l, 5), gm(l), *xw(l), f"l{l}_xattn")
        x, r4 = _ffn_fwd(x, gn(l, 6), gn(l, 7), *ffn2(l), f"l{l}_ffn2")
        saved.append((r1, r2, r3, r4))

    loss_part, dx = loss_and_grad(x, target, "loss")

    gb = {k: [None, None] for k in ("ffn1_wg", "ffn1_wu", "ffn1_wd", "ffn2_wg", "ffn2_wu", "ffn2_wd", "x_wq", "x_wk", "x_wv", "x_wo")}
    g_norm_rows = [[None] * 8, [None] * 8]
    g_mem_rows = [None, None]
    g_small = {}
    for l in (1, 0):
        r1, r2, r3, r4 = saved[l]
        dx, g_norm_rows[l][6], g_norm_rows[l][7], gb["ffn2_wg"][l], gb["ffn2_wu"][l], gb["ffn2_wd"][l] = _ffn_bwd(
            r4, gn(l, 6), gn(l, 7), *ffn2(l), dx, f"l{l}_ffn2")
        dx, g_norm_rows[l][4], g_norm_rows[l][5], g_mem_rows[l], gb["x_wq"][l], gb["x_wk"][l], gb["x_wv"][l], gb["x_wo"][l] = _xattn_bwd(
            r3, mem, gn(l, 4), gn(l, 5), gm(l), *xw(l), dx, f"l{l}_xattn")
        if l == 0:
            dx, g_norm_rows[l][2], g_norm_rows[l][3], dw_in, dw_out, sm = _attn_sgu_bwd(r2, gn(l, 2), gn(l, 3), *mix0, dx, f"l{l}_mix")
            gb["mix_w_in"], gb["mix_w_out"] = [dw_in], [dw_out]
            g_small.update(sm)
        else:
            dx, g_norm_rows[l][2], g_norm_rows[l][3], dpw, dscale = _pool_mix_bwd(r2, gn(l, 2), gn(l, 3), *mix1, dx, f"l{l}_mix")
            gb["pool_w"] = [dpw]
            g_small["pool_scale"] = dscale
        dx, g_norm_rows[l][0], g_norm_rows[l][1], gb["ffn1_wg"][l], gb["ffn1_wu"][l], gb["ffn1_wd"][l] = _ffn_bwd(
            r1, gn(l, 0), gn(l, 1), *ffn1(l), dx, f"l{l}_ffn1")
    g_norms = jnp.stack([jnp.concatenate([_rowsum8(p) for p in g_norm_rows[l]], axis=0) for l in range(2)])
    g_mem_norm = jnp.concatenate([_rowsum8(p) for p in g_mem_rows], axis=0)
    return loss_part, dx, gb, g_small, g_norms, g_mem_norm


ANY = pl.BlockSpec(memory_space=pl.ANY)


def _place():
    x, y, c = lax.axis_index("x"), lax.axis_index("y"), lax.axis_index("c")
    other_chips = [(1 - x, y), (x, 1 - y), (1 - x, 1 - y)]
    return x, y, c, other_chips


def _half_rows(core, n_rows):
    half = n_rows // 2
    return pl.ds(pl.multiple_of(core * half, 16), half)


def _remote(src, dst, send_sem, recv_sem, device):
    return pltpu.make_async_remote_copy(src_ref=src, dst_ref=dst, send_sem=send_sem, recv_sem=recv_sem,
                                        device_id=device, device_id_type=MESH)


def gather_shards(shards, name):
    n = len(shards)

    def body(*refs):
        src, out = refs[:n], refs[n:2 * n]
        send_sems, recv_sems, local_sems = refs[2 * n:]
        x, y, c, chips = _place()
        me, sibling = 2 * x + y, (x, y, 1 - c)
        local = [pltpu.make_async_copy(src[i], out[i].at[me], local_sems.at[i]) for i in range(n)]
        for cp in local:
            cp.start()
        sends = []
        for i in range(n):
            mine = _half_rows(c, src[i].shape[0])
            for r, (px, py) in enumerate(chips):
                sends.append(_remote(src[i].at[mine], out[i].at[me, mine], send_sems.at[6 * i + r], recv_sems.at[6 * i + r], (px, py, c)))
                sends[-1].start()
        for i in range(n):
            mine = _half_rows(c, src[i].shape[0])
            for r, (px, py) in enumerate(chips):
                landed = out[i].at[2 * px + py, mine]
                _remote(landed, landed, send_sems.at[6 * i + r], recv_sems.at[6 * i + r], (px, py, c)).wait_recv()
                sends.append(_remote(landed, landed, send_sems.at[6 * i + 3 + r], recv_sems.at[6 * i + 3 + r], sibling))
                sends[-1].start()
        for i in range(n):
            theirs = _half_rows(1 - c, src[i].shape[0])
            for r, (px, py) in enumerate(chips):
                landed = out[i].at[2 * px + py, theirs]
                _remote(landed, landed, send_sems.at[6 * i + 3 + r], recv_sems.at[6 * i + 3 + r], sibling).wait_recv()
        for cp in sends:
            cp.wait_send()
        for cp in local:
            cp.wait()

    return pl.pallas_call(
        body, name=name, in_specs=[ANY] * n, out_specs=[ANY] * n,
        out_shape=[jax.ShapeDtypeStruct((N_CHIPS,) + s.shape, s.dtype) for s in shards],
        scratch_shapes=[pltpu.SemaphoreType.DMA((6 * n,)), pltpu.SemaphoreType.DMA((6 * n,)), pltpu.SemaphoreType.DMA((n,))],
    )(*shards)


def swap_other_halves(grads, name):
    n = len(grads)

    def body(*refs):
        src, out = refs[:n], refs[n:2 * n]
        send_sems, recv_sems = refs[2 * n:]
        x, y, c, _ = _place()
        copies = [_remote(src[i].at[:, _half_rows(1 - c, src[i].shape[1])], out[i], send_sems.at[i], recv_sems.at[i], (x, y, 1 - c))
                  for i in range(n)]
        for cp in copies:
            cp.start()
        for cp in copies:
            cp.wait()

    return pl.pallas_call(
        body, name=name, in_specs=[ANY] * n, out_specs=[ANY] * n,
        out_shape=[jax.ShapeDtypeStruct((g.shape[0], g.shape[1] // 2, g.shape[2]), g.dtype) for g in grads],
        scratch_shapes=[pltpu.SemaphoreType.DMA((n,)), pltpu.SemaphoreType.DMA((n,))],
    )(*grads)


def add_own_half(g, p, core, name):
    n_j, n_rows, n_cols = g.shape
    half = n_rows // 2
    tr = _rows(half, n_cols * 10)

    def body(c_ref, g_ref, p_ref, o_ref):
        o_ref[...] = (g_ref[...] + p_ref[...]).astype(o_ref.dtype)

    grid_spec = pltpu.PrefetchScalarGridSpec(
        num_scalar_prefetch=1, grid=(n_j, half // tr),
        in_specs=[pl.BlockSpec((None, None, tr, n_cols), lambda j, i, c_ref: (j, c_ref[0], i, 0)),
                  pl.BlockSpec((None, tr, n_cols), lambda j, i, c_ref: (j, i, 0))],
        out_specs=pl.BlockSpec((None, tr, n_cols), lambda j, i, c_ref: (j, i, 0)))
    return pl.pallas_call(
        body, name=name, grid_spec=grid_spec, out_shape=jax.ShapeDtypeStruct((n_j, half, n_cols), BF16),
        compiler_params=_params(("parallel", "parallel")),
    )(core, g.reshape(n_j, 2, half, n_cols), p)


def scatter_to_chips(parts, name):
    n = len(parts)

    def body(*refs):
        src, out = refs[:n], refs[n:2 * n]
        send_sems, recv_sems, local_sems = refs[2 * n:]
        x, y, c, chips = _place()
        me = 2 * x + y
        local = [pltpu.make_async_copy(src[i].at[me], out[i].at[me], local_sems.at[i]) for i in range(n)]
        for cp in local:
            cp.start()
        sends = []
        for i in range(n):
            for r, (px, py) in enumerate(chips):
                sends.append(_remote(src[i].at[2 * px + py], out[i].at[me], send_sems.at[3 * i + r], recv_sems.at[3 * i + r], (px, py, c)))
                sends[-1].start()
        for i in range(n):
            for r, (px, py) in enumerate(chips):
                landed = out[i].at[2 * px + py]
                _remote(landed, landed, send_sems.at[3 * i + r], recv_sems.at[3 * i + r], (px, py, c)).wait_recv()
        for cp in sends:
            cp.wait_send()
        for cp in local:
            cp.wait()

    return pl.pallas_call(
        body, name=name, in_specs=[ANY] * n, out_specs=[ANY] * n,
        out_shape=[jax.ShapeDtypeStruct(p.shape, p.dtype) for p in parts],
        scratch_shapes=[pltpu.SemaphoreType.DMA((3 * n,)), pltpu.SemaphoreType.DMA((3 * n,)), pltpu.SemaphoreType.DMA((n,))],
    )(*parts)


def sum_slots(q, name):
    n_s, n_rows, n_cols = q.shape
    tr = _rows(n_rows, n_cols * (q.dtype.itemsize * n_s + 4))

    def body(q_ref, o_ref):
        acc = q_ref[0].astype(F32)
        for s in range(1, n_s):
            acc = acc + q_ref[s].astype(F32)
        o_ref[...] = acc

    return pl.pallas_call(
        body, name=name, grid=(n_rows // tr,), in_specs=[pl.BlockSpec((n_s, tr, n_cols), lambda i: (0, i, 0))],
        out_specs=pl.BlockSpec((tr, n_cols), lambda i: (i, 0)), out_shape=jax.ShapeDtypeStruct((n_rows, n_cols), F32),
        compiler_params=_params(("parallel",)),
    )(q)


def join_halves(halves, name):
    flat = [h for per in halves for h in per]
    n = len(flat)

    def body(*refs):
        src, out = refs[:n], refs[n:n + len(halves)]
        send_sems, recv_sems, local_sems = refs[n + len(halves):]
        x, y, c, _ = _place()
        sibling = (x, y, 1 - c)
        local, sends, k = [], [], 0
        for w, per in enumerate(halves):
            for l in range(len(per)):
                mine = _half_rows(c, out[w].shape[1])
                local.append(pltpu.make_async_copy(src[k], out[w].at[l, mine], local_sems.at[k]))
                sends.append(_remote(src[k], out[w].at[l, mine], send_sems.at[k], recv_sems.at[k], sibling))
                local[-1].start()
                sends[-1].start()
                k += 1
        k = 0
        for w, per in enumerate(halves):
            for l in range(len(per)):
                theirs = out[w].at[l, _half_rows(1 - c, out[w].shape[1])]
                _remote(theirs, theirs, send_sems.at[k], recv_sems.at[k], sibling).wait_recv()
                k += 1
        for cp in sends:
            cp.wait_send()
        for cp in local:
            cp.wait()

    return pl.pallas_call(
        body, name=name, in_specs=[ANY] * n, out_specs=[ANY] * len(halves),
        out_shape=[jax.ShapeDtypeStruct((len(per), 2 * per[0].shape[0], per[0].shape[1]), F32) for per in halves],
        scratch_shapes=[pltpu.SemaphoreType.DMA((n,)), pltpu.SemaphoreType.DMA((n,)), pltpu.SemaphoreType.DMA((n,))],
    )(*flat)


def gather_all_devices(s, name):
    def body(s_ref, o_ref, send_sems, recv_sems, local_sem):
        x, y, c, _ = _place()
        me = 4 * x + 2 * y + c
        local = pltpu.make_async_copy(s_ref, o_ref.at[me], local_sem)
        local.start()
        sends = []
        for f in range(1, N_DEV):
            px, py, pc = x ^ (f >> 2), y ^ ((f >> 1) & 1), c ^ (f & 1)
            sends.append(_remote(s_ref, o_ref.at[me], send_sems.at[f - 1], recv_sems.at[f - 1], (px, py, pc)))
            sends[-1].start()
        for f in range(1, N_DEV):
            px, py, pc = x ^ (f >> 2), y ^ ((f >> 1) & 1), c ^ (f & 1)
            landed = o_ref.at[4 * px + 2 * py + pc]
            _remote(landed, landed, send_sems.at[f - 1], recv_sems.at[f - 1], (px, py, pc)).wait_recv()
        for cp in sends:
            cp.wait_send()
        local.wait()

    return pl.pallas_call(
        body, name=name, in_specs=[ANY], out_specs=ANY, out_shape=jax.ShapeDtypeStruct((N_DEV,) + s.shape, s.dtype),
        scratch_shapes=[pltpu.SemaphoreType.DMA((N_DEV - 1,)), pltpu.SemaphoreType.DMA((N_DEV - 1,)), pltpu.SemaphoreType.DMA],
    )(s)


FFN_NAMES = ("ffn1_wg", "ffn1_wu", "ffn1_wd", "ffn2_wg", "ffn2_wu", "ffn2_wd")
XATTN_NAMES = ("x_wq", "x_wk", "x_wv", "x_wo")
BIG_NAMES = FFN_NAMES + XATTN_NAMES + ("mix_w_in", "mix_w_out", "pool_w")
SMALL_NAMES = ("norms", "mem_norm", "attn_sinks", "sgu_ln_g", "sgu_ln_b", "sgu_w", "sgu_b", "pool_scale")
WEIGHT_ORDER = ("norms", "mem_norm") + BIG_NAMES[:-1] + ("attn_sinks", "sgu_ln_g", "sgu_ln_b", "sgu_w", "sgu_b", "pool_w", "pool_scale")
COLUMN_CUT = ("x_wo", "mix_w_in")
N_POOL = len(POOL_WINDOWS)


def _to_matmul_layout(name, g):
    n_j, n_rows, n_cols = g.shape
    if name in COLUMN_CUT:
        return g.transpose(1, 0, 2).reshape(n_rows, n_j * n_cols)
    if name == "pool_w":
        return g.reshape(n_j, N_POOL, n_rows // N_POOL, n_cols).transpose(1, 0, 2, 3).reshape(N_POOL, n_j * n_rows // N_POOL, n_cols)
    if name in ("x_wq", "x_wk", "x_wv", "mix_w_out"):
        return g.reshape(n_j * n_rows, n_cols)
    return g


def _from_matmul_layout(name, d):
    if name in COLUMN_CUT:
        n_rows, wide = d.shape
        return d.reshape(n_rows, N_CHIPS, wide // N_CHIPS).transpose(1, 0, 2)
    if name == "pool_w":
        n_g, n_in, n_cols = d.shape
        return d.reshape(n_g, N_CHIPS, n_in // N_CHIPS, n_cols).transpose(1, 0, 2, 3).reshape(N_CHIPS, n_g * n_in // N_CHIPS, n_cols)
    if name in ("x_wq", "x_wk", "x_wv", "mix_w_out"):
        return d.reshape(N_CHIPS, d.shape[0] // N_CHIPS, d.shape[1])
    return d


def _as3(w):
    return w.reshape(w.shape[0], -1, w.shape[-1])


def _pack(arrays, row_multiple):
    flat = jnp.concatenate([a.reshape(-1) for a in arrays])
    per = LANES * row_multiple
    total = -(-flat.shape[0] // per) * per
    return jnp.pad(flat, (0, total - flat.shape[0])).reshape(total // LANES, LANES)


def _unpack(packed, like):
    flat, out, at = packed.reshape(-1), [], 0
    for a in like:
        out.append(flat[at:at + a.size].reshape(a.shape))
        at += a.size
    return out


def kernel(x, mem, norms, mem_norm, ffn1_wg, ffn1_wu, ffn1_wd, ffn2_wg, ffn2_wu, ffn2_wd, x_wq, x_wk, x_wv, x_wo, mix_w_in, mix_w_out, attn_sinks, sgu_ln_g, sgu_ln_b, sgu_w, sgu_b, pool_w, pool_scale, loss_target, m_norms, m_mem_norm, m_ffn1_wg, m_ffn1_wu, m_ffn1_wd, m_ffn2_wg, m_ffn2_wu, m_ffn2_wd, m_x_wq, m_x_wk, m_x_wv, m_x_wo, m_mix_w_in, m_mix_w_out, m_attn_sinks, m_sgu_ln_g, m_sgu_ln_b, m_sgu_w, m_sgu_b, m_pool_w, m_pool_scale, v_norms, v_mem_norm, v_ffn1_wg, v_ffn1_wu, v_ffn1_wd, v_ffn2_wg, v_ffn2_wu, v_ffn2_wd, v_x_wq, v_x_wk, v_x_wv, v_x_wo, v_mix_w_in, v_mix_w_out, v_attn_sinks, v_sgu_ln_g, v_sgu_ln_b, v_sgu_w, v_sgu_b, v_pool_w, v_pool_scale):
    given = dict(locals())
    w = {n: given[n] for n in WEIGHT_ORDER}
    mom = {n: given["m_" + n] for n in WEIGHT_ORDER}
    var = {n: given["v_" + n] for n in WEIGHT_ORDER}
    chip = 2 * lax.axis_index("x") + lax.axis_index("y")
    core = lax.axis_index("c").astype(jnp.int32).reshape(1)
    n_shard = norms.shape[-1]

    shards, keys = [], []
    for name in BIG_NAMES:
        w3 = _as3(w[name])
        for l in range(w3.shape[0]):
            shards.append(cast_layer(w3, l, f"cast_{name}"))
            keys.append((name, l))
    small_rows = jnp.concatenate([norms.reshape(-1, n_shard), pool_scale, jnp.zeros((15, n_shard), F32)], axis=0)
    gathered = gather_shards(shards + [small_rows], "gather_weights")
    big = {name: [] for name in BIG_NAMES}
    for (name, l), g in zip(keys, gathered[:-1]):
        big[name].append(_to_matmul_layout(name, g))
    small_all = gathered[-1]
    n_norm_rows = norms.shape[0] * norms.shape[1]
    norms_all = jnp.concatenate([small_all[j, :n_norm_rows].reshape(norms.shape) for j in range(N_CHIPS)], axis=-1)
    pool_scale_all = jnp.concatenate([small_all[j, n_norm_rows:n_norm_rows + 1] for j in range(N_CHIPS)], axis=-1)
    small = dict(attn_sinks=attn_sinks[0], sgu_ln_g=sgu_ln_g, sgu_ln_b=sgu_ln_b, sgu_w=sgu_w[0], sgu_b=sgu_b[0], pool_scale=pool_scale_all)

    loss_part, dx, g_big, g_small, g_norms, g_mem_norm = device_step(x[0], mem[0], loss_target[0], norms_all, mem_norm, big, small)
    loss = lax.psum(0.5 * jnp.sum(loss_part) / x.shape[-1], ("x", "y", "c"))

    full = [_from_matmul_layout(name, g_big[name][l]) for name, l in keys]
    from_sibling = swap_other_halves(full, "grads_to_sibling")
    chip_sums = [add_own_half(g, p, core, f"chip_sum_{name}") for (name, l), g, p in zip(keys, full, from_sibling)]
    per_chip = scatter_to_chips(chip_sums, "grads_to_chips")
    reduced = {name: [] for name in BIG_NAMES}
    for (name, l), q in zip(keys, per_chip):
        reduced[name].append(sum_slots(q, f"sum_{name}"))
    joined = join_halves([reduced[name] for name in BIG_NAMES], "grads_join")
    grads = {name: g.reshape(w[name].shape) for name, g in zip(BIG_NAMES, joined)}

    small_g = [g_norms, g_mem_norm, g_small["attn_sinks"], g_small["sgu_ln_g"], g_small["sgu_ln_b"], g_small["sgu_w"], g_small["sgu_b"],
               g_small["pool_scale"]]
    packed = _pack(small_g, 16)
    summed = sum_slots(gather_all_devices(packed, "small_grads_all"), "small_grads_sum")
    s_norms, s_mem, s_sinks, s_lg, s_lb, s_w, s_b, s_scale = _unpack(summed, small_g)
    grads["norms"] = lax.dynamic_slice_in_dim(s_norms, chip * n_shard, n_shard, axis=2)
    grads["pool_scale"] = lax.dynamic_slice_in_dim(s_scale, chip * n_shard, n_shard, axis=1)
    grads.update(mem_norm=s_mem, attn_sinks=s_sinks, sgu_ln_g=s_lg, sgu_ln_b=s_lb, sgu_w=s_w, sgu_b=s_b)

    delta, new_m, new_v = {}, {}, {}
    for name in BIG_NAMES:
        d, m2, v2 = adamw(_as3(w[name]), _as3(grads[name]), _as3(mom[name]), _as3(var[name]), f"adamw_{name}")
        delta[name], new_m[name], new_v[name] = (t.reshape(w[name].shape) for t in (d, m2, v2))
    like = [w[n] for n in SMALL_NAMES]
    packs = [_pack([src[n] for n in SMALL_NAMES], 16)[None] for src in (w, grads, mom, var)]
    for dst, t in zip((delta, new_m, new_v), adamw(*packs, "adamw_small")):
        for n, a in zip(SMALL_NAMES, _unpack(t[0], like)):
            dst[n] = a

    outs = [loss, dx[None]]
    for group in (grads, delta, new_m, new_v):
        outs += [group[n] for n in WEIGHT_ORDER]
    return tuple(outs)
```

```python
import functools

import jax
import jax.numpy as jnp
from jax import lax
from jax.experimental import pallas as pl
from jax.experimental.pallas import tpu as pltpu

F32 = jnp.float32
BF16 = jnp.bfloat16
MESH = pl.DeviceIdType.MESH

EPS = 1e-6
ROPE_THETA = 500000.0
ROPE_HALF = 8
HEAD_DIM = 64
N_Q_HEADS = 16
N_KV_HEADS = 2
Q_PER_KV = 8
BLOCK = 128
ATTN_WIDTH = 1024
KV_WIDTH = 128
QK_WIDTH = ATTN_WIDTH + KV_WIDTH
SGU_WIDTH = 1024
SGU_GROUPS = 8
POOL_WINDOWS = (2, 4, 8, 16)
POOL_HALO = 16
X_HEADS = 4
X_HEAD_DIM = 128
N_CHIPS = 4
N_DEV = 8

ADAM_LR = 0.001
ADAM_B1 = 0.9
ADAM_B2 = 0.999
ADAM_EPS = 1e-08
ADAM_WD = 0.01
ADAM_STEP = 10

VMEM_LIMIT_V7X = 52 * 1024 * 1024
MM_VMEM_BUDGET = 40 * 1024 * 1024
LANES = 128
ROW_TILE_BYTES = 6 * 1024 * 1024
MXU_FLOPS_V7X = 1.0e15
HBM_BYTES_PER_S_V7X = 3.0e12
VMEM_STORE_BYTES_PER_S = 4.0e12
MXU_WEIGHT_LOAD_ROWS = 192
GRID_STEP_S = 0.35e-6


def _params(sem):
    return pltpu.CompilerParams(dimension_semantics=sem, vmem_limit_bytes=VMEM_LIMIT_V7X)


def _pick(dim, pref, align):
    cands = [t for t in range(align, dim + 1, align) if dim % t == 0]
    small = [t for t in cands if t <= pref]
    if small and small[-1] * 2 >= min(pref, dim):
        return small[-1]
    return dim


def _rows(n_rows, bytes_per_row):
    want = max(16, min(1024, ROW_TILE_BYTES // max(1, bytes_per_row)))
    cands = [t for t in range(16, n_rows + 1, 16) if n_rows % t == 0 and t <= want]
    return cands[-1] if cands else n_rows


def _divisors(dim, align, most):
    return [t for t in range(align, min(dim, most) + 1, align) if dim % t == 0] or [dim]


def _mm_tiles(M, N, K, J, m_align, k_align, a_bytes, b_bytes, o_bytes, reduce, ta):
    best = None
    for tm in _divisors(M, m_align, 2048):
        for tn in _divisors(N, LANES, 2048):
            for tk in _divisors(K, k_align, 4096):
                split = K // tk > 1 or reduce
                vmem = 2 * (tm * tk * a_bytes + tk * tn * b_bytes + tm * tn * o_bytes) + tm * tn * 4 * (2 if split else 1)
                if ta:
                    vmem += tm * tk * a_bytes
                if vmem > MM_VMEM_BUDGET:
                    continue
                steps = J * (M // tm) * (N // tn) * (K // tk)
                mxu = 2.0 * J * M * N * K / MXU_FLOPS_V7X * (tm + MXU_WEIGHT_LOAD_ROWS) / tm
                acc = J * M * N * (K // tk) * 8 / VMEM_STORE_BYTES_PER_S if split else 0.0
                hbm = J * (M * K * a_bytes * (N // tn) + K * N * b_bytes * (M // tm) + M * N * o_bytes) / HBM_BYTES_PER_S_V7X
                cost = max(mxu + 0.5 * acc, hbm) + steps * GRID_STEP_S
                if best is None or cost < best[0]:
                    best = (cost, tm, tn, tk)
    return best[1:]


def _rsum8(v):
    r, c = v.shape
    return v.reshape(r // 8, 8, c).sum(axis=0)


def _mm(a, b, *, name, ta=False, tb=False, batch="none", groups=0, a_cb=False, b_cb=False, o_cb=False,
        out_dtype=F32):
    J = groups or (a.shape[0] if a.ndim == 3 else (b.shape[0] if b.ndim == 3 else 1))
    a2, b2 = a.shape[-2:], b.shape[-2:]
    M, K = (a2[1], a2[0]) if ta else a2
    N, Kb = b2 if tb else (b2[1], b2[0])
    if a_cb:
        if ta:
            M //= J
        else:
            K //= J
    if b_cb:
        if tb:
            Kb //= J
        else:
            N //= J
    assert K == Kb, (name, a.shape, b.shape)
    reduce = batch == "reduce"
    tm, tn, tk = _mm_tiles(M, N, K, J, LANES if ta else 16, LANES if (not ta or tb) else 16, a.dtype.itemsize, b.dtype.itemsize,
                           jnp.dtype(out_dtype).itemsize, reduce, ta)
    nm, nn, nk = M // tm, N // tn, K // tk
    if reduce:
        grid = (nm, nn, J, nk)
        unpack = lambda m, n, j, k: (j, m, n, k)
        sem = ("parallel", "parallel", "arbitrary", "arbitrary")
    else:
        grid = (J, nm, nn, nk)
        unpack = lambda j, m, n, k: (j, m, n, k)
        sem = ("parallel", "parallel", "parallel", "arbitrary")

    def a_map(*g):
        j, m, n, k = unpack(*g)
        r, c = (k, m) if ta else (m, k)
        if a_cb:
            c = c + j * (nm if ta else nk)
        return (j, r, c) if a.ndim == 3 else (r, c)

    def b_map(*g):
        j, m, n, k = unpack(*g)
        r, c = (n, k) if tb else (k, n)
        if b_cb:
            c = c + j * (nk if tb else nn)
        return (j, r, c) if b.ndim == 3 else (r, c)

    def o_map(*g):
        j, m, n, k = unpack(*g)
        if o_cb:
            return (m, n + j * nn)
        return (j, m, n) if batch == "map" else (m, n)

    a_blk = (tk, tm) if ta else (tm, tk)
    b_blk = (tn, tk) if tb else (tk, tn)
    a_spec = pl.BlockSpec(((None,) + a_blk) if a.ndim == 3 else a_blk, a_map)
    b_spec = pl.BlockSpec(((None,) + b_blk) if b.ndim == 3 else b_blk, b_map)
    if o_cb:
        out_shape, o_blk = (M, N * J), (tm, tn)
    elif batch == "map":
        out_shape, o_blk = (J, M, N), (None, tm, tn)
    else:
        out_shape, o_blk = (M, N), (tm, tn)
    dims = (((0 if ta else 1,), (1 if tb else 0,)), ((), ()))
    red_axes = (2, 3) if reduce else (3,)

    split = reduce or nk > 1

    def body(a_ref, b_ref, o_ref, *acc):
        prod = lax.dot_general(a_ref[...], b_ref[...], dims, preferred_element_type=F32)
        if not split:
            o_ref[...] = prod.astype(o_ref.dtype)
            return
        acc_ref, = acc
        first = functools.reduce(jnp.logical_and, [pl.program_id(ax) == 0 for ax in red_axes])
        last = functools.reduce(jnp.logical_and, [pl.program_id(ax) == grid[ax] - 1 for ax in red_axes])

        @pl.when(first)
        def _():
            acc_ref[...] = prod

        @pl.when(jnp.logical_not(first))
        def _():
            acc_ref[...] += prod

        @pl.when(last)
        def _():
            o_ref[...] = acc_ref[...].astype(o_ref.dtype)

    return pl.pallas_call(
        body, name=name, grid=grid, in_specs=[a_spec, b_spec], out_specs=pl.BlockSpec(o_blk, o_map),
        out_shape=jax.ShapeDtypeStruct(out_shape, out_dtype), scratch_shapes=[pltpu.VMEM((tm, tn), F32)] if split else [],
        compiler_params=_params(sem),
    )(a, b)


def _rowwise(fn, tiled, whole, outs, accs, *, name):
    n_rows = tiled[0].shape[0]
    row_bytes = sum(t.shape[1] * t.dtype.itemsize for t in tiled) + sum(c * jnp.dtype(d).itemsize for c, d in outs)
    tr = _rows(n_rows, row_bytes)
    n_t, n_w, n_o = len(tiled), len(whole), len(outs)

    def body(*refs):
        i = pl.program_id(0)
        t_refs, w_refs = refs[:n_t], refs[n_t:n_t + n_w]
        o_refs, a_refs = refs[n_t + n_w:n_t + n_w + n_o], refs[n_t + n_w + n_o:]
        o_vals, a_vals = fn(i, *[r[...] for r in t_refs], *[r[...] for r in w_refs])
        for r, v in zip(o_refs, o_vals):
            r[...] = v.astype(r.dtype)
        if a_refs:
            @pl.when(i == 0)
            def _():
                for r in a_refs:
                    r[...] = jnp.zeros_like(r)
            for r, v in zip(a_refs, a_vals):
                r[...] += v

    in_specs = [pl.BlockSpec((tr, t.shape[1]), lambda i: (i, 0)) for t in tiled]
    in_specs += [pl.BlockSpec(w.shape, lambda i, nd=w.ndim: (0,) * nd) for w in whole]
    out_specs = [pl.BlockSpec((tr, c), lambda i: (i, 0)) for c, _ in outs]
    out_specs += [pl.BlockSpec(s, lambda i, nd=len(s): (0,) * nd) for s, _ in accs]
    out_shape = [jax.ShapeDtypeStruct((n_rows, c), d) for c, d in outs]
    out_shape += [jax.ShapeDtypeStruct(s, d) for s, d in accs]
    res = pl.pallas_call(
        body, name=name, grid=(n_rows // tr,), in_specs=in_specs, out_specs=out_specs, out_shape=out_shape,
        compiler_params=_params(("arbitrary",) if accs else ("parallel",)),
    )(*tiled, *whole)
    return res


def _rms_stats(x):
    r = lax.rsqrt(jnp.mean(x * x, axis=-1, keepdims=True) + EPS)
    return x * r, r


def rms_fwd(x, g, out_dtype, name):
    def fn(i, x, g):
        xhat, _ = _rms_stats(x)
        return [xhat * g], []
    return _rowwise(fn, [x], [g], [(x.shape[1], out_dtype)], [], name=name)[0]


def postnorm_res(x, y, g, s, name):
    def fn(i, x, y, g):
        yhat, _ = _rms_stats(y)
        return [x + s * (yhat * g)], []
    return _rowwise(fn, [x, y], [g], [(x.shape[1], F32)], [], name=name)[0]


def rms_bwd(xin, g, douts, scale, add, out_dtype, name):
    n_d = len(douts)

    def fn(i, x, *rest):
        ds, rest = rest[:n_d], rest[n_d:]
        ad = rest[0] if add is not None else None
        g = rest[-1]
        xhat, r = _rms_stats(x)
        d = ds[0].astype(F32)
        for e in ds[1:]:
            d = d + e.astype(F32)
        if scale != 1.0:
            d = d * scale
        dg = _rsum8(d * xhat)
        dxhat = d * g
        dx = r * (dxhat - xhat * jnp.mean(dxhat * xhat, axis=-1, keepdims=True))
        if ad is not None:
            dx = dx + ad
        return [dx], [dg]

    tiled = [xin, *douts] + ([add] if add is not None else [])
    dx, dg = _rowwise(fn, tiled, [g], [(xin.shape[1], out_dtype)], [((8, xin.shape[1]), F32)], name=name)
    return dx, dg


def _silu_parts(g):
    sg = 1.0 / (1.0 + jnp.exp(-g))
    return g * sg, sg


def swiglu_fwd(G, U, name):
    shp = G.shape

    def fn(i, g, u):
        s, _ = _silu_parts(g.astype(F32))
        return [s * u.astype(F32)], []
    a = _rowwise(fn, [G.reshape(-1, shp[-1]), U.reshape(-1, shp[-1])], [], [(shp[-1], BF16)], [], name=name)[0]
    return a.reshape(shp)


def swiglu_bwd(G, U, dA, name):
    shp = G.shape

    def fn(i, g, u, da):
        g, u, da = g.astype(F32), u.astype(F32), da.astype(F32)
        s, sg = _silu_parts(g)
        dsilu = sg * (1.0 + g * (1.0 - sg))
        return [da * u * dsilu, da * s], []
    flat = lambda t: t.reshape(-1, shp[-1])
    dG, dU = _rowwise(fn, [flat(G), flat(U), flat(dA)], [], [(shp[-1], BF16), (shp[-1], BF16)], [], name=name)
    return dG.reshape(shp), dU.reshape(shp)


def scale_cols(y, s, name):
    def fn(i, y, s):
        return [y * s], []
    return _rowwise(fn, [y], [s], [(y.shape[1], F32)], [], name=name)[0]


def scale_cols_bwd(dm, y, s, name):
    def fn(i, dm, y, s):
        return [dm * s], [_rsum8(dm * y)]
    return _rowwise(fn, [dm, y], [s], [(y.shape[1], BF16)], [((8, y.shape[1]), F32)], name=name)


def loss_and_grad(y, target, name):
    n_feat = y.shape[1]

    def fn(i, y, t):
        e = y - t
        return [e * (1.0 / n_feat)], [_rsum8(e * e)]
    dy, part = _rowwise(fn, [y, target], [], [(n_feat, F32)], [((8, n_feat), F32)], name=name)
    return part, dy


def cast_into_slot(w3, layer, chip, name):
    _, n_rows, n_cols = w3.shape
    tr = _rows(n_rows, n_cols * 6)

    def body(chip_ref, w_ref, o_ref):
        o_ref[...] = w_ref[...].astype(BF16)

    grid_spec = pltpu.PrefetchScalarGridSpec(
        num_scalar_prefetch=1, grid=(n_rows // tr,),
        in_specs=[pl.BlockSpec((None, tr, n_cols), lambda i, chip_ref: (layer, i, 0))],
        out_specs=pl.BlockSpec((None, tr, n_cols), lambda i, chip_ref: (chip_ref[0], i, 0)))
    return pl.pallas_call(
        body, name=name, grid_spec=grid_spec, out_shape=jax.ShapeDtypeStruct((N_CHIPS, n_rows, n_cols), BF16),
        compiler_params=_params(("parallel",)),
    )(chip, w3)


def _adam_update(w, g, m, v):
    c1 = 1.0 / (1.0 - ADAM_B1 ** ADAM_STEP)
    c2 = 1.0 / (1.0 - ADAM_B2 ** ADAM_STEP)
    m = ADAM_B1 * m + (1.0 - ADAM_B1) * g
    v = ADAM_B2 * v + (1.0 - ADAM_B2) * (g * g)
    return -ADAM_LR * ((m * c1) / (jnp.sqrt(v * c2) + ADAM_EPS) + ADAM_WD * w), m, v


def adamw(w, g, m, v, name):
    n_l, n_rows, n_cols = w.shape
    tr = _rows(n_rows, n_cols * 4 * 7)

    def body(w_ref, g_ref, m_ref, v_ref, d_ref, mo_ref, vo_ref):
        d_ref[...], mo_ref[...], vo_ref[...] = _adam_update(w_ref[...], g_ref[...], m_ref[...], v_ref[...])

    spec = pl.BlockSpec((None, tr, n_cols), lambda l, i: (l, i, 0))
    shp = jax.ShapeDtypeStruct(w.shape, F32)
    return pl.pallas_call(
        body, name=name, grid=(n_l, n_rows // tr), in_specs=[spec] * 4, out_specs=[spec] * 3, out_shape=[shp] * 3,
        compiler_params=_params(("parallel", "parallel")),
    )(w, g, m, v)


def adamw_from_halves(w, own, recv, m, v, core, name):
    n_l, n_rows, n_cols = w.shape
    half = n_rows // 2
    tr = _rows(half, n_cols * 4 * 9)
    per = half // tr

    def body(core_ref, *refs):
        w_ref, m_ref, v_ref = refs[:3]
        own_refs, recv_refs = refs[3:3 + n_l], refs[3 + n_l:3 + 2 * n_l]
        g_ref, d_ref, mo_ref, vo_ref = refs[3 + 2 * n_l:]
        l, h = pl.program_id(0), pl.program_id(1)
        mine = h == core_ref[0]
        g = jnp.where(mine, own_refs[0][...], recv_refs[0][...])
        for k in range(1, n_l):
            g = jnp.where(l == k, jnp.where(mine, own_refs[k][...], recv_refs[k][...]), g)
        g_ref[...] = g
        d_ref[...], mo_ref[...], vo_ref[...] = _adam_update(w_ref[...], g, m_ref[...], v_ref[...])

    full = pl.BlockSpec((None, tr, n_cols), lambda l, h, i, core_ref: (l, h * per + i, 0))

    def piece(layer, is_own):
        def index(l, h, i, core_ref):
            used = (l == layer) & ((h == core_ref[0]) == is_own)
            return (jnp.where(used, i, 0), 0)
        return pl.BlockSpec((tr, n_cols), index)

    grid_spec = pltpu.PrefetchScalarGridSpec(
        num_scalar_prefetch=1, grid=(n_l, 2, per),
        in_specs=[full] * 3 + [piece(k, True) for k in range(n_l)] + [piece(k, False) for k in range(n_l)],
        out_specs=[full] * 4)
    return pl.pallas_call(
        body, name=name, grid_spec=grid_spec, out_shape=[jax.ShapeDtypeStruct(w.shape, F32)] * 4,
        compiler_params=_params(("parallel", "parallel", "parallel")),
    )(core, w, m, v, *own, *recv)


def rope_tables(n_tok):
    inv = ROPE_THETA ** (-jnp.arange(ROPE_HALF, dtype=F32) * 2.0 / (2 * ROPE_HALF))
    ang = jnp.arange(n_tok, dtype=F32)[:, None] * inv[None, :]
    cos, sin = jnp.cos(ang), jnp.sin(ang)
    rest = HEAD_DIM - 2 * ROPE_HALF
    one, zero, z8 = jnp.ones((n_tok, rest), F32), jnp.zeros((n_tok, rest), F32), jnp.zeros((n_tok, ROPE_HALF), F32)
    c = jnp.concatenate([cos, cos, one], axis=1)
    s1 = jnp.concatenate([-sin, z8, zero], axis=1)
    s2 = jnp.concatenate([z8, sin, zero], axis=1)
    two = lambda t: jnp.concatenate([t, t], axis=1)
    return two(c), two(s1), two(s2)


def rope_apply(x, tabs, n_col_blocks, inverse, out_dtype, name):
    n_tok = x.shape[0]
    tr = _rows(n_tok, LANES * 4 * 6)

    def body(x_ref, c_ref, s1_ref, s2_ref, o_ref):
        x = x_ref[...].astype(F32)
        if inverse:
            out = x * c_ref[...] + pltpu.roll(x * s1_ref[...], ROPE_HALF, 1) + pltpu.roll(x * s2_ref[...], LANES - ROPE_HALF, 1)
        else:
            out = x * c_ref[...] + pltpu.roll(x, LANES - ROPE_HALF, 1) * s1_ref[...] + pltpu.roll(x, ROPE_HALF, 1) * s2_ref[...]
        o_ref[...] = out.astype(o_ref.dtype)

    tab_spec = pl.BlockSpec((tr, LANES), lambda i, c: (i, 0))
    blk = pl.BlockSpec((tr, LANES), lambda i, c: (i, c))
    return pl.pallas_call(
        body, name=name, grid=(n_tok // tr, n_col_blocks), in_specs=[blk, tab_spec, tab_spec, tab_spec], out_specs=blk,
        out_shape=jax.ShapeDtypeStruct((n_tok, n_col_blocks * LANES), out_dtype), compiler_params=_params(("parallel", "parallel")),
    )(x, *tabs)


def _swa_probs(q, k, sink, n):
    rows = Q_PER_KV * BLOCK
    s = lax.dot_general(q, k, (((1,), (1,)), ((), ())), preferred_element_type=F32) * (HEAD_DIM ** -0.5)
    qi = lax.broadcasted_iota(jnp.int32, (rows, 2 * BLOCK), 0) & (BLOCK - 1)
    kj = lax.broadcasted_iota(jnp.int32, (rows, 2 * BLOCK), 1)
    rel = qi + BLOCK - kj
    valid = (rel >= 0) & (rel < BLOCK) & ((n > 0) | (kj >= BLOCK))
    s = jnp.where(valid, s, -1e30)
    m = jnp.maximum(jnp.max(s, axis=-1, keepdims=True), sink)
    e = jnp.exp(s - m)
    es = jnp.exp(sink - m)
    inv = 1.0 / (jnp.sum(e, axis=-1, keepdims=True) + es)
    return e * inv, es * inv


def _swa_specs(n_blocks):
    q_spec = pl.BlockSpec((Q_PER_KV, BLOCK, HEAD_DIM), lambda h, n: (h, n, 0))
    prev = pl.BlockSpec((None, BLOCK, HEAD_DIM), lambda h, n: (h, jnp.maximum(n - 1, 0), 0))
    cur = pl.BlockSpec((None, BLOCK, HEAD_DIM), lambda h, n: (h, n, 0))
    sink = pl.BlockSpec((None, Q_PER_KV * BLOCK, 1), lambda h, n: (h, 0, 0))
    return q_spec, prev, cur, sink


def swa_fwd(q, k, v, sink_rows, name):
    n_tok = q.shape[1]
    q_spec, prev, cur, sink = _swa_specs(n_tok // BLOCK)

    def body(q_ref, kp_ref, kc_ref, vp_ref, vc_ref, s_ref, o_ref):
        n = pl.program_id(1)
        qq = q_ref[...].reshape(Q_PER_KV * BLOCK, HEAD_DIM)
        kk = jnp.concatenate([kp_ref[...], kc_ref[...]], axis=0)
        vv = jnp.concatenate([vp_ref[...], vc_ref[...]], axis=0)
        p, _ = _swa_probs(qq, kk, s_ref[...], n)
        o = jnp.dot(p.astype(BF16), vv, preferred_element_type=F32)
        o_ref[...] = o.reshape(Q_PER_KV, BLOCK, HEAD_DIM).astype(o_ref.dtype)

    return pl.pallas_call(
        body, name=name, grid=(N_KV_HEADS, n_tok // BLOCK), in_specs=[q_spec, prev, cur, prev, cur, sink], out_specs=q_spec,
        out_shape=jax.ShapeDtypeStruct(q.shape, BF16), compiler_params=_params(("parallel", "parallel")),
    )(q, k, k, v, v, sink_rows)


def swa_bwd(q, k, v, sink_rows, do, name):
    n_tok = q.shape[1]
    nb = n_tok // BLOCK
    q_spec, prev, cur, sink = _swa_specs(nb)
    rows = Q_PER_KV * BLOCK

    def body(q_ref, kp_ref, kc_ref, vp_ref, vc_ref, s_ref, do_ref, dq_ref, dkp_ref, dkc_ref, dvp_ref, dvc_ref, ds_ref):
        n = pl.program_id(1)
        qq = q_ref[...].reshape(rows, HEAD_DIM)
        dd = do_ref[...].reshape(rows, HEAD_DIM)
        kk = jnp.concatenate([kp_ref[...], kc_ref[...]], axis=0)
        vv = jnp.concatenate([vp_ref[...], vc_ref[...]], axis=0)
        p, ps = _swa_probs(qq, kk, s_ref[...], n)
        dp = lax.dot_general(dd, vv, (((1,), (1,)), ((), ())), preferred_element_type=F32)
        delta = jnp.sum(p * dp, axis=-1, keepdims=True)
        ds = (p * (dp - delta) * (HEAD_DIM ** -0.5)).astype(BF16)
        dq = jnp.dot(ds, kk, preferred_element_type=F32)
        dk = lax.dot_general(ds, qq, (((0,), (0,)), ((), ())), preferred_element_type=F32)
        dv = lax.dot_general(p.astype(BF16), dd, (((0,), (0,)), ((), ())), preferred_element_type=F32)
        dq_ref[...] = dq.reshape(Q_PER_KV, BLOCK, HEAD_DIM).astype(dq_ref.dtype)
        dkp_ref[...] = dk[:BLOCK]
        dkc_ref[...] = dk[BLOCK:]
        dvp_ref[...] = dv[:BLOCK]
        dvc_ref[...] = dv[BLOCK:]
        dsink = jnp.broadcast_to(-ps * delta, (rows, LANES)).reshape(Q_PER_KV, BLOCK, LANES)
        ds_ref[...] = jnp.sum(dsink, axis=1)

    part = pl.BlockSpec((None, None, BLOCK, HEAD_DIM), lambda h, n: (h, n, 0, 0))
    part_shape = jax.ShapeDtypeStruct((N_KV_HEADS, nb, BLOCK, HEAD_DIM), F32)
    return pl.pallas_call(
        body, name=name, grid=(N_KV_HEADS, nb), in_specs=[q_spec, prev, cur, prev, cur, sink, q_spec],
        out_specs=[q_spec, part, part, part, part, pl.BlockSpec((None, None, Q_PER_KV, LANES), lambda h, n: (h, n, 0, 0))],
        out_shape=[jax.ShapeDtypeStruct(q.shape, BF16), part_shape, part_shape, part_shape, part_shape,
                   jax.ShapeDtypeStruct((N_KV_HEADS, nb, Q_PER_KV, LANES), F32)],
        compiler_params=_params(("parallel", "parallel")),
    )(q, k, k, v, v, sink_rows, do)


def _to_heads(t, n_heads):
    return t.reshape(t.shape[0], n_heads, HEAD_DIM).transpose(1, 0, 2)


def _from_heads(t):
    return t.transpose(1, 0, 2).reshape(t.shape[1], -1)


def _fold_kv_grad(prev, cur):
    shifted = jnp.concatenate([prev[:, 1:], jnp.zeros_like(prev[:, :1])], axis=1)
    tot = (cur + shifted).reshape(N_KV_HEADS, -1, HEAD_DIM)
    return _from_heads(tot)


def _x_probs(qh, kh):
    s = lax.dot_general(qh, kh, (((1,), (1,)), ((), ())), preferred_element_type=F32) * (X_HEAD_DIM ** -0.5)
    e = jnp.exp(s - jnp.max(s, axis=-1, keepdims=True))
    return e * (1.0 / jnp.sum(e, axis=-1, keepdims=True))


def xattn_fwd(q, k, v, name):
    n_tok, width = q.shape
    n_mem = k.shape[0]
    tq = _pick(n_tok, 512, 16)

    def body(q_ref, k_ref, v_ref, o_ref):
        for h in range(X_HEADS):
            cols = slice(h * X_HEAD_DIM, (h + 1) * X_HEAD_DIM)
            p = _x_probs(q_ref[:, cols], k_ref[:, cols])
            o_ref[:, cols] = jnp.dot(p.astype(BF16), v_ref[:, cols], preferred_element_type=F32).astype(o_ref.dtype)

    row = pl.BlockSpec((tq, width), lambda i: (i, 0))
    mem = pl.BlockSpec((n_mem, width), lambda i: (0, 0))
    return pl.pallas_call(
        body, name=name, grid=(n_tok // tq,), in_specs=[row, mem, mem], out_specs=row,
        out_shape=jax.ShapeDtypeStruct(q.shape, BF16), compiler_params=_params(("parallel",)),
    )(q, k, v)


def xattn_bwd(q, k, v, do, name):
    n_tok, width = q.shape
    n_mem = k.shape[0]
    tq = _pick(n_tok, 512, 16)

    def body(q_ref, k_ref, v_ref, do_ref, dq_ref, dk_ref, dv_ref):
        @pl.when(pl.program_id(0) == 0)
        def _():
            dk_ref[...] = jnp.zeros_like(dk_ref)
            dv_ref[...] = jnp.zeros_like(dv_ref)

        for h in range(X_HEADS):
            cols = slice(h * X_HEAD_DIM, (h + 1) * X_HEAD_DIM)
            qh, kh, vh, dh = q_ref[:, cols], k_ref[:, cols], v_ref[:, cols], do_ref[:, cols]
            p = _x_probs(qh, kh)
            dp = lax.dot_general(dh, vh, (((1,), (1,)), ((), ())), preferred_element_type=F32)
            delta = jnp.sum(p * dp, axis=-1, keepdims=True)
            ds = (p * (dp - delta) * (X_HEAD_DIM ** -0.5)).astype(BF16)
            dq_ref[:, cols] = jnp.dot(ds, kh, preferred_element_type=F32).astype(dq_ref.dtype)
            dk_ref[:, cols] += lax.dot_general(ds, qh, (((0,), (0,)), ((), ())), preferred_element_type=F32)
            dv_ref[:, cols] += lax.dot_general(p.astype(BF16), dh, (((0,), (0,)), ((), ())), preferred_element_type=F32)

    row = pl.BlockSpec((tq, width), lambda i: (i, 0))
    mem = pl.BlockSpec((n_mem, width), lambda i: (0, 0))
    return pl.pallas_call(
        body, name=name, grid=(n_tok // tq,), in_specs=[row, mem, mem, row], out_specs=[row, mem, mem],
        out_shape=[jax.ShapeDtypeStruct(q.shape, BF16), jax.ShapeDtypeStruct(k.shape, F32), jax.ShapeDtypeStruct(k.shape, F32)],
        compiler_params=_params(("arbitrary",)),
    )(q, k, v, do)


GELU_C = 0.7978845608028654
GELU_A = 0.044715


def _gelu_parts(x):
    x2 = x * x
    t = jnp.tanh(GELU_C * x * (1.0 + GELU_A * x2))
    y = 0.5 * x * (1.0 + t)
    dy = 0.5 * (1.0 + t) + 0.5 * x * (1.0 - t * t) * GELU_C * (1.0 + 3.0 * GELU_A * x2)
    return y, dy


def _sgu_norm(v, ln_g, ln_b):
    mu = jnp.mean(v, axis=-1, keepdims=True)
    vc = v - mu
    r = lax.rsqrt(jnp.mean(vc * vc, axis=-1, keepdims=True) + EPS)
    xhat = vc * r
    return xhat * ln_g + ln_b, xhat, r


def _causal_weights(w_ref):
    i = lax.broadcasted_iota(jnp.int32, (BLOCK, BLOCK), 0)
    j = lax.broadcasted_iota(jnp.int32, (BLOCK, BLOCK), 1)
    return [jnp.where(i >= j, w_ref[g], 0.0).astype(BF16) for g in range(SGU_GROUPS)]


def sgu_fwd(u_pre, v_pre, ln_g, ln_b, w_s, bias_rows, name):
    n_tok = u_pre.shape[0]
    tm = _pick(n_tok, 512, BLOCK)

    def body(u_ref, v_ref, g_ref, b_ref, w_ref, bb_ref, o_ref):
        vn, _, _ = _sgu_norm(_gelu_parts(v_ref[...])[0], g_ref[...], b_ref[...])
        vn = vn.astype(BF16)
        wc = _causal_weights(w_ref)
        for c in range(tm // BLOCK):
            rows = slice(c * BLOCK, (c + 1) * BLOCK)
            for g in range(SGU_GROUPS):
                cols = slice(g * LANES, (g + 1) * LANES)
                mixed = jnp.dot(wc[g], vn[rows, cols], preferred_element_type=F32) + bb_ref[g]
                u = _gelu_parts(u_ref[rows, cols])[0]
                o_ref[rows, cols] = (u * mixed).astype(o_ref.dtype)

    row = pl.BlockSpec((tm, SGU_WIDTH), lambda i: (i, 0))
    vec = pl.BlockSpec((1, SGU_WIDTH), lambda i: (0, 0))
    mat = pl.BlockSpec((SGU_GROUPS, BLOCK, LANES), lambda i: (0, 0, 0))
    return pl.pallas_call(
        body, name=name, grid=(n_tok // tm,), in_specs=[row, row, vec, vec, mat, mat], out_specs=row,
        out_shape=jax.ShapeDtypeStruct((n_tok, SGU_WIDTH), BF16), compiler_params=_params(("parallel",)),
    )(u_pre, v_pre, ln_g, ln_b, w_s, bias_rows)


def sgu_bwd(u_pre, v_pre, ln_g, ln_b, w_s, bias_rows, dgate, name):
    n_tok = u_pre.shape[0]
    tm = _pick(n_tok, 512, BLOCK)

    def body(u_ref, v_ref, g_ref, b_ref, w_ref, bb_ref, dg_ref, du_ref, dv_ref, dw_ref, db_ref, dlg_ref, dlb_ref, dvn_ref):
        @pl.when(pl.program_id(0) == 0)
        def _():
            dw_ref[...] = jnp.zeros_like(dw_ref)
            db_ref[...] = jnp.zeros_like(db_ref)
            dlg_ref[...] = jnp.zeros_like(dlg_ref)
            dlb_ref[...] = jnp.zeros_like(dlb_ref)

        gv, dgv = _gelu_parts(v_ref[...])
        vn, xhat, r = _sgu_norm(gv, g_ref[...], b_ref[...])
        vn = vn.astype(BF16)
        wc = _causal_weights(w_ref)
        for c in range(tm // BLOCK):
            rows = slice(c * BLOCK, (c + 1) * BLOCK)
            for g in range(SGU_GROUPS):
                cols = slice(g * LANES, (g + 1) * LANES)
                vt = vn[rows, cols]
                mixed = jnp.dot(wc[g], vt, preferred_element_type=F32) + bb_ref[g]
                u, du_dpre = _gelu_parts(u_ref[rows, cols])
                dgate_t = dg_ref[rows, cols].astype(F32)
                du_ref[rows, cols] = (dgate_t * mixed * du_dpre).astype(du_ref.dtype)
                dmix = dgate_t * u
                dmix_b = dmix.astype(BF16)
                db_ref[g] += dmix
                dw_ref[g] += lax.dot_general(dmix_b, vt, (((1,), (1,)), ((), ())), preferred_element_type=F32)
                dvn_ref[rows, cols] = lax.dot_general(wc[g], dmix_b, (((0,), (0,)), ((), ())), preferred_element_type=F32)
        dvn = dvn_ref[...]
        dlg_ref[...] += _rsum8(dvn * xhat)
        dlb_ref[...] += _rsum8(dvn)
        dxhat = dvn * g_ref[...]
        dgv_in = r * (dxhat - jnp.mean(dxhat, axis=-1, keepdims=True) - xhat * jnp.mean(dxhat * xhat, axis=-1, keepdims=True))
        dv_ref[...] = (dgv_in * dgv).astype(dv_ref.dtype)

    row = pl.BlockSpec((tm, SGU_WIDTH), lambda i: (i, 0))
    vec = pl.BlockSpec((1, SGU_WIDTH), lambda i: (0, 0))
    mat = pl.BlockSpec((SGU_GROUPS, BLOCK, LANES), lambda i: (0, 0, 0))
    part = pl.BlockSpec((8, SGU_WIDTH), lambda i: (0, 0))
    mat_shape = jax.ShapeDtypeStruct((SGU_GROUPS, BLOCK, LANES), F32)
    part_shape = jax.ShapeDtypeStruct((8, SGU_WIDTH), F32)
    act_shape = jax.ShapeDtypeStruct((n_tok, SGU_WIDTH), BF16)
    return pl.pallas_call(
        body, name=name, grid=(n_tok // tm,), in_specs=[row, row, vec, vec, mat, mat, row],
        out_specs=[row, row, mat, mat, part, part], out_shape=[act_shape, act_shape, mat_shape, mat_shape, part_shape, part_shape],
        scratch_shapes=[pltpu.VMEM((tm, SGU_WIDTH), F32)], compiler_params=_params(("arbitrary",)),
    )(u_pre, v_pre, ln_g, ln_b, w_s, bias_rows, dgate)


def _pool_tile(n_tok):
    return _pick(n_tok, 256, POOL_HALO)


def pool_fwd(h, name):
    n_tok, width = h.shape
    gw = width // len(POOL_WINDOWS)
    tm = _pool_tile(n_tok)
    per = tm // POOL_HALO

    def body(cur_ref, halo_ref, o_ref, buf_ref):
        i = pl.program_id(0)
        buf_ref[0:POOL_HALO, :] = jnp.where(i > 0, halo_ref[...], 0.0)
        buf_ref[POOL_HALO:, :] = cur_ref[...]
        tok = i * tm + lax.broadcasted_iota(jnp.int32, (tm, 1), 0)
        for g, w in enumerate(POOL_WINDOWS):
            cols = slice(g * gw, (g + 1) * gw)
            acc = buf_ref[POOL_HALO:, cols]
            for j in range(1, w):
                acc = acc + buf_ref[POOL_HALO - j:POOL_HALO - j + tm, cols]
            cnt = jnp.minimum(tok + 1, w).astype(F32)
            o_ref[:, cols] = (acc / cnt - cur_ref[:, cols]).astype(o_ref.dtype)

    return pl.pallas_call(
        body, name=name, grid=(n_tok // tm,),
        in_specs=[pl.BlockSpec((tm, width), lambda i: (i, 0)),
                  pl.BlockSpec((POOL_HALO, width), lambda i: (jnp.maximum(i * per - 1, 0), 0))],
        out_specs=pl.BlockSpec((tm, width), lambda i: (i, 0)), out_shape=jax.ShapeDtypeStruct(h.shape, BF16),
        scratch_shapes=[pltpu.VMEM((tm + POOL_HALO, width), F32)], compiler_params=_params(("parallel",)),
    )(h, h)


def pool_bwd(dp, name):
    n_tok, width = dp.shape
    gw = width // len(POOL_WINDOWS)
    tm = _pool_tile(n_tok)
    per = tm // POOL_HALO
    n_steps = n_tok // tm

    def body(cur_ref, halo_ref, o_ref, buf_ref):
        i = pl.program_id(0)
        tok = i * tm + lax.broadcasted_iota(jnp.int32, (tm, 1), 0)
        for g, w in enumerate(POOL_WINDOWS):
            cols = slice(g * gw, (g + 1) * gw)
            cnt = jnp.minimum(tok + 1, w).astype(F32)
            buf_ref[0:tm, cols] = cur_ref[:, cols] / cnt
            buf_ref[tm:, cols] = jnp.where(i < n_steps - 1, halo_ref[:, cols] / float(w), 0.0)
        for g, w in enumerate(POOL_WINDOWS):
            cols = slice(g * gw, (g + 1) * gw)
            acc = buf_ref[0:tm, cols]
            for j in range(1, w):
                acc = acc + buf_ref[j:j + tm, cols]
            o_ref[:, cols] = acc - cur_ref[:, cols]

    return pl.pallas_call(
        body, name=name, grid=(n_steps,),
        in_specs=[pl.BlockSpec((tm, width), lambda i: (i, 0)),
                  pl.BlockSpec((POOL_HALO, width), lambda i: (jnp.minimum((i + 1) * per, n_tok // POOL_HALO - 1), 0))],
        out_specs=pl.BlockSpec((tm, width), lambda i: (i, 0)), out_shape=jax.ShapeDtypeStruct(dp.shape, F32),
        scratch_shapes=[pltpu.VMEM((tm + POOL_HALO, width), F32)], compiler_params=_params(("parallel",)),
    )(dp, dp)


def _ffn_fwd(x, ga, gb, wg, wu, wd, tag):
    h = rms_fwd(x, ga, BF16, f"{tag}_prenorm")
    G = _mm(h, wg, batch="map", out_dtype=BF16, name=f"{tag}_gate")
    U = _mm(h, wu, batch="map", out_dtype=BF16, name=f"{tag}_up")
    A = swiglu_fwd(G, U, f"{tag}_swiglu")
    y = _mm(A, wd, batch="reduce", name=f"{tag}_down")
    return postnorm_res(x, y, gb, 0.5, f"{tag}_postnorm"), (x, h, G, U, A, y)


def _ffn_bwd(res, ga, gb, wg, wu, wd, dx2, tag):
    x, h, G, U, A, y = res
    dy, dgb = rms_bwd(y, gb, [dx2], 0.5, None, BF16, f"{tag}_postnorm_bwd")
    dA = _mm(dy, wd, tb=True, batch="map", out_dtype=BF16, name=f"{tag}_down_dx")
    dwd = _mm(A, dy, ta=True, batch="map", name=f"{tag}_down_dw")
    dG, dU = swiglu_bwd(G, U, dA, f"{tag}_swiglu_bwd")
    dwg = _mm(h, dG, ta=True, batch="map", name=f"{tag}_gate_dw")
    dwu = _mm(h, dU, ta=True, batch="map", name=f"{tag}_up_dw")
    dh1 = _mm(dG, wg, tb=True, batch="reduce", name=f"{tag}_gate_dx")
    dh2 = _mm(dU, wu, tb=True, batch="reduce", name=f"{tag}_up_dx")
    dx, dga = rms_bwd(x, ga, [dh1, dh2], 1.0, dx2, F32, f"{tag}_prenorm_bwd")
    return dx, dga, dgb, dwg, dwu, dwd


def _sink_rows(sinks):
    return jnp.repeat(sinks.reshape(N_KV_HEADS, Q_PER_KV), BLOCK, axis=1)[..., None]


def _attn_sgu_fwd(x, g_pre, g_post, w_in, w_out, sinks, ln_g, ln_b, sgu_w, bias_rows, tabs, tag):
    h = rms_fwd(x, g_pre, BF16, f"{tag}_prenorm")
    z = _mm(h, w_in, name=f"{tag}_in")
    qk = rope_apply(z, tabs, QK_WIDTH // LANES, False, BF16, f"{tag}_rope")
    q = _to_heads(qk[:, :ATTN_WIDTH], N_Q_HEADS)
    k = _to_heads(qk[:, ATTN_WIDTH:], N_KV_HEADS)
    v = _to_heads(z[:, QK_WIDTH:QK_WIDTH + KV_WIDTH].astype(BF16), N_KV_HEADS)
    o = swa_fwd(q, k, v, _sink_rows(sinks), f"{tag}_swa")
    u_pre = z[:, QK_WIDTH + KV_WIDTH:QK_WIDTH + KV_WIDTH + SGU_WIDTH]
    v_pre = z[:, QK_WIDTH + KV_WIDTH + SGU_WIDTH:]
    gate = sgu_fwd(u_pre, v_pre, ln_g, ln_b, sgu_w, bias_rows, f"{tag}_sgu")
    cat = jnp.concatenate([_from_heads(o), gate], axis=1)
    m = _mm(cat, w_out, name=f"{tag}_out")
    return postnorm_res(x, m, g_post, 1.0, f"{tag}_postnorm"), (x, h, q, k, v, u_pre, v_pre, cat, m)


def _attn_sgu_bwd(res, g_pre, g_post, w_in, w_out, sinks, ln_g, ln_b, sgu_w, bias_rows, tabs, dx2, tag):
    x, h, q, k, v, u_pre, v_pre, cat, m = res
    dm, dg_post = rms_bwd(m, g_post, [dx2], 1.0, None, BF16, f"{tag}_postnorm_bwd")
    dcat = _mm(dm, w_out, tb=True, out_dtype=BF16, name=f"{tag}_out_dx")
    dw_out = _mm(cat, dm, ta=True, name=f"{tag}_out_dw")
    do = _to_heads(dcat[:, :ATTN_WIDTH], N_Q_HEADS)
    dq, dkp, dkc, dvp, dvc, dsink = swa_bwd(q, k, v, _sink_rows(sinks), do, f"{tag}_swa_bwd")
    d_sinks = jnp.sum(dsink[..., 0], axis=1).reshape(1, N_Q_HEADS)
    dqk_rot = jnp.concatenate([_from_heads(dq).astype(F32), _fold_kv_grad(dkp, dkc)], axis=1)
    dqk = rope_apply(dqk_rot, tabs, QK_WIDTH // LANES, True, BF16, f"{tag}_rope_bwd")
    dv = _fold_kv_grad(dvp, dvc).astype(BF16)
    du_pre, dv_pre, dw_s, dbias, dlg, dlb = sgu_bwd(u_pre, v_pre, ln_g, ln_b, sgu_w, bias_rows, dcat[:, ATTN_WIDTH:], f"{tag}_sgu_bwd")
    dz = jnp.concatenate([dqk, dv, du_pre, dv_pre], axis=1)
    dw_in = _mm(h, dz, ta=True, name=f"{tag}_in_dw")
    dh = _mm(dz, w_in, tb=True, name=f"{tag}_in_dx")
    dx, dg_pre = rms_bwd(x, g_pre, [dh], 1.0, dx2, F32, f"{tag}_prenorm_bwd")
    causal = jnp.tril(jnp.ones((BLOCK, BLOCK), F32))
    small = dict(attn_sinks=d_sinks, sgu_ln_g=jnp.sum(dlg, axis=0, keepdims=True), sgu_ln_b=jnp.sum(dlb, axis=0, keepdims=True),
                 sgu_w=(dw_s * causal[None])[None], sgu_b=jnp.sum(dbias, axis=-1)[None])
    return dx, dg_pre, dg_post, dw_in, dw_out, small


def _pool_mix_fwd(x, g_pre, g_post, pool_w, pool_scale, tag):
    hf = rms_fwd(x, g_pre, F32, f"{tag}_prenorm")
    pooled = pool_fwd(hf, f"{tag}_pool")
    n_g = len(POOL_WINDOWS)
    ypre = _mm(pooled, pool_w, batch="map", groups=n_g, a_cb=True, o_cb=True, name=f"{tag}_proj")
    m = scale_cols(ypre, pool_scale, f"{tag}_scale")
    return postnorm_res(x, m, g_post, 1.0, f"{tag}_postnorm"), (x, pooled, ypre, m)


def _pool_mix_bwd(res, g_pre, g_post, pool_w, pool_scale, dx2, tag):
    x, pooled, ypre, m = res
    n_g = len(POOL_WINDOWS)
    dm, dg_post = rms_bwd(m, g_post, [dx2], 1.0, None, F32, f"{tag}_postnorm_bwd")
    dypre, dscale = scale_cols_bwd(dm, ypre, pool_scale, f"{tag}_scale_bwd")
    dpooled = _mm(dypre, pool_w, tb=True, batch="map", groups=n_g, a_cb=True, o_cb=True, name=f"{tag}_proj_dx")
    dpw = _mm(pooled, dypre, ta=True, batch="map", groups=n_g, a_cb=True, b_cb=True, name=f"{tag}_proj_dw")
    dhf = pool_bwd(dpooled, f"{tag}_pool_bwd")
    dx, dg_pre = rms_bwd(x, g_pre, [dhf], 1.0, dx2, F32, f"{tag}_prenorm_bwd")
    return dx, dg_pre, dg_post, dpw, jnp.sum(dscale, axis=0, keepdims=True)


def _xattn_fwd(x, mem, g_pre, g_post, g_mem, wq, wk, wv, wo, tag):
    h = rms_fwd(x, g_pre, BF16, f"{tag}_prenorm")
    mem_n = rms_fwd(mem, g_mem, BF16, f"{tag}_memnorm")
    q = _mm(h, wq, out_dtype=BF16, name=f"{tag}_q")
    k = _mm(mem_n, wk, out_dtype=BF16, name=f"{tag}_k")
    v = _mm(mem_n, wv, out_dtype=BF16, name=f"{tag}_v")
    o = xattn_fwd(q, k, v, f"{tag}_core")
    r = _mm(o, wo, name=f"{tag}_o")
    return postnorm_res(x, r, g_post, 1.0, f"{tag}_postnorm"), (x, h, mem_n, q, k, v, o, r)


def _xattn_bwd(res, mem, g_pre, g_post, g_mem, wq, wk, wv, wo, dx2, tag):
    x, h, mem_n, q, k, v, o, r = res
    dr, dg_post = rms_bwd(r, g_post, [dx2], 1.0, None, BF16, f"{tag}_postnorm_bwd")
    do = _mm(dr, wo, tb=True, out_dtype=BF16, name=f"{tag}_o_dx")
    dwo = _mm(o, dr, ta=True, name=f"{tag}_o_dw")
    dq, dk, dv = xattn_bwd(q, k, v, do, f"{tag}_core_bwd")
    dk, dv = dk.astype(BF16), dv.astype(BF16)
    dwq = _mm(h, dq, ta=True, name=f"{tag}_q_dw")
    dwk = _mm(mem_n, dk, ta=True, name=f"{tag}_k_dw")
    dwv = _mm(mem_n, dv, ta=True, name=f"{tag}_v_dw")
    dh = _mm(dq, wq, tb=True, name=f"{tag}_q_dx")
    dmem1 = _mm(dk, wk, tb=True, name=f"{tag}_k_dx")
    dmem2 = _mm(dv, wv, tb=True, name=f"{tag}_v_dx")
    _, dg_mem = rms_bwd(mem, g_mem, [dmem1, dmem2], 1.0, None, BF16, f"{tag}_memnorm_bwd")
    dx, dg_pre = rms_bwd(x, g_pre, [dh], 1.0, dx2, F32, f"{tag}_prenorm_bwd")
    return dx, dg_pre, dg_post, dg_mem, dwq, dwk, dwv, dwo


def _rowsum8(part):
    return jnp.sum(part, axis=0, keepdims=True)


def device_step(x, mem, target, norms, mem_norm, big, small):
    n_tok = x.shape[0]
    tabs = rope_tables(n_tok)
    bias_rows = jnp.broadcast_to(small["sgu_b"][:, :, None], (SGU_GROUPS, BLOCK, LANES))
    gn = lambda l, i: norms[l, i][None, :]
    gm = lambda l: mem_norm[l][None, :]
    ffn1 = lambda l: (big["ffn1_wg"][l], big["ffn1_wu"][l], big["ffn1_wd"][l])
    ffn2 = lambda l: (big["ffn2_wg"][l], big["ffn2_wu"][l], big["ffn2_wd"][l])
    xw = lambda l: (big["x_wq"][l], big["x_wk"][l], big["x_wv"][l], big["x_wo"][l])
    mix0 = (big["mix_w_in"][0], big["mix_w_out"][0], small["attn_sinks"], small["sgu_ln_g"], small["sgu_ln_b"], small["sgu_w"], bias_rows, tabs)
    mix1 = (big["pool_w"][0], small["pool_scale"])

    saved = []
    for l in range(2):
        x, r1 = _ffn_fwd(x, gn(l, 0), gn(l, 1), *ffn1(l), f"l{l}_ffn1")
        if l == 0:
            x, r2 = _attn_sgu_fwd(x, gn(l, 2), gn(l, 3), *mix0, f"l{l}_mix")
        else:
            x, r2 = _pool_mix_fwd(x, gn(l, 2), gn(l, 3), *mix1, f"l{l}_mix")
        x, r3 = _xattn_fwd(x, mem, gn(l, 4), gn(l, 5), gm(l), *xw(l), f"l{l}_xattn")
        x, r4 = _ffn_fwd(x, gn(l, 6), gn(l, 7), *ffn2(l), f"l{l}_ffn2")
        saved.append((r1, r2, r3, r4))

    loss_part, dx = loss_and_grad(x, target, "loss")

    gb = {k: [None, None] for k in ("ffn1_wg", "ffn1_wu", "ffn1_wd", "ffn2_wg", "ffn2_wu", "ffn2_wd", "x_wq", "x_wk", "x_wv", "x_wo")}
    g_norm_rows = [[None] * 8, [None] * 8]
    g_mem_rows = [None, None]
    g_small = {}
    for l in (1, 0):
        r1, r2, r3, r4 = saved[l]
        dx, g_norm_rows[l][6], g_norm_rows[l][7], gb["ffn2_wg"][l], gb["ffn2_wu"][l], gb["ffn2_wd"][l] = _ffn_bwd(
            r4, gn(l, 6), gn(l, 7), *ffn2(l), dx, f"l{l}_ffn2")
        dx, g_norm_rows[l][4], g_norm_rows[l][5], g_mem_rows[l], gb["x_wq"][l], gb["x_wk"][l], gb["x_wv"][l], gb["x_wo"][l] = _xattn_bwd(
            r3, mem, gn(l, 4), gn(l, 5), gm(l), *xw(l), dx, f"l{l}_xattn")
        if l == 0:
            dx, g_norm_rows[l][2], g_norm_rows[l][3], dw_in, dw_out, sm = _attn_sgu_bwd(r2, gn(l, 2), gn(l, 3), *mix0, dx, f"l{l}_mix")
            gb["mix_w_in"], gb["mix_w_out"] = [dw_in], [dw_out]
            g_small.update(sm)
        else:
            dx, g_norm_rows[l][2], g_norm_rows[l][3], dpw, dscale = _pool_mix_bwd(r2, gn(l, 2), gn(l, 3), *mix1, dx, f"l{l}_mix")
            gb["pool_w"] = [dpw]
            g_small["pool_scale"] = dscale
        dx, g_norm_rows[l][0], g_norm_rows[l][1], gb["ffn1_wg"][l], gb["ffn1_wu"][l], gb["ffn1_wd"][l] = _ffn_bwd(
            r1, gn(l, 0), gn(l, 1), *ffn1(l), dx, f"l{l}_ffn1")
    g_norms = jnp.stack([jnp.concatenate([_rowsum8(p) for p in g_norm_rows[l]], axis=0) for l in range(2)])
    g_mem_norm = jnp.concatenate([_rowsum8(p) for p in g_mem_rows], axis=0)
    return loss_part, dx, gb, g_small, g_norms, g_mem_norm


ANY = pl.BlockSpec(memory_space=pl.ANY)


def _place():
    x, y, c = lax.axis_index("x"), lax.axis_index("y"), lax.axis_index("c")
    other_chips = [(1 - x, y), (x, 1 - y), (1 - x, 1 - y)]
    return x, y, c, other_chips


def _half_rows(core, n_rows):
    half = n_rows // 2
    return pl.ds(pl.multiple_of(core * half, 16), half)


def _remote(src, dst, send_sem, recv_sem, device):
    return pltpu.make_async_remote_copy(src_ref=src, dst_ref=dst, send_sem=send_sem, recv_sem=recv_sem,
                                        device_id=device, device_id_type=MESH)


def gather_shards(slots, name):
    n = len(slots)

    def body(*refs):
        out = refs[n:2 * n]
        send_sems, recv_sems = refs[2 * n:]
        x, y, c, chips = _place()
        me, sibling = 2 * x + y, (x, y, 1 - c)
        sends = []
        for i in range(n):
            mine = out[i].at[me, _half_rows(c, out[i].shape[1])]
            for r, (px, py) in enumerate(chips):
                sends.append(_remote(mine, mine, send_sems.at[6 * i + r], recv_sems.at[6 * i + r], (px, py, c)))
                sends[-1].start()
        for i in range(n):
            for r, (px, py) in enumerate(chips):
                landed = out[i].at[2 * px + py, _half_rows(c, out[i].shape[1])]
                _remote(landed, landed, send_sems.at[6 * i + r], recv_sems.at[6 * i + r], (px, py, c)).wait_recv()
                sends.append(_remote(landed, landed, send_sems.at[6 * i + 3 + r], recv_sems.at[6 * i + 3 + r], sibling))
                sends[-1].start()
        for i in range(n):
            for r, (px, py) in enumerate(chips):
                landed = out[i].at[2 * px + py, _half_rows(1 - c, out[i].shape[1])]
                _remote(landed, landed, send_sems.at[6 * i + 3 + r], recv_sems.at[6 * i + 3 + r], sibling).wait_recv()
        for cp in sends:
            cp.wait_send()

    return pl.pallas_call(
        body, name=name, in_specs=[ANY] * n, out_specs=[ANY] * n, input_output_aliases={i: i for i in range(n)},
        out_shape=[jax.ShapeDtypeStruct(s.shape, s.dtype) for s in slots],
        scratch_shapes=[pltpu.SemaphoreType.DMA((6 * n,)), pltpu.SemaphoreType.DMA((6 * n,))],
    )(*slots)


def swap_other_halves(grads, name):
    n = len(grads)

    def body(*refs):
        src, out = refs[:n], refs[n:2 * n]
        send_sems, recv_sems = refs[2 * n:]
        x, y, c, _ = _place()
        copies = [_remote(src[i].at[:, _half_rows(1 - c, src[i].shape[1])], out[i], send_sems.at[i], recv_sems.at[i], (x, y, 1 - c))
                  for i in range(n)]
        for cp in copies:
            cp.start()
        for cp in copies:
            cp.wait()

    return pl.pallas_call(
        body, name=name, in_specs=[ANY] * n, out_specs=[ANY] * n,
        out_shape=[jax.ShapeDtypeStruct((g.shape[0], g.shape[1] // 2, g.shape[2]), g.dtype) for g in grads],
        scratch_shapes=[pltpu.SemaphoreType.DMA((n,)), pltpu.SemaphoreType.DMA((n,))],
    )(*grads)


def add_own_half(g, p, core, name):
    n_j, n_rows, n_cols = g.shape
    half = n_rows // 2
    tr = _rows(half, n_cols * 10)

    def body(c_ref, g_ref, p_ref, o_ref):
        o_ref[...] = (g_ref[...] + p_ref[...]).astype(o_ref.dtype)

    grid_spec = pltpu.PrefetchScalarGridSpec(
        num_scalar_prefetch=1, grid=(n_j, half // tr),
        in_specs=[pl.BlockSpec((None, None, tr, n_cols), lambda j, i, c_ref: (j, c_ref[0], i, 0)),
                  pl.BlockSpec((None, tr, n_cols), lambda j, i, c_ref: (j, i, 0))],
        out_specs=pl.BlockSpec((None, tr, n_cols), lambda j, i, c_ref: (j, i, 0)))
    return pl.pallas_call(
        body, name=name, grid_spec=grid_spec, out_shape=jax.ShapeDtypeStruct((n_j, half, n_cols), BF16),
        compiler_params=_params(("parallel", "parallel")),
    )(core, g.reshape(n_j, 2, half, n_cols), p)


def scatter_to_chips(parts, name):
    n = len(parts)

    def body(*refs):
        src, out = refs[:n], refs[n:2 * n]
        send_sems, recv_sems = refs[2 * n:]
        x, y, c, chips = _place()
        me = 2 * x + y
        sends = []
        for i in range(n):
            for r, (px, py) in enumerate(chips):
                sends.append(_remote(src[i].at[2 * px + py], out[i].at[me], send_sems.at[3 * i + r], recv_sems.at[3 * i + r], (px, py, c)))
                sends[-1].start()
        for i in range(n):
            for r, (px, py) in enumerate(chips):
                landed = out[i].at[2 * px + py]
                _remote(landed, landed, send_sems.at[3 * i + r], recv_sems.at[3 * i + r], (px, py, c)).wait_recv()
        for cp in sends:
            cp.wait_send()

    return pl.pallas_call(
        body, name=name, in_specs=[ANY] * n, out_specs=[ANY] * n,
        out_shape=[jax.ShapeDtypeStruct(p.shape, p.dtype) for p in parts],
        scratch_shapes=[pltpu.SemaphoreType.DMA((3 * n,)), pltpu.SemaphoreType.DMA((3 * n,))],
    )(*parts)


def sum_over_chips(own, got, chip, name):
    n_s, n_rows, n_cols = got.shape
    tr = _rows(n_rows, n_cols * (got.dtype.itemsize * (n_s + 1) + 4))

    def body(chip_ref, own_ref, *refs):
        got_refs, o_ref = refs[:n_s], refs[n_s]
        me = chip_ref[0]
        acc = jnp.where(me == 0, own_ref[...], got_refs[0][...]).astype(F32)
        for k in range(1, n_s):
            acc = acc + jnp.where(me == k, own_ref[...], got_refs[k][...]).astype(F32)
        o_ref[...] = acc

    def slot(k):
        return pl.BlockSpec((None, tr, n_cols), lambda i, chip_ref: (jnp.where(chip_ref[0] == k, (k + 1) % n_s, k), i, 0))

    grid_spec = pltpu.PrefetchScalarGridSpec(
        num_scalar_prefetch=1, grid=(n_rows // tr,),
        in_specs=[pl.BlockSpec((None, tr, n_cols), lambda i, chip_ref: (chip_ref[0], i, 0))] + [slot(k) for k in range(n_s)],
        out_specs=pl.BlockSpec((tr, n_cols), lambda i, chip_ref: (i, 0)))
    return pl.pallas_call(
        body, name=name, grid_spec=grid_spec, out_shape=jax.ShapeDtypeStruct((n_rows, n_cols), F32),
        compiler_params=_params(("parallel",)),
    )(chip, own, *([got] * n_s))


def swap_with_sibling(arrays, name):
    n = len(arrays)

    def body(*refs):
        src, out = refs[:n], refs[n:2 * n]
        send_sems, recv_sems = refs[2 * n:]
        x, y, c, _ = _place()
        copies = [_remote(src[i], out[i], send_sems.at[i], recv_sems.at[i], (x, y, 1 - c)) for i in range(n)]
        for cp in copies:
            cp.start()
        for cp in copies:
            cp.wait()

    return pl.pallas_call(
        body, name=name, in_specs=[ANY] * n, out_specs=[ANY] * n, out_shape=[jax.ShapeDtypeStruct(a.shape, a.dtype) for a in arrays],
        scratch_shapes=[pltpu.SemaphoreType.DMA((n,)), pltpu.SemaphoreType.DMA((n,))],
    )(*arrays)


def sum_slots(q, name):
    n_s, n_rows, n_cols = q.shape
    tr = _rows(n_rows, n_cols * (q.dtype.itemsize * n_s + 4))

    def body(q_ref, o_ref):
        acc = q_ref[0].astype(F32)
        for s in range(1, n_s):
            acc = acc + q_ref[s].astype(F32)
        o_ref[...] = acc

    return pl.pallas_call(
        body, name=name, grid=(n_rows // tr,), in_specs=[pl.BlockSpec((n_s, tr, n_cols), lambda i: (0, i, 0))],
        out_specs=pl.BlockSpec((tr, n_cols), lambda i: (i, 0)), out_shape=jax.ShapeDtypeStruct((n_rows, n_cols), F32),
        compiler_params=_params(("parallel",)),
    )(q)


def gather_all_devices(s, name):
    def body(s_ref, o_ref, send_sems, recv_sems, local_sem):
        x, y, c, _ = _place()
        me = 4 * x + 2 * y + c
        local = pltpu.make_async_copy(s_ref, o_ref.at[me], local_sem)
        local.start()
        sends = []
        for f in range(1, N_DEV):
            px, py, pc = x ^ (f >> 2), y ^ ((f >> 1) & 1), c ^ (f & 1)
            sends.append(_remote(s_ref, o_ref.at[me], send_sems.at[f - 1], recv_sems.at[f - 1], (px, py, pc)))
            sends[-1].start()
        for f in range(1, N_DEV):
            px, py, pc = x ^ (f >> 2), y ^ ((f >> 1) & 1), c ^ (f & 1)
            landed = o_ref.at[4 * px + 2 * py + pc]
            _remote(landed, landed, send_sems.at[f - 1], recv_sems.at[f - 1], (px, py, pc)).wait_recv()
        for cp in sends:
            cp.wait_send()
        local.wait()

    return pl.pallas_call(
        body, name=name, in_specs=[ANY], out_specs=ANY, out_shape=jax.ShapeDtypeStruct((N_DEV,) + s.shape, s.dtype),
        scratch_shapes=[pltpu.SemaphoreType.DMA((N_DEV - 1,)), pltpu.SemaphoreType.DMA((N_DEV - 1,)), pltpu.SemaphoreType.DMA],
    )(s)


FFN_NAMES = ("ffn1_wg", "ffn1_wu", "ffn1_wd", "ffn2_wg", "ffn2_wu", "ffn2_wd")
XATTN_NAMES = ("x_wq", "x_wk", "x_wv", "x_wo")
BIG_NAMES = FFN_NAMES + XATTN_NAMES + ("mix_w_in", "mix_w_out", "pool_w")
SMALL_NAMES = ("norms", "mem_norm", "attn_sinks", "sgu_ln_g", "sgu_ln_b", "sgu_w", "sgu_b", "pool_scale")
WEIGHT_ORDER = ("norms", "mem_norm") + BIG_NAMES[:-1] + ("attn_sinks", "sgu_ln_g", "sgu_ln_b", "sgu_w", "sgu_b", "pool_w", "pool_scale")
COLUMN_CUT = ("x_wo", "mix_w_in")
N_POOL = len(POOL_WINDOWS)


def _to_matmul_layout(name, g):
    n_j, n_rows, n_cols = g.shape
    if name in COLUMN_CUT:
        return g.transpose(1, 0, 2).reshape(n_rows, n_j * n_cols)
    if name == "pool_w":
        return g.reshape(n_j, N_POOL, n_rows // N_POOL, n_cols).transpose(1, 0, 2, 3).reshape(N_POOL, n_j * n_rows // N_POOL, n_cols)
    if name in ("x_wq", "x_wk", "x_wv", "mix_w_out"):
        return g.reshape(n_j * n_rows, n_cols)
    return g


def _from_matmul_layout(name, d):
    if name in COLUMN_CUT:
        n_rows, wide = d.shape
        return d.reshape(n_rows, N_CHIPS, wide // N_CHIPS).transpose(1, 0, 2)
    if name == "pool_w":
        n_g, n_in, n_cols = d.shape
        return d.reshape(n_g, N_CHIPS, n_in // N_CHIPS, n_cols).transpose(1, 0, 2, 3).reshape(N_CHIPS, n_g * n_in // N_CHIPS, n_cols)
    if name in ("x_wq", "x_wk", "x_wv", "mix_w_out"):
        return d.reshape(N_CHIPS, d.shape[0] // N_CHIPS, d.shape[1])
    return d


def _as3(w):
    return w.reshape(w.shape[0], -1, w.shape[-1])


def _pack(arrays, row_multiple):
    flat = jnp.concatenate([a.reshape(-1) for a in arrays])
    per = LANES * row_multiple
    total = -(-flat.shape[0] // per) * per
    return jnp.pad(flat, (0, total - flat.shape[0])).reshape(total // LANES, LANES)


def _unpack(packed, like):
    flat, out, at = packed.reshape(-1), [], 0
    for a in like:
        out.append(flat[at:at + a.size].reshape(a.shape))
        at += a.size
    return out


def kernel(x, mem, norms, mem_norm, ffn1_wg, ffn1_wu, ffn1_wd, ffn2_wg, ffn2_wu, ffn2_wd, x_wq, x_wk, x_wv, x_wo, mix_w_in, mix_w_out, attn_sinks, sgu_ln_g, sgu_ln_b, sgu_w, sgu_b, pool_w, pool_scale, loss_target, m_norms, m_mem_norm, m_ffn1_wg, m_ffn1_wu, m_ffn1_wd, m_ffn2_wg, m_ffn2_wu, m_ffn2_wd, m_x_wq, m_x_wk, m_x_wv, m_x_wo, m_mix_w_in, m_mix_w_out, m_attn_sinks, m_sgu_ln_g, m_sgu_ln_b, m_sgu_w, m_sgu_b, m_pool_w, m_pool_scale, v_norms, v_mem_norm, v_ffn1_wg, v_ffn1_wu, v_ffn1_wd, v_ffn2_wg, v_ffn2_wu, v_ffn2_wd, v_x_wq, v_x_wk, v_x_wv, v_x_wo, v_mix_w_in, v_mix_w_out, v_attn_sinks, v_sgu_ln_g, v_sgu_ln_b, v_sgu_w, v_sgu_b, v_pool_w, v_pool_scale):
    given = dict(locals())
    w = {n: given[n] for n in WEIGHT_ORDER}
    mom = {n: given["m_" + n] for n in WEIGHT_ORDER}
    var = {n: given["v_" + n] for n in WEIGHT_ORDER}
    chip_id = 2 * lax.axis_index("x") + lax.axis_index("y")
    chip = chip_id.astype(jnp.int32).reshape(1)
    core = lax.axis_index("c").astype(jnp.int32).reshape(1)
    n_shard = norms.shape[-1]

    slots, keys = [], []
    for name in BIG_NAMES:
        w3 = _as3(w[name])
        for l in range(w3.shape[0]):
            slots.append(cast_into_slot(w3, l, chip, f"cast_{name}"))
            keys.append((name, l))
    small_rows = jnp.concatenate([norms.reshape(-1, n_shard), pool_scale, jnp.zeros((15, n_shard), F32)], axis=0)
    small_slot = lax.dynamic_update_slice_in_dim(jnp.zeros((N_CHIPS,) + small_rows.shape, F32), small_rows[None], chip_id, axis=0)
    gathered = gather_shards(slots + [small_slot], "gather_weights")
    big = {name: [] for name in BIG_NAMES}
    for (name, l), g in zip(keys, gathered[:-1]):
        big[name].append(_to_matmul_layout(name, g))
    small_all = gathered[-1]
    n_norm_rows = norms.shape[0] * norms.shape[1]
    norms_all = jnp.concatenate([small_all[j, :n_norm_rows].reshape(norms.shape) for j in range(N_CHIPS)], axis=-1)
    pool_scale_all = jnp.concatenate([small_all[j, n_norm_rows:n_norm_rows + 1] for j in range(N_CHIPS)], axis=-1)
    small = dict(attn_sinks=attn_sinks[0], sgu_ln_g=sgu_ln_g, sgu_ln_b=sgu_ln_b, sgu_w=sgu_w[0], sgu_b=sgu_b[0], pool_scale=pool_scale_all)

    loss_part, dx, g_big, g_small, g_norms, g_mem_norm = device_step(x[0], mem[0], loss_target[0], norms_all, mem_norm, big, small)
    loss = lax.psum(0.5 * jnp.sum(loss_part) / x.shape[-1], ("x", "y", "c"))

    full = [_from_matmul_layout(name, g_big[name][l]) for name, l in keys]
    from_sibling = swap_other_halves(full, "grads_to_sibling")
    chip_sums = [add_own_half(g, p, core, f"chip_sum_{name}") for (name, l), g, p in zip(keys, full, from_sibling)]
    per_chip = scatter_to_chips(chip_sums, "grads_to_chips")
    own = [sum_over_chips(t, q, chip, f"sum_{name}") for (name, l), t, q in zip(keys, chip_sums, per_chip)]
    recv = swap_with_sibling(own, "grads_halves_to_sibling")
    own_of, recv_of = {name: [] for name in BIG_NAMES}, {name: [] for name in BIG_NAMES}
    for (name, l), o, r in zip(keys, own, recv):
        own_of[name].append(o)
        recv_of[name].append(r)
    grads, delta, new_m, new_v = {}, {}, {}, {}
    for name in BIG_NAMES:
        res = adamw_from_halves(_as3(w[name]), own_of[name], recv_of[name], _as3(mom[name]), _as3(var[name]), core, f"adamw_{name}")
        grads[name], delta[name], new_m[name], new_v[name] = (t.reshape(w[name].shape) for t in res)

    small_g = [g_norms, g_mem_norm, g_small["attn_sinks"], g_small["sgu_ln_g"], g_small["sgu_ln_b"], g_small["sgu_w"], g_small["sgu_b"],
               g_small["pool_scale"]]
    packed = _pack(small_g, 16)
    summed = sum_slots(gather_all_devices(packed, "small_grads_all"), "small_grads_sum")
    s_norms, s_mem, s_sinks, s_lg, s_lb, s_w, s_b, s_scale = _unpack(summed, small_g)
    grads["norms"] = lax.dynamic_slice_in_dim(s_norms, chip_id * n_shard, n_shard, axis=2)
    grads["pool_scale"] = lax.dynamic_slice_in_dim(s_scale, chip_id * n_shard, n_shard, axis=1)
    grads.update(mem_norm=s_mem, attn_sinks=s_sinks, sgu_ln_g=s_lg, sgu_ln_b=s_lb, sgu_w=s_w, sgu_b=s_b)
    like = [w[n] for n in SMALL_NAMES]
    packs = [_pack([src[n] for n in SMALL_NAMES], 16)[None] for src in (w, grads, mom, var)]
    for dst, t in zip((delta, new_m, new_v), adamw(*packs, "adamw_small")):
        for n, a in zip(SMALL_NAMES, _unpack(t[0], like)):
            dst[n] = a

    outs = [loss, dx[None]]
    for group in (grads, delta, new_m, new_v):
        outs += [group[n] for n in WEIGHT_ORDER]
    return tuple(outs)
```

```python
import functools

import jax
import jax.numpy as jnp
from jax import lax
from jax.experimental import pallas as pl
from jax.experimental.pallas import tpu as pltpu

F32 = jnp.float32
BF16 = jnp.bfloat16
MESH = pl.DeviceIdType.MESH

EPS = 1e-6
ROPE_THETA = 500000.0
ROPE_HALF = 8
HEAD_DIM = 64
N_Q_HEADS = 16
N_KV_HEADS = 2
Q_PER_KV = 8
BLOCK = 128
ATTN_WIDTH = 1024
KV_WIDTH = 128
QK_WIDTH = ATTN_WIDTH + KV_WIDTH
SGU_WIDTH = 1024
SGU_GROUPS = 8
POOL_WINDOWS = (2, 4, 8, 16)
POOL_HALO = 16
X_HEADS = 4
X_HEAD_DIM = 128
N_CHIPS = 4
N_DEV = 8

ADAM_LR = 0.001
ADAM_B1 = 0.9
ADAM_B2 = 0.999
ADAM_EPS = 1e-08
ADAM_WD = 0.01
ADAM_STEP = 10

VMEM_LIMIT_V7X = 52 * 1024 * 1024
MM_VMEM_BUDGET = 40 * 1024 * 1024
LANES = 128
ROW_TILE_BYTES = 6 * 1024 * 1024
MXU_FLOPS_V7X = 1.0e15
HBM_BYTES_PER_S_V7X = 3.0e12
VMEM_STORE_BYTES_PER_S = 4.0e12
MXU_WEIGHT_LOAD_ROWS = 192
GRID_STEP_S = 0.35e-6


def _params(sem):
    return pltpu.CompilerParams(dimension_semantics=sem, vmem_limit_bytes=VMEM_LIMIT_V7X)


def _pick(dim, pref, align):
    cands = [t for t in range(align, dim + 1, align) if dim % t == 0]
    small = [t for t in cands if t <= pref]
    if small and small[-1] * 2 >= min(pref, dim):
        return small[-1]
    return dim


def _rows(n_rows, bytes_per_row):
    want = max(16, min(1024, ROW_TILE_BYTES // max(1, bytes_per_row)))
    cands = [t for t in range(16, n_rows + 1, 16) if n_rows % t == 0 and t <= want]
    return cands[-1] if cands else n_rows


def _divisors(dim, align, most):
    return [t for t in range(align, min(dim, most) + 1, align) if dim % t == 0] or [dim]


def _mm_tiles(M, N, K, J, m_align, k_align, a_bytes, b_bytes, o_bytes, reduce, ta):
    best = None
    for tm in _divisors(M, m_align, 2048):
        for tn in _divisors(N, LANES, 2048):
            for tk in _divisors(K, k_align, 4096):
                split = K // tk > 1 or reduce
                vmem = 2 * (tm * tk * a_bytes + tk * tn * b_bytes + tm * tn * o_bytes) + tm * tn * 4 * (2 if split else 1)
                if ta:
                    vmem += tm * tk * a_bytes
                if vmem > MM_VMEM_BUDGET:
                    continue
                steps = J * (M // tm) * (N // tn) * (K // tk)
                mxu = 2.0 * J * M * N * K / MXU_FLOPS_V7X * (tm + MXU_WEIGHT_LOAD_ROWS) / tm
                acc = J * M * N * (K // tk) * 8 / VMEM_STORE_BYTES_PER_S if split else 0.0
                hbm = J * (M * K * a_bytes * (N // tn) + K * N * b_bytes * (M // tm) + M * N * o_bytes) / HBM_BYTES_PER_S_V7X
                cost = max(mxu + 0.5 * acc, hbm) + steps * GRID_STEP_S
                if best is None or cost < best[0]:
                    best = (cost, tm, tn, tk)
    return best[1:]


def _rsum8(v):
    r, c = v.shape
    return v.reshape(r // 8, 8, c).sum(axis=0)


def _mm(a, b, *, name, ta=False, tb=False, batch="none", groups=0, a_cb=False, b_cb=False, o_cb=False,
        out_dtype=F32):
    J = groups or (a.shape[0] if a.ndim == 3 else (b.shape[0] if b.ndim == 3 else 1))
    a2, b2 = a.shape[-2:], b.shape[-2:]
    M, K = (a2[1], a2[0]) if ta else a2
    N, Kb = b2 if tb else (b2[1], b2[0])
    if a_cb:
        if ta:
            M //= J
        else:
            K //= J
    if b_cb:
        if tb:
            Kb //= J
        else:
            N //= J
    assert K == Kb, (name, a.shape, b.shape)
    reduce = batch == "reduce"
    tm, tn, tk = _mm_tiles(M, N, K, J, LANES if ta else 16, LANES if (not ta or tb) else 16, a.dtype.itemsize, b.dtype.itemsize,
                           jnp.dtype(out_dtype).itemsize, reduce, ta)
    nm, nn, nk = M // tm, N // tn, K // tk
    if reduce:
        grid = (nm, nn, J, nk)
        unpack = lambda m, n, j, k: (j, m, n, k)
        sem = ("parallel", "parallel", "arbitrary", "arbitrary")
    else:
        grid = (J, nm, nn, nk)
        unpack = lambda j, m, n, k: (j, m, n, k)
        sem = ("parallel", "parallel", "parallel", "arbitrary")

    def a_map(*g):
        j, m, n, k = unpack(*g)
        r, c = (k, m) if ta else (m, k)
        if a_cb:
            c = c + j * (nm if ta else nk)
        return (j, r, c) if a.ndim == 3 else (r, c)

    def b_map(*g):
        j, m, n, k = unpack(*g)
        r, c = (n, k) if tb else (k, n)
        if b_cb:
            c = c + j * (nk if tb else nn)
        return (j, r, c) if b.ndim == 3 else (r, c)

    def o_map(*g):
        j, m, n, k = unpack(*g)
        if o_cb:
            return (m, n + j * nn)
        return (j, m, n) if batch == "map" else (m, n)

    a_blk = (tk, tm) if ta else (tm, tk)
    b_blk = (tn, tk) if tb else (tk, tn)
    a_spec = pl.BlockSpec(((None,) + a_blk) if a.ndim == 3 else a_blk, a_map)
    b_spec = pl.BlockSpec(((None,) + b_blk) if b.ndim == 3 else b_blk, b_map)
    if o_cb:
        out_shape, o_blk = (M, N * J), (tm, tn)
    elif batch == "map":
        out_shape, o_blk = (J, M, N), (None, tm, tn)
    else:
        out_shape, o_blk = (M, N), (tm, tn)
    dims = (((0 if ta else 1,), (1 if tb else 0,)), ((), ()))
    red_axes = (2, 3) if reduce else (3,)

    split = reduce or nk > 1

    def body(a_ref, b_ref, o_ref, *acc):
        prod = lax.dot_general(a_ref[...], b_ref[...], dims, preferred_element_type=F32)
        if not split:
            o_ref[...] = prod.astype(o_ref.dtype)
            return
        acc_ref, = acc
        first = functools.reduce(jnp.logical_and, [pl.program_id(ax) == 0 for ax in red_axes])
        last = functools.reduce(jnp.logical_and, [pl.program_id(ax) == grid[ax] - 1 for ax in red_axes])

        @pl.when(first)
        def _():
            acc_ref[...] = prod

        @pl.when(jnp.logical_not(first))
        def _():
            acc_ref[...] += prod

        @pl.when(last)
        def _():
            o_ref[...] = acc_ref[...].astype(o_ref.dtype)

    return pl.pallas_call(
        body, name=name, grid=grid, in_specs=[a_spec, b_spec], out_specs=pl.BlockSpec(o_blk, o_map),
        out_shape=jax.ShapeDtypeStruct(out_shape, out_dtype), scratch_shapes=[pltpu.VMEM((tm, tn), F32)] if split else [],
        compiler_params=_params(sem),
    )(a, b)


def _rowwise(fn, tiled, whole, outs, accs, *, name):
    n_rows = tiled[0].shape[0]
    row_bytes = sum(t.shape[1] * t.dtype.itemsize for t in tiled) + sum(c * jnp.dtype(d).itemsize for c, d in outs)
    tr = _rows(n_rows, row_bytes)
    n_t, n_w, n_o = len(tiled), len(whole), len(outs)

    def body(*refs):
        i = pl.program_id(0)
        t_refs, w_refs = refs[:n_t], refs[n_t:n_t + n_w]
        o_refs, a_refs = refs[n_t + n_w:n_t + n_w + n_o], refs[n_t + n_w + n_o:]
        o_vals, a_vals = fn(i, *[r[...] for r in t_refs], *[r[...] for r in w_refs])
        for r, v in zip(o_refs, o_vals):
            r[...] = v.astype(r.dtype)
        if a_refs:
            @pl.when(i == 0)
            def _():
                for r in a_refs:
                    r[...] = jnp.zeros_like(r)
            for r, v in zip(a_refs, a_vals):
                r[...] += v

    in_specs = [pl.BlockSpec((tr, t.shape[1]), lambda i: (i, 0)) for t in tiled]
    in_specs += [pl.BlockSpec(w.shape, lambda i, nd=w.ndim: (0,) * nd) for w in whole]
    out_specs = [pl.BlockSpec((tr, c), lambda i: (i, 0)) for c, _ in outs]
    out_specs += [pl.BlockSpec(s, lambda i, nd=len(s): (0,) * nd) for s, _ in accs]
    out_shape = [jax.ShapeDtypeStruct((n_rows, c), d) for c, d in outs]
    out_shape += [jax.ShapeDtypeStruct(s, d) for s, d in accs]
    res = pl.pallas_call(
        body, name=name, grid=(n_rows // tr,), in_specs=in_specs, out_specs=out_specs, out_shape=out_shape,
        compiler_params=_params(("arbitrary",) if accs else ("parallel",)),
    )(*tiled, *whole)
    return res


def _rms_stats(x):
    r = lax.rsqrt(jnp.mean(x * x, axis=-1, keepdims=True) + EPS)
    return x * r, r


def rms_fwd(x, g, out_dtype, name):
    def fn(i, x, g):
        xhat, _ = _rms_stats(x)
        return [xhat * g], []
    return _rowwise(fn, [x], [g], [(x.shape[1], out_dtype)], [], name=name)[0]


def postnorm_res(x, y, g, s, name):
    def fn(i, x, y, g):
        yhat, _ = _rms_stats(y)
        return [x + s * (yhat * g)], []
    return _rowwise(fn, [x, y], [g], [(x.shape[1], F32)], [], name=name)[0]


def rms_bwd(xin, g, douts, scale, add, out_dtype, name):
    n_d = len(douts)

    def fn(i, x, *rest):
        ds, rest = rest[:n_d], rest[n_d:]
        ad = rest[0] if add is not None else None
        g = rest[-1]
        xhat, r = _rms_stats(x)
        d = ds[0].astype(F32)
        for e in ds[1:]:
            d = d + e.astype(F32)
        if scale != 1.0:
            d = d * scale
        dg = _rsum8(d * xhat)
        dxhat = d * g
        dx = r * (dxhat - xhat * jnp.mean(dxhat * xhat, axis=-1, keepdims=True))
        if ad is not None:
            dx = dx + ad
        return [dx], [dg]

    tiled = [xin, *douts] + ([add] if add is not None else [])
    dx, dg = _rowwise(fn, tiled, [g], [(xin.shape[1], out_dtype)], [((8, xin.shape[1]), F32)], name=name)
    return dx, dg


def _silu_parts(g):
    sg = 1.0 / (1.0 + jnp.exp(-g))
    return g * sg, sg


def swiglu_fwd(G, U, name):
    shp = G.shape

    def fn(i, g, u):
        s, _ = _silu_parts(g.astype(F32))
        return [s * u.astype(F32)], []
    a = _rowwise(fn, [G.reshape(-1, shp[-1]), U.reshape(-1, shp[-1])], [], [(shp[-1], BF16)], [], name=name)[0]
    return a.reshape(shp)


def swiglu_bwd(G, U, dA, name):
    shp = G.shape

    def fn(i, g, u, da):
        g, u, da = g.astype(F32), u.astype(F32), da.astype(F32)
        s, sg = _silu_parts(g)
        dsilu = sg * (1.0 + g * (1.0 - sg))
        return [da * u * dsilu, da * s], []
    flat = lambda t: t.reshape(-1, shp[-1])
    dG, dU = _rowwise(fn, [flat(G), flat(U), flat(dA)], [], [(shp[-1], BF16), (shp[-1], BF16)], [], name=name)
    return dG.reshape(shp), dU.reshape(shp)


def scale_cols(y, s, name):
    def fn(i, y, s):
        return [y * s], []
    return _rowwise(fn, [y], [s], [(y.shape[1], F32)], [], name=name)[0]


def scale_cols_bwd(dm, y, s, name):
    def fn(i, dm, y, s):
        return [dm * s], [_rsum8(dm * y)]
    return _rowwise(fn, [dm, y], [s], [(y.shape[1], BF16)], [((8, y.shape[1]), F32)], name=name)


def loss_and_grad(y, target, name):
    n_feat = y.shape[1]

    def fn(i, y, t):
        e = y - t
        return [e * (1.0 / n_feat)], [_rsum8(e * e)]
    dy, part = _rowwise(fn, [y, target], [], [(n_feat, F32)], [((8, n_feat), F32)], name=name)
    return part, dy


def cast_into_slot(w3, layer, chip, name):
    _, n_rows, n_cols = w3.shape
    tr = _rows(n_rows, n_cols * 6)

    def body(chip_ref, w_ref, o_ref):
        o_ref[...] = w_ref[...].astype(BF16)

    grid_spec = pltpu.PrefetchScalarGridSpec(
        num_scalar_prefetch=1, grid=(n_rows // tr,),
        in_specs=[pl.BlockSpec((None, tr, n_cols), lambda i, chip_ref: (layer, i, 0))],
        out_specs=pl.BlockSpec((None, tr, n_cols), lambda i, chip_ref: (chip_ref[0], i, 0)))
    return pl.pallas_call(
        body, name=name, grid_spec=grid_spec, out_shape=jax.ShapeDtypeStruct((N_CHIPS, n_rows, n_cols), BF16),
        compiler_params=_params(("parallel",)),
    )(chip, w3)


def _adam_update(w, g, m, v):
    c1 = 1.0 / (1.0 - ADAM_B1 ** ADAM_STEP)
    c2 = 1.0 / (1.0 - ADAM_B2 ** ADAM_STEP)
    m = ADAM_B1 * m + (1.0 - ADAM_B1) * g
    v = ADAM_B2 * v + (1.0 - ADAM_B2) * (g * g)
    return -ADAM_LR * ((m * c1) / (jnp.sqrt(v * c2) + ADAM_EPS) + ADAM_WD * w), m, v


def adamw(w, g, m, v, name):
    n_l, n_rows, n_cols = w.shape
    tr = _rows(n_rows, n_cols * 4 * 7)

    def body(w_ref, g_ref, m_ref, v_ref, d_ref, mo_ref, vo_ref):
        d_ref[...], mo_ref[...], vo_ref[...] = _adam_update(w_ref[...], g_ref[...], m_ref[...], v_ref[...])

    spec = pl.BlockSpec((None, tr, n_cols), lambda l, i: (l, i, 0))
    shp = jax.ShapeDtypeStruct(w.shape, F32)
    return pl.pallas_call(
        body, name=name, grid=(n_l, n_rows // tr), in_specs=[spec] * 4, out_specs=[spec] * 3, out_shape=[shp] * 3,
        compiler_params=_params(("parallel", "parallel")),
    )(w, g, m, v)


def adamw_from_halves(w, own, recv, m, v, core, name):
    n_l, n_rows, n_cols = w.shape
    half = n_rows // 2
    tr = _rows(half, n_cols * 4 * 9)
    per = half // tr

    def body(core_ref, *refs):
        w_ref, m_ref, v_ref = refs[:3]
        own_refs, recv_refs = refs[3:3 + n_l], refs[3 + n_l:3 + 2 * n_l]
        g_ref, d_ref, mo_ref, vo_ref = refs[3 + 2 * n_l:]
        l, h = pl.program_id(0), pl.program_id(1)
        mine = h == core_ref[0]
        g = jnp.where(mine, own_refs[0][...], recv_refs[0][...])
        for k in range(1, n_l):
            g = jnp.where(l == k, jnp.where(mine, own_refs[k][...], recv_refs[k][...]), g)
        g_ref[...] = g
        d_ref[...], mo_ref[...], vo_ref[...] = _adam_update(w_ref[...], g, m_ref[...], v_ref[...])

    full = pl.BlockSpec((None, tr, n_cols), lambda l, h, i, core_ref: (l, h * per + i, 0))

    def piece(layer, is_own):
        def index(l, h, i, core_ref):
            used = (l == layer) & ((h == core_ref[0]) == is_own)
            return (jnp.where(used, i, 0), 0)
        return pl.BlockSpec((tr, n_cols), index)

    grid_spec = pltpu.PrefetchScalarGridSpec(
        num_scalar_prefetch=1, grid=(n_l, 2, per),
        in_specs=[full] * 3 + [piece(k, True) for k in range(n_l)] + [piece(k, False) for k in range(n_l)],
        out_specs=[full] * 4)
    return pl.pallas_call(
        body, name=name, grid_spec=grid_spec, out_shape=[jax.ShapeDtypeStruct(w.shape, F32)] * 4,
        compiler_params=_params(("parallel", "parallel", "parallel")),
    )(core, w, m, v, *own, *recv)


def rope_tables(n_tok):
    inv = ROPE_THETA ** (-jnp.arange(ROPE_HALF, dtype=F32) * 2.0 / (2 * ROPE_HALF))
    ang = jnp.arange(n_tok, dtype=F32)[:, None] * inv[None, :]
    cos, sin = jnp.cos(ang), jnp.sin(ang)
    rest = HEAD_DIM - 2 * ROPE_HALF
    one, zero, z8 = jnp.ones((n_tok, rest), F32), jnp.zeros((n_tok, rest), F32), jnp.zeros((n_tok, ROPE_HALF), F32)
    c = jnp.concatenate([cos, cos, one], axis=1)
    s1 = jnp.concatenate([-sin, z8, zero], axis=1)
    s2 = jnp.concatenate([z8, sin, zero], axis=1)
    two = lambda t: jnp.concatenate([t, t], axis=1)
    return two(c), two(s1), two(s2)


def rope_apply(x, tabs, n_col_blocks, inverse, out_dtype, name):
    n_tok = x.shape[0]
    tr = _rows(n_tok, LANES * 4 * 6)

    def body(x_ref, c_ref, s1_ref, s2_ref, o_ref):
        x = x_ref[...].astype(F32)
        if inverse:
            out = x * c_ref[...] + pltpu.roll(x * s1_ref[...], ROPE_HALF, 1) + pltpu.roll(x * s2_ref[...], LANES - ROPE_HALF, 1)
        else:
            out = x * c_ref[...] + pltpu.roll(x, LANES - ROPE_HALF, 1) * s1_ref[...] + pltpu.roll(x, ROPE_HALF, 1) * s2_ref[...]
        o_ref[...] = out.astype(o_ref.dtype)

    tab_spec = pl.BlockSpec((tr, LANES), lambda i, c: (i, 0))
    blk = pl.BlockSpec((tr, LANES), lambda i, c: (i, c))
    return pl.pallas_call(
        body, name=name, grid=(n_tok // tr, n_col_blocks), in_specs=[blk, tab_spec, tab_spec, tab_spec], out_specs=blk,
        out_shape=jax.ShapeDtypeStruct((n_tok, n_col_blocks * LANES), out_dtype), compiler_params=_params(("parallel", "parallel")),
    )(x, *tabs)


def _swa_probs(q, k, sink, n):
    rows = Q_PER_KV * BLOCK
    s = lax.dot_general(q, k, (((1,), (1,)), ((), ())), preferred_element_type=F32) * (HEAD_DIM ** -0.5)
    qi = lax.broadcasted_iota(jnp.int32, (rows, 2 * BLOCK), 0) & (BLOCK - 1)
    kj = lax.broadcasted_iota(jnp.int32, (rows, 2 * BLOCK), 1)
    rel = qi + BLOCK - kj
    valid = (rel >= 0) & (rel < BLOCK) & ((n > 0) | (kj >= BLOCK))
    s = jnp.where(valid, s, -1e30)
    m = jnp.maximum(jnp.max(s, axis=-1, keepdims=True), sink)
    e = jnp.exp(s - m)
    es = jnp.exp(sink - m)
    inv = 1.0 / (jnp.sum(e, axis=-1, keepdims=True) + es)
    return e * inv, es * inv


def _swa_specs(n_blocks):
    q_spec = pl.BlockSpec((Q_PER_KV, BLOCK, HEAD_DIM), lambda h, n: (h, n, 0))
    prev = pl.BlockSpec((None, BLOCK, HEAD_DIM), lambda h, n: (h, jnp.maximum(n - 1, 0), 0))
    cur = pl.BlockSpec((None, BLOCK, HEAD_DIM), lambda h, n: (h, n, 0))
    sink = pl.BlockSpec((None, Q_PER_KV * BLOCK, 1), lambda h, n: (h, 0, 0))
    return q_spec, prev, cur, sink


def swa_fwd(q, k, v, sink_rows, name):
    n_tok = q.shape[1]
    q_spec, prev, cur, sink = _swa_specs(n_tok // BLOCK)

    def body(q_ref, kp_ref, kc_ref, vp_ref, vc_ref, s_ref, o_ref):
        n = pl.program_id(1)
        qq = q_ref[...].reshape(Q_PER_KV * BLOCK, HEAD_DIM)
        kk = jnp.concatenate([kp_ref[...], kc_ref[...]], axis=0)
        vv = jnp.concatenate([vp_ref[...], vc_ref[...]], axis=0)
        p, _ = _swa_probs(qq, kk, s_ref[...], n)
        o = jnp.dot(p.astype(BF16), vv, preferred_element_type=F32)
        o_ref[...] = o.reshape(Q_PER_KV, BLOCK, HEAD_DIM).astype(o_ref.dtype)

    return pl.pallas_call(
        body, name=name, grid=(N_KV_HEADS, n_tok // BLOCK), in_specs=[q_spec, prev, cur, prev, cur, sink], out_specs=q_spec,
        out_shape=jax.ShapeDtypeStruct(q.shape, BF16), compiler_params=_params(("parallel", "parallel")),
    )(q, k, k, v, v, sink_rows)


def swa_bwd(q, k, v, sink_rows, do, name):
    n_tok = q.shape[1]
    nb = n_tok // BLOCK
    q_spec, prev, cur, sink = _swa_specs(nb)
    rows = Q_PER_KV * BLOCK

    def body(q_ref, kp_ref, kc_ref, vp_ref, vc_ref, s_ref, do_ref, dq_ref, dkp_ref, dkc_ref, dvp_ref, dvc_ref, ds_ref):
        n = pl.program_id(1)
        qq = q_ref[...].reshape(rows, HEAD_DIM)
        dd = do_ref[...].reshape(rows, HEAD_DIM)
        kk = jnp.concatenate([kp_ref[...], kc_ref[...]], axis=0)
        vv = jnp.concatenate([vp_ref[...], vc_ref[...]], axis=0)
        p, ps = _swa_probs(qq, kk, s_ref[...], n)
        dp = lax.dot_general(dd, vv, (((1,), (1,)), ((), ())), preferred_element_type=F32)
        delta = jnp.sum(p * dp, axis=-1, keepdims=True)
        ds = (p * (dp - delta) * (HEAD_DIM ** -0.5)).astype(BF16)
        dq = jnp.dot(ds, kk, preferred_element_type=F32)
        dk = lax.dot_general(ds, qq, (((0,), (0,)), ((), ())), preferred_element_type=F32)
        dv = lax.dot_general(p.astype(BF16), dd, (((0,), (0,)), ((), ())), preferred_element_type=F32)
        dq_ref[...] = dq.reshape(Q_PER_KV, BLOCK, HEAD_DIM).astype(dq_ref.dtype)
        dkp_ref[...] = dk[:BLOCK]
        dkc_ref[...] = dk[BLOCK:]
        dvp_ref[...] = dv[:BLOCK]
        dvc_ref[...] = dv[BLOCK:]
        dsink = jnp.broadcast_to(-ps * delta, (rows, LANES)).reshape(Q_PER_KV, BLOCK, LANES)
        ds_ref[...] = jnp.sum(dsink, axis=1)

    part = pl.BlockSpec((None, None, BLOCK, HEAD_DIM), lambda h, n: (h, n, 0, 0))
    part_shape = jax.ShapeDtypeStruct((N_KV_HEADS, nb, BLOCK, HEAD_DIM), F32)
    return pl.pallas_call(
        body, name=name, grid=(N_KV_HEADS, nb), in_specs=[q_spec, prev, cur, prev, cur, sink, q_spec],
        out_specs=[q_spec, part, part, part, part, pl.BlockSpec((None, None, Q_PER_KV, LANES), lambda h, n: (h, n, 0, 0))],
        out_shape=[jax.ShapeDtypeStruct(q.shape, BF16), part_shape, part_shape, part_shape, part_shape,
                   jax.ShapeDtypeStruct((N_KV_HEADS, nb, Q_PER_KV, LANES), F32)],
        compiler_params=_params(("parallel", "parallel")),
    )(q, k, k, v, v, sink_rows, do)


def _to_heads(t, n_heads):
    return t.reshape(t.shape[0], n_heads, HEAD_DIM).transpose(1, 0, 2)


def _from_heads(t):
    return t.transpose(1, 0, 2).reshape(t.shape[1], -1)


def _fold_kv_grad(prev, cur):
    shifted = jnp.concatenate([prev[:, 1:], jnp.zeros_like(prev[:, :1])], axis=1)
    tot = (cur + shifted).reshape(N_KV_HEADS, -1, HEAD_DIM)
    return _from_heads(tot)


def _x_probs(qh, kh):
    s = lax.dot_general(qh, kh, (((1,), (1,)), ((), ())), preferred_element_type=F32) * (X_HEAD_DIM ** -0.5)
    e = jnp.exp(s - jnp.max(s, axis=-1, keepdims=True))
    return e * (1.0 / jnp.sum(e, axis=-1, keepdims=True))


def xattn_fwd(q, k, v, name):
    n_tok, width = q.shape
    n_mem = k.shape[0]
    tq = _pick(n_tok, 512, 16)

    def body(q_ref, k_ref, v_ref, o_ref):
        for h in range(X_HEADS):
            cols = slice(h * X_HEAD_DIM, (h + 1) * X_HEAD_DIM)
            p = _x_probs(q_ref[:, cols], k_ref[:, cols])
            o_ref[:, cols] = jnp.dot(p.astype(BF16), v_ref[:, cols], preferred_element_type=F32).astype(o_ref.dtype)

    row = pl.BlockSpec((tq, width), lambda i: (i, 0))
    mem = pl.BlockSpec((n_mem, width), lambda i: (0, 0))
    return pl.pallas_call(
        body, name=name, grid=(n_tok // tq,), in_specs=[row, mem, mem], out_specs=row,
        out_shape=jax.ShapeDtypeStruct(q.shape, BF16), compiler_params=_params(("parallel",)),
    )(q, k, v)


def xattn_bwd(q, k, v, do, name):
    n_tok, width = q.shape
    n_mem = k.shape[0]
    tq = _pick(n_tok, 512, 16)

    def body(q_ref, k_ref, v_ref, do_ref, dq_ref, dk_ref, dv_ref):
        @pl.when(pl.program_id(0) == 0)
        def _():
            dk_ref[...] = jnp.zeros_like(dk_ref)
            dv_ref[...] = jnp.zeros_like(dv_ref)

        for h in range(X_HEADS):
            cols = slice(h * X_HEAD_DIM, (h + 1) * X_HEAD_DIM)
            qh, kh, vh, dh = q_ref[:, cols], k_ref[:, cols], v_ref[:, cols], do_ref[:, cols]
            p = _x_probs(qh, kh)
            dp = lax.dot_general(dh, vh, (((1,), (1,)), ((), ())), preferred_element_type=F32)
            delta = jnp.sum(p * dp, axis=-1, keepdims=True)
            ds = (p * (dp - delta) * (X_HEAD_DIM ** -0.5)).astype(BF16)
            dq_ref[:, cols] = jnp.dot(ds, kh, preferred_element_type=F32).astype(dq_ref.dtype)
            dk_ref[:, cols] += lax.dot_general(ds, qh, (((0,), (0,)), ((), ())), preferred_element_type=F32)
            dv_ref[:, cols] += lax.dot_general(p.astype(BF16), dh, (((0,), (0,)), ((), ())), preferred_element_type=F32)

    row = pl.BlockSpec((tq, width), lambda i: (i, 0))
    mem = pl.BlockSpec((n_mem, width), lambda i: (0, 0))
    return pl.pallas_call(
        body, name=name, grid=(n_tok // tq,), in_specs=[row, mem, mem, row], out_specs=[row, mem, mem],
        out_shape=[jax.ShapeDtypeStruct(q.shape, BF16), jax.ShapeDtypeStruct(k.shape, F32), jax.ShapeDtypeStruct(k.shape, F32)],
        compiler_params=_params(("arbitrary",)),
    )(q, k, v, do)


GELU_C = 0.7978845608028654
GELU_A = 0.044715


def _gelu_parts(x):
    x2 = x * x
    t = jnp.tanh(GELU_C * x * (1.0 + GELU_A * x2))
    y = 0.5 * x * (1.0 + t)
    dy = 0.5 * (1.0 + t) + 0.5 * x * (1.0 - t * t) * GELU_C * (1.0 + 3.0 * GELU_A * x2)
    return y, dy


def _sgu_norm(v, ln_g, ln_b):
    mu = jnp.mean(v, axis=-1, keepdims=True)
    vc = v - mu
    r = lax.rsqrt(jnp.mean(vc * vc, axis=-1, keepdims=True) + EPS)
    xhat = vc * r
    return xhat * ln_g + ln_b, xhat, r


def _causal_weights(w_ref):
    i = lax.broadcasted_iota(jnp.int32, (BLOCK, BLOCK), 0)
    j = lax.broadcasted_iota(jnp.int32, (BLOCK, BLOCK), 1)
    return [jnp.where(i >= j, w_ref[g], 0.0).astype(BF16) for g in range(SGU_GROUPS)]


def sgu_fwd(u_pre, v_pre, ln_g, ln_b, w_s, bias_rows, name):
    n_tok = u_pre.shape[0]
    tm = _pick(n_tok, 512, BLOCK)

    def body(u_ref, v_ref, g_ref, b_ref, w_ref, bb_ref, o_ref):
        vn, _, _ = _sgu_norm(_gelu_parts(v_ref[...])[0], g_ref[...], b_ref[...])
        vn = vn.astype(BF16)
        wc = _causal_weights(w_ref)
        for c in range(tm // BLOCK):
            rows = slice(c * BLOCK, (c + 1) * BLOCK)
            for g in range(SGU_GROUPS):
                cols = slice(g * LANES, (g + 1) * LANES)
                mixed = jnp.dot(wc[g], vn[rows, cols], preferred_element_type=F32) + bb_ref[g]
                u = _gelu_parts(u_ref[rows, cols])[0]
                o_ref[rows, cols] = (u * mixed).astype(o_ref.dtype)

    row = pl.BlockSpec((tm, SGU_WIDTH), lambda i: (i, 0))
    vec = pl.BlockSpec((1, SGU_WIDTH), lambda i: (0, 0))
    mat = pl.BlockSpec((SGU_GROUPS, BLOCK, LANES), lambda i: (0, 0, 0))
    return pl.pallas_call(
        body, name=name, grid=(n_tok // tm,), in_specs=[row, row, vec, vec, mat, mat], out_specs=row,
        out_shape=jax.ShapeDtypeStruct((n_tok, SGU_WIDTH), BF16), compiler_params=_params(("parallel",)),
    )(u_pre, v_pre, ln_g, ln_b, w_s, bias_rows)


def sgu_bwd(u_pre, v_pre, ln_g, ln_b, w_s, bias_rows, dgate, name):
    n_tok = u_pre.shape[0]
    tm = _pick(n_tok, 512, BLOCK)

    def body(u_ref, v_ref, g_ref, b_ref, w_ref, bb_ref, dg_ref, du_ref, dv_ref, dw_ref, db_ref, dlg_ref, dlb_ref, dvn_ref):
        @pl.when(pl.program_id(0) == 0)
        def _():
            dw_ref[...] = jnp.zeros_like(dw_ref)
            db_ref[...] = jnp.zeros_like(db_ref)
            dlg_ref[...] = jnp.zeros_like(dlg_ref)
            dlb_ref[...] = jnp.zeros_like(dlb_ref)

        gv, dgv = _gelu_parts(v_ref[...])
        vn, xhat, r = _sgu_norm(gv, g_ref[...], b_ref[...])
        vn = vn.astype(BF16)
        wc = _causal_weights(w_ref)
        for c in range(tm // BLOCK):
            rows = slice(c * BLOCK, (c + 1) * BLOCK)
            for g in range(SGU_GROUPS):
                cols = slice(g * LANES, (g + 1) * LANES)
                vt = vn[rows, cols]
                mixed = jnp.dot(wc[g], vt, preferred_element_type=F32) + bb_ref[g]
                u, du_dpre = _gelu_parts(u_ref[rows, cols])
                dgate_t = dg_ref[rows, cols].astype(F32)
                du_ref[rows, cols] = (dgate_t * mixed * du_dpre).astype(du_ref.dtype)
                dmix = dgate_t * u
                dmix_b = dmix.astype(BF16)
                db_ref[g] += dmix
                dw_ref[g] += lax.dot_general(dmix_b, vt, (((1,), (1,)), ((), ())), preferred_element_type=F32)
                dvn_ref[rows, cols] = lax.dot_general(wc[g], dmix_b, (((0,), (0,)), ((), ())), preferred_element_type=F32)
        dvn = dvn_ref[...]
        dlg_ref[...] += _rsum8(dvn * xhat)
        dlb_ref[...] += _rsum8(dvn)
        dxhat = dvn * g_ref[...]
        dgv_in = r * (dxhat - jnp.mean(dxhat, axis=-1, keepdims=True) - xhat * jnp.mean(dxhat * xhat, axis=-1, keepdims=True))
        dv_ref[...] = (dgv_in * dgv).astype(dv_ref.dtype)

    row = pl.BlockSpec((tm, SGU_WIDTH), lambda i: (i, 0))
    vec = pl.BlockSpec((1, SGU_WIDTH), lambda i: (0, 0))
    mat = pl.BlockSpec((SGU_GROUPS, BLOCK, LANES), lambda i: (0, 0, 0))
    part = pl.BlockSpec((8, SGU_WIDTH), lambda i: (0, 0))
    mat_shape = jax.ShapeDtypeStruct((SGU_GROUPS, BLOCK, LANES), F32)
    part_shape = jax.ShapeDtypeStruct((8, SGU_WIDTH), F32)
    act_shape = jax.ShapeDtypeStruct((n_tok, SGU_WIDTH), BF16)
    return pl.pallas_call(
        body, name=name, grid=(n_tok // tm,), in_specs=[row, row, vec, vec, mat, mat, row],
        out_specs=[row, row, mat, mat, part, part], out_shape=[act_shape, act_shape, mat_shape, mat_shape, part_shape, part_shape],
        scratch_shapes=[pltpu.VMEM((tm, SGU_WIDTH), F32)], compiler_params=_params(("arbitrary",)),
    )(u_pre, v_pre, ln_g, ln_b, w_s, bias_rows, dgate)


def _pool_tile(n_tok):
    return _pick(n_tok, 256, POOL_HALO)


def pool_fwd(h, name):
    n_tok, width = h.shape
    gw = width // len(POOL_WINDOWS)
    tm = _pool_tile(n_tok)
    per = tm // POOL_HALO

    def body(cur_ref, halo_ref, o_ref, buf_ref):
        i = pl.program_id(0)
        buf_ref[0:POOL_HALO, :] = jnp.where(i > 0, halo_ref[...], 0.0)
        buf_ref[POOL_HALO:, :] = cur_ref[...]
        tok = i * tm + lax.broadcasted_iota(jnp.int32, (tm, 1), 0)
        for g, w in enumerate(POOL_WINDOWS):
            cols = slice(g * gw, (g + 1) * gw)
            acc = buf_ref[POOL_HALO:, cols]
            for j in range(1, w):
                acc = acc + buf_ref[POOL_HALO - j:POOL_HALO - j + tm, cols]
            cnt = jnp.minimum(tok + 1, w).astype(F32)
            o_ref[:, cols] = (acc / cnt - cur_ref[:, cols]).astype(o_ref.dtype)

    return pl.pallas_call(
        body, name=name, grid=(n_tok // tm,),
        in_specs=[pl.BlockSpec((tm, width), lambda i: (i, 0)),
                  pl.BlockSpec((POOL_HALO, width), lambda i: (jnp.maximum(i * per - 1, 0), 0))],
        out_specs=pl.BlockSpec((tm, width), lambda i: (i, 0)), out_shape=jax.ShapeDtypeStruct(h.shape, BF16),
        scratch_shapes=[pltpu.VMEM((tm + POOL_HALO, width), F32)], compiler_params=_params(("parallel",)),
    )(h, h)


def pool_bwd(dp, name):
    n_tok, width = dp.shape
    gw = width // len(POOL_WINDOWS)
    tm = _pool_tile(n_tok)
    per = tm // POOL_HALO
    n_steps = n_tok // tm

    def body(cur_ref, halo_ref, o_ref, buf_ref):
        i = pl.program_id(0)
        tok = i * tm + lax.broadcasted_iota(jnp.int32, (tm, 1), 0)
        for g, w in enumerate(POOL_WINDOWS):
            cols = slice(g * gw, (g + 1) * gw)
            cnt = jnp.minimum(tok + 1, w).astype(F32)
            buf_ref[0:tm, cols] = cur_ref[:, cols] / cnt
            buf_ref[tm:, cols] = jnp.where(i < n_steps - 1, halo_ref[:, cols] / float(w), 0.0)
        for g, w in enumerate(POOL_WINDOWS):
            cols = slice(g * gw, (g + 1) * gw)
            acc = buf_ref[0:tm, cols]
            for j in range(1, w):
                acc = acc + buf_ref[j:j + tm, cols]
            o_ref[:, cols] = acc - cur_ref[:, cols]

    return pl.pallas_call(
        body, name=name, grid=(n_steps,),
        in_specs=[pl.BlockSpec((tm, width), lambda i: (i, 0)),
                  pl.BlockSpec((POOL_HALO, width), lambda i: (jnp.minimum((i + 1) * per, n_tok // POOL_HALO - 1), 0))],
        out_specs=pl.BlockSpec((tm, width), lambda i: (i, 0)), out_shape=jax.ShapeDtypeStruct(dp.shape, F32),
        scratch_shapes=[pltpu.VMEM((tm + POOL_HALO, width), F32)], compiler_params=_params(("parallel",)),
    )(dp, dp)


def _ffn_fwd(x, ga, gb, wg, wu, wd, tag):
    h = rms_fwd(x, ga, BF16, f"{tag}_prenorm")
    G = _mm(h, wg, batch="map", out_dtype=BF16, name=f"{tag}_gate")
    U = _mm(h, wu, batch="map", out_dtype=BF16, name=f"{tag}_up")
    A = swiglu_fwd(G, U, f"{tag}_swiglu")
    y = _mm(A, wd, batch="reduce", name=f"{tag}_down")
    return postnorm_res(x, y, gb, 0.5, f"{tag}_postnorm"), (x, h, G, U, A, y)


def _ffn_bwd(res, ga, gb, wg, wu, wd, dx2, tag):
    x, h, G, U, A, y = res
    dy, dgb = rms_bwd(y, gb, [dx2], 0.5, None, BF16, f"{tag}_postnorm_bwd")
    dA = _mm(dy, wd, tb=True, batch="map", out_dtype=BF16, name=f"{tag}_down_dx")
    dwd = _mm(A, dy, ta=True, batch="map", name=f"{tag}_down_dw", out_dtype=BF16)
    dG, dU = swiglu_bwd(G, U, dA, f"{tag}_swiglu_bwd")
    dwg = _mm(h, dG, ta=True, batch="map", name=f"{tag}_gate_dw", out_dtype=BF16)
    dwu = _mm(h, dU, ta=True, batch="map", name=f"{tag}_up_dw", out_dtype=BF16)
    dh1 = _mm(dG, wg, tb=True, batch="reduce", name=f"{tag}_gate_dx")
    dh2 = _mm(dU, wu, tb=True, batch="reduce", name=f"{tag}_up_dx")
    dx, dga = rms_bwd(x, ga, [dh1, dh2], 1.0, dx2, F32, f"{tag}_prenorm_bwd")
    return dx, dga, dgb, dwg, dwu, dwd


def _sink_rows(sinks):
    return jnp.repeat(sinks.reshape(N_KV_HEADS, Q_PER_KV), BLOCK, axis=1)[..., None]


def _attn_sgu_fwd(x, g_pre, g_post, w_in, w_out, sinks, ln_g, ln_b, sgu_w, bias_rows, tabs, tag):
    h = rms_fwd(x, g_pre, BF16, f"{tag}_prenorm")
    z = _mm(h, w_in, name=f"{tag}_in")
    qk = rope_apply(z, tabs, QK_WIDTH // LANES, False, BF16, f"{tag}_rope")
    q = _to_heads(qk[:, :ATTN_WIDTH], N_Q_HEADS)
    k = _to_heads(qk[:, ATTN_WIDTH:], N_KV_HEADS)
    v = _to_heads(z[:, QK_WIDTH:QK_WIDTH + KV_WIDTH].astype(BF16), N_KV_HEADS)
    o = swa_fwd(q, k, v, _sink_rows(sinks), f"{tag}_swa")
    u_pre = z[:, QK_WIDTH + KV_WIDTH:QK_WIDTH + KV_WIDTH + SGU_WIDTH]
    v_pre = z[:, QK_WIDTH + KV_WIDTH + SGU_WIDTH:]
    gate = sgu_fwd(u_pre, v_pre, ln_g, ln_b, sgu_w, bias_rows, f"{tag}_sgu")
    cat = jnp.concatenate([_from_heads(o), gate], axis=1)
    m = _mm(cat, w_out, name=f"{tag}_out")
    return postnorm_res(x, m, g_post, 1.0, f"{tag}_postnorm"), (x, h, q, k, v, u_pre, v_pre, cat, m)


def _attn_sgu_bwd(res, g_pre, g_post, w_in, w_out, sinks, ln_g, ln_b, sgu_w, bias_rows, tabs, dx2, tag):
    x, h, q, k, v, u_pre, v_pre, cat, m = res
    dm, dg_post = rms_bwd(m, g_post, [dx2], 1.0, None, BF16, f"{tag}_postnorm_bwd")
    dcat = _mm(dm, w_out, tb=True, out_dtype=BF16, name=f"{tag}_out_dx")
    dw_out = _mm(cat, dm, ta=True, name=f"{tag}_out_dw", out_dtype=BF16)
    do = _to_heads(dcat[:, :ATTN_WIDTH], N_Q_HEADS)
    dq, dkp, dkc, dvp, dvc, dsink = swa_bwd(q, k, v, _sink_rows(sinks), do, f"{tag}_swa_bwd")
    d_sinks = jnp.sum(dsink[..., 0], axis=1).reshape(1, N_Q_HEADS)
    dqk_rot = jnp.concatenate([_from_heads(dq).astype(F32), _fold_kv_grad(dkp, dkc)], axis=1)
    dqk = rope_apply(dqk_rot, tabs, QK_WIDTH // LANES, True, BF16, f"{tag}_rope_bwd")
    dv = _fold_kv_grad(dvp, dvc).astype(BF16)
    du_pre, dv_pre, dw_s, dbias, dlg, dlb = sgu_bwd(u_pre, v_pre, ln_g, ln_b, sgu_w, bias_rows, dcat[:, ATTN_WIDTH:], f"{tag}_sgu_bwd")
    dz = jnp.concatenate([dqk, dv, du_pre, dv_pre], axis=1)
    dw_in = _mm(h, dz, ta=True, name=f"{tag}_in_dw", out_dtype=BF16)
    dh = _mm(dz, w_in, tb=True, name=f"{tag}_in_dx")
    dx, dg_pre = rms_bwd(x, g_pre, [dh], 1.0, dx2, F32, f"{tag}_prenorm_bwd")
    causal = jnp.tril(jnp.ones((BLOCK, BLOCK), F32))
    small = dict(attn_sinks=d_sinks, sgu_ln_g=jnp.sum(dlg, axis=0, keepdims=True), sgu_ln_b=jnp.sum(dlb, axis=0, keepdims=True),
                 sgu_w=(dw_s * causal[None])[None], sgu_b=jnp.sum(dbias, axis=-1)[None])
    return dx, dg_pre, dg_post, dw_in, dw_out, small


def _pool_mix_fwd(x, g_pre, g_post, pool_w, pool_scale, tag):
    hf = rms_fwd(x, g_pre, F32, f"{tag}_prenorm")
    pooled = pool_fwd(hf, f"{tag}_pool")
    n_g = len(POOL_WINDOWS)
    ypre = _mm(pooled, pool_w, batch="map", groups=n_g, a_cb=True, o_cb=True, name=f"{tag}_proj")
    m = scale_cols(ypre, pool_scale, f"{tag}_scale")
    return postnorm_res(x, m, g_post, 1.0, f"{tag}_postnorm"), (x, pooled, ypre, m)


def _pool_mix_bwd(res, g_pre, g_post, pool_w, pool_scale, dx2, tag):
    x, pooled, ypre, m = res
    n_g = len(POOL_WINDOWS)
    dm, dg_post = rms_bwd(m, g_post, [dx2], 1.0, None, F32, f"{tag}_postnorm_bwd")
    dypre, dscale = scale_cols_bwd(dm, ypre, pool_scale, f"{tag}_scale_bwd")
    dpooled = _mm(dypre, pool_w, tb=True, batch="map", groups=n_g, a_cb=True, o_cb=True, name=f"{tag}_proj_dx")
    dpw = _mm(pooled, dypre, ta=True, batch="map", groups=n_g, a_cb=True, b_cb=True, name=f"{tag}_proj_dw", out_dtype=BF16)
    dhf = pool_bwd(dpooled, f"{tag}_pool_bwd")
    dx, dg_pre = rms_bwd(x, g_pre, [dhf], 1.0, dx2, F32, f"{tag}_prenorm_bwd")
    return dx, dg_pre, dg_post, dpw, jnp.sum(dscale, axis=0, keepdims=True)


def _xattn_fwd(x, mem, g_pre, g_post, g_mem, wq, wk, wv, wo, tag):
    h = rms_fwd(x, g_pre, BF16, f"{tag}_prenorm")
    mem_n = rms_fwd(mem, g_mem, BF16, f"{tag}_memnorm")
    q = _mm(h, wq, out_dtype=BF16, name=f"{tag}_q")
    k = _mm(mem_n, wk, out_dtype=BF16, name=f"{tag}_k")
    v = _mm(mem_n, wv, out_dtype=BF16, name=f"{tag}_v")
    o = xattn_fwd(q, k, v, f"{tag}_core")
    r = _mm(o, wo, name=f"{tag}_o")
    return postnorm_res(x, r, g_post, 1.0, f"{tag}_postnorm"), (x, h, mem_n, q, k, v, o, r)


def _xattn_bwd(res, mem, g_pre, g_post, g_mem, wq, wk, wv, wo, dx2, tag):
    x, h, mem_n, q, k, v, o, r = res
    dr, dg_post = rms_bwd(r, g_post, [dx2], 1.0, None, BF16, f"{tag}_postnorm_bwd")
    do = _mm(dr, wo, tb=True, out_dtype=BF16, name=f"{tag}_o_dx")
    dwo = _mm(o, dr, ta=True, name=f"{tag}_o_dw", out_dtype=BF16)
    dq, dk, dv = xattn_bwd(q, k, v, do, f"{tag}_core_bwd")
    dk, dv = dk.astype(BF16), dv.astype(BF16)
    dwq = _mm(h, dq, ta=True, name=f"{tag}_q_dw", out_dtype=BF16)
    dwk = _mm(mem_n, dk, ta=True, name=f"{tag}_k_dw", out_dtype=BF16)
    dwv = _mm(mem_n, dv, ta=True, name=f"{tag}_v_dw", out_dtype=BF16)
    dh = _mm(dq, wq, tb=True, name=f"{tag}_q_dx")
    dmem1 = _mm(dk, wk, tb=True, name=f"{tag}_k_dx")
    dmem2 = _mm(dv, wv, tb=True, name=f"{tag}_v_dx")
    _, dg_mem = rms_bwd(mem, g_mem, [dmem1, dmem2], 1.0, None, BF16, f"{tag}_memnorm_bwd")
    dx, dg_pre = rms_bwd(x, g_pre, [dh], 1.0, dx2, F32, f"{tag}_prenorm_bwd")
    return dx, dg_pre, dg_post, dg_mem, dwq, dwk, dwv, dwo


def _rowsum8(part):
    return jnp.sum(part, axis=0, keepdims=True)


def device_step(x, mem, target, norms, mem_norm, small, fetch, emit):
    n_tok = x.shape[0]
    tabs = rope_tables(n_tok)
    bias_rows = jnp.broadcast_to(small["sgu_b"][:, :, None], (SGU_GROUPS, BLOCK, LANES))
    gn = lambda l, i: norms[l, i][None, :]
    gm = lambda l: mem_norm[l][None, :]
    mix0 = (small["attn_sinks"], small["sgu_ln_g"], small["sgu_ln_b"], small["sgu_w"], bias_rows, tabs)

    wts, saved = {}, []
    for l in range(2):
        wts["ffn1", l], tok = fetch(("ffn1", l), x)
        x, r1 = _ffn_fwd(x, gn(l, 0) + tok, gn(l, 1), *wts["ffn1", l], f"l{l}_ffn1")
        wts["mix", l], tok = fetch(("mix", l), x)
        if l == 0:
            x, r2 = _attn_sgu_fwd(x, gn(l, 2) + tok, gn(l, 3), *wts["mix", l], *mix0, f"l{l}_mix")
        else:
            x, r2 = _pool_mix_fwd(x, gn(l, 2) + tok, gn(l, 3), *wts["mix", l], small["pool_scale"], f"l{l}_mix")
        wts["xattn", l], tok = fetch(("xattn", l), x)
        x, r3 = _xattn_fwd(x, mem, gn(l, 4) + tok, gn(l, 5), gm(l), *wts["xattn", l], f"l{l}_xattn")
        wts["ffn2", l], tok = fetch(("ffn2", l), x)
        x, r4 = _ffn_fwd(x, gn(l, 6) + tok, gn(l, 7), *wts["ffn2", l], f"l{l}_ffn2")
        saved.append((r1, r2, r3, r4))

    loss_part, dx = loss_and_grad(x, target, "loss")

    g_norm_rows = [[None] * 8, [None] * 8]
    g_mem_rows = [None, None]
    g_small = {}
    for l in (1, 0):
        r1, r2, r3, r4 = saved[l]
        dx, g_norm_rows[l][6], g_norm_rows[l][7], dwg, dwu, dwd = _ffn_bwd(r4, gn(l, 6), gn(l, 7), *wts["ffn2", l], dx, f"l{l}_ffn2")
        emit(("ffn2", l), dict(ffn2_wg=dwg, ffn2_wu=dwu, ffn2_wd=dwd))
        dx, g_norm_rows[l][4], g_norm_rows[l][5], g_mem_rows[l], dwq, dwk, dwv, dwo = _xattn_bwd(
            r3, mem, gn(l, 4), gn(l, 5), gm(l), *wts["xattn", l], dx, f"l{l}_xattn")
        emit(("xattn", l), dict(x_wq=dwq, x_wk=dwk, x_wv=dwv, x_wo=dwo))
        if l == 0:
            dx, g_norm_rows[l][2], g_norm_rows[l][3], dw_in, dw_out, sm = _attn_sgu_bwd(
                r2, gn(l, 2), gn(l, 3), *wts["mix", l], *mix0, dx, f"l{l}_mix")
            emit(("mix", l), dict(mix_w_in=dw_in, mix_w_out=dw_out))
            g_small.update(sm)
        else:
            dx, g_norm_rows[l][2], g_norm_rows[l][3], dpw, dscale = _pool_mix_bwd(
                r2, gn(l, 2), gn(l, 3), *wts["mix", l], small["pool_scale"], dx, f"l{l}_mix")
            emit(("mix", l), dict(pool_w=dpw))
            g_small["pool_scale"] = dscale
        dx, g_norm_rows[l][0], g_norm_rows[l][1], dwg, dwu, dwd = _ffn_bwd(r1, gn(l, 0), gn(l, 1), *wts["ffn1", l], dx, f"l{l}_ffn1")
        emit(("ffn1", l), dict(ffn1_wg=dwg, ffn1_wu=dwu, ffn1_wd=dwd))
    g_norms = jnp.stack([jnp.concatenate([_rowsum8(p) for p in g_norm_rows[l]], axis=0) for l in range(2)])
    g_mem_norm = jnp.concatenate([_rowsum8(p) for p in g_mem_rows], axis=0)
    return loss_part, dx, g_small, g_norms, g_mem_norm


ANY = pl.BlockSpec(memory_space=pl.ANY)


def _place():
    x, y, c = lax.axis_index("x"), lax.axis_index("y"), lax.axis_index("c")
    other_chips = [(1 - x, y), (x, 1 - y), (1 - x, 1 - y)]
    return x, y, c, other_chips


def _half_rows(core, n_rows):
    half = n_rows // 2
    return pl.ds(pl.multiple_of(core * half, 16), half)


def _remote(src, dst, send_sem, recv_sem, device):
    return pltpu.make_async_remote_copy(src_ref=src, dst_ref=dst, send_sem=send_sem, recv_sem=recv_sem,
                                        device_id=device, device_id_type=MESH)


def gather_shards(slots, name):
    n = len(slots)

    def body(*refs):
        out = refs[n:2 * n]
        send_sems, recv_sems = refs[2 * n:]
        x, y, c, chips = _place()
        me, sibling = 2 * x + y, (x, y, 1 - c)
        sends = []
        for i in range(n):
            mine = out[i].at[me, _half_rows(c, out[i].shape[1])]
            for r, (px, py) in enumerate(chips):
                sends.append(_remote(mine, mine, send_sems.at[6 * i + r], recv_sems.at[6 * i + r], (px, py, c)))
                sends[-1].start()
        for i in range(n):
            for r, (px, py) in enumerate(chips):
                landed = out[i].at[2 * px + py, _half_rows(c, out[i].shape[1])]
                _remote(landed, landed, send_sems.at[6 * i + r], recv_sems.at[6 * i + r], (px, py, c)).wait_recv()
                sends.append(_remote(landed, landed, send_sems.at[6 * i + 3 + r], recv_sems.at[6 * i + 3 + r], sibling))
                sends[-1].start()
        for i in range(n):
            for r, (px, py) in enumerate(chips):
                landed = out[i].at[2 * px + py, _half_rows(1 - c, out[i].shape[1])]
                _remote(landed, landed, send_sems.at[6 * i + 3 + r], recv_sems.at[6 * i + 3 + r], sibling).wait_recv()
        for cp in sends:
            cp.wait_send()

    return pl.pallas_call(
        body, name=name, in_specs=[ANY] * n, out_specs=[ANY] * n, input_output_aliases={i: i for i in range(n)},
        out_shape=[jax.ShapeDtypeStruct(s.shape, s.dtype) for s in slots],
        scratch_shapes=[pltpu.SemaphoreType.DMA((6 * n,)), pltpu.SemaphoreType.DMA((6 * n,))],
    )(*slots)


HBM = pl.BlockSpec(memory_space=pltpu.HBM)
SEM = pl.BlockSpec(memory_space=pltpu.SEMAPHORE)
DATAFLOW = pltpu.SideEffectType.DATAFLOW_SIDE_EFFECTING


def _chip_copies(bufs, send_sems, recv_sems):
    x, y, c, chips = _place()
    me = 2 * x + y
    sends, arrivals = [], []
    for i, buf in enumerate(bufs):
        rows = _half_rows(c, buf.shape[1])
        for r, (px, py) in enumerate(chips):
            mine, theirs = buf.at[me, rows], buf.at[2 * px + py, rows]
            sends.append(_remote(mine, mine, send_sems.at[3 * i + r], recv_sems.at[3 * i + r], (px, py, c)))
            arrivals.append(_remote(theirs, theirs, send_sems.at[3 * i + r], recv_sems.at[3 * i + r], (px, py, c)))
    return sends, arrivals


def gather_start(slots, after, name):
    n = len(slots)

    def body(*refs):
        send_sems, recv_sems = refs[n + 1], refs[n + 2]
        bufs, token = refs[n + 3:2 * n + 3], refs[2 * n + 3]
        sends, _ = _chip_copies(bufs, send_sems, recv_sems)
        for cp in sends:
            cp.start()
        token[...] = jnp.zeros_like(token)

    res = pl.pallas_call(
        body, name=name, in_specs=[HBM] * n + [ANY],
        out_specs=[SEM, SEM] + [HBM] * n + [pl.BlockSpec(memory_space=pltpu.VMEM)],
        out_shape=[pltpu.SemaphoreType.DMA((3 * n,)), pltpu.SemaphoreType.DMA((3 * n,))]
        + [pltpu.HBM(s.shape, s.dtype) for s in slots] + [jax.ShapeDtypeStruct((8, LANES), F32)],
        input_output_aliases={i: 2 + i for i in range(n)}, compiler_params=pltpu.CompilerParams(has_side_effects=DATAFLOW),
    )(*[pltpu.with_memory_space_constraint(s, pltpu.HBM) for s in slots], after)
    return res[0], res[1], list(res[2:2 + n]), res[2 + n]


def gather_wait(send_sems, recv_sems, bufs, after, name):
    n = len(bufs)

    def body(*refs):
        send_ref, recv_ref = refs[n], refs[n + 1]
        _, arrivals = _chip_copies(refs[:n], send_ref, recv_ref)
        for cp in arrivals:
            cp.wait_send()
            cp.wait_recv()

    return pl.pallas_call(
        body, name=name, in_specs=[HBM] * n + [SEM, SEM, ANY], out_specs=[HBM] * n,
        out_shape=[pltpu.HBM(b.shape, b.dtype) for b in bufs], input_output_aliases={i: i for i in range(n)},
        compiler_params=pltpu.CompilerParams(has_side_effects=DATAFLOW),
    )(*bufs, send_sems, recv_sems, after)


def forward_to_sibling(bufs, name):
    n = len(bufs)

    def body(*refs):
        out = refs[n:2 * n]
        send_sems, recv_sems = refs[2 * n:]
        x, y, c, chips = _place()
        copies = []
        for i in range(n):
            for r, (px, py) in enumerate(chips):
                mine = out[i].at[2 * px + py, _half_rows(c, out[i].shape[1])]
                copies.append(_remote(mine, mine, send_sems.at[3 * i + r], recv_sems.at[3 * i + r], (x, y, 1 - c)))
                copies[-1].start()
        for i in range(n):
            for r, (px, py) in enumerate(chips):
                theirs = out[i].at[2 * px + py, _half_rows(1 - c, out[i].shape[1])]
                _remote(theirs, theirs, send_sems.at[3 * i + r], recv_sems.at[3 * i + r], (x, y, 1 - c)).wait_recv()
        for cp in copies:
            cp.wait_send()

    return pl.pallas_call(
        body, name=name, in_specs=[ANY] * n, out_specs=[ANY] * n, input_output_aliases={i: i for i in range(n)},
        out_shape=[jax.ShapeDtypeStruct(b.shape, b.dtype) for b in bufs],
        scratch_shapes=[pltpu.SemaphoreType.DMA((3 * n,)), pltpu.SemaphoreType.DMA((3 * n,))],
    )(*bufs)


def swap_other_halves(grads, name):
    n = len(grads)

    def body(*refs):
        src, out = refs[:n], refs[n:2 * n]
        send_sems, recv_sems = refs[2 * n:]
        x, y, c, _ = _place()
        copies = [_remote(src[i].at[:, _half_rows(1 - c, src[i].shape[1])], out[i], send_sems.at[i], recv_sems.at[i], (x, y, 1 - c))
                  for i in range(n)]
        for cp in copies:
            cp.start()
        for cp in copies:
            cp.wait()

    return pl.pallas_call(
        body, name=name, in_specs=[ANY] * n, out_specs=[ANY] * n,
        out_shape=[jax.ShapeDtypeStruct((g.shape[0], g.shape[1] // 2, g.shape[2]), g.dtype) for g in grads],
        scratch_shapes=[pltpu.SemaphoreType.DMA((n,)), pltpu.SemaphoreType.DMA((n,))],
    )(*grads)


def add_own_half(g, p, core, name):
    n_j, n_rows, n_cols = g.shape
    half = n_rows // 2
    tr = _rows(half, n_cols * 10)

    def body(c_ref, g_ref, p_ref, o_ref):
        o_ref[...] = (g_ref[...] + p_ref[...]).astype(o_ref.dtype)

    grid_spec = pltpu.PrefetchScalarGridSpec(
        num_scalar_prefetch=1, grid=(n_j, half // tr),
        in_specs=[pl.BlockSpec((None, None, tr, n_cols), lambda j, i, c_ref: (j, c_ref[0], i, 0)),
                  pl.BlockSpec((None, tr, n_cols), lambda j, i, c_ref: (j, i, 0))],
        out_specs=pl.BlockSpec((None, tr, n_cols), lambda j, i, c_ref: (j, i, 0)))
    return pl.pallas_call(
        body, name=name, grid_spec=grid_spec, out_shape=jax.ShapeDtypeStruct((n_j, half, n_cols), BF16),
        compiler_params=_params(("parallel", "parallel")),
    )(core, g.reshape(n_j, 2, half, n_cols), p)


def scatter_to_chips(parts, name):
    n = len(parts)

    def body(*refs):
        src, out = refs[:n], refs[n:2 * n]
        send_sems, recv_sems = refs[2 * n:]
        x, y, c, chips = _place()
        me = 2 * x + y
        sends = []
        for i in range(n):
            for r, (px, py) in enumerate(chips):
                sends.append(_remote(src[i].at[2 * px + py], out[i].at[me], send_sems.at[3 * i + r], recv_sems.at[3 * i + r], (px, py, c)))
                sends[-1].start()
        for i in range(n):
            for r, (px, py) in enumerate(chips):
                landed = out[i].at[2 * px + py]
                _remote(landed, landed, send_sems.at[3 * i + r], recv_sems.at[3 * i + r], (px, py, c)).wait_recv()
        for cp in sends:
            cp.wait_send()

    return pl.pallas_call(
        body, name=name, in_specs=[ANY] * n, out_specs=[ANY] * n,
        out_shape=[jax.ShapeDtypeStruct(p.shape, p.dtype) for p in parts],
        scratch_shapes=[pltpu.SemaphoreType.DMA((3 * n,)), pltpu.SemaphoreType.DMA((3 * n,))],
    )(*parts)


def sum_over_chips(own, got, chip, name):
    n_s, n_rows, n_cols = got.shape
    tr = _rows(n_rows, n_cols * (got.dtype.itemsize * (n_s + 1) + 4))

    def body(chip_ref, own_ref, *refs):
        got_refs, o_ref = refs[:n_s], refs[n_s]
        me = chip_ref[0]
        acc = jnp.where(me == 0, own_ref[...], got_refs[0][...]).astype(F32)
        for k in range(1, n_s):
            acc = acc + jnp.where(me == k, own_ref[...], got_refs[k][...]).astype(F32)
        o_ref[...] = acc

    def slot(k):
        return pl.BlockSpec((None, tr, n_cols), lambda i, chip_ref: (jnp.where(chip_ref[0] == k, (k + 1) % n_s, k), i, 0))

    grid_spec = pltpu.PrefetchScalarGridSpec(
        num_scalar_prefetch=1, grid=(n_rows // tr,),
        in_specs=[pl.BlockSpec((None, tr, n_cols), lambda i, chip_ref: (chip_ref[0], i, 0))] + [slot(k) for k in range(n_s)],
        out_specs=pl.BlockSpec((tr, n_cols), lambda i, chip_ref: (i, 0)))
    return pl.pallas_call(
        body, name=name, grid_spec=grid_spec, out_shape=jax.ShapeDtypeStruct((n_rows, n_cols), F32),
        compiler_params=_params(("parallel",)),
    )(chip, own, *([got] * n_s))


def swap_with_sibling(arrays, name):
    n = len(arrays)

    def body(*refs):
        src, out = refs[:n], refs[n:2 * n]
        send_sems, recv_sems = refs[2 * n:]
        x, y, c, _ = _place()
        copies = [_remote(src[i], out[i], send_sems.at[i], recv_sems.at[i], (x, y, 1 - c)) for i in range(n)]
        for cp in copies:
            cp.start()
        for cp in copies:
            cp.wait()

    return pl.pallas_call(
        body, name=name, in_specs=[ANY] * n, out_specs=[ANY] * n, out_shape=[jax.ShapeDtypeStruct(a.shape, a.dtype) for a in arrays],
        scratch_shapes=[pltpu.SemaphoreType.DMA((n,)), pltpu.SemaphoreType.DMA((n,))],
    )(*arrays)


def sum_slots(q, name):
    n_s, n_rows, n_cols = q.shape
    tr = _rows(n_rows, n_cols * (q.dtype.itemsize * n_s + 4))

    def body(q_ref, o_ref):
        acc = q_ref[0].astype(F32)
        for s in range(1, n_s):
            acc = acc + q_ref[s].astype(F32)
        o_ref[...] = acc

    return pl.pallas_call(
        body, name=name, grid=(n_rows // tr,), in_specs=[pl.BlockSpec((n_s, tr, n_cols), lambda i: (0, i, 0))],
        out_specs=pl.BlockSpec((tr, n_cols), lambda i: (i, 0)), out_shape=jax.ShapeDtypeStruct((n_rows, n_cols), F32),
        compiler_params=_params(("parallel",)),
    )(q)


def gather_all_devices(s, name):
    def body(s_ref, o_ref, send_sems, recv_sems, local_sem):
        x, y, c, _ = _place()
        me = 4 * x + 2 * y + c
        local = pltpu.make_async_copy(s_ref, o_ref.at[me], local_sem)
        local.start()
        sends = []
        for f in range(1, N_DEV):
            px, py, pc = x ^ (f >> 2), y ^ ((f >> 1) & 1), c ^ (f & 1)
            sends.append(_remote(s_ref, o_ref.at[me], send_sems.at[f - 1], recv_sems.at[f - 1], (px, py, pc)))
            sends[-1].start()
        for f in range(1, N_DEV):
            px, py, pc = x ^ (f >> 2), y ^ ((f >> 1) & 1), c ^ (f & 1)
            landed = o_ref.at[4 * px + 2 * py + pc]
            _remote(landed, landed, send_sems.at[f - 1], recv_sems.at[f - 1], (px, py, pc)).wait_recv()
        for cp in sends:
            cp.wait_send()
        local.wait()

    return pl.pallas_call(
        body, name=name, in_specs=[ANY], out_specs=ANY, out_shape=jax.ShapeDtypeStruct((N_DEV,) + s.shape, s.dtype),
        scratch_shapes=[pltpu.SemaphoreType.DMA((N_DEV - 1,)), pltpu.SemaphoreType.DMA((N_DEV - 1,)), pltpu.SemaphoreType.DMA],
    )(s)


FFN_NAMES = ("ffn1_wg", "ffn1_wu", "ffn1_wd", "ffn2_wg", "ffn2_wu", "ffn2_wd")
XATTN_NAMES = ("x_wq", "x_wk", "x_wv", "x_wo")
BIG_NAMES = FFN_NAMES + XATTN_NAMES + ("mix_w_in", "mix_w_out", "pool_w")
SMALL_NAMES = ("norms", "mem_norm", "attn_sinks", "sgu_ln_g", "sgu_ln_b", "sgu_w", "sgu_b", "pool_scale")
WEIGHT_ORDER = ("norms", "mem_norm") + BIG_NAMES[:-1] + ("attn_sinks", "sgu_ln_g", "sgu_ln_b", "sgu_w", "sgu_b", "pool_w", "pool_scale")
COLUMN_CUT = ("x_wo", "mix_w_in")
N_POOL = len(POOL_WINDOWS)
BLOCK_ORDER = (("ffn1", 0), ("mix", 0), ("xattn", 0), ("ffn2", 0), ("ffn1", 1), ("mix", 1), ("xattn", 1), ("ffn2", 1))


def block_weight_names(kind, layer):
    if kind == "mix":
        return ("mix_w_in", "mix_w_out") if layer == 0 else ("pool_w",)
    return XATTN_NAMES if kind == "xattn" else tuple(f"{kind}_{part}" for part in ("wg", "wu", "wd"))


def _to_matmul_layout(name, g):
    n_j, n_rows, n_cols = g.shape
    if name in COLUMN_CUT:
        return g.transpose(1, 0, 2).reshape(n_rows, n_j * n_cols)
    if name == "pool_w":
        return g.reshape(n_j, N_POOL, n_rows // N_POOL, n_cols).transpose(1, 0, 2, 3).reshape(N_POOL, n_j * n_rows // N_POOL, n_cols)
    if name in ("x_wq", "x_wk", "x_wv", "mix_w_out"):
        return g.reshape(n_j * n_rows, n_cols)
    return g


def _from_matmul_layout(name, d):
    if name in COLUMN_CUT:
        n_rows, wide = d.shape
        return d.reshape(n_rows, N_CHIPS, wide // N_CHIPS).transpose(1, 0, 2)
    if name == "pool_w":
        n_g, n_in, n_cols = d.shape
        return d.reshape(n_g, N_CHIPS, n_in // N_CHIPS, n_cols).transpose(1, 0, 2, 3).reshape(N_CHIPS, n_g * n_in // N_CHIPS, n_cols)
    if name in ("x_wq", "x_wk", "x_wv", "mix_w_out"):
        return d.reshape(N_CHIPS, d.shape[0] // N_CHIPS, d.shape[1])
    return d


def _as3(w):
    return w.reshape(w.shape[0], -1, w.shape[-1])


def _pack(arrays, row_multiple):
    flat = jnp.concatenate([a.reshape(-1) for a in arrays])
    per = LANES * row_multiple
    total = -(-flat.shape[0] // per) * per
    return jnp.pad(flat, (0, total - flat.shape[0])).reshape(total // LANES, LANES)


def _unpack(packed, like):
    flat, out, at = packed.reshape(-1), [], 0
    for a in like:
        out.append(flat[at:at + a.size].reshape(a.shape))
        at += a.size
    return out


def kernel(x, mem, norms, mem_norm, ffn1_wg, ffn1_wu, ffn1_wd, ffn2_wg, ffn2_wu, ffn2_wd, x_wq, x_wk, x_wv, x_wo, mix_w_in, mix_w_out, attn_sinks, sgu_ln_g, sgu_ln_b, sgu_w, sgu_b, pool_w, pool_scale, loss_target, m_norms, m_mem_norm, m_ffn1_wg, m_ffn1_wu, m_ffn1_wd, m_ffn2_wg, m_ffn2_wu, m_ffn2_wd, m_x_wq, m_x_wk, m_x_wv, m_x_wo, m_mix_w_in, m_mix_w_out, m_attn_sinks, m_sgu_ln_g, m_sgu_ln_b, m_sgu_w, m_sgu_b, m_pool_w, m_pool_scale, v_norms, v_mem_norm, v_ffn1_wg, v_ffn1_wu, v_ffn1_wd, v_ffn2_wg, v_ffn2_wu, v_ffn2_wd, v_x_wq, v_x_wk, v_x_wv, v_x_wo, v_mix_w_in, v_mix_w_out, v_attn_sinks, v_sgu_ln_g, v_sgu_ln_b, v_sgu_w, v_sgu_b, v_pool_w, v_pool_scale):
    given = dict(locals())
    w = {n: given[n] for n in WEIGHT_ORDER}
    mom = {n: given["m_" + n] for n in WEIGHT_ORDER}
    var = {n: given["v_" + n] for n in WEIGHT_ORDER}
    chip_id = 2 * lax.axis_index("x") + lax.axis_index("y")
    chip = chip_id.astype(jnp.int32).reshape(1)
    core = lax.axis_index("c").astype(jnp.int32).reshape(1)
    n_shard = norms.shape[-1]

    slots, keys = [], []
    for name in BIG_NAMES:
        w3 = _as3(w[name])
        for l in range(w3.shape[0]):
            slots.append(cast_into_slot(w3, l, chip, f"cast_{name}"))
            keys.append((name, l))
    small_rows = jnp.concatenate([norms.reshape(-1, n_shard), pool_scale, jnp.zeros((15, n_shard), F32)], axis=0)
    small_slot = lax.dynamic_update_slice_in_dim(jnp.zeros((N_CHIPS,) + small_rows.shape, F32), small_rows[None], chip_id, axis=0)
    slot_of = dict(zip(keys, slots))

    pending, ready = {}, {}

    def block_slots(k):
        kind, layer = BLOCK_ORDER[k]
        return [slot_of[name, layer if kind != "mix" else 0] for name in block_weight_names(kind, layer)]

    def start(k, after):
        pending[k] = gather_start(block_slots(k), after, f"gather_start_{k}")

    def finish(k, after):
        send_sems, recv_sems, bufs, _ = pending[k]
        return forward_to_sibling(gather_wait(send_sems, recv_sems, bufs, after, f"gather_wait_{k}"), f"gather_forward_{k}")

    def fetch(key, after):
        k = BLOCK_ORDER.index(key)
        bufs = ready.pop(k) if k in ready else finish(k, after)
        token = 0.0
        if k + 1 < len(BLOCK_ORDER):
            start(k + 1, bufs[0])
            token = pending[k + 1][3][0, 0]
        return tuple(_to_matmul_layout(name, b) for name, b in zip(block_weight_names(*key), bufs)), token

    g_big = {}

    def emit(key, grads_of):
        kind, layer = key
        for name, g in grads_of.items():
            g_big[name, layer if kind != "mix" else 0] = g

    first = gather_shards(block_slots(0) + [small_slot], "gather_first")
    ready[0], small_all = first[:-1], first[-1]
    n_norm_rows = norms.shape[0] * norms.shape[1]
    norms_all = jnp.concatenate([small_all[j, :n_norm_rows].reshape(norms.shape) for j in range(N_CHIPS)], axis=-1)
    pool_scale_all = jnp.concatenate([small_all[j, n_norm_rows:n_norm_rows + 1] for j in range(N_CHIPS)], axis=-1)
    small = dict(attn_sinks=attn_sinks[0], sgu_ln_g=sgu_ln_g, sgu_ln_b=sgu_ln_b, sgu_w=sgu_w[0], sgu_b=sgu_b[0], pool_scale=pool_scale_all)

    loss_part, dx, g_small, g_norms, g_mem_norm = device_step(x[0], mem[0], loss_target[0], norms_all, mem_norm, small, fetch, emit)
    loss = lax.psum(0.5 * jnp.sum(loss_part) / x.shape[-1], ("x", "y", "c"))

    full = [_from_matmul_layout(name, g_big[name, l]) for name, l in keys]
    from_sibling = swap_other_halves(full, "grads_to_sibling")
    chip_sums = [add_own_half(g, p, core, f"chip_sum_{name}") for (name, l), g, p in zip(keys, full, from_sibling)]
    per_chip = scatter_to_chips(chip_sums, "grads_to_chips")
    own = [sum_over_chips(t, q, chip, f"sum_{name}") for (name, l), t, q in zip(keys, chip_sums, per_chip)]
    recv = swap_with_sibling(own, "grads_halves_to_sibling")
    own_of, recv_of = {name: [] for name in BIG_NAMES}, {name: [] for name in BIG_NAMES}
    for (name, l), o, r in zip(keys, own, recv):
        own_of[name].append(o)
        recv_of[name].append(r)
    grads, delta, new_m, new_v = {}, {}, {}, {}
    for name in BIG_NAMES:
        res = adamw_from_halves(_as3(w[name]), own_of[name], recv_of[name], _as3(mom[name]), _as3(var[name]), core, f"adamw_{name}")
        grads[name], delta[name], new_m[name], new_v[name] = (t.reshape(w[name].shape) for t in res)

    small_g = [g_norms, g_mem_norm, g_small["attn_sinks"], g_small["sgu_ln_g"], g_small["sgu_ln_b"], g_small["sgu_w"], g_small["sgu_b"],
               g_small["pool_scale"]]
    packed = _pack(small_g, 16)
    summed = sum_slots(gather_all_devices(packed, "small_grads_all"), "small_grads_sum")
    s_norms, s_mem, s_sinks, s_lg, s_lb, s_w, s_b, s_scale = _unpack(summed, small_g)
    grads["norms"] = lax.dynamic_slice_in_dim(s_norms, chip_id * n_shard, n_shard, axis=2)
    grads["pool_scale"] = lax.dynamic_slice_in_dim(s_scale, chip_id * n_shard, n_shard, axis=1)
    grads.update(mem_norm=s_mem, attn_sinks=s_sinks, sgu_ln_g=s_lg, sgu_ln_b=s_lb, sgu_w=s_w, sgu_b=s_b)
    like = [w[n] for n in SMALL_NAMES]
    packs = [_pack([src[n] for n in SMALL_NAMES], 16)[None] for src in (w, grads, mom, var)]
    for dst, t in zip((delta, new_m, new_v), adamw(*packs, "adamw_small")):
        for n, a in zip(SMALL_NAMES, _unpack(t[0], like)):
            dst[n] = a

    outs = [loss, dx[None]]
    for group in (grads, delta, new_m, new_v):
        outs += [group[n] for n in WEIGHT_ORDER]
    return tuple(outs)
```

```python
import functools

import jax
import jax.numpy as jnp
from jax import lax
from jax.experimental import pallas as pl
from jax.experimental.pallas import tpu as pltpu

F32 = jnp.float32
BF16 = jnp.bfloat16
MESH = pl.DeviceIdType.MESH

EPS = 1e-6
ROPE_THETA = 500000.0
ROPE_HALF = 8
HEAD_DIM = 64
N_Q_HEADS = 16
N_KV_HEADS = 2
Q_PER_KV = 8
BLOCK = 128
ATTN_WIDTH = 1024
KV_WIDTH = 128
QK_WIDTH = ATTN_WIDTH + KV_WIDTH
SGU_WIDTH = 1024
SGU_GROUPS = 8
POOL_WINDOWS = (2, 4, 8, 16)
POOL_HALO = 16
X_HEADS = 4
X_HEAD_DIM = 128
N_CHIPS = 4
N_DEV = 8

ADAM_LR = 0.001
ADAM_B1 = 0.9
ADAM_B2 = 0.999
ADAM_EPS = 1e-08
ADAM_WD = 0.01
ADAM_STEP = 10

VMEM_LIMIT_V7X = 52 * 1024 * 1024
MM_VMEM_BUDGET = 40 * 1024 * 1024
LANES = 128
ROW_TILE_BYTES = 6 * 1024 * 1024
MXU_FLOPS_V7X = 1.0e15
HBM_BYTES_PER_S_V7X = 3.0e12
VMEM_STORE_BYTES_PER_S = 4.0e12
MXU_WEIGHT_LOAD_ROWS = 192
GRID_STEP_S = 0.35e-6


def _params(sem):
    return pltpu.CompilerParams(dimension_semantics=sem, vmem_limit_bytes=VMEM_LIMIT_V7X)


def _pick(dim, pref, align):
    cands = [t for t in range(align, dim + 1, align) if dim % t == 0]
    small = [t for t in cands if t <= pref]
    if small and small[-1] * 2 >= min(pref, dim):
        return small[-1]
    return dim


def _rows(n_rows, bytes_per_row):
    want = max(16, min(1024, ROW_TILE_BYTES // max(1, bytes_per_row)))
    cands = [t for t in range(16, n_rows + 1, 16) if n_rows % t == 0 and t <= want]
    return cands[-1] if cands else n_rows


def _divisors(dim, align, most):
    return [t for t in range(align, min(dim, most) + 1, align) if dim % t == 0] or [dim]


def _mm_tiles(M, N, K, J, m_align, k_align, a_bytes, b_bytes, o_bytes, reduce, ta):
    best = None
    for tm in _divisors(M, m_align, 2048):
        for tn in _divisors(N, LANES, 2048):
            for tk in _divisors(K, k_align, 4096):
                split = K // tk > 1 or reduce
                vmem = 2 * (tm * tk * a_bytes + tk * tn * b_bytes + tm * tn * o_bytes) + tm * tn * 4 * (2 if split else 1)
                if ta:
                    vmem += tm * tk * a_bytes
                if vmem > MM_VMEM_BUDGET:
                    continue
                steps = J * (M // tm) * (N // tn) * (K // tk)
                mxu = 2.0 * J * M * N * K / MXU_FLOPS_V7X * (tm + MXU_WEIGHT_LOAD_ROWS) / tm
                acc = J * M * N * (K // tk) * 8 / VMEM_STORE_BYTES_PER_S if split else 0.0
                hbm = J * (M * K * a_bytes * (N // tn) + K * N * b_bytes * (M // tm) + M * N * o_bytes) / HBM_BYTES_PER_S_V7X
                cost = max(mxu + 0.5 * acc, hbm) + steps * GRID_STEP_S
                if best is None or cost < best[0]:
                    best = (cost, tm, tn, tk)
    return best[1:]


def _rsum8(v):
    r, c = v.shape
    return v.reshape(r // 8, 8, c).sum(axis=0)


def _mm(a, b, *, name, ta=False, tb=False, batch="none", groups=0, a_cb=False, b_cb=False, o_cb=False,
        out_dtype=F32):
    J = groups or (a.shape[0] if a.ndim == 3 else (b.shape[0] if b.ndim == 3 else 1))
    a2, b2 = a.shape[-2:], b.shape[-2:]
    M, K = (a2[1], a2[0]) if ta else a2
    N, Kb = b2 if tb else (b2[1], b2[0])
    if a_cb:
        if ta:
            M //= J
        else:
            K //= J
    if b_cb:
        if tb:
            Kb //= J
        else:
            N //= J
    assert K == Kb, (name, a.shape, b.shape)
    reduce = batch == "reduce"
    tm, tn, tk = _mm_tiles(M, N, K, J, LANES if ta else 16, LANES if (not ta or tb) else 16, a.dtype.itemsize, b.dtype.itemsize,
                           jnp.dtype(out_dtype).itemsize, reduce, ta)
    nm, nn, nk = M // tm, N // tn, K // tk
    if reduce:
        grid = (nm, nn, J, nk)
        unpack = lambda m, n, j, k: (j, m, n, k)
        sem = ("parallel", "parallel", "arbitrary", "arbitrary")
    else:
        grid = (J, nm, nn, nk)
        unpack = lambda j, m, n, k: (j, m, n, k)
        sem = ("parallel", "parallel", "parallel", "arbitrary")

    def a_map(*g):
        j, m, n, k = unpack(*g)
        r, c = (k, m) if ta else (m, k)
        if a_cb:
            c = c + j * (nm if ta else nk)
        return (j, r, c) if a.ndim == 3 else (r, c)

    def b_map(*g):
        j, m, n, k = unpack(*g)
        r, c = (n, k) if tb else (k, n)
        if b_cb:
            c = c + j * (nk if tb else nn)
        return (j, r, c) if b.ndim == 3 else (r, c)

    def o_map(*g):
        j, m, n, k = unpack(*g)
        if o_cb:
            return (m, n + j * nn)
        return (j, m, n) if batch == "map" else (m, n)

    a_blk = (tk, tm) if ta else (tm, tk)
    b_blk = (tn, tk) if tb else (tk, tn)
    a_spec = pl.BlockSpec(((None,) + a_blk) if a.ndim == 3 else a_blk, a_map)
    b_spec = pl.BlockSpec(((None,) + b_blk) if b.ndim == 3 else b_blk, b_map)
    if o_cb:
        out_shape, o_blk = (M, N * J), (tm, tn)
    elif batch == "map":
        out_shape, o_blk = (J, M, N), (None, tm, tn)
    else:
        out_shape, o_blk = (M, N), (tm, tn)
    dims = (((0 if ta else 1,), (1 if tb else 0,)), ((), ()))
    red_axes = (2, 3) if reduce else (3,)

    split = reduce or nk > 1

    def body(a_ref, b_ref, o_ref, *acc):
        prod = lax.dot_general(a_ref[...], b_ref[...], dims, preferred_element_type=F32)
        if not split:
            o_ref[...] = prod.astype(o_ref.dtype)
            return
        acc_ref, = acc
        first = functools.reduce(jnp.logical_and, [pl.program_id(ax) == 0 for ax in red_axes])
        last = functools.reduce(jnp.logical_and, [pl.program_id(ax) == grid[ax] - 1 for ax in red_axes])

        @pl.when(first)
        def _():
            acc_ref[...] = prod

        @pl.when(jnp.logical_not(first))
        def _():
            acc_ref[...] += prod

        @pl.when(last)
        def _():
            o_ref[...] = acc_ref[...].astype(o_ref.dtype)

    return pl.pallas_call(
        body, name=name, grid=grid, in_specs=[a_spec, b_spec], out_specs=pl.BlockSpec(o_blk, o_map),
        out_shape=jax.ShapeDtypeStruct(out_shape, out_dtype), scratch_shapes=[pltpu.VMEM((tm, tn), F32)] if split else [],
        compiler_params=_params(sem),
    )(a, b)


def _rowwise(fn, tiled, whole, outs, accs, *, name):
    n_rows = tiled[0].shape[0]
    row_bytes = sum(t.shape[1] * t.dtype.itemsize for t in tiled) + sum(c * jnp.dtype(d).itemsize for c, d in outs)
    tr = _rows(n_rows, row_bytes)
    n_t, n_w, n_o = len(tiled), len(whole), len(outs)

    def body(*refs):
        i = pl.program_id(0)
        t_refs, w_refs = refs[:n_t], refs[n_t:n_t + n_w]
        o_refs, a_refs = refs[n_t + n_w:n_t + n_w + n_o], refs[n_t + n_w + n_o:]
        o_vals, a_vals = fn(i, *[r[...] for r in t_refs], *[r[...] for r in w_refs])
        for r, v in zip(o_refs, o_vals):
            r[...] = v.astype(r.dtype)
        if a_refs:
            @pl.when(i == 0)
            def _():
                for r in a_refs:
                    r[...] = jnp.zeros_like(r)
            for r, v in zip(a_refs, a_vals):
                r[...] += v

    in_specs = [pl.BlockSpec((tr, t.shape[1]), lambda i: (i, 0)) for t in tiled]
    in_specs += [pl.BlockSpec(w.shape, lambda i, nd=w.ndim: (0,) * nd) for w in whole]
    out_specs = [pl.BlockSpec((tr, c), lambda i: (i, 0)) for c, _ in outs]
    out_specs += [pl.BlockSpec(s, lambda i, nd=len(s): (0,) * nd) for s, _ in accs]
    out_shape = [jax.ShapeDtypeStruct((n_rows, c), d) for c, d in outs]
    out_shape += [jax.ShapeDtypeStruct(s, d) for s, d in accs]
    res = pl.pallas_call(
        body, name=name, grid=(n_rows // tr,), in_specs=in_specs, out_specs=out_specs, out_shape=out_shape,
        compiler_params=_params(("arbitrary",) if accs else ("parallel",)),
    )(*tiled, *whole)
    return res


def _rms_stats(x):
    r = lax.rsqrt(jnp.mean(x * x, axis=-1, keepdims=True) + EPS)
    return x * r, r


def rms_fwd(x, g, out_dtype, name):
    def fn(i, x, g):
        xhat, _ = _rms_stats(x)
        return [xhat * g], []
    return _rowwise(fn, [x], [g], [(x.shape[1], out_dtype)], [], name=name)[0]


def postnorm_res(x, y, g, s, name):
    def fn(i, x, y, g):
        yhat, _ = _rms_stats(y)
        return [x + s * (yhat * g)], []
    return _rowwise(fn, [x, y], [g], [(x.shape[1], F32)], [], name=name)[0]


def rms_bwd(xin, g, douts, scale, add, out_dtype, name):
    n_d = len(douts)

    def fn(i, x, *rest):
        ds, rest = rest[:n_d], rest[n_d:]
        ad = rest[0] if add is not None else None
        g = rest[-1]
        xhat, r = _rms_stats(x)
        d = ds[0].astype(F32)
        for e in ds[1:]:
            d = d + e.astype(F32)
        if scale != 1.0:
            d = d * scale
        dg = _rsum8(d * xhat)
        dxhat = d * g
        dx = r * (dxhat - xhat * jnp.mean(dxhat * xhat, axis=-1, keepdims=True))
        if ad is not None:
            dx = dx + ad
        return [dx], [dg]

    tiled = [xin, *douts] + ([add] if add is not None else [])
    dx, dg = _rowwise(fn, tiled, [g], [(xin.shape[1], out_dtype)], [((8, xin.shape[1]), F32)], name=name)
    return dx, dg


def _silu_parts(g):
    sg = 1.0 / (1.0 + jnp.exp(-g))
    return g * sg, sg


def swiglu_fwd(G, U, name):
    shp = G.shape

    def fn(i, g, u):
        s, _ = _silu_parts(g.astype(F32))
        return [s * u.astype(F32)], []
    a = _rowwise(fn, [G.reshape(-1, shp[-1]), U.reshape(-1, shp[-1])], [], [(shp[-1], BF16)], [], name=name)[0]
    return a.reshape(shp)


def swiglu_bwd(G, U, dA, name):
    shp = G.shape

    def fn(i, g, u, da):
        g, u, da = g.astype(F32), u.astype(F32), da.astype(F32)
        s, sg = _silu_parts(g)
        dsilu = sg * (1.0 + g * (1.0 - sg))
        return [da * u * dsilu, da * s], []
    flat = lambda t: t.reshape(-1, shp[-1])
    dG, dU = _rowwise(fn, [flat(G), flat(U), flat(dA)], [], [(shp[-1], BF16), (shp[-1], BF16)], [], name=name)
    return dG.reshape(shp), dU.reshape(shp)


def scale_cols(y, s, name):
    def fn(i, y, s):
        return [y * s], []
    return _rowwise(fn, [y], [s], [(y.shape[1], F32)], [], name=name)[0]


def scale_cols_bwd(dm, y, s, name):
    def fn(i, dm, y, s):
        return [dm * s], [_rsum8(dm * y)]
    return _rowwise(fn, [dm, y], [s], [(y.shape[1], BF16)], [((8, y.shape[1]), F32)], name=name)


def loss_and_grad(y, target, name):
    n_feat = y.shape[1]

    def fn(i, y, t):
        e = y - t
        return [e * (1.0 / n_feat)], [_rsum8(e * e)]
    dy, part = _rowwise(fn, [y, target], [], [(n_feat, F32)], [((8, n_feat), F32)], name=name)
    return part, dy


def cast_into_slot(w3, layer, chip, name):
    _, n_rows, n_cols = w3.shape
    tr = _rows(n_rows, n_cols * 6)

    def body(chip_ref, w_ref, o_ref):
        o_ref[...] = w_ref[...].astype(BF16)

    grid_spec = pltpu.PrefetchScalarGridSpec(
        num_scalar_prefetch=1, grid=(n_rows // tr,),
        in_specs=[pl.BlockSpec((None, tr, n_cols), lambda i, chip_ref: (layer, i, 0))],
        out_specs=pl.BlockSpec((None, tr, n_cols), lambda i, chip_ref: (chip_ref[0], i, 0)))
    return pl.pallas_call(
        body, name=name, grid_spec=grid_spec, out_shape=jax.ShapeDtypeStruct((N_CHIPS, n_rows, n_cols), BF16),
        compiler_params=_params(("parallel",)),
    )(chip, w3)


def _adam_update(w, g, m, v):
    c1 = 1.0 / (1.0 - ADAM_B1 ** ADAM_STEP)
    c2 = 1.0 / (1.0 - ADAM_B2 ** ADAM_STEP)
    m = ADAM_B1 * m + (1.0 - ADAM_B1) * g
    v = ADAM_B2 * v + (1.0 - ADAM_B2) * (g * g)
    return -ADAM_LR * ((m * c1) / (jnp.sqrt(v * c2) + ADAM_EPS) + ADAM_WD * w), m, v


def adamw(w, g, m, v, name):
    n_l, n_rows, n_cols = w.shape
    tr = _rows(n_rows, n_cols * 4 * 7)

    def body(w_ref, g_ref, m_ref, v_ref, d_ref, mo_ref, vo_ref):
        d_ref[...], mo_ref[...], vo_ref[...] = _adam_update(w_ref[...], g_ref[...], m_ref[...], v_ref[...])

    spec = pl.BlockSpec((None, tr, n_cols), lambda l, i: (l, i, 0))
    shp = jax.ShapeDtypeStruct(w.shape, F32)
    return pl.pallas_call(
        body, name=name, grid=(n_l, n_rows // tr), in_specs=[spec] * 4, out_specs=[spec] * 3, out_shape=[shp] * 3,
        compiler_params=_params(("parallel", "parallel")),
    )(w, g, m, v)


def adamw_from_halves(w, own, recv, m, v, core, name):
    n_l, n_rows, n_cols = w.shape
    half = n_rows // 2
    tr = _rows(half, n_cols * 4 * 9)
    per = half // tr

    def body(core_ref, *refs):
        w_ref, m_ref, v_ref = refs[:3]
        own_refs, recv_refs = refs[3:3 + n_l], refs[3 + n_l:3 + 2 * n_l]
        g_ref, d_ref, mo_ref, vo_ref = refs[3 + 2 * n_l:]
        l, h = pl.program_id(0), pl.program_id(1)
        mine = h == core_ref[0]
        g = jnp.where(mine, own_refs[0][...], recv_refs[0][...])
        for k in range(1, n_l):
            g = jnp.where(l == k, jnp.where(mine, own_refs[k][...], recv_refs[k][...]), g)
        g_ref[...] = g
        d_ref[...], mo_ref[...], vo_ref[...] = _adam_update(w_ref[...], g, m_ref[...], v_ref[...])

    full = pl.BlockSpec((None, tr, n_cols), lambda l, h, i, core_ref: (l, h * per + i, 0))

    def piece(layer, is_own):
        def index(l, h, i, core_ref):
            used = (l == layer) & ((h == core_ref[0]) == is_own)
            return (jnp.where(used, i, 0), 0)
        return pl.BlockSpec((tr, n_cols), index)

    grid_spec = pltpu.PrefetchScalarGridSpec(
        num_scalar_prefetch=1, grid=(n_l, 2, per),
        in_specs=[full] * 3 + [piece(k, True) for k in range(n_l)] + [piece(k, False) for k in range(n_l)],
        out_specs=[full] * 4)
    return pl.pallas_call(
        body, name=name, grid_spec=grid_spec, out_shape=[jax.ShapeDtypeStruct(w.shape, F32)] * 4,
        compiler_params=_params(("parallel", "parallel", "parallel")),
    )(core, w, m, v, *own, *recv)


def rope_tables(n_tok):
    inv = ROPE_THETA ** (-jnp.arange(ROPE_HALF, dtype=F32) * 2.0 / (2 * ROPE_HALF))
    ang = jnp.arange(n_tok, dtype=F32)[:, None] * inv[None, :]
    cos, sin = jnp.cos(ang), jnp.sin(ang)
    rest = HEAD_DIM - 2 * ROPE_HALF
    one, zero, z8 = jnp.ones((n_tok, rest), F32), jnp.zeros((n_tok, rest), F32), jnp.zeros((n_tok, ROPE_HALF), F32)
    c = jnp.concatenate([cos, cos, one], axis=1)
    s1 = jnp.concatenate([-sin, z8, zero], axis=1)
    s2 = jnp.concatenate([z8, sin, zero], axis=1)
    two = lambda t: jnp.concatenate([t, t], axis=1)
    return two(c), two(s1), two(s2)


def rope_apply(x, tabs, n_col_blocks, inverse, out_dtype, name):
    n_tok = x.shape[0]
    tr = _rows(n_tok, LANES * 4 * 6)

    def body(x_ref, c_ref, s1_ref, s2_ref, o_ref):
        x = x_ref[...].astype(F32)
        if inverse:
            out = x * c_ref[...] + pltpu.roll(x * s1_ref[...], ROPE_HALF, 1) + pltpu.roll(x * s2_ref[...], LANES - ROPE_HALF, 1)
        else:
            out = x * c_ref[...] + pltpu.roll(x, LANES - ROPE_HALF, 1) * s1_ref[...] + pltpu.roll(x, ROPE_HALF, 1) * s2_ref[...]
        o_ref[...] = out.astype(o_ref.dtype)

    tab_spec = pl.BlockSpec((tr, LANES), lambda i, c: (i, 0))
    blk = pl.BlockSpec((tr, LANES), lambda i, c: (i, c))
    return pl.pallas_call(
        body, name=name, grid=(n_tok // tr, n_col_blocks), in_specs=[blk, tab_spec, tab_spec, tab_spec], out_specs=blk,
        out_shape=jax.ShapeDtypeStruct((n_tok, n_col_blocks * LANES), out_dtype), compiler_params=_params(("parallel", "parallel")),
    )(x, *tabs)


def _swa_probs(q, k, sink, n):
    rows = Q_PER_KV * BLOCK
    s = lax.dot_general(q, k, (((1,), (1,)), ((), ())), preferred_element_type=F32) * (HEAD_DIM ** -0.5)
    qi = lax.broadcasted_iota(jnp.int32, (rows, 2 * BLOCK), 0) & (BLOCK - 1)
    kj = lax.broadcasted_iota(jnp.int32, (rows, 2 * BLOCK), 1)
    rel = qi + BLOCK - kj
    valid = (rel >= 0) & (rel < BLOCK) & ((n > 0) | (kj >= BLOCK))
    s = jnp.where(valid, s, -1e30)
    m = jnp.maximum(jnp.max(s, axis=-1, keepdims=True), sink)
    e = jnp.exp(s - m)
    es = jnp.exp(sink - m)
    inv = 1.0 / (jnp.sum(e, axis=-1, keepdims=True) + es)
    return e * inv, es * inv


def _swa_specs(n_blocks):
    q_spec = pl.BlockSpec((Q_PER_KV, BLOCK, HEAD_DIM), lambda h, n: (h, n, 0))
    prev = pl.BlockSpec((None, BLOCK, HEAD_DIM), lambda h, n: (h, jnp.maximum(n - 1, 0), 0))
    cur = pl.BlockSpec((None, BLOCK, HEAD_DIM), lambda h, n: (h, n, 0))
    sink = pl.BlockSpec((None, Q_PER_KV * BLOCK, 1), lambda h, n: (h, 0, 0))
    return q_spec, prev, cur, sink


def swa_fwd(q, k, v, sink_rows, name):
    n_tok = q.shape[1]
    q_spec, prev, cur, sink = _swa_specs(n_tok // BLOCK)

    def body(q_ref, kp_ref, kc_ref, vp_ref, vc_ref, s_ref, o_ref):
        n = pl.program_id(1)
        qq = q_ref[...].reshape(Q_PER_KV * BLOCK, HEAD_DIM)
        kk = jnp.concatenate([kp_ref[...], kc_ref[...]], axis=0)
        vv = jnp.concatenate([vp_ref[...], vc_ref[...]], axis=0)
        p, _ = _swa_probs(qq, kk, s_ref[...], n)
        o = jnp.dot(p.astype(BF16), vv, preferred_element_type=F32)
        o_ref[...] = o.reshape(Q_PER_KV, BLOCK, HEAD_DIM).astype(o_ref.dtype)

    return pl.pallas_call(
        body, name=name, grid=(N_KV_HEADS, n_tok // BLOCK), in_specs=[q_spec, prev, cur, prev, cur, sink], out_specs=q_spec,
        out_shape=jax.ShapeDtypeStruct(q.shape, BF16), compiler_params=_params(("parallel", "parallel")),
    )(q, k, k, v, v, sink_rows)


def swa_bwd(q, k, v, sink_rows, do, name):
    n_tok = q.shape[1]
    nb = n_tok // BLOCK
    q_spec, prev, cur, sink = _swa_specs(nb)
    rows = Q_PER_KV * BLOCK

    def body(q_ref, kp_ref, kc_ref, vp_ref, vc_ref, s_ref, do_ref, dq_ref, dkp_ref, dkc_ref, dvp_ref, dvc_ref, ds_ref):
        n = pl.program_id(1)
        qq = q_ref[...].reshape(rows, HEAD_DIM)
        dd = do_ref[...].reshape(rows, HEAD_DIM)
        kk = jnp.concatenate([kp_ref[...], kc_ref[...]], axis=0)
        vv = jnp.concatenate([vp_ref[...], vc_ref[...]], axis=0)
        p, ps = _swa_probs(qq, kk, s_ref[...], n)
        dp = lax.dot_general(dd, vv, (((1,), (1,)), ((), ())), preferred_element_type=F32)
        delta = jnp.sum(p * dp, axis=-1, keepdims=True)
        ds = (p * (dp - delta) * (HEAD_DIM ** -0.5)).astype(BF16)
        dq = jnp.dot(ds, kk, preferred_element_type=F32)
        dk = lax.dot_general(ds, qq, (((0,), (0,)), ((), ())), preferred_element_type=F32)
        dv = lax.dot_general(p.astype(BF16), dd, (((0,), (0,)), ((), ())), preferred_element_type=F32)
        dq_ref[...] = dq.reshape(Q_PER_KV, BLOCK, HEAD_DIM).astype(dq_ref.dtype)
        dkp_ref[...] = dk[:BLOCK]
        dkc_ref[...] = dk[BLOCK:]
        dvp_ref[...] = dv[:BLOCK]
        dvc_ref[...] = dv[BLOCK:]
        dsink = jnp.broadcast_to(-ps * delta, (rows, LANES)).reshape(Q_PER_KV, BLOCK, LANES)
        ds_ref[...] = jnp.sum(dsink, axis=1)

    part = pl.BlockSpec((None, None, BLOCK, HEAD_DIM), lambda h, n: (h, n, 0, 0))
    part_shape = jax.ShapeDtypeStruct((N_KV_HEADS, nb, BLOCK, HEAD_DIM), F32)
    return pl.pallas_call(
        body, name=name, grid=(N_KV_HEADS, nb), in_specs=[q_spec, prev, cur, prev, cur, sink, q_spec],
        out_specs=[q_spec, part, part, part, part, pl.BlockSpec((None, None, Q_PER_KV, LANES), lambda h, n: (h, n, 0, 0))],
        out_shape=[jax.ShapeDtypeStruct(q.shape, BF16), part_shape, part_shape, part_shape, part_shape,
                   jax.ShapeDtypeStruct((N_KV_HEADS, nb, Q_PER_KV, LANES), F32)],
        compiler_params=_params(("parallel", "parallel")),
    )(q, k, k, v, v, sink_rows, do)


def _to_heads(t, n_heads):
    return t.reshape(t.shape[0], n_heads, HEAD_DIM).transpose(1, 0, 2)


def _from_heads(t):
    return t.transpose(1, 0, 2).reshape(t.shape[1], -1)


def _fold_kv_grad(prev, cur):
    shifted = jnp.concatenate([prev[:, 1:], jnp.zeros_like(prev[:, :1])], axis=1)
    tot = (cur + shifted).reshape(N_KV_HEADS, -1, HEAD_DIM)
    return _from_heads(tot)


def _x_probs(qh, kh):
    s = lax.dot_general(qh, kh, (((1,), (1,)), ((), ())), preferred_element_type=F32) * (X_HEAD_DIM ** -0.5)
    e = jnp.exp(s - jnp.max(s, axis=-1, keepdims=True))
    return e * (1.0 / jnp.sum(e, axis=-1, keepdims=True))


def xattn_fwd(q, k, v, name):
    n_tok, width = q.shape
    n_mem = k.shape[0]
    tq = _pick(n_tok, 512, 16)

    def body(q_ref, k_ref, v_ref, o_ref):
        for h in range(X_HEADS):
            cols = slice(h * X_HEAD_DIM, (h + 1) * X_HEAD_DIM)
            p = _x_probs(q_ref[:, cols], k_ref[:, cols])
            o_ref[:, cols] = jnp.dot(p.astype(BF16), v_ref[:, cols], preferred_element_type=F32).astype(o_ref.dtype)

    row = pl.BlockSpec((tq, width), lambda i: (i, 0))
    mem = pl.BlockSpec((n_mem, width), lambda i: (0, 0))
    return pl.pallas_call(
        body, name=name, grid=(n_tok // tq,), in_specs=[row, mem, mem], out_specs=row,
        out_shape=jax.ShapeDtypeStruct(q.shape, BF16), compiler_params=_params(("parallel",)),
    )(q, k, v)


def xattn_bwd(q, k, v, do, name):
    n_tok, width = q.shape
    n_mem = k.shape[0]
    tq = _pick(n_tok, 512, 16)

    def body(q_ref, k_ref, v_ref, do_ref, dq_ref, dk_ref, dv_ref):
        @pl.when(pl.program_id(0) == 0)
        def _():
            dk_ref[...] = jnp.zeros_like(dk_ref)
            dv_ref[...] = jnp.zeros_like(dv_ref)

        for h in range(X_HEADS):
            cols = slice(h * X_HEAD_DIM, (h + 1) * X_HEAD_DIM)
            qh, kh, vh, dh = q_ref[:, cols], k_ref[:, cols], v_ref[:, cols], do_ref[:, cols]
            p = _x_probs(qh, kh)
            dp = lax.dot_general(dh, vh, (((1,), (1,)), ((), ())), preferred_element_type=F32)
            delta = jnp.sum(p * dp, axis=-1, keepdims=True)
            ds = (p * (dp - delta) * (X_HEAD_DIM ** -0.5)).astype(BF16)
            dq_ref[:, cols] = jnp.dot(ds, kh, preferred_element_type=F32).astype(dq_ref.dtype)
            dk_ref[:, cols] += lax.dot_general(ds, qh, (((0,), (0,)), ((), ())), preferred_element_type=F32)
            dv_ref[:, cols] += lax.dot_general(p.astype(BF16), dh, (((0,), (0,)), ((), ())), preferred_element_type=F32)

    row = pl.BlockSpec((tq, width), lambda i: (i, 0))
    mem = pl.BlockSpec((n_mem, width), lambda i: (0, 0))
    return pl.pallas_call(
        body, name=name, grid=(n_tok // tq,), in_specs=[row, mem, mem, row], out_specs=[row, mem, mem],
        out_shape=[jax.ShapeDtypeStruct(q.shape, BF16), jax.ShapeDtypeStruct(k.shape, F32), jax.ShapeDtypeStruct(k.shape, F32)],
        compiler_params=_params(("arbitrary",)),
    )(q, k, v, do)


GELU_C = 0.7978845608028654
GELU_A = 0.044715


def _gelu_parts(x):
    x2 = x * x
    t = jnp.tanh(GELU_C * x * (1.0 + GELU_A * x2))
    y = 0.5 * x * (1.0 + t)
    dy = 0.5 * (1.0 + t) + 0.5 * x * (1.0 - t * t) * GELU_C * (1.0 + 3.0 * GELU_A * x2)
    return y, dy


def _sgu_norm(v, ln_g, ln_b):
    mu = jnp.mean(v, axis=-1, keepdims=True)
    vc = v - mu
    r = lax.rsqrt(jnp.mean(vc * vc, axis=-1, keepdims=True) + EPS)
    xhat = vc * r
    return xhat * ln_g + ln_b, xhat, r


def _causal_weights(w_ref):
    i = lax.broadcasted_iota(jnp.int32, (BLOCK, BLOCK), 0)
    j = lax.broadcasted_iota(jnp.int32, (BLOCK, BLOCK), 1)
    return [jnp.where(i >= j, w_ref[g], 0.0).astype(BF16) for g in range(SGU_GROUPS)]


def sgu_fwd(u_pre, v_pre, ln_g, ln_b, w_s, bias_rows, name):
    n_tok = u_pre.shape[0]
    tm = _pick(n_tok, 512, BLOCK)

    def body(u_ref, v_ref, g_ref, b_ref, w_ref, bb_ref, o_ref):
        vn, _, _ = _sgu_norm(_gelu_parts(v_ref[...])[0], g_ref[...], b_ref[...])
        vn = vn.astype(BF16)
        wc = _causal_weights(w_ref)
        for c in range(tm // BLOCK):
            rows = slice(c * BLOCK, (c + 1) * BLOCK)
            for g in range(SGU_GROUPS):
                cols = slice(g * LANES, (g + 1) * LANES)
                mixed = jnp.dot(wc[g], vn[rows, cols], preferred_element_type=F32) + bb_ref[g]
                u = _gelu_parts(u_ref[rows, cols])[0]
                o_ref[rows, cols] = (u * mixed).astype(o_ref.dtype)

    row = pl.BlockSpec((tm, SGU_WIDTH), lambda i: (i, 0))
    vec = pl.BlockSpec((1, SGU_WIDTH), lambda i: (0, 0))
    mat = pl.BlockSpec((SGU_GROUPS, BLOCK, LANES), lambda i: (0, 0, 0))
    return pl.pallas_call(
        body, name=name, grid=(n_tok // tm,), in_specs=[row, row, vec, vec, mat, mat], out_specs=row,
        out_shape=jax.ShapeDtypeStruct((n_tok, SGU_WIDTH), BF16), compiler_params=_params(("parallel",)),
    )(u_pre, v_pre, ln_g, ln_b, w_s, bias_rows)


def sgu_bwd(u_pre, v_pre, ln_g, ln_b, w_s, bias_rows, dgate, name):
    n_tok = u_pre.shape[0]
    tm = _pick(n_tok, 512, BLOCK)

    def body(u_ref, v_ref, g_ref, b_ref, w_ref, bb_ref, dg_ref, du_ref, dv_ref, dw_ref, db_ref, dlg_ref, dlb_ref, dvn_ref):
        @pl.when(pl.program_id(0) == 0)
        def _():
            dw_ref[...] = jnp.zeros_like(dw_ref)
            db_ref[...] = jnp.zeros_like(db_ref)
            dlg_ref[...] = jnp.zeros_like(dlg_ref)
            dlb_ref[...] = jnp.zeros_like(dlb_ref)

        gv, dgv = _gelu_parts(v_ref[...])
        vn, xhat, r = _sgu_norm(gv, g_ref[...], b_ref[...])
        vn = vn.astype(BF16)
        wc = _causal_weights(w_ref)
        for c in range(tm // BLOCK):
            rows = slice(c * BLOCK, (c + 1) * BLOCK)
            for g in range(SGU_GROUPS):
                cols = slice(g * LANES, (g + 1) * LANES)
                vt = vn[rows, cols]
                mixed = jnp.dot(wc[g], vt, preferred_element_type=F32) + bb_ref[g]
                u, du_dpre = _gelu_parts(u_ref[rows, cols])
                dgate_t = dg_ref[rows, cols].astype(F32)
                du_ref[rows, cols] = (dgate_t * mixed * du_dpre).astype(du_ref.dtype)
                dmix = dgate_t * u
                dmix_b = dmix.astype(BF16)
                db_ref[g] += dmix
                dw_ref[g] += lax.dot_general(dmix_b, vt, (((1,), (1,)), ((), ())), preferred_element_type=F32)
                dvn_ref[rows, cols] = lax.dot_general(wc[g], dmix_b, (((0,), (0,)), ((), ())), preferred_element_type=F32)
        dvn = dvn_ref[...]
        dlg_ref[...] += _rsum8(dvn * xhat)
        dlb_ref[...] += _rsum8(dvn)
        dxhat = dvn * g_ref[...]
        dgv_in = r * (dxhat - jnp.mean(dxhat, axis=-1, keepdims=True) - xhat * jnp.mean(dxhat * xhat, axis=-1, keepdims=True))
        dv_ref[...] = (dgv_in * dgv).astype(dv_ref.dtype)

    row = pl.BlockSpec((tm, SGU_WIDTH), lambda i: (i, 0))
    vec = pl.BlockSpec((1, SGU_WIDTH), lambda i: (0, 0))
    mat = pl.BlockSpec((SGU_GROUPS, BLOCK, LANES), lambda i: (0, 0, 0))
    part = pl.BlockSpec((8, SGU_WIDTH), lambda i: (0, 0))
    mat_shape = jax.ShapeDtypeStruct((SGU_GROUPS, BLOCK, LANES), F32)
    part_shape = jax.ShapeDtypeStruct((8, SGU_WIDTH), F32)
    act_shape = jax.ShapeDtypeStruct((n_tok, SGU_WIDTH), BF16)
    return pl.pallas_call(
        body, name=name, grid=(n_tok // tm,), in_specs=[row, row, vec, vec, mat, mat, row],
        out_specs=[row, row, mat, mat, part, part], out_shape=[act_shape, act_shape, mat_shape, mat_shape, part_shape, part_shape],
        scratch_shapes=[pltpu.VMEM((tm, SGU_WIDTH), F32)], compiler_params=_params(("arbitrary",)),
    )(u_pre, v_pre, ln_g, ln_b, w_s, bias_rows, dgate)


def _pool_tile(n_tok):
    return _pick(n_tok, 256, POOL_HALO)


def pool_fwd(h, name):
    n_tok, width = h.shape
    gw = width // len(POOL_WINDOWS)
    tm = _pool_tile(n_tok)
    per = tm // POOL_HALO

    def body(cur_ref, halo_ref, o_ref, buf_ref):
        i = pl.program_id(0)
        buf_ref[0:POOL_HALO, :] = jnp.where(i > 0, halo_ref[...], 0.0)
        buf_ref[POOL_HALO:, :] = cur_ref[...]
        tok = i * tm + lax.broadcasted_iota(jnp.int32, (tm, 1), 0)
        for g, w in enumerate(POOL_WINDOWS):
            cols = slice(g * gw, (g + 1) * gw)
            acc = buf_ref[POOL_HALO:, cols]
            for j in range(1, w):
                acc = acc + buf_ref[POOL_HALO - j:POOL_HALO - j + tm, cols]
            cnt = jnp.minimum(tok + 1, w).astype(F32)
            o_ref[:, cols] = (acc / cnt - cur_ref[:, cols]).astype(o_ref.dtype)

    return pl.pallas_call(
        body, name=name, grid=(n_tok // tm,),
        in_specs=[pl.BlockSpec((tm, width), lambda i: (i, 0)),
                  pl.BlockSpec((POOL_HALO, width), lambda i: (jnp.maximum(i * per - 1, 0), 0))],
        out_specs=pl.BlockSpec((tm, width), lambda i: (i, 0)), out_shape=jax.ShapeDtypeStruct(h.shape, BF16),
        scratch_shapes=[pltpu.VMEM((tm + POOL_HALO, width), F32)], compiler_params=_params(("parallel",)),
    )(h, h)


def pool_bwd(dp, name):
    n_tok, width = dp.shape
    gw = width // len(POOL_WINDOWS)
    tm = _pool_tile(n_tok)
    per = tm // POOL_HALO
    n_steps = n_tok // tm

    def body(cur_ref, halo_ref, o_ref, buf_ref):
        i = pl.program_id(0)
        tok = i * tm + lax.broadcasted_iota(jnp.int32, (tm, 1), 0)
        for g, w in enumerate(POOL_WINDOWS):
            cols = slice(g * gw, (g + 1) * gw)
            cnt = jnp.minimum(tok + 1, w).astype(F32)
            buf_ref[0:tm, cols] = cur_ref[:, cols] / cnt
            buf_ref[tm:, cols] = jnp.where(i < n_steps - 1, halo_ref[:, cols] / float(w), 0.0)
        for g, w in enumerate(POOL_WINDOWS):
            cols = slice(g * gw, (g + 1) * gw)
            acc = buf_ref[0:tm, cols]
            for j in range(1, w):
                acc = acc + buf_ref[j:j + tm, cols]
            o_ref[:, cols] = acc - cur_ref[:, cols]

    return pl.pallas_call(
        body, name=name, grid=(n_steps,),
        in_specs=[pl.BlockSpec((tm, width), lambda i: (i, 0)),
                  pl.BlockSpec((POOL_HALO, width), lambda i: (jnp.minimum((i + 1) * per, n_tok // POOL_HALO - 1), 0))],
        out_specs=pl.BlockSpec((tm, width), lambda i: (i, 0)), out_shape=jax.ShapeDtypeStruct(dp.shape, F32),
        scratch_shapes=[pltpu.VMEM((tm + POOL_HALO, width), F32)], compiler_params=_params(("parallel",)),
    )(dp, dp)


def _ffn_fwd(x, ga, gb, wg, wu, wd, tag):
    h = rms_fwd(x, ga, BF16, f"{tag}_prenorm")
    G = _mm(h, wg, batch="map", out_dtype=BF16, name=f"{tag}_gate")
    U = _mm(h, wu, batch="map", out_dtype=BF16, name=f"{tag}_up")
    A = swiglu_fwd(G, U, f"{tag}_swiglu")
    y = _mm(A, wd, batch="reduce", name=f"{tag}_down")
    return postnorm_res(x, y, gb, 0.5, f"{tag}_postnorm"), (x, h, G, U, A, y)


def _ffn_bwd(res, ga, gb, wg, wu, wd, dx2, tag):
    x, h, G, U, A, y = res
    dy, dgb = rms_bwd(y, gb, [dx2], 0.5, None, BF16, f"{tag}_postnorm_bwd")
    dA = _mm(dy, wd, tb=True, batch="map", out_dtype=BF16, name=f"{tag}_down_dx")
    dwd = _mm(A, dy, ta=True, batch="map", name=f"{tag}_down_dw", out_dtype=BF16)
    dG, dU = swiglu_bwd(G, U, dA, f"{tag}_swiglu_bwd")
    dwg = _mm(h, dG, ta=True, batch="map", name=f"{tag}_gate_dw", out_dtype=BF16)
    dwu = _mm(h, dU, ta=True, batch="map", name=f"{tag}_up_dw", out_dtype=BF16)
    dh1 = _mm(dG, wg, tb=True, batch="reduce", name=f"{tag}_gate_dx")
    dh2 = _mm(dU, wu, tb=True, batch="reduce", name=f"{tag}_up_dx")
    dx, dga = rms_bwd(x, ga, [dh1, dh2], 1.0, dx2, F32, f"{tag}_prenorm_bwd")
    return dx, dga, dgb, dwg, dwu, dwd


def _sink_rows(sinks):
    return jnp.repeat(sinks.reshape(N_KV_HEADS, Q_PER_KV), BLOCK, axis=1)[..., None]


def _attn_sgu_fwd(x, g_pre, g_post, w_in, w_out, sinks, ln_g, ln_b, sgu_w, bias_rows, tabs, tag):
    h = rms_fwd(x, g_pre, BF16, f"{tag}_prenorm")
    z = _mm(h, w_in, name=f"{tag}_in")
    qk = rope_apply(z, tabs, QK_WIDTH // LANES, False, BF16, f"{tag}_rope")
    q = _to_heads(qk[:, :ATTN_WIDTH], N_Q_HEADS)
    k = _to_heads(qk[:, ATTN_WIDTH:], N_KV_HEADS)
    v = _to_heads(z[:, QK_WIDTH:QK_WIDTH + KV_WIDTH].astype(BF16), N_KV_HEADS)
    o = swa_fwd(q, k, v, _sink_rows(sinks), f"{tag}_swa")
    u_pre = z[:, QK_WIDTH + KV_WIDTH:QK_WIDTH + KV_WIDTH + SGU_WIDTH]
    v_pre = z[:, QK_WIDTH + KV_WIDTH + SGU_WIDTH:]
    gate = sgu_fwd(u_pre, v_pre, ln_g, ln_b, sgu_w, bias_rows, f"{tag}_sgu")
    cat = jnp.concatenate([_from_heads(o), gate], axis=1)
    m = _mm(cat, w_out, name=f"{tag}_out")
    return postnorm_res(x, m, g_post, 1.0, f"{tag}_postnorm"), (x, h, q, k, v, u_pre, v_pre, cat, m)


def _attn_sgu_bwd(res, g_pre, g_post, w_in, w_out, sinks, ln_g, ln_b, sgu_w, bias_rows, tabs, dx2, tag):
    x, h, q, k, v, u_pre, v_pre, cat, m = res
    dm, dg_post = rms_bwd(m, g_post, [dx2], 1.0, None, BF16, f"{tag}_postnorm_bwd")
    dcat = _mm(dm, w_out, tb=True, out_dtype=BF16, name=f"{tag}_out_dx")
    dw_out = _mm(cat, dm, ta=True, name=f"{tag}_out_dw", out_dtype=BF16)
    do = _to_heads(dcat[:, :ATTN_WIDTH], N_Q_HEADS)
    dq, dkp, dkc, dvp, dvc, dsink = swa_bwd(q, k, v, _sink_rows(sinks), do, f"{tag}_swa_bwd")
    d_sinks = jnp.sum(dsink[..., 0], axis=1).reshape(1, N_Q_HEADS)
    dqk_rot = jnp.concatenate([_from_heads(dq).astype(F32), _fold_kv_grad(dkp, dkc)], axis=1)
    dqk = rope_apply(dqk_rot, tabs, QK_WIDTH // LANES, True, BF16, f"{tag}_rope_bwd")
    dv = _fold_kv_grad(dvp, dvc).astype(BF16)
    du_pre, dv_pre, dw_s, dbias, dlg, dlb = sgu_bwd(u_pre, v_pre, ln_g, ln_b, sgu_w, bias_rows, dcat[:, ATTN_WIDTH:], f"{tag}_sgu_bwd")
    dz = jnp.concatenate([dqk, dv, du_pre, dv_pre], axis=1)
    dw_in = _mm(h, dz, ta=True, name=f"{tag}_in_dw", out_dtype=BF16)
    dh = _mm(dz, w_in, tb=True, name=f"{tag}_in_dx")
    dx, dg_pre = rms_bwd(x, g_pre, [dh], 1.0, dx2, F32, f"{tag}_prenorm_bwd")
    causal = jnp.tril(jnp.ones((BLOCK, BLOCK), F32))
    small = dict(attn_sinks=d_sinks, sgu_ln_g=jnp.sum(dlg, axis=0, keepdims=True), sgu_ln_b=jnp.sum(dlb, axis=0, keepdims=True),
                 sgu_w=(dw_s * causal[None])[None], sgu_b=jnp.sum(dbias, axis=-1)[None])
    return dx, dg_pre, dg_post, dw_in, dw_out, small


def _pool_mix_fwd(x, g_pre, g_post, pool_w, pool_scale, tag):
    hf = rms_fwd(x, g_pre, F32, f"{tag}_prenorm")
    pooled = pool_fwd(hf, f"{tag}_pool")
    n_g = len(POOL_WINDOWS)
    ypre = _mm(pooled, pool_w, batch="map", groups=n_g, a_cb=True, o_cb=True, name=f"{tag}_proj")
    m = scale_cols(ypre, pool_scale, f"{tag}_scale")
    return postnorm_res(x, m, g_post, 1.0, f"{tag}_postnorm"), (x, pooled, ypre, m)


def _pool_mix_bwd(res, g_pre, g_post, pool_w, pool_scale, dx2, tag):
    x, pooled, ypre, m = res
    n_g = len(POOL_WINDOWS)
    dm, dg_post = rms_bwd(m, g_post, [dx2], 1.0, None, F32, f"{tag}_postnorm_bwd")
    dypre, dscale = scale_cols_bwd(dm, ypre, pool_scale, f"{tag}_scale_bwd")
    dpooled = _mm(dypre, pool_w, tb=True, batch="map", groups=n_g, a_cb=True, o_cb=True, name=f"{tag}_proj_dx")
    dpw = _mm(pooled, dypre, ta=True, batch="map", groups=n_g, a_cb=True, b_cb=True, name=f"{tag}_proj_dw", out_dtype=BF16)
    dhf = pool_bwd(dpooled, f"{tag}_pool_bwd")
    dx, dg_pre = rms_bwd(x, g_pre, [dhf], 1.0, dx2, F32, f"{tag}_prenorm_bwd")
    return dx, dg_pre, dg_post, dpw, jnp.sum(dscale, axis=0, keepdims=True)


def _xattn_fwd(x, mem, g_pre, g_post, g_mem, wq, wk, wv, wo, tag):
    h = rms_fwd(x, g_pre, BF16, f"{tag}_prenorm")
    mem_n = rms_fwd(mem, g_mem, BF16, f"{tag}_memnorm")
    q = _mm(h, wq, out_dtype=BF16, name=f"{tag}_q")
    k = _mm(mem_n, wk, out_dtype=BF16, name=f"{tag}_k")
    v = _mm(mem_n, wv, out_dtype=BF16, name=f"{tag}_v")
    o = xattn_fwd(q, k, v, f"{tag}_core")
    r = _mm(o, wo, name=f"{tag}_o")
    return postnorm_res(x, r, g_post, 1.0, f"{tag}_postnorm"), (x, h, mem_n, q, k, v, o, r)


def _xattn_bwd(res, mem, g_pre, g_post, g_mem, wq, wk, wv, wo, dx2, tag):
    x, h, mem_n, q, k, v, o, r = res
    dr, dg_post = rms_bwd(r, g_post, [dx2], 1.0, None, BF16, f"{tag}_postnorm_bwd")
    do = _mm(dr, wo, tb=True, out_dtype=BF16, name=f"{tag}_o_dx")
    dwo = _mm(o, dr, ta=True, name=f"{tag}_o_dw", out_dtype=BF16)
    dq, dk, dv = xattn_bwd(q, k, v, do, f"{tag}_core_bwd")
    dk, dv = dk.astype(BF16), dv.astype(BF16)
    dwq = _mm(h, dq, ta=True, name=f"{tag}_q_dw", out_dtype=BF16)
    dwk = _mm(mem_n, dk, ta=True, name=f"{tag}_k_dw", out_dtype=BF16)
    dwv = _mm(mem_n, dv, ta=True, name=f"{tag}_v_dw", out_dtype=BF16)
    dh = _mm(dq, wq, tb=True, name=f"{tag}_q_dx")
    dmem1 = _mm(dk, wk, tb=True, name=f"{tag}_k_dx")
    dmem2 = _mm(dv, wv, tb=True, name=f"{tag}_v_dx")
    _, dg_mem = rms_bwd(mem, g_mem, [dmem1, dmem2], 1.0, None, BF16, f"{tag}_memnorm_bwd")
    dx, dg_pre = rms_bwd(x, g_pre, [dh], 1.0, dx2, F32, f"{tag}_prenorm_bwd")
    return dx, dg_pre, dg_post, dg_mem, dwq, dwk, dwv, dwo


def _rowsum8(part):
    return jnp.sum(part, axis=0, keepdims=True)


def device_step(x, mem, target, norms, mem_norm, small, fetch, emit):
    n_tok = x.shape[0]
    tabs = rope_tables(n_tok)
    bias_rows = jnp.broadcast_to(small["sgu_b"][:, :, None], (SGU_GROUPS, BLOCK, LANES))
    gn = lambda l, i: norms[l, i][None, :]
    gm = lambda l: mem_norm[l][None, :]
    mix0 = (small["attn_sinks"], small["sgu_ln_g"], small["sgu_ln_b"], small["sgu_w"], bias_rows, tabs)

    wts, saved = {}, []
    for l in range(2):
        wts["ffn1", l], tok = fetch(("ffn1", l), x)
        x, r1 = _ffn_fwd(x, gn(l, 0) + tok, gn(l, 1), *wts["ffn1", l], f"l{l}_ffn1")
        wts["mix", l], tok = fetch(("mix", l), x)
        if l == 0:
            x, r2 = _attn_sgu_fwd(x, gn(l, 2) + tok, gn(l, 3), *wts["mix", l], *mix0, f"l{l}_mix")
        else:
            x, r2 = _pool_mix_fwd(x, gn(l, 2) + tok, gn(l, 3), *wts["mix", l], small["pool_scale"], f"l{l}_mix")
        wts["xattn", l], tok = fetch(("xattn", l), x)
        x, r3 = _xattn_fwd(x, mem, gn(l, 4) + tok, gn(l, 5), gm(l), *wts["xattn", l], f"l{l}_xattn")
        wts["ffn2", l], tok = fetch(("ffn2", l), x)
        x, r4 = _ffn_fwd(x, gn(l, 6) + tok, gn(l, 7), *wts["ffn2", l], f"l{l}_ffn2")
        saved.append((r1, r2, r3, r4))

    loss_part, dx = loss_and_grad(x, target, "loss")

    g_norm_rows = [[None] * 8, [None] * 8]
    g_mem_rows = [None, None]
    g_small = {}
    tok = 0.0
    for l in (1, 0):
        r1, r2, r3, r4 = saved[l]
        dx, g_norm_rows[l][6], g_norm_rows[l][7], dwg, dwu, dwd = _ffn_bwd(r4, gn(l, 6), gn(l, 7) + tok, *wts["ffn2", l], dx, f"l{l}_ffn2")
        tok = emit(("ffn2", l), dict(ffn2_wg=dwg, ffn2_wu=dwu, ffn2_wd=dwd), dx)
        dx, g_norm_rows[l][4], g_norm_rows[l][5], g_mem_rows[l], dwq, dwk, dwv, dwo = _xattn_bwd(
            r3, mem, gn(l, 4), gn(l, 5) + tok, gm(l), *wts["xattn", l], dx, f"l{l}_xattn")
        tok = emit(("xattn", l), dict(x_wq=dwq, x_wk=dwk, x_wv=dwv, x_wo=dwo), dx)
        if l == 0:
            dx, g_norm_rows[l][2], g_norm_rows[l][3], dw_in, dw_out, sm = _attn_sgu_bwd(
                r2, gn(l, 2), gn(l, 3) + tok, *wts["mix", l], *mix0, dx, f"l{l}_mix")
            tok = emit(("mix", l), dict(mix_w_in=dw_in, mix_w_out=dw_out), dx)
            g_small.update(sm)
        else:
            dx, g_norm_rows[l][2], g_norm_rows[l][3], dpw, dscale = _pool_mix_bwd(
                r2, gn(l, 2), gn(l, 3) + tok, *wts["mix", l], small["pool_scale"], dx, f"l{l}_mix")
            tok = emit(("mix", l), dict(pool_w=dpw), dx)
            g_small["pool_scale"] = dscale
        dx, g_norm_rows[l][0], g_norm_rows[l][1], dwg, dwu, dwd = _ffn_bwd(r1, gn(l, 0), gn(l, 1) + tok, *wts["ffn1", l], dx, f"l{l}_ffn1")
        tok = emit(("ffn1", l), dict(ffn1_wg=dwg, ffn1_wu=dwu, ffn1_wd=dwd), dx)
    g_norms = jnp.stack([jnp.concatenate([_rowsum8(p) for p in g_norm_rows[l]], axis=0) for l in range(2)])
    g_mem_norm = jnp.concatenate([_rowsum8(p) for p in g_mem_rows], axis=0)
    return loss_part, dx, g_small, g_norms, g_mem_norm


ANY = pl.BlockSpec(memory_space=pl.ANY)


def _place():
    x, y, c = lax.axis_index("x"), lax.axis_index("y"), lax.axis_index("c")
    other_chips = [(1 - x, y), (x, 1 - y), (1 - x, 1 - y)]
    return x, y, c, other_chips


def _half_rows(core, n_rows):
    half = n_rows // 2
    return pl.ds(pl.multiple_of(core * half, 16), half)


def _remote(src, dst, send_sem, recv_sem, device):
    return pltpu.make_async_remote_copy(src_ref=src, dst_ref=dst, send_sem=send_sem, recv_sem=recv_sem,
                                        device_id=device, device_id_type=MESH)


def gather_shards(slots, name):
    n = len(slots)

    def body(*refs):
        out = refs[n:2 * n]
        send_sems, recv_sems = refs[2 * n:]
        x, y, c, chips = _place()
        me, sibling = 2 * x + y, (x, y, 1 - c)
        sends = []
        for i in range(n):
            mine = out[i].at[me, _half_rows(c, out[i].shape[1])]
            for r, (px, py) in enumerate(chips):
                sends.append(_remote(mine, mine, send_sems.at[6 * i + r], recv_sems.at[6 * i + r], (px, py, c)))
                sends[-1].start()
        for i in range(n):
            for r, (px, py) in enumerate(chips):
                landed = out[i].at[2 * px + py, _half_rows(c, out[i].shape[1])]
                _remote(landed, landed, send_sems.at[6 * i + r], recv_sems.at[6 * i + r], (px, py, c)).wait_recv()
                sends.append(_remote(landed, landed, send_sems.at[6 * i + 3 + r], recv_sems.at[6 * i + 3 + r], sibling))
                sends[-1].start()
        for i in range(n):
            for r, (px, py) in enumerate(chips):
                landed = out[i].at[2 * px + py, _half_rows(1 - c, out[i].shape[1])]
                _remote(landed, landed, send_sems.at[6 * i + 3 + r], recv_sems.at[6 * i + 3 + r], sibling).wait_recv()
        for cp in sends:
            cp.wait_send()

    return pl.pallas_call(
        body, name=name, in_specs=[ANY] * n, out_specs=[ANY] * n, input_output_aliases={i: i for i in range(n)},
        out_shape=[jax.ShapeDtypeStruct(s.shape, s.dtype) for s in slots],
        scratch_shapes=[pltpu.SemaphoreType.DMA((6 * n,)), pltpu.SemaphoreType.DMA((6 * n,))],
    )(*slots)


HBM = pl.BlockSpec(memory_space=pltpu.HBM)
SEM = pl.BlockSpec(memory_space=pltpu.SEMAPHORE)
DATAFLOW = pltpu.SideEffectType.DATAFLOW_SIDE_EFFECTING


def _chip_copies(bufs, send_sems, recv_sems):
    x, y, c, chips = _place()
    me = 2 * x + y
    sends, arrivals = [], []
    for i, buf in enumerate(bufs):
        rows = _half_rows(c, buf.shape[1])
        for r, (px, py) in enumerate(chips):
            mine, theirs = buf.at[me, rows], buf.at[2 * px + py, rows]
            sends.append(_remote(mine, mine, send_sems.at[3 * i + r], recv_sems.at[3 * i + r], (px, py, c)))
            arrivals.append(_remote(theirs, theirs, send_sems.at[3 * i + r], recv_sems.at[3 * i + r], (px, py, c)))
    return sends, arrivals


def gather_start(slots, after, name):
    n = len(slots)

    def body(*refs):
        send_sems, recv_sems = refs[n + 1], refs[n + 2]
        bufs, token = refs[n + 3:2 * n + 3], refs[2 * n + 3]
        sends, _ = _chip_copies(bufs, send_sems, recv_sems)
        for cp in sends:
            cp.start()
        token[...] = jnp.zeros_like(token)

    res = pl.pallas_call(
        body, name=name, in_specs=[HBM] * n + [ANY],
        out_specs=[SEM, SEM] + [HBM] * n + [pl.BlockSpec(memory_space=pltpu.VMEM)],
        out_shape=[pltpu.SemaphoreType.DMA((3 * n,)), pltpu.SemaphoreType.DMA((3 * n,))]
        + [pltpu.HBM(s.shape, s.dtype) for s in slots] + [jax.ShapeDtypeStruct((8, LANES), F32)],
        input_output_aliases={i: 2 + i for i in range(n)}, compiler_params=pltpu.CompilerParams(has_side_effects=DATAFLOW),
    )(*[pltpu.with_memory_space_constraint(s, pltpu.HBM) for s in slots], after)
    return res[0], res[1], list(res[2:2 + n]), res[2 + n]


def gather_wait(send_sems, recv_sems, bufs, after, name):
    n = len(bufs)

    def body(*refs):
        send_ref, recv_ref = refs[n], refs[n + 1]
        _, arrivals = _chip_copies(refs[:n], send_ref, recv_ref)
        for cp in arrivals:
            cp.wait_send()
            cp.wait_recv()

    return pl.pallas_call(
        body, name=name, in_specs=[HBM] * n + [SEM, SEM, ANY], out_specs=[HBM] * n,
        out_shape=[pltpu.HBM(b.shape, b.dtype) for b in bufs], input_output_aliases={i: i for i in range(n)},
        compiler_params=pltpu.CompilerParams(has_side_effects=DATAFLOW),
    )(*bufs, send_sems, recv_sems, after)


def forward_to_sibling(bufs, name):
    n = len(bufs)

    def body(*refs):
        out = refs[n:2 * n]
        send_sems, recv_sems = refs[2 * n:]
        x, y, c, chips = _place()
        copies = []
        for i in range(n):
            for r, (px, py) in enumerate(chips):
                mine = out[i].at[2 * px + py, _half_rows(c, out[i].shape[1])]
                copies.append(_remote(mine, mine, send_sems.at[3 * i + r], recv_sems.at[3 * i + r], (x, y, 1 - c)))
                copies[-1].start()
        for i in range(n):
            for r, (px, py) in enumerate(chips):
                theirs = out[i].at[2 * px + py, _half_rows(1 - c, out[i].shape[1])]
                _remote(theirs, theirs, send_sems.at[3 * i + r], recv_sems.at[3 * i + r], (x, y, 1 - c)).wait_recv()
        for cp in copies:
            cp.wait_send()

    return pl.pallas_call(
        body, name=name, in_specs=[ANY] * n, out_specs=[ANY] * n, input_output_aliases={i: i for i in range(n)},
        out_shape=[jax.ShapeDtypeStruct(b.shape, b.dtype) for b in bufs],
        scratch_shapes=[pltpu.SemaphoreType.DMA((3 * n,)), pltpu.SemaphoreType.DMA((3 * n,))],
    )(*bufs)


def swap_other_halves(grads, name):
    n = len(grads)

    def body(*refs):
        src, out = refs[:n], refs[n:2 * n]
        send_sems, recv_sems = refs[2 * n:]
        x, y, c, _ = _place()
        copies = [_remote(src[i].at[:, _half_rows(1 - c, src[i].shape[1])], out[i], send_sems.at[i], recv_sems.at[i], (x, y, 1 - c))
                  for i in range(n)]
        for cp in copies:
            cp.start()
        for cp in copies:
            cp.wait()

    return pl.pallas_call(
        body, name=name, in_specs=[ANY] * n, out_specs=[ANY] * n,
        out_shape=[jax.ShapeDtypeStruct((g.shape[0], g.shape[1] // 2, g.shape[2]), g.dtype) for g in grads],
        scratch_shapes=[pltpu.SemaphoreType.DMA((n,)), pltpu.SemaphoreType.DMA((n,))],
    )(*grads)


def add_own_half(g, p, core, name):
    n_j, n_rows, n_cols = g.shape
    half = n_rows // 2
    tr = _rows(half, n_cols * 10)

    def body(c_ref, g_ref, p_ref, o_ref):
        o_ref[...] = (g_ref[...] + p_ref[...]).astype(o_ref.dtype)

    grid_spec = pltpu.PrefetchScalarGridSpec(
        num_scalar_prefetch=1, grid=(n_j, half // tr),
        in_specs=[pl.BlockSpec((None, None, tr, n_cols), lambda j, i, c_ref: (j, c_ref[0], i, 0)),
                  pl.BlockSpec((None, tr, n_cols), lambda j, i, c_ref: (j, i, 0))],
        out_specs=pl.BlockSpec((None, tr, n_cols), lambda j, i, c_ref: (j, i, 0)))
    return pl.pallas_call(
        body, name=name, grid_spec=grid_spec, out_shape=jax.ShapeDtypeStruct((n_j, half, n_cols), BF16),
        compiler_params=_params(("parallel", "parallel")),
    )(core, g.reshape(n_j, 2, half, n_cols), p)


def _scatter_copies(parts, lands, send_sems, recv_sems):
    x, y, c, chips = _place()
    me = 2 * x + y
    sends, arrivals = [], []
    for i in range(len(parts)):
        for r, (px, py) in enumerate(chips):
            theirs = lands[i].at[2 * px + py]
            sends.append(_remote(parts[i].at[2 * px + py], lands[i].at[me], send_sems.at[3 * i + r], recv_sems.at[3 * i + r], (px, py, c)))
            arrivals.append(_remote(theirs, theirs, send_sems.at[3 * i + r], recv_sems.at[3 * i + r], (px, py, c)))
    return sends, arrivals


def scatter_start(parts, after, name):
    n = len(parts)

    def body(*refs):
        send_sems, recv_sems = refs[2 * n + 1], refs[2 * n + 2]
        src, lands, token = refs[2 * n + 3:3 * n + 3], refs[3 * n + 3:4 * n + 3], refs[4 * n + 3]
        sends, _ = _scatter_copies(src, lands, send_sems, recv_sems)
        for cp in sends:
            cp.start()
        token[...] = jnp.zeros_like(token)

    fresh = [pltpu.with_memory_space_constraint(lax.empty(p.shape, p.dtype), pltpu.HBM) for p in parts]
    res = pl.pallas_call(
        body, name=name, in_specs=[HBM] * (2 * n) + [ANY],
        out_specs=[SEM, SEM] + [HBM] * (2 * n) + [pl.BlockSpec(memory_space=pltpu.VMEM)],
        out_shape=[pltpu.SemaphoreType.DMA((3 * n,)), pltpu.SemaphoreType.DMA((3 * n,))]
        + [pltpu.HBM(p.shape, p.dtype) for p in parts] * 2 + [jax.ShapeDtypeStruct((8, LANES), F32)],
        input_output_aliases={i: 2 + i for i in range(2 * n)}, compiler_params=pltpu.CompilerParams(has_side_effects=DATAFLOW),
    )(*[pltpu.with_memory_space_constraint(p, pltpu.HBM) for p in parts], *fresh, after)
    return res[0], res[1], list(res[2:2 + n]), list(res[2 + n:2 + 2 * n]), res[2 + 2 * n]


def scatter_wait(send_sems, recv_sems, parts, lands, after, name):
    n = len(parts)

    def body(*refs):
        _, arrivals = _scatter_copies(refs[:n], refs[n:2 * n], refs[2 * n], refs[2 * n + 1])
        for cp in arrivals:
            cp.wait_send()
            cp.wait_recv()

    res = pl.pallas_call(
        body, name=name, in_specs=[HBM] * (2 * n) + [SEM, SEM, ANY], out_specs=[HBM] * (2 * n),
        out_shape=[pltpu.HBM(p.shape, p.dtype) for p in parts] * 2, input_output_aliases={i: i for i in range(2 * n)},
        compiler_params=pltpu.CompilerParams(has_side_effects=DATAFLOW),
    )(*parts, *lands, send_sems, recv_sems, after)
    return list(res[:n]), list(res[n:])


def sum_over_chips(own, got, chip, name):
    n_s, n_rows, n_cols = got.shape
    tr = _rows(n_rows, n_cols * (got.dtype.itemsize * (n_s + 1) + 4))

    def body(chip_ref, own_ref, *refs):
        got_refs, o_ref = refs[:n_s], refs[n_s]
        me = chip_ref[0]
        acc = jnp.where(me == 0, own_ref[...], got_refs[0][...]).astype(F32)
        for k in range(1, n_s):
            acc = acc + jnp.where(me == k, own_ref[...], got_refs[k][...]).astype(F32)
        o_ref[...] = acc

    def slot(k):
        return pl.BlockSpec((None, tr, n_cols), lambda i, chip_ref: (jnp.where(chip_ref[0] == k, (k + 1) % n_s, k), i, 0))

    grid_spec = pltpu.PrefetchScalarGridSpec(
        num_scalar_prefetch=1, grid=(n_rows // tr,),
        in_specs=[pl.BlockSpec((None, tr, n_cols), lambda i, chip_ref: (chip_ref[0], i, 0))] + [slot(k) for k in range(n_s)],
        out_specs=pl.BlockSpec((tr, n_cols), lambda i, chip_ref: (i, 0)))
    return pl.pallas_call(
        body, name=name, grid_spec=grid_spec, out_shape=jax.ShapeDtypeStruct((n_rows, n_cols), F32),
        compiler_params=_params(("parallel",)),
    )(chip, own, *([got] * n_s))


def swap_with_sibling(arrays, name):
    n = len(arrays)

    def body(*refs):
        src, out = refs[:n], refs[n:2 * n]
        send_sems, recv_sems = refs[2 * n:]
        x, y, c, _ = _place()
        copies = [_remote(src[i], out[i], send_sems.at[i], recv_sems.at[i], (x, y, 1 - c)) for i in range(n)]
        for cp in copies:
            cp.start()
        for cp in copies:
            cp.wait()

    return pl.pallas_call(
        body, name=name, in_specs=[ANY] * n, out_specs=[ANY] * n, out_shape=[jax.ShapeDtypeStruct(a.shape, a.dtype) for a in arrays],
        scratch_shapes=[pltpu.SemaphoreType.DMA((n,)), pltpu.SemaphoreType.DMA((n,))],
    )(*arrays)


def sum_slots(q, name):
    n_s, n_rows, n_cols = q.shape
    tr = _rows(n_rows, n_cols * (q.dtype.itemsize * n_s + 4))

    def body(q_ref, o_ref):
        acc = q_ref[0].astype(F32)
        for s in range(1, n_s):
            acc = acc + q_ref[s].astype(F32)
        o_ref[...] = acc

    return pl.pallas_call(
        body, name=name, grid=(n_rows // tr,), in_specs=[pl.BlockSpec((n_s, tr, n_cols), lambda i: (0, i, 0))],
        out_specs=pl.BlockSpec((tr, n_cols), lambda i: (i, 0)), out_shape=jax.ShapeDtypeStruct((n_rows, n_cols), F32),
        compiler_params=_params(("parallel",)),
    )(q)


def gather_all_devices(s, name):
    def body(s_ref, o_ref, send_sems, recv_sems, local_sem):
        x, y, c, _ = _place()
        me = 4 * x + 2 * y + c
        local = pltpu.make_async_copy(s_ref, o_ref.at[me], local_sem)
        local.start()
        sends = []
        for f in range(1, N_DEV):
            px, py, pc = x ^ (f >> 2), y ^ ((f >> 1) & 1), c ^ (f & 1)
            sends.append(_remote(s_ref, o_ref.at[me], send_sems.at[f - 1], recv_sems.at[f - 1], (px, py, pc)))
            sends[-1].start()
        for f in range(1, N_DEV):
            px, py, pc = x ^ (f >> 2), y ^ ((f >> 1) & 1), c ^ (f & 1)
            landed = o_ref.at[4 * px + 2 * py + pc]
            _remote(landed, landed, send_sems.at[f - 1], recv_sems.at[f - 1], (px, py, pc)).wait_recv()
        for cp in sends:
            cp.wait_send()
        local.wait()

    return pl.pallas_call(
        body, name=name, in_specs=[ANY], out_specs=ANY, out_shape=jax.ShapeDtypeStruct((N_DEV,) + s.shape, s.dtype),
        scratch_shapes=[pltpu.SemaphoreType.DMA((N_DEV - 1,)), pltpu.SemaphoreType.DMA((N_DEV - 1,)), pltpu.SemaphoreType.DMA],
    )(s)


FFN_NAMES = ("ffn1_wg", "ffn1_wu", "ffn1_wd", "ffn2_wg", "ffn2_wu", "ffn2_wd")
XATTN_NAMES = ("x_wq", "x_wk", "x_wv", "x_wo")
BIG_NAMES = FFN_NAMES + XATTN_NAMES + ("mix_w_in", "mix_w_out", "pool_w")
SMALL_NAMES = ("norms", "mem_norm", "attn_sinks", "sgu_ln_g", "sgu_ln_b", "sgu_w", "sgu_b", "pool_scale")
WEIGHT_ORDER = ("norms", "mem_norm") + BIG_NAMES[:-1] + ("attn_sinks", "sgu_ln_g", "sgu_ln_b", "sgu_w", "sgu_b", "pool_w", "pool_scale")
COLUMN_CUT = ("x_wo", "mix_w_in")
N_POOL = len(POOL_WINDOWS)
BLOCK_ORDER = (("ffn1", 0), ("mix", 0), ("xattn", 0), ("ffn2", 0), ("ffn1", 1), ("mix", 1), ("xattn", 1), ("ffn2", 1))
PREFETCH_AT = ((1,), (2, 3), (), (4,), (5,), (6, 7), (), ())


def block_weight_names(kind, layer):
    if kind == "mix":
        return ("mix_w_in", "mix_w_out") if layer == 0 else ("pool_w",)
    return XATTN_NAMES if kind == "xattn" else tuple(f"{kind}_{part}" for part in ("wg", "wu", "wd"))


def _to_matmul_layout(name, g):
    n_j, n_rows, n_cols = g.shape
    if name in COLUMN_CUT:
        return g.transpose(1, 0, 2).reshape(n_rows, n_j * n_cols)
    if name == "pool_w":
        return g.reshape(n_j, N_POOL, n_rows // N_POOL, n_cols).transpose(1, 0, 2, 3).reshape(N_POOL, n_j * n_rows // N_POOL, n_cols)
    if name in ("x_wq", "x_wk", "x_wv", "mix_w_out"):
        return g.reshape(n_j * n_rows, n_cols)
    return g


def _from_matmul_layout(name, d):
    if name in COLUMN_CUT:
        n_rows, wide = d.shape
        return d.reshape(n_rows, N_CHIPS, wide // N_CHIPS).transpose(1, 0, 2)
    if name == "pool_w":
        n_g, n_in, n_cols = d.shape
        return d.reshape(n_g, N_CHIPS, n_in // N_CHIPS, n_cols).transpose(1, 0, 2, 3).reshape(N_CHIPS, n_g * n_in // N_CHIPS, n_cols)
    if name in ("x_wq", "x_wk", "x_wv", "mix_w_out"):
        return d.reshape(N_CHIPS, d.shape[0] // N_CHIPS, d.shape[1])
    return d


def _as3(w):
    return w.reshape(w.shape[0], -1, w.shape[-1])


def _pack(arrays, row_multiple):
    flat = jnp.concatenate([a.reshape(-1) for a in arrays])
    per = LANES * row_multiple
    total = -(-flat.shape[0] // per) * per
    return jnp.pad(flat, (0, total - flat.shape[0])).reshape(total // LANES, LANES)


def _unpack(packed, like):
    flat, out, at = packed.reshape(-1), [], 0
    for a in like:
        out.append(flat[at:at + a.size].reshape(a.shape))
        at += a.size
    return out


def kernel(x, mem, norms, mem_norm, ffn1_wg, ffn1_wu, ffn1_wd, ffn2_wg, ffn2_wu, ffn2_wd, x_wq, x_wk, x_wv, x_wo, mix_w_in, mix_w_out, attn_sinks, sgu_ln_g, sgu_ln_b, sgu_w, sgu_b, pool_w, pool_scale, loss_target, m_norms, m_mem_norm, m_ffn1_wg, m_ffn1_wu, m_ffn1_wd, m_ffn2_wg, m_ffn2_wu, m_ffn2_wd, m_x_wq, m_x_wk, m_x_wv, m_x_wo, m_mix_w_in, m_mix_w_out, m_attn_sinks, m_sgu_ln_g, m_sgu_ln_b, m_sgu_w, m_sgu_b, m_pool_w, m_pool_scale, v_norms, v_mem_norm, v_ffn1_wg, v_ffn1_wu, v_ffn1_wd, v_ffn2_wg, v_ffn2_wu, v_ffn2_wd, v_x_wq, v_x_wk, v_x_wv, v_x_wo, v_mix_w_in, v_mix_w_out, v_attn_sinks, v_sgu_ln_g, v_sgu_ln_b, v_sgu_w, v_sgu_b, v_pool_w, v_pool_scale):
    given = dict(locals())
    w = {n: given[n] for n in WEIGHT_ORDER}
    mom = {n: given["m_" + n] for n in WEIGHT_ORDER}
    var = {n: given["v_" + n] for n in WEIGHT_ORDER}
    chip_id = 2 * lax.axis_index("x") + lax.axis_index("y")
    chip = chip_id.astype(jnp.int32).reshape(1)
    core = lax.axis_index("c").astype(jnp.int32).reshape(1)
    n_shard = norms.shape[-1]

    slots, keys = [], []
    for name in BIG_NAMES:
        w3 = _as3(w[name])
        for l in range(w3.shape[0]):
            slots.append(cast_into_slot(w3, l, chip, f"cast_{name}"))
            keys.append((name, l))
    small_rows = jnp.concatenate([norms.reshape(-1, n_shard), pool_scale, jnp.zeros((15, n_shard), F32)], axis=0)
    small_slot = lax.dynamic_update_slice_in_dim(jnp.zeros((N_CHIPS,) + small_rows.shape, F32), small_rows[None], chip_id, axis=0)
    slot_of = dict(zip(keys, slots))

    pending, ready = {}, {}

    def block_slots(k):
        kind, layer = BLOCK_ORDER[k]
        return [slot_of[name, layer if kind != "mix" else 0] for name in block_weight_names(kind, layer)]

    def start(k, after):
        pending[k] = gather_start(block_slots(k), after, f"gather_start_{k}")

    def finish(k, after):
        send_sems, recv_sems, bufs, _ = pending[k]
        return forward_to_sibling(gather_wait(send_sems, recv_sems, bufs, after, f"gather_wait_{k}"), f"gather_forward_{k}")

    def fetch(key, after):
        k = BLOCK_ORDER.index(key)
        bufs = ready.pop(k) if k in ready else finish(k, after)
        token = 0.0
        for ahead in PREFETCH_AT[k]:
            start(ahead, bufs[0])
            token = token + pending[ahead][3][0, 0]
        return tuple(_to_matmul_layout(name, b) for name, b in zip(block_weight_names(*key), bufs)), token

    in_flight, own_of, recv_of = [], {}, {}

    def land(after):
        tag, names, send_sems, recv_sems, parts, lands = in_flight.pop()
        parts, lands = scatter_wait(send_sems, recv_sems, parts, lands, after, f"grads_wait_{tag}")
        return tag, names, parts, lands

    def reduce_landed(tag, names, parts, lands):
        own = [sum_over_chips(t, q, chip, f"sum_{name}") for (name, _), t, q in zip(names, parts, lands)]
        for key, o, r in zip(names, own, swap_with_sibling(own, f"grads_halves_{tag}")):
            own_of[key], recv_of[key] = o, r

    def emit(key, grads_of, after):
        kind, layer = key
        tag = f"{kind}{layer}"
        names = [(name, layer if kind != "mix" else 0) for name in grads_of]
        landed = land(after) if in_flight else None
        full = [_from_matmul_layout(name, g) for name, g in grads_of.items()]
        from_sibling = swap_other_halves(full, f"grads_to_sibling_{tag}")
        chip_sums = [add_own_half(g, p, core, f"chip_sum_{name}") for (name, _), g, p in zip(names, full, from_sibling)]
        send_sems, recv_sems, parts, lands, token = scatter_start(chip_sums, after, f"grads_start_{tag}")
        in_flight.append((tag, names, send_sems, recv_sems, parts, lands))
        if landed:
            reduce_landed(*landed)
        return token[0, 0]

    first = gather_shards(block_slots(0) + [small_slot], "gather_first")
    ready[0], small_all = first[:-1], first[-1]
    n_norm_rows = norms.shape[0] * norms.shape[1]
    norms_all = jnp.concatenate([small_all[j, :n_norm_rows].reshape(norms.shape) for j in range(N_CHIPS)], axis=-1)
    pool_scale_all = jnp.concatenate([small_all[j, n_norm_rows:n_norm_rows + 1] for j in range(N_CHIPS)], axis=-1)
    small = dict(attn_sinks=attn_sinks[0], sgu_ln_g=sgu_ln_g, sgu_ln_b=sgu_ln_b, sgu_w=sgu_w[0], sgu_b=sgu_b[0], pool_scale=pool_scale_all)

    loss_part, dx, g_small, g_norms, g_mem_norm = device_step(x[0], mem[0], loss_target[0], norms_all, mem_norm, small, fetch, emit)
    loss = lax.psum(0.5 * jnp.sum(loss_part) / x.shape[-1], ("x", "y", "c"))

    grads, delta, new_m, new_v = {}, {}, {}, {}

    def update(name):
        n_l = _as3(w[name]).shape[0]
        res = adamw_from_halves(_as3(w[name]), [own_of[name, l] for l in range(n_l)], [recv_of[name, l] for l in range(n_l)],
                                _as3(mom[name]), _as3(var[name]), core, f"adamw_{name}")
        grads[name], delta[name], new_m[name], new_v[name] = (t.reshape(w[name].shape) for t in res)

    last_names = {name for name, _ in in_flight[0][1]}
    for name in BIG_NAMES:
        if name not in last_names:
            update(name)
    reduce_landed(*land(delta[BIG_NAMES[-1]]))
    for name in BIG_NAMES:
        if name in last_names:
            update(name)

    small_g = [g_norms, g_mem_norm, g_small["attn_sinks"], g_small["sgu_ln_g"], g_small["sgu_ln_b"], g_small["sgu_w"], g_small["sgu_b"],
               g_small["pool_scale"]]
    packed = _pack(small_g, 16)
    summed = sum_slots(gather_all_devices(packed, "small_grads_all"), "small_grads_sum")
    s_norms, s_mem, s_sinks, s_lg, s_lb, s_w, s_b, s_scale = _unpack(summed, small_g)
    grads["norms"] = lax.dynamic_slice_in_dim(s_norms, chip_id * n_shard, n_shard, axis=2)
    grads["pool_scale"] = lax.dynamic_slice_in_dim(s_scale, chip_id * n_shard, n_shard, axis=1)
    grads.update(mem_norm=s_mem, attn_sinks=s_sinks, sgu_ln_g=s_lg, sgu_ln_b=s_lb, sgu_w=s_w, sgu_b=s_b)
    like = [w[n] for n in SMALL_NAMES]
    packs = [_pack([src[n] for n in SMALL_NAMES], 16)[None] for src in (w, grads, mom, var)]
    for dst, t in zip((delta, new_m, new_v), adamw(*packs, "adamw_small")):
        for n, a in zip(SMALL_NAMES, _unpack(t[0], like)):
            dst[n] = a

    outs = [loss, dx[None]]
    for group in (grads, delta, new_m, new_v):
        outs += [group[n] for n in WEIGHT_ORDER]
    return tuple(outs)
```

```python
import functools

import jax
import jax.numpy as jnp
from jax import lax
from jax.experimental import pallas as pl
from jax.experimental.pallas import tpu as pltpu

F32 = jnp.float32
BF16 = jnp.bfloat16
MESH = pl.DeviceIdType.MESH

EPS = 1e-6
ROPE_THETA = 500000.0
ROPE_HALF = 8
HEAD_DIM = 64
N_Q_HEADS = 16
N_KV_HEADS = 2
Q_PER_KV = 8
BLOCK = 128
ATTN_WIDTH = 1024
KV_WIDTH = 128
QK_WIDTH = ATTN_WIDTH + KV_WIDTH
SGU_WIDTH = 1024
SGU_GROUPS = 8
POOL_WINDOWS = (2, 4, 8, 16)
POOL_HALO = 16
X_HEADS = 4
X_HEAD_DIM = 128
N_CHIPS = 4
N_DEV = 8

ADAM_LR = 0.001
ADAM_B1 = 0.9
ADAM_B2 = 0.999
ADAM_EPS = 1e-08
ADAM_WD = 0.01
ADAM_STEP = 10

VMEM_LIMIT_V7X = 52 * 1024 * 1024
MM_VMEM_BUDGET = 40 * 1024 * 1024
LANES = 128
ROW_TILE_BYTES = 6 * 1024 * 1024
MXU_FLOPS_V7X = 1.0e15
HBM_BYTES_PER_S_V7X = 3.0e12
VMEM_STORE_BYTES_PER_S = 4.0e12
MXU_WEIGHT_LOAD_ROWS = 192
GRID_STEP_S = 0.35e-6


def _params(sem):
    return pltpu.CompilerParams(dimension_semantics=sem, vmem_limit_bytes=VMEM_LIMIT_V7X)


def _pick(dim, pref, align):
    cands = [t for t in range(align, dim + 1, align) if dim % t == 0]
    small = [t for t in cands if t <= pref]
    if small and small[-1] * 2 >= min(pref, dim):
        return small[-1]
    return dim


def _rows(n_rows, bytes_per_row):
    want = max(16, min(1024, ROW_TILE_BYTES // max(1, bytes_per_row)))
    cands = [t for t in range(16, n_rows + 1, 16) if n_rows % t == 0 and t <= want]
    return cands[-1] if cands else n_rows


def _divisors(dim, align, most):
    return [t for t in range(align, min(dim, most) + 1, align) if dim % t == 0] or [dim]


def _mm_tiles(M, N, K, J, m_align, k_align, a_bytes, b_bytes, o_bytes, reduce, ta, products=1):
    best = None
    for tm in _divisors(M, m_align, 2048):
        for tn in _divisors(N, LANES, 2048):
            for tk in _divisors(K, k_align, 4096):
                split = K // tk > 1 or reduce
                vmem = 2 * (tm * tk * a_bytes + tk * tn * b_bytes + tm * tn * o_bytes) + tm * tn * 4 * products * (2 if split else 1)
                if ta:
                    vmem += tm * tk * a_bytes
                if vmem > MM_VMEM_BUDGET:
                    continue
                steps = J * (M // tm) * (N // tn) * (K // tk)
                mxu = 2.0 * J * M * N * K / MXU_FLOPS_V7X * (tm + MXU_WEIGHT_LOAD_ROWS) / tm
                acc = J * M * N * (K // tk) * 8 / VMEM_STORE_BYTES_PER_S if split else 0.0
                hbm = J * (M * K * a_bytes * (N // tn) + K * N * b_bytes * (M // tm) + M * N * o_bytes) / HBM_BYTES_PER_S_V7X
                cost = max(mxu + 0.5 * acc, hbm) + steps * GRID_STEP_S
                if best is None or cost < best[0]:
                    best = (cost, tm, tn, tk)
    return best[1:]


def _rsum8(v):
    r, c = v.shape
    return v.reshape(r // 8, 8, c).sum(axis=0)


def _mm(a, b, *, name, ta=False, tb=False, batch="none", groups=0, a_cb=False, b_cb=False, o_cb=False,
        out_dtype=F32, more=(), extras=(), epilogue=None, out_dtypes=None):
    J = groups or (a.shape[0] if a.ndim == 3 else (b.shape[0] if b.ndim == 3 else 1))
    a2, b2 = a.shape[-2:], b.shape[-2:]
    M, K = (a2[1], a2[0]) if ta else a2
    N, Kb = b2 if tb else (b2[1], b2[0])
    if a_cb:
        if ta:
            M //= J
        else:
            K //= J
    if b_cb:
        if tb:
            Kb //= J
        else:
            N //= J
    assert K == Kb, (name, a.shape, b.shape)
    reduce = batch == "reduce"
    out_dtypes = list(out_dtypes or [out_dtype])
    n_terms = 1 + len(more)
    o_bytes = sum(jnp.dtype(d).itemsize for d in out_dtypes) + sum(e.dtype.itemsize for e in extras)
    n_prod = n_terms if epilogue is not None else 1
    tm, tn, tk = _mm_tiles(M, N, K, J, LANES if ta else 16, LANES if (not ta or tb) else 16, a.dtype.itemsize * n_terms,
                           b.dtype.itemsize * n_terms, o_bytes, reduce, ta, n_prod)
    nm, nn, nk = M // tm, N // tn, K // tk
    if reduce:
        grid = (nm, nn, J, nk)
        unpack = lambda m, n, j, k: (j, m, n, k)
        sem = ("parallel", "parallel", "arbitrary", "arbitrary")
    else:
        grid = (J, nm, nn, nk)
        unpack = lambda j, m, n, k: (j, m, n, k)
        sem = ("parallel", "parallel", "parallel", "arbitrary")

    def a_map(*g):
        j, m, n, k = unpack(*g)
        r, c = (k, m) if ta else (m, k)
        if a_cb:
            c = c + j * (nm if ta else nk)
        return (j, r, c) if a.ndim == 3 else (r, c)

    def b_map(*g):
        j, m, n, k = unpack(*g)
        r, c = (n, k) if tb else (k, n)
        if b_cb:
            c = c + j * (nk if tb else nn)
        return (j, r, c) if b.ndim == 3 else (r, c)

    def o_map(*g):
        j, m, n, k = unpack(*g)
        if o_cb:
            return (m, n + j * nn)
        return (j, m, n) if batch == "map" else (m, n)

    a_blk = (tk, tm) if ta else (tm, tk)
    b_blk = (tn, tk) if tb else (tk, tn)
    a_spec = pl.BlockSpec(((None,) + a_blk) if a.ndim == 3 else a_blk, a_map)
    b_spec = pl.BlockSpec(((None,) + b_blk) if b.ndim == 3 else b_blk, b_map)
    if o_cb:
        out_shape, o_blk = (M, N * J), (tm, tn)
    elif batch == "map":
        out_shape, o_blk = (J, M, N), (None, tm, tn)
    else:
        out_shape, o_blk = (M, N), (tm, tn)
    o_spec = pl.BlockSpec(o_blk, o_map)
    dims = (((0 if ta else 1,), (1 if tb else 0,)), ((), ()))
    red_axes = (2, 3) if reduce else (3,)
    split = reduce or nk > 1
    n_ex, n_out = len(extras), len(out_dtypes)

    def body(*refs):
        ab_refs, ex_refs = refs[:2 * n_terms], refs[2 * n_terms:2 * n_terms + n_ex]
        o_refs, acc = refs[2 * n_terms + n_ex:2 * n_terms + n_ex + n_out], refs[2 * n_terms + n_ex + n_out:]
        prods = [lax.dot_general(ab_refs[2 * t][...], ab_refs[2 * t + 1][...], dims, preferred_element_type=F32) for t in range(n_terms)]
        if epilogue is None:
            prods = [functools.reduce(lambda p, q: p + q, prods)]

        def finish(vals):
            outs = epilogue(vals, [e[...] for e in ex_refs]) if epilogue is not None else vals
            for o_ref, val in zip(o_refs, outs):
                o_ref[...] = val.astype(o_ref.dtype)

        if not split:
            finish(prods)
            return
        first = functools.reduce(jnp.logical_and, [pl.program_id(ax) == 0 for ax in red_axes])
        last = functools.reduce(jnp.logical_and, [pl.program_id(ax) == grid[ax] - 1 for ax in red_axes])

        @pl.when(first)
        def _():
            for acc_ref, prod in zip(acc, prods):
                acc_ref[...] = prod

        @pl.when(jnp.logical_not(first))
        def _():
            for acc_ref, prod in zip(acc, prods):
                acc_ref[...] += prod

        @pl.when(last)
        def _():
            finish([acc_ref[...] for acc_ref in acc])

    operands = [a, b] + [t for pair in more for t in pair] + list(extras)
    res = pl.pallas_call(
        body, name=name, grid=grid, in_specs=[a_spec, b_spec] * n_terms + [o_spec] * n_ex, out_specs=[o_spec] * n_out,
        out_shape=[jax.ShapeDtypeStruct(out_shape, d) for d in out_dtypes],
        scratch_shapes=[pltpu.VMEM((tm, tn), F32)] * (n_prod if split else 0), compiler_params=_params(sem),
    )(*operands)
    return res if epilogue is not None else res[0]


def _rowwise(fn, tiled, whole, outs, accs, *, name):
    n_rows = tiled[0].shape[0]
    row_bytes = sum(t.shape[1] * t.dtype.itemsize for t in tiled) + sum(c * jnp.dtype(d).itemsize for c, d in outs)
    tr = _rows(n_rows, row_bytes)
    n_t, n_w, n_o = len(tiled), len(whole), len(outs)

    def body(*refs):
        i = pl.program_id(0)
        t_refs, w_refs = refs[:n_t], refs[n_t:n_t + n_w]
        o_refs, a_refs = refs[n_t + n_w:n_t + n_w + n_o], refs[n_t + n_w + n_o:]
        o_vals, a_vals = fn(i, *[r[...] for r in t_refs], *[r[...] for r in w_refs])
        for r, v in zip(o_refs, o_vals):
            r[...] = v.astype(r.dtype)
        if a_refs:
            @pl.when(i == 0)
            def _():
                for r in a_refs:
                    r[...] = jnp.zeros_like(r)
            for r, v in zip(a_refs, a_vals):
                r[...] += v

    in_specs = [pl.BlockSpec((tr, t.shape[1]), lambda i: (i, 0)) for t in tiled]
    in_specs += [pl.BlockSpec(w.shape, lambda i, nd=w.ndim: (0,) * nd) for w in whole]
    out_specs = [pl.BlockSpec((tr, c), lambda i: (i, 0)) for c, _ in outs]
    out_specs += [pl.BlockSpec(s, lambda i, nd=len(s): (0,) * nd) for s, _ in accs]
    out_shape = [jax.ShapeDtypeStruct((n_rows, c), d) for c, d in outs]
    out_shape += [jax.ShapeDtypeStruct(s, d) for s, d in accs]
    res = pl.pallas_call(
        body, name=name, grid=(n_rows // tr,), in_specs=in_specs, out_specs=out_specs, out_shape=out_shape,
        compiler_params=_params(("arbitrary",) if accs else ("parallel",)),
    )(*tiled, *whole)
    return res


def _rms_stats(x):
    r = lax.rsqrt(jnp.mean(x * x, axis=-1, keepdims=True) + EPS)
    return x * r, r


def rms_fwd(x, g, out_dtype, name):
    def fn(i, x, g):
        xhat, _ = _rms_stats(x)
        return [xhat * g], []
    return _rowwise(fn, [x], [g], [(x.shape[1], out_dtype)], [], name=name)[0]


def postnorm_res(x, y, g, s, name):
    def fn(i, x, y, g):
        yhat, _ = _rms_stats(y)
        return [x + s * (yhat * g)], []
    return _rowwise(fn, [x, y], [g], [(x.shape[1], F32)], [], name=name)[0]


def rms_bwd(xin, g, douts, scale, add, out_dtype, name):
    n_d = len(douts)

    def fn(i, x, *rest):
        ds, rest = rest[:n_d], rest[n_d:]
        ad = rest[0] if add is not None else None
        g = rest[-1]
        xhat, r = _rms_stats(x)
        d = ds[0].astype(F32)
        for e in ds[1:]:
            d = d + e.astype(F32)
        if scale != 1.0:
            d = d * scale
        dg = _rsum8(d * xhat)
        dxhat = d * g
        dx = r * (dxhat - xhat * jnp.mean(dxhat * xhat, axis=-1, keepdims=True))
        if ad is not None:
            dx = dx + ad
        return [dx], [dg]

    tiled = [xin, *douts] + ([add] if add is not None else [])
    dx, dg = _rowwise(fn, tiled, [g], [(xin.shape[1], out_dtype)], [((8, xin.shape[1]), F32)], name=name)
    return dx, dg


def _silu_parts(g):
    sg = 1.0 / (1.0 + jnp.exp(-g))
    return g * sg, sg


def _swiglu_tiles(products, _):
    g, u = products
    return [g, u, _silu_parts(g)[0] * u]


def _swiglu_bwd_tiles(products, saved):
    da, = products
    g, u = saved[0].astype(F32), saved[1].astype(F32)
    s, sg = _silu_parts(g)
    return [da * u * (sg * (1.0 + g * (1.0 - sg))), da * s]


def scale_cols(y, s, name):
    def fn(i, y, s):
        return [y * s], []
    return _rowwise(fn, [y], [s], [(y.shape[1], F32)], [], name=name)[0]


def scale_cols_bwd(dm, y, s, name):
    def fn(i, dm, y, s):
        return [dm * s], [_rsum8(dm * y)]
    return _rowwise(fn, [dm, y], [s], [(y.shape[1], BF16)], [((8, y.shape[1]), F32)], name=name)


def loss_and_grad(y, target, name):
    n_feat = y.shape[1]

    def fn(i, y, t):
        e = y - t
        return [e * (1.0 / n_feat)], [_rsum8(e * e)]
    dy, part = _rowwise(fn, [y, target], [], [(n_feat, F32)], [((8, n_feat), F32)], name=name)
    return part, dy


def cast_into_slot(w3, layer, chip, name):
    _, n_rows, n_cols = w3.shape
    tr = _rows(n_rows, n_cols * 6)

    def body(chip_ref, w_ref, o_ref):
        o_ref[...] = w_ref[...].astype(BF16)

    grid_spec = pltpu.PrefetchScalarGridSpec(
        num_scalar_prefetch=1, grid=(n_rows // tr,),
        in_specs=[pl.BlockSpec((None, tr, n_cols), lambda i, chip_ref: (layer, i, 0))],
        out_specs=pl.BlockSpec((None, tr, n_cols), lambda i, chip_ref: (chip_ref[0], i, 0)))
    return pl.pallas_call(
        body, name=name, grid_spec=grid_spec, out_shape=jax.ShapeDtypeStruct((N_CHIPS, n_rows, n_cols), BF16),
        compiler_params=_params(("parallel",)),
    )(chip, w3)


def _adam_update(w, g, m, v):
    c1 = 1.0 / (1.0 - ADAM_B1 ** ADAM_STEP)
    c2 = 1.0 / (1.0 - ADAM_B2 ** ADAM_STEP)
    m = ADAM_B1 * m + (1.0 - ADAM_B1) * g
    v = ADAM_B2 * v + (1.0 - ADAM_B2) * (g * g)
    return -ADAM_LR * ((m * c1) / (jnp.sqrt(v * c2) + ADAM_EPS) + ADAM_WD * w), m, v


def adamw(w, g, m, v, name):
    n_l, n_rows, n_cols = w.shape
    tr = _rows(n_rows, n_cols * 4 * 7)

    def body(w_ref, g_ref, m_ref, v_ref, d_ref, mo_ref, vo_ref):
        d_ref[...], mo_ref[...], vo_ref[...] = _adam_update(w_ref[...], g_ref[...], m_ref[...], v_ref[...])

    spec = pl.BlockSpec((None, tr, n_cols), lambda l, i: (l, i, 0))
    shp = jax.ShapeDtypeStruct(w.shape, F32)
    return pl.pallas_call(
        body, name=name, grid=(n_l, n_rows // tr), in_specs=[spec] * 4, out_specs=[spec] * 3, out_shape=[shp] * 3,
        compiler_params=_params(("parallel", "parallel")),
    )(w, g, m, v)


def adamw_from_halves(w, own, recv, m, v, core, order, name):
    n_l, n_rows, n_cols = w.shape
    half = n_rows // 2
    tr = _rows(half, n_cols * 4 * 9)
    per = half // tr

    def body(core_ref, *refs):
        w_ref, m_ref, v_ref, order_ref = refs[:4]
        own_refs, recv_refs = refs[4:4 + n_l], refs[4 + n_l:4 + 2 * n_l]
        g_ref, d_ref, mo_ref, vo_ref = refs[4 + 2 * n_l:]
        l, h = pl.program_id(0), pl.program_id(1)
        mine = h == core_ref[0]
        g = jnp.where(mine, own_refs[0][...], recv_refs[0][...])
        for k in range(1, n_l):
            g = jnp.where(l == k, jnp.where(mine, own_refs[k][...], recv_refs[k][...]), g)
        g = g + order_ref[...]
        g_ref[...] = g
        d_ref[...], mo_ref[...], vo_ref[...] = _adam_update(w_ref[...], g, m_ref[...], v_ref[...])

    full = pl.BlockSpec((None, tr, n_cols), lambda l, h, i, core_ref: (l, h * per + i, 0))

    def piece(layer, is_own):
        def index(l, h, i, core_ref):
            used = (l == layer) & ((h == core_ref[0]) == is_own)
            return (jnp.where(used, i, 0), 0)
        return pl.BlockSpec((tr, n_cols), index)

    grid_spec = pltpu.PrefetchScalarGridSpec(
        num_scalar_prefetch=1, grid=(n_l, 2, per),
        in_specs=[full] * 3 + [pl.BlockSpec((1, 1), lambda l, h, i, core_ref: (0, 0))]
        + [piece(k, True) for k in range(n_l)] + [piece(k, False) for k in range(n_l)],
        out_specs=[full] * 4)
    return pl.pallas_call(
        body, name=name, grid_spec=grid_spec, out_shape=[jax.ShapeDtypeStruct(w.shape, F32)] * 4,
        compiler_params=_params(("parallel", "parallel", "parallel")),
    )(core, w, m, v, order, *own, *recv)


def rope_tables(n_tok):
    inv = ROPE_THETA ** (-jnp.arange(ROPE_HALF, dtype=F32) * 2.0 / (2 * ROPE_HALF))
    ang = jnp.arange(n_tok, dtype=F32)[:, None] * inv[None, :]
    cos, sin = jnp.cos(ang), jnp.sin(ang)
    rest = HEAD_DIM - 2 * ROPE_HALF
    one, zero, z8 = jnp.ones((n_tok, rest), F32), jnp.zeros((n_tok, rest), F32), jnp.zeros((n_tok, ROPE_HALF), F32)
    c = jnp.concatenate([cos, cos, one], axis=1)
    s1 = jnp.concatenate([-sin, z8, zero], axis=1)
    s2 = jnp.concatenate([z8, sin, zero], axis=1)
    two = lambda t: jnp.concatenate([t, t], axis=1)
    return two(c), two(s1), two(s2)


def rope_apply(x, tabs, n_col_blocks, inverse, out_dtype, name):
    n_tok = x.shape[0]
    tr = _rows(n_tok, LANES * 4 * 6)

    def body(x_ref, c_ref, s1_ref, s2_ref, o_ref):
        x = x_ref[...].astype(F32)
        if inverse:
            out = x * c_ref[...] + pltpu.roll(x * s1_ref[...], ROPE_HALF, 1) + pltpu.roll(x * s2_ref[...], LANES - ROPE_HALF, 1)
        else:
            out = x * c_ref[...] + pltpu.roll(x, LANES - ROPE_HALF, 1) * s1_ref[...] + pltpu.roll(x, ROPE_HALF, 1) * s2_ref[...]
        o_ref[...] = out.astype(o_ref.dtype)

    tab_spec = pl.BlockSpec((tr, LANES), lambda i, c: (i, 0))
    blk = pl.BlockSpec((tr, LANES), lambda i, c: (i, c))
    return pl.pallas_call(
        body, name=name, grid=(n_tok // tr, n_col_blocks), in_specs=[blk, tab_spec, tab_spec, tab_spec], out_specs=blk,
        out_shape=jax.ShapeDtypeStruct((n_tok, n_col_blocks * LANES), out_dtype), compiler_params=_params(("parallel", "parallel")),
    )(x, *tabs)


def _swa_probs(q, k, sink, n):
    rows = Q_PER_KV * BLOCK
    s = lax.dot_general(q, k, (((1,), (1,)), ((), ())), preferred_element_type=F32) * (HEAD_DIM ** -0.5)
    qi = lax.broadcasted_iota(jnp.int32, (rows, 2 * BLOCK), 0) & (BLOCK - 1)
    kj = lax.broadcasted_iota(jnp.int32, (rows, 2 * BLOCK), 1)
    rel = qi + BLOCK - kj
    valid = (rel >= 0) & (rel < BLOCK) & ((n > 0) | (kj >= BLOCK))
    s = jnp.where(valid, s, -1e30)
    m = jnp.maximum(jnp.max(s, axis=-1, keepdims=True), sink)
    e = jnp.exp(s - m)
    es = jnp.exp(sink - m)
    inv = 1.0 / (jnp.sum(e, axis=-1, keepdims=True) + es)
    return e * inv, es * inv


def _swa_specs(n_blocks):
    q_spec = pl.BlockSpec((Q_PER_KV, BLOCK, HEAD_DIM), lambda h, n: (h, n, 0))
    prev = pl.BlockSpec((None, BLOCK, HEAD_DIM), lambda h, n: (h, jnp.maximum(n - 1, 0), 0))
    cur = pl.BlockSpec((None, BLOCK, HEAD_DIM), lambda h, n: (h, n, 0))
    sink = pl.BlockSpec((None, Q_PER_KV * BLOCK, 1), lambda h, n: (h, 0, 0))
    return q_spec, prev, cur, sink


def swa_fwd(q, k, v, sink_rows, name):
    n_tok = q.shape[1]
    q_spec, prev, cur, sink = _swa_specs(n_tok // BLOCK)

    def body(q_ref, kp_ref, kc_ref, vp_ref, vc_ref, s_ref, o_ref):
        n = pl.program_id(1)
        qq = q_ref[...].reshape(Q_PER_KV * BLOCK, HEAD_DIM)
        kk = jnp.concatenate([kp_ref[...], kc_ref[...]], axis=0)
        vv = jnp.concatenate([vp_ref[...], vc_ref[...]], axis=0)
        p, _ = _swa_probs(qq, kk, s_ref[...], n)
        o = jnp.dot(p.astype(BF16), vv, preferred_element_type=F32)
        o_ref[...] = o.reshape(Q_PER_KV, BLOCK, HEAD_DIM).astype(o_ref.dtype)

    return pl.pallas_call(
        body, name=name, grid=(N_KV_HEADS, n_tok // BLOCK), in_specs=[q_spec, prev, cur, prev, cur, sink], out_specs=q_spec,
        out_shape=jax.ShapeDtypeStruct(q.shape, BF16), compiler_params=_params(("parallel", "parallel")),
    )(q, k, k, v, v, sink_rows)


def swa_bwd(q, k, v, sink_rows, do, name):
    n_tok = q.shape[1]
    nb = n_tok // BLOCK
    q_spec, prev, cur, sink = _swa_specs(nb)
    rows = Q_PER_KV * BLOCK

    def body(q_ref, kp_ref, kc_ref, vp_ref, vc_ref, s_ref, do_ref, dq_ref, dkp_ref, dkc_ref, dvp_ref, dvc_ref, ds_ref):
        n = pl.program_id(1)
        qq = q_ref[...].reshape(rows, HEAD_DIM)
        dd = do_ref[...].reshape(rows, HEAD_DIM)
        kk = jnp.concatenate([kp_ref[...], kc_ref[...]], axis=0)
        vv = jnp.concatenate([vp_ref[...], vc_ref[...]], axis=0)
        p, ps = _swa_probs(qq, kk, s_ref[...], n)
        dp = lax.dot_general(dd, vv, (((1,), (1,)), ((), ())), preferred_element_type=F32)
        delta = jnp.sum(p * dp, axis=-1, keepdims=True)
        ds = (p * (dp - delta) * (HEAD_DIM ** -0.5)).astype(BF16)
        dq = jnp.dot(ds, kk, preferred_element_type=F32)
        dk = lax.dot_general(ds, qq, (((0,), (0,)), ((), ())), preferred_element_type=F32)
        dv = lax.dot_general(p.astype(BF16), dd, (((0,), (0,)), ((), ())), preferred_element_type=F32)
        dq_ref[...] = dq.reshape(Q_PER_KV, BLOCK, HEAD_DIM).astype(dq_ref.dtype)
        dkp_ref[...] = dk[:BLOCK]
        dkc_ref[...] = dk[BLOCK:]
        dvp_ref[...] = dv[:BLOCK]
        dvc_ref[...] = dv[BLOCK:]
        dsink = jnp.broadcast_to(-ps * delta, (rows, LANES)).reshape(Q_PER_KV, BLOCK, LANES)
        ds_ref[...] = jnp.sum(dsink, axis=1)

    part = pl.BlockSpec((None, None, BLOCK, HEAD_DIM), lambda h, n: (h, n, 0, 0))
    part_shape = jax.ShapeDtypeStruct((N_KV_HEADS, nb, BLOCK, HEAD_DIM), F32)
    return pl.pallas_call(
        body, name=name, grid=(N_KV_HEADS, nb), in_specs=[q_spec, prev, cur, prev, cur, sink, q_spec],
        out_specs=[q_spec, part, part, part, part, pl.BlockSpec((None, None, Q_PER_KV, LANES), lambda h, n: (h, n, 0, 0))],
        out_shape=[jax.ShapeDtypeStruct(q.shape, BF16), part_shape, part_shape, part_shape, part_shape,
                   jax.ShapeDtypeStruct((N_KV_HEADS, nb, Q_PER_KV, LANES), F32)],
        compiler_params=_params(("parallel", "parallel")),
    )(q, k, k, v, v, sink_rows, do)


def _to_heads(t, n_heads):
    return t.reshape(t.shape[0], n_heads, HEAD_DIM).transpose(1, 0, 2)


def _from_heads(t):
    return t.transpose(1, 0, 2).reshape(t.shape[1], -1)


def _fold_kv_grad(prev, cur):
    shifted = jnp.concatenate([prev[:, 1:], jnp.zeros_like(prev[:, :1])], axis=1)
    tot = (cur + shifted).reshape(N_KV_HEADS, -1, HEAD_DIM)
    return _from_heads(tot)


def _x_probs(qh, kh):
    s = lax.dot_general(qh, kh, (((1,), (1,)), ((), ())), preferred_element_type=F32) * (X_HEAD_DIM ** -0.5)
    e = jnp.exp(s - jnp.max(s, axis=-1, keepdims=True))
    return e * (1.0 / jnp.sum(e, axis=-1, keepdims=True))


def xattn_fwd(q, k, v, name):
    n_tok, width = q.shape
    n_mem = k.shape[0]
    tq = _pick(n_tok, 512, 16)

    def body(q_ref, k_ref, v_ref, o_ref):
        for h in range(X_HEADS):
            cols = slice(h * X_HEAD_DIM, (h + 1) * X_HEAD_DIM)
            p = _x_probs(q_ref[:, cols], k_ref[:, cols])
            o_ref[:, cols] = jnp.dot(p.astype(BF16), v_ref[:, cols], preferred_element_type=F32).astype(o_ref.dtype)

    row = pl.BlockSpec((tq, width), lambda i: (i, 0))
    mem = pl.BlockSpec((n_mem, width), lambda i: (0, 0))
    return pl.pallas_call(
        body, name=name, grid=(n_tok // tq,), in_specs=[row, mem, mem], out_specs=row,
        out_shape=jax.ShapeDtypeStruct(q.shape, BF16), compiler_params=_params(("parallel",)),
    )(q, k, v)


def xattn_bwd(q, k, v, do, name):
    n_tok, width = q.shape
    n_mem = k.shape[0]
    tq = _pick(n_tok, 512, 16)

    def body(q_ref, k_ref, v_ref, do_ref, dq_ref, dk_ref, dv_ref):
        @pl.when(pl.program_id(0) == 0)
        def _():
            dk_ref[...] = jnp.zeros_like(dk_ref)
            dv_ref[...] = jnp.zeros_like(dv_ref)

        for h in range(X_HEADS):
            cols = slice(h * X_HEAD_DIM, (h + 1) * X_HEAD_DIM)
            qh, kh, vh, dh = q_ref[:, cols], k_ref[:, cols], v_ref[:, cols], do_ref[:, cols]
            p = _x_probs(qh, kh)
            dp = lax.dot_general(dh, vh, (((1,), (1,)), ((), ())), preferred_element_type=F32)
            delta = jnp.sum(p * dp, axis=-1, keepdims=True)
            ds = (p * (dp - delta) * (X_HEAD_DIM ** -0.5)).astype(BF16)
            dq_ref[:, cols] = jnp.dot(ds, kh, preferred_element_type=F32).astype(dq_ref.dtype)
            dk_ref[:, cols] += lax.dot_general(ds, qh, (((0,), (0,)), ((), ())), preferred_element_type=F32)
            dv_ref[:, cols] += lax.dot_general(p.astype(BF16), dh, (((0,), (0,)), ((), ())), preferred_element_type=F32)

    row = pl.BlockSpec((tq, width), lambda i: (i, 0))
    mem = pl.BlockSpec((n_mem, width), lambda i: (0, 0))
    return pl.pallas_call(
        body, name=name, grid=(n_tok // tq,), in_specs=[row, mem, mem, row], out_specs=[row, mem, mem],
        out_shape=[jax.ShapeDtypeStruct(q.shape, BF16), jax.ShapeDtypeStruct(k.shape, F32), jax.ShapeDtypeStruct(k.shape, F32)],
        compiler_params=_params(("arbitrary",)),
    )(q, k, v, do)


GELU_C = 0.7978845608028654
GELU_A = 0.044715


def _gelu_parts(x):
    x2 = x * x
    t = jnp.tanh(GELU_C * x * (1.0 + GELU_A * x2))
    y = 0.5 * x * (1.0 + t)
    dy = 0.5 * (1.0 + t) + 0.5 * x * (1.0 - t * t) * GELU_C * (1.0 + 3.0 * GELU_A * x2)
    return y, dy


def _sgu_norm(v, ln_g, ln_b):
    mu = jnp.mean(v, axis=-1, keepdims=True)
    vc = v - mu
    r = lax.rsqrt(jnp.mean(vc * vc, axis=-1, keepdims=True) + EPS)
    xhat = vc * r
    return xhat * ln_g + ln_b, xhat, r


def _causal_weights(w_ref):
    i = lax.broadcasted_iota(jnp.int32, (BLOCK, BLOCK), 0)
    j = lax.broadcasted_iota(jnp.int32, (BLOCK, BLOCK), 1)
    return [jnp.where(i >= j, w_ref[g], 0.0).astype(BF16) for g in range(SGU_GROUPS)]


def sgu_fwd(u_pre, v_pre, ln_g, ln_b, w_s, bias_rows, name):
    n_tok = u_pre.shape[0]
    tm = _pick(n_tok, 512, BLOCK)

    def body(u_ref, v_ref, g_ref, b_ref, w_ref, bb_ref, o_ref):
        vn, _, _ = _sgu_norm(_gelu_parts(v_ref[...])[0], g_ref[...], b_ref[...])
        vn = vn.astype(BF16)
        wc = _causal_weights(w_ref)
        for c in range(tm // BLOCK):
            rows = slice(c * BLOCK, (c + 1) * BLOCK)
            for g in range(SGU_GROUPS):
                cols = slice(g * LANES, (g + 1) * LANES)
                mixed = jnp.dot(wc[g], vn[rows, cols], preferred_element_type=F32) + bb_ref[g]
                u = _gelu_parts(u_ref[rows, cols])[0]
                o_ref[rows, cols] = (u * mixed).astype(o_ref.dtype)

    row = pl.BlockSpec((tm, SGU_WIDTH), lambda i: (i, 0))
    vec = pl.BlockSpec((1, SGU_WIDTH), lambda i: (0, 0))
    mat = pl.BlockSpec((SGU_GROUPS, BLOCK, LANES), lambda i: (0, 0, 0))
    return pl.pallas_call(
        body, name=name, grid=(n_tok // tm,), in_specs=[row, row, vec, vec, mat, mat], out_specs=row,
        out_shape=jax.ShapeDtypeStruct((n_tok, SGU_WIDTH), BF16), compiler_params=_params(("parallel",)),
    )(u_pre, v_pre, ln_g, ln_b, w_s, bias_rows)


def sgu_bwd(u_pre, v_pre, ln_g, ln_b, w_s, bias_rows, dgate, name):
    n_tok = u_pre.shape[0]
    tm = _pick(n_tok, 512, BLOCK)

    def body(u_ref, v_ref, g_ref, b_ref, w_ref, bb_ref, dg_ref, du_ref, dv_ref, dw_ref, db_ref, dlg_ref, dlb_ref, dvn_ref):
        @pl.when(pl.program_id(0) == 0)
        def _():
            dw_ref[...] = jnp.zeros_like(dw_ref)
            db_ref[...] = jnp.zeros_like(db_ref)
            dlg_ref[...] = jnp.zeros_like(dlg_ref)
            dlb_ref[...] = jnp.zeros_like(dlb_ref)

        gv, dgv = _gelu_parts(v_ref[...])
        vn, xhat, r = _sgu_norm(gv, g_ref[...], b_ref[...])
        vn = vn.astype(BF16)
        wc = _causal_weights(w_ref)
        for c in range(tm // BLOCK):
            rows = slice(c * BLOCK, (c + 1) * BLOCK)
            for g in range(SGU_GROUPS):
                cols = slice(g * LANES, (g + 1) * LANES)
                vt = vn[rows, cols]
                mixed = jnp.dot(wc[g], vt, preferred_element_type=F32) + bb_ref[g]
                u, du_dpre = _gelu_parts(u_ref[rows, cols])
                dgate_t = dg_ref[rows, cols].astype(F32)
                du_ref[rows, cols] = (dgate_t * mixed * du_dpre).astype(du_ref.dtype)
                dmix = dgate_t * u
                dmix_b = dmix.astype(BF16)
                db_ref[g] += dmix
                dw_ref[g] += lax.dot_general(dmix_b, vt, (((1,), (1,)), ((), ())), preferred_element_type=F32)
                dvn_ref[rows, cols] = lax.dot_general(wc[g], dmix_b, (((0,), (0,)), ((), ())), preferred_element_type=F32)
        dvn = dvn_ref[...]
        dlg_ref[...] += _rsum8(dvn * xhat)
        dlb_ref[...] += _rsum8(dvn)
        dxhat = dvn * g_ref[...]
        dgv_in = r * (dxhat - jnp.mean(dxhat, axis=-1, keepdims=True) - xhat * jnp.mean(dxhat * xhat, axis=-1, keepdims=True))
        dv_ref[...] = (dgv_in * dgv).astype(dv_ref.dtype)

    row = pl.BlockSpec((tm, SGU_WIDTH), lambda i: (i, 0))
    vec = pl.BlockSpec((1, SGU_WIDTH), lambda i: (0, 0))
    mat = pl.BlockSpec((SGU_GROUPS, BLOCK, LANES), lambda i: (0, 0, 0))
    part = pl.BlockSpec((8, SGU_WIDTH), lambda i: (0, 0))
    mat_shape = jax.ShapeDtypeStruct((SGU_GROUPS, BLOCK, LANES), F32)
    part_shape = jax.ShapeDtypeStruct((8, SGU_WIDTH), F32)
    act_shape = jax.ShapeDtypeStruct((n_tok, SGU_WIDTH), BF16)
    return pl.pallas_call(
        body, name=name, grid=(n_tok // tm,), in_specs=[row, row, vec, vec, mat, mat, row],
        out_specs=[row, row, mat, mat, part, part], out_shape=[act_shape, act_shape, mat_shape, mat_shape, part_shape, part_shape],
        scratch_shapes=[pltpu.VMEM((tm, SGU_WIDTH), F32)], compiler_params=_params(("arbitrary",)),
    )(u_pre, v_pre, ln_g, ln_b, w_s, bias_rows, dgate)


def _pool_tile(n_tok):
    return _pick(n_tok, 256, POOL_HALO)


def pool_fwd(h, name):
    n_tok, width = h.shape
    gw = width // len(POOL_WINDOWS)
    tm = _pool_tile(n_tok)
    per = tm // POOL_HALO

    def body(cur_ref, halo_ref, o_ref, buf_ref):
        i = pl.program_id(0)
        buf_ref[0:POOL_HALO, :] = jnp.where(i > 0, halo_ref[...], 0.0)
        buf_ref[POOL_HALO:, :] = cur_ref[...]
        tok = i * tm + lax.broadcasted_iota(jnp.int32, (tm, 1), 0)
        for g, w in enumerate(POOL_WINDOWS):
            cols = slice(g * gw, (g + 1) * gw)
            acc = buf_ref[POOL_HALO:, cols]
            for j in range(1, w):
                acc = acc + buf_ref[POOL_HALO - j:POOL_HALO - j + tm, cols]
            cnt = jnp.minimum(tok + 1, w).astype(F32)
            o_ref[:, cols] = (acc / cnt - cur_ref[:, cols]).astype(o_ref.dtype)

    return pl.pallas_call(
        body, name=name, grid=(n_tok // tm,),
        in_specs=[pl.BlockSpec((tm, width), lambda i: (i, 0)),
                  pl.BlockSpec((POOL_HALO, width), lambda i: (jnp.maximum(i * per - 1, 0), 0))],
        out_specs=pl.BlockSpec((tm, width), lambda i: (i, 0)), out_shape=jax.ShapeDtypeStruct(h.shape, BF16),
        scratch_shapes=[pltpu.VMEM((tm + POOL_HALO, width), F32)], compiler_params=_params(("parallel",)),
    )(h, h)


def pool_bwd(dp, name):
    n_tok, width = dp.shape
    gw = width // len(POOL_WINDOWS)
    tm = _pool_tile(n_tok)
    per = tm // POOL_HALO
    n_steps = n_tok // tm

    def body(cur_ref, halo_ref, o_ref, buf_ref):
        i = pl.program_id(0)
        tok = i * tm + lax.broadcasted_iota(jnp.int32, (tm, 1), 0)
        for g, w in enumerate(POOL_WINDOWS):
            cols = slice(g * gw, (g + 1) * gw)
            cnt = jnp.minimum(tok + 1, w).astype(F32)
            buf_ref[0:tm, cols] = cur_ref[:, cols] / cnt
            buf_ref[tm:, cols] = jnp.where(i < n_steps - 1, halo_ref[:, cols] / float(w), 0.0)
        for g, w in enumerate(POOL_WINDOWS):
            cols = slice(g * gw, (g + 1) * gw)
            acc = buf_ref[0:tm, cols]
            for j in range(1, w):
                acc = acc + buf_ref[j:j + tm, cols]
            o_ref[:, cols] = acc - cur_ref[:, cols]

    return pl.pallas_call(
        body, name=name, grid=(n_steps,),
        in_specs=[pl.BlockSpec((tm, width), lambda i: (i, 0)),
                  pl.BlockSpec((POOL_HALO, width), lambda i: (jnp.minimum((i + 1) * per, n_tok // POOL_HALO - 1), 0))],
        out_specs=pl.BlockSpec((tm, width), lambda i: (i, 0)), out_shape=jax.ShapeDtypeStruct(dp.shape, F32),
        scratch_shapes=[pltpu.VMEM((tm + POOL_HALO, width), F32)], compiler_params=_params(("parallel",)),
    )(dp, dp)


def _ffn_fwd(x, ga, gb, wg, wu, wd, tag):
    h = rms_fwd(x, ga, BF16, f"{tag}_prenorm")
    G, U, A = _mm(h, wg, more=[(h, wu)], batch="map", epilogue=_swiglu_tiles, out_dtypes=[BF16] * 3, name=f"{tag}_gate_up")
    y = _mm(A, wd, batch="reduce", name=f"{tag}_down")
    return postnorm_res(x, y, gb, 0.5, f"{tag}_postnorm"), (x, h, G, U, A, y)


def _ffn_bwd(res, ga, gb, wg, wu, wd, dx2, tag):
    x, h, G, U, A, y = res
    dy, dgb = rms_bwd(y, gb, [dx2], 0.5, None, BF16, f"{tag}_postnorm_bwd")
    dG, dU = _mm(dy, wd, tb=True, batch="map", extras=[G, U], epilogue=_swiglu_bwd_tiles, out_dtypes=[BF16] * 2, name=f"{tag}_down_dx")
    dwd = _mm(A, dy, ta=True, batch="map", name=f"{tag}_down_dw", out_dtype=BF16)
    dwg = _mm(h, dG, ta=True, batch="map", name=f"{tag}_gate_dw", out_dtype=BF16)
    dwu = _mm(h, dU, ta=True, batch="map", name=f"{tag}_up_dw", out_dtype=BF16)
    dh = _mm(dG, wg, tb=True, batch="reduce", more=[(dU, wu)], name=f"{tag}_gate_up_dx")
    dx, dga = rms_bwd(x, ga, [dh], 1.0, dx2, F32, f"{tag}_prenorm_bwd")
    return dx, dga, dgb, dwg, dwu, dwd


def _sink_rows(sinks):
    return jnp.repeat(sinks.reshape(N_KV_HEADS, Q_PER_KV), BLOCK, axis=1)[..., None]


def _attn_sgu_fwd(x, g_pre, g_post, w_in, w_out, sinks, ln_g, ln_b, sgu_w, bias_rows, tabs, tag):
    h = rms_fwd(x, g_pre, BF16, f"{tag}_prenorm")
    z = _mm(h, w_in, name=f"{tag}_in")
    qk = rope_apply(z, tabs, QK_WIDTH // LANES, False, BF16, f"{tag}_rope")
    q = _to_heads(qk[:, :ATTN_WIDTH], N_Q_HEADS)
    k = _to_heads(qk[:, ATTN_WIDTH:], N_KV_HEADS)
    v = _to_heads(z[:, QK_WIDTH:QK_WIDTH + KV_WIDTH].astype(BF16), N_KV_HEADS)
    o = swa_fwd(q, k, v, _sink_rows(sinks), f"{tag}_swa")
    u_pre = z[:, QK_WIDTH + KV_WIDTH:QK_WIDTH + KV_WIDTH + SGU_WIDTH]
    v_pre = z[:, QK_WIDTH + KV_WIDTH + SGU_WIDTH:]
    gate = sgu_fwd(u_pre, v_pre, ln_g, ln_b, sgu_w, bias_rows, f"{tag}_sgu")
    cat = jnp.concatenate([_from_heads(o), gate], axis=1)
    m = _mm(cat, w_out, name=f"{tag}_out")
    return postnorm_res(x, m, g_post, 1.0, f"{tag}_postnorm"), (x, h, q, k, v, u_pre, v_pre, cat, m)


def _attn_sgu_bwd(res, g_pre, g_post, w_in, w_out, sinks, ln_g, ln_b, sgu_w, bias_rows, tabs, dx2, tag):
    x, h, q, k, v, u_pre, v_pre, cat, m = res
    dm, dg_post = rms_bwd(m, g_post, [dx2], 1.0, None, BF16, f"{tag}_postnorm_bwd")
    dcat = _mm(dm, w_out, tb=True, out_dtype=BF16, name=f"{tag}_out_dx")
    dw_out = _mm(cat, dm, ta=True, name=f"{tag}_out_dw", out_dtype=BF16)
    do = _to_heads(dcat[:, :ATTN_WIDTH], N_Q_HEADS)
    dq, dkp, dkc, dvp, dvc, dsink = swa_bwd(q, k, v, _sink_rows(sinks), do, f"{tag}_swa_bwd")
    d_sinks = jnp.sum(dsink[..., 0], axis=1).reshape(1, N_Q_HEADS)
    dqk_rot = jnp.concatenate([_from_heads(dq).astype(F32), _fold_kv_grad(dkp, dkc)], axis=1)
    dqk = rope_apply(dqk_rot, tabs, QK_WIDTH // LANES, True, BF16, f"{tag}_rope_bwd")
    dv = _fold_kv_grad(dvp, dvc).astype(BF16)
    du_pre, dv_pre, dw_s, dbias, dlg, dlb = sgu_bwd(u_pre, v_pre, ln_g, ln_b, sgu_w, bias_rows, dcat[:, ATTN_WIDTH:], f"{tag}_sgu_bwd")
    dz = jnp.concatenate([dqk, dv, du_pre, dv_pre], axis=1)
    dw_in = _mm(h, dz, ta=True, name=f"{tag}_in_dw", out_dtype=BF16)
    dh = _mm(dz, w_in, tb=True, name=f"{tag}_in_dx")
    dx, dg_pre = rms_bwd(x, g_pre, [dh], 1.0, dx2, F32, f"{tag}_prenorm_bwd")
    causal = jnp.tril(jnp.ones((BLOCK, BLOCK), F32))
    small = dict(attn_sinks=d_sinks, sgu_ln_g=jnp.sum(dlg, axis=0, keepdims=True), sgu_ln_b=jnp.sum(dlb, axis=0, keepdims=True),
                 sgu_w=(dw_s * causal[None])[None], sgu_b=jnp.sum(dbias, axis=-1)[None])
    return dx, dg_pre, dg_post, dw_in, dw_out, small


def _pool_mix_fwd(x, g_pre, g_post, pool_w, pool_scale, tag):
    hf = rms_fwd(x, g_pre, F32, f"{tag}_prenorm")
    pooled = pool_fwd(hf, f"{tag}_pool")
    n_g = len(POOL_WINDOWS)
    ypre = _mm(pooled, pool_w, batch="map", groups=n_g, a_cb=True, o_cb=True, name=f"{tag}_proj")
    m = scale_cols(ypre, pool_scale, f"{tag}_scale")
    return postnorm_res(x, m, g_post, 1.0, f"{tag}_postnorm"), (x, pooled, ypre, m)


def _pool_mix_bwd(res, g_pre, g_post, pool_w, pool_scale, dx2, tag):
    x, pooled, ypre, m = res
    n_g = len(POOL_WINDOWS)
    dm, dg_post = rms_bwd(m, g_post, [dx2], 1.0, None, F32, f"{tag}_postnorm_bwd")
    dypre, dscale = scale_cols_bwd(dm, ypre, pool_scale, f"{tag}_scale_bwd")
    dpooled = _mm(dypre, pool_w, tb=True, batch="map", groups=n_g, a_cb=True, o_cb=True, name=f"{tag}_proj_dx")
    dpw = _mm(pooled, dypre, ta=True, batch="map", groups=n_g, a_cb=True, b_cb=True, name=f"{tag}_proj_dw", out_dtype=BF16)
    dhf = pool_bwd(dpooled, f"{tag}_pool_bwd")
    dx, dg_pre = rms_bwd(x, g_pre, [dhf], 1.0, dx2, F32, f"{tag}_prenorm_bwd")
    return dx, dg_pre, dg_post, dpw, jnp.sum(dscale, axis=0, keepdims=True)


def _xattn_fwd(x, mem, g_pre, g_post, g_mem, wq, wk, wv, wo, tag):
    h = rms_fwd(x, g_pre, BF16, f"{tag}_prenorm")
    mem_n = rms_fwd(mem, g_mem, BF16, f"{tag}_memnorm")
    q = _mm(h, wq, out_dtype=BF16, name=f"{tag}_q")
    k = _mm(mem_n, wk, out_dtype=BF16, name=f"{tag}_k")
    v = _mm(mem_n, wv, out_dtype=BF16, name=f"{tag}_v")
    o = xattn_fwd(q, k, v, f"{tag}_core")
    r = _mm(o, wo, name=f"{tag}_o")
    return postnorm_res(x, r, g_post, 1.0, f"{tag}_postnorm"), (x, h, mem_n, q, k, v, o, r)


def _xattn_bwd(res, mem, g_pre, g_post, g_mem, wq, wk, wv, wo, dx2, tag):
    x, h, mem_n, q, k, v, o, r = res
    dr, dg_post = rms_bwd(r, g_post, [dx2], 1.0, None, BF16, f"{tag}_postnorm_bwd")
    do = _mm(dr, wo, tb=True, out_dtype=BF16, name=f"{tag}_o_dx")
    dwo = _mm(o, dr, ta=True, name=f"{tag}_o_dw", out_dtype=BF16)
    dq, dk, dv = xattn_bwd(q, k, v, do, f"{tag}_core_bwd")
    dk, dv = dk.astype(BF16), dv.astype(BF16)
    dwq = _mm(h, dq, ta=True, name=f"{tag}_q_dw", out_dtype=BF16)
    dwk = _mm(mem_n, dk, ta=True, name=f"{tag}_k_dw", out_dtype=BF16)
    dwv = _mm(mem_n, dv, ta=True, name=f"{tag}_v_dw", out_dtype=BF16)
    dh = _mm(dq, wq, tb=True, name=f"{tag}_q_dx")
    dmem1 = _mm(dk, wk, tb=True, name=f"{tag}_k_dx")
    dmem2 = _mm(dv, wv, tb=True, name=f"{tag}_v_dx")
    _, dg_mem = rms_bwd(mem, g_mem, [dmem1, dmem2], 1.0, None, BF16, f"{tag}_memnorm_bwd")
    dx, dg_pre = rms_bwd(x, g_pre, [dh], 1.0, dx2, F32, f"{tag}_prenorm_bwd")
    return dx, dg_pre, dg_post, dg_mem, dwq, dwk, dwv, dwo


def _rowsum8(part):
    return jnp.sum(part, axis=0, keepdims=True)


def device_step(x, mem, target, norms, mem_norm, small, fetch, emit):
    n_tok = x.shape[0]
    tabs = rope_tables(n_tok)
    bias_rows = jnp.broadcast_to(small["sgu_b"][:, :, None], (SGU_GROUPS, BLOCK, LANES))
    gn = lambda l, i: norms[l, i][None, :]
    gm = lambda l: mem_norm[l][None, :]
    mix0 = (small["attn_sinks"], small["sgu_ln_g"], small["sgu_ln_b"], small["sgu_w"], bias_rows, tabs)

    wts, saved = {}, []
    for l in range(2):
        wts["ffn1", l], tok = fetch(("ffn1", l), x)
        x, r1 = _ffn_fwd(x, gn(l, 0) + tok, gn(l, 1), *wts["ffn1", l], f"l{l}_ffn1")
        wts["mix", l], tok = fetch(("mix", l), x)
        if l == 0:
            x, r2 = _attn_sgu_fwd(x, gn(l, 2) + tok, gn(l, 3), *wts["mix", l], *mix0, f"l{l}_mix")
        else:
            x, r2 = _pool_mix_fwd(x, gn(l, 2) + tok, gn(l, 3), *wts["mix", l], small["pool_scale"], f"l{l}_mix")
        wts["xattn", l], tok = fetch(("xattn", l), x)
        x, r3 = _xattn_fwd(x, mem, gn(l, 4) + tok, gn(l, 5), gm(l), *wts["xattn", l], f"l{l}_xattn")
        wts["ffn2", l], tok = fetch(("ffn2", l), x)
        x, r4 = _ffn_fwd(x, gn(l, 6) + tok, gn(l, 7), *wts["ffn2", l], f"l{l}_ffn2")
        saved.append((r1, r2, r3, r4))

    loss_part, dx = loss_and_grad(x, target, "loss")

    g_norm_rows = [[None] * 8, [None] * 8]
    g_mem_rows = [None, None]
    g_small = {}
    tok = 0.0
    for l in (1, 0):
        r1, r2, r3, r4 = saved[l]
        dx, g_norm_rows[l][6], g_norm_rows[l][7], dwg, dwu, dwd = _ffn_bwd(r4, gn(l, 6), gn(l, 7) + tok, *wts["ffn2", l], dx, f"l{l}_ffn2")
        tok = emit(("ffn2", l), dict(ffn2_wg=dwg, ffn2_wu=dwu, ffn2_wd=dwd), dx)
        dx, g_norm_rows[l][4], g_norm_rows[l][5], g_mem_rows[l], dwq, dwk, dwv, dwo = _xattn_bwd(
            r3, mem, gn(l, 4), gn(l, 5) + tok, gm(l), *wts["xattn", l], dx, f"l{l}_xattn")
        tok = emit(("xattn", l), dict(x_wq=dwq, x_wk=dwk, x_wv=dwv, x_wo=dwo), dx)
        if l == 0:
            dx, g_norm_rows[l][2], g_norm_rows[l][3], dw_in, dw_out, sm = _attn_sgu_bwd(
                r2, gn(l, 2), gn(l, 3) + tok, *wts["mix", l], *mix0, dx, f"l{l}_mix")
            tok = emit(("mix", l), dict(mix_w_in=dw_in, mix_w_out=dw_out), dx)
            g_small.update(sm)
        else:
            dx, g_norm_rows[l][2], g_norm_rows[l][3], dpw, dscale = _pool_mix_bwd(
                r2, gn(l, 2), gn(l, 3) + tok, *wts["mix", l], small["pool_scale"], dx, f"l{l}_mix")
            tok = emit(("mix", l), dict(pool_w=dpw), dx)
            g_small["pool_scale"] = dscale
        dx, g_norm_rows[l][0], g_norm_rows[l][1], dwg, dwu, dwd = _ffn_bwd(r1, gn(l, 0), gn(l, 1) + tok, *wts["ffn1", l], dx, f"l{l}_ffn1")
        tok = emit(("ffn1", l), dict(ffn1_wg=dwg, ffn1_wu=dwu, ffn1_wd=dwd), dx)
    g_norms = jnp.stack([jnp.concatenate([_rowsum8(p) for p in g_norm_rows[l]], axis=0) for l in range(2)])
    g_mem_norm = jnp.concatenate([_rowsum8(p) for p in g_mem_rows], axis=0)
    return loss_part, dx, g_small, g_norms, g_mem_norm, tok


ANY = pl.BlockSpec(memory_space=pl.ANY)


def _place():
    x, y, c = lax.axis_index("x"), lax.axis_index("y"), lax.axis_index("c")
    other_chips = [(1 - x, y), (x, 1 - y), (1 - x, 1 - y)]
    return x, y, c, other_chips


def _half_rows(core, n_rows):
    half = n_rows // 2
    return pl.ds(pl.multiple_of(core * half, 16), half)


def _remote(src, dst, send_sem, recv_sem, device):
    return pltpu.make_async_remote_copy(src_ref=src, dst_ref=dst, send_sem=send_sem, recv_sem=recv_sem,
                                        device_id=device, device_id_type=MESH)


HBM =pl.BlockSpec(memory_space=pltpu.HBM)
SEM = pl.BlockSpec(memory_space=pltpu.SEMAPHORE)
DATAFLOW = pltpu.SideEffectType.DATAFLOW_SIDE_EFFECTING


def _chip_copies(bufs, send_sems, recv_sems):
    x, y, c, chips = _place()
    me = 2 * x + y
    sends, arrivals = [], []
    for i, buf in enumerate(bufs):
        rows = _half_rows(c, buf.shape[1])
        for r, (px, py) in enumerate(chips):
            mine, theirs = buf.at[me, rows], buf.at[2 * px + py, rows]
            sends.append(_remote(mine, mine, send_sems.at[3 * i + r], recv_sems.at[3 * i + r], (px, py, c)))
            arrivals.append(_remote(theirs, theirs, send_sems.at[3 * i + r], recv_sems.at[3 * i + r], (px, py, c)))
    return sends, arrivals


def gather_start(slots, after, name):
    n = len(slots)

    def body(*refs):
        send_sems, recv_sems = refs[n + 1], refs[n + 2]
        bufs, token = refs[n + 3:2 * n + 3], refs[2 * n + 3]
        sends, _ = _chip_copies(bufs, send_sems, recv_sems)
        for cp in sends:
            cp.start()
        token[...] = jnp.zeros_like(token)

    res = pl.pallas_call(
        body, name=name, in_specs=[HBM] * n + [ANY],
        out_specs=[SEM, SEM] + [HBM] * n + [pl.BlockSpec(memory_space=pltpu.VMEM)],
        out_shape=[pltpu.SemaphoreType.DMA((3 * n,)), pltpu.SemaphoreType.DMA((3 * n,))]
        + [pltpu.HBM(s.shape, s.dtype) for s in slots] + [jax.ShapeDtypeStruct((8, LANES), F32)],
        input_output_aliases={i: 2 + i for i in range(n)}, compiler_params=pltpu.CompilerParams(has_side_effects=DATAFLOW),
    )(*[pltpu.with_memory_space_constraint(s, pltpu.HBM) for s in slots], after)
    return res[0], res[1], list(res[2:2 + n]), res[2 + n]


def gather_wait(send_sems, recv_sems, bufs, after, name):
    n = len(bufs)

    def body(*refs):
        send_ref, recv_ref = refs[n], refs[n + 1]
        _, arrivals = _chip_copies(refs[:n], send_ref, recv_ref)
        for cp in arrivals:
            cp.wait_send()
            cp.wait_recv()

    return pl.pallas_call(
        body, name=name, in_specs=[HBM] * n + [SEM, SEM, ANY], out_specs=[HBM] * n,
        out_shape=[pltpu.HBM(b.shape, b.dtype) for b in bufs], input_output_aliases={i: i for i in range(n)},
        compiler_params=pltpu.CompilerParams(has_side_effects=DATAFLOW),
    )(*bufs, send_sems, recv_sems, after)


def forward_to_sibling(bufs, name):
    n = len(bufs)

    def body(*refs):
        out = refs[n:2 * n]
        send_sems, recv_sems = refs[2 * n:]
        x, y, c, chips = _place()
        copies = []
        for i in range(n):
            for r, (px, py) in enumerate(chips):
                mine = out[i].at[2 * px + py, _half_rows(c, out[i].shape[1])]
                copies.append(_remote(mine, mine, send_sems.at[3 * i + r], recv_sems.at[3 * i + r], (x, y, 1 - c)))
                copies[-1].start()
        for i in range(n):
            for r, (px, py) in enumerate(chips):
                theirs = out[i].at[2 * px + py, _half_rows(1 - c, out[i].shape[1])]
                _remote(theirs, theirs, send_sems.at[3 * i + r], recv_sems.at[3 * i + r], (x, y, 1 - c)).wait_recv()
        for cp in copies:
            cp.wait_send()

    return pl.pallas_call(
        body, name=name, in_specs=[ANY] * n, out_specs=[ANY] * n, input_output_aliases={i: i for i in range(n)},
        out_shape=[jax.ShapeDtypeStruct(b.shape, b.dtype) for b in bufs],
        scratch_shapes=[pltpu.SemaphoreType.DMA((3 * n,)), pltpu.SemaphoreType.DMA((3 * n,))],
    )(*bufs)


def swap_other_halves(grads, name):
    n = len(grads)

    def body(*refs):
        src, out = refs[:n], refs[n:2 * n]
        send_sems, recv_sems = refs[2 * n:]
        x, y, c, _ = _place()
        copies = [_remote(src[i].at[:, _half_rows(1 - c, src[i].shape[1])], out[i], send_sems.at[i], recv_sems.at[i], (x, y, 1 - c))
                  for i in range(n)]
        for cp in copies:
            cp.start()
        for cp in copies:
            cp.wait()

    return pl.pallas_call(
        body, name=name, in_specs=[ANY] * n, out_specs=[ANY] * n,
        out_shape=[jax.ShapeDtypeStruct((g.shape[0], g.shape[1] // 2, g.shape[2]), g.dtype) for g in grads],
        scratch_shapes=[pltpu.SemaphoreType.DMA((n,)), pltpu.SemaphoreType.DMA((n,))],
    )(*grads)


def add_own_half(g, p, core, name):
    n_j, n_rows, n_cols = g.shape
    half = n_rows // 2
    tr = _rows(half, n_cols * 10)

    def body(c_ref, g_ref, p_ref, o_ref):
        o_ref[...] = (g_ref[...] + p_ref[...]).astype(o_ref.dtype)

    grid_spec = pltpu.PrefetchScalarGridSpec(
        num_scalar_prefetch=1, grid=(n_j, half // tr),
        in_specs=[pl.BlockSpec((None, None, tr, n_cols), lambda j, i, c_ref: (j, c_ref[0], i, 0)),
                  pl.BlockSpec((None, tr, n_cols), lambda j, i, c_ref: (j, i, 0))],
        out_specs=pl.BlockSpec((None, tr, n_cols), lambda j, i, c_ref: (j, i, 0)))
    return pl.pallas_call(
        body, name=name, grid_spec=grid_spec, out_shape=jax.ShapeDtypeStruct((n_j, half, n_cols), BF16),
        compiler_params=_params(("parallel", "parallel")),
    )(core, g.reshape(n_j, 2, half, n_cols), p)


def _scatter_copies(parts, lands, send_sems, recv_sems):
    x, y, c, chips = _place()
    me = 2 * x + y
    sends, arrivals = [], []
    for i in range(len(parts)):
        for r, (px, py) in enumerate(chips):
            theirs = lands[i].at[2 * px + py]
            sends.append(_remote(parts[i].at[2 * px + py], lands[i].at[me], send_sems.at[3 * i + r], recv_sems.at[3 * i + r], (px, py, c)))
            arrivals.append(_remote(theirs, theirs, send_sems.at[3 * i + r], recv_sems.at[3 * i + r], (px, py, c)))
    return sends, arrivals


def scatter_start(parts, after, name):
    n = len(parts)

    def body(*refs):
        send_sems, recv_sems = refs[2 * n + 1], refs[2 * n + 2]
        src, lands, token = refs[2 * n + 3:3 * n + 3], refs[3 * n + 3:4 * n + 3], refs[4 * n + 3]
        sends, _ = _scatter_copies(src, lands, send_sems, recv_sems)
        for cp in sends:
            cp.start()
        token[...] = jnp.zeros_like(token)

    fresh = [pltpu.with_memory_space_constraint(lax.empty(p.shape, p.dtype), pltpu.HBM) for p in parts]
    res = pl.pallas_call(
        body, name=name, in_specs=[HBM] * (2 * n) + [ANY],
        out_specs=[SEM, SEM] + [HBM] * (2 * n) + [pl.BlockSpec(memory_space=pltpu.VMEM)],
        out_shape=[pltpu.SemaphoreType.DMA((3 * n,)), pltpu.SemaphoreType.DMA((3 * n,))]
        + [pltpu.HBM(p.shape, p.dtype) for p in parts] * 2 + [jax.ShapeDtypeStruct((8, LANES), F32)],
        input_output_aliases={i: 2 + i for i in range(2 * n)}, compiler_params=pltpu.CompilerParams(has_side_effects=DATAFLOW),
    )(*[pltpu.with_memory_space_constraint(p, pltpu.HBM) for p in parts], *fresh, after)
    return res[0], res[1], list(res[2:2 + n]), list(res[2 + n:2 + 2 * n]), res[2 + 2 * n]


def scatter_wait(send_sems, recv_sems, parts, lands, after, name):
    n = len(parts)

    def body(*refs):
        _, arrivals = _scatter_copies(refs[:n], refs[n:2 * n], refs[2 * n], refs[2 * n + 1])
        for cp in arrivals:
            cp.wait_send()
            cp.wait_recv()

    res = pl.pallas_call(
        body, name=name, in_specs=[HBM] * (2 * n) + [SEM, SEM, ANY], out_specs=[HBM] * (2 * n),
        out_shape=[pltpu.HBM(p.shape, p.dtype) for p in parts] * 2, input_output_aliases={i: i for i in range(2 * n)},
        compiler_params=pltpu.CompilerParams(has_side_effects=DATAFLOW),
    )(*parts, *lands, send_sems, recv_sems, after)
    return list(res[:n]), list(res[n:])


def sum_over_chips(own, got, chip, name):
    n_s, n_rows, n_cols = got.shape
    tr = _rows(n_rows, n_cols * (got.dtype.itemsize * (n_s + 1) + 4))

    def body(chip_ref, own_ref, *refs):
        got_refs, o_ref = refs[:n_s], refs[n_s]
        me = chip_ref[0]
        acc = jnp.where(me == 0, own_ref[...], got_refs[0][...]).astype(F32)
        for k in range(1, n_s):
            acc = acc + jnp.where(me == k, own_ref[...], got_refs[k][...]).astype(F32)
        o_ref[...] = acc

    def slot(k):
        return pl.BlockSpec((None, tr, n_cols), lambda i, chip_ref: (jnp.where(chip_ref[0] == k, (k + 1) % n_s, k), i, 0))

    grid_spec = pltpu.PrefetchScalarGridSpec(
        num_scalar_prefetch=1, grid=(n_rows // tr,),
        in_specs=[pl.BlockSpec((None, tr, n_cols), lambda i, chip_ref: (chip_ref[0], i, 0))] + [slot(k) for k in range(n_s)],
        out_specs=pl.BlockSpec((tr, n_cols), lambda i, chip_ref: (i, 0)))
    return pl.pallas_call(
        body, name=name, grid_spec=grid_spec, out_shape=jax.ShapeDtypeStruct((n_rows, n_cols), F32),
        compiler_params=_params(("parallel",)),
    )(chip, own, *([got] * n_s))


def swap_with_sibling(arrays, name):
    n = len(arrays)

    def body(*refs):
        src, out = refs[:n], refs[n:2 * n]
        send_sems, recv_sems = refs[2 * n:]
        x, y, c, _ = _place()
        copies = [_remote(src[i], out[i], send_sems.at[i], recv_sems.at[i], (x, y, 1 - c)) for i in range(n)]
        for cp in copies:
            cp.start()
        for cp in copies:
            cp.wait()

    return pl.pallas_call(
        body, name=name, in_specs=[ANY] * n, out_specs=[ANY] * n, out_shape=[jax.ShapeDtypeStruct(a.shape, a.dtype) for a in arrays],
        scratch_shapes=[pltpu.SemaphoreType.DMA((n,)), pltpu.SemaphoreType.DMA((n,))],
    )(*arrays)


def sum_slots(q, name):
    n_s, n_rows, n_cols = q.shape
    tr = _rows(n_rows, n_cols * (q.dtype.itemsize * n_s + 4))

    def body(q_ref, o_ref):
        acc = q_ref[0].astype(F32)
        for s in range(1, n_s):
            acc = acc + q_ref[s].astype(F32)
        o_ref[...] = acc

    return pl.pallas_call(
        body, name=name, grid=(n_rows // tr,), in_specs=[pl.BlockSpec((n_s, tr, n_cols), lambda i: (0, i, 0))],
        out_specs=pl.BlockSpec((tr, n_cols), lambda i: (i, 0)), out_shape=jax.ShapeDtypeStruct((n_rows, n_cols), F32),
        compiler_params=_params(("parallel",)),
    )(q)


def gather_all_devices(s, name):
    def body(s_ref, o_ref, send_sems, recv_sems, local_sem):
        x, y, c, _ = _place()
        me = 4 * x + 2 * y + c
        local = pltpu.make_async_copy(s_ref, o_ref.at[me], local_sem)
        local.start()
        sends = []
        for f in range(1, N_DEV):
            px, py, pc = x ^ (f >> 2), y ^ ((f >> 1) & 1), c ^ (f & 1)
            sends.append(_remote(s_ref, o_ref.at[me], send_sems.at[f - 1], recv_sems.at[f - 1], (px, py, pc)))
            sends[-1].start()
        for f in range(1, N_DEV):
            px, py, pc = x ^ (f >> 2), y ^ ((f >> 1) & 1), c ^ (f & 1)
            landed = o_ref.at[4 * px + 2 * py + pc]
            _remote(landed, landed, send_sems.at[f - 1], recv_sems.at[f - 1], (px, py, pc)).wait_recv()
        for cp in sends:
            cp.wait_send()
        local.wait()

    return pl.pallas_call(
        body, name=name, in_specs=[ANY], out_specs=ANY, out_shape=jax.ShapeDtypeStruct((N_DEV,) + s.shape, s.dtype),
        scratch_shapes=[pltpu.SemaphoreType.DMA((N_DEV - 1,)), pltpu.SemaphoreType.DMA((N_DEV - 1,)), pltpu.SemaphoreType.DMA],
    )(s)


FFN_NAMES = ("ffn1_wg", "ffn1_wu", "ffn1_wd", "ffn2_wg", "ffn2_wu", "ffn2_wd")
XATTN_NAMES = ("x_wq", "x_wk", "x_wv", "x_wo")
BIG_NAMES = FFN_NAMES + XATTN_NAMES + ("mix_w_in", "mix_w_out", "pool_w")
SMALL_NAMES = ("norms", "mem_norm", "attn_sinks", "sgu_ln_g", "sgu_ln_b", "sgu_w", "sgu_b", "pool_scale")
WEIGHT_ORDER = ("norms", "mem_norm") + BIG_NAMES[:-1] + ("attn_sinks", "sgu_ln_g", "sgu_ln_b", "sgu_w", "sgu_b", "pool_w", "pool_scale")
COLUMN_CUT = ("x_wo", "mix_w_in")
N_POOL = len(POOL_WINDOWS)
BLOCK_ORDER = (("ffn1", 0), ("mix", 0), ("xattn", 0), ("ffn2", 0), ("ffn1", 1), ("mix", 1), ("xattn", 1), ("ffn2", 1))
PREFETCH_AT = ((1, 2), (3,), (), (4,), (5, 6), (7,), (), ())


def block_weight_names(kind, layer):
    if kind == "mix":
        return ("mix_w_in", "mix_w_out") if layer == 0 else ("pool_w",)
    return XATTN_NAMES if kind == "xattn" else tuple(f"{kind}_{part}" for part in ("wg", "wu", "wd"))


def _to_matmul_layout(name, g):
    n_j, n_rows, n_cols = g.shape
    if name in COLUMN_CUT:
        return g.transpose(1, 0, 2).reshape(n_rows, n_j * n_cols)
    if name == "pool_w":
        return g.reshape(n_j, N_POOL, n_rows // N_POOL, n_cols).transpose(1, 0, 2, 3).reshape(N_POOL, n_j * n_rows // N_POOL, n_cols)
    if name in ("x_wq", "x_wk", "x_wv", "mix_w_out"):
        return g.reshape(n_j * n_rows, n_cols)
    return g


def _from_matmul_layout(name, d):
    if name in COLUMN_CUT:
        n_rows, wide = d.shape
        return d.reshape(n_rows, N_CHIPS, wide // N_CHIPS).transpose(1, 0, 2)
    if name == "pool_w":
        n_g, n_in, n_cols = d.shape
        return d.reshape(n_g, N_CHIPS, n_in // N_CHIPS, n_cols).transpose(1, 0, 2, 3).reshape(N_CHIPS, n_g * n_in // N_CHIPS, n_cols)
    if name in ("x_wq", "x_wk", "x_wv", "mix_w_out"):
        return d.reshape(N_CHIPS, d.shape[0] // N_CHIPS, d.shape[1])
    return d


def _as3(w):
    return w.reshape(w.shape[0], -1, w.shape[-1])


def _pack(arrays, row_multiple):
    flat = jnp.concatenate([a.reshape(-1) for a in arrays])
    per = LANES * row_multiple
    total = -(-flat.shape[0] // per) * per
    return jnp.pad(flat, (0, total - flat.shape[0])).reshape(total // LANES, LANES)


def _unpack(packed, like):
    flat, out, at = packed.reshape(-1), [], 0
    for a in like:
        out.append(flat[at:at + a.size].reshape(a.shape))
        at += a.size
    return out


def kernel(x, mem, norms, mem_norm, ffn1_wg, ffn1_wu, ffn1_wd, ffn2_wg, ffn2_wu, ffn2_wd, x_wq, x_wk, x_wv, x_wo, mix_w_in, mix_w_out, attn_sinks, sgu_ln_g, sgu_ln_b, sgu_w, sgu_b, pool_w, pool_scale, loss_target, m_norms, m_mem_norm, m_ffn1_wg, m_ffn1_wu, m_ffn1_wd, m_ffn2_wg, m_ffn2_wu, m_ffn2_wd, m_x_wq, m_x_wk, m_x_wv, m_x_wo, m_mix_w_in, m_mix_w_out, m_attn_sinks, m_sgu_ln_g, m_sgu_ln_b, m_sgu_w, m_sgu_b, m_pool_w, m_pool_scale, v_norms, v_mem_norm, v_ffn1_wg, v_ffn1_wu, v_ffn1_wd, v_ffn2_wg, v_ffn2_wu, v_ffn2_wd, v_x_wq, v_x_wk, v_x_wv, v_x_wo, v_mix_w_in, v_mix_w_out, v_attn_sinks, v_sgu_ln_g, v_sgu_ln_b, v_sgu_w, v_sgu_b, v_pool_w, v_pool_scale):
    given = dict(locals())
    w = {n: given[n] for n in WEIGHT_ORDER}
    mom = {n: given["m_" + n] for n in WEIGHT_ORDER}
    var = {n: given["v_" + n] for n in WEIGHT_ORDER}
    chip_id = 2 * lax.axis_index("x") + lax.axis_index("y")
    chip = chip_id.astype(jnp.int32).reshape(1)
    core = lax.axis_index("c").astype(jnp.int32).reshape(1)
    n_shard = norms.shape[-1]

    keys = [(name, l) for name in BIG_NAMES for l in range(_as3(w[name]).shape[0])]
    first_keys = [(name, 0) for name in block_weight_names(*BLOCK_ORDER[0])]
    cast = lambda name, l: cast_into_slot(_as3(w[name]), l, chip, f"cast_{name}")
    slot_of = {key: cast(*key) for key in first_keys}
    small_rows = jnp.concatenate([norms.reshape(-1, n_shard), pool_scale, jnp.zeros((15, n_shard), F32)], axis=0)
    small_slot = lax.dynamic_update_slice_in_dim(jnp.zeros((N_CHIPS,) + small_rows.shape, F32), small_rows[None], chip_id, axis=0)
    pending, ready = {}, {}
    pending[0] = gather_start([slot_of[key] for key in first_keys] + [small_slot], chip, "gather_start_0")
    for key in keys:
        if key not in slot_of:
            slot_of[key] = cast(*key)


    def block_slots(k):
        kind, layer = BLOCK_ORDER[k]
        return [slot_of[name, layer if kind != "mix" else 0] for name in block_weight_names(kind, layer)]

    def start(k, after):
        pending[k] = gather_start(block_slots(k), after, f"gather_start_{k}")

    def finish(k, after):
        send_sems, recv_sems, bufs, _ = pending[k]
        return forward_to_sibling(gather_wait(send_sems, recv_sems, bufs, after, f"gather_wait_{k}"), f"gather_forward_{k}")

    def fetch(key, after):
        k = BLOCK_ORDER.index(key)
        bufs = ready.pop(k) if k in ready else finish(k, after)
        token = 0.0
        for ahead in PREFETCH_AT[k]:
            start(ahead, bufs[0])
            token = token + pending[ahead][3][0, 0]
        return tuple(_to_matmul_layout(name, b) for name, b in zip(block_weight_names(*key), bufs)), token

    in_flight, own_of, recv_of = [], {}, {}

    def land(after):
        tag, names, send_sems, recv_sems, parts, lands = in_flight.pop()
        parts, lands = scatter_wait(send_sems, recv_sems, parts, lands, after, f"grads_wait_{tag}")
        return tag, names, parts, lands

    def reduce_landed(tag, names, parts, lands):
        own = [sum_over_chips(t, q, chip, f"sum_{name}") for (name, _), t, q in zip(names, parts, lands)]
        for key, o, r in zip(names, own, swap_with_sibling(own, f"grads_halves_{tag}")):
            own_of[key], recv_of[key] = o, r

    def emit(key, grads_of, after):
        kind, layer = key
        tag = f"{kind}{layer}"
        names = [(name, layer if kind != "mix" else 0) for name in grads_of]
        landed = land(after) if in_flight else None
        full = [_from_matmul_layout(name, g) for name, g in grads_of.items()]
        from_sibling = swap_other_halves(full, f"grads_to_sibling_{tag}")
        chip_sums = [add_own_half(g, p, core, f"chip_sum_{name}") for (name, _), g, p in zip(names, full, from_sibling)]
        send_sems, recv_sems, parts, lands, token = scatter_start(chip_sums, after, f"grads_start_{tag}")
        in_flight.append((tag, names, send_sems, recv_sems, parts, lands))
        if landed:
            reduce_landed(*landed)
        return token[0, 0]

    first = finish(0, slot_of[keys[-1]])
    ready[0], small_all = first[:-1], first[-1]
    n_norm_rows = norms.shape[0] * norms.shape[1]
    norms_all = jnp.concatenate([small_all[j, :n_norm_rows].reshape(norms.shape) for j in range(N_CHIPS)], axis=-1)
    pool_scale_all = jnp.concatenate([small_all[j, n_norm_rows:n_norm_rows + 1] for j in range(N_CHIPS)], axis=-1)
    small = dict(attn_sinks=attn_sinks[0], sgu_ln_g=sgu_ln_g, sgu_ln_b=sgu_ln_b, sgu_w=sgu_w[0], sgu_b=sgu_b[0], pool_scale=pool_scale_all)

    loss_part, dx, g_small, g_norms, g_mem_norm, last_token = device_step(
        x[0], mem[0], loss_target[0], norms_all, mem_norm, small, fetch, emit)
    loss = lax.psum(0.5 * jnp.sum(loss_part) / x.shape[-1], ("x", "y", "c"))
    order = last_token.reshape(1, 1)

    grads, delta, new_m, new_v = {}, {}, {}, {}

    def update(name):
        n_l = _as3(w[name]).shape[0]
        res = adamw_from_halves(_as3(w[name]), [own_of[name, l] for l in range(n_l)], [recv_of[name, l] for l in range(n_l)],
                                _as3(mom[name]), _as3(var[name]), core, order, f"adamw_{name}")
        grads[name], delta[name], new_m[name], new_v[name] = (t.reshape(w[name].shape) for t in res)

    last_names = {name for name, _ in in_flight[0][1]}
    for name in BIG_NAMES:
        if name not in last_names:
            update(name)
    reduce_landed(*land(delta[BIG_NAMES[-1]]))
    for name in BIG_NAMES:
        if name in last_names:
            update(name)

    small_g = [g_norms, g_mem_norm, g_small["attn_sinks"], g_small["sgu_ln_g"], g_small["sgu_ln_b"], g_small["sgu_w"], g_small["sgu_b"],
               g_small["pool_scale"]]
    packed = _pack(small_g, 16)
    summed = sum_slots(gather_all_devices(packed, "small_grads_all"), "small_grads_sum")
    s_norms, s_mem, s_sinks, s_lg, s_lb, s_w, s_b, s_scale = _unpack(summed, small_g)
    grads["norms"] = lax.dynamic_slice_in_dim(s_norms, chip_id * n_shard, n_shard, axis=2)
    grads["pool_scale"] = lax.dynamic_slice_in_dim(s_scale, chip_id * n_shard, n_shard, axis=1)
    grads.update(mem_norm=s_mem, attn_sinks=s_sinks, sgu_ln_g=s_lg, sgu_ln_b=s_lb, sgu_w=s_w, sgu_b=s_b)
    like = [w[n] for n in SMALL_NAMES]
    packs = [_pack([src[n] for n in SMALL_NAMES], 16)[None] for src in (w, grads, mom, var)]
    for dst, t in zip((delta, new_m, new_v), adamw(*packs, "adamw_small")):
        for n, a in zip(SMALL_NAMES, _unpack(t[0], like)):
            dst[n] = a

    outs = [loss, dx[None]]
    for group in (grads, delta, new_m, new_v):
        outs += [group[n] for n in WEIGHT_ORDER]
    return tuple(outs)
```

```python
import functools

import jax
import jax.numpy as jnp
from jax import lax
from jax.experimental import pallas as pl
from jax.experimental.pallas import tpu as pltpu

F32 = jnp.float32
BF16 = jnp.bfloat16
MESH = pl.DeviceIdType.MESH

EPS = 1e-6
ROPE_THETA = 500000.0
ROPE_HALF = 8
HEAD_DIM = 64
N_Q_HEADS = 16
N_KV_HEADS = 2
Q_PER_KV = 8
BLOCK = 128
ATTN_WIDTH = 1024
KV_WIDTH = 128
QK_WIDTH = ATTN_WIDTH + KV_WIDTH
SGU_WIDTH = 1024
SGU_GROUPS = 8
POOL_WINDOWS = (2, 4, 8, 16)
POOL_HALO = 16
X_HEADS = 4
X_HEAD_DIM = 128
N_CHIPS = 4
N_DEV = 8

ADAM_LR = 0.001
ADAM_B1 = 0.9
ADAM_B2 = 0.999
ADAM_EPS = 1e-08
ADAM_WD = 0.01
ADAM_STEP = 10

VMEM_LIMIT_V7X = 52 * 1024 * 1024
MM_VMEM_BUDGET = 44 * 1024 * 1024
LANES = 128
ROW_TILE_BYTES = 6 * 1024 * 1024
MXU_FLOPS_V7X = 1.0e15
HBM_BYTES_PER_S_V7X = 3.0e12
VMEM_STORE_BYTES_PER_S = 4.0e12
MXU_WEIGHT_LOAD_ROWS = 192
GRID_STEP_S = 0.35e-6


def _params(sem):
    return pltpu.CompilerParams(dimension_semantics=sem, vmem_limit_bytes=VMEM_LIMIT_V7X)


def _pick(dim, pref, align):
    cands = [t for t in range(align, dim + 1, align) if dim % t == 0]
    small = [t for t in cands if t <= pref]
    if small and small[-1] * 2 >= min(pref, dim):
        return small[-1]
    return dim


def _rows(n_rows, bytes_per_row):
    want = max(16, min(1024, ROW_TILE_BYTES // max(1, bytes_per_row)))
    cands = [t for t in range(16, n_rows + 1, 16) if n_rows % t == 0 and t <= want]
    return cands[-1] if cands else n_rows


def _divisors(dim, align, most):
    return [t for t in range(align, min(dim, most) + 1, align) if dim % t == 0] or [dim]


def _mm_tiles(M, N, K, J, m_align, k_align, a_bytes, b_bytes, o_bytes, reduce, ta, products=1):
    best = None
    for tm in _divisors(M, m_align, 2048):
        for tn in _divisors(N, LANES, 2048):
            for tk in _divisors(K, k_align, 4096):
                split = K // tk > 1 or reduce
                vmem = 2 * (tm * tk * a_bytes + tk * tn * b_bytes + tm * tn * o_bytes) + tm * tn * 4 * products * (2 if split else 1)
                if ta:
                    vmem += tm * tk * a_bytes
                if vmem > MM_VMEM_BUDGET:
                    continue
                steps = J * (M // tm) * (N // tn) * (K // tk)
                mxu = 2.0 * J * M * N * K / MXU_FLOPS_V7X * (tm + MXU_WEIGHT_LOAD_ROWS) / tm
                acc = J * M * N * (K // tk) * 8 / VMEM_STORE_BYTES_PER_S if split else 0.0
                hbm = J * (M * K * a_bytes * (N // tn) + K * N * b_bytes * (M // tm) + M * N * o_bytes) / HBM_BYTES_PER_S_V7X
                cost = max(mxu + 0.5 * acc, hbm) + steps * GRID_STEP_S
                if best is None or cost < best[0]:
                    best = (cost, tm, tn, tk)
    return best[1:]


def _rsum8(v):
    r, c = v.shape
    return v.reshape(r // 8, 8, c).sum(axis=0)


def _mm(a, b, *, name, ta=False, tb=False, batch="none", groups=0, a_cb=False, b_cb=False, o_cb=False,
        out_dtype=F32, more=(), extras=(), epilogue=None, out_dtypes=None):
    J = groups or (a.shape[0] if a.ndim == 3 else (b.shape[0] if b.ndim == 3 else 1))
    a2, b2 = a.shape[-2:], b.shape[-2:]
    M, K = (a2[1], a2[0]) if ta else a2
    N, Kb = b2 if tb else (b2[1], b2[0])
    if a_cb:
        if ta:
            M //= J
        else:
            K //= J
    if b_cb:
        if tb:
            Kb //= J
        else:
            N //= J
    assert K == Kb, (name, a.shape, b.shape)
    reduce = batch == "reduce"
    out_dtypes = list(out_dtypes or [out_dtype])
    n_terms = 1 + len(more)
    o_bytes = sum(jnp.dtype(d).itemsize for d in out_dtypes) + sum(e.dtype.itemsize for e in extras)
    n_prod = n_terms if epilogue is not None else 1
    tm, tn, tk = _mm_tiles(M, N, K, J, LANES if ta else 16, LANES if (not ta or tb) else 16, a.dtype.itemsize * n_terms,
                           b.dtype.itemsize * n_terms, o_bytes, reduce, ta, n_prod)
    nm, nn, nk = M // tm, N // tn, K // tk
    if reduce:
        grid = (nm, nn, J, nk)
        unpack = lambda m, n, j, k: (j, m, n, k)
        sem = ("parallel", "parallel", "arbitrary", "arbitrary")
    else:
        grid = (J, nm, nn, nk)
        unpack = lambda j, m, n, k: (j, m, n, k)
        sem = ("parallel", "parallel", "parallel", "arbitrary")

    def a_map(*g):
        j, m, n, k = unpack(*g)
        r, c = (k, m) if ta else (m, k)
        if a_cb:
            c = c + j * (nm if ta else nk)
        return (j, r, c) if a.ndim == 3 else (r, c)

    def b_map(*g):
        j, m, n, k = unpack(*g)
        r, c = (n, k) if tb else (k, n)
        if b_cb:
            c = c + j * (nk if tb else nn)
        return (j, r, c) if b.ndim == 3 else (r, c)

    def o_map(*g):
        j, m, n, k = unpack(*g)
        if o_cb:
            return (m, n + j * nn)
        return (j, m, n) if batch == "map" else (m, n)

    a_blk = (tk, tm) if ta else (tm, tk)
    b_blk = (tn, tk) if tb else (tk, tn)
    a_spec = pl.BlockSpec(((None,) + a_blk) if a.ndim == 3 else a_blk, a_map)
    b_spec = pl.BlockSpec(((None,) + b_blk) if b.ndim == 3 else b_blk, b_map)
    if o_cb:
        out_shape, o_blk = (M, N * J), (tm, tn)
    elif batch == "map":
        out_shape, o_blk = (J, M, N), (None, tm, tn)
    else:
        out_shape, o_blk = (M, N), (tm, tn)
    o_spec = pl.BlockSpec(o_blk, o_map)
    dims = (((0 if ta else 1,), (1 if tb else 0,)), ((), ()))
    red_axes = (2, 3) if reduce else (3,)
    split = reduce or nk > 1
    n_ex, n_out = len(extras), len(out_dtypes)

    def body(*refs):
        ab_refs, ex_refs = refs[:2 * n_terms], refs[2 * n_terms:2 * n_terms + n_ex]
        o_refs, acc = refs[2 * n_terms + n_ex:2 * n_terms + n_ex + n_out], refs[2 * n_terms + n_ex + n_out:]
        prods = [lax.dot_general(ab_refs[2 * t][...], ab_refs[2 * t + 1][...], dims, preferred_element_type=F32) for t in range(n_terms)]
        if epilogue is None:
            prods = [functools.reduce(lambda p, q: p + q, prods)]

        def finish(vals):
            outs = epilogue(vals, [e[...] for e in ex_refs]) if epilogue is not None else vals
            for o_ref, val in zip(o_refs, outs):
                o_ref[...] = val.astype(o_ref.dtype)

        if not split:
            finish(prods)
            return
        first = functools.reduce(jnp.logical_and, [pl.program_id(ax) == 0 for ax in red_axes])
        last = functools.reduce(jnp.logical_and, [pl.program_id(ax) == grid[ax] - 1 for ax in red_axes])

        @pl.when(first)
        def _():
            for acc_ref, prod in zip(acc, prods):
                acc_ref[...] = prod

        @pl.when(jnp.logical_not(first))
        def _():
            for acc_ref, prod in zip(acc, prods):
                acc_ref[...] += prod

        @pl.when(last)
        def _():
            finish([acc_ref[...] for acc_ref in acc])

    operands = [a, b] + [t for pair in more for t in pair] + list(extras)
    res = pl.pallas_call(
        body, name=name, grid=grid, in_specs=[a_spec, b_spec] * n_terms + [o_spec] * n_ex, out_specs=[o_spec] * n_out,
        out_shape=[jax.ShapeDtypeStruct(out_shape, d) for d in out_dtypes],
        scratch_shapes=[pltpu.VMEM((tm, tn), F32)] * (n_prod if split else 0), compiler_params=_params(sem),
    )(*operands)
    return res if epilogue is not None else res[0]


def _rowwise(fn, tiled, whole, outs, accs, *, name):
    n_rows = tiled[0].shape[0]
    row_bytes = sum(t.shape[1] * t.dtype.itemsize for t in tiled) + sum(c * jnp.dtype(d).itemsize for c, d in outs)
    tr = _rows(n_rows, row_bytes)
    n_t, n_w, n_o = len(tiled), len(whole), len(outs)

    def body(*refs):
        i = pl.program_id(0)
        t_refs, w_refs = refs[:n_t], refs[n_t:n_t + n_w]
        o_refs, a_refs = refs[n_t + n_w:n_t + n_w + n_o], refs[n_t + n_w + n_o:]
        o_vals, a_vals = fn(i, *[r[...] for r in t_refs], *[r[...] for r in w_refs])
        for r, v in zip(o_refs, o_vals):
            r[...] = v.astype(r.dtype)
        if a_refs:
            @pl.when(i == 0)
            def _():
                for r in a_refs:
                    r[...] = jnp.zeros_like(r)
            for r, v in zip(a_refs, a_vals):
                r[...] += v

    in_specs = [pl.BlockSpec((tr, t.shape[1]), lambda i: (i, 0)) for t in tiled]
    in_specs += [pl.BlockSpec(w.shape, lambda i, nd=w.ndim: (0,) * nd) for w in whole]
    out_specs = [pl.BlockSpec((tr, c), lambda i: (i, 0)) for c, _ in outs]
    out_specs += [pl.BlockSpec(s, lambda i, nd=len(s): (0,) * nd) for s, _ in accs]
    out_shape = [jax.ShapeDtypeStruct((n_rows, c), d) for c, d in outs]
    out_shape += [jax.ShapeDtypeStruct(s, d) for s, d in accs]
    res = pl.pallas_call(
        body, name=name, grid=(n_rows // tr,), in_specs=in_specs, out_specs=out_specs, out_shape=out_shape,
        compiler_params=_params(("arbitrary",) if accs else ("parallel",)),
    )(*tiled, *whole)
    return res


def _rms_stats(x):
    r = lax.rsqrt(jnp.mean(x * x, axis=-1, keepdims=True) + EPS)
    return x * r, r


def rms_fwd(x, g, out_dtype, name):
    def fn(i, x, g):
        xhat, _ = _rms_stats(x)
        return [xhat * g], []
    return _rowwise(fn, [x], [g], [(x.shape[1], out_dtype)], [], name=name)[0]


def postnorm_res(x, y, g, s, name):
    def fn(i, x, y, g):
        yhat, _ = _rms_stats(y)
        return [x + s * (yhat * g)], []
    return _rowwise(fn, [x, y], [g], [(x.shape[1], F32)], [], name=name)[0]


def rms_bwd(xin, g, douts, scale, add, out_dtype, name):
    n_d = len(douts)

    def fn(i, x, *rest):
        ds, rest = rest[:n_d], rest[n_d:]
        ad = rest[0] if add is not None else None
        g = rest[-1]
        xhat, r = _rms_stats(x)
        d = ds[0].astype(F32)
        for e in ds[1:]:
            d = d + e.astype(F32)
        if scale != 1.0:
            d = d * scale
        dg = _rsum8(d * xhat)
        dxhat = d * g
        dx = r * (dxhat - xhat * jnp.mean(dxhat * xhat, axis=-1, keepdims=True))
        if ad is not None:
            dx = dx + ad
        return [dx], [dg]

    tiled = [xin, *douts] + ([add] if add is not None else [])
    dx, dg = _rowwise(fn, tiled, [g], [(xin.shape[1], out_dtype)], [((8, xin.shape[1]), F32)], name=name)
    return dx, dg


def _silu_parts(g):
    sg = 1.0 / (1.0 + jnp.exp(-g))
    return g * sg, sg


def _swiglu_tiles(products, saved):
    u, = products
    return [u, _silu_parts(saved[0].astype(F32))[0] * u]


def _swiglu_bwd_tiles(products, saved):
    da, = products
    g, u = saved[0].astype(F32), saved[1].astype(F32)
    s, sg = _silu_parts(g)
    return [da * u * (sg * (1.0 + g * (1.0 - sg))), da * s]


def scale_cols(y, s, name):
    def fn(i, y, s):
        return [y * s], []
    return _rowwise(fn, [y], [s], [(y.shape[1], F32)], [], name=name)[0]


def scale_cols_bwd(dm, y, s, name):
    def fn(i, dm, y, s):
        return [dm * s], [_rsum8(dm * y)]
    return _rowwise(fn, [dm, y], [s], [(y.shape[1], BF16)], [((8, y.shape[1]), F32)], name=name)


def loss_and_grad(y, target, name):
    n_feat = y.shape[1]

    def fn(i, y, t):
        e = y - t
        return [e * (1.0 / n_feat)], [_rsum8(e * e)]
    dy, part = _rowwise(fn, [y, target], [], [(n_feat, F32)], [((8, n_feat), F32)], name=name)
    return part, dy


def cast_into_slot(w3, layer, chip, name):
    _, n_rows, n_cols = w3.shape
    tr = _rows(n_rows, n_cols * 6)

    def body(chip_ref, w_ref, o_ref):
        o_ref[...] = w_ref[...].astype(BF16)

    grid_spec = pltpu.PrefetchScalarGridSpec(
        num_scalar_prefetch=1, grid=(n_rows // tr,),
        in_specs=[pl.BlockSpec((None, tr, n_cols), lambda i, chip_ref: (layer, i, 0))],
        out_specs=pl.BlockSpec((None, tr, n_cols), lambda i, chip_ref: (chip_ref[0], i, 0)))
    return pl.pallas_call(
        body, name=name, grid_spec=grid_spec, out_shape=jax.ShapeDtypeStruct((N_CHIPS, n_rows, n_cols), BF16),
        compiler_params=_params(("parallel",)),
    )(chip, w3)


def _adam_update(w, g, m, v):
    c1 = 1.0 / (1.0 - ADAM_B1 ** ADAM_STEP)
    c2 = 1.0 / (1.0 - ADAM_B2 ** ADAM_STEP)
    m = ADAM_B1 * m + (1.0 - ADAM_B1) * g
    v = ADAM_B2 * v + (1.0 - ADAM_B2) * (g * g)
    return -ADAM_LR * ((m * c1) / (jnp.sqrt(v * c2) + ADAM_EPS) + ADAM_WD * w), m, v


def adamw(w, g, m, v, name):
    n_l, n_rows, n_cols = w.shape
    tr = _rows(n_rows, n_cols * 4 * 7)

    def body(w_ref, g_ref, m_ref, v_ref, d_ref, mo_ref, vo_ref):
        d_ref[...], mo_ref[...], vo_ref[...] = _adam_update(w_ref[...], g_ref[...], m_ref[...], v_ref[...])

    spec = pl.BlockSpec((None, tr, n_cols), lambda l, i: (l, i, 0))
    shp = jax.ShapeDtypeStruct(w.shape, F32)
    return pl.pallas_call(
        body, name=name, grid=(n_l, n_rows // tr), in_specs=[spec] * 4, out_specs=[spec] * 3, out_shape=[shp] * 3,
        compiler_params=_params(("parallel", "parallel")),
    )(w, g, m, v)


def adamw_from_halves(w, own, recv, m, v, core, order, name):
    n_l, n_rows, n_cols = w.shape
    half = n_rows // 2
    tr = _rows(half, n_cols * 4 * 9)
    per = half // tr

    def body(core_ref, *refs):
        w_ref, m_ref, v_ref, order_ref = refs[:4]
        own_refs, recv_refs = refs[4:4 + n_l], refs[4 + n_l:4 + 2 * n_l]
        g_ref, d_ref, mo_ref, vo_ref = refs[4 + 2 * n_l:]
        l, h = pl.program_id(0), pl.program_id(1)
        mine = h == core_ref[0]
        g = jnp.where(mine, own_refs[0][...], recv_refs[0][...])
        for k in range(1, n_l):
            g = jnp.where(l == k, jnp.where(mine, own_refs[k][...], recv_refs[k][...]), g)
        g = g + order_ref[...]
        g_ref[...] = g
        d_ref[...], mo_ref[...], vo_ref[...] = _adam_update(w_ref[...], g, m_ref[...], v_ref[...])

    full = pl.BlockSpec((None, tr, n_cols), lambda l, h, i, core_ref: (l, h * per + i, 0))

    def piece(layer, is_own):
        def index(l, h, i, core_ref):
            used = (l == layer) & ((h == core_ref[0]) == is_own)
            return (jnp.where(used, i, 0), 0)
        return pl.BlockSpec((tr, n_cols), index)

    grid_spec = pltpu.PrefetchScalarGridSpec(
        num_scalar_prefetch=1, grid=(n_l, 2, per),
        in_specs=[full] * 3 + [pl.BlockSpec((1, 1), lambda l, h, i, core_ref: (0, 0))]
        + [piece(k, True) for k in range(n_l)] + [piece(k, False) for k in range(n_l)],
        out_specs=[full] * 4)
    return pl.pallas_call(
        body, name=name, grid_spec=grid_spec, out_shape=[jax.ShapeDtypeStruct(w.shape, F32)] * 4,
        compiler_params=_params(("parallel", "parallel", "parallel")),
    )(core, w, m, v, order, *own, *recv)


def rope_tables(n_tok):
    inv = ROPE_THETA ** (-jnp.arange(ROPE_HALF, dtype=F32) * 2.0 / (2 * ROPE_HALF))
    ang = jnp.arange(n_tok, dtype=F32)[:, None] * inv[None, :]
    cos, sin = jnp.cos(ang), jnp.sin(ang)
    rest = HEAD_DIM - 2 * ROPE_HALF
    one, zero, z8 = jnp.ones((n_tok, rest), F32), jnp.zeros((n_tok, rest), F32), jnp.zeros((n_tok, ROPE_HALF), F32)
    c = jnp.concatenate([cos, cos, one], axis=1)
    s1 = jnp.concatenate([-sin, z8, zero], axis=1)
    s2 = jnp.concatenate([z8, sin, zero], axis=1)
    two = lambda t: jnp.concatenate([t, t], axis=1)
    return two(c), two(s1), two(s2)


def rope_apply(x, tabs, n_col_blocks, inverse, out_dtype, name):
    n_tok = x.shape[0]
    tr = _rows(n_tok, LANES * 4 * 6)

    def body(x_ref, c_ref, s1_ref, s2_ref, o_ref):
        x = x_ref[...].astype(F32)
        if inverse:
            out = x * c_ref[...] + pltpu.roll(x * s1_ref[...], ROPE_HALF, 1) + pltpu.roll(x * s2_ref[...], LANES - ROPE_HALF, 1)
        else:
            out = x * c_ref[...] + pltpu.roll(x, LANES - ROPE_HALF, 1) * s1_ref[...] + pltpu.roll(x, ROPE_HALF, 1) * s2_ref[...]
        o_ref[...] = out.astype(o_ref.dtype)

    tab_spec = pl.BlockSpec((tr, LANES), lambda i, c: (i, 0))
    blk = pl.BlockSpec((tr, LANES), lambda i, c: (i, c))
    return pl.pallas_call(
        body, name=name, grid=(n_tok // tr, n_col_blocks), in_specs=[blk, tab_spec, tab_spec, tab_spec], out_specs=blk,
        out_shape=jax.ShapeDtypeStruct((n_tok, n_col_blocks * LANES), out_dtype), compiler_params=_params(("parallel", "parallel")),
    )(x, *tabs)


def _swa_probs(q, k, sink, n):
    rows = Q_PER_KV * BLOCK
    s = lax.dot_general(q, k, (((1,), (1,)), ((), ())), preferred_element_type=F32) * (HEAD_DIM ** -0.5)
    qi = lax.broadcasted_iota(jnp.int32, (rows, 2 * BLOCK), 0) & (BLOCK - 1)
    kj = lax.broadcasted_iota(jnp.int32, (rows, 2 * BLOCK), 1)
    rel = qi + BLOCK - kj
    valid = (rel >= 0) & (rel < BLOCK) & ((n > 0) | (kj >= BLOCK))
    s = jnp.where(valid, s, -1e30)
    m = jnp.maximum(jnp.max(s, axis=-1, keepdims=True), sink)
    e = jnp.exp(s - m)
    es = jnp.exp(sink - m)
    inv = 1.0 / (jnp.sum(e, axis=-1, keepdims=True) + es)
    return e * inv, es * inv


def _swa_specs(n_blocks):
    q_spec = pl.BlockSpec((Q_PER_KV, BLOCK, HEAD_DIM), lambda h, n: (h, n, 0))
    prev = pl.BlockSpec((None, BLOCK, HEAD_DIM), lambda h, n: (h, jnp.maximum(n - 1, 0), 0))
    cur = pl.BlockSpec((None, BLOCK, HEAD_DIM), lambda h, n: (h, n, 0))
    sink = pl.BlockSpec((None, Q_PER_KV * BLOCK, 1), lambda h, n: (h, 0, 0))
    return q_spec, prev, cur, sink


def swa_fwd(q, k, v, sink_rows, name):
    n_tok = q.shape[1]
    q_spec, prev, cur, sink = _swa_specs(n_tok // BLOCK)

    def body(q_ref, kp_ref, kc_ref, vp_ref, vc_ref, s_ref, o_ref):
        n = pl.program_id(1)
        qq = q_ref[...].reshape(Q_PER_KV * BLOCK, HEAD_DIM)
        kk = jnp.concatenate([kp_ref[...], kc_ref[...]], axis=0)
        vv = jnp.concatenate([vp_ref[...], vc_ref[...]], axis=0)
        p, _ = _swa_probs(qq, kk, s_ref[...], n)
        o = jnp.dot(p.astype(BF16), vv, preferred_element_type=F32)
        o_ref[...] = o.reshape(Q_PER_KV, BLOCK, HEAD_DIM).astype(o_ref.dtype)

    return pl.pallas_call(
        body, name=name, grid=(N_KV_HEADS, n_tok // BLOCK), in_specs=[q_spec, prev, cur, prev, cur, sink], out_specs=q_spec,
        out_shape=jax.ShapeDtypeStruct(q.shape, BF16), compiler_params=_params(("parallel", "parallel")),
    )(q, k, k, v, v, sink_rows)


def swa_bwd(q, k, v, sink_rows, do, name):
    n_tok = q.shape[1]
    nb = n_tok // BLOCK
    q_spec, prev, cur, sink = _swa_specs(nb)
    rows = Q_PER_KV * BLOCK

    def body(q_ref, kp_ref, kc_ref, vp_ref, vc_ref, s_ref, do_ref, dq_ref, dkp_ref, dkc_ref, dvp_ref, dvc_ref, ds_ref):
        n = pl.program_id(1)
        qq = q_ref[...].reshape(rows, HEAD_DIM)
        dd = do_ref[...].reshape(rows, HEAD_DIM)
        kk = jnp.concatenate([kp_ref[...], kc_ref[...]], axis=0)
        vv = jnp.concatenate([vp_ref[...], vc_ref[...]], axis=0)
        p, ps = _swa_probs(qq, kk, s_ref[...], n)
        dp = lax.dot_general(dd, vv, (((1,), (1,)), ((), ())), preferred_element_type=F32)
        delta = jnp.sum(p * dp, axis=-1, keepdims=True)
        ds = (p * (dp - delta) * (HEAD_DIM ** -0.5)).astype(BF16)
        dq = jnp.dot(ds, kk, preferred_element_type=F32)
        dk = lax.dot_general(ds, qq, (((0,), (0,)), ((), ())), preferred_element_type=F32)
        dv = lax.dot_general(p.astype(BF16), dd, (((0,), (0,)), ((), ())), preferred_element_type=F32)
        dq_ref[...] = dq.reshape(Q_PER_KV, BLOCK, HEAD_DIM).astype(dq_ref.dtype)
        dkp_ref[...] = dk[:BLOCK]
        dkc_ref[...] = dk[BLOCK:]
        dvp_ref[...] = dv[:BLOCK]
        dvc_ref[...] = dv[BLOCK:]
        dsink = jnp.broadcast_to(-ps * delta, (rows, LANES)).reshape(Q_PER_KV, BLOCK, LANES)
        ds_ref[...] = jnp.sum(dsink, axis=1)

    part = pl.BlockSpec((None, None, BLOCK, HEAD_DIM), lambda h, n: (h, n, 0, 0))
    part_shape = jax.ShapeDtypeStruct((N_KV_HEADS, nb, BLOCK, HEAD_DIM), F32)
    return pl.pallas_call(
        body, name=name, grid=(N_KV_HEADS, nb), in_specs=[q_spec, prev, cur, prev, cur, sink, q_spec],
        out_specs=[q_spec, part, part, part, part, pl.BlockSpec((None, None, Q_PER_KV, LANES), lambda h, n: (h, n, 0, 0))],
        out_shape=[jax.ShapeDtypeStruct(q.shape, BF16), part_shape, part_shape, part_shape, part_shape,
                   jax.ShapeDtypeStruct((N_KV_HEADS, nb, Q_PER_KV, LANES), F32)],
        compiler_params=_params(("parallel", "parallel")),
    )(q, k, k, v, v, sink_rows, do)


def _to_heads(t, n_heads):
    return t.reshape(t.shape[0], n_heads, HEAD_DIM).transpose(1, 0, 2)


def _from_heads(t):
    return t.transpose(1, 0, 2).reshape(t.shape[1], -1)


def _fold_kv_grad(prev, cur):
    shifted = jnp.concatenate([prev[:, 1:], jnp.zeros_like(prev[:, :1])], axis=1)
    tot = (cur + shifted).reshape(N_KV_HEADS, -1, HEAD_DIM)
    return _from_heads(tot)


def _x_probs(qh, kh):
    s = lax.dot_general(qh, kh, (((1,), (1,)), ((), ())), preferred_element_type=F32) * (X_HEAD_DIM ** -0.5)
    e = jnp.exp(s - jnp.max(s, axis=-1, keepdims=True))
    return e * (1.0 / jnp.sum(e, axis=-1, keepdims=True))


def xattn_fwd(q, k, v, name):
    n_tok, width = q.shape
    n_mem = k.shape[0]
    tq = _pick(n_tok, 512, 16)

    def body(q_ref, k_ref, v_ref, o_ref):
        for h in range(X_HEADS):
            cols = slice(h * X_HEAD_DIM, (h + 1) * X_HEAD_DIM)
            p = _x_probs(q_ref[:, cols], k_ref[:, cols])
            o_ref[:, cols] = jnp.dot(p.astype(BF16), v_ref[:, cols], preferred_element_type=F32).astype(o_ref.dtype)

    row = pl.BlockSpec((tq, width), lambda i: (i, 0))
    mem = pl.BlockSpec((n_mem, width), lambda i: (0, 0))
    return pl.pallas_call(
        body, name=name, grid=(n_tok // tq,), in_specs=[row, mem, mem], out_specs=row,
        out_shape=jax.ShapeDtypeStruct(q.shape, BF16), compiler_params=_params(("parallel",)),
    )(q, k, v)


def xattn_bwd(q, k, v, do, name):
    n_tok, width = q.shape
    n_mem = k.shape[0]
    tq = _pick(n_tok, 512, 16)

    def body(q_ref, k_ref, v_ref, do_ref, dq_ref, dk_ref, dv_ref):
        @pl.when(pl.program_id(0) == 0)
        def _():
            dk_ref[...] = jnp.zeros_like(dk_ref)
            dv_ref[...] = jnp.zeros_like(dv_ref)

        for h in range(X_HEADS):
            cols = slice(h * X_HEAD_DIM, (h + 1) * X_HEAD_DIM)
            qh, kh, vh, dh = q_ref[:, cols], k_ref[:, cols], v_ref[:, cols], do_ref[:, cols]
            p = _x_probs(qh, kh)
            dp = lax.dot_general(dh, vh, (((1,), (1,)), ((), ())), preferred_element_type=F32)
            delta = jnp.sum(p * dp, axis=-1, keepdims=True)
            ds = (p * (dp - delta) * (X_HEAD_DIM ** -0.5)).astype(BF16)
            dq_ref[:, cols] = jnp.dot(ds, kh, preferred_element_type=F32).astype(dq_ref.dtype)
            dk_ref[:, cols] += lax.dot_general(ds, qh, (((0,), (0,)), ((), ())), preferred_element_type=F32)
            dv_ref[:, cols] += lax.dot_general(p.astype(BF16), dh, (((0,), (0,)), ((), ())), preferred_element_type=F32)

    row = pl.BlockSpec((tq, width), lambda i: (i, 0))
    mem = pl.BlockSpec((n_mem, width), lambda i: (0, 0))
    return pl.pallas_call(
        body, name=name, grid=(n_tok // tq,), in_specs=[row, mem, mem, row], out_specs=[row, mem, mem],
        out_shape=[jax.ShapeDtypeStruct(q.shape, BF16), jax.ShapeDtypeStruct(k.shape, F32), jax.ShapeDtypeStruct(k.shape, F32)],
        compiler_params=_params(("arbitrary",)),
    )(q, k, v, do)


GELU_C = 0.7978845608028654
GELU_A = 0.044715


def _gelu_parts(x):
    x2 = x * x
    t = jnp.tanh(GELU_C * x * (1.0 + GELU_A * x2))
    y = 0.5 * x * (1.0 + t)
    dy = 0.5 * (1.0 + t) + 0.5 * x * (1.0 - t * t) * GELU_C * (1.0 + 3.0 * GELU_A * x2)
    return y, dy


def _sgu_norm(v, ln_g, ln_b):
    mu = jnp.mean(v, axis=-1, keepdims=True)
    vc = v - mu
    r = lax.rsqrt(jnp.mean(vc * vc, axis=-1, keepdims=True) + EPS)
    xhat = vc * r
    return xhat * ln_g + ln_b, xhat, r


def _causal_weights(w_ref):
    i = lax.broadcasted_iota(jnp.int32, (BLOCK, BLOCK), 0)
    j = lax.broadcasted_iota(jnp.int32, (BLOCK, BLOCK), 1)
    return [jnp.where(i >= j, w_ref[g], 0.0).astype(BF16) for g in range(SGU_GROUPS)]


def sgu_fwd(u_pre, v_pre, ln_g, ln_b, w_s, bias_rows, name):
    n_tok = u_pre.shape[0]
    tm = _pick(n_tok, 512, BLOCK)

    def body(u_ref, v_ref, g_ref, b_ref, w_ref, bb_ref, o_ref):
        vn, _, _ = _sgu_norm(_gelu_parts(v_ref[...])[0], g_ref[...], b_ref[...])
        vn = vn.astype(BF16)
        wc = _causal_weights(w_ref)
        for c in range(tm // BLOCK):
            rows = slice(c * BLOCK, (c + 1) * BLOCK)
            for g in range(SGU_GROUPS):
                cols = slice(g * LANES, (g + 1) * LANES)
                mixed = jnp.dot(wc[g], vn[rows, cols], preferred_element_type=F32) + bb_ref[g]
                u = _gelu_parts(u_ref[rows, cols])[0]
                o_ref[rows, cols] = (u * mixed).astype(o_ref.dtype)

    row = pl.BlockSpec((tm, SGU_WIDTH), lambda i: (i, 0))
    vec = pl.BlockSpec((1, SGU_WIDTH), lambda i: (0, 0))
    mat = pl.BlockSpec((SGU_GROUPS, BLOCK, LANES), lambda i: (0, 0, 0))
    return pl.pallas_call(
        body, name=name, grid=(n_tok // tm,), in_specs=[row, row, vec, vec, mat, mat], out_specs=row,
        out_shape=jax.ShapeDtypeStruct((n_tok, SGU_WIDTH), BF16), compiler_params=_params(("parallel",)),
    )(u_pre, v_pre, ln_g, ln_b, w_s, bias_rows)


def sgu_bwd(u_pre, v_pre, ln_g, ln_b, w_s, bias_rows, dgate, name):
    n_tok = u_pre.shape[0]
    tm = _pick(n_tok, 512, BLOCK)

    def body(u_ref, v_ref, g_ref, b_ref, w_ref, bb_ref, dg_ref, du_ref, dv_ref, dw_ref, db_ref, dlg_ref, dlb_ref, dvn_ref):
        @pl.when(pl.program_id(0) == 0)
        def _():
            dw_ref[...] = jnp.zeros_like(dw_ref)
            db_ref[...] = jnp.zeros_like(db_ref)
            dlg_ref[...] = jnp.zeros_like(dlg_ref)
            dlb_ref[...] = jnp.zeros_like(dlb_ref)

        gv, dgv = _gelu_parts(v_ref[...])
        vn, xhat, r = _sgu_norm(gv, g_ref[...], b_ref[...])
        vn = vn.astype(BF16)
        wc = _causal_weights(w_ref)
        for c in range(tm // BLOCK):
            rows = slice(c * BLOCK, (c + 1) * BLOCK)
            for g in range(SGU_GROUPS):
                cols = slice(g * LANES, (g + 1) * LANES)
                vt = vn[rows, cols]
                mixed = jnp.dot(wc[g], vt, preferred_element_type=F32) + bb_ref[g]
                u, du_dpre = _gelu_parts(u_ref[rows, cols])
                dgate_t = dg_ref[rows, cols].astype(F32)
                du_ref[rows, cols] = (dgate_t * mixed * du_dpre).astype(du_ref.dtype)
                dmix = dgate_t * u
                dmix_b = dmix.astype(BF16)
                db_ref[g] += dmix
                dw_ref[g] += lax.dot_general(dmix_b, vt, (((1,), (1,)), ((), ())), preferred_element_type=F32)
                dvn_ref[rows, cols] = lax.dot_general(wc[g], dmix_b, (((0,), (0,)), ((), ())), preferred_element_type=F32)
        dvn = dvn_ref[...]
        dlg_ref[...] += _rsum8(dvn * xhat)
        dlb_ref[...] += _rsum8(dvn)
        dxhat = dvn * g_ref[...]
        dgv_in = r * (dxhat - jnp.mean(dxhat, axis=-1, keepdims=True) - xhat * jnp.mean(dxhat * xhat, axis=-1, keepdims=True))
        dv_ref[...] = (dgv_in * dgv).astype(dv_ref.dtype)

    row = pl.BlockSpec((tm, SGU_WIDTH), lambda i: (i, 0))
    vec = pl.BlockSpec((1, SGU_WIDTH), lambda i: (0, 0))
    mat = pl.BlockSpec((SGU_GROUPS, BLOCK, LANES), lambda i: (0, 0, 0))
    part = pl.BlockSpec((8, SGU_WIDTH), lambda i: (0, 0))
    mat_shape = jax.ShapeDtypeStruct((SGU_GROUPS, BLOCK, LANES), F32)
    part_shape = jax.ShapeDtypeStruct((8, SGU_WIDTH), F32)
    act_shape = jax.ShapeDtypeStruct((n_tok, SGU_WIDTH), BF16)
    return pl.pallas_call(
        body, name=name, grid=(n_tok // tm,), in_specs=[row, row, vec, vec, mat, mat, row],
        out_specs=[row, row, mat, mat, part, part], out_shape=[act_shape, act_shape, mat_shape, mat_shape, part_shape, part_shape],
        scratch_shapes=[pltpu.VMEM((tm, SGU_WIDTH), F32)], compiler_params=_params(("arbitrary",)),
    )(u_pre, v_pre, ln_g, ln_b, w_s, bias_rows, dgate)


def _pool_tile(n_tok):
    return _pick(n_tok, 256, POOL_HALO)


def pool_fwd(h, name):
    n_tok, width = h.shape
    gw = width // len(POOL_WINDOWS)
    tm = _pool_tile(n_tok)
    per = tm // POOL_HALO

    def body(cur_ref, halo_ref, o_ref, buf_ref):
        i = pl.program_id(0)
        buf_ref[0:POOL_HALO, :] = jnp.where(i > 0, halo_ref[...], 0.0)
        buf_ref[POOL_HALO:, :] = cur_ref[...]
        tok = i * tm + lax.broadcasted_iota(jnp.int32, (tm, 1), 0)
        for g, w in enumerate(POOL_WINDOWS):
            cols = slice(g * gw, (g + 1) * gw)
            acc = buf_ref[POOL_HALO:, cols]
            for j in range(1, w):
                acc = acc + buf_ref[POOL_HALO - j:POOL_HALO - j + tm, cols]
            cnt = jnp.minimum(tok + 1, w).astype(F32)
            o_ref[:, cols] = (acc / cnt - cur_ref[:, cols]).astype(o_ref.dtype)

    return pl.pallas_call(
        body, name=name, grid=(n_tok // tm,),
        in_specs=[pl.BlockSpec((tm, width), lambda i: (i, 0)),
                  pl.BlockSpec((POOL_HALO, width), lambda i: (jnp.maximum(i * per - 1, 0), 0))],
        out_specs=pl.BlockSpec((tm, width), lambda i: (i, 0)), out_shape=jax.ShapeDtypeStruct(h.shape, BF16),
        scratch_shapes=[pltpu.VMEM((tm + POOL_HALO, width), F32)], compiler_params=_params(("parallel",)),
    )(h, h)


def pool_bwd(dp, name):
    n_tok, width = dp.shape
    gw = width // len(POOL_WINDOWS)
    tm = _pool_tile(n_tok)
    per = tm // POOL_HALO
    n_steps = n_tok // tm

    def body(cur_ref, halo_ref, o_ref, buf_ref):
        i = pl.program_id(0)
        tok = i * tm + lax.broadcasted_iota(jnp.int32, (tm, 1), 0)
        for g, w in enumerate(POOL_WINDOWS):
            cols = slice(g * gw, (g + 1) * gw)
            cnt = jnp.minimum(tok + 1, w).astype(F32)
            buf_ref[0:tm, cols] = cur_ref[:, cols] / cnt
            buf_ref[tm:, cols] = jnp.where(i < n_steps - 1, halo_ref[:, cols] / float(w), 0.0)
        for g, w in enumerate(POOL_WINDOWS):
            cols = slice(g * gw, (g + 1) * gw)
            acc = buf_ref[0:tm, cols]
            for j in range(1, w):
                acc = acc + buf_ref[j:j + tm, cols]
            o_ref[:, cols] = acc - cur_ref[:, cols]

    return pl.pallas_call(
        body, name=name, grid=(n_steps,),
        in_specs=[pl.BlockSpec((tm, width), lambda i: (i, 0)),
                  pl.BlockSpec((POOL_HALO, width), lambda i: (jnp.minimum((i + 1) * per, n_tok // POOL_HALO - 1), 0))],
        out_specs=pl.BlockSpec((tm, width), lambda i: (i, 0)), out_shape=jax.ShapeDtypeStruct(dp.shape, F32),
        scratch_shapes=[pltpu.VMEM((tm + POOL_HALO, width), F32)], compiler_params=_params(("parallel",)),
    )(dp, dp)


def _ffn_fwd(x, ga, gb, wg, wu, wd, tag):
    h = rms_fwd(x, ga, BF16, f"{tag}_prenorm")
    G = _mm(h, wg, batch="map", out_dtype=BF16, name=f"{tag}_gate")
    wu = wu(G) if callable(wu) else wu
    U, A = _mm(h, wu, batch="map", extras=[G], epilogue=_swiglu_tiles, out_dtypes=[BF16] * 2, name=f"{tag}_up")
    wd = wd(A) if callable(wd) else wd
    y = _mm(A, wd, batch="reduce", name=f"{tag}_down")
    return postnorm_res(x, y, gb, 0.5, f"{tag}_postnorm"), (x, h, G, U, A, y, wg, wu, wd)


def _ffn_bwd(res, ga, gb, dx2, tag):
    x, h, G, U, A, y, wg, wu, wd = res
    dy, dgb = rms_bwd(y, gb, [dx2], 0.5, None, BF16, f"{tag}_postnorm_bwd")
    dG, dU = _mm(dy, wd, tb=True, batch="map", extras=[G, U], epilogue=_swiglu_bwd_tiles, out_dtypes=[BF16] * 2, name=f"{tag}_down_dx")
    dwd = _mm(A, dy, ta=True, batch="map", name=f"{tag}_down_dw", out_dtype=BF16)
    dwg = _mm(h, dG, ta=True, batch="map", name=f"{tag}_gate_dw", out_dtype=BF16)
    dwu = _mm(h, dU, ta=True, batch="map", name=f"{tag}_up_dw", out_dtype=BF16)
    dh = _mm(dG, wg, tb=True, batch="reduce", more=[(dU, wu)], name=f"{tag}_gate_up_dx")
    dx, dga = rms_bwd(x, ga, [dh], 1.0, dx2, F32, f"{tag}_prenorm_bwd")
    return dx, dga, dgb, dwg, dwu, dwd


def _sink_rows(sinks):
    return jnp.repeat(sinks.reshape(N_KV_HEADS, Q_PER_KV), BLOCK, axis=1)[..., None]


def _attn_sgu_fwd(x, g_pre, g_post, w_in, w_out, sinks, ln_g, ln_b, sgu_w, bias_rows, tabs, tag):
    h = rms_fwd(x, g_pre, BF16, f"{tag}_prenorm")
    z = _mm(h, w_in, name=f"{tag}_in")
    qk = rope_apply(z, tabs, QK_WIDTH // LANES, False, BF16, f"{tag}_rope")
    q = _to_heads(qk[:, :ATTN_WIDTH], N_Q_HEADS)
    k = _to_heads(qk[:, ATTN_WIDTH:], N_KV_HEADS)
    v = _to_heads(z[:, QK_WIDTH:QK_WIDTH + KV_WIDTH].astype(BF16), N_KV_HEADS)
    o = swa_fwd(q, k, v, _sink_rows(sinks), f"{tag}_swa")
    u_pre = z[:, QK_WIDTH + KV_WIDTH:QK_WIDTH + KV_WIDTH + SGU_WIDTH]
    v_pre = z[:, QK_WIDTH + KV_WIDTH + SGU_WIDTH:]
    gate = sgu_fwd(u_pre, v_pre, ln_g, ln_b, sgu_w, bias_rows, f"{tag}_sgu")
    cat = jnp.concatenate([_from_heads(o), gate], axis=1)
    m = _mm(cat, w_out, name=f"{tag}_out")
    return postnorm_res(x, m, g_post, 1.0, f"{tag}_postnorm"), (x, h, q, k, v, u_pre, v_pre, cat, m)


def _attn_sgu_bwd(res, g_pre, g_post, w_in, w_out, sinks, ln_g, ln_b, sgu_w, bias_rows, tabs, dx2, tag):
    x, h, q, k, v, u_pre, v_pre, cat, m = res
    dm, dg_post = rms_bwd(m, g_post, [dx2], 1.0, None, BF16, f"{tag}_postnorm_bwd")
    dcat = _mm(dm, w_out, tb=True, out_dtype=BF16, name=f"{tag}_out_dx")
    dw_out = _mm(cat, dm, ta=True, name=f"{tag}_out_dw", out_dtype=BF16)
    do = _to_heads(dcat[:, :ATTN_WIDTH], N_Q_HEADS)
    dq, dkp, dkc, dvp, dvc, dsink = swa_bwd(q, k, v, _sink_rows(sinks), do, f"{tag}_swa_bwd")
    d_sinks = jnp.sum(dsink[..., 0], axis=1).reshape(1, N_Q_HEADS)
    dqk_rot = jnp.concatenate([_from_heads(dq).astype(F32), _fold_kv_grad(dkp, dkc)], axis=1)
    dqk = rope_apply(dqk_rot, tabs, QK_WIDTH // LANES, True, BF16, f"{tag}_rope_bwd")
    dv = _fold_kv_grad(dvp, dvc).astype(BF16)
    du_pre, dv_pre, dw_s, dbias, dlg, dlb = sgu_bwd(u_pre, v_pre, ln_g, ln_b, sgu_w, bias_rows, dcat[:, ATTN_WIDTH:], f"{tag}_sgu_bwd")
    dz = jnp.concatenate([dqk, dv, du_pre, dv_pre], axis=1)
    dw_in = _mm(h, dz, ta=True, name=f"{tag}_in_dw", out_dtype=BF16)
    dh = _mm(dz, w_in, tb=True, name=f"{tag}_in_dx")
    dx, dg_pre = rms_bwd(x, g_pre, [dh], 1.0, dx2, F32, f"{tag}_prenorm_bwd")
    causal = jnp.tril(jnp.ones((BLOCK, BLOCK), F32))
    small = dict(attn_sinks=d_sinks, sgu_ln_g=jnp.sum(dlg, axis=0, keepdims=True), sgu_ln_b=jnp.sum(dlb, axis=0, keepdims=True),
                 sgu_w=(dw_s * causal[None])[None], sgu_b=jnp.sum(dbias, axis=-1)[None])
    return dx, dg_pre, dg_post, dw_in, dw_out, small


def _pool_mix_fwd(x, g_pre, g_post, pool_w, pool_scale, tag):
    hf = rms_fwd(x, g_pre, F32, f"{tag}_prenorm")
    pooled = pool_fwd(hf, f"{tag}_pool")
    n_g = len(POOL_WINDOWS)
    ypre = _mm(pooled, pool_w, batch="map", groups=n_g, a_cb=True, o_cb=True, name=f"{tag}_proj")
    m = scale_cols(ypre, pool_scale, f"{tag}_scale")
    return postnorm_res(x, m, g_post, 1.0, f"{tag}_postnorm"), (x, pooled, ypre, m)


def _pool_mix_bwd(res, g_pre, g_post, pool_w, pool_scale, dx2, tag):
    x, pooled, ypre, m = res
    n_g = len(POOL_WINDOWS)
    dm, dg_post = rms_bwd(m, g_post, [dx2], 1.0, None, F32, f"{tag}_postnorm_bwd")
    dypre, dscale = scale_cols_bwd(dm, ypre, pool_scale, f"{tag}_scale_bwd")
    dpooled = _mm(dypre, pool_w, tb=True, batch="map", groups=n_g, a_cb=True, o_cb=True, name=f"{tag}_proj_dx")
    dpw = _mm(pooled, dypre, ta=True, batch="map", groups=n_g, a_cb=True, b_cb=True, name=f"{tag}_proj_dw", out_dtype=BF16)
    dhf = pool_bwd(dpooled, f"{tag}_pool_bwd")
    dx, dg_pre = rms_bwd(x, g_pre, [dhf], 1.0, dx2, F32, f"{tag}_prenorm_bwd")
    return dx, dg_pre, dg_post, dpw, jnp.sum(dscale, axis=0, keepdims=True)


def _xattn_fwd(x, mem, g_pre, g_post, g_mem, wq, wk, wv, wo, tag):
    h = rms_fwd(x, g_pre, BF16, f"{tag}_prenorm")
    mem_n = rms_fwd(mem, g_mem, BF16, f"{tag}_memnorm")
    q = _mm(h, wq, out_dtype=BF16, name=f"{tag}_q")
    k = _mm(mem_n, wk, out_dtype=BF16, name=f"{tag}_k")
    v = _mm(mem_n, wv, out_dtype=BF16, name=f"{tag}_v")
    o = xattn_fwd(q, k, v, f"{tag}_core")
    r = _mm(o, wo, name=f"{tag}_o")
    return postnorm_res(x, r, g_post, 1.0, f"{tag}_postnorm"), (x, h, mem_n, q, k, v, o, r)


def _xattn_bwd(res, mem, g_pre, g_post, g_mem, wq, wk, wv, wo, dx2, tag):
    x, h, mem_n, q, k, v, o, r = res
    dr, dg_post = rms_bwd(r, g_post, [dx2], 1.0, None, BF16, f"{tag}_postnorm_bwd")
    do = _mm(dr, wo, tb=True, out_dtype=BF16, name=f"{tag}_o_dx")
    dwo = _mm(o, dr, ta=True, name=f"{tag}_o_dw", out_dtype=BF16)
    dq, dk, dv = xattn_bwd(q, k, v, do, f"{tag}_core_bwd")
    dk, dv = dk.astype(BF16), dv.astype(BF16)
    dwq = _mm(h, dq, ta=True, name=f"{tag}_q_dw", out_dtype=BF16)
    dwk = _mm(mem_n, dk, ta=True, name=f"{tag}_k_dw", out_dtype=BF16)
    dwv = _mm(mem_n, dv, ta=True, name=f"{tag}_v_dw", out_dtype=BF16)
    dh = _mm(dq, wq, tb=True, name=f"{tag}_q_dx")
    dmem1 = _mm(dk, wk, tb=True, name=f"{tag}_k_dx")
    dmem2 = _mm(dv, wv, tb=True, name=f"{tag}_v_dx")
    _, dg_mem = rms_bwd(mem, g_mem, [dmem1, dmem2], 1.0, None, BF16, f"{tag}_memnorm_bwd")
    dx, dg_pre = rms_bwd(x, g_pre, [dh], 1.0, dx2, F32, f"{tag}_prenorm_bwd")
    return dx, dg_pre, dg_post, dg_mem, dwq, dwk, dwv, dwo


def _rowsum8(part):
    return jnp.sum(part, axis=0, keepdims=True)


def device_step(x, mem, target, norms, mem_norm, small, fetch, emit):
    n_tok = x.shape[0]
    tabs = rope_tables(n_tok)
    bias_rows = jnp.broadcast_to(small["sgu_b"][:, :, None], (SGU_GROUPS, BLOCK, LANES))
    gn = lambda l, i: norms[l, i][None, :]
    gm = lambda l: mem_norm[l][None, :]
    mix0 = (small["attn_sinks"], small["sgu_ln_g"], small["sgu_ln_b"], small["sgu_w"], bias_rows, tabs)

    wts, saved = {}, []
    for l in range(2):
        wts["ffn1", l], tok = fetch(("ffn1", l), x)
        x, r1 = _ffn_fwd(x, gn(l, 0) + tok, gn(l, 1), *wts["ffn1", l], f"l{l}_ffn1")
        wts["mix", l], tok = fetch(("mix", l), x)
        if l == 0:
            x, r2 = _attn_sgu_fwd(x, gn(l, 2) + tok, gn(l, 3), *wts["mix", l], *mix0, f"l{l}_mix")
        else:
            x, r2 = _pool_mix_fwd(x, gn(l, 2) + tok, gn(l, 3), *wts["mix", l], small["pool_scale"], f"l{l}_mix")
        wts["xattn", l], tok = fetch(("xattn", l), x)
        x, r3 = _xattn_fwd(x, mem, gn(l, 4) + tok, gn(l, 5), gm(l), *wts["xattn", l], f"l{l}_xattn")
        wts["ffn2", l], tok = fetch(("ffn2", l), x)
        x, r4 = _ffn_fwd(x, gn(l, 6) + tok, gn(l, 7), *wts["ffn2", l], f"l{l}_ffn2")
        saved.append((r1, r2, r3, r4))

    loss_part, dx = loss_and_grad(x, target, "loss")

    g_norm_rows = [[None] * 8, [None] * 8]
    g_mem_rows = [None, None]
    g_small = {}
    tok = 0.0
    for l in (1, 0):
        r1, r2, r3, r4 = saved[l]
        dx, g_norm_rows[l][6], g_norm_rows[l][7], dwg, dwu, dwd = _ffn_bwd(r4, gn(l, 6), gn(l, 7) + tok, dx, f"l{l}_ffn2")
        tok = emit(("ffn2", l), dict(ffn2_wg=dwg, ffn2_wu=dwu, ffn2_wd=dwd), dx)
        dx, g_norm_rows[l][4], g_norm_rows[l][5], g_mem_rows[l], dwq, dwk, dwv, dwo = _xattn_bwd(
            r3, mem, gn(l, 4), gn(l, 5) + tok, gm(l), *wts["xattn", l], dx, f"l{l}_xattn")
        tok = emit(("xattn", l), dict(x_wq=dwq, x_wk=dwk, x_wv=dwv, x_wo=dwo), dx)
        if l == 0:
            dx, g_norm_rows[l][2], g_norm_rows[l][3], dw_in, dw_out, sm = _attn_sgu_bwd(
                r2, gn(l, 2), gn(l, 3) + tok, *wts["mix", l], *mix0, dx, f"l{l}_mix")
            tok = emit(("mix", l), dict(mix_w_in=dw_in, mix_w_out=dw_out), dx)
            g_small.update(sm)
        else:
            dx, g_norm_rows[l][2], g_norm_rows[l][3], dpw, dscale = _pool_mix_bwd(
                r2, gn(l, 2), gn(l, 3) + tok, *wts["mix", l], small["pool_scale"], dx, f"l{l}_mix")
            tok = emit(("mix", l), dict(pool_w=dpw), dx)
            g_small["pool_scale"] = dscale
        dx, g_norm_rows[l][0], g_norm_rows[l][1], dwg, dwu, dwd = _ffn_bwd(r1, gn(l, 0), gn(l, 1) + tok, dx, f"l{l}_ffn1")
        tok = emit(("ffn1", l), dict(ffn1_wg=dwg, ffn1_wu=dwu, ffn1_wd=dwd), dx)
    g_norms = jnp.stack([jnp.concatenate([_rowsum8(p) for p in g_norm_rows[l]], axis=0) for l in range(2)])
    g_mem_norm = jnp.concatenate([_rowsum8(p) for p in g_mem_rows], axis=0)
    return loss_part, dx, g_small, g_norms, g_mem_norm, tok


ANY = pl.BlockSpec(memory_space=pl.ANY)


def _place():
    x, y, c = lax.axis_index("x"), lax.axis_index("y"), lax.axis_index("c")
    other_chips = [(1 - x, y), (x, 1 - y), (1 - x, 1 - y)]
    return x, y, c, other_chips


def _half_rows(core, n_rows):
    half = n_rows // 2
    return pl.ds(pl.multiple_of(core * half, 16), half)


def _remote(src, dst, send_sem, recv_sem, device):
    return pltpu.make_async_remote_copy(src_ref=src, dst_ref=dst, send_sem=send_sem, recv_sem=recv_sem,
                                        device_id=device, device_id_type=MESH)


HBM =pl.BlockSpec(memory_space=pltpu.HBM)
SEM = pl.BlockSpec(memory_space=pltpu.SEMAPHORE)
DATAFLOW = pltpu.SideEffectType.DATAFLOW_SIDE_EFFECTING


def _chip_copies(bufs, send_sems, recv_sems):
    x, y, c, chips = _place()
    me = 2 * x + y
    sends, arrivals = [], []
    for i, buf in enumerate(bufs):
        rows = _half_rows(c, buf.shape[1])
        for r, (px, py) in enumerate(chips):
            mine, theirs = buf.at[me, rows], buf.at[2 * px + py, rows]
            sends.append(_remote(mine, mine, send_sems.at[3 * i + r], recv_sems.at[3 * i + r], (px, py, c)))
            arrivals.append(_remote(theirs, theirs, send_sems.at[3 * i + r], recv_sems.at[3 * i + r], (px, py, c)))
    return sends, arrivals


def gather_start(slots, after, name):
    n = len(slots)

    def body(*refs):
        send_sems, recv_sems = refs[n + 1], refs[n + 2]
        bufs, token = refs[n + 3:2 * n + 3], refs[2 * n + 3]
        sends, _ = _chip_copies(bufs, send_sems, recv_sems)
        for cp in sends:
            cp.start()
        token[...] = jnp.zeros_like(token)

    res = pl.pallas_call(
        body, name=name, in_specs=[HBM] * n + [ANY],
        out_specs=[SEM, SEM] + [HBM] * n + [pl.BlockSpec(memory_space=pltpu.VMEM)],
        out_shape=[pltpu.SemaphoreType.DMA((3 * n,)), pltpu.SemaphoreType.DMA((3 * n,))]
        + [pltpu.HBM(s.shape, s.dtype) for s in slots] + [jax.ShapeDtypeStruct((8, LANES), F32)],
        input_output_aliases={i: 2 + i for i in range(n)}, compiler_params=pltpu.CompilerParams(has_side_effects=DATAFLOW),
    )(*[pltpu.with_memory_space_constraint(s, pltpu.HBM) for s in slots], after)
    return res[0], res[1], list(res[2:2 + n]), res[2 + n]


def gather_wait(send_sems, recv_sems, bufs, after, name):
    n = len(bufs)

    def body(*refs):
        send_ref, recv_ref = refs[n], refs[n + 1]
        _, arrivals = _chip_copies(refs[:n], send_ref, recv_ref)
        for cp in arrivals:
            cp.wait_send()
            cp.wait_recv()

    return pl.pallas_call(
        body, name=name, in_specs=[HBM] * n + [SEM, SEM, ANY], out_specs=[HBM] * n,
        out_shape=[pltpu.HBM(b.shape, b.dtype) for b in bufs], input_output_aliases={i: i for i in range(n)},
        compiler_params=pltpu.CompilerParams(has_side_effects=DATAFLOW),
    )(*bufs, send_sems, recv_sems, after)


def forward_to_sibling(bufs, name):
    n = len(bufs)

    def body(*refs):
        out = refs[n:2 * n]
        send_sems, recv_sems = refs[2 * n:]
        x, y, c, chips = _place()
        copies = []
        for i in range(n):
            for r, (px, py) in enumerate(chips):
                mine = out[i].at[2 * px + py, _half_rows(c, out[i].shape[1])]
                copies.append(_remote(mine, mine, send_sems.at[3 * i + r], recv_sems.at[3 * i + r], (x, y, 1 - c)))
                copies[-1].start()
        for i in range(n):
            for r, (px, py) in enumerate(chips):
                theirs = out[i].at[2 * px + py, _half_rows(1 - c, out[i].shape[1])]
                _remote(theirs, theirs, send_sems.at[3 * i + r], recv_sems.at[3 * i + r], (x, y, 1 - c)).wait_recv()
        for cp in copies:
            cp.wait_send()

    return pl.pallas_call(
        body, name=name, in_specs=[ANY] * n, out_specs=[ANY] * n, input_output_aliases={i: i for i in range(n)},
        out_shape=[jax.ShapeDtypeStruct(b.shape, b.dtype) for b in bufs],
        scratch_shapes=[pltpu.SemaphoreType.DMA((3 * n,)), pltpu.SemaphoreType.DMA((3 * n,))],
    )(*bufs)


def swap_other_halves(grads, name):
    n = len(grads)

    def body(*refs):
        src, out = refs[:n], refs[n:2 * n]
        send_sems, recv_sems = refs[2 * n:]
        x, y, c, _ = _place()
        copies = [_remote(src[i].at[:, _half_rows(1 - c, src[i].shape[1])], out[i], send_sems.at[i], recv_sems.at[i], (x, y, 1 - c))
                  for i in range(n)]
        for cp in copies:
            cp.start()
        for cp in copies:
            cp.wait()

    return pl.pallas_call(
        body, name=name, in_specs=[ANY] * n, out_specs=[ANY] * n,
        out_shape=[jax.ShapeDtypeStruct((g.shape[0], g.shape[1] // 2, g.shape[2]), g.dtype) for g in grads],
        scratch_shapes=[pltpu.SemaphoreType.DMA((n,)), pltpu.SemaphoreType.DMA((n,))],
    )(*grads)


def add_own_half(g, p, core, name):
    n_j, n_rows, n_cols = g.shape
    half = n_rows // 2
    tr = _rows(half, n_cols * 10)

    def body(c_ref, g_ref, p_ref, o_ref):
        o_ref[...] = (g_ref[...] + p_ref[...]).astype(o_ref.dtype)

    grid_spec = pltpu.PrefetchScalarGridSpec(
        num_scalar_prefetch=1, grid=(n_j, half // tr),
        in_specs=[pl.BlockSpec((None, None, tr, n_cols), lambda j, i, c_ref: (j, c_ref[0], i, 0)),
                  pl.BlockSpec((None, tr, n_cols), lambda j, i, c_ref: (j, i, 0))],
        out_specs=pl.BlockSpec((None, tr, n_cols), lambda j, i, c_ref: (j, i, 0)))
    return pl.pallas_call(
        body, name=name, grid_spec=grid_spec, out_shape=jax.ShapeDtypeStruct((n_j, half, n_cols), BF16),
        compiler_params=_params(("parallel", "parallel")),
    )(core, g.reshape(n_j, 2, half, n_cols), p)


def _scatter_copies(parts, lands, send_sems, recv_sems):
    x, y, c, chips = _place()
    me = 2 * x + y
    sends, arrivals = [], []
    for i in range(len(parts)):
        for r, (px, py) in enumerate(chips):
            theirs = lands[i].at[2 * px + py]
            sends.append(_remote(parts[i].at[2 * px + py], lands[i].at[me], send_sems.at[3 * i + r], recv_sems.at[3 * i + r], (px, py, c)))
            arrivals.append(_remote(theirs, theirs, send_sems.at[3 * i + r], recv_sems.at[3 * i + r], (px, py, c)))
    return sends, arrivals


def scatter_start(parts, after, name):
    n = len(parts)

    def body(*refs):
        send_sems, recv_sems = refs[2 * n + 1], refs[2 * n + 2]
        src, lands, token = refs[2 * n + 3:3 * n + 3], refs[3 * n + 3:4 * n + 3], refs[4 * n + 3]
        sends, _ = _scatter_copies(src, lands, send_sems, recv_sems)
        for cp in sends:
            cp.start()
        token[...] = jnp.zeros_like(token)

    fresh = [pltpu.with_memory_space_constraint(lax.empty(p.shape, p.dtype), pltpu.HBM) for p in parts]
    res = pl.pallas_call(
        body, name=name, in_specs=[HBM] * (2 * n) + [ANY],
        out_specs=[SEM, SEM] + [HBM] * (2 * n) + [pl.BlockSpec(memory_space=pltpu.VMEM)],
        out_shape=[pltpu.SemaphoreType.DMA((3 * n,)), pltpu.SemaphoreType.DMA((3 * n,))]
        + [pltpu.HBM(p.shape, p.dtype) for p in parts] * 2 + [jax.ShapeDtypeStruct((8, LANES), F32)],
        input_output_aliases={i: 2 + i for i in range(2 * n)}, compiler_params=pltpu.CompilerParams(has_side_effects=DATAFLOW),
    )(*[pltpu.with_memory_space_constraint(p, pltpu.HBM) for p in parts], *fresh, after)
    return res[0], res[1], list(res[2:2 + n]), list(res[2 + n:2 + 2 * n]), res[2 + 2 * n]


def scatter_wait(send_sems, recv_sems, parts, lands, after, name):
    n = len(parts)

    def body(*refs):
        _, arrivals = _scatter_copies(refs[:n], refs[n:2 * n], refs[2 * n], refs[2 * n + 1])
        for cp in arrivals:
            cp.wait_send()
            cp.wait_recv()

    res = pl.pallas_call(
        body, name=name, in_specs=[HBM] * (2 * n) + [SEM, SEM, ANY], out_specs=[HBM] * (2 * n),
        out_shape=[pltpu.HBM(p.shape, p.dtype) for p in parts] * 2, input_output_aliases={i: i for i in range(2 * n)},
        compiler_params=pltpu.CompilerParams(has_side_effects=DATAFLOW),
    )(*parts, *lands, send_sems, recv_sems, after)
    return list(res[:n]), list(res[n:])


def sum_over_chips(own, got, chip, name):
    n_s, n_rows, n_cols = got.shape
    tr = _rows(n_rows, n_cols * (got.dtype.itemsize * (n_s + 1) + 4))

    def body(chip_ref, own_ref, *refs):
        got_refs, o_ref = refs[:n_s], refs[n_s]
        me = chip_ref[0]
        acc = jnp.where(me == 0, own_ref[...], got_refs[0][...]).astype(F32)
        for k in range(1, n_s):
            acc = acc + jnp.where(me == k, own_ref[...], got_refs[k][...]).astype(F32)
        o_ref[...] = acc

    def slot(k):
        return pl.BlockSpec((None, tr, n_cols), lambda i, chip_ref: (jnp.where(chip_ref[0] == k, (k + 1) % n_s, k), i, 0))

    grid_spec = pltpu.PrefetchScalarGridSpec(
        num_scalar_prefetch=1, grid=(n_rows // tr,),
        in_specs=[pl.BlockSpec((None, tr, n_cols), lambda i, chip_ref: (chip_ref[0], i, 0))] + [slot(k) for k in range(n_s)],
        out_specs=pl.BlockSpec((tr, n_cols), lambda i, chip_ref: (i, 0)))
    return pl.pallas_call(
        body, name=name, grid_spec=grid_spec, out_shape=jax.ShapeDtypeStruct((n_rows, n_cols), F32),
        compiler_params=_params(("parallel",)),
    )(chip, own, *([got] * n_s))


def swap_with_sibling(arrays, name):
    n = len(arrays)

    def body(*refs):
        src, out = refs[:n], refs[n:2 * n]
        send_sems, recv_sems = refs[2 * n:]
        x, y, c, _ = _place()
        copies = [_remote(src[i], out[i], send_sems.at[i], recv_sems.at[i], (x, y, 1 - c)) for i in range(n)]
        for cp in copies:
            cp.start()
        for cp in copies:
            cp.wait()

    return pl.pallas_call(
        body, name=name, in_specs=[ANY] * n, out_specs=[ANY] * n, out_shape=[jax.ShapeDtypeStruct(a.shape, a.dtype) for a in arrays],
        scratch_shapes=[pltpu.SemaphoreType.DMA((n,)), pltpu.SemaphoreType.DMA((n,))],
    )(*arrays)


def sum_slots(q, name):
    n_s, n_rows, n_cols = q.shape
    tr = _rows(n_rows, n_cols * (q.dtype.itemsize * n_s + 4))

    def body(q_ref, o_ref):
        acc = q_ref[0].astype(F32)
        for s in range(1, n_s):
            acc = acc + q_ref[s].astype(F32)
        o_ref[...] = acc

    return pl.pallas_call(
        body, name=name, grid=(n_rows // tr,), in_specs=[pl.BlockSpec((n_s, tr, n_cols), lambda i: (0, i, 0))],
        out_specs=pl.BlockSpec((tr, n_cols), lambda i: (i, 0)), out_shape=jax.ShapeDtypeStruct((n_rows, n_cols), F32),
        compiler_params=_params(("parallel",)),
    )(q)


def gather_all_devices(s, name):
    def body(s_ref, o_ref, send_sems, recv_sems, local_sem):
        x, y, c, _ = _place()
        me = 4 * x + 2 * y + c
        local = pltpu.make_async_copy(s_ref, o_ref.at[me], local_sem)
        local.start()
        sends = []
        for f in range(1, N_DEV):
            px, py, pc = x ^ (f >> 2), y ^ ((f >> 1) & 1), c ^ (f & 1)
            sends.append(_remote(s_ref, o_ref.at[me], send_sems.at[f - 1], recv_sems.at[f - 1], (px, py, pc)))
            sends[-1].start()
        for f in range(1, N_DEV):
            px, py, pc = x ^ (f >> 2), y ^ ((f >> 1) & 1), c ^ (f & 1)
            landed = o_ref.at[4 * px + 2 * py + pc]
            _remote(landed, landed, send_sems.at[f - 1], recv_sems.at[f - 1], (px, py, pc)).wait_recv()
        for cp in sends:
            cp.wait_send()
        local.wait()

    return pl.pallas_call(
        body, name=name, in_specs=[ANY], out_specs=ANY, out_shape=jax.ShapeDtypeStruct((N_DEV,) + s.shape, s.dtype),
        scratch_shapes=[pltpu.SemaphoreType.DMA((N_DEV - 1,)), pltpu.SemaphoreType.DMA((N_DEV - 1,)), pltpu.SemaphoreType.DMA],
    )(s)


FFN_NAMES = ("ffn1_wg", "ffn1_wu", "ffn1_wd", "ffn2_wg", "ffn2_wu", "ffn2_wd")
XATTN_NAMES = ("x_wq", "x_wk", "x_wv", "x_wo")
BIG_NAMES = FFN_NAMES + XATTN_NAMES + ("mix_w_in", "mix_w_out", "pool_w")
SMALL_NAMES = ("norms", "mem_norm", "attn_sinks", "sgu_ln_g", "sgu_ln_b", "sgu_w", "sgu_b", "pool_scale")
WEIGHT_ORDER = ("norms", "mem_norm") + BIG_NAMES[:-1] + ("attn_sinks", "sgu_ln_g", "sgu_ln_b", "sgu_w", "sgu_b", "pool_w", "pool_scale")
COLUMN_CUT = ("x_wo", "mix_w_in")
N_POOL = len(POOL_WINDOWS)
BLOCK_ORDER = (("ffn1", 0), ("mix", 0), ("xattn", 0), ("ffn2", 0), ("ffn1", 1), ("mix", 1), ("xattn", 1), ("ffn2", 1))
PREFETCH_AT = ((1, 2), (3,), (), (4,), (5, 6), (7,), (), ())


def block_weight_names(kind, layer):
    if kind == "mix":
        return ("mix_w_in", "mix_w_out") if layer == 0 else ("pool_w",)
    return XATTN_NAMES if kind == "xattn" else tuple(f"{kind}_{part}" for part in ("wg", "wu", "wd"))


def _to_matmul_layout(name, g):
    n_j, n_rows, n_cols = g.shape
    if name in COLUMN_CUT:
        return g.transpose(1, 0, 2).reshape(n_rows, n_j * n_cols)
    if name == "pool_w":
        return g.reshape(n_j, N_POOL, n_rows // N_POOL, n_cols).transpose(1, 0, 2, 3).reshape(N_POOL, n_j * n_rows // N_POOL, n_cols)
    if name in ("x_wq", "x_wk", "x_wv", "mix_w_out"):
        return g.reshape(n_j * n_rows, n_cols)
    return g


def _from_matmul_layout(name, d):
    if name in COLUMN_CUT:
        n_rows, wide = d.shape
        return d.reshape(n_rows, N_CHIPS, wide // N_CHIPS).transpose(1, 0, 2)
    if name == "pool_w":
        n_g, n_in, n_cols = d.shape
        return d.reshape(n_g, N_CHIPS, n_in // N_CHIPS, n_cols).transpose(1, 0, 2, 3).reshape(N_CHIPS, n_g * n_in // N_CHIPS, n_cols)
    if name in ("x_wq", "x_wk", "x_wv", "mix_w_out"):
        return d.reshape(N_CHIPS, d.shape[0] // N_CHIPS, d.shape[1])
    return d


def _as3(w):
    return w.reshape(w.shape[0], -1, w.shape[-1])


def _pack(arrays, row_multiple):
    flat = jnp.concatenate([a.reshape(-1) for a in arrays])
    per = LANES * row_multiple
    total = -(-flat.shape[0] // per) * per
    return jnp.pad(flat, (0, total - flat.shape[0])).reshape(total // LANES, LANES)


def _unpack(packed, like):
    flat, out, at = packed.reshape(-1), [], 0
    for a in like:
        out.append(flat[at:at + a.size].reshape(a.shape))
        at += a.size
    return out


def kernel(x, mem, norms, mem_norm, ffn1_wg, ffn1_wu, ffn1_wd, ffn2_wg, ffn2_wu, ffn2_wd, x_wq, x_wk, x_wv, x_wo, mix_w_in, mix_w_out, attn_sinks, sgu_ln_g, sgu_ln_b, sgu_w, sgu_b, pool_w, pool_scale, loss_target, m_norms, m_mem_norm, m_ffn1_wg, m_ffn1_wu, m_ffn1_wd, m_ffn2_wg, m_ffn2_wu, m_ffn2_wd, m_x_wq, m_x_wk, m_x_wv, m_x_wo, m_mix_w_in, m_mix_w_out, m_attn_sinks, m_sgu_ln_g, m_sgu_ln_b, m_sgu_w, m_sgu_b, m_pool_w, m_pool_scale, v_norms, v_mem_norm, v_ffn1_wg, v_ffn1_wu, v_ffn1_wd, v_ffn2_wg, v_ffn2_wu, v_ffn2_wd, v_x_wq, v_x_wk, v_x_wv, v_x_wo, v_mix_w_in, v_mix_w_out, v_attn_sinks, v_sgu_ln_g, v_sgu_ln_b, v_sgu_w, v_sgu_b, v_pool_w, v_pool_scale):
    given = dict(locals())
    w = {n: given[n] for n in WEIGHT_ORDER}
    mom = {n: given["m_" + n] for n in WEIGHT_ORDER}
    var = {n: given["v_" + n] for n in WEIGHT_ORDER}
    chip_id = 2 * lax.axis_index("x") + lax.axis_index("y")
    chip = chip_id.astype(jnp.int32).reshape(1)
    core = lax.axis_index("c").astype(jnp.int32).reshape(1)
    n_shard = norms.shape[-1]

    keys = [(name, l) for name in BIG_NAMES for l in range(_as3(w[name]).shape[0])]
    first_keys = [(name, 0) for name in block_weight_names(*BLOCK_ORDER[0])]
    cast = lambda name, l: cast_into_slot(_as3(w[name]), l, chip, f"cast_{name}")
    slot_of = {key: cast(*key) for key in first_keys}
    small_rows = jnp.concatenate([norms.reshape(-1, n_shard), pool_scale, jnp.zeros((15, n_shard), F32)], axis=0)
    small_slot = lax.dynamic_update_slice_in_dim(jnp.zeros((N_CHIPS,) + small_rows.shape, F32), small_rows[None], chip_id, axis=0)
    pending = {}
    pending["0a"] = gather_start([slot_of[first_keys[0]], small_slot], chip, "gather_start_0a")
    pending["0b"] = gather_start([slot_of[first_keys[1]]], pending["0a"][2][0], "gather_start_0b")
    pending["0c"] = gather_start([slot_of[first_keys[2]]], pending["0b"][2][0], "gather_start_0c")
    for key in keys:
        if key not in slot_of:
            slot_of[key] = cast(*key)


    def block_slots(k):
        kind, layer = BLOCK_ORDER[k]
        return [slot_of[name, layer if kind != "mix" else 0] for name in block_weight_names(kind, layer)]

    def start(k, after):
        pending[k] = gather_start(block_slots(k), after, f"gather_start_{k}")

    def finish(k, after):
        send_sems, recv_sems, bufs, _ = pending[k]
        return forward_to_sibling(gather_wait(send_sems, recv_sems, bufs, after, f"gather_wait_{k}"), f"gather_forward_{k}")

    def fetch(key, after):
        k = BLOCK_ORDER.index(key)
        names = block_weight_names(*key)
        if k == 0:
            late = lambda part, name: lambda after: _to_matmul_layout(name, finish(part, after)[0])
            bufs = [first_gate]
            weights = (_to_matmul_layout(names[0], first_gate), late("0b", names[1]), late("0c", names[2]))
        else:
            bufs = finish(k, after)
            weights = tuple(_to_matmul_layout(name, b) for name, b in zip(names, bufs))
        token = 0.0
        for ahead in PREFETCH_AT[k]:
            start(ahead, bufs[0])
            token = token + pending[ahead][3][0, 0]
        return weights, token

    in_flight, own_of, recv_of = [], {}, {}

    def land(after):
        tag, names, send_sems, recv_sems, parts, lands = in_flight.pop()
        parts, lands = scatter_wait(send_sems, recv_sems, parts, lands, after, f"grads_wait_{tag}")
        return tag, names, parts, lands

    def reduce_landed(tag, names, parts, lands):
        own = [sum_over_chips(t, q, chip, f"sum_{name}") for (name, _), t, q in zip(names, parts, lands)]
        for key, o, r in zip(names, own, swap_with_sibling(own, f"grads_halves_{tag}")):
            own_of[key], recv_of[key] = o, r

    def emit(key, grads_of, after):
        kind, layer = key
        tag = f"{kind}{layer}"
        names = [(name, layer if kind != "mix" else 0) for name in grads_of]
        landed = land(after) if in_flight else None
        full = [_from_matmul_layout(name, g) for name, g in grads_of.items()]
        from_sibling = swap_other_halves(full, f"grads_to_sibling_{tag}")
        chip_sums = [add_own_half(g, p, core, f"chip_sum_{name}") for (name, _), g, p in zip(names, full, from_sibling)]
        send_sems, recv_sems, parts, lands, token = scatter_start(chip_sums, after, f"grads_start_{tag}")
        in_flight.append((tag, names, send_sems, recv_sems, parts, lands))
        if landed:
            reduce_landed(*landed)
        return token[0, 0]

    first_gate, small_all = finish("0a", slot_of[keys[-1]])
    n_norm_rows = norms.shape[0] * norms.shape[1]
    norms_all = jnp.concatenate([small_all[j, :n_norm_rows].reshape(norms.shape) for j in range(N_CHIPS)], axis=-1)
    pool_scale_all = jnp.concatenate([small_all[j, n_norm_rows:n_norm_rows + 1] for j in range(N_CHIPS)], axis=-1)
    small = dict(attn_sinks=attn_sinks[0], sgu_ln_g=sgu_ln_g, sgu_ln_b=sgu_ln_b, sgu_w=sgu_w[0], sgu_b=sgu_b[0], pool_scale=pool_scale_all)

    loss_part, dx, g_small, g_norms, g_mem_norm, last_token = device_step(
        x[0], mem[0], loss_target[0], norms_all, mem_norm, small, fetch, emit)
    loss = lax.psum(0.5 * jnp.sum(loss_part) / x.shape[-1], ("x", "y", "c"))
    order = last_token.reshape(1, 1)

    grads, delta, new_m, new_v = {}, {}, {}, {}

    def update(name):
        n_l = _as3(w[name]).shape[0]
        res = adamw_from_halves(_as3(w[name]), [own_of[name, l] for l in range(n_l)], [recv_of[name, l] for l in range(n_l)],
                                _as3(mom[name]), _as3(var[name]), core, order, f"adamw_{name}")
        grads[name], delta[name], new_m[name], new_v[name] = (t.reshape(w[name].shape) for t in res)

    last_names = {name for name, _ in in_flight[0][1]}
    for name in BIG_NAMES:
        if name not in last_names:
            update(name)
    reduce_landed(*land(delta[BIG_NAMES[-1]]))
    for name in BIG_NAMES:
        if name in last_names:
            update(name)

    small_g = [g_norms, g_mem_norm, g_small["attn_sinks"], g_small["sgu_ln_g"], g_small["sgu_ln_b"], g_small["sgu_w"], g_small["sgu_b"],
               g_small["pool_scale"]]
    packed = _pack(small_g, 16)
    summed = sum_slots(gather_all_devices(packed, "small_grads_all"), "small_grads_sum")
    s_norms, s_mem, s_sinks, s_lg, s_lb, s_w, s_b, s_scale = _unpack(summed, small_g)
    grads["norms"] = lax.dynamic_slice_in_dim(s_norms, chip_id * n_shard, n_shard, axis=2)
    grads["pool_scale"] = lax.dynamic_slice_in_dim(s_scale, chip_id * n_shard, n_shard, axis=1)
    grads.update(mem_norm=s_mem, attn_sinks=s_sinks, sgu_ln_g=s_lg, sgu_ln_b=s_lb, sgu_w=s_w, sgu_b=s_b)
    like = [w[n] for n in SMALL_NAMES]
    packs = [_pack([src[n] for n in SMALL_NAMES], 16)[None] for src in (w, grads, mom, var)]
    for dst, t in zip((delta, new_m, new_v), adamw(*packs, "adamw_small")):
        for n, a in zip(SMALL_NAMES, _unpack(t[0], like)):
            dst[n] = a

    outs = [loss, dx[None]]
    for group in (grads, delta, new_m, new_v):
        outs += [group[n] for n in WEIGHT_ORDER]
    return tuple(outs)
```

```python
import functools

import jax
import jax.numpy as jnp
from jax import lax
from jax.experimental import pallas as pl
from jax.experimental.pallas import tpu as pltpu

F32 = jnp.float32
BF16 = jnp.bfloat16
MESH = pl.DeviceIdType.MESH

EPS = 1e-6
ROPE_THETA = 500000.0
ROPE_HALF = 8
HEAD_DIM = 64
N_Q_HEADS = 16
N_KV_HEADS = 2
Q_PER_KV = 8
BLOCK = 128
ATTN_WIDTH = 1024
KV_WIDTH = 128
QK_WIDTH = ATTN_WIDTH + KV_WIDTH
SGU_WIDTH = 1024
SGU_GROUPS = 8
POOL_WINDOWS = (2, 4, 8, 16)
POOL_HALO = 16
X_HEADS = 4
X_HEAD_DIM = 128
N_CHIPS = 4
N_DEV = 8

ADAM_LR = 0.001
ADAM_B1 = 0.9
ADAM_B2 = 0.999
ADAM_EPS = 1e-08
ADAM_WD = 0.01
ADAM_STEP = 10

VMEM_LIMIT_V7X = 52 * 1024 * 1024
MM_VMEM_BUDGET = 44 * 1024 * 1024
LANES = 128
ROW_TILE_BYTES = 6 * 1024 * 1024
MXU_FLOPS_V7X = 1.0e15
HBM_BYTES_PER_S_V7X = 3.0e12
VMEM_STORE_BYTES_PER_S = 4.0e12
MXU_WEIGHT_LOAD_ROWS = 192
GRID_STEP_S = 0.35e-6


def _params(sem):
    return pltpu.CompilerParams(dimension_semantics=sem, vmem_limit_bytes=VMEM_LIMIT_V7X)


def _pick(dim, pref, align):
    cands = [t for t in range(align, dim + 1, align) if dim % t == 0]
    small = [t for t in cands if t <= pref]
    if small and small[-1] * 2 >= min(pref, dim):
        return small[-1]
    return dim


def _rows(n_rows, bytes_per_row):
    want = max(16, min(1024, ROW_TILE_BYTES // max(1, bytes_per_row)))
    cands = [t for t in range(16, n_rows + 1, 16) if n_rows % t == 0 and t <= want]
    return cands[-1] if cands else n_rows


def _divisors(dim, align, most):
    return [t for t in range(align, min(dim, most) + 1, align) if dim % t == 0] or [dim]


def _mm_tiles(M, N, K, J, m_align, k_align, a_bytes, b_bytes, o_bytes, reduce, ta, products=1):
    best = None
    for tm in _divisors(M, m_align, 2048):
        for tn in _divisors(N, LANES, 2048):
            for tk in _divisors(K, k_align, 4096):
                split = K // tk > 1 or reduce
                vmem = 2 * (tm * tk * a_bytes + tk * tn * b_bytes + tm * tn * o_bytes) + tm * tn * 4 * products * (2 if split else 1)
                if ta:
                    vmem += tm * tk * a_bytes
                if vmem > MM_VMEM_BUDGET:
                    continue
                steps = J * (M // tm) * (N // tn) * (K // tk)
                mxu = 2.0 * J * M * N * K / MXU_FLOPS_V7X * (tm + MXU_WEIGHT_LOAD_ROWS) / tm
                acc = J * M * N * (K // tk) * 8 / VMEM_STORE_BYTES_PER_S if split else 0.0
                hbm = J * (M * K * a_bytes * (N // tn) + K * N * b_bytes * (M // tm) + M * N * o_bytes) / HBM_BYTES_PER_S_V7X
                cost = max(mxu + 0.5 * acc, hbm) + steps * GRID_STEP_S
                if best is None or cost < best[0]:
                    best = (cost, tm, tn, tk)
    return best[1:]


def _rsum8(v):
    r, c = v.shape
    return v.reshape(r // 8, 8, c).sum(axis=0)


def _mm(a, b, *, name, ta=False, tb=False, batch="none", groups=0, a_cb=False, b_cb=False, o_cb=False,
        out_dtype=F32, more=(), extras=(), epilogue=None, out_dtypes=None):
    J = groups or (a.shape[0] if a.ndim == 3 else (b.shape[0] if b.ndim == 3 else 1))
    a2, b2 = a.shape[-2:], b.shape[-2:]
    M, K = (a2[1], a2[0]) if ta else a2
    N, Kb = b2 if tb else (b2[1], b2[0])
    if a_cb:
        if ta:
            M //= J
        else:
            K //= J
    if b_cb:
        if tb:
            Kb //= J
        else:
            N //= J
    assert K == Kb, (name, a.shape, b.shape)
    reduce = batch == "reduce"
    out_dtypes = list(out_dtypes or [out_dtype])
    n_terms = 1 + len(more)
    o_bytes = sum(jnp.dtype(d).itemsize for d in out_dtypes) + sum(e.dtype.itemsize for e in extras)
    n_prod = n_terms if epilogue is not None else 1
    tm, tn, tk = _mm_tiles(M, N, K, J, LANES if ta else 16, LANES if (not ta or tb) else 16, a.dtype.itemsize * n_terms,
                           b.dtype.itemsize * n_terms, o_bytes, reduce, ta, n_prod)
    nm, nn, nk = M // tm, N // tn, K // tk
    if reduce:
        grid = (nm, nn, J, nk)
        unpack = lambda m, n, j, k: (j, m, n, k)
        sem = ("parallel", "parallel", "arbitrary", "arbitrary")
    else:
        grid = (J, nm, nn, nk)
        unpack = lambda j, m, n, k: (j, m, n, k)
        sem = ("parallel", "parallel", "parallel", "arbitrary")

    def a_map(*g):
        j, m, n, k = unpack(*g)
        r, c = (k, m) if ta else (m, k)
        if a_cb:
            c = c + j * (nm if ta else nk)
        return (j, r, c) if a.ndim == 3 else (r, c)

    def b_map(*g):
        j, m, n, k = unpack(*g)
        r, c = (n, k) if tb else (k, n)
        if b_cb:
            c = c + j * (nk if tb else nn)
        return (j, r, c) if b.ndim == 3 else (r, c)

    def o_map(*g):
        j, m, n, k = unpack(*g)
        if o_cb:
            return (m, n + j * nn)
        return (j, m, n) if batch == "map" else (m, n)

    a_blk = (tk, tm) if ta else (tm, tk)
    b_blk = (tn, tk) if tb else (tk, tn)
    a_spec = pl.BlockSpec(((None,) + a_blk) if a.ndim == 3 else a_blk, a_map)
    b_spec = pl.BlockSpec(((None,) + b_blk) if b.ndim == 3 else b_blk, b_map)
    if o_cb:
        out_shape, o_blk = (M, N * J), (tm, tn)
    elif batch == "map":
        out_shape, o_blk = (J, M, N), (None, tm, tn)
    else:
        out_shape, o_blk = (M, N), (tm, tn)
    o_spec = pl.BlockSpec(o_blk, o_map)
    dims = (((0 if ta else 1,), (1 if tb else 0,)), ((), ()))
    red_axes = (2, 3) if reduce else (3,)
    split = reduce or nk > 1
    n_ex, n_out = len(extras), len(out_dtypes)

    def body(*refs):
        ab_refs, ex_refs = refs[:2 * n_terms], refs[2 * n_terms:2 * n_terms + n_ex]
        o_refs, acc = refs[2 * n_terms + n_ex:2 * n_terms + n_ex + n_out], refs[2 * n_terms + n_ex + n_out:]
        prods = [lax.dot_general(ab_refs[2 * t][...], ab_refs[2 * t + 1][...], dims, preferred_element_type=F32) for t in range(n_terms)]
        if epilogue is None:
            prods = [functools.reduce(lambda p, q: p + q, prods)]

        def finish(vals):
            outs = epilogue(vals, [e[...] for e in ex_refs]) if epilogue is not None else vals
            for o_ref, val in zip(o_refs, outs):
                o_ref[...] = val.astype(o_ref.dtype)

        if not split:
            finish(prods)
            return
        first = functools.reduce(jnp.logical_and, [pl.program_id(ax) == 0 for ax in red_axes])
        last = functools.reduce(jnp.logical_and, [pl.program_id(ax) == grid[ax] - 1 for ax in red_axes])

        @pl.when(first)
        def _():
            for acc_ref, prod in zip(acc, prods):
                acc_ref[...] = prod

        @pl.when(jnp.logical_not(first))
        def _():
            for acc_ref, prod in zip(acc, prods):
                acc_ref[...] += prod

        @pl.when(last)
        def _():
            finish([acc_ref[...] for acc_ref in acc])

    operands = [a, b] + [t for pair in more for t in pair] + list(extras)
    res = pl.pallas_call(
        body, name=name, grid=grid, in_specs=[a_spec, b_spec] * n_terms + [o_spec] * n_ex, out_specs=[o_spec] * n_out,
        out_shape=[jax.ShapeDtypeStruct(out_shape, d) for d in out_dtypes],
        scratch_shapes=[pltpu.VMEM((tm, tn), F32)] * (n_prod if split else 0), compiler_params=_params(sem),
    )(*operands)
    return res if epilogue is not None else res[0]


def _rowwise(fn, tiled, whole, outs, accs, *, name):
    n_rows = tiled[0].shape[0]
    row_bytes = sum(t.shape[1] * t.dtype.itemsize for t in tiled) + sum(c * jnp.dtype(d).itemsize for c, d in outs)
    tr = _rows(n_rows, row_bytes)
    n_t, n_w, n_o = len(tiled), len(whole), len(outs)

    def body(*refs):
        i = pl.program_id(0)
        t_refs, w_refs = refs[:n_t], refs[n_t:n_t + n_w]
        o_refs, a_refs = refs[n_t + n_w:n_t + n_w + n_o], refs[n_t + n_w + n_o:]
        o_vals, a_vals = fn(i, *[r[...] for r in t_refs], *[r[...] for r in w_refs])
        for r, v in zip(o_refs, o_vals):
            r[...] = v.astype(r.dtype)
        if a_refs:
            @pl.when(i == 0)
            def _():
                for r in a_refs:
                    r[...] = jnp.zeros_like(r)
            for r, v in zip(a_refs, a_vals):
                r[...] += v

    in_specs = [pl.BlockSpec((tr, t.shape[1]), lambda i: (i, 0)) for t in tiled]
    in_specs += [pl.BlockSpec(w.shape, lambda i, nd=w.ndim: (0,) * nd) for w in whole]
    out_specs = [pl.BlockSpec((tr, c), lambda i: (i, 0)) for c, _ in outs]
    out_specs += [pl.BlockSpec(s, lambda i, nd=len(s): (0,) * nd) for s, _ in accs]
    out_shape = [jax.ShapeDtypeStruct((n_rows, c), d) for c, d in outs]
    out_shape += [jax.ShapeDtypeStruct(s, d) for s, d in accs]
    res = pl.pallas_call(
        body, name=name, grid=(n_rows // tr,), in_specs=in_specs, out_specs=out_specs, out_shape=out_shape,
        compiler_params=_params(("arbitrary",) if accs else ("parallel",)),
    )(*tiled, *whole)
    return res


def _rms_stats(x):
    r = lax.rsqrt(jnp.mean(x * x, axis=-1, keepdims=True) + EPS)
    return x * r, r


def rms_fwd(x, g, out_dtype, name):
    def fn(i, x, g):
        xhat, _ = _rms_stats(x)
        return [xhat * g], []
    return _rowwise(fn, [x], [g], [(x.shape[1], out_dtype)], [], name=name)[0]


def postnorm_res(x, y, g, s, name):
    def fn(i, x, y, g):
        yhat, _ = _rms_stats(y)
        return [x + s * (yhat * g)], []
    return _rowwise(fn, [x, y], [g], [(x.shape[1], F32)], [], name=name)[0]


def rms_bwd(xin, g, douts, scale, add, out_dtype, name):
    n_d = len(douts)

    def fn(i, x, *rest):
        ds, rest = rest[:n_d], rest[n_d:]
        ad = rest[0] if add is not None else None
        g = rest[-1]
        xhat, r = _rms_stats(x)
        d = ds[0].astype(F32)
        for e in ds[1:]:
            d = d + e.astype(F32)
        if scale != 1.0:
            d = d * scale
        dg = _rsum8(d * xhat)
        dxhat = d * g
        dx = r * (dxhat - xhat * jnp.mean(dxhat * xhat, axis=-1, keepdims=True))
        if ad is not None:
            dx = dx + ad
        return [dx], [dg]

    tiled = [xin, *douts] + ([add] if add is not None else [])
    dx, dg = _rowwise(fn, tiled, [g], [(xin.shape[1], out_dtype)], [((8, xin.shape[1]), F32)], name=name)
    return dx, dg


def _silu_parts(g):
    sg = 1.0 / (1.0 + jnp.exp(-g))
    return g * sg, sg


def _swiglu_tiles(products, saved):
    u, = products
    return [u, _silu_parts(saved[0].astype(F32))[0] * u]


def _swiglu_bwd_tiles(products, saved):
    da, = products
    g, u = saved[0].astype(F32), saved[1].astype(F32)
    s, sg = _silu_parts(g)
    return [da * u * (sg * (1.0 + g * (1.0 - sg))), da * s]


def scale_cols(y, s, name):
    def fn(i, y, s):
        return [y * s], []
    return _rowwise(fn, [y], [s], [(y.shape[1], F32)], [], name=name)[0]


def scale_cols_bwd(dm, y, s, name):
    def fn(i, dm, y, s):
        return [dm * s], [_rsum8(dm * y)]
    return _rowwise(fn, [dm, y], [s], [(y.shape[1], BF16)], [((8, y.shape[1]), F32)], name=name)


def loss_and_grad(y, target, name):
    n_feat = y.shape[1]

    def fn(i, y, t):
        e = y - t
        return [e * (1.0 / n_feat)], [_rsum8(e * e)]
    dy, part = _rowwise(fn, [y, target], [], [(n_feat, F32)], [((8, n_feat), F32)], name=name)
    return part, dy


def cast_into_slot(w3, layer, chip, name):
    _, n_rows, n_cols = w3.shape
    tr = _rows(n_rows, n_cols * 6)

    def body(chip_ref, w_ref, o_ref):
        o_ref[...] = w_ref[...].astype(BF16)

    grid_spec = pltpu.PrefetchScalarGridSpec(
        num_scalar_prefetch=1, grid=(n_rows // tr,),
        in_specs=[pl.BlockSpec((None, tr, n_cols), lambda i, chip_ref: (layer, i, 0))],
        out_specs=pl.BlockSpec((None, tr, n_cols), lambda i, chip_ref: (chip_ref[0], i, 0)))
    return pl.pallas_call(
        body, name=name, grid_spec=grid_spec, out_shape=jax.ShapeDtypeStruct((N_CHIPS, n_rows, n_cols), BF16),
        compiler_params=_params(("parallel",)),
    )(chip, w3)


def _adam_update(w, g, m, v):
    c1 = 1.0 / (1.0 - ADAM_B1 ** ADAM_STEP)
    c2 = 1.0 / (1.0 - ADAM_B2 ** ADAM_STEP)
    m = ADAM_B1 * m + (1.0 - ADAM_B1) * g
    v = ADAM_B2 * v + (1.0 - ADAM_B2) * (g * g)
    return -ADAM_LR * ((m * c1) / (jnp.sqrt(v * c2) + ADAM_EPS) + ADAM_WD * w), m, v


def adamw(w, g, m, v, name):
    n_l, n_rows, n_cols = w.shape
    tr = _rows(n_rows, n_cols * 4 * 7)

    def body(w_ref, g_ref, m_ref, v_ref, d_ref, mo_ref, vo_ref):
        d_ref[...], mo_ref[...], vo_ref[...] = _adam_update(w_ref[...], g_ref[...], m_ref[...], v_ref[...])

    spec = pl.BlockSpec((None, tr, n_cols), lambda l, i: (l, i, 0))
    shp = jax.ShapeDtypeStruct(w.shape, F32)
    return pl.pallas_call(
        body, name=name, grid=(n_l, n_rows // tr), in_specs=[spec] * 4, out_specs=[spec] * 3, out_shape=[shp] * 3,
        compiler_params=_params(("parallel", "parallel")),
    )(w, g, m, v)


def adamw_from_halves(w, own, recv, m, v, core, order, name):
    n_l, n_rows, n_cols = w.shape
    half = n_rows // 2
    tr = _rows(half, n_cols * 4 * 9)
    per = half // tr

    def body(core_ref, *refs):
        w_ref, m_ref, v_ref, order_ref = refs[:4]
        own_refs, recv_refs = refs[4:4 + n_l], refs[4 + n_l:4 + 2 * n_l]
        g_ref, d_ref, mo_ref, vo_ref = refs[4 + 2 * n_l:]
        l, h = pl.program_id(0), pl.program_id(1)
        mine = h == core_ref[0]
        g = jnp.where(mine, own_refs[0][...], recv_refs[0][...])
        for k in range(1, n_l):
            g = jnp.where(l == k, jnp.where(mine, own_refs[k][...], recv_refs[k][...]), g)
        g = g + order_ref[...]
        g_ref[...] = g
        d_ref[...], mo_ref[...], vo_ref[...] = _adam_update(w_ref[...], g, m_ref[...], v_ref[...])

    full = pl.BlockSpec((None, tr, n_cols), lambda l, h, i, core_ref: (l, h * per + i, 0))

    def piece(layer, is_own):
        def index(l, h, i, core_ref):
            used = (l == layer) & ((h == core_ref[0]) == is_own)
            return (jnp.where(used, i, 0), 0)
        return pl.BlockSpec((tr, n_cols), index)

    grid_spec = pltpu.PrefetchScalarGridSpec(
        num_scalar_prefetch=1, grid=(n_l, 2, per),
        in_specs=[full] * 3 + [pl.BlockSpec((1, 1), lambda l, h, i, core_ref: (0, 0))]
        + [piece(k, True) for k in range(n_l)] + [piece(k, False) for k in range(n_l)],
        out_specs=[full] * 4)
    return pl.pallas_call(
        body, name=name, grid_spec=grid_spec, out_shape=[jax.ShapeDtypeStruct(w.shape, F32)] * 4,
        compiler_params=_params(("parallel", "parallel", "parallel")),
    )(core, w, m, v, order, *own, *recv)


def rope_tables(n_tok):
    inv = ROPE_THETA ** (-jnp.arange(ROPE_HALF, dtype=F32) * 2.0 / (2 * ROPE_HALF))
    ang = jnp.arange(n_tok, dtype=F32)[:, None] * inv[None, :]
    cos, sin = jnp.cos(ang), jnp.sin(ang)
    rest = HEAD_DIM - 2 * ROPE_HALF
    one, zero, z8 = jnp.ones((n_tok, rest), F32), jnp.zeros((n_tok, rest), F32), jnp.zeros((n_tok, ROPE_HALF), F32)
    c = jnp.concatenate([cos, cos, one], axis=1)
    s1 = jnp.concatenate([-sin, z8, zero], axis=1)
    s2 = jnp.concatenate([z8, sin, zero], axis=1)
    two = lambda t: jnp.concatenate([t, t], axis=1)
    return two(c), two(s1), two(s2)


def rope_apply(x, tabs, n_col_blocks, inverse, out_dtype, name):
    n_tok = x.shape[0]
    tr = _rows(n_tok, LANES * 4 * 6)

    def body(x_ref, c_ref, s1_ref, s2_ref, o_ref):
        x = x_ref[...].astype(F32)
        if inverse:
            out = x * c_ref[...] + pltpu.roll(x * s1_ref[...], ROPE_HALF, 1) + pltpu.roll(x * s2_ref[...], LANES - ROPE_HALF, 1)
        else:
            out = x * c_ref[...] + pltpu.roll(x, LANES - ROPE_HALF, 1) * s1_ref[...] + pltpu.roll(x, ROPE_HALF, 1) * s2_ref[...]
        o_ref[...] = out.astype(o_ref.dtype)

    tab_spec = pl.BlockSpec((tr, LANES), lambda i, c: (i, 0))
    blk = pl.BlockSpec((tr, LANES), lambda i, c: (i, c))
    return pl.pallas_call(
        body, name=name, grid=(n_tok // tr, n_col_blocks), in_specs=[blk, tab_spec, tab_spec, tab_spec], out_specs=blk,
        out_shape=jax.ShapeDtypeStruct((n_tok, n_col_blocks * LANES), out_dtype), compiler_params=_params(("parallel", "parallel")),
    )(x, *tabs)


def _swa_probs(q, k, sink, n):
    rows = Q_PER_KV * BLOCK
    s = lax.dot_general(q, k, (((1,), (1,)), ((), ())), preferred_element_type=F32) * (HEAD_DIM ** -0.5)
    qi = lax.broadcasted_iota(jnp.int32, (rows, 2 * BLOCK), 0) & (BLOCK - 1)
    kj = lax.broadcasted_iota(jnp.int32, (rows, 2 * BLOCK), 1)
    rel = qi + BLOCK - kj
    valid = (rel >= 0) & (rel < BLOCK) & ((n > 0) | (kj >= BLOCK))
    s = jnp.where(valid, s, -1e30)
    m = jnp.maximum(jnp.max(s, axis=-1, keepdims=True), sink)
    e = jnp.exp(s - m)
    es = jnp.exp(sink - m)
    inv = 1.0 / (jnp.sum(e, axis=-1, keepdims=True) + es)
    return e * inv, es * inv


def _swa_specs(n_blocks):
    q_spec = pl.BlockSpec((Q_PER_KV, BLOCK, HEAD_DIM), lambda h, n: (h, n, 0))
    prev = pl.BlockSpec((None, BLOCK, HEAD_DIM), lambda h, n: (h, jnp.maximum(n - 1, 0), 0))
    cur = pl.BlockSpec((None, BLOCK, HEAD_DIM), lambda h, n: (h, n, 0))
    sink = pl.BlockSpec((None, Q_PER_KV * BLOCK, 1), lambda h, n: (h, 0, 0))
    return q_spec, prev, cur, sink


def swa_fwd(q, k, v, sink_rows, name):
    n_tok = q.shape[1]
    q_spec, prev, cur, sink = _swa_specs(n_tok // BLOCK)

    def body(q_ref, kp_ref, kc_ref, vp_ref, vc_ref, s_ref, o_ref):
        n = pl.program_id(1)
        qq = q_ref[...].reshape(Q_PER_KV * BLOCK, HEAD_DIM)
        kk = jnp.concatenate([kp_ref[...], kc_ref[...]], axis=0)
        vv = jnp.concatenate([vp_ref[...], vc_ref[...]], axis=0)
        p, _ = _swa_probs(qq, kk, s_ref[...], n)
        o = jnp.dot(p.astype(BF16), vv, preferred_element_type=F32)
        o_ref[...] = o.reshape(Q_PER_KV, BLOCK, HEAD_DIM).astype(o_ref.dtype)

    return pl.pallas_call(
        body, name=name, grid=(N_KV_HEADS, n_tok // BLOCK), in_specs=[q_spec, prev, cur, prev, cur, sink], out_specs=q_spec,
        out_shape=jax.ShapeDtypeStruct(q.shape, BF16), compiler_params=_params(("parallel", "parallel")),
    )(q, k, k, v, v, sink_rows)


def swa_bwd(q, k, v, sink_rows, do, name):
    n_tok = q.shape[1]
    nb = n_tok // BLOCK
    q_spec, prev, cur, sink = _swa_specs(nb)
    rows = Q_PER_KV * BLOCK

    def body(q_ref, kp_ref, kc_ref, vp_ref, vc_ref, s_ref, do_ref, dq_ref, dkp_ref, dkc_ref, dvp_ref, dvc_ref, ds_ref):
        n = pl.program_id(1)
        qq = q_ref[...].reshape(rows, HEAD_DIM)
        dd = do_ref[...].reshape(rows, HEAD_DIM)
        kk = jnp.concatenate([kp_ref[...], kc_ref[...]], axis=0)
        vv = jnp.concatenate([vp_ref[...], vc_ref[...]], axis=0)
        p, ps = _swa_probs(qq, kk, s_ref[...], n)
        dp = lax.dot_general(dd, vv, (((1,), (1,)), ((), ())), preferred_element_type=F32)
        delta = jnp.sum(p * dp, axis=-1, keepdims=True)
        ds = (p * (dp - delta) * (HEAD_DIM ** -0.5)).astype(BF16)
        dq = jnp.dot(ds, kk, preferred_element_type=F32)
        dk = lax.dot_general(ds, qq, (((0,), (0,)), ((), ())), preferred_element_type=F32)
        dv = lax.dot_general(p.astype(BF16), dd, (((0,), (0,)), ((), ())), preferred_element_type=F32)
        dq_ref[...] = dq.reshape(Q_PER_KV, BLOCK, HEAD_DIM).astype(dq_ref.dtype)
        dkp_ref[...] = dk[:BLOCK]
        dkc_ref[...] = dk[BLOCK:]
        dvp_ref[...] = dv[:BLOCK]
        dvc_ref[...] = dv[BLOCK:]
        dsink = jnp.broadcast_to(-ps * delta, (rows, LANES)).reshape(Q_PER_KV, BLOCK, LANES)
        ds_ref[...] = jnp.sum(dsink, axis=1)

    part = pl.BlockSpec((None, None, BLOCK, HEAD_DIM), lambda h, n: (h, n, 0, 0))
    part_shape = jax.ShapeDtypeStruct((N_KV_HEADS, nb, BLOCK, HEAD_DIM), F32)
    return pl.pallas_call(
        body, name=name, grid=(N_KV_HEADS, nb), in_specs=[q_spec, prev, cur, prev, cur, sink, q_spec],
        out_specs=[q_spec, part, part, part, part, pl.BlockSpec((None, None, Q_PER_KV, LANES), lambda h, n: (h, n, 0, 0))],
        out_shape=[jax.ShapeDtypeStruct(q.shape, BF16), part_shape, part_shape, part_shape, part_shape,
                   jax.ShapeDtypeStruct((N_KV_HEADS, nb, Q_PER_KV, LANES), F32)],
        compiler_params=_params(("parallel", "parallel")),
    )(q, k, k, v, v, sink_rows, do)


def _to_heads(t, n_heads):
    return t.reshape(t.shape[0], n_heads, HEAD_DIM).transpose(1, 0, 2)


def _from_heads(t):
    return t.transpose(1, 0, 2).reshape(t.shape[1], -1)


def _fold_kv_grad(prev, cur):
    shifted = jnp.concatenate([prev[:, 1:], jnp.zeros_like(prev[:, :1])], axis=1)
    tot = (cur + shifted).reshape(N_KV_HEADS, -1, HEAD_DIM)
    return _from_heads(tot)


def _x_probs(qh, kh):
    s = lax.dot_general(qh, kh, (((1,), (1,)), ((), ())), preferred_element_type=F32) * (X_HEAD_DIM ** -0.5)
    e = jnp.exp(s - jnp.max(s, axis=-1, keepdims=True))
    return e * (1.0 / jnp.sum(e, axis=-1, keepdims=True))


def xattn_fwd(q, k, v, name):
    n_tok, width = q.shape
    n_mem = k.shape[0]
    tq = _pick(n_tok, 512, 16)

    def body(q_ref, k_ref, v_ref, o_ref):
        for h in range(X_HEADS):
            cols = slice(h * X_HEAD_DIM, (h + 1) * X_HEAD_DIM)
            p = _x_probs(q_ref[:, cols], k_ref[:, cols])
            o_ref[:, cols] = jnp.dot(p.astype(BF16), v_ref[:, cols], preferred_element_type=F32).astype(o_ref.dtype)

    row = pl.BlockSpec((tq, width), lambda i: (i, 0))
    mem = pl.BlockSpec((n_mem, width), lambda i: (0, 0))
    return pl.pallas_call(
        body, name=name, grid=(n_tok // tq,), in_specs=[row, mem, mem], out_specs=row,
        out_shape=jax.ShapeDtypeStruct(q.shape, BF16), compiler_params=_params(("parallel",)),
    )(q, k, v)


def xattn_bwd(q, k, v, do, name):
    n_tok, width = q.shape
    n_mem = k.shape[0]
    tq = _pick(n_tok, 512, 16)

    def body(q_ref, k_ref, v_ref, do_ref, dq_ref, dk_ref, dv_ref):
        @pl.when(pl.program_id(0) == 0)
        def _():
            dk_ref[...] = jnp.zeros_like(dk_ref)
            dv_ref[...] = jnp.zeros_like(dv_ref)

        for h in range(X_HEADS):
            cols = slice(h * X_HEAD_DIM, (h + 1) * X_HEAD_DIM)
            qh, kh, vh, dh = q_ref[:, cols], k_ref[:, cols], v_ref[:, cols], do_ref[:, cols]
            p = _x_probs(qh, kh)
            dp = lax.dot_general(dh, vh, (((1,), (1,)), ((), ())), preferred_element_type=F32)
            delta = jnp.sum(p * dp, axis=-1, keepdims=True)
            ds = (p * (dp - delta) * (X_HEAD_DIM ** -0.5)).astype(BF16)
            dq_ref[:, cols] = jnp.dot(ds, kh, preferred_element_type=F32).astype(dq_ref.dtype)
            dk_ref[:, cols] += lax.dot_general(ds, qh, (((0,), (0,)), ((), ())), preferred_element_type=F32)
            dv_ref[:, cols] += lax.dot_general(p.astype(BF16), dh, (((0,), (0,)), ((), ())), preferred_element_type=F32)

    row = pl.BlockSpec((tq, width), lambda i: (i, 0))
    mem = pl.BlockSpec((n_mem, width), lambda i: (0, 0))
    return pl.pallas_call(
        body, name=name, grid=(n_tok // tq,), in_specs=[row, mem, mem, row], out_specs=[row, mem, mem],
        out_shape=[jax.ShapeDtypeStruct(q.shape, BF16), jax.ShapeDtypeStruct(k.shape, F32), jax.ShapeDtypeStruct(k.shape, F32)],
        compiler_params=_params(("arbitrary",)),
    )(q, k, v, do)


GELU_C = 0.7978845608028654
GELU_A = 0.044715


def _gelu_parts(x):
    x2 = x * x
    t = jnp.tanh(GELU_C * x * (1.0 + GELU_A * x2))
    y = 0.5 * x * (1.0 + t)
    dy = 0.5 * (1.0 + t) + 0.5 * x * (1.0 - t * t) * GELU_C * (1.0 + 3.0 * GELU_A * x2)
    return y, dy


def _sgu_norm(v, ln_g, ln_b):
    mu = jnp.mean(v, axis=-1, keepdims=True)
    vc = v - mu
    r = lax.rsqrt(jnp.mean(vc * vc, axis=-1, keepdims=True) + EPS)
    xhat = vc * r
    return xhat * ln_g + ln_b, xhat, r


def _causal_weights(w_ref):
    i = lax.broadcasted_iota(jnp.int32, (BLOCK, BLOCK), 0)
    j = lax.broadcasted_iota(jnp.int32, (BLOCK, BLOCK), 1)
    return [jnp.where(i >= j, w_ref[g], 0.0).astype(BF16) for g in range(SGU_GROUPS)]


def sgu_fwd(u_pre, v_pre, ln_g, ln_b, w_s, bias_rows, name):
    n_tok = u_pre.shape[0]
    tm = _pick(n_tok, 512, BLOCK)

    def body(u_ref, v_ref, g_ref, b_ref, w_ref, bb_ref, o_ref):
        vn, _, _ = _sgu_norm(_gelu_parts(v_ref[...])[0], g_ref[...], b_ref[...])
        vn = vn.astype(BF16)
        wc = _causal_weights(w_ref)
        for c in range(tm // BLOCK):
            rows = slice(c * BLOCK, (c + 1) * BLOCK)
            for g in range(SGU_GROUPS):
                cols = slice(g * LANES, (g + 1) * LANES)
                mixed = jnp.dot(wc[g], vn[rows, cols], preferred_element_type=F32) + bb_ref[g]
                u = _gelu_parts(u_ref[rows, cols])[0]
                o_ref[rows, cols] = (u * mixed).astype(o_ref.dtype)

    row = pl.BlockSpec((tm, SGU_WIDTH), lambda i: (i, 0))
    vec = pl.BlockSpec((1, SGU_WIDTH), lambda i: (0, 0))
    mat = pl.BlockSpec((SGU_GROUPS, BLOCK, LANES), lambda i: (0, 0, 0))
    return pl.pallas_call(
        body, name=name, grid=(n_tok // tm,), in_specs=[row, row, vec, vec, mat, mat], out_specs=row,
        out_shape=jax.ShapeDtypeStruct((n_tok, SGU_WIDTH), BF16), compiler_params=_params(("parallel",)),
    )(u_pre, v_pre, ln_g, ln_b, w_s, bias_rows)


def sgu_bwd(u_pre, v_pre, ln_g, ln_b, w_s, bias_rows, dgate, name):
    n_tok = u_pre.shape[0]
    tm = _pick(n_tok, 512, BLOCK)

    def body(u_ref, v_ref, g_ref, b_ref, w_ref, bb_ref, dg_ref, du_ref, dv_ref, dw_ref, db_ref, dlg_ref, dlb_ref, dvn_ref):
        @pl.when(pl.program_id(0) == 0)
        def _():
            dw_ref[...] = jnp.zeros_like(dw_ref)
            db_ref[...] = jnp.zeros_like(db_ref)
            dlg_ref[...] = jnp.zeros_like(dlg_ref)
            dlb_ref[...] = jnp.zeros_like(dlb_ref)

        gv, dgv = _gelu_parts(v_ref[...])
        vn, xhat, r = _sgu_norm(gv, g_ref[...], b_ref[...])
        vn = vn.astype(BF16)
        wc = _causal_weights(w_ref)
        for c in range(tm // BLOCK):
            rows = slice(c * BLOCK, (c + 1) * BLOCK)
            for g in range(SGU_GROUPS):
                cols = slice(g * LANES, (g + 1) * LANES)
                vt = vn[rows, cols]
                mixed = jnp.dot(wc[g], vt, preferred_element_type=F32) + bb_ref[g]
                u, du_dpre = _gelu_parts(u_ref[rows, cols])
                dgate_t = dg_ref[rows, cols].astype(F32)
                du_ref[rows, cols] = (dgate_t * mixed * du_dpre).astype(du_ref.dtype)
                dmix = dgate_t * u
                dmix_b = dmix.astype(BF16)
                db_ref[g] += dmix
                dw_ref[g] += lax.dot_general(dmix_b, vt, (((1,), (1,)), ((), ())), preferred_element_type=F32)
                dvn_ref[rows, cols] = lax.dot_general(wc[g], dmix_b, (((0,), (0,)), ((), ())), preferred_element_type=F32)
        dvn = dvn_ref[...]
        dlg_ref[...] += _rsum8(dvn * xhat)
        dlb_ref[...] += _rsum8(dvn)
        dxhat = dvn * g_ref[...]
        dgv_in = r * (dxhat - jnp.mean(dxhat, axis=-1, keepdims=True) - xhat * jnp.mean(dxhat * xhat, axis=-1, keepdims=True))
        dv_ref[...] = (dgv_in * dgv).astype(dv_ref.dtype)

    row = pl.BlockSpec((tm, SGU_WIDTH), lambda i: (i, 0))
    vec = pl.BlockSpec((1, SGU_WIDTH), lambda i: (0, 0))
    mat = pl.BlockSpec((SGU_GROUPS, BLOCK, LANES), lambda i: (0, 0, 0))
    part = pl.BlockSpec((8, SGU_WIDTH), lambda i: (0, 0))
    mat_shape = jax.ShapeDtypeStruct((SGU_GROUPS, BLOCK, LANES), F32)
    part_shape = jax.ShapeDtypeStruct((8, SGU_WIDTH), F32)
    act_shape = jax.ShapeDtypeStruct((n_tok, SGU_WIDTH), BF16)
    return pl.pallas_call(
        body, name=name, grid=(n_tok // tm,), in_specs=[row, row, vec, vec, mat, mat, row],
        out_specs=[row, row, mat, mat, part, part], out_shape=[act_shape, act_shape, mat_shape, mat_shape, part_shape, part_shape],
        scratch_shapes=[pltpu.VMEM((tm, SGU_WIDTH), F32)], compiler_params=_params(("arbitrary",)),
    )(u_pre, v_pre, ln_g, ln_b, w_s, bias_rows, dgate)


def _pool_tile(n_tok):
    return _pick(n_tok, 256, POOL_HALO)


def pool_fwd(h, name):
    n_tok, width = h.shape
    gw = width // len(POOL_WINDOWS)
    tm = _pool_tile(n_tok)
    per = tm // POOL_HALO

    def body(cur_ref, halo_ref, o_ref, buf_ref):
        i = pl.program_id(0)
        buf_ref[0:POOL_HALO, :] = jnp.where(i > 0, halo_ref[...], 0.0)
        buf_ref[POOL_HALO:, :] = cur_ref[...]
        tok = i * tm + lax.broadcasted_iota(jnp.int32, (tm, 1), 0)
        for g, w in enumerate(POOL_WINDOWS):
            cols = slice(g * gw, (g + 1) * gw)
            acc = buf_ref[POOL_HALO:, cols]
            for j in range(1, w):
                acc = acc + buf_ref[POOL_HALO - j:POOL_HALO - j + tm, cols]
            cnt = jnp.minimum(tok + 1, w).astype(F32)
            o_ref[:, cols] = (acc / cnt - cur_ref[:, cols]).astype(o_ref.dtype)

    return pl.pallas_call(
        body, name=name, grid=(n_tok // tm,),
        in_specs=[pl.BlockSpec((tm, width), lambda i: (i, 0)),
                  pl.BlockSpec((POOL_HALO, width), lambda i: (jnp.maximum(i * per - 1, 0), 0))],
        out_specs=pl.BlockSpec((tm, width), lambda i: (i, 0)), out_shape=jax.ShapeDtypeStruct(h.shape, BF16),
        scratch_shapes=[pltpu.VMEM((tm + POOL_HALO, width), F32)], compiler_params=_params(("parallel",)),
    )(h, h)


def pool_bwd(dp, name):
    n_tok, width = dp.shape
    gw = width // len(POOL_WINDOWS)
    tm = _pool_tile(n_tok)
    per = tm // POOL_HALO
    n_steps = n_tok // tm

    def body(cur_ref, halo_ref, o_ref, buf_ref):
        i = pl.program_id(0)
        tok = i * tm + lax.broadcasted_iota(jnp.int32, (tm, 1), 0)
        for g, w in enumerate(POOL_WINDOWS):
            cols = slice(g * gw, (g + 1) * gw)
            cnt = jnp.minimum(tok + 1, w).astype(F32)
            buf_ref[0:tm, cols] = cur_ref[:, cols] / cnt
            buf_ref[tm:, cols] = jnp.where(i < n_steps - 1, halo_ref[:, cols] / float(w), 0.0)
        for g, w in enumerate(POOL_WINDOWS):
            cols = slice(g * gw, (g + 1) * gw)
            acc = buf_ref[0:tm, cols]
            for j in range(1, w):
                acc = acc + buf_ref[j:j + tm, cols]
            o_ref[:, cols] = acc - cur_ref[:, cols]

    return pl.pallas_call(
        body, name=name, grid=(n_steps,),
        in_specs=[pl.BlockSpec((tm, width), lambda i: (i, 0)),
                  pl.BlockSpec((POOL_HALO, width), lambda i: (jnp.minimum((i + 1) * per, n_tok // POOL_HALO - 1), 0))],
        out_specs=pl.BlockSpec((tm, width), lambda i: (i, 0)), out_shape=jax.ShapeDtypeStruct(dp.shape, F32),
        scratch_shapes=[pltpu.VMEM((tm + POOL_HALO, width), F32)], compiler_params=_params(("parallel",)),
    )(dp, dp)


def _ffn_fwd(x, ga, gb, wg, wu, wd, tag):
    h = rms_fwd(x, ga, BF16, f"{tag}_prenorm")
    G = _mm(h, wg, batch="map", out_dtype=BF16, name=f"{tag}_gate")
    wu = wu(G) if callable(wu) else wu
    U, A = _mm(h, wu, batch="map", extras=[G], epilogue=_swiglu_tiles, out_dtypes=[BF16] * 2, name=f"{tag}_up")
    wd = wd(A) if callable(wd) else wd
    y = _mm(A, wd, batch="reduce", name=f"{tag}_down")
    return postnorm_res(x, y, gb, 0.5, f"{tag}_postnorm"), (x, h, G, U, A, y, wg, wu, wd)


def _ffn_bwd(res, ga, gb, dx2, tag):
    x, h, G, U, A, y, wg, wu, wd = res
    dy, dgb = rms_bwd(y, gb, [dx2], 0.5, None, BF16, f"{tag}_postnorm_bwd")
    dG, dU = _mm(dy, wd, tb=True, batch="map", extras=[G, U], epilogue=_swiglu_bwd_tiles, out_dtypes=[BF16] * 2, name=f"{tag}_down_dx")
    dwd = _mm(A, dy, ta=True, batch="map", name=f"{tag}_down_dw", out_dtype=BF16)
    dwg = _mm(h, dG, ta=True, batch="map", name=f"{tag}_gate_dw", out_dtype=BF16)
    dwu = _mm(h, dU, ta=True, batch="map", name=f"{tag}_up_dw", out_dtype=BF16)
    dh = _mm(dG, wg, tb=True, batch="reduce", more=[(dU, wu)], name=f"{tag}_gate_up_dx")
    dx, dga = rms_bwd(x, ga, [dh], 1.0, dx2, F32, f"{tag}_prenorm_bwd")
    return dx, dga, dgb, dwg, dwu, dwd


def _sink_rows(sinks):
    return jnp.repeat(sinks.reshape(N_KV_HEADS, Q_PER_KV), BLOCK, axis=1)[..., None]


def _attn_sgu_fwd(x, g_pre, g_post, w_in, w_out, sinks, ln_g, ln_b, sgu_w, bias_rows, tabs, tag):
    h = rms_fwd(x, g_pre, BF16, f"{tag}_prenorm")
    z = _mm(h, w_in, name=f"{tag}_in")
    qk = rope_apply(z, tabs, QK_WIDTH // LANES, False, BF16, f"{tag}_rope")
    q = _to_heads(qk[:, :ATTN_WIDTH], N_Q_HEADS)
    k = _to_heads(qk[:, ATTN_WIDTH:], N_KV_HEADS)
    v = _to_heads(z[:, QK_WIDTH:QK_WIDTH + KV_WIDTH].astype(BF16), N_KV_HEADS)
    o = swa_fwd(q, k, v, _sink_rows(sinks), f"{tag}_swa")
    u_pre = z[:, QK_WIDTH + KV_WIDTH:QK_WIDTH + KV_WIDTH + SGU_WIDTH]
    v_pre = z[:, QK_WIDTH + KV_WIDTH + SGU_WIDTH:]
    gate = sgu_fwd(u_pre, v_pre, ln_g, ln_b, sgu_w, bias_rows, f"{tag}_sgu")
    cat = jnp.concatenate([_from_heads(o), gate], axis=1)
    m = _mm(cat, w_out, name=f"{tag}_out")
    return postnorm_res(x, m, g_post, 1.0, f"{tag}_postnorm"), (x, h, q, k, v, u_pre, v_pre, cat, m)


def _attn_sgu_bwd(res, g_pre, g_post, w_in, w_out, sinks, ln_g, ln_b, sgu_w, bias_rows, tabs, dx2, tag):
    x, h, q, k, v, u_pre, v_pre, cat, m = res
    dm, dg_post = rms_bwd(m, g_post, [dx2], 1.0, None, BF16, f"{tag}_postnorm_bwd")
    dcat = _mm(dm, w_out, tb=True, out_dtype=BF16, name=f"{tag}_out_dx")
    dw_out = _mm(cat, dm, ta=True, name=f"{tag}_out_dw", out_dtype=BF16)
    do = _to_heads(dcat[:, :ATTN_WIDTH], N_Q_HEADS)
    dq, dkp, dkc, dvp, dvc, dsink = swa_bwd(q, k, v, _sink_rows(sinks), do, f"{tag}_swa_bwd")
    d_sinks = jnp.sum(dsink[..., 0], axis=1).reshape(1, N_Q_HEADS)
    dqk_rot = jnp.concatenate([_from_heads(dq).astype(F32), _fold_kv_grad(dkp, dkc)], axis=1)
    dqk = rope_apply(dqk_rot, tabs, QK_WIDTH // LANES, True, BF16, f"{tag}_rope_bwd")
    dv = _fold_kv_grad(dvp, dvc).astype(BF16)
    du_pre, dv_pre, dw_s, dbias, dlg, dlb = sgu_bwd(u_pre, v_pre, ln_g, ln_b, sgu_w, bias_rows, dcat[:, ATTN_WIDTH:], f"{tag}_sgu_bwd")
    dz = jnp.concatenate([dqk, dv, du_pre, dv_pre], axis=1)
    dw_in = _mm(h, dz, ta=True, name=f"{tag}_in_dw", out_dtype=BF16)
    dh = _mm(dz, w_in, tb=True, name=f"{tag}_in_dx")
    dx, dg_pre = rms_bwd(x, g_pre, [dh], 1.0, dx2, F32, f"{tag}_prenorm_bwd")
    causal = jnp.tril(jnp.ones((BLOCK, BLOCK), F32))
    small = dict(attn_sinks=d_sinks, sgu_ln_g=jnp.sum(dlg, axis=0, keepdims=True), sgu_ln_b=jnp.sum(dlb, axis=0, keepdims=True),
                 sgu_w=(dw_s * causal[None])[None], sgu_b=jnp.sum(dbias, axis=-1)[None])
    return dx, dg_pre, dg_post, dw_in, dw_out, small


def _pool_mix_fwd(x, g_pre, g_post, pool_w, pool_scale, tag):
    hf = rms_fwd(x, g_pre, F32, f"{tag}_prenorm")
    pooled = pool_fwd(hf, f"{tag}_pool")
    n_g = len(POOL_WINDOWS)
    ypre = _mm(pooled, pool_w, batch="map", groups=n_g, a_cb=True, o_cb=True, name=f"{tag}_proj")
    m = scale_cols(ypre, pool_scale, f"{tag}_scale")
    return postnorm_res(x, m, g_post, 1.0, f"{tag}_postnorm"), (x, pooled, ypre, m)


def _pool_mix_bwd(res, g_pre, g_post, pool_w, pool_scale, dx2, tag):
    x, pooled, ypre, m = res
    n_g = len(POOL_WINDOWS)
    dm, dg_post = rms_bwd(m, g_post, [dx2], 1.0, None, F32, f"{tag}_postnorm_bwd")
    dypre, dscale = scale_cols_bwd(dm, ypre, pool_scale, f"{tag}_scale_bwd")
    dpooled = _mm(dypre, pool_w, tb=True, batch="map", groups=n_g, a_cb=True, o_cb=True, name=f"{tag}_proj_dx")
    dpw = _mm(pooled, dypre, ta=True, batch="map", groups=n_g, a_cb=True, b_cb=True, name=f"{tag}_proj_dw", out_dtype=BF16)
    dhf = pool_bwd(dpooled, f"{tag}_pool_bwd")
    dx, dg_pre = rms_bwd(x, g_pre, [dhf], 1.0, dx2, F32, f"{tag}_prenorm_bwd")
    return dx, dg_pre, dg_post, dpw, jnp.sum(dscale, axis=0, keepdims=True)


def _xattn_fwd(x, mem, g_pre, g_post, g_mem, wq, wk, wv, wo, tag):
    h = rms_fwd(x, g_pre, BF16, f"{tag}_prenorm")
    mem_n = rms_fwd(mem, g_mem, BF16, f"{tag}_memnorm")
    q = _mm(h, wq, out_dtype=BF16, name=f"{tag}_q")
    k = _mm(mem_n, wk, out_dtype=BF16, name=f"{tag}_k")
    v = _mm(mem_n, wv, out_dtype=BF16, name=f"{tag}_v")
    o = xattn_fwd(q, k, v, f"{tag}_core")
    r = _mm(o, wo, name=f"{tag}_o")
    return postnorm_res(x, r, g_post, 1.0, f"{tag}_postnorm"), (x, h, mem_n, q, k, v, o, r)


def _xattn_bwd(res, mem, g_pre, g_post, g_mem, wq, wk, wv, wo, dx2, tag):
    x, h, mem_n, q, k, v, o, r = res
    dr, dg_post = rms_bwd(r, g_post, [dx2], 1.0, None, BF16, f"{tag}_postnorm_bwd")
    do = _mm(dr, wo, tb=True, out_dtype=BF16, name=f"{tag}_o_dx")
    dwo = _mm(o, dr, ta=True, name=f"{tag}_o_dw", out_dtype=BF16)
    dq, dk, dv = xattn_bwd(q, k, v, do, f"{tag}_core_bwd")
    dk, dv = dk.astype(BF16), dv.astype(BF16)
    dwq = _mm(h, dq, ta=True, name=f"{tag}_q_dw", out_dtype=BF16)
    dwk = _mm(mem_n, dk, ta=True, name=f"{tag}_k_dw", out_dtype=BF16)
    dwv = _mm(mem_n, dv, ta=True, name=f"{tag}_v_dw", out_dtype=BF16)
    dh = _mm(dq, wq, tb=True, name=f"{tag}_q_dx")
    dmem1 = _mm(dk, wk, tb=True, name=f"{tag}_k_dx")
    dmem2 = _mm(dv, wv, tb=True, name=f"{tag}_v_dx")
    _, dg_mem = rms_bwd(mem, g_mem, [dmem1, dmem2], 1.0, None, BF16, f"{tag}_memnorm_bwd")
    dx, dg_pre = rms_bwd(x, g_pre, [dh], 1.0, dx2, F32, f"{tag}_prenorm_bwd")
    return dx, dg_pre, dg_post, dg_mem, dwq, dwk, dwv, dwo


def _rowsum8(part):
    return jnp.sum(part, axis=0, keepdims=True)


def device_step(x, mem, target, norms, mem_norm, small, fetch, emit):
    n_tok = x.shape[0]
    tabs = rope_tables(n_tok)
    bias_rows = jnp.broadcast_to(small["sgu_b"][:, :, None], (SGU_GROUPS, BLOCK, LANES))
    gn = lambda l, i: norms[l, i][None, :]
    gm = lambda l: mem_norm[l][None, :]
    mix0 = (small["attn_sinks"], small["sgu_ln_g"], small["sgu_ln_b"], small["sgu_w"], bias_rows, tabs)

    wts, saved = {}, []
    for l in range(2):
        wts["ffn1", l], tok = fetch(("ffn1", l), x)
        x, r1 = _ffn_fwd(x, gn(l, 0) + tok, gn(l, 1), *wts["ffn1", l], f"l{l}_ffn1")
        wts["mix", l], tok = fetch(("mix", l), x)
        if l == 0:
            x, r2 = _attn_sgu_fwd(x, gn(l, 2) + tok, gn(l, 3), *wts["mix", l], *mix0, f"l{l}_mix")
        else:
            x, r2 = _pool_mix_fwd(x, gn(l, 2) + tok, gn(l, 3), *wts["mix", l], small["pool_scale"], f"l{l}_mix")
        wts["xattn", l], tok = fetch(("xattn", l), x)
        x, r3 = _xattn_fwd(x, mem, gn(l, 4) + tok, gn(l, 5), gm(l), *wts["xattn", l], f"l{l}_xattn")
        wts["ffn2", l], tok = fetch(("ffn2", l), x)
        x, r4 = _ffn_fwd(x, gn(l, 6) + tok, gn(l, 7), *wts["ffn2", l], f"l{l}_ffn2")
        saved.append((r1, r2, r3, r4))

    loss_part, dx = loss_and_grad(x, target, "loss")

    g_norm_rows = [[None] * 8, [None] * 8]
    g_mem_rows = [None, None]
    g_small = {}
    tok = 0.0
    for l in (1, 0):
        r1, r2, r3, r4 = saved[l]
        dx, g_norm_rows[l][6], g_norm_rows[l][7], dwg, dwu, dwd = _ffn_bwd(r4, gn(l, 6), gn(l, 7) + tok, dx, f"l{l}_ffn2")
        tok = emit(("ffn2", l), dict(ffn2_wg=dwg, ffn2_wu=dwu, ffn2_wd=dwd), dx)
        dx, g_norm_rows[l][4], g_norm_rows[l][5], g_mem_rows[l], dwq, dwk, dwv, dwo = _xattn_bwd(
            r3, mem, gn(l, 4), gn(l, 5) + tok, gm(l), *wts["xattn", l], dx, f"l{l}_xattn")
        tok = emit(("xattn", l), dict(x_wq=dwq, x_wk=dwk, x_wv=dwv, x_wo=dwo), dx)
        if l == 0:
            dx, g_norm_rows[l][2], g_norm_rows[l][3], dw_in, dw_out, sm = _attn_sgu_bwd(
                r2, gn(l, 2), gn(l, 3) + tok, *wts["mix", l], *mix0, dx, f"l{l}_mix")
            tok = emit(("mix", l), dict(mix_w_in=dw_in, mix_w_out=dw_out), dx)
            g_small.update(sm)
        else:
            dx, g_norm_rows[l][2], g_norm_rows[l][3], dpw, dscale = _pool_mix_bwd(
                r2, gn(l, 2), gn(l, 3) + tok, *wts["mix", l], small["pool_scale"], dx, f"l{l}_mix")
            tok = emit(("mix", l), dict(pool_w=dpw), dx)
            g_small["pool_scale"] = dscale
        dx, g_norm_rows[l][0], g_norm_rows[l][1], dwg, dwu, dwd = _ffn_bwd(r1, gn(l, 0), gn(l, 1) + tok, dx, f"l{l}_ffn1")
        tok = emit(("ffn1", l), dict(ffn1_wg=dwg, ffn1_wu=dwu, ffn1_wd=dwd), dx)
    g_norms = jnp.stack([jnp.concatenate([_rowsum8(p) for p in g_norm_rows[l]], axis=0) for l in range(2)])
    g_mem_norm = jnp.concatenate([_rowsum8(p) for p in g_mem_rows], axis=0)
    return loss_part, dx, g_small, g_norms, g_mem_norm, tok


ANY = pl.BlockSpec(memory_space=pl.ANY)


def _place():
    x, y, c = lax.axis_index("x"), lax.axis_index("y"), lax.axis_index("c")
    other_chips = [(1 - x, y), (x, 1 - y), (1 - x, 1 - y)]
    return x, y, c, other_chips


def _half_rows(core, n_rows):
    half = n_rows // 2
    return pl.ds(pl.multiple_of(core * half, 16), half)


def _remote(src, dst, send_sem, recv_sem, device):
    return pltpu.make_async_remote_copy(src_ref=src, dst_ref=dst, send_sem=send_sem, recv_sem=recv_sem,
                                        device_id=device, device_id_type=MESH)


HBM = pl.BlockSpec(memory_space=pltpu.HBM)
SEM = pl.BlockSpec(memory_space=pltpu.SEMAPHORE)
DATAFLOW = pltpu.SideEffectType.DATAFLOW_SIDE_EFFECTING


def _chip_copies(bufs, send_sems, recv_sems):
    x, y, c, chips = _place()
    me = 2 * x + y
    sends, arrivals = [], []
    for i, buf in enumerate(bufs):
        rows = _half_rows(c, buf.shape[1])
        for r, (px, py) in enumerate(chips):
            mine, theirs = buf.at[me, rows], buf.at[2 * px + py, rows]
            sends.append(_remote(mine, mine, send_sems.at[3 * i + r], recv_sems.at[3 * i + r], (px, py, c)))
            arrivals.append(_remote(theirs, theirs, send_sems.at[3 * i + r], recv_sems.at[3 * i + r], (px, py, c)))
    return sends, arrivals


def copies_start(arrays, fresh, copies, n_sems, after, name):
    operands = [pltpu.with_memory_space_constraint(a, pltpu.HBM) for a in arrays]
    operands += [pltpu.with_memory_space_constraint(lax.empty(f.shape, f.dtype), pltpu.HBM) for f in fresh]
    n = len(operands)

    def body(*refs):
        send_sems, recv_sems = refs[n + 1], refs[n + 2]
        bufs, token = refs[n + 3:2 * n + 3], refs[2 * n + 3]
        for cp in copies(bufs, send_sems, recv_sems)[0]:
            cp.start()
        token[...] = jnp.zeros_like(token)

    res = pl.pallas_call(
        body, name=name, in_specs=[HBM] * n + [ANY],
        out_specs=[SEM, SEM] + [HBM] * n + [pl.BlockSpec(memory_space=pltpu.VMEM)],
        out_shape=[pltpu.SemaphoreType.DMA((n_sems,)), pltpu.SemaphoreType.DMA((n_sems,))]
        + [pltpu.HBM(o.shape, o.dtype) for o in operands] + [jax.ShapeDtypeStruct((8, LANES), F32)],
        input_output_aliases={i: 2 + i for i in range(n)}, compiler_params=pltpu.CompilerParams(has_side_effects=DATAFLOW),
    )(*operands, after)
    return res[0], res[1], list(res[2:2 + n]), res[2 + n]


def copies_wait(send_sems, recv_sems, bufs, copies, after, name):
    n = len(bufs)

    def body(*refs):
        for cp in copies(refs[:n], refs[n], refs[n + 1])[1]:
            cp.wait_send()
            cp.wait_recv()

    return list(pl.pallas_call(
        body, name=name, in_specs=[HBM] * n + [SEM, SEM] + [ANY] * len(after), out_specs=[HBM] * n,
        out_shape=[pltpu.HBM(b.shape, b.dtype) for b in bufs], input_output_aliases={i: i for i in range(n)},
        compiler_params=pltpu.CompilerParams(has_side_effects=DATAFLOW),
    )(*bufs, send_sems, recv_sems, *after))


def _to_sibling_copies(n_src, src_rows):
    def copies(bufs, send_sems, recv_sems):
        x, y, c, _ = _place()
        sends, arrivals = [], []
        for i in range(n_src):
            src, land = bufs[i], bufs[n_src + i]
            src = src.at[:, src_rows(c, src)] if src_rows is not None else src
            sends.append(_remote(src, land, send_sems.at[i], recv_sems.at[i], (x, y, 1 - c)))
            arrivals.append(_remote(land, land, send_sems.at[i], recv_sems.at[i], (x, y, 1 - c)))
        return sends, arrivals
    return copies


def forward_to_sibling(bufs, name):
    n = len(bufs)

    def body(*refs):
        out = refs[n:2 * n]
        send_sems, recv_sems = refs[2 * n:]
        x, y, c, chips = _place()
        copies = []
        for i in range(n):
            for r, (px, py) in enumerate(chips):
                mine = out[i].at[2 * px + py, _half_rows(c, out[i].shape[1])]
                copies.append(_remote(mine, mine, send_sems.at[3 * i + r], recv_sems.at[3 * i + r], (x, y, 1 - c)))
                copies[-1].start()
        for i in range(n):
            for r, (px, py) in enumerate(chips):
                theirs = out[i].at[2 * px + py, _half_rows(1 - c, out[i].shape[1])]
                _remote(theirs, theirs, send_sems.at[3 * i + r], recv_sems.at[3 * i + r], (x, y, 1 - c)).wait_recv()
        for cp in copies:
            cp.wait_send()

    return pl.pallas_call(
        body, name=name, in_specs=[ANY] * n, out_specs=[ANY] * n, input_output_aliases={i: i for i in range(n)},
        out_shape=[jax.ShapeDtypeStruct(b.shape, b.dtype) for b in bufs],
        scratch_shapes=[pltpu.SemaphoreType.DMA((3 * n,)), pltpu.SemaphoreType.DMA((3 * n,))],
    )(*bufs)


def add_own_half(g, p, core, name):
    n_j, n_rows, n_cols = g.shape
    half = n_rows // 2
    tr = _rows(half, n_cols * 10)

    def body(c_ref, g_ref, p_ref, o_ref):
        o_ref[...] = (g_ref[...] + p_ref[...]).astype(o_ref.dtype)

    grid_spec = pltpu.PrefetchScalarGridSpec(
        num_scalar_prefetch=1, grid=(n_j, half // tr),
        in_specs=[pl.BlockSpec((None, None, tr, n_cols), lambda j, i, c_ref: (j, c_ref[0], i, 0)),
                  pl.BlockSpec((None, tr, n_cols), lambda j, i, c_ref: (j, i, 0))],
        out_specs=pl.BlockSpec((None, tr, n_cols), lambda j, i, c_ref: (j, i, 0)))
    return pl.pallas_call(
        body, name=name, grid_spec=grid_spec, out_shape=jax.ShapeDtypeStruct((n_j, half, n_cols), BF16),
        compiler_params=_params(("parallel", "parallel")),
    )(core, g.reshape(n_j, 2, half, n_cols), p)


def _scatter_copies(n_parts):
    def copies(bufs, send_sems, recv_sems):
        x, y, c, chips = _place()
        me = 2 * x + y
        sends, arrivals = [], []
        for i in range(n_parts):
            part, land = bufs[i], bufs[n_parts + i]
            for r, (px, py) in enumerate(chips):
                theirs = land.at[2 * px + py]
                sends.append(_remote(part.at[2 * px + py], land.at[me], send_sems.at[3 * i + r], recv_sems.at[3 * i + r], (px, py, c)))
                arrivals.append(_remote(theirs, theirs, send_sems.at[3 * i + r], recv_sems.at[3 * i + r], (px, py, c)))
        return sends, arrivals
    return copies


def sum_over_chips(own, got, chip, name):
    n_s, n_rows, n_cols = got.shape
    tr = _rows(n_rows, n_cols * (got.dtype.itemsize * (n_s + 1) + 4))

    def body(chip_ref, own_ref, *refs):
        got_refs, o_ref = refs[:n_s], refs[n_s]
        me = chip_ref[0]
        acc = jnp.where(me == 0, own_ref[...], got_refs[0][...]).astype(F32)
        for k in range(1, n_s):
            acc = acc + jnp.where(me == k, own_ref[...], got_refs[k][...]).astype(F32)
        o_ref[...] = acc

    def slot(k):
        return pl.BlockSpec((None, tr, n_cols), lambda i, chip_ref: (jnp.where(chip_ref[0] == k, (k + 1) % n_s, k), i, 0))

    grid_spec = pltpu.PrefetchScalarGridSpec(
        num_scalar_prefetch=1, grid=(n_rows // tr,),
        in_specs=[pl.BlockSpec((None, tr, n_cols), lambda i, chip_ref: (chip_ref[0], i, 0))] + [slot(k) for k in range(n_s)],
        out_specs=pl.BlockSpec((tr, n_cols), lambda i, chip_ref: (i, 0)))
    return pl.pallas_call(
        body, name=name, grid_spec=grid_spec, out_shape=jax.ShapeDtypeStruct((n_rows, n_cols), F32),
        compiler_params=_params(("parallel",)),
    )(chip, own, *([got] * n_s))


def sum_slots(q, name):
    n_s, n_rows, n_cols = q.shape
    tr = _rows(n_rows, n_cols * (q.dtype.itemsize * n_s + 4))

    def body(q_ref, o_ref):
        acc = q_ref[0].astype(F32)
        for s in range(1, n_s):
            acc = acc + q_ref[s].astype(F32)
        o_ref[...] = acc

    return pl.pallas_call(
        body, name=name, grid=(n_rows // tr,), in_specs=[pl.BlockSpec((n_s, tr, n_cols), lambda i: (0, i, 0))],
        out_specs=pl.BlockSpec((tr, n_cols), lambda i: (i, 0)), out_shape=jax.ShapeDtypeStruct((n_rows, n_cols), F32),
        compiler_params=_params(("parallel",)),
    )(q)


def gather_all_devices(s, name):
    def body(s_ref, o_ref, send_sems, recv_sems, local_sem):
        x, y, c, _ = _place()
        me = 4 * x + 2 * y + c
        local = pltpu.make_async_copy(s_ref, o_ref.at[me], local_sem)
        local.start()
        sends = []
        for f in range(1, N_DEV):
            px, py, pc = x ^ (f >> 2), y ^ ((f >> 1) & 1), c ^ (f & 1)
            sends.append(_remote(s_ref, o_ref.at[me], send_sems.at[f - 1], recv_sems.at[f - 1], (px, py, pc)))
            sends[-1].start()
        for f in range(1, N_DEV):
            px, py, pc = x ^ (f >> 2), y ^ ((f >> 1) & 1), c ^ (f & 1)
            landed = o_ref.at[4 * px + 2 * py + pc]
            _remote(landed, landed, send_sems.at[f - 1], recv_sems.at[f - 1], (px, py, pc)).wait_recv()
        for cp in sends:
            cp.wait_send()
        local.wait()

    return pl.pallas_call(
        body, name=name, in_specs=[ANY], out_specs=ANY, out_shape=jax.ShapeDtypeStruct((N_DEV,) + s.shape, s.dtype),
        scratch_shapes=[pltpu.SemaphoreType.DMA((N_DEV - 1,)), pltpu.SemaphoreType.DMA((N_DEV - 1,)), pltpu.SemaphoreType.DMA],
    )(s)


FFN_NAMES = ("ffn1_wg", "ffn1_wu", "ffn1_wd", "ffn2_wg", "ffn2_wu", "ffn2_wd")
XATTN_NAMES = ("x_wq", "x_wk", "x_wv", "x_wo")
BIG_NAMES = FFN_NAMES + XATTN_NAMES + ("mix_w_in", "mix_w_out", "pool_w")
SMALL_NAMES = ("norms", "mem_norm", "attn_sinks", "sgu_ln_g", "sgu_ln_b", "sgu_w", "sgu_b", "pool_scale")
WEIGHT_ORDER = ("norms", "mem_norm") + BIG_NAMES[:-1] + ("attn_sinks", "sgu_ln_g", "sgu_ln_b", "sgu_w", "sgu_b", "pool_w", "pool_scale")
COLUMN_CUT = ("x_wo", "mix_w_in")
N_POOL = len(POOL_WINDOWS)
BLOCK_ORDER = (("ffn1", 0), ("mix", 0), ("xattn", 0), ("ffn2", 0), ("ffn1", 1), ("mix", 1), ("xattn", 1), ("ffn2", 1))
PREFETCH_AT = ((1, 2), (3,), (), (4,), (5, 6), (7,), (), ())


def block_weight_names(kind, layer):
    if kind == "mix":
        return ("mix_w_in", "mix_w_out") if layer == 0 else ("pool_w",)
    return XATTN_NAMES if kind == "xattn" else tuple(f"{kind}_{part}" for part in ("wg", "wu", "wd"))


def _to_matmul_layout(name, g):
    n_j, n_rows, n_cols = g.shape
    if name in COLUMN_CUT:
        return g.transpose(1, 0, 2).reshape(n_rows, n_j * n_cols)
    if name == "pool_w":
        return g.reshape(n_j, N_POOL, n_rows // N_POOL, n_cols).transpose(1, 0, 2, 3).reshape(N_POOL, n_j * n_rows // N_POOL, n_cols)
    if name in ("x_wq", "x_wk", "x_wv", "mix_w_out"):
        return g.reshape(n_j * n_rows, n_cols)
    return g


def _from_matmul_layout(name, d):
    if name in COLUMN_CUT:
        n_rows, wide = d.shape
        return d.reshape(n_rows, N_CHIPS, wide // N_CHIPS).transpose(1, 0, 2)
    if name == "pool_w":
        n_g, n_in, n_cols = d.shape
        return d.reshape(n_g, N_CHIPS, n_in // N_CHIPS, n_cols).transpose(1, 0, 2, 3).reshape(N_CHIPS, n_g * n_in // N_CHIPS, n_cols)
    if name in ("x_wq", "x_wk", "x_wv", "mix_w_out"):
        return d.reshape(N_CHIPS, d.shape[0] // N_CHIPS, d.shape[1])
    return d


def _as3(w):
    return w.reshape(w.shape[0], -1, w.shape[-1])


def _pack(arrays, row_multiple):
    flat = jnp.concatenate([a.reshape(-1) for a in arrays])
    per = LANES * row_multiple
    total = -(-flat.shape[0] // per) * per
    return jnp.pad(flat, (0, total - flat.shape[0])).reshape(total // LANES, LANES)


def _unpack(packed, like):
    flat, out, at = packed.reshape(-1), [], 0
    for a in like:
        out.append(flat[at:at + a.size].reshape(a.shape))
        at += a.size
    return out


def kernel(x, mem, norms, mem_norm, ffn1_wg, ffn1_wu, ffn1_wd, ffn2_wg, ffn2_wu, ffn2_wd, x_wq, x_wk, x_wv, x_wo, mix_w_in, mix_w_out, attn_sinks, sgu_ln_g, sgu_ln_b, sgu_w, sgu_b, pool_w, pool_scale, loss_target, m_norms, m_mem_norm, m_ffn1_wg, m_ffn1_wu, m_ffn1_wd, m_ffn2_wg, m_ffn2_wu, m_ffn2_wd, m_x_wq, m_x_wk, m_x_wv, m_x_wo, m_mix_w_in, m_mix_w_out, m_attn_sinks, m_sgu_ln_g, m_sgu_ln_b, m_sgu_w, m_sgu_b, m_pool_w, m_pool_scale, v_norms, v_mem_norm, v_ffn1_wg, v_ffn1_wu, v_ffn1_wd, v_ffn2_wg, v_ffn2_wu, v_ffn2_wd, v_x_wq, v_x_wk, v_x_wv, v_x_wo, v_mix_w_in, v_mix_w_out, v_attn_sinks, v_sgu_ln_g, v_sgu_ln_b, v_sgu_w, v_sgu_b, v_pool_w, v_pool_scale):
    given = dict(locals())
    w = {n: given[n] for n in WEIGHT_ORDER}
    mom = {n: given["m_" + n] for n in WEIGHT_ORDER}
    var = {n: given["v_" + n] for n in WEIGHT_ORDER}
    chip_id = 2 * lax.axis_index("x") + lax.axis_index("y")
    chip = chip_id.astype(jnp.int32).reshape(1)
    core = lax.axis_index("c").astype(jnp.int32).reshape(1)
    n_shard = norms.shape[-1]

    keys = [(name, l) for name in BIG_NAMES for l in range(_as3(w[name]).shape[0])]
    first_keys = [(name, 0) for name in block_weight_names(*BLOCK_ORDER[0])]
    cast = lambda name, l: cast_into_slot(_as3(w[name]), l, chip, f"cast_{name}")
    slot_of = {key: cast(*key) for key in first_keys}
    small_rows = jnp.concatenate([norms.reshape(-1, n_shard), pool_scale, jnp.zeros((15, n_shard), F32)], axis=0)
    small_slot = lax.dynamic_update_slice_in_dim(jnp.zeros((N_CHIPS,) + small_rows.shape, F32), small_rows[None], chip_id, axis=0)
    pending = {}
    gather_start = lambda slots, after, name: copies_start(slots, [], _chip_copies, 3 * len(slots), after, name)
    pending["0a"] = gather_start([slot_of[first_keys[0]], small_slot], chip, "gather_start_0a")
    pending["0b"] = gather_start([slot_of[first_keys[1]]], pending["0a"][2][0], "gather_start_0b")
    pending["0c"] = gather_start([slot_of[first_keys[2]]], pending["0b"][2][0], "gather_start_0c")
    for key in keys:
        if key not in slot_of:
            slot_of[key] = cast(*key)


    def block_slots(k):
        kind, layer = BLOCK_ORDER[k]
        return [slot_of[name, layer if kind != "mix" else 0] for name in block_weight_names(kind, layer)]

    def start(k, after):
        pending[k] = gather_start(block_slots(k), after, f"gather_start_{k}")

    def finish(k, after):
        send_sems, recv_sems, bufs, _ = pending[k]
        return forward_to_sibling(copies_wait(send_sems, recv_sems, bufs, _chip_copies, [after], f"gather_wait_{k}"), f"gather_forward_{k}")

    def fetch(key, after):
        k = BLOCK_ORDER.index(key)
        names = block_weight_names(*key)
        if k == 0:
            late = lambda part, name: lambda after: _to_matmul_layout(name, finish(part, after)[0])
            bufs = [first_gate]
            weights = (_to_matmul_layout(names[0], first_gate), late("0b", names[1]), late("0c", names[2]))
        else:
            bufs = finish(k, after)
            weights = tuple(_to_matmul_layout(name, b) for name, b in zip(names, bufs))
        token = 0.0
        for ahead in PREFETCH_AT[k]:
            start(ahead, bufs[0])
            token = token + pending[ahead][3][0, 0]
        return weights, token

    to_sibling, to_chips, halves, own_of, recv_of = [], [], [], {}, {}
    other_half = lambda c, src: _half_rows(1 - c, src.shape[1])

    def sibling_start(tag, names, full, after):
        lands = [jax.ShapeDtypeStruct((g.shape[0], g.shape[1] // 2, g.shape[2]), g.dtype) for g in full]
        send_sems, recv_sems, bufs, token = copies_start(full, lands, _to_sibling_copies(len(full), other_half), len(full), after,
                                                         f"grads_to_sibling_{tag}")
        to_sibling.append((tag, names, send_sems, recv_sems, bufs))
        return token[0, 0]

    def chips_start(after):
        tag, names, send_sems, recv_sems, bufs = to_sibling.pop()
        n = len(names)
        bufs = copies_wait(send_sems, recv_sems, bufs, _to_sibling_copies(n, other_half), after, f"grads_from_sibling_{tag}")
        chip_sums = [add_own_half(g, p, core, f"chip_sum_{name}") for (name, _), g, p in zip(names, bufs[:n], bufs[n:])]
        send_sems, recv_sems, bufs, token = copies_start(chip_sums, chip_sums, _scatter_copies(n), 3 * n, after[0], f"grads_start_{tag}")
        to_chips.append((tag, names, send_sems, recv_sems, bufs))
        return token[0, 0]

    def chips_land(after):
        tag, names, send_sems, recv_sems, bufs = to_chips.pop()
        n = len(names)
        bufs = copies_wait(send_sems, recv_sems, bufs, _scatter_copies(n), after, f"grads_wait_{tag}")
        own = [sum_over_chips(t, q, chip, f"sum_{name}") for (name, _), t, q in zip(names, bufs[:n], bufs[n:])]
        send_sems, recv_sems, bufs, token = copies_start(own, own, _to_sibling_copies(n, None), n, after[0], f"grads_halves_{tag}")
        halves.append((tag, names, send_sems, recv_sems, bufs))
        return token[0, 0]

    def halves_land(after):
        for tag, names, send_sems, recv_sems, bufs in halves:
            n = len(names)
            bufs = copies_wait(send_sems, recv_sems, bufs, _to_sibling_copies(n, None), after, f"grads_halves_wait_{tag}")
            for key, o, r in zip(names, bufs[:n], bufs[n:]):
                own_of[key], recv_of[key] = o, r
        halves.clear()

    def emit(key, grads_of, after):
        kind, layer = key
        names = [(name, layer if kind != "mix" else 0) for name in grads_of]
        token = 0.0
        if to_chips:
            token = token + chips_land([after])
        if to_sibling:
            token = token + chips_start([after])
        full = [_from_matmul_layout(name, g) for name, g in grads_of.items()]
        return token + sibling_start(f"{kind}{layer}", names, full, after)

    first_gate, small_all = finish("0a", pending["0c"][2][0])
    n_norm_rows = norms.shape[0] * norms.shape[1]
    norms_all = jnp.concatenate([small_all[j, :n_norm_rows].reshape(norms.shape) for j in range(N_CHIPS)], axis=-1)
    pool_scale_all = jnp.concatenate([small_all[j, n_norm_rows:n_norm_rows + 1] for j in range(N_CHIPS)], axis=-1)
    small = dict(attn_sinks=attn_sinks[0], sgu_ln_g=sgu_ln_g, sgu_ln_b=sgu_ln_b, sgu_w=sgu_w[0], sgu_b=sgu_b[0], pool_scale=pool_scale_all)

    loss_part, dx, g_small, g_norms, g_mem_norm, last_token = device_step(
        x[0], mem[0], loss_target[0], norms_all, mem_norm, small, fetch, emit)
    loss = lax.psum(0.5 * jnp.sum(loss_part) / x.shape[-1], ("x", "y", "c"))
    last_names = {name for name, _ in to_sibling[0][1]}
    chips_land([dx])
    order = (last_token + chips_start([dx])).reshape(1, 1)
    halves_land([dx])

    grads, delta, new_m, new_v = {}, {}, {}, {}

    def update(name):
        n_l = _as3(w[name]).shape[0]
        res = adamw_from_halves(_as3(w[name]), [own_of[name, l] for l in range(n_l)], [recv_of[name, l] for l in range(n_l)],
                                _as3(mom[name]), _as3(var[name]), core, order, f"adamw_{name}")
        grads[name], delta[name], new_m[name], new_v[name] = (t.reshape(w[name].shape) for t in res)

    for name in BIG_NAMES:
        if name not in last_names:
            update(name)
    done = [delta[name] for name in BIG_NAMES if name not in last_names]
    chips_land(done)
    halves_land(done)
    for name in BIG_NAMES:
        if name in last_names:
            update(name)

    small_g = [g_norms, g_mem_norm, g_small["attn_sinks"], g_small["sgu_ln_g"], g_small["sgu_ln_b"], g_small["sgu_w"], g_small["sgu_b"],
               g_small["pool_scale"]]
    packed = _pack(small_g, 16)
    summed = sum_slots(gather_all_devices(packed, "small_grads_all"), "small_grads_sum")
    s_norms, s_mem, s_sinks, s_lg, s_lb, s_w, s_b, s_scale = _unpack(summed, small_g)
    grads["norms"] = lax.dynamic_slice_in_dim(s_norms, chip_id * n_shard, n_shard, axis=2)
    grads["pool_scale"] = lax.dynamic_slice_in_dim(s_scale, chip_id * n_shard, n_shard, axis=1)
    grads.update(mem_norm=s_mem, attn_sinks=s_sinks, sgu_ln_g=s_lg, sgu_ln_b=s_lb, sgu_w=s_w, sgu_b=s_b)
    like = [w[n] for n in SMALL_NAMES]
    packs = [_pack([src[n] for n in SMALL_NAMES], 16)[None] for src in (w, grads, mom, var)]
    for dst, t in zip((delta, new_m, new_v), adamw(*packs, "adamw_small")):
        for n, a in zip(SMALL_NAMES, _unpack(t[0], like)):
            dst[n] = a

    outs = [loss, dx[None]]
    for group in (grads, delta, new_m, new_v):
        outs += [group[n] for n in WEIGHT_ORDER]
    return tuple(outs)
```

```python
import functools

import jax
import jax.numpy as jnp
from jax import lax
from jax.experimental import pallas as pl
from jax.experimental.pallas import tpu as pltpu

F32 = jnp.float32
BF16 = jnp.bfloat16
MESH = pl.DeviceIdType.MESH

EPS = 1e-6
ROPE_THETA = 500000.0
ROPE_HALF = 8
HEAD_DIM = 64
N_Q_HEADS = 16
N_KV_HEADS = 2
Q_PER_KV = 8
BLOCK = 128
ATTN_WIDTH = 1024
KV_WIDTH = 128
QK_WIDTH = ATTN_WIDTH + KV_WIDTH
SGU_WIDTH = 1024
SGU_GROUPS = 8
POOL_WINDOWS = (2, 4, 8, 16)
POOL_HALO = 16
X_HEADS = 4
X_HEAD_DIM = 128
N_CHIPS = 4
N_DEV = 8

ADAM_LR = 0.001
ADAM_B1 = 0.9
ADAM_B2 = 0.999
ADAM_EPS = 1e-08
ADAM_WD = 0.01
ADAM_STEP = 10

VMEM_LIMIT_V7X = 52 * 1024 * 1024
MM_VMEM_BUDGET = 44 * 1024 * 1024
LANES = 128
ROW_TILE_BYTES = 6 * 1024 * 1024
MXU_FLOPS_V7X = 1.0e15
HBM_BYTES_PER_S_V7X = 3.0e12
VMEM_STORE_BYTES_PER_S = 4.0e12
MXU_WEIGHT_LOAD_ROWS = 192
GRID_STEP_S = 0.35e-6


def _params(sem):
    return pltpu.CompilerParams(dimension_semantics=sem, vmem_limit_bytes=VMEM_LIMIT_V7X)


def _pick(dim, pref, align):
    cands = [t for t in range(align, dim + 1, align) if dim % t == 0]
    small = [t for t in cands if t <= pref]
    if small and small[-1] * 2 >= min(pref, dim):
        return small[-1]
    return dim


def _rows(n_rows, bytes_per_row):
    want = max(16, min(1024, ROW_TILE_BYTES // max(1, bytes_per_row)))
    cands = [t for t in range(16, n_rows + 1, 16) if n_rows % t == 0 and t <= want]
    return cands[-1] if cands else n_rows


def _divisors(dim, align, most):
    return [t for t in range(align, min(dim, most) + 1, align) if dim % t == 0] or [dim]


def _mm_tiles(M, N, K, J, m_align, k_align, a_bytes, b_bytes, o_bytes, reduce, ta, products=1):
    best = None
    for tm in _divisors(M, m_align, 2048):
        for tn in _divisors(N, LANES, 2048):
            for tk in _divisors(K, k_align, 4096):
                split = K // tk > 1 or reduce
                vmem = 2 * (tm * tk * a_bytes + tk * tn * b_bytes + tm * tn * o_bytes) + tm * tn * 4 * products * (2 if split else 1)
                if ta:
                    vmem += tm * tk * a_bytes
                if vmem > MM_VMEM_BUDGET:
                    continue
                steps = J * (M // tm) * (N // tn) * (K // tk)
                mxu = 2.0 * J * M * N * K / MXU_FLOPS_V7X * (tm + MXU_WEIGHT_LOAD_ROWS) / tm
                acc = J * M * N * (K // tk) * 8 / VMEM_STORE_BYTES_PER_S if split else 0.0
                hbm = J * (M * K * a_bytes * (N // tn) + K * N * b_bytes * (M // tm) + M * N * o_bytes) / HBM_BYTES_PER_S_V7X
                cost = max(mxu + 0.5 * acc, hbm) + steps * GRID_STEP_S
                if best is None or cost < best[0]:
                    best = (cost, tm, tn, tk)
    return best[1:]


def _rsum8(v):
    r, c = v.shape
    return v.reshape(r // 8, 8, c).sum(axis=0)


def _mm(a, b, *, name, ta=False, tb=False, batch="none", groups=0, a_cb=False, b_cb=False, o_cb=False,
        out_dtype=F32, more=(), extras=(), epilogue=None, out_dtypes=None, order=None):
    J = groups or (a.shape[0] if a.ndim == 3 else (b.shape[0] if b.ndim == 3 else 1))
    a2, b2 = a.shape[-2:], b.shape[-2:]
    M, K = (a2[1], a2[0]) if ta else a2
    N, Kb = b2 if tb else (b2[1], b2[0])
    if a_cb:
        if ta:
            M //= J
        else:
            K //= J
    if b_cb:
        if tb:
            Kb //= J
        else:
            N //= J
    assert K == Kb, (name, a.shape, b.shape)
    reduce = batch == "reduce"
    out_dtypes = list(out_dtypes or [out_dtype])
    n_terms = 1 + len(more)
    o_bytes = sum(jnp.dtype(d).itemsize for d in out_dtypes) + sum(e.dtype.itemsize for e in extras)
    n_prod = n_terms if epilogue is not None else 1
    tm, tn, tk = _mm_tiles(M, N, K, J, LANES if ta else 16, LANES if (not ta or tb) else 16, a.dtype.itemsize * n_terms,
                           b.dtype.itemsize * n_terms, o_bytes, reduce, ta, n_prod)
    nm, nn, nk = M // tm, N // tn, K // tk
    if reduce:
        grid = (nm, nn, J, nk)
        unpack = lambda m, n, j, k: (j, m, n, k)
        sem = ("parallel", "parallel", "arbitrary", "arbitrary")
    else:
        grid = (J, nm, nn, nk)
        unpack = lambda j, m, n, k: (j, m, n, k)
        sem = ("parallel", "parallel", "parallel", "arbitrary")

    def a_map(*g):
        j, m, n, k = unpack(*g)
        r, c = (k, m) if ta else (m, k)
        if a_cb:
            c = c + j * (nm if ta else nk)
        return (j, r, c) if a.ndim == 3 else (r, c)

    def b_map(*g):
        j, m, n, k = unpack(*g)
        r, c = (n, k) if tb else (k, n)
        if b_cb:
            c = c + j * (nk if tb else nn)
        return (j, r, c) if b.ndim == 3 else (r, c)

    def o_map(*g):
        j, m, n, k = unpack(*g)
        if o_cb:
            return (m, n + j * nn)
        return (j, m, n) if batch == "map" else (m, n)

    a_blk = (tk, tm) if ta else (tm, tk)
    b_blk = (tn, tk) if tb else (tk, tn)
    a_spec = pl.BlockSpec(((None,) + a_blk) if a.ndim == 3 else a_blk, a_map)
    b_spec = pl.BlockSpec(((None,) + b_blk) if b.ndim == 3 else b_blk, b_map)
    if o_cb:
        out_shape, o_blk = (M, N * J), (tm, tn)
    elif batch == "map":
        out_shape, o_blk = (J, M, N), (None, tm, tn)
    else:
        out_shape, o_blk = (M, N), (tm, tn)
    o_spec = pl.BlockSpec(o_blk, o_map)
    dims = (((0 if ta else 1,), (1 if tb else 0,)), ((), ()))
    red_axes = (2, 3) if reduce else (3,)
    split = reduce or nk > 1
    n_ex, n_out = len(extras), len(out_dtypes)
    n_ord = 0 if order is None else 1

    def body(*refs):
        refs = refs[n_ord:]
        ab_refs, ex_refs = refs[:2 * n_terms], refs[2 * n_terms:2 * n_terms + n_ex]
        o_refs, acc = refs[2 * n_terms + n_ex:2 * n_terms + n_ex + n_out], refs[2 * n_terms + n_ex + n_out:]
        prods = [lax.dot_general(ab_refs[2 * t][...], ab_refs[2 * t + 1][...], dims, preferred_element_type=F32) for t in range(n_terms)]
        if epilogue is None:
            prods = [functools.reduce(lambda p, q: p + q, prods)]

        def finish(vals):
            outs = epilogue(vals, [e[...] for e in ex_refs]) if epilogue is not None else vals
            for o_ref, val in zip(o_refs, outs):
                o_ref[...] = val.astype(o_ref.dtype)

        if not split:
            finish(prods)
            return
        first = functools.reduce(jnp.logical_and, [pl.program_id(ax) == 0 for ax in red_axes])
        last = functools.reduce(jnp.logical_and, [pl.program_id(ax) == grid[ax] - 1 for ax in red_axes])

        @pl.when(first)
        def _():
            for acc_ref, prod in zip(acc, prods):
                acc_ref[...] = prod

        @pl.when(jnp.logical_not(first))
        def _():
            for acc_ref, prod in zip(acc, prods):
                acc_ref[...] += prod

        @pl.when(last)
        def _():
            finish([acc_ref[...] for acc_ref in acc])

    operands = [order] * n_ord + [a, b] + [t for pair in more for t in pair] + list(extras)
    res = pl.pallas_call(
        body, name=name, grid=grid, out_specs=[o_spec] * n_out,
        in_specs=[pl.BlockSpec(memory_space=pl.ANY)] * n_ord + [a_spec, b_spec] * n_terms + [o_spec] * n_ex,
        out_shape=[jax.ShapeDtypeStruct(out_shape, d) for d in out_dtypes],
        scratch_shapes=[pltpu.VMEM((tm, tn), F32)] * (n_prod if split else 0), compiler_params=_params(sem),
    )(*operands)
    return res if epilogue is not None else res[0]


def _rowwise(fn, tiled, whole, outs, accs, *, name):
    n_rows = tiled[0].shape[0]
    row_bytes = sum(t.shape[1] * t.dtype.itemsize for t in tiled) + sum(c * jnp.dtype(d).itemsize for c, d in outs)
    tr = _rows(n_rows, row_bytes)
    n_t, n_w, n_o = len(tiled), len(whole), len(outs)

    def body(*refs):
        i = pl.program_id(0)
        t_refs, w_refs = refs[:n_t], refs[n_t:n_t + n_w]
        o_refs, a_refs = refs[n_t + n_w:n_t + n_w + n_o], refs[n_t + n_w + n_o:]
        o_vals, a_vals = fn(i, *[r[...] for r in t_refs], *[r[...] for r in w_refs])
        for r, v in zip(o_refs, o_vals):
            r[...] = v.astype(r.dtype)
        if a_refs:
            @pl.when(i == 0)
            def _():
                for r in a_refs:
                    r[...] = jnp.zeros_like(r)
            for r, v in zip(a_refs, a_vals):
                r[...] += v

    in_specs = [pl.BlockSpec((tr, t.shape[1]), lambda i: (i, 0)) for t in tiled]
    in_specs += [pl.BlockSpec(w.shape, lambda i, nd=w.ndim: (0,) * nd) for w in whole]
    out_specs = [pl.BlockSpec((tr, c), lambda i: (i, 0)) for c, _ in outs]
    out_specs += [pl.BlockSpec(s, lambda i, nd=len(s): (0,) * nd) for s, _ in accs]
    out_shape = [jax.ShapeDtypeStruct((n_rows, c), d) for c, d in outs]
    out_shape += [jax.ShapeDtypeStruct(s, d) for s, d in accs]
    res = pl.pallas_call(
        body, name=name, grid=(n_rows // tr,), in_specs=in_specs, out_specs=out_specs, out_shape=out_shape,
        compiler_params=_params(("arbitrary",) if accs else ("parallel",)),
    )(*tiled, *whole)
    return res


def _rms_stats(x):
    r = lax.rsqrt(jnp.mean(x * x, axis=-1, keepdims=True) + EPS)
    return x * r, r


def rms_fwd(x, g, out_dtype, name):
    def fn(i, x, g):
        xhat, _ = _rms_stats(x)
        return [xhat * g], []
    return _rowwise(fn, [x], [g], [(x.shape[1], out_dtype)], [], name=name)[0]


def postnorm_res(x, y, g, s, name):
    def fn(i, x, y, g):
        yhat, _ = _rms_stats(y)
        return [x + s * (yhat * g)], []
    return _rowwise(fn, [x, y], [g], [(x.shape[1], F32)], [], name=name)[0]


def rms_bwd(xin, g, douts, scale, add, out_dtype, name):
    n_d = len(douts)

    def fn(i, x, *rest):
        ds, rest = rest[:n_d], rest[n_d:]
        ad = rest[0] if add is not None else None
        g = rest[-1]
        xhat, r = _rms_stats(x)
        d = ds[0].astype(F32)
        for e in ds[1:]:
            d = d + e.astype(F32)
        if scale != 1.0:
            d = d * scale
        dg = _rsum8(d * xhat)
        dxhat = d * g
        dx = r * (dxhat - xhat * jnp.mean(dxhat * xhat, axis=-1, keepdims=True))
        if ad is not None:
            dx = dx + ad
        return [dx], [dg]

    tiled = [xin, *douts] + ([add] if add is not None else [])
    dx, dg = _rowwise(fn, tiled, [g], [(xin.shape[1], out_dtype)], [((8, xin.shape[1]), F32)], name=name)
    return dx, dg


def _silu_parts(g):
    sg = 1.0 / (1.0 + jnp.exp(-g))
    return g * sg, sg


def _swiglu_tiles(products, saved):
    u, = products
    return [u, _silu_parts(saved[0].astype(F32))[0] * u]


def _swiglu_bwd_tiles(products, saved):
    da, = products
    g, u = saved[0].astype(F32), saved[1].astype(F32)
    s, sg = _silu_parts(g)
    return [da * u * (sg * (1.0 + g * (1.0 - sg))), da * s]


def scale_cols(y, s, name):
    def fn(i, y, s):
        return [y * s], []
    return _rowwise(fn, [y], [s], [(y.shape[1], F32)], [], name=name)[0]


def scale_cols_bwd(dm, y, s, name):
    def fn(i, dm, y, s):
        return [dm * s], [_rsum8(dm * y)]
    return _rowwise(fn, [dm, y], [s], [(y.shape[1], BF16)], [((8, y.shape[1]), F32)], name=name)


def loss_and_grad(y, target, name):
    n_feat = y.shape[1]

    def fn(i, y, t):
        e = y - t
        return [e * (1.0 / n_feat)], [_rsum8(e * e)]
    dy, part = _rowwise(fn, [y, target], [], [(n_feat, F32)], [((8, n_feat), F32)], name=name)
    return part, dy


def cast_into_slot(w3, layer, chip, name):
    _, n_rows, n_cols = w3.shape
    tr = _rows(n_rows, n_cols * 6)

    def body(chip_ref, w_ref, o_ref):
        o_ref[...] = w_ref[...].astype(BF16)

    grid_spec = pltpu.PrefetchScalarGridSpec(
        num_scalar_prefetch=1, grid=(n_rows // tr,),
        in_specs=[pl.BlockSpec((None, tr, n_cols), lambda i, chip_ref: (layer, i, 0))],
        out_specs=pl.BlockSpec((None, tr, n_cols), lambda i, chip_ref: (chip_ref[0], i, 0)))
    return pl.pallas_call(
        body, name=name, grid_spec=grid_spec, out_shape=jax.ShapeDtypeStruct((N_CHIPS, n_rows, n_cols), BF16),
        compiler_params=_params(("parallel",)),
    )(chip, w3)


def _adam_update(w, g, m, v):
    c1 = 1.0 / (1.0 - ADAM_B1 ** ADAM_STEP)
    c2 = 1.0 / (1.0 - ADAM_B2 ** ADAM_STEP)
    m = ADAM_B1 * m + (1.0 - ADAM_B1) * g
    v = ADAM_B2 * v + (1.0 - ADAM_B2) * (g * g)
    return -ADAM_LR * ((m * c1) / (jnp.sqrt(v * c2) + ADAM_EPS) + ADAM_WD * w), m, v


def adamw(w, g, m, v, name):
    n_l, n_rows, n_cols = w.shape
    tr = _rows(n_rows, n_cols * 4 * 7)

    def body(w_ref, g_ref, m_ref, v_ref, d_ref, mo_ref, vo_ref):
        d_ref[...], mo_ref[...], vo_ref[...] = _adam_update(w_ref[...], g_ref[...], m_ref[...], v_ref[...])

    spec = pl.BlockSpec((None, tr, n_cols), lambda l, i: (l, i, 0))
    shp = jax.ShapeDtypeStruct(w.shape, F32)
    return pl.pallas_call(
        body, name=name, grid=(n_l, n_rows // tr), in_specs=[spec] * 4, out_specs=[spec] * 3, out_shape=[shp] * 3,
        compiler_params=_params(("parallel", "parallel")),
    )(w, g, m, v)


def adamw_from_halves(w, own, recv, m, v, core, order, name):
    n_l, n_rows, n_cols = w.shape
    half = n_rows // 2
    tr = _rows(half, n_cols * 4 * 9)
    per = half // tr

    def body(core_ref, *refs):
        w_ref, m_ref, v_ref, order_ref = refs[:4]
        own_refs, recv_refs = refs[4:4 + n_l], refs[4 + n_l:4 + 2 * n_l]
        g_ref, d_ref, mo_ref, vo_ref = refs[4 + 2 * n_l:]
        l, h = pl.program_id(0), pl.program_id(1)
        mine = h == core_ref[0]
        g = jnp.where(mine, own_refs[0][...], recv_refs[0][...])
        for k in range(1, n_l):
            g = jnp.where(l == k, jnp.where(mine, own_refs[k][...], recv_refs[k][...]), g)
        g = g + order_ref[...]
        g_ref[...] = g
        d_ref[...], mo_ref[...], vo_ref[...] = _adam_update(w_ref[...], g, m_ref[...], v_ref[...])

    full = pl.BlockSpec((None, tr, n_cols), lambda l, h, i, core_ref: (l, h * per + i, 0))

    def piece(layer, is_own):
        def index(l, h, i, core_ref):
            used = (l == layer) & ((h == core_ref[0]) == is_own)
            return (jnp.where(used, i, 0), 0)
        return pl.BlockSpec((tr, n_cols), index)

    grid_spec = pltpu.PrefetchScalarGridSpec(
        num_scalar_prefetch=1, grid=(n_l, 2, per),
        in_specs=[full] * 3 + [pl.BlockSpec((1, 1), lambda l, h, i, core_ref: (0, 0))]
        + [piece(k, True) for k in range(n_l)] + [piece(k, False) for k in range(n_l)],
        out_specs=[full] * 4)
    return pl.pallas_call(
        body, name=name, grid_spec=grid_spec, out_shape=[jax.ShapeDtypeStruct(w.shape, F32)] * 4,
        compiler_params=_params(("parallel", "parallel", "parallel")),
    )(core, w, m, v, order, *own, *recv)


def rope_tables(n_tok):
    inv = ROPE_THETA ** (-jnp.arange(ROPE_HALF, dtype=F32) * 2.0 / (2 * ROPE_HALF))
    ang = jnp.arange(n_tok, dtype=F32)[:, None] * inv[None, :]
    cos, sin = jnp.cos(ang), jnp.sin(ang)
    rest = HEAD_DIM - 2 * ROPE_HALF
    one, zero, z8 = jnp.ones((n_tok, rest), F32), jnp.zeros((n_tok, rest), F32), jnp.zeros((n_tok, ROPE_HALF), F32)
    c = jnp.concatenate([cos, cos, one], axis=1)
    s1 = jnp.concatenate([-sin, z8, zero], axis=1)
    s2 = jnp.concatenate([z8, sin, zero], axis=1)
    two = lambda t: jnp.concatenate([t, t], axis=1)
    return two(c), two(s1), two(s2)


def rope_apply(x, tabs, n_col_blocks, inverse, out_dtype, name):
    n_tok = x.shape[0]
    tr = _rows(n_tok, LANES * 4 * 6)

    def body(x_ref, c_ref, s1_ref, s2_ref, o_ref):
        x = x_ref[...].astype(F32)
        if inverse:
            out = x * c_ref[...] + pltpu.roll(x * s1_ref[...], ROPE_HALF, 1) + pltpu.roll(x * s2_ref[...], LANES - ROPE_HALF, 1)
        else:
            out = x * c_ref[...] + pltpu.roll(x, LANES - ROPE_HALF, 1) * s1_ref[...] + pltpu.roll(x, ROPE_HALF, 1) * s2_ref[...]
        o_ref[...] = out.astype(o_ref.dtype)

    tab_spec = pl.BlockSpec((tr, LANES), lambda i, c: (i, 0))
    blk = pl.BlockSpec((tr, LANES), lambda i, c: (i, c))
    return pl.pallas_call(
        body, name=name, grid=(n_tok // tr, n_col_blocks), in_specs=[blk, tab_spec, tab_spec, tab_spec], out_specs=blk,
        out_shape=jax.ShapeDtypeStruct((n_tok, n_col_blocks * LANES), out_dtype), compiler_params=_params(("parallel", "parallel")),
    )(x, *tabs)


def _swa_probs(q, k, sink, n):
    rows = Q_PER_KV * BLOCK
    s = lax.dot_general(q, k, (((1,), (1,)), ((), ())), preferred_element_type=F32) * (HEAD_DIM ** -0.5)
    qi = lax.broadcasted_iota(jnp.int32, (rows, 2 * BLOCK), 0) & (BLOCK - 1)
    kj = lax.broadcasted_iota(jnp.int32, (rows, 2 * BLOCK), 1)
    rel = qi + BLOCK - kj
    valid = (rel >= 0) & (rel < BLOCK) & ((n > 0) | (kj >= BLOCK))
    s = jnp.where(valid, s, -1e30)
    m = jnp.maximum(jnp.max(s, axis=-1, keepdims=True), sink)
    e = jnp.exp(s - m)
    es = jnp.exp(sink - m)
    inv = 1.0 / (jnp.sum(e, axis=-1, keepdims=True) + es)
    return e * inv, es * inv


def _swa_specs(n_blocks):
    q_spec = pl.BlockSpec((Q_PER_KV, BLOCK, HEAD_DIM), lambda h, n: (h, n, 0))
    prev = pl.BlockSpec((None, BLOCK, HEAD_DIM), lambda h, n: (h, jnp.maximum(n - 1, 0), 0))
    cur = pl.BlockSpec((None, BLOCK, HEAD_DIM), lambda h, n: (h, n, 0))
    sink = pl.BlockSpec((None, Q_PER_KV * BLOCK, 1), lambda h, n: (h, 0, 0))
    return q_spec, prev, cur, sink


def swa_fwd(q, k, v, sink_rows, name):
    n_tok = q.shape[1]
    q_spec, prev, cur, sink = _swa_specs(n_tok // BLOCK)

    def body(q_ref, kp_ref, kc_ref, vp_ref, vc_ref, s_ref, o_ref):
        n = pl.program_id(1)
        qq = q_ref[...].reshape(Q_PER_KV * BLOCK, HEAD_DIM)
        kk = jnp.concatenate([kp_ref[...], kc_ref[...]], axis=0)
        vv = jnp.concatenate([vp_ref[...], vc_ref[...]], axis=0)
        p, _ = _swa_probs(qq, kk, s_ref[...], n)
        o = jnp.dot(p.astype(BF16), vv, preferred_element_type=F32)
        o_ref[...] = o.reshape(Q_PER_KV, BLOCK, HEAD_DIM).astype(o_ref.dtype)

    return pl.pallas_call(
        body, name=name, grid=(N_KV_HEADS, n_tok // BLOCK), in_specs=[q_spec, prev, cur, prev, cur, sink], out_specs=q_spec,
        out_shape=jax.ShapeDtypeStruct(q.shape, BF16), compiler_params=_params(("parallel", "parallel")),
    )(q, k, k, v, v, sink_rows)


def swa_bwd(q, k, v, sink_rows, do, name):
    n_tok = q.shape[1]
    nb = n_tok // BLOCK
    q_spec, prev, cur, sink = _swa_specs(nb)
    rows = Q_PER_KV * BLOCK

    def body(q_ref, kp_ref, kc_ref, vp_ref, vc_ref, s_ref, do_ref, dq_ref, dkp_ref, dkc_ref, dvp_ref, dvc_ref, ds_ref):
        n = pl.program_id(1)
        qq = q_ref[...].reshape(rows, HEAD_DIM)
        dd = do_ref[...].reshape(rows, HEAD_DIM)
        kk = jnp.concatenate([kp_ref[...], kc_ref[...]], axis=0)
        vv = jnp.concatenate([vp_ref[...], vc_ref[...]], axis=0)
        p, ps = _swa_probs(qq, kk, s_ref[...], n)
        dp = lax.dot_general(dd, vv, (((1,), (1,)), ((), ())), preferred_element_type=F32)
        delta = jnp.sum(p * dp, axis=-1, keepdims=True)
        ds = (p * (dp - delta) * (HEAD_DIM ** -0.5)).astype(BF16)
        dq = jnp.dot(ds, kk, preferred_element_type=F32)
        dk = lax.dot_general(ds, qq, (((0,), (0,)), ((), ())), preferred_element_type=F32)
        dv = lax.dot_general(p.astype(BF16), dd, (((0,), (0,)), ((), ())), preferred_element_type=F32)
        dq_ref[...] = dq.reshape(Q_PER_KV, BLOCK, HEAD_DIM).astype(dq_ref.dtype)
        dkp_ref[...] = dk[:BLOCK]
        dkc_ref[...] = dk[BLOCK:]
        dvp_ref[...] = dv[:BLOCK]
        dvc_ref[...] = dv[BLOCK:]
        dsink = jnp.broadcast_to(-ps * delta, (rows, LANES)).reshape(Q_PER_KV, BLOCK, LANES)
        ds_ref[...] = jnp.sum(dsink, axis=1)

    part = pl.BlockSpec((None, None, BLOCK, HEAD_DIM), lambda h, n: (h, n, 0, 0))
    part_shape = jax.ShapeDtypeStruct((N_KV_HEADS, nb, BLOCK, HEAD_DIM), F32)
    return pl.pallas_call(
        body, name=name, grid=(N_KV_HEADS, nb), in_specs=[q_spec, prev, cur, prev, cur, sink, q_spec],
        out_specs=[q_spec, part, part, part, part, pl.BlockSpec((None, None, Q_PER_KV, LANES), lambda h, n: (h, n, 0, 0))],
        out_shape=[jax.ShapeDtypeStruct(q.shape, BF16), part_shape, part_shape, part_shape, part_shape,
                   jax.ShapeDtypeStruct((N_KV_HEADS, nb, Q_PER_KV, LANES), F32)],
        compiler_params=_params(("parallel", "parallel")),
    )(q, k, k, v, v, sink_rows, do)


def _to_heads(t, n_heads):
    return t.reshape(t.shape[0], n_heads, HEAD_DIM).transpose(1, 0, 2)


def _from_heads(t):
    return t.transpose(1, 0, 2).reshape(t.shape[1], -1)


def _fold_kv_grad(prev, cur):
    shifted = jnp.concatenate([prev[:, 1:], jnp.zeros_like(prev[:, :1])], axis=1)
    tot = (cur + shifted).reshape(N_KV_HEADS, -1, HEAD_DIM)
    return _from_heads(tot)


def _x_probs(qh, kh):
    s = lax.dot_general(qh, kh, (((1,), (1,)), ((), ())), preferred_element_type=F32) * (X_HEAD_DIM ** -0.5)
    e = jnp.exp(s - jnp.max(s, axis=-1, keepdims=True))
    return e * (1.0 / jnp.sum(e, axis=-1, keepdims=True))


def xattn_fwd(q, k, v, name):
    n_tok, width = q.shape
    n_mem = k.shape[0]
    tq = _pick(n_tok, 512, 16)

    def body(q_ref, k_ref, v_ref, o_ref):
        for h in range(X_HEADS):
            cols = slice(h * X_HEAD_DIM, (h + 1) * X_HEAD_DIM)
            p = _x_probs(q_ref[:, cols], k_ref[:, cols])
            o_ref[:, cols] = jnp.dot(p.astype(BF16), v_ref[:, cols], preferred_element_type=F32).astype(o_ref.dtype)

    row = pl.BlockSpec((tq, width), lambda i: (i, 0))
    mem = pl.BlockSpec((n_mem, width), lambda i: (0, 0))
    return pl.pallas_call(
        body, name=name, grid=(n_tok // tq,), in_specs=[row, mem, mem], out_specs=row,
        out_shape=jax.ShapeDtypeStruct(q.shape, BF16), compiler_params=_params(("parallel",)),
    )(q, k, v)


def xattn_bwd(q, k, v, do, name):
    n_tok, width = q.shape
    n_mem = k.shape[0]
    tq = _pick(n_tok, 512, 16)

    def body(q_ref, k_ref, v_ref, do_ref, dq_ref, dk_ref, dv_ref):
        @pl.when(pl.program_id(0) == 0)
        def _():
            dk_ref[...] = jnp.zeros_like(dk_ref)
            dv_ref[...] = jnp.zeros_like(dv_ref)

        for h in range(X_HEADS):
            cols = slice(h * X_HEAD_DIM, (h + 1) * X_HEAD_DIM)
            qh, kh, vh, dh = q_ref[:, cols], k_ref[:, cols], v_ref[:, cols], do_ref[:, cols]
            p = _x_probs(qh, kh)
            dp = lax.dot_general(dh, vh, (((1,), (1,)), ((), ())), preferred_element_type=F32)
            delta = jnp.sum(p * dp, axis=-1, keepdims=True)
            ds = (p * (dp - delta) * (X_HEAD_DIM ** -0.5)).astype(BF16)
            dq_ref[:, cols] = jnp.dot(ds, kh, preferred_element_type=F32).astype(dq_ref.dtype)
            dk_ref[:, cols] += lax.dot_general(ds, qh, (((0,), (0,)), ((), ())), preferred_element_type=F32)
            dv_ref[:, cols] += lax.dot_general(p.astype(BF16), dh, (((0,), (0,)), ((), ())), preferred_element_type=F32)

    row = pl.BlockSpec((tq, width), lambda i: (i, 0))
    mem = pl.BlockSpec((n_mem, width), lambda i: (0, 0))
    return pl.pallas_call(
        body, name=name, grid=(n_tok // tq,), in_specs=[row, mem, mem, row], out_specs=[row, mem, mem],
        out_shape=[jax.ShapeDtypeStruct(q.shape, BF16), jax.ShapeDtypeStruct(k.shape, F32), jax.ShapeDtypeStruct(k.shape, F32)],
        compiler_params=_params(("arbitrary",)),
    )(q, k, v, do)


GELU_C = 0.7978845608028654
GELU_A = 0.044715


def _gelu_parts(x):
    x2 = x * x
    t = jnp.tanh(GELU_C * x * (1.0 + GELU_A * x2))
    y = 0.5 * x * (1.0 + t)
    dy = 0.5 * (1.0 + t) + 0.5 * x * (1.0 - t * t) * GELU_C * (1.0 + 3.0 * GELU_A * x2)
    return y, dy


def _sgu_norm(v, ln_g, ln_b):
    mu = jnp.mean(v, axis=-1, keepdims=True)
    vc = v - mu
    r = lax.rsqrt(jnp.mean(vc * vc, axis=-1, keepdims=True) + EPS)
    xhat = vc * r
    return xhat * ln_g + ln_b, xhat, r


def _causal_weights(w_ref):
    i = lax.broadcasted_iota(jnp.int32, (BLOCK, BLOCK), 0)
    j = lax.broadcasted_iota(jnp.int32, (BLOCK, BLOCK), 1)
    return [jnp.where(i >= j, w_ref[g], 0.0).astype(BF16) for g in range(SGU_GROUPS)]


def sgu_fwd(u_pre, v_pre, ln_g, ln_b, w_s, bias_rows, name):
    n_tok = u_pre.shape[0]
    tm = _pick(n_tok, 512, BLOCK)

    def body(u_ref, v_ref, g_ref, b_ref, w_ref, bb_ref, o_ref):
        vn, _, _ = _sgu_norm(_gelu_parts(v_ref[...])[0], g_ref[...], b_ref[...])
        vn = vn.astype(BF16)
        wc = _causal_weights(w_ref)
        for c in range(tm // BLOCK):
            rows = slice(c * BLOCK, (c + 1) * BLOCK)
            for g in range(SGU_GROUPS):
                cols = slice(g * LANES, (g + 1) * LANES)
                mixed = jnp.dot(wc[g], vn[rows, cols], preferred_element_type=F32) + bb_ref[g]
                u = _gelu_parts(u_ref[rows, cols])[0]
                o_ref[rows, cols] = (u * mixed).astype(o_ref.dtype)

    row = pl.BlockSpec((tm, SGU_WIDTH), lambda i: (i, 0))
    vec = pl.BlockSpec((1, SGU_WIDTH), lambda i: (0, 0))
    mat = pl.BlockSpec((SGU_GROUPS, BLOCK, LANES), lambda i: (0, 0, 0))
    return pl.pallas_call(
        body, name=name, grid=(n_tok // tm,), in_specs=[row, row, vec, vec, mat, mat], out_specs=row,
        out_shape=jax.ShapeDtypeStruct((n_tok, SGU_WIDTH), BF16), compiler_params=_params(("parallel",)),
    )(u_pre, v_pre, ln_g, ln_b, w_s, bias_rows)


def sgu_bwd(u_pre, v_pre, ln_g, ln_b, w_s, bias_rows, dgate, name):
    n_tok = u_pre.shape[0]
    tm = _pick(n_tok, 512, BLOCK)

    def body(u_ref, v_ref, g_ref, b_ref, w_ref, bb_ref, dg_ref, du_ref, dv_ref, dw_ref, db_ref, dlg_ref, dlb_ref, dvn_ref):
        @pl.when(pl.program_id(0) == 0)
        def _():
            dw_ref[...] = jnp.zeros_like(dw_ref)
            db_ref[...] = jnp.zeros_like(db_ref)
            dlg_ref[...] = jnp.zeros_like(dlg_ref)
            dlb_ref[...] = jnp.zeros_like(dlb_ref)

        gv, dgv = _gelu_parts(v_ref[...])
        vn, xhat, r = _sgu_norm(gv, g_ref[...], b_ref[...])
        vn = vn.astype(BF16)
        wc = _causal_weights(w_ref)
        for c in range(tm // BLOCK):
            rows = slice(c * BLOCK, (c + 1) * BLOCK)
            for g in range(SGU_GROUPS):
                cols = slice(g * LANES, (g + 1) * LANES)
                vt = vn[rows, cols]
                mixed = jnp.dot(wc[g], vt, preferred_element_type=F32) + bb_ref[g]
                u, du_dpre = _gelu_parts(u_ref[rows, cols])
                dgate_t = dg_ref[rows, cols].astype(F32)
                du_ref[rows, cols] = (dgate_t * mixed * du_dpre).astype(du_ref.dtype)
                dmix = dgate_t * u
                dmix_b = dmix.astype(BF16)
                db_ref[g] += dmix
                dw_ref[g] += lax.dot_general(dmix_b, vt, (((1,), (1,)), ((), ())), preferred_element_type=F32)
                dvn_ref[rows, cols] = lax.dot_general(wc[g], dmix_b, (((0,), (0,)), ((), ())), preferred_element_type=F32)
        dvn = dvn_ref[...]
        dlg_ref[...] += _rsum8(dvn * xhat)
        dlb_ref[...] += _rsum8(dvn)
        dxhat = dvn * g_ref[...]
        dgv_in = r * (dxhat - jnp.mean(dxhat, axis=-1, keepdims=True) - xhat * jnp.mean(dxhat * xhat, axis=-1, keepdims=True))
        dv_ref[...] = (dgv_in * dgv).astype(dv_ref.dtype)

    row = pl.BlockSpec((tm, SGU_WIDTH), lambda i: (i, 0))
    vec = pl.BlockSpec((1, SGU_WIDTH), lambda i: (0, 0))
    mat = pl.BlockSpec((SGU_GROUPS, BLOCK, LANES), lambda i: (0, 0, 0))
    part = pl.BlockSpec((8, SGU_WIDTH), lambda i: (0, 0))
    mat_shape = jax.ShapeDtypeStruct((SGU_GROUPS, BLOCK, LANES), F32)
    part_shape = jax.ShapeDtypeStruct((8, SGU_WIDTH), F32)
    act_shape = jax.ShapeDtypeStruct((n_tok, SGU_WIDTH), BF16)
    return pl.pallas_call(
        body, name=name, grid=(n_tok // tm,), in_specs=[row, row, vec, vec, mat, mat, row],
        out_specs=[row, row, mat, mat, part, part], out_shape=[act_shape, act_shape, mat_shape, mat_shape, part_shape, part_shape],
        scratch_shapes=[pltpu.VMEM((tm, SGU_WIDTH), F32)], compiler_params=_params(("arbitrary",)),
    )(u_pre, v_pre, ln_g, ln_b, w_s, bias_rows, dgate)


def _pool_tile(n_tok):
    return _pick(n_tok, 256, POOL_HALO)


def pool_fwd(h, name):
    n_tok, width = h.shape
    gw = width // len(POOL_WINDOWS)
    tm = _pool_tile(n_tok)
    per = tm // POOL_HALO

    def body(cur_ref, halo_ref, o_ref, buf_ref):
        i = pl.program_id(0)
        buf_ref[0:POOL_HALO, :] = jnp.where(i > 0, halo_ref[...], 0.0)
        buf_ref[POOL_HALO:, :] = cur_ref[...]
        tok = i * tm + lax.broadcasted_iota(jnp.int32, (tm, 1), 0)
        for g, w in enumerate(POOL_WINDOWS):
            cols = slice(g * gw, (g + 1) * gw)
            acc = buf_ref[POOL_HALO:, cols]
            for j in range(1, w):
                acc = acc + buf_ref[POOL_HALO - j:POOL_HALO - j + tm, cols]
            cnt = jnp.minimum(tok + 1, w).astype(F32)
            o_ref[:, cols] = (acc / cnt - cur_ref[:, cols]).astype(o_ref.dtype)

    return pl.pallas_call(
        body, name=name, grid=(n_tok // tm,),
        in_specs=[pl.BlockSpec((tm, width), lambda i: (i, 0)),
                  pl.BlockSpec((POOL_HALO, width), lambda i: (jnp.maximum(i * per - 1, 0), 0))],
        out_specs=pl.BlockSpec((tm, width), lambda i: (i, 0)), out_shape=jax.ShapeDtypeStruct(h.shape, BF16),
        scratch_shapes=[pltpu.VMEM((tm + POOL_HALO, width), F32)], compiler_params=_params(("parallel",)),
    )(h, h)


def pool_bwd(dp, name):
    n_tok, width = dp.shape
    gw = width // len(POOL_WINDOWS)
    tm = _pool_tile(n_tok)
    per = tm // POOL_HALO
    n_steps = n_tok // tm

    def body(cur_ref, halo_ref, o_ref, buf_ref):
        i = pl.program_id(0)
        tok = i * tm + lax.broadcasted_iota(jnp.int32, (tm, 1), 0)
        for g, w in enumerate(POOL_WINDOWS):
            cols = slice(g * gw, (g + 1) * gw)
            cnt = jnp.minimum(tok + 1, w).astype(F32)
            buf_ref[0:tm, cols] = cur_ref[:, cols] / cnt
            buf_ref[tm:, cols] = jnp.where(i < n_steps - 1, halo_ref[:, cols] / float(w), 0.0)
        for g, w in enumerate(POOL_WINDOWS):
            cols = slice(g * gw, (g + 1) * gw)
            acc = buf_ref[0:tm, cols]
            for j in range(1, w):
                acc = acc + buf_ref[j:j + tm, cols]
            o_ref[:, cols] = acc - cur_ref[:, cols]

    return pl.pallas_call(
        body, name=name, grid=(n_steps,),
        in_specs=[pl.BlockSpec((tm, width), lambda i: (i, 0)),
                  pl.BlockSpec((POOL_HALO, width), lambda i: (jnp.minimum((i + 1) * per, n_tok // POOL_HALO - 1), 0))],
        out_specs=pl.BlockSpec((tm, width), lambda i: (i, 0)), out_shape=jax.ShapeDtypeStruct(dp.shape, F32),
        scratch_shapes=[pltpu.VMEM((tm + POOL_HALO, width), F32)], compiler_params=_params(("parallel",)),
    )(dp, dp)


def _ffn_fwd(x, ga, gb, wg, wu, wd, tag):
    h = rms_fwd(x, ga, BF16, f"{tag}_prenorm")
    G = _mm(h, wg, batch="map", out_dtype=BF16, name=f"{tag}_gate")
    wu = wu(G) if callable(wu) else wu
    U, A = _mm(h, wu, batch="map", extras=[G], epilogue=_swiglu_tiles, out_dtypes=[BF16] * 2, name=f"{tag}_up")
    wd = wd(A) if callable(wd) else wd
    y = _mm(A, wd, batch="reduce", name=f"{tag}_down")
    return postnorm_res(x, y, gb, 0.5, f"{tag}_postnorm"), (x, h, G, U, A, y, wg, wu, wd)


def _ffn_bwd(res, ga, gb, dx2, tag, early):
    x, h, G, U, A, y, wg, wu, wd = res
    dy, dgb = rms_bwd(y, gb, [dx2], 0.5, None, BF16, f"{tag}_postnorm_bwd")
    dG, dU = _mm(dy, wd, tb=True, batch="map", extras=[G, U], epilogue=_swiglu_bwd_tiles, out_dtypes=[BF16] * 2, name=f"{tag}_down_dx")
    dwd = _mm(A, dy, ta=True, batch="map", name=f"{tag}_down_dw", out_dtype=BF16)
    dwg = _mm(h, dG, ta=True, batch="map", name=f"{tag}_gate_dw", out_dtype=BF16, order=early("wd", dwd))
    dwu = _mm(h, dU, ta=True, batch="map", name=f"{tag}_up_dw", out_dtype=BF16, order=early("wg", dwg))
    dh = _mm(dG, wg, tb=True, batch="reduce", more=[(dU, wu)], name=f"{tag}_gate_up_dx", order=early("wu", dwu))
    dx, dga = rms_bwd(x, ga, [dh], 1.0, dx2, F32, f"{tag}_prenorm_bwd")
    return dx, dga, dgb, dwg, dwu, dwd


def _sink_rows(sinks):
    return jnp.repeat(sinks.reshape(N_KV_HEADS, Q_PER_KV), BLOCK, axis=1)[..., None]


def _attn_sgu_fwd(x, g_pre, g_post, w_in, w_out, sinks, ln_g, ln_b, sgu_w, bias_rows, tabs, tag):
    h = rms_fwd(x, g_pre, BF16, f"{tag}_prenorm")
    z = _mm(h, w_in, name=f"{tag}_in")
    qk = rope_apply(z, tabs, QK_WIDTH // LANES, False, BF16, f"{tag}_rope")
    q = _to_heads(qk[:, :ATTN_WIDTH], N_Q_HEADS)
    k = _to_heads(qk[:, ATTN_WIDTH:], N_KV_HEADS)
    v = _to_heads(z[:, QK_WIDTH:QK_WIDTH + KV_WIDTH].astype(BF16), N_KV_HEADS)
    o = swa_fwd(q, k, v, _sink_rows(sinks), f"{tag}_swa")
    u_pre = z[:, QK_WIDTH + KV_WIDTH:QK_WIDTH + KV_WIDTH + SGU_WIDTH]
    v_pre = z[:, QK_WIDTH + KV_WIDTH + SGU_WIDTH:]
    gate = sgu_fwd(u_pre, v_pre, ln_g, ln_b, sgu_w, bias_rows, f"{tag}_sgu")
    cat = jnp.concatenate([_from_heads(o), gate], axis=1)
    m = _mm(cat, w_out, name=f"{tag}_out")
    return postnorm_res(x, m, g_post, 1.0, f"{tag}_postnorm"), (x, h, q, k, v, u_pre, v_pre, cat, m)


def _attn_sgu_bwd(res, g_pre, g_post, w_in, w_out, sinks, ln_g, ln_b, sgu_w, bias_rows, tabs, dx2, tag):
    x, h, q, k, v, u_pre, v_pre, cat, m = res
    dm, dg_post = rms_bwd(m, g_post, [dx2], 1.0, None, BF16, f"{tag}_postnorm_bwd")
    dcat = _mm(dm, w_out, tb=True, out_dtype=BF16, name=f"{tag}_out_dx")
    dw_out = _mm(cat, dm, ta=True, name=f"{tag}_out_dw", out_dtype=BF16)
    do = _to_heads(dcat[:, :ATTN_WIDTH], N_Q_HEADS)
    dq, dkp, dkc, dvp, dvc, dsink = swa_bwd(q, k, v, _sink_rows(sinks), do, f"{tag}_swa_bwd")
    d_sinks = jnp.sum(dsink[..., 0], axis=1).reshape(1, N_Q_HEADS)
    dqk_rot = jnp.concatenate([_from_heads(dq).astype(F32), _fold_kv_grad(dkp, dkc)], axis=1)
    dqk = rope_apply(dqk_rot, tabs, QK_WIDTH // LANES, True, BF16, f"{tag}_rope_bwd")
    dv = _fold_kv_grad(dvp, dvc).astype(BF16)
    du_pre, dv_pre, dw_s, dbias, dlg, dlb = sgu_bwd(u_pre, v_pre, ln_g, ln_b, sgu_w, bias_rows, dcat[:, ATTN_WIDTH:], f"{tag}_sgu_bwd")
    dz = jnp.concatenate([dqk, dv, du_pre, dv_pre], axis=1)
    dw_in = _mm(h, dz, ta=True, name=f"{tag}_in_dw", out_dtype=BF16)
    dh = _mm(dz, w_in, tb=True, name=f"{tag}_in_dx")
    dx, dg_pre = rms_bwd(x, g_pre, [dh], 1.0, dx2, F32, f"{tag}_prenorm_bwd")
    causal = jnp.tril(jnp.ones((BLOCK, BLOCK), F32))
    small = dict(attn_sinks=d_sinks, sgu_ln_g=jnp.sum(dlg, axis=0, keepdims=True), sgu_ln_b=jnp.sum(dlb, axis=0, keepdims=True),
                 sgu_w=(dw_s * causal[None])[None], sgu_b=jnp.sum(dbias, axis=-1)[None])
    return dx, dg_pre, dg_post, dw_in, dw_out, small


def _pool_mix_fwd(x, g_pre, g_post, pool_w, pool_scale, tag):
    hf = rms_fwd(x, g_pre, F32, f"{tag}_prenorm")
    pooled = pool_fwd(hf, f"{tag}_pool")
    n_g = len(POOL_WINDOWS)
    ypre = _mm(pooled, pool_w, batch="map", groups=n_g, a_cb=True, o_cb=True, name=f"{tag}_proj")
    m = scale_cols(ypre, pool_scale, f"{tag}_scale")
    return postnorm_res(x, m, g_post, 1.0, f"{tag}_postnorm"), (x, pooled, ypre, m)


def _pool_mix_bwd(res, g_pre, g_post, pool_w, pool_scale, dx2, tag):
    x, pooled, ypre, m = res
    n_g = len(POOL_WINDOWS)
    dm, dg_post = rms_bwd(m, g_post, [dx2], 1.0, None, F32, f"{tag}_postnorm_bwd")
    dypre, dscale = scale_cols_bwd(dm, ypre, pool_scale, f"{tag}_scale_bwd")
    dpooled = _mm(dypre, pool_w, tb=True, batch="map", groups=n_g, a_cb=True, o_cb=True, name=f"{tag}_proj_dx")
    dpw = _mm(pooled, dypre, ta=True, batch="map", groups=n_g, a_cb=True, b_cb=True, name=f"{tag}_proj_dw", out_dtype=BF16)
    dhf = pool_bwd(dpooled, f"{tag}_pool_bwd")
    dx, dg_pre = rms_bwd(x, g_pre, [dhf], 1.0, dx2, F32, f"{tag}_prenorm_bwd")
    return dx, dg_pre, dg_post, dpw, jnp.sum(dscale, axis=0, keepdims=True)


def _xattn_fwd(x, mem, g_pre, g_post, g_mem, wq, wk, wv, wo, tag):
    h = rms_fwd(x, g_pre, BF16, f"{tag}_prenorm")
    mem_n = rms_fwd(mem, g_mem, BF16, f"{tag}_memnorm")
    q = _mm(h, wq, out_dtype=BF16, name=f"{tag}_q")
    k = _mm(mem_n, wk, out_dtype=BF16, name=f"{tag}_k")
    v = _mm(mem_n, wv, out_dtype=BF16, name=f"{tag}_v")
    o = xattn_fwd(q, k, v, f"{tag}_core")
    r = _mm(o, wo, name=f"{tag}_o")
    return postnorm_res(x, r, g_post, 1.0, f"{tag}_postnorm"), (x, h, mem_n, q, k, v, o, r)


def _xattn_bwd(res, mem, g_pre, g_post, g_mem, wq, wk, wv, wo, dx2, tag):
    x, h, mem_n, q, k, v, o, r = res
    dr, dg_post = rms_bwd(r, g_post, [dx2], 1.0, None, BF16, f"{tag}_postnorm_bwd")
    do = _mm(dr, wo, tb=True, out_dtype=BF16, name=f"{tag}_o_dx")
    dwo = _mm(o, dr, ta=True, name=f"{tag}_o_dw", out_dtype=BF16)
    dq, dk, dv = xattn_bwd(q, k, v, do, f"{tag}_core_bwd")
    dk, dv = dk.astype(BF16), dv.astype(BF16)
    dwq = _mm(h, dq, ta=True, name=f"{tag}_q_dw", out_dtype=BF16)
    dwk = _mm(mem_n, dk, ta=True, name=f"{tag}_k_dw", out_dtype=BF16)
    dwv = _mm(mem_n, dv, ta=True, name=f"{tag}_v_dw", out_dtype=BF16)
    dh = _mm(dq, wq, tb=True, name=f"{tag}_q_dx")
    dmem1 = _mm(dk, wk, tb=True, name=f"{tag}_k_dx")
    dmem2 = _mm(dv, wv, tb=True, name=f"{tag}_v_dx")
    _, dg_mem = rms_bwd(mem, g_mem, [dmem1, dmem2], 1.0, None, BF16, f"{tag}_memnorm_bwd")
    dx, dg_pre = rms_bwd(x, g_pre, [dh], 1.0, dx2, F32, f"{tag}_prenorm_bwd")
    return dx, dg_pre, dg_post, dg_mem, dwq, dwk, dwv, dwo


def _rowsum8(part):
    return jnp.sum(part, axis=0, keepdims=True)


def device_step(x, mem, target, norms, mem_norm, small, fetch, emit, early):
    n_tok = x.shape[0]
    tabs = rope_tables(n_tok)
    bias_rows = jnp.broadcast_to(small["sgu_b"][:, :, None], (SGU_GROUPS, BLOCK, LANES))
    gn = lambda l, i: norms[l, i][None, :]
    gm = lambda l: mem_norm[l][None, :]
    mix0 = (small["attn_sinks"], small["sgu_ln_g"], small["sgu_ln_b"], small["sgu_w"], bias_rows, tabs)

    wts, saved = {}, []
    for l in range(2):
        wts["ffn1", l], tok = fetch(("ffn1", l), x)
        x, r1 = _ffn_fwd(x, gn(l, 0) + tok, gn(l, 1), *wts["ffn1", l], f"l{l}_ffn1")
        wts["mix", l], tok = fetch(("mix", l), x)
        if l == 0:
            x, r2 = _attn_sgu_fwd(x, gn(l, 2) + tok, gn(l, 3), *wts["mix", l], *mix0, f"l{l}_mix")
        else:
            x, r2 = _pool_mix_fwd(x, gn(l, 2) + tok, gn(l, 3), *wts["mix", l], small["pool_scale"], f"l{l}_mix")
        wts["xattn", l], tok = fetch(("xattn", l), x)
        x, r3 = _xattn_fwd(x, mem, gn(l, 4) + tok, gn(l, 5), gm(l), *wts["xattn", l], f"l{l}_xattn")
        wts["ffn2", l], tok = fetch(("ffn2", l), x)
        x, r4 = _ffn_fwd(x, gn(l, 6) + tok, gn(l, 7), *wts["ffn2", l], f"l{l}_ffn2")
        saved.append((r1, r2, r3, r4))

    loss_part, dx = loss_and_grad(x, target, "loss")

    g_norm_rows = [[None] * 8, [None] * 8]
    g_mem_rows = [None, None]
    g_small = {}
    tok = 0.0
    for l in (1, 0):
        r1, r2, r3, r4 = saved[l]
        dx, g_norm_rows[l][6], g_norm_rows[l][7], dwg, dwu, dwd = _ffn_bwd(
            r4, gn(l, 6), gn(l, 7) + tok, dx, f"l{l}_ffn2", lambda part, g, l=l: early(("ffn2", l), f"ffn2_{part}", g))
        tok = emit(("ffn2", l), dict(ffn2_wg=dwg, ffn2_wu=dwu, ffn2_wd=dwd), dx)
        dx, g_norm_rows[l][4], g_norm_rows[l][5], g_mem_rows[l], dwq, dwk, dwv, dwo = _xattn_bwd(
            r3, mem, gn(l, 4), gn(l, 5) + tok, gm(l), *wts["xattn", l], dx, f"l{l}_xattn")
        tok = emit(("xattn", l), dict(x_wq=dwq, x_wk=dwk, x_wv=dwv, x_wo=dwo), dx)
        if l == 0:
            dx, g_norm_rows[l][2], g_norm_rows[l][3], dw_in, dw_out, sm = _attn_sgu_bwd(
                r2, gn(l, 2), gn(l, 3) + tok, *wts["mix", l], *mix0, dx, f"l{l}_mix")
            tok = emit(("mix", l), dict(mix_w_in=dw_in, mix_w_out=dw_out), dx)
            g_small.update(sm)
        else:
            dx, g_norm_rows[l][2], g_norm_rows[l][3], dpw, dscale = _pool_mix_bwd(
                r2, gn(l, 2), gn(l, 3) + tok, *wts["mix", l], small["pool_scale"], dx, f"l{l}_mix")
            tok = emit(("mix", l), dict(pool_w=dpw), dx)
            g_small["pool_scale"] = dscale
        dx, g_norm_rows[l][0], g_norm_rows[l][1], dwg, dwu, dwd = _ffn_bwd(
            r1, gn(l, 0), gn(l, 1) + tok, dx, f"l{l}_ffn1", lambda part, g, l=l: early(("ffn1", l), f"ffn1_{part}", g))
        tok = emit(("ffn1", l), dict(ffn1_wg=dwg, ffn1_wu=dwu, ffn1_wd=dwd), dx)
    g_norms = jnp.stack([jnp.concatenate([_rowsum8(p) for p in g_norm_rows[l]], axis=0) for l in range(2)])
    g_mem_norm = jnp.concatenate([_rowsum8(p) for p in g_mem_rows], axis=0)
    return loss_part, dx, g_small, g_norms, g_mem_norm, tok


ANY = pl.BlockSpec(memory_space=pl.ANY)


def _place():
    x, y, c = lax.axis_index("x"), lax.axis_index("y"), lax.axis_index("c")
    other_chips = [(1 - x, y), (x, 1 - y), (1 - x, 1 - y)]
    return x, y, c, other_chips


def _half_rows(core, n_rows):
    half = n_rows // 2
    return pl.ds(pl.multiple_of(core * half, 16), half)


def _remote(src, dst, send_sem, recv_sem, device):
    return pltpu.make_async_remote_copy(src_ref=src, dst_ref=dst, send_sem=send_sem, recv_sem=recv_sem,
                                        device_id=device, device_id_type=MESH)


HBM = pl.BlockSpec(memory_space=pltpu.HBM)
SEM = pl.BlockSpec(memory_space=pltpu.SEMAPHORE)
DATAFLOW = pltpu.SideEffectType.DATAFLOW_SIDE_EFFECTING


def _chip_copies(bufs, send_sems, recv_sems):
    x, y, c, chips = _place()
    me = 2 * x + y
    sends, arrivals = [], []
    for i, buf in enumerate(bufs):
        rows = _half_rows(c, buf.shape[1])
        for r, (px, py) in enumerate(chips):
            mine, theirs = buf.at[me, rows], buf.at[2 * px + py, rows]
            sends.append(_remote(mine, mine, send_sems.at[3 * i + r], recv_sems.at[3 * i + r], (px, py, c)))
            arrivals.append(_remote(theirs, theirs, send_sems.at[3 * i + r], recv_sems.at[3 * i + r], (px, py, c)))
    return sends, arrivals


def copies_start(arrays, fresh, copies, n_sems, after, name):
    operands = [pltpu.with_memory_space_constraint(a, pltpu.HBM) for a in arrays]
    operands += [pltpu.with_memory_space_constraint(lax.empty(f.shape, f.dtype), pltpu.HBM) for f in fresh]
    n = len(operands)

    def body(*refs):
        send_sems, recv_sems = refs[n + 1], refs[n + 2]
        bufs, token = refs[n + 3:2 * n + 3], refs[2 * n + 3]
        for cp in copies(bufs, send_sems, recv_sems)[0]:
            cp.start()
        token[...] = jnp.zeros_like(token)

    res = pl.pallas_call(
        body, name=name, in_specs=[HBM] * n + [ANY],
        out_specs=[SEM, SEM] + [HBM] * n + [pl.BlockSpec(memory_space=pltpu.VMEM)],
        out_shape=[pltpu.SemaphoreType.DMA((n_sems,)), pltpu.SemaphoreType.DMA((n_sems,))]
        + [pltpu.HBM(o.shape, o.dtype) for o in operands] + [jax.ShapeDtypeStruct((8, LANES), F32)],
        input_output_aliases={i: 2 + i for i in range(n)}, compiler_params=pltpu.CompilerParams(has_side_effects=DATAFLOW),
    )(*operands, after)
    return res[0], res[1], list(res[2:2 + n]), res[2 + n]


def copies_wait(send_sems, recv_sems, bufs, copies, after, name):
    n = len(bufs)

    def body(*refs):
        for cp in copies(refs[:n], refs[n], refs[n + 1])[1]:
            cp.wait_send()
            cp.wait_recv()

    return list(pl.pallas_call(
        body, name=name, in_specs=[HBM] * n + [SEM, SEM] + [ANY] * len(after), out_specs=[HBM] * n,
        out_shape=[pltpu.HBM(b.shape, b.dtype) for b in bufs], input_output_aliases={i: i for i in range(n)},
        compiler_params=pltpu.CompilerParams(has_side_effects=DATAFLOW),
    )(*bufs, send_sems, recv_sems, *after))


def _to_sibling_copies(n_src, src_rows):
    def copies(bufs, send_sems, recv_sems):
        x, y, c, _ = _place()
        sends, arrivals = [], []
        for i in range(n_src):
            src, land = bufs[i], bufs[n_src + i]
            src = src.at[:, src_rows(c, src)] if src_rows is not None else src
            sends.append(_remote(src, land, send_sems.at[i], recv_sems.at[i], (x, y, 1 - c)))
            arrivals.append(_remote(land, land, send_sems.at[i], recv_sems.at[i], (x, y, 1 - c)))
        return sends, arrivals
    return copies


def forward_to_sibling(bufs, name):
    n = len(bufs)

    def body(*refs):
        out = refs[n:2 * n]
        send_sems, recv_sems = refs[2 * n:]
        x, y, c, chips = _place()
        copies = []
        for i in range(n):
            for r, (px, py) in enumerate(chips):
                mine = out[i].at[2 * px + py, _half_rows(c, out[i].shape[1])]
                copies.append(_remote(mine, mine, send_sems.at[3 * i + r], recv_sems.at[3 * i + r], (x, y, 1 - c)))
                copies[-1].start()
        for i in range(n):
            for r, (px, py) in enumerate(chips):
                theirs = out[i].at[2 * px + py, _half_rows(1 - c, out[i].shape[1])]
                _remote(theirs, theirs, send_sems.at[3 * i + r], recv_sems.at[3 * i + r], (x, y, 1 - c)).wait_recv()
        for cp in copies:
            cp.wait_send()

    return pl.pallas_call(
        body, name=name, in_specs=[ANY] * n, out_specs=[ANY] * n, input_output_aliases={i: i for i in range(n)},
        out_shape=[jax.ShapeDtypeStruct(b.shape, b.dtype) for b in bufs],
        scratch_shapes=[pltpu.SemaphoreType.DMA((3 * n,)), pltpu.SemaphoreType.DMA((3 * n,))],
    )(*bufs)


def add_own_half(g, p, core, name):
    n_j, n_rows, n_cols = g.shape
    half = n_rows // 2
    tr = _rows(half, n_cols * 10)

    def body(c_ref, g_ref, p_ref, o_ref):
        o_ref[...] = (g_ref[...] + p_ref[...]).astype(o_ref.dtype)

    grid_spec = pltpu.PrefetchScalarGridSpec(
        num_scalar_prefetch=1, grid=(n_j, half // tr),
        in_specs=[pl.BlockSpec((None, None, tr, n_cols), lambda j, i, c_ref: (j, c_ref[0], i, 0)),
                  pl.BlockSpec((None, tr, n_cols), lambda j, i, c_ref: (j, i, 0))],
        out_specs=pl.BlockSpec((None, tr, n_cols), lambda j, i, c_ref: (j, i, 0)))
    return pl.pallas_call(
        body, name=name, grid_spec=grid_spec, out_shape=jax.ShapeDtypeStruct((n_j, half, n_cols), BF16),
        compiler_params=_params(("parallel", "parallel")),
    )(core, g.reshape(n_j, 2, half, n_cols), p)


def _scatter_copies(n_parts):
    def copies(bufs, send_sems, recv_sems):
        x, y, c, chips = _place()
        me = 2 * x + y
        sends, arrivals = [], []
        for i in range(n_parts):
            part, land = bufs[i], bufs[n_parts + i]
            for r, (px, py) in enumerate(chips):
                theirs = land.at[2 * px + py]
                sends.append(_remote(part.at[2 * px + py], land.at[me], send_sems.at[3 * i + r], recv_sems.at[3 * i + r], (px, py, c)))
                arrivals.append(_remote(theirs, theirs, send_sems.at[3 * i + r], recv_sems.at[3 * i + r], (px, py, c)))
        return sends, arrivals
    return copies


def sum_over_chips(own, got, chip, name):
    n_s, n_rows, n_cols = got.shape
    tr = _rows(n_rows, n_cols * (got.dtype.itemsize * (n_s + 1) + 4))

    def body(chip_ref, own_ref, *refs):
        got_refs, o_ref = refs[:n_s], refs[n_s]
        me = chip_ref[0]
        acc = jnp.where(me == 0, own_ref[...], got_refs[0][...]).astype(F32)
        for k in range(1, n_s):
            acc = acc + jnp.where(me == k, own_ref[...], got_refs[k][...]).astype(F32)
        o_ref[...] = acc

    def slot(k):
        return pl.BlockSpec((None, tr, n_cols), lambda i, chip_ref: (jnp.where(chip_ref[0] == k, (k + 1) % n_s, k), i, 0))

    grid_spec = pltpu.PrefetchScalarGridSpec(
        num_scalar_prefetch=1, grid=(n_rows // tr,),
        in_specs=[pl.BlockSpec((None, tr, n_cols), lambda i, chip_ref: (chip_ref[0], i, 0))] + [slot(k) for k in range(n_s)],
        out_specs=pl.BlockSpec((tr, n_cols), lambda i, chip_ref: (i, 0)))
    return pl.pallas_call(
        body, name=name, grid_spec=grid_spec, out_shape=jax.ShapeDtypeStruct((n_rows, n_cols), F32),
        compiler_params=_params(("parallel",)),
    )(chip, own, *([got] * n_s))


def sum_slots(q, name):
    n_s, n_rows, n_cols = q.shape
    tr = _rows(n_rows, n_cols * (q.dtype.itemsize * n_s + 4))

    def body(q_ref, o_ref):
        acc = q_ref[0].astype(F32)
        for s in range(1, n_s):
            acc = acc + q_ref[s].astype(F32)
        o_ref[...] = acc

    return pl.pallas_call(
        body, name=name, grid=(n_rows // tr,), in_specs=[pl.BlockSpec((n_s, tr, n_cols), lambda i: (0, i, 0))],
        out_specs=pl.BlockSpec((tr, n_cols), lambda i: (i, 0)), out_shape=jax.ShapeDtypeStruct((n_rows, n_cols), F32),
        compiler_params=_params(("parallel",)),
    )(q)


def gather_all_devices(s, name):
    def body(s_ref, o_ref, send_sems, recv_sems, local_sem):
        x, y, c, _ = _place()
        me = 4 * x + 2 * y + c
        local = pltpu.make_async_copy(s_ref, o_ref.at[me], local_sem)
        local.start()
        sends = []
        for f in range(1, N_DEV):
            px, py, pc = x ^ (f >> 2), y ^ ((f >> 1) & 1), c ^ (f & 1)
            sends.append(_remote(s_ref, o_ref.at[me], send_sems.at[f - 1], recv_sems.at[f - 1], (px, py, pc)))
            sends[-1].start()
        for f in range(1, N_DEV):
            px, py, pc = x ^ (f >> 2), y ^ ((f >> 1) & 1), c ^ (f & 1)
            landed = o_ref.at[4 * px + 2 * py + pc]
            _remote(landed, landed, send_sems.at[f - 1], recv_sems.at[f - 1], (px, py, pc)).wait_recv()
        for cp in sends:
            cp.wait_send()
        local.wait()

    return pl.pallas_call(
        body, name=name, in_specs=[ANY], out_specs=ANY, out_shape=jax.ShapeDtypeStruct((N_DEV,) + s.shape, s.dtype),
        scratch_shapes=[pltpu.SemaphoreType.DMA((N_DEV - 1,)), pltpu.SemaphoreType.DMA((N_DEV - 1,)), pltpu.SemaphoreType.DMA],
    )(s)


FFN_NAMES = ("ffn1_wg", "ffn1_wu", "ffn1_wd", "ffn2_wg", "ffn2_wu", "ffn2_wd")
XATTN_NAMES = ("x_wq", "x_wk", "x_wv", "x_wo")
BIG_NAMES = FFN_NAMES + XATTN_NAMES + ("mix_w_in", "mix_w_out", "pool_w")
SMALL_NAMES = ("norms", "mem_norm", "attn_sinks", "sgu_ln_g", "sgu_ln_b", "sgu_w", "sgu_b", "pool_scale")
WEIGHT_ORDER = ("norms", "mem_norm") + BIG_NAMES[:-1] + ("attn_sinks", "sgu_ln_g", "sgu_ln_b", "sgu_w", "sgu_b", "pool_w", "pool_scale")
COLUMN_CUT = ("x_wo", "mix_w_in")
N_POOL = len(POOL_WINDOWS)
BLOCK_ORDER = (("ffn1", 0), ("mix", 0), ("xattn", 0), ("ffn2", 0), ("ffn1", 1), ("mix", 1), ("xattn", 1), ("ffn2", 1))
PREFETCH_AT = ((1, 2), (3,), (), (4,), (5, 6), (7,), (), ())


def block_weight_names(kind, layer):
    if kind == "mix":
        return ("mix_w_in", "mix_w_out") if layer == 0 else ("pool_w",)
    return XATTN_NAMES if kind == "xattn" else tuple(f"{kind}_{part}" for part in ("wg", "wu", "wd"))


def _to_matmul_layout(name, g):
    n_j, n_rows, n_cols = g.shape
    if name in COLUMN_CUT:
        return g.transpose(1, 0, 2).reshape(n_rows, n_j * n_cols)
    if name == "pool_w":
        return g.reshape(n_j, N_POOL, n_rows // N_POOL, n_cols).transpose(1, 0, 2, 3).reshape(N_POOL, n_j * n_rows // N_POOL, n_cols)
    if name in ("x_wq", "x_wk", "x_wv", "mix_w_out"):
        return g.reshape(n_j * n_rows, n_cols)
    return g


def _from_matmul_layout(name, d):
    if name in COLUMN_CUT:
        n_rows, wide = d.shape
        return d.reshape(n_rows, N_CHIPS, wide // N_CHIPS).transpose(1, 0, 2)
    if name == "pool_w":
        n_g, n_in, n_cols = d.shape
        return d.reshape(n_g, N_CHIPS, n_in // N_CHIPS, n_cols).transpose(1, 0, 2, 3).reshape(N_CHIPS, n_g * n_in // N_CHIPS, n_cols)
    if name in ("x_wq", "x_wk", "x_wv", "mix_w_out"):
        return d.reshape(N_CHIPS, d.shape[0] // N_CHIPS, d.shape[1])
    return d


def _as3(w):
    return w.reshape(w.shape[0], -1, w.shape[-1])


def _pack(arrays, row_multiple):
    flat = jnp.concatenate([a.reshape(-1) for a in arrays])
    per = LANES * row_multiple
    total = -(-flat.shape[0] // per) * per
    return jnp.pad(flat, (0, total - flat.shape[0])).reshape(total // LANES, LANES)


def _unpack(packed, like):
    flat, out, at = packed.reshape(-1), [], 0
    for a in like:
        out.append(flat[at:at + a.size].reshape(a.shape))
        at += a.size
    return out


def kernel(x, mem, norms, mem_norm, ffn1_wg, ffn1_wu, ffn1_wd, ffn2_wg, ffn2_wu, ffn2_wd, x_wq, x_wk, x_wv, x_wo, mix_w_in, mix_w_out, attn_sinks, sgu_ln_g, sgu_ln_b, sgu_w, sgu_b, pool_w, pool_scale, loss_target, m_norms, m_mem_norm, m_ffn1_wg, m_ffn1_wu, m_ffn1_wd, m_ffn2_wg, m_ffn2_wu, m_ffn2_wd, m_x_wq, m_x_wk, m_x_wv, m_x_wo, m_mix_w_in, m_mix_w_out, m_attn_sinks, m_sgu_ln_g, m_sgu_ln_b, m_sgu_w, m_sgu_b, m_pool_w, m_pool_scale, v_norms, v_mem_norm, v_ffn1_wg, v_ffn1_wu, v_ffn1_wd, v_ffn2_wg, v_ffn2_wu, v_ffn2_wd, v_x_wq, v_x_wk, v_x_wv, v_x_wo, v_mix_w_in, v_mix_w_out, v_attn_sinks, v_sgu_ln_g, v_sgu_ln_b, v_sgu_w, v_sgu_b, v_pool_w, v_pool_scale):
    given = dict(locals())
    w = {n: given[n] for n in WEIGHT_ORDER}
    mom = {n: given["m_" + n] for n in WEIGHT_ORDER}
    var = {n: given["v_" + n] for n in WEIGHT_ORDER}
    chip_id = 2 * lax.axis_index("x") + lax.axis_index("y")
    chip = chip_id.astype(jnp.int32).reshape(1)
    core = lax.axis_index("c").astype(jnp.int32).reshape(1)
    n_shard = norms.shape[-1]

    keys = [(name, l) for name in BIG_NAMES for l in range(_as3(w[name]).shape[0])]
    first_keys = [(name, 0) for name in block_weight_names(*BLOCK_ORDER[0])]
    cast = lambda name, l: cast_into_slot(_as3(w[name]), l, chip, f"cast_{name}")
    slot_of = {key: cast(*key) for key in first_keys}
    small_rows = jnp.concatenate([norms.reshape(-1, n_shard), pool_scale, jnp.zeros((15, n_shard), F32)], axis=0)
    small_slot = lax.dynamic_update_slice_in_dim(jnp.zeros((N_CHIPS,) + small_rows.shape, F32), small_rows[None], chip_id, axis=0)
    pending = {}
    gather_start = lambda slots, after, name: copies_start(slots, [], _chip_copies, 3 * len(slots), after, name)
    pending["0a"] = gather_start([slot_of[first_keys[0]], small_slot], chip, "gather_start_0a")
    pending["0b"] = gather_start([slot_of[first_keys[1]]], pending["0a"][2][0], "gather_start_0b")
    pending["0c"] = gather_start([slot_of[first_keys[2]]], pending["0b"][2][0], "gather_start_0c")
    for key in keys:
        if key not in slot_of:
            slot_of[key] = cast(*key)


    def block_slots(k):
        kind, layer = BLOCK_ORDER[k]
        return [slot_of[name, layer if kind != "mix" else 0] for name in block_weight_names(kind, layer)]

    def start(k, after):
        pending[k] = gather_start(block_slots(k), after, f"gather_start_{k}")

    def finish(k, after):
        send_sems, recv_sems, bufs, _ = pending[k]
        return forward_to_sibling(copies_wait(send_sems, recv_sems, bufs, _chip_copies, [after], f"gather_wait_{k}"), f"gather_forward_{k}")

    def fetch(key, after):
        k = BLOCK_ORDER.index(key)
        names = block_weight_names(*key)
        if k == 0:
            late = lambda part, name: lambda after: _to_matmul_layout(name, finish(part, after)[0])
            bufs = [first_gate]
            weights = (_to_matmul_layout(names[0], first_gate), late("0b", names[1]), late("0c", names[2]))
        else:
            bufs = finish(k, after)
            weights = tuple(_to_matmul_layout(name, b) for name, b in zip(names, bufs))
        token = 0.0
        for ahead in PREFETCH_AT[k]:
            start(ahead, bufs[0])
            token = token + pending[ahead][3][0, 0]
        return weights, token

    to_sibling, to_chips, halves, own_of, recv_of = [], [], [], {}, {}
    other_half = lambda c, src: _half_rows(1 - c, src.shape[1])

    def sibling_start(tag, names, full, after):
        lands = [jax.ShapeDtypeStruct((g.shape[0], g.shape[1] // 2, g.shape[2]), g.dtype) for g in full]
        send_sems, recv_sems, bufs, token = copies_start(full, lands, _to_sibling_copies(len(full), other_half), len(full), after,
                                                         f"grads_to_sibling_{tag}")
        to_sibling.append((tag, names, send_sems, recv_sems, bufs))
        return token

    def early(key, name, g):
        kind, layer = key
        return sibling_start(f"{name}{layer}", [(name, layer)], [_from_matmul_layout(name, g)], g)

    def chips_start(tag, after):
        names, chip_sums = [], []
        for part, part_names, send_sems, recv_sems, bufs in to_sibling:
            n = len(part_names)
            bufs = copies_wait(send_sems, recv_sems, bufs, _to_sibling_copies(n, other_half), after, f"grads_from_sibling_{part}")
            chip_sums += [add_own_half(g, p, core, f"chip_sum_{name}") for (name, _), g, p in zip(part_names, bufs[:n], bufs[n:])]
            names += part_names
        to_sibling.clear()
        n = len(names)
        send_sems, recv_sems, bufs, token = copies_start(chip_sums, chip_sums, _scatter_copies(n), 3 * n, after[0], f"grads_start_{tag}")
        to_chips.append((tag, names, send_sems, recv_sems, bufs))
        return token[0, 0]

    def chips_wait(after):
        tag, names, send_sems, recv_sems, bufs = to_chips.pop()
        return tag, names, copies_wait(send_sems, recv_sems, bufs, _scatter_copies(len(names)), after, f"grads_wait_{tag}")

    def reduce_landed(tag, names, bufs):
        n = len(names)
        own = [sum_over_chips(t, q, chip, f"sum_{name}") for (name, _), t, q in zip(names, bufs[:n], bufs[n:])]
        send_sems, recv_sems, bufs, _ = copies_start(own, own, _to_sibling_copies(n, None), n, own[0], f"grads_halves_{tag}")
        halves.append((tag, names, send_sems, recv_sems, bufs))

    def halves_land(after):
        for tag, names, send_sems, recv_sems, bufs in halves:
            n = len(names)
            bufs = copies_wait(send_sems, recv_sems, bufs, _to_sibling_copies(n, None), after, f"grads_halves_wait_{tag}")
            for key, o, r in zip(names, bufs[:n], bufs[n:]):
                own_of[key], recv_of[key] = o, r
        halves.clear()

    def emit(key, grads_of, after):
        kind, layer = key
        landed = chips_wait([after]) if to_chips else None
        begun = {name for _, part_names, _, _, _ in to_sibling for name, _ in part_names}
        rest = {name: g for name, g in grads_of.items() if name not in begun}
        if rest:
            sibling_start(f"{kind}{layer}", [(name, layer if kind != "mix" else 0) for name in rest],
                          [_from_matmul_layout(name, g) for name, g in rest.items()], after)
        token = chips_start(f"{kind}{layer}", [after])
        if landed:
            reduce_landed(*landed)
        return token

    first_gate, small_all = finish("0a", pending["0c"][2][0])
    n_norm_rows = norms.shape[0] * norms.shape[1]
    norms_all = jnp.concatenate([small_all[j, :n_norm_rows].reshape(norms.shape) for j in range(N_CHIPS)], axis=-1)
    pool_scale_all = jnp.concatenate([small_all[j, n_norm_rows:n_norm_rows + 1] for j in range(N_CHIPS)], axis=-1)
    small = dict(attn_sinks=attn_sinks[0], sgu_ln_g=sgu_ln_g, sgu_ln_b=sgu_ln_b, sgu_w=sgu_w[0], sgu_b=sgu_b[0], pool_scale=pool_scale_all)

    loss_part, dx, g_small, g_norms, g_mem_norm, last_token = device_step(
        x[0], mem[0], loss_target[0], norms_all, mem_norm, small, fetch, emit, early)
    loss = lax.psum(0.5 * jnp.sum(loss_part) / x.shape[-1], ("x", "y", "c"))
    last_names = {name for name, _ in to_chips[0][1]}
    order = last_token.reshape(1, 1)
    halves_land([dx])

    grads, delta, new_m, new_v = {}, {}, {}, {}

    def update(name):
        n_l = _as3(w[name]).shape[0]
        res = adamw_from_halves(_as3(w[name]), [own_of[name, l] for l in range(n_l)], [recv_of[name, l] for l in range(n_l)],
                                _as3(mom[name]), _as3(var[name]), core, order, f"adamw_{name}")
        grads[name], delta[name], new_m[name], new_v[name] = (t.reshape(w[name].shape) for t in res)

    for name in BIG_NAMES:
        if name not in last_names:
            update(name)
    done = [delta[name] for name in BIG_NAMES if name not in last_names]
    reduce_landed(*chips_wait(done))
    halves_land(done)
    for name in BIG_NAMES:
        if name in last_names:
            update(name)

    small_g = [g_norms, g_mem_norm, g_small["attn_sinks"], g_small["sgu_ln_g"], g_small["sgu_ln_b"], g_small["sgu_w"], g_small["sgu_b"],
               g_small["pool_scale"]]
    packed = _pack(small_g, 16)
    summed = sum_slots(gather_all_devices(packed, "small_grads_all"), "small_grads_sum")
    s_norms, s_mem, s_sinks, s_lg, s_lb, s_w, s_b, s_scale = _unpack(summed, small_g)
    grads["norms"] = lax.dynamic_slice_in_dim(s_norms, chip_id * n_shard, n_shard, axis=2)
    grads["pool_scale"] = lax.dynamic_slice_in_dim(s_scale, chip_id * n_shard, n_shard, axis=1)
    grads.update(mem_norm=s_mem, attn_sinks=s_sinks, sgu_ln_g=s_lg, sgu_ln_b=s_lb, sgu_w=s_w, sgu_b=s_b)
    like = [w[n] for n in SMALL_NAMES]
    packs = [_pack([src[n] for n in SMALL_NAMES], 16)[None] for src in (w, grads, mom, var)]
    for dst, t in zip((delta, new_m, new_v), adamw(*packs, "adamw_small")):
        for n, a in zip(SMALL_NAMES, _unpack(t[0], like)):
            dst[n] = a

    outs = [loss, dx[None]]
    for group in (grads, delta, new_m, new_v):
        outs += [group[n] for n in WEIGHT_ORDER]
    return tuple(outs)
```

```python
import functools

import jax
import jax.numpy as jnp
from jax import lax
from jax.experimental import pallas as pl
from jax.experimental.pallas import tpu as pltpu

F32 = jnp.float32
BF16 = jnp.bfloat16
MESH = pl.DeviceIdType.MESH

EPS = 1e-6
ROPE_THETA = 500000.0
ROPE_HALF = 8
HEAD_DIM = 64
N_Q_HEADS = 16
N_KV_HEADS = 2
Q_PER_KV = 8
BLOCK = 128
ATTN_WIDTH = 1024
KV_WIDTH = 128
QK_WIDTH = ATTN_WIDTH + KV_WIDTH
SGU_WIDTH = 1024
SGU_GROUPS = 8
POOL_WINDOWS = (2, 4, 8, 16)
POOL_HALO = 16
X_HEADS = 4
X_HEAD_DIM = 128
N_CHIPS = 4
N_DEV = 8

ADAM_LR = 0.001
ADAM_B1 = 0.9
ADAM_B2 = 0.999
ADAM_EPS = 1e-08
ADAM_WD = 0.01
ADAM_STEP = 10

VMEM_LIMIT_V7X = 52 * 1024 * 1024
MM_VMEM_BUDGET = 44 * 1024 * 1024
LANES = 128
ROW_TILE_BYTES = 6 * 1024 * 1024
MXU_FLOPS_V7X = 1.0e15
HBM_BYTES_PER_S_V7X = 3.0e12
VMEM_STORE_BYTES_PER_S = 4.0e12
MXU_WEIGHT_LOAD_ROWS = 192
MXU_NARROW_COLS = 128
GRID_STEP_S = 0.35e-6


def _params(sem):
    return pltpu.CompilerParams(dimension_semantics=sem, vmem_limit_bytes=VMEM_LIMIT_V7X)


def _pick(dim, pref, align):
    cands = [t for t in range(align, dim + 1, align) if dim % t == 0]
    small = [t for t in cands if t <= pref]
    if small and small[-1] * 2 >= min(pref, dim):
        return small[-1]
    return dim


def _rows(n_rows, bytes_per_row):
    want = max(16, min(1024, ROW_TILE_BYTES // max(1, bytes_per_row)))
    cands = [t for t in range(16, n_rows + 1, 16) if n_rows % t == 0 and t <= want]
    return cands[-1] if cands else n_rows


def _divisors(dim, align, most):
    return [t for t in range(align, min(dim, most) + 1, align) if dim % t == 0] or [dim]


def _mm_tiles(M, N, K, J, m_align, k_align, a_bytes, b_bytes, o_bytes, reduce, ta, products=1):
    best = None
    for tm in _divisors(M, m_align, 2048):
        for tn in _divisors(N, LANES, 2048):
            for tk in _divisors(K, k_align, 4096):
                split = K // tk > 1 or reduce
                vmem = 2 * (tm * tk * a_bytes + tk * tn * b_bytes + tm * tn * o_bytes) + tm * tn * 4 * products * (2 if split else 1)
                if ta:
                    vmem += tm * tk * a_bytes
                if vmem > MM_VMEM_BUDGET:
                    continue
                steps = J * (M // tm) * (N // tn) * (K // tk)
                mxu = 2.0 * J * M * N * K / MXU_FLOPS_V7X * (tm + MXU_WEIGHT_LOAD_ROWS) / tm * (tn + MXU_NARROW_COLS) / tn
                acc = J * M * N * (K // tk) * 8 / VMEM_STORE_BYTES_PER_S if split else 0.0
                hbm = J * (M * K * a_bytes * (N // tn) + K * N * b_bytes * (M // tm) + M * N * o_bytes) / HBM_BYTES_PER_S_V7X
                cost = max(mxu + 0.5 * acc, hbm) + steps * GRID_STEP_S
                if best is None or cost < best[0]:
                    best = (cost, tm, tn, tk)
    return best[1:]


def _rsum8(v):
    r, c = v.shape
    return v.reshape(r // 8, 8, c).sum(axis=0)


def _mm(a, b, *, name, ta=False, tb=False, batch="none", groups=0, a_cb=False, b_cb=False, o_cb=False,
        out_dtype=F32, more=(), extras=(), epilogue=None, out_dtypes=None, order=None):
    J = groups or (a.shape[0] if a.ndim == 3 else (b.shape[0] if b.ndim == 3 else 1))
    a2, b2 = a.shape[-2:], b.shape[-2:]
    M, K = (a2[1], a2[0]) if ta else a2
    N, Kb = b2 if tb else (b2[1], b2[0])
    if a_cb:
        if ta:
            M //= J
        else:
            K //= J
    if b_cb:
        if tb:
            Kb //= J
        else:
            N //= J
    assert K == Kb, (name, a.shape, b.shape)
    reduce = batch == "reduce"
    out_dtypes = list(out_dtypes or [out_dtype])
    n_terms = 1 + len(more)
    o_bytes = sum(jnp.dtype(d).itemsize for d in out_dtypes) + sum(e.dtype.itemsize for e in extras)
    n_prod = n_terms if epilogue is not None else 1
    tm, tn, tk = _mm_tiles(M, N, K, J, LANES if ta else 16, LANES if (not ta or tb) else 16, a.dtype.itemsize * n_terms,
                           b.dtype.itemsize * n_terms, o_bytes, reduce, ta, n_prod)
    nm, nn, nk = M // tm, N // tn, K // tk
    if reduce:
        grid = (nm, nn, J, nk)
        unpack = lambda m, n, j, k: (j, m, n, k)
        sem = ("parallel", "parallel", "arbitrary", "arbitrary")
    else:
        grid = (J, nm, nn, nk)
        unpack = lambda j, m, n, k: (j, m, n, k)
        sem = ("parallel", "parallel", "parallel", "arbitrary")

    def a_map(*g):
        j, m, n, k = unpack(*g)
        r, c = (k, m) if ta else (m, k)
        if a_cb:
            c = c + j * (nm if ta else nk)
        return (j, r, c) if a.ndim == 3 else (r, c)

    def b_map(*g):
        j, m, n, k = unpack(*g)
        r, c = (n, k) if tb else (k, n)
        if b_cb:
            c = c + j * (nk if tb else nn)
        return (j, r, c) if b.ndim == 3 else (r, c)

    def o_map(*g):
        j, m, n, k = unpack(*g)
        if o_cb:
            return (m, n + j * nn)
        return (j, m, n) if batch == "map" else (m, n)

    a_blk = (tk, tm) if ta else (tm, tk)
    b_blk = (tn, tk) if tb else (tk, tn)
    a_spec = pl.BlockSpec(((None,) + a_blk) if a.ndim == 3 else a_blk, a_map)
    b_spec = pl.BlockSpec(((None,) + b_blk) if b.ndim == 3 else b_blk, b_map)
    if o_cb:
        out_shape, o_blk = (M, N * J), (tm, tn)
    elif batch == "map":
        out_shape, o_blk = (J, M, N), (None, tm, tn)
    else:
        out_shape, o_blk = (M, N), (tm, tn)
    o_spec = pl.BlockSpec(o_blk, o_map)
    dims = (((0 if ta else 1,), (1 if tb else 0,)), ((), ()))
    red_axes = (2, 3) if reduce else (3,)
    split = reduce or nk > 1
    n_ex, n_out = len(extras), len(out_dtypes)
    n_ord = 0 if order is None else 1

    def body(*refs):
        refs = refs[n_ord:]
        ab_refs, ex_refs = refs[:2 * n_terms], refs[2 * n_terms:2 * n_terms + n_ex]
        o_refs, acc = refs[2 * n_terms + n_ex:2 * n_terms + n_ex + n_out], refs[2 * n_terms + n_ex + n_out:]
        prods = [lax.dot_general(ab_refs[2 * t][...], ab_refs[2 * t + 1][...], dims, preferred_element_type=F32) for t in range(n_terms)]
        if epilogue is None:
            prods = [functools.reduce(lambda p, q: p + q, prods)]

        def finish(vals):
            outs = epilogue(vals, [e[...] for e in ex_refs]) if epilogue is not None else vals
            for o_ref, val in zip(o_refs, outs):
                o_ref[...] = val.astype(o_ref.dtype)

        if not split:
            finish(prods)
            return
        first = functools.reduce(jnp.logical_and, [pl.program_id(ax) == 0 for ax in red_axes])
        last = functools.reduce(jnp.logical_and, [pl.program_id(ax) == grid[ax] - 1 for ax in red_axes])

        @pl.when(first)
        def _():
            for acc_ref, prod in zip(acc, prods):
                acc_ref[...] = prod

        @pl.when(jnp.logical_not(first))
        def _():
            for acc_ref, prod in zip(acc, prods):
                acc_ref[...] += prod

        @pl.when(last)
        def _():
            finish([acc_ref[...] for acc_ref in acc])

    operands = [order] * n_ord + [a, b] + [t for pair in more for t in pair] + list(extras)
    res = pl.pallas_call(
        body, name=name, grid=grid, out_specs=[o_spec] * n_out,
        in_specs=[pl.BlockSpec(memory_space=pl.ANY)] * n_ord + [a_spec, b_spec] * n_terms + [o_spec] * n_ex,
        out_shape=[jax.ShapeDtypeStruct(out_shape, d) for d in out_dtypes],
        scratch_shapes=[pltpu.VMEM((tm, tn), F32)] * (n_prod if split else 0), compiler_params=_params(sem),
    )(*operands)
    return res if epilogue is not None else res[0]


def _rowwise(fn, tiled, whole, outs, accs, *, name):
    n_rows = tiled[0].shape[0]
    row_bytes = sum(t.shape[1] * t.dtype.itemsize for t in tiled) + sum(c * jnp.dtype(d).itemsize for c, d in outs)
    tr = _rows(n_rows, row_bytes)
    n_t, n_w, n_o = len(tiled), len(whole), len(outs)

    def body(*refs):
        i = pl.program_id(0)
        t_refs, w_refs = refs[:n_t], refs[n_t:n_t + n_w]
        o_refs, a_refs = refs[n_t + n_w:n_t + n_w + n_o], refs[n_t + n_w + n_o:]
        o_vals, a_vals = fn(i, *[r[...] for r in t_refs], *[r[...] for r in w_refs])
        for r, v in zip(o_refs, o_vals):
            r[...] = v.astype(r.dtype)
        if a_refs:
            @pl.when(i == 0)
            def _():
                for r in a_refs:
                    r[...] = jnp.zeros_like(r)
            for r, v in zip(a_refs, a_vals):
                r[...] += v

    in_specs = [pl.BlockSpec((tr, t.shape[1]), lambda i: (i, 0)) for t in tiled]
    in_specs += [pl.BlockSpec(w.shape, lambda i, nd=w.ndim: (0,) * nd) for w in whole]
    out_specs = [pl.BlockSpec((tr, c), lambda i: (i, 0)) for c, _ in outs]
    out_specs += [pl.BlockSpec(s, lambda i, nd=len(s): (0,) * nd) for s, _ in accs]
    out_shape = [jax.ShapeDtypeStruct((n_rows, c), d) for c, d in outs]
    out_shape += [jax.ShapeDtypeStruct(s, d) for s, d in accs]
    res = pl.pallas_call(
        body, name=name, grid=(n_rows // tr,), in_specs=in_specs, out_specs=out_specs, out_shape=out_shape,
        compiler_params=_params(("arbitrary",) if accs else ("parallel",)),
    )(*tiled, *whole)
    return res


def _rms_stats(x):
    r = lax.rsqrt(jnp.mean(x * x, axis=-1, keepdims=True) + EPS)
    return x * r, r


def rms_fwd(x, g, out_dtype, name):
    def fn(i, x, g):
        xhat, _ = _rms_stats(x)
        return [xhat * g], []
    return _rowwise(fn, [x], [g], [(x.shape[1], out_dtype)], [], name=name)[0]


def postnorm_res(x, y, g, s, name):
    def fn(i, x, y, g):
        yhat, _ = _rms_stats(y)
        return [x + s * (yhat * g)], []
    return _rowwise(fn, [x, y], [g], [(x.shape[1], F32)], [], name=name)[0]


def rms_bwd(xin, g, douts, scale, add, out_dtype, name):
    n_d = len(douts)

    def fn(i, x, *rest):
        ds, rest = rest[:n_d], rest[n_d:]
        ad = rest[0] if add is not None else None
        g = rest[-1]
        xhat, r = _rms_stats(x)
        d = ds[0].astype(F32)
        for e in ds[1:]:
            d = d + e.astype(F32)
        if scale != 1.0:
            d = d * scale
        dg = _rsum8(d * xhat)
        dxhat = d * g
        dx = r * (dxhat - xhat * jnp.mean(dxhat * xhat, axis=-1, keepdims=True))
        if ad is not None:
            dx = dx + ad
        return [dx], [dg]

    tiled = [xin, *douts] + ([add] if add is not None else [])
    dx, dg = _rowwise(fn, tiled, [g], [(xin.shape[1], out_dtype)], [((8, xin.shape[1]), F32)], name=name)
    return dx, dg


def _silu_parts(g):
    sg = 1.0 / (1.0 + jnp.exp(-g))
    return g * sg, sg


def _swiglu_tiles(products, saved):
    u, = products
    return [u, _silu_parts(saved[0].astype(F32))[0] * u]


def _swiglu_bwd_tiles(products, saved):
    da, = products
    g, u = saved[0].astype(F32), saved[1].astype(F32)
    s, sg = _silu_parts(g)
    return [da * u * (sg * (1.0 + g * (1.0 - sg))), da * s]


def scale_cols(y, s, name):
    def fn(i, y, s):
        return [y * s], []
    return _rowwise(fn, [y], [s], [(y.shape[1], F32)], [], name=name)[0]


def scale_cols_bwd(dm, y, s, name):
    def fn(i, dm, y, s):
        return [dm * s], [_rsum8(dm * y)]
    return _rowwise(fn, [dm, y], [s], [(y.shape[1], BF16)], [((8, y.shape[1]), F32)], name=name)


def loss_and_grad(y, target, name):
    n_feat = y.shape[1]

    def fn(i, y, t):
        e = y - t
        return [e * (1.0 / n_feat)], [_rsum8(e * e)]
    dy, part = _rowwise(fn, [y, target], [], [(n_feat, F32)], [((8, n_feat), F32)], name=name)
    return part, dy


def cast_into_slot(w3, layer, chip, name):
    _, n_rows, n_cols = w3.shape
    tr = _rows(n_rows, n_cols * 6)

    def body(chip_ref, w_ref, o_ref):
        o_ref[...] = w_ref[...].astype(BF16)

    grid_spec = pltpu.PrefetchScalarGridSpec(
        num_scalar_prefetch=1, grid=(n_rows // tr,),
        in_specs=[pl.BlockSpec((None, tr, n_cols), lambda i, chip_ref: (layer, i, 0))],
        out_specs=pl.BlockSpec((None, tr, n_cols), lambda i, chip_ref: (chip_ref[0], i, 0)))
    return pl.pallas_call(
        body, name=name, grid_spec=grid_spec, out_shape=jax.ShapeDtypeStruct((N_CHIPS, n_rows, n_cols), BF16),
        compiler_params=_params(("parallel",)),
    )(chip, w3)


def _adam_update(w, g, m, v):
    c1 = 1.0 / (1.0 - ADAM_B1 ** ADAM_STEP)
    c2 = 1.0 / (1.0 - ADAM_B2 ** ADAM_STEP)
    m = ADAM_B1 * m + (1.0 - ADAM_B1) * g
    v = ADAM_B2 * v + (1.0 - ADAM_B2) * (g * g)
    return -ADAM_LR * ((m * c1) / (jnp.sqrt(v * c2) + ADAM_EPS) + ADAM_WD * w), m, v


def adamw(w, g, m, v, name):
    n_l, n_rows, n_cols = w.shape
    tr = _rows(n_rows, n_cols * 4 * 7)

    def body(w_ref, g_ref, m_ref, v_ref, d_ref, mo_ref, vo_ref):
        d_ref[...], mo_ref[...], vo_ref[...] = _adam_update(w_ref[...], g_ref[...], m_ref[...], v_ref[...])

    spec = pl.BlockSpec((None, tr, n_cols), lambda l, i: (l, i, 0))
    shp = jax.ShapeDtypeStruct(w.shape, F32)
    return pl.pallas_call(
        body, name=name, grid=(n_l, n_rows // tr), in_specs=[spec] * 4, out_specs=[spec] * 3, out_shape=[shp] * 3,
        compiler_params=_params(("parallel", "parallel")),
    )(w, g, m, v)


def adamw_from_halves(w, own, recv, m, v, core, order, name):
    n_l, n_rows, n_cols = w.shape
    half = n_rows // 2
    tr = _rows(half, n_cols * 4 * 9)
    per = half // tr

    def body(core_ref, *refs):
        w_ref, m_ref, v_ref, order_ref = refs[:4]
        own_refs, recv_refs = refs[4:4 + n_l], refs[4 + n_l:4 + 2 * n_l]
        g_ref, d_ref, mo_ref, vo_ref = refs[4 + 2 * n_l:]
        l, h = pl.program_id(0), pl.program_id(1)
        mine = h == core_ref[0]
        g = jnp.where(mine, own_refs[0][...], recv_refs[0][...])
        for k in range(1, n_l):
            g = jnp.where(l == k, jnp.where(mine, own_refs[k][...], recv_refs[k][...]), g)
        g = g + order_ref[...]
        g_ref[...] = g
        d_ref[...], mo_ref[...], vo_ref[...] = _adam_update(w_ref[...], g, m_ref[...], v_ref[...])

    full = pl.BlockSpec((None, tr, n_cols), lambda l, h, i, core_ref: (l, h * per + i, 0))

    def piece(layer, is_own):
        def index(l, h, i, core_ref):
            used = (l == layer) & ((h == core_ref[0]) == is_own)
            return (jnp.where(used, i, 0), 0)
        return pl.BlockSpec((tr, n_cols), index)

    grid_spec = pltpu.PrefetchScalarGridSpec(
        num_scalar_prefetch=1, grid=(n_l, 2, per),
        in_specs=[full] * 3 + [pl.BlockSpec((1, 1), lambda l, h, i, core_ref: (0, 0))]
        + [piece(k, True) for k in range(n_l)] + [piece(k, False) for k in range(n_l)],
        out_specs=[full] * 4)
    return pl.pallas_call(
        body, name=name, grid_spec=grid_spec, out_shape=[jax.ShapeDtypeStruct(w.shape, F32)] * 4,
        compiler_params=_params(("parallel", "parallel", "parallel")),
    )(core, w, m, v, order, *own, *recv)


def rope_tables(n_tok):
    inv = ROPE_THETA ** (-jnp.arange(ROPE_HALF, dtype=F32) * 2.0 / (2 * ROPE_HALF))
    ang = jnp.arange(n_tok, dtype=F32)[:, None] * inv[None, :]
    cos, sin = jnp.cos(ang), jnp.sin(ang)
    rest = HEAD_DIM - 2 * ROPE_HALF
    one, zero, z8 = jnp.ones((n_tok, rest), F32), jnp.zeros((n_tok, rest), F32), jnp.zeros((n_tok, ROPE_HALF), F32)
    c = jnp.concatenate([cos, cos, one], axis=1)
    s1 = jnp.concatenate([-sin, z8, zero], axis=1)
    s2 = jnp.concatenate([z8, sin, zero], axis=1)
    two = lambda t: jnp.concatenate([t, t], axis=1)
    return two(c), two(s1), two(s2)


def rope_apply(x, tabs, n_col_blocks, inverse, out_dtype, name):
    n_tok = x.shape[0]
    tr = _rows(n_tok, LANES * 4 * 6)

    def body(x_ref, c_ref, s1_ref, s2_ref, o_ref):
        x = x_ref[...].astype(F32)
        if inverse:
            out = x * c_ref[...] + pltpu.roll(x * s1_ref[...], ROPE_HALF, 1) + pltpu.roll(x * s2_ref[...], LANES - ROPE_HALF, 1)
        else:
            out = x * c_ref[...] + pltpu.roll(x, LANES - ROPE_HALF, 1) * s1_ref[...] + pltpu.roll(x, ROPE_HALF, 1) * s2_ref[...]
        o_ref[...] = out.astype(o_ref.dtype)

    tab_spec = pl.BlockSpec((tr, LANES), lambda i, c: (i, 0))
    blk = pl.BlockSpec((tr, LANES), lambda i, c: (i, c))
    return pl.pallas_call(
        body, name=name, grid=(n_tok // tr, n_col_blocks), in_specs=[blk, tab_spec, tab_spec, tab_spec], out_specs=blk,
        out_shape=jax.ShapeDtypeStruct((n_tok, n_col_blocks * LANES), out_dtype), compiler_params=_params(("parallel", "parallel")),
    )(x, *tabs)


def _swa_probs(q, k, sink, n):
    rows = Q_PER_KV * BLOCK
    s = lax.dot_general(q, k, (((1,), (1,)), ((), ())), preferred_element_type=F32) * (HEAD_DIM ** -0.5)
    qi = lax.broadcasted_iota(jnp.int32, (rows, 2 * BLOCK), 0) & (BLOCK - 1)
    kj = lax.broadcasted_iota(jnp.int32, (rows, 2 * BLOCK), 1)
    rel = qi + BLOCK - kj
    valid = (rel >= 0) & (rel < BLOCK) & ((n > 0) | (kj >= BLOCK))
    s = jnp.where(valid, s, -1e30)
    m = jnp.maximum(jnp.max(s, axis=-1, keepdims=True), sink)
    e = jnp.exp(s - m)
    es = jnp.exp(sink - m)
    inv = 1.0 / (jnp.sum(e, axis=-1, keepdims=True) + es)
    return e * inv, es * inv


def _swa_specs(n_blocks):
    q_spec = pl.BlockSpec((Q_PER_KV, BLOCK, HEAD_DIM), lambda h, n: (h, n, 0))
    prev = pl.BlockSpec((None, BLOCK, HEAD_DIM), lambda h, n: (h, jnp.maximum(n - 1, 0), 0))
    cur = pl.BlockSpec((None, BLOCK, HEAD_DIM), lambda h, n: (h, n, 0))
    sink = pl.BlockSpec((None, Q_PER_KV * BLOCK, 1), lambda h, n: (h, 0, 0))
    return q_spec, prev, cur, sink


def swa_fwd(q, k, v, sink_rows, name):
    n_tok = q.shape[1]
    q_spec, prev, cur, sink = _swa_specs(n_tok // BLOCK)

    def body(q_ref, kp_ref, kc_ref, vp_ref, vc_ref, s_ref, o_ref):
        n = pl.program_id(1)
        qq = q_ref[...].reshape(Q_PER_KV * BLOCK, HEAD_DIM)
        kk = jnp.concatenate([kp_ref[...], kc_ref[...]], axis=0)
        vv = jnp.concatenate([vp_ref[...], vc_ref[...]], axis=0)
        p, _ = _swa_probs(qq, kk, s_ref[...], n)
        o = jnp.dot(p.astype(BF16), vv, preferred_element_type=F32)
        o_ref[...] = o.reshape(Q_PER_KV, BLOCK, HEAD_DIM).astype(o_ref.dtype)

    return pl.pallas_call(
        body, name=name, grid=(N_KV_HEADS, n_tok // BLOCK), in_specs=[q_spec, prev, cur, prev, cur, sink], out_specs=q_spec,
        out_shape=jax.ShapeDtypeStruct(q.shape, BF16), compiler_params=_params(("parallel", "parallel")),
    )(q, k, k, v, v, sink_rows)


def swa_bwd(q, k, v, sink_rows, do, name):
    n_tok = q.shape[1]
    nb = n_tok // BLOCK
    q_spec, prev, cur, sink = _swa_specs(nb)
    rows = Q_PER_KV * BLOCK

    def body(q_ref, kp_ref, kc_ref, vp_ref, vc_ref, s_ref, do_ref, dq_ref, dkp_ref, dkc_ref, dvp_ref, dvc_ref, ds_ref):
        n = pl.program_id(1)
        qq = q_ref[...].reshape(rows, HEAD_DIM)
        dd = do_ref[...].reshape(rows, HEAD_DIM)
        kk = jnp.concatenate([kp_ref[...], kc_ref[...]], axis=0)
        vv = jnp.concatenate([vp_ref[...], vc_ref[...]], axis=0)
        p, ps = _swa_probs(qq, kk, s_ref[...], n)
        dp = lax.dot_general(dd, vv, (((1,), (1,)), ((), ())), preferred_element_type=F32)
        delta = jnp.sum(p * dp, axis=-1, keepdims=True)
        ds = (p * (dp - delta) * (HEAD_DIM ** -0.5)).astype(BF16)
        dq = jnp.dot(ds, kk, preferred_element_type=F32)
        dk = lax.dot_general(ds, qq, (((0,), (0,)), ((), ())), preferred_element_type=F32)
        dv = lax.dot_general(p.astype(BF16), dd, (((0,), (0,)), ((), ())), preferred_element_type=F32)
        dq_ref[...] = dq.reshape(Q_PER_KV, BLOCK, HEAD_DIM).astype(dq_ref.dtype)
        dkp_ref[...] = dk[:BLOCK]
        dkc_ref[...] = dk[BLOCK:]
        dvp_ref[...] = dv[:BLOCK]
        dvc_ref[...] = dv[BLOCK:]
        dsink = jnp.broadcast_to(-ps * delta, (rows, LANES)).reshape(Q_PER_KV, BLOCK, LANES)
        ds_ref[...] = jnp.sum(dsink, axis=1)

    part = pl.BlockSpec((None, None, BLOCK, HEAD_DIM), lambda h, n: (h, n, 0, 0))
    part_shape = jax.ShapeDtypeStruct((N_KV_HEADS, nb, BLOCK, HEAD_DIM), F32)
    return pl.pallas_call(
        body, name=name, grid=(N_KV_HEADS, nb), in_specs=[q_spec, prev, cur, prev, cur, sink, q_spec],
        out_specs=[q_spec, part, part, part, part, pl.BlockSpec((None, None, Q_PER_KV, LANES), lambda h, n: (h, n, 0, 0))],
        out_shape=[jax.ShapeDtypeStruct(q.shape, BF16), part_shape, part_shape, part_shape, part_shape,
                   jax.ShapeDtypeStruct((N_KV_HEADS, nb, Q_PER_KV, LANES), F32)],
        compiler_params=_params(("parallel", "parallel")),
    )(q, k, k, v, v, sink_rows, do)


def _to_heads(t, n_heads):
    return t.reshape(t.shape[0], n_heads, HEAD_DIM).transpose(1, 0, 2)


def _from_heads(t):
    return t.transpose(1, 0, 2).reshape(t.shape[1], -1)


def _fold_kv_grad(prev, cur):
    shifted = jnp.concatenate([prev[:, 1:], jnp.zeros_like(prev[:, :1])], axis=1)
    tot = (cur + shifted).reshape(N_KV_HEADS, -1, HEAD_DIM)
    return _from_heads(tot)


def _x_probs(qh, kh):
    s = lax.dot_general(qh, kh, (((1,), (1,)), ((), ())), preferred_element_type=F32) * (X_HEAD_DIM ** -0.5)
    e = jnp.exp(s - jnp.max(s, axis=-1, keepdims=True))
    return e * (1.0 / jnp.sum(e, axis=-1, keepdims=True))


def xattn_fwd(q, k, v, name):
    n_tok, width = q.shape
    n_mem = k.shape[0]
    tq = _pick(n_tok, 512, 16)

    def body(q_ref, k_ref, v_ref, o_ref):
        for h in range(X_HEADS):
            cols = slice(h * X_HEAD_DIM, (h + 1) * X_HEAD_DIM)
            p = _x_probs(q_ref[:, cols], k_ref[:, cols])
            o_ref[:, cols] = jnp.dot(p.astype(BF16), v_ref[:, cols], preferred_element_type=F32).astype(o_ref.dtype)

    row = pl.BlockSpec((tq, width), lambda i: (i, 0))
    mem = pl.BlockSpec((n_mem, width), lambda i: (0, 0))
    return pl.pallas_call(
        body, name=name, grid=(n_tok // tq,), in_specs=[row, mem, mem], out_specs=row,
        out_shape=jax.ShapeDtypeStruct(q.shape, BF16), compiler_params=_params(("parallel",)),
    )(q, k, v)


def xattn_bwd(q, k, v, do, name):
    n_tok, width = q.shape
    n_mem = k.shape[0]
    tq = _pick(n_tok, 512, 16)

    def body(q_ref, k_ref, v_ref, do_ref, dq_ref, dk_ref, dv_ref):
        @pl.when(pl.program_id(0) == 0)
        def _():
            dk_ref[...] = jnp.zeros_like(dk_ref)
            dv_ref[...] = jnp.zeros_like(dv_ref)

        for h in range(X_HEADS):
            cols = slice(h * X_HEAD_DIM, (h + 1) * X_HEAD_DIM)
            qh, kh, vh, dh = q_ref[:, cols], k_ref[:, cols], v_ref[:, cols], do_ref[:, cols]
            p = _x_probs(qh, kh)
            dp = lax.dot_general(dh, vh, (((1,), (1,)), ((), ())), preferred_element_type=F32)
            delta = jnp.sum(p * dp, axis=-1, keepdims=True)
            ds = (p * (dp - delta) * (X_HEAD_DIM ** -0.5)).astype(BF16)
            dq_ref[:, cols] = jnp.dot(ds, kh, preferred_element_type=F32).astype(dq_ref.dtype)
            dk_ref[:, cols] += lax.dot_general(ds, qh, (((0,), (0,)), ((), ())), preferred_element_type=F32)
            dv_ref[:, cols] += lax.dot_general(p.astype(BF16), dh, (((0,), (0,)), ((), ())), preferred_element_type=F32)

    row = pl.BlockSpec((tq, width), lambda i: (i, 0))
    mem = pl.BlockSpec((n_mem, width), lambda i: (0, 0))
    return pl.pallas_call(
        body, name=name, grid=(n_tok // tq,), in_specs=[row, mem, mem, row], out_specs=[row, mem, mem],
        out_shape=[jax.ShapeDtypeStruct(q.shape, BF16), jax.ShapeDtypeStruct(k.shape, F32), jax.ShapeDtypeStruct(k.shape, F32)],
        compiler_params=_params(("arbitrary",)),
    )(q, k, v, do)


GELU_C = 0.7978845608028654
GELU_A = 0.044715


def _gelu_parts(x):
    x2 = x * x
    t = jnp.tanh(GELU_C * x * (1.0 + GELU_A * x2))
    y = 0.5 * x * (1.0 + t)
    dy = 0.5 * (1.0 + t) + 0.5 * x * (1.0 - t * t) * GELU_C * (1.0 + 3.0 * GELU_A * x2)
    return y, dy


def _sgu_norm(v, ln_g, ln_b):
    mu = jnp.mean(v, axis=-1, keepdims=True)
    vc = v - mu
    r = lax.rsqrt(jnp.mean(vc * vc, axis=-1, keepdims=True) + EPS)
    xhat = vc * r
    return xhat * ln_g + ln_b, xhat, r


def _causal_weights(w_ref):
    i = lax.broadcasted_iota(jnp.int32, (BLOCK, BLOCK), 0)
    j = lax.broadcasted_iota(jnp.int32, (BLOCK, BLOCK), 1)
    return [jnp.where(i >= j, w_ref[g], 0.0).astype(BF16) for g in range(SGU_GROUPS)]


def sgu_fwd(u_pre, v_pre, ln_g, ln_b, w_s, bias_rows, name):
    n_tok = u_pre.shape[0]
    tm = _pick(n_tok, 512, BLOCK)

    def body(u_ref, v_ref, g_ref, b_ref, w_ref, bb_ref, o_ref):
        vn, _, _ = _sgu_norm(_gelu_parts(v_ref[...])[0], g_ref[...], b_ref[...])
        vn = vn.astype(BF16)
        wc = _causal_weights(w_ref)
        for c in range(tm // BLOCK):
            rows = slice(c * BLOCK, (c + 1) * BLOCK)
            for g in range(SGU_GROUPS):
                cols = slice(g * LANES, (g + 1) * LANES)
                mixed = jnp.dot(wc[g], vn[rows, cols], preferred_element_type=F32) + bb_ref[g]
                u = _gelu_parts(u_ref[rows, cols])[0]
                o_ref[rows, cols] = (u * mixed).astype(o_ref.dtype)

    row = pl.BlockSpec((tm, SGU_WIDTH), lambda i: (i, 0))
    vec = pl.BlockSpec((1, SGU_WIDTH), lambda i: (0, 0))
    mat = pl.BlockSpec((SGU_GROUPS, BLOCK, LANES), lambda i: (0, 0, 0))
    return pl.pallas_call(
        body, name=name, grid=(n_tok // tm,), in_specs=[row, row, vec, vec, mat, mat], out_specs=row,
        out_shape=jax.ShapeDtypeStruct((n_tok, SGU_WIDTH), BF16), compiler_params=_params(("parallel",)),
    )(u_pre, v_pre, ln_g, ln_b, w_s, bias_rows)


def sgu_bwd(u_pre, v_pre, ln_g, ln_b, w_s, bias_rows, dgate, name):
    n_tok = u_pre.shape[0]
    tm = _pick(n_tok, 512, BLOCK)

    def body(u_ref, v_ref, g_ref, b_ref, w_ref, bb_ref, dg_ref, du_ref, dv_ref, dw_ref, db_ref, dlg_ref, dlb_ref, dvn_ref):
        @pl.when(pl.program_id(0) == 0)
        def _():
            dw_ref[...] = jnp.zeros_like(dw_ref)
            db_ref[...] = jnp.zeros_like(db_ref)
            dlg_ref[...] = jnp.zeros_like(dlg_ref)
            dlb_ref[...] = jnp.zeros_like(dlb_ref)

        gv, dgv = _gelu_parts(v_ref[...])
        vn, xhat, r = _sgu_norm(gv, g_ref[...], b_ref[...])
        vn = vn.astype(BF16)
        wc = _causal_weights(w_ref)
        for c in range(tm // BLOCK):
            rows = slice(c * BLOCK, (c + 1) * BLOCK)
            for g in range(SGU_GROUPS):
                cols = slice(g * LANES, (g + 1) * LANES)
                vt = vn[rows, cols]
                mixed = jnp.dot(wc[g], vt, preferred_element_type=F32) + bb_ref[g]
                u, du_dpre = _gelu_parts(u_ref[rows, cols])
                dgate_t = dg_ref[rows, cols].astype(F32)
                du_ref[rows, cols] = (dgate_t * mixed * du_dpre).astype(du_ref.dtype)
                dmix = dgate_t * u
                dmix_b = dmix.astype(BF16)
                db_ref[g] += dmix
                dw_ref[g] += lax.dot_general(dmix_b, vt, (((1,), (1,)), ((), ())), preferred_element_type=F32)
                dvn_ref[rows, cols] = lax.dot_general(wc[g], dmix_b, (((0,), (0,)), ((), ())), preferred_element_type=F32)
        dvn = dvn_ref[...]
        dlg_ref[...] += _rsum8(dvn * xhat)
        dlb_ref[...] += _rsum8(dvn)
        dxhat = dvn * g_ref[...]
        dgv_in = r * (dxhat - jnp.mean(dxhat, axis=-1, keepdims=True) - xhat * jnp.mean(dxhat * xhat, axis=-1, keepdims=True))
        dv_ref[...] = (dgv_in * dgv).astype(dv_ref.dtype)

    row = pl.BlockSpec((tm, SGU_WIDTH), lambda i: (i, 0))
    vec = pl.BlockSpec((1, SGU_WIDTH), lambda i: (0, 0))
    mat = pl.BlockSpec((SGU_GROUPS, BLOCK, LANES), lambda i: (0, 0, 0))
    part = pl.BlockSpec((8, SGU_WIDTH), lambda i: (0, 0))
    mat_shape = jax.ShapeDtypeStruct((SGU_GROUPS, BLOCK, LANES), F32)
    part_shape = jax.ShapeDtypeStruct((8, SGU_WIDTH), F32)
    act_shape = jax.ShapeDtypeStruct((n_tok, SGU_WIDTH), BF16)
    return pl.pallas_call(
        body, name=name, grid=(n_tok // tm,), in_specs=[row, row, vec, vec, mat, mat, row],
        out_specs=[row, row, mat, mat, part, part], out_shape=[act_shape, act_shape, mat_shape, mat_shape, part_shape, part_shape],
        scratch_shapes=[pltpu.VMEM((tm, SGU_WIDTH), F32)], compiler_params=_params(("arbitrary",)),
    )(u_pre, v_pre, ln_g, ln_b, w_s, bias_rows, dgate)


def _pool_tile(n_tok):
    return _pick(n_tok, 256, POOL_HALO)


def pool_fwd(h, name):
    n_tok, width = h.shape
    gw = width // len(POOL_WINDOWS)
    tm = _pool_tile(n_tok)
    per = tm // POOL_HALO

    def body(cur_ref, halo_ref, o_ref, buf_ref):
        i = pl.program_id(0)
        buf_ref[0:POOL_HALO, :] = jnp.where(i > 0, halo_ref[...], 0.0)
        buf_ref[POOL_HALO:, :] = cur_ref[...]
        tok = i * tm + lax.broadcasted_iota(jnp.int32, (tm, 1), 0)
        for g, w in enumerate(POOL_WINDOWS):
            cols = slice(g * gw, (g + 1) * gw)
            acc = buf_ref[POOL_HALO:, cols]
            for j in range(1, w):
                acc = acc + buf_ref[POOL_HALO - j:POOL_HALO - j + tm, cols]
            cnt = jnp.minimum(tok + 1, w).astype(F32)
            o_ref[:, cols] = (acc / cnt - cur_ref[:, cols]).astype(o_ref.dtype)

    return pl.pallas_call(
        body, name=name, grid=(n_tok // tm,),
        in_specs=[pl.BlockSpec((tm, width), lambda i: (i, 0)),
                  pl.BlockSpec((POOL_HALO, width), lambda i: (jnp.maximum(i * per - 1, 0), 0))],
        out_specs=pl.BlockSpec((tm, width), lambda i: (i, 0)), out_shape=jax.ShapeDtypeStruct(h.shape, BF16),
        scratch_shapes=[pltpu.VMEM((tm + POOL_HALO, width), F32)], compiler_params=_params(("parallel",)),
    )(h, h)


def pool_bwd(dp, name):
    n_tok, width = dp.shape
    gw = width // len(POOL_WINDOWS)
    tm = _pool_tile(n_tok)
    per = tm // POOL_HALO
    n_steps = n_tok // tm

    def body(cur_ref, halo_ref, o_ref, buf_ref):
        i = pl.program_id(0)
        tok = i * tm + lax.broadcasted_iota(jnp.int32, (tm, 1), 0)
        for g, w in enumerate(POOL_WINDOWS):
            cols = slice(g * gw, (g + 1) * gw)
            cnt = jnp.minimum(tok + 1, w).astype(F32)
            buf_ref[0:tm, cols] = cur_ref[:, cols] / cnt
            buf_ref[tm:, cols] = jnp.where(i < n_steps - 1, halo_ref[:, cols] / float(w), 0.0)
        for g, w in enumerate(POOL_WINDOWS):
            cols = slice(g * gw, (g + 1) * gw)
            acc = buf_ref[0:tm, cols]
            for j in range(1, w):
                acc = acc + buf_ref[j:j + tm, cols]
            o_ref[:, cols] = acc - cur_ref[:, cols]

    return pl.pallas_call(
        body, name=name, grid=(n_steps,),
        in_specs=[pl.BlockSpec((tm, width), lambda i: (i, 0)),
                  pl.BlockSpec((POOL_HALO, width), lambda i: (jnp.minimum((i + 1) * per, n_tok // POOL_HALO - 1), 0))],
        out_specs=pl.BlockSpec((tm, width), lambda i: (i, 0)), out_shape=jax.ShapeDtypeStruct(dp.shape, F32),
        scratch_shapes=[pltpu.VMEM((tm + POOL_HALO, width), F32)], compiler_params=_params(("parallel",)),
    )(dp, dp)


def _ffn_fwd(x, ga, gb, wg, wu, wd, tag):
    h = rms_fwd(x, ga, BF16, f"{tag}_prenorm")
    G = _mm(h, wg, batch="map", out_dtype=BF16, name=f"{tag}_gate")
    wu = wu(G) if callable(wu) else wu
    U, A = _mm(h, wu, batch="map", extras=[G], epilogue=_swiglu_tiles, out_dtypes=[BF16] * 2, name=f"{tag}_up")
    wd = wd(A) if callable(wd) else wd
    y = _mm(A, wd, batch="reduce", name=f"{tag}_down")
    return postnorm_res(x, y, gb, 0.5, f"{tag}_postnorm"), (x, h, G, U, A, y, wg, wu, wd)


def _ffn_bwd(res, ga, gb, dx2, tag, early):
    x, h, G, U, A, y, wg, wu, wd = res
    dy, dgb = rms_bwd(y, gb, [dx2], 0.5, None, BF16, f"{tag}_postnorm_bwd")
    dG, dU = _mm(dy, wd, tb=True, batch="map", extras=[G, U], epilogue=_swiglu_bwd_tiles, out_dtypes=[BF16] * 2, name=f"{tag}_down_dx")
    dwd = _mm(A, dy, ta=True, batch="map", name=f"{tag}_down_dw", out_dtype=BF16)
    dwg = _mm(h, dG, ta=True, batch="map", name=f"{tag}_gate_dw", out_dtype=BF16, order=early("wd", dwd))
    dwu = _mm(h, dU, ta=True, batch="map", name=f"{tag}_up_dw", out_dtype=BF16, order=early("wg", dwg))
    dh = _mm(dG, wg, tb=True, batch="reduce", more=[(dU, wu)], name=f"{tag}_gate_up_dx", order=early("wu", dwu))
    dx, dga = rms_bwd(x, ga, [dh], 1.0, dx2, F32, f"{tag}_prenorm_bwd")
    return dx, dga, dgb, dwg, dwu, dwd


def _sink_rows(sinks):
    return jnp.repeat(sinks.reshape(N_KV_HEADS, Q_PER_KV), BLOCK, axis=1)[..., None]


def _attn_sgu_fwd(x, g_pre, g_post, w_in, w_out, sinks, ln_g, ln_b, sgu_w, bias_rows, tabs, tag):
    h = rms_fwd(x, g_pre, BF16, f"{tag}_prenorm")
    z = _mm(h, w_in, name=f"{tag}_in")
    qk = rope_apply(z, tabs, QK_WIDTH // LANES, False, BF16, f"{tag}_rope")
    q = _to_heads(qk[:, :ATTN_WIDTH], N_Q_HEADS)
    k = _to_heads(qk[:, ATTN_WIDTH:], N_KV_HEADS)
    v = _to_heads(z[:, QK_WIDTH:QK_WIDTH + KV_WIDTH].astype(BF16), N_KV_HEADS)
    o = swa_fwd(q, k, v, _sink_rows(sinks), f"{tag}_swa")
    u_pre = z[:, QK_WIDTH + KV_WIDTH:QK_WIDTH + KV_WIDTH + SGU_WIDTH]
    v_pre = z[:, QK_WIDTH + KV_WIDTH + SGU_WIDTH:]
    gate = sgu_fwd(u_pre, v_pre, ln_g, ln_b, sgu_w, bias_rows, f"{tag}_sgu")
    cat = jnp.concatenate([_from_heads(o), gate], axis=1)
    m = _mm(cat, w_out, name=f"{tag}_out")
    return postnorm_res(x, m, g_post, 1.0, f"{tag}_postnorm"), (x, h, q, k, v, u_pre, v_pre, cat, m)


def _attn_sgu_bwd(res, g_pre, g_post, w_in, w_out, sinks, ln_g, ln_b, sgu_w, bias_rows, tabs, dx2, tag):
    x, h, q, k, v, u_pre, v_pre, cat, m = res
    dm, dg_post = rms_bwd(m, g_post, [dx2], 1.0, None, BF16, f"{tag}_postnorm_bwd")
    dcat = _mm(dm, w_out, tb=True, out_dtype=BF16, name=f"{tag}_out_dx")
    dw_out = _mm(cat, dm, ta=True, name=f"{tag}_out_dw", out_dtype=BF16)
    do = _to_heads(dcat[:, :ATTN_WIDTH], N_Q_HEADS)
    dq, dkp, dkc, dvp, dvc, dsink = swa_bwd(q, k, v, _sink_rows(sinks), do, f"{tag}_swa_bwd")
    d_sinks = jnp.sum(dsink[..., 0], axis=1).reshape(1, N_Q_HEADS)
    dqk_rot = jnp.concatenate([_from_heads(dq).astype(F32), _fold_kv_grad(dkp, dkc)], axis=1)
    dqk = rope_apply(dqk_rot, tabs, QK_WIDTH // LANES, True, BF16, f"{tag}_rope_bwd")
    dv = _fold_kv_grad(dvp, dvc).astype(BF16)
    du_pre, dv_pre, dw_s, dbias, dlg, dlb = sgu_bwd(u_pre, v_pre, ln_g, ln_b, sgu_w, bias_rows, dcat[:, ATTN_WIDTH:], f"{tag}_sgu_bwd")
    dz = jnp.concatenate([dqk, dv, du_pre, dv_pre], axis=1)
    dw_in = _mm(h, dz, ta=True, name=f"{tag}_in_dw", out_dtype=BF16)
    dh = _mm(dz, w_in, tb=True, name=f"{tag}_in_dx")
    dx, dg_pre = rms_bwd(x, g_pre, [dh], 1.0, dx2, F32, f"{tag}_prenorm_bwd")
    causal = jnp.tril(jnp.ones((BLOCK, BLOCK), F32))
    small = dict(attn_sinks=d_sinks, sgu_ln_g=jnp.sum(dlg, axis=0, keepdims=True), sgu_ln_b=jnp.sum(dlb, axis=0, keepdims=True),
                 sgu_w=(dw_s * causal[None])[None], sgu_b=jnp.sum(dbias, axis=-1)[None])
    return dx, dg_pre, dg_post, dw_in, dw_out, small


def _pool_mix_fwd(x, g_pre, g_post, pool_w, pool_scale, tag):
    hf = rms_fwd(x, g_pre, F32, f"{tag}_prenorm")
    pooled = pool_fwd(hf, f"{tag}_pool")
    n_g = len(POOL_WINDOWS)
    ypre = _mm(pooled, pool_w, batch="map", groups=n_g, a_cb=True, o_cb=True, name=f"{tag}_proj")
    m = scale_cols(ypre, pool_scale, f"{tag}_scale")
    return postnorm_res(x, m, g_post, 1.0, f"{tag}_postnorm"), (x, pooled, ypre, m)


def _pool_mix_bwd(res, g_pre, g_post, pool_w, pool_scale, dx2, tag):
    x, pooled, ypre, m = res
    n_g = len(POOL_WINDOWS)
    dm, dg_post = rms_bwd(m, g_post, [dx2], 1.0, None, F32, f"{tag}_postnorm_bwd")
    dypre, dscale = scale_cols_bwd(dm, ypre, pool_scale, f"{tag}_scale_bwd")
    dpooled = _mm(dypre, pool_w, tb=True, batch="map", groups=n_g, a_cb=True, o_cb=True, name=f"{tag}_proj_dx")
    dpw = _mm(pooled, dypre, ta=True, batch="map", groups=n_g, a_cb=True, b_cb=True, name=f"{tag}_proj_dw", out_dtype=BF16)
    dhf = pool_bwd(dpooled, f"{tag}_pool_bwd")
    dx, dg_pre = rms_bwd(x, g_pre, [dhf], 1.0, dx2, F32, f"{tag}_prenorm_bwd")
    return dx, dg_pre, dg_post, dpw, jnp.sum(dscale, axis=0, keepdims=True)


def _xattn_fwd(x, mem, g_pre, g_post, g_mem, wq, wk, wv, wo, tag):
    h = rms_fwd(x, g_pre, BF16, f"{tag}_prenorm")
    mem_n = rms_fwd(mem, g_mem, BF16, f"{tag}_memnorm")
    q = _mm(h, wq, out_dtype=BF16, name=f"{tag}_q")
    k = _mm(mem_n, wk, out_dtype=BF16, name=f"{tag}_k")
    v = _mm(mem_n, wv, out_dtype=BF16, name=f"{tag}_v")
    o = xattn_fwd(q, k, v, f"{tag}_core")
    r = _mm(o, wo, name=f"{tag}_o")
    return postnorm_res(x, r, g_post, 1.0, f"{tag}_postnorm"), (x, h, mem_n, q, k, v, o, r)


def _xattn_bwd(res, mem, g_pre, g_post, g_mem, wq, wk, wv, wo, dx2, tag):
    x, h, mem_n, q, k, v, o, r = res
    dr, dg_post = rms_bwd(r, g_post, [dx2], 1.0, None, BF16, f"{tag}_postnorm_bwd")
    do = _mm(dr, wo, tb=True, out_dtype=BF16, name=f"{tag}_o_dx")
    dwo = _mm(o, dr, ta=True, name=f"{tag}_o_dw", out_dtype=BF16)
    dq, dk, dv = xattn_bwd(q, k, v, do, f"{tag}_core_bwd")
    dk, dv = dk.astype(BF16), dv.astype(BF16)
    dwq = _mm(h, dq, ta=True, name=f"{tag}_q_dw", out_dtype=BF16)
    dwk = _mm(mem_n, dk, ta=True, name=f"{tag}_k_dw", out_dtype=BF16)
    dwv = _mm(mem_n, dv, ta=True, name=f"{tag}_v_dw", out_dtype=BF16)
    dh = _mm(dq, wq, tb=True, name=f"{tag}_q_dx")
    dmem1 = _mm(dk, wk, tb=True, name=f"{tag}_k_dx")
    dmem2 = _mm(dv, wv, tb=True, name=f"{tag}_v_dx")
    _, dg_mem = rms_bwd(mem, g_mem, [dmem1, dmem2], 1.0, None, BF16, f"{tag}_memnorm_bwd")
    dx, dg_pre = rms_bwd(x, g_pre, [dh], 1.0, dx2, F32, f"{tag}_prenorm_bwd")
    return dx, dg_pre, dg_post, dg_mem, dwq, dwk, dwv, dwo


def _rowsum8(part):
    return jnp.sum(part, axis=0, keepdims=True)


def device_step(x, mem, target, norms, mem_norm, small, fetch, emit, early):
    n_tok = x.shape[0]
    tabs = rope_tables(n_tok)
    bias_rows = jnp.broadcast_to(small["sgu_b"][:, :, None], (SGU_GROUPS, BLOCK, LANES))
    gn = lambda l, i: norms[l, i][None, :]
    gm = lambda l: mem_norm[l][None, :]
    mix0 = (small["attn_sinks"], small["sgu_ln_g"], small["sgu_ln_b"], small["sgu_w"], bias_rows, tabs)

    wts, saved = {}, []
    for l in range(2):
        wts["ffn1", l], tok = fetch(("ffn1", l), x)
        x, r1 = _ffn_fwd(x, gn(l, 0) + tok, gn(l, 1), *wts["ffn1", l], f"l{l}_ffn1")
        wts["mix", l], tok = fetch(("mix", l), x)
        if l == 0:
            x, r2 = _attn_sgu_fwd(x, gn(l, 2) + tok, gn(l, 3), *wts["mix", l], *mix0, f"l{l}_mix")
        else:
            x, r2 = _pool_mix_fwd(x, gn(l, 2) + tok, gn(l, 3), *wts["mix", l], small["pool_scale"], f"l{l}_mix")
        wts["xattn", l], tok = fetch(("xattn", l), x)
        x, r3 = _xattn_fwd(x, mem, gn(l, 4) + tok, gn(l, 5), gm(l), *wts["xattn", l], f"l{l}_xattn")
        wts["ffn2", l], tok = fetch(("ffn2", l), x)
        x, r4 = _ffn_fwd(x, gn(l, 6) + tok, gn(l, 7), *wts["ffn2", l], f"l{l}_ffn2")
        saved.append((r1, r2, r3, r4))

    loss_part, dx = loss_and_grad(x, target, "loss")

    g_norm_rows = [[None] * 8, [None] * 8]
    g_mem_rows = [None, None]
    g_small = {}
    tok = 0.0
    for l in (1, 0):
        r1, r2, r3, r4 = saved[l]
        dx, g_norm_rows[l][6], g_norm_rows[l][7], dwg, dwu, dwd = _ffn_bwd(
            r4, gn(l, 6), gn(l, 7) + tok, dx, f"l{l}_ffn2", lambda part, g, l=l: early(("ffn2", l), f"ffn2_{part}", g))
        tok = emit(("ffn2", l), dict(ffn2_wg=dwg, ffn2_wu=dwu, ffn2_wd=dwd), dx)
        dx, g_norm_rows[l][4], g_norm_rows[l][5], g_mem_rows[l], dwq, dwk, dwv, dwo = _xattn_bwd(
            r3, mem, gn(l, 4), gn(l, 5) + tok, gm(l), *wts["xattn", l], dx, f"l{l}_xattn")
        tok = emit(("xattn", l), dict(x_wq=dwq, x_wk=dwk, x_wv=dwv, x_wo=dwo), dx)
        if l == 0:
            dx, g_norm_rows[l][2], g_norm_rows[l][3], dw_in, dw_out, sm = _attn_sgu_bwd(
                r2, gn(l, 2), gn(l, 3) + tok, *wts["mix", l], *mix0, dx, f"l{l}_mix")
            tok = emit(("mix", l), dict(mix_w_in=dw_in, mix_w_out=dw_out), dx)
            g_small.update(sm)
        else:
            dx, g_norm_rows[l][2], g_norm_rows[l][3], dpw, dscale = _pool_mix_bwd(
                r2, gn(l, 2), gn(l, 3) + tok, *wts["mix", l], small["pool_scale"], dx, f"l{l}_mix")
            tok = emit(("mix", l), dict(pool_w=dpw), dx)
            g_small["pool_scale"] = dscale
        dx, g_norm_rows[l][0], g_norm_rows[l][1], dwg, dwu, dwd = _ffn_bwd(
            r1, gn(l, 0), gn(l, 1) + tok, dx, f"l{l}_ffn1", lambda part, g, l=l: early(("ffn1", l), f"ffn1_{part}", g))
        tok = emit(("ffn1", l), dict(ffn1_wg=dwg, ffn1_wu=dwu, ffn1_wd=dwd), dx)
    g_norms = jnp.stack([jnp.concatenate([_rowsum8(p) for p in g_norm_rows[l]], axis=0) for l in range(2)])
    g_mem_norm = jnp.concatenate([_rowsum8(p) for p in g_mem_rows], axis=0)
    return loss_part, dx, g_small, g_norms, g_mem_norm, tok


ANY = pl.BlockSpec(memory_space=pl.ANY)


def _place():
    x, y, c = lax.axis_index("x"), lax.axis_index("y"), lax.axis_index("c")
    other_chips = [(1 - x, y), (x, 1 - y), (1 - x, 1 - y)]
    return x, y, c, other_chips


def _half_rows(core, n_rows):
    half = n_rows // 2
    return pl.ds(pl.multiple_of(core * half, 16), half)


def _remote(src, dst, send_sem, recv_sem, device):
    return pltpu.make_async_remote_copy(src_ref=src, dst_ref=dst, send_sem=send_sem, recv_sem=recv_sem,
                                        device_id=device, device_id_type=MESH)


HBM = pl.BlockSpec(memory_space=pltpu.HBM)
SEM = pl.BlockSpec(memory_space=pltpu.SEMAPHORE)
DATAFLOW = pltpu.SideEffectType.DATAFLOW_SIDE_EFFECTING


def _chip_copies(bufs, send_sems, recv_sems):
    x, y, c, chips = _place()
    me = 2 * x + y
    sends, arrivals = [], []
    for i, buf in enumerate(bufs):
        rows = _half_rows(c, buf.shape[1])
        for r, (px, py) in enumerate(chips):
            mine, theirs = buf.at[me, rows], buf.at[2 * px + py, rows]
            sends.append(_remote(mine, mine, send_sems.at[3 * i + r], recv_sems.at[3 * i + r], (px, py, c)))
            arrivals.append(_remote(theirs, theirs, send_sems.at[3 * i + r], recv_sems.at[3 * i + r], (px, py, c)))
    return sends, arrivals


def copies_start(arrays, fresh, copies, n_sems, after, name):
    operands = [pltpu.with_memory_space_constraint(a, pltpu.HBM) for a in arrays]
    operands += [pltpu.with_memory_space_constraint(lax.empty(f.shape, f.dtype), pltpu.HBM) for f in fresh]
    n = len(operands)

    def body(*refs):
        send_sems, recv_sems = refs[n + 1], refs[n + 2]
        bufs, token = refs[n + 3:2 * n + 3], refs[2 * n + 3]
        for cp in copies(bufs, send_sems, recv_sems)[0]:
            cp.start()
        token[...] = jnp.zeros_like(token)

    res = pl.pallas_call(
        body, name=name, in_specs=[HBM] * n + [ANY],
        out_specs=[SEM, SEM] + [HBM] * n + [pl.BlockSpec(memory_space=pltpu.VMEM)],
        out_shape=[pltpu.SemaphoreType.DMA((n_sems,)), pltpu.SemaphoreType.DMA((n_sems,))]
        + [pltpu.HBM(o.shape, o.dtype) for o in operands] + [jax.ShapeDtypeStruct((8, LANES), F32)],
        input_output_aliases={i: 2 + i for i in range(n)}, compiler_params=pltpu.CompilerParams(has_side_effects=DATAFLOW),
    )(*operands, after)
    return res[0], res[1], list(res[2:2 + n]), res[2 + n]


def copies_wait(send_sems, recv_sems, bufs, copies, after, name):
    n = len(bufs)

    def body(*refs):
        for cp in copies(refs[:n], refs[n], refs[n + 1])[1]:
            cp.wait_send()
            cp.wait_recv()

    return list(pl.pallas_call(
        body, name=name, in_specs=[HBM] * n + [SEM, SEM] + [ANY] * len(after), out_specs=[HBM] * n,
        out_shape=[pltpu.HBM(b.shape, b.dtype) for b in bufs], input_output_aliases={i: i for i in range(n)},
        compiler_params=pltpu.CompilerParams(has_side_effects=DATAFLOW),
    )(*bufs, send_sems, recv_sems, *after))


def _to_sibling_copies(n_src, src_rows):
    def copies(bufs, send_sems, recv_sems):
        x, y, c, _ = _place()
        sends, arrivals = [], []
        for i in range(n_src):
            src, land = bufs[i], bufs[n_src + i]
            src = src.at[:, src_rows(c, src)] if src_rows is not None else src
            sends.append(_remote(src, land, send_sems.at[i], recv_sems.at[i], (x, y, 1 - c)))
            arrivals.append(_remote(land, land, send_sems.at[i], recv_sems.at[i], (x, y, 1 - c)))
        return sends, arrivals
    return copies


def _forward_copies(bufs, send_sems, recv_sems):
    x, y, c, chips = _place()
    sends, arrivals = [], []
    for i, buf in enumerate(bufs):
        for r, (px, py) in enumerate(chips):
            mine = buf.at[2 * px + py, _half_rows(c, buf.shape[1])]
            theirs = buf.at[2 * px + py, _half_rows(1 - c, buf.shape[1])]
            sends.append(_remote(mine, mine, send_sems.at[3 * i + r], recv_sems.at[3 * i + r], (x, y, 1 - c)))
            arrivals.append(_remote(theirs, theirs, send_sems.at[3 * i + r], recv_sems.at[3 * i + r], (x, y, 1 - c)))
    return sends, arrivals


def forward_to_sibling(bufs, name):
    n = len(bufs)

    def body(*refs):
        sends, arrivals = _forward_copies(refs[n:2 * n], *refs[2 * n:])
        for cp in sends:
            cp.start()
        for cp in arrivals:
            cp.wait_recv()
        for cp in sends:
            cp.wait_send()

    return pl.pallas_call(
        body, name=name, in_specs=[ANY] * n, out_specs=[ANY] * n, input_output_aliases={i: i for i in range(n)},
        out_shape=[jax.ShapeDtypeStruct(b.shape, b.dtype) for b in bufs],
        scratch_shapes=[pltpu.SemaphoreType.DMA((3 * n,)), pltpu.SemaphoreType.DMA((3 * n,))],
    )(*bufs)


def add_own_half(g, p, core, name):
    n_j, n_rows, n_cols = g.shape
    half = n_rows // 2
    tr = _rows(half, n_cols * 10)

    def body(c_ref, g_ref, p_ref, o_ref):
        o_ref[...] = (g_ref[...] + p_ref[...]).astype(o_ref.dtype)

    grid_spec = pltpu.PrefetchScalarGridSpec(
        num_scalar_prefetch=1, grid=(n_j, half // tr),
        in_specs=[pl.BlockSpec((None, None, tr, n_cols), lambda j, i, c_ref: (j, c_ref[0], i, 0)),
                  pl.BlockSpec((None, tr, n_cols), lambda j, i, c_ref: (j, i, 0))],
        out_specs=pl.BlockSpec((None, tr, n_cols), lambda j, i, c_ref: (j, i, 0)))
    return pl.pallas_call(
        body, name=name, grid_spec=grid_spec, out_shape=jax.ShapeDtypeStruct((n_j, half, n_cols), BF16),
        compiler_params=_params(("parallel", "parallel")),
    )(core, g.reshape(n_j, 2, half, n_cols), p)


def _scatter_copies(n_parts):
    def copies(bufs, send_sems, recv_sems):
        x, y, c, chips = _place()
        me = 2 * x + y
        sends, arrivals = [], []
        for i in range(n_parts):
            part, land = bufs[i], bufs[n_parts + i]
            for r, (px, py) in enumerate(chips):
                theirs = land.at[2 * px + py]
                sends.append(_remote(part.at[2 * px + py], land.at[me], send_sems.at[3 * i + r], recv_sems.at[3 * i + r], (px, py, c)))
                arrivals.append(_remote(theirs, theirs, send_sems.at[3 * i + r], recv_sems.at[3 * i + r], (px, py, c)))
        return sends, arrivals
    return copies


def sum_over_chips(own, got, chip, name):
    n_s, n_rows, n_cols = got.shape
    tr = _rows(n_rows, n_cols * (got.dtype.itemsize * (n_s + 1) + 4))

    def body(chip_ref, own_ref, *refs):
        got_refs, o_ref = refs[:n_s], refs[n_s]
        me = chip_ref[0]
        acc = jnp.where(me == 0, own_ref[...], got_refs[0][...]).astype(F32)
        for k in range(1, n_s):
            acc = acc + jnp.where(me == k, own_ref[...], got_refs[k][...]).astype(F32)
        o_ref[...] = acc

    def slot(k):
        return pl.BlockSpec((None, tr, n_cols), lambda i, chip_ref: (jnp.where(chip_ref[0] == k, (k + 1) % n_s, k), i, 0))

    grid_spec = pltpu.PrefetchScalarGridSpec(
        num_scalar_prefetch=1, grid=(n_rows // tr,),
        in_specs=[pl.BlockSpec((None, tr, n_cols), lambda i, chip_ref: (chip_ref[0], i, 0))] + [slot(k) for k in range(n_s)],
        out_specs=pl.BlockSpec((tr, n_cols), lambda i, chip_ref: (i, 0)))
    return pl.pallas_call(
        body, name=name, grid_spec=grid_spec, out_shape=jax.ShapeDtypeStruct((n_rows, n_cols), F32),
        compiler_params=_params(("parallel",)),
    )(chip, own, *([got] * n_s))


def sum_slots(q, name):
    n_s, n_rows, n_cols = q.shape
    tr = _rows(n_rows, n_cols * (q.dtype.itemsize * n_s + 4))

    def body(q_ref, o_ref):
        acc = q_ref[0].astype(F32)
        for s in range(1, n_s):
            acc = acc + q_ref[s].astype(F32)
        o_ref[...] = acc

    return pl.pallas_call(
        body, name=name, grid=(n_rows // tr,), in_specs=[pl.BlockSpec((n_s, tr, n_cols), lambda i: (0, i, 0))],
        out_specs=pl.BlockSpec((tr, n_cols), lambda i: (i, 0)), out_shape=jax.ShapeDtypeStruct((n_rows, n_cols), F32),
        compiler_params=_params(("parallel",)),
    )(q)


def gather_all_devices(s, name):
    def body(s_ref, o_ref, send_sems, recv_sems, local_sem):
        x, y, c, _ = _place()
        me = 4 * x + 2 * y + c
        local = pltpu.make_async_copy(s_ref, o_ref.at[me], local_sem)
        local.start()
        sends = []
        for f in range(1, N_DEV):
            px, py, pc = x ^ (f >> 2), y ^ ((f >> 1) & 1), c ^ (f & 1)
            sends.append(_remote(s_ref, o_ref.at[me], send_sems.at[f - 1], recv_sems.at[f - 1], (px, py, pc)))
            sends[-1].start()
        for f in range(1, N_DEV):
            px, py, pc = x ^ (f >> 2), y ^ ((f >> 1) & 1), c ^ (f & 1)
            landed = o_ref.at[4 * px + 2 * py + pc]
            _remote(landed, landed, send_sems.at[f - 1], recv_sems.at[f - 1], (px, py, pc)).wait_recv()
        for cp in sends:
            cp.wait_send()
        local.wait()

    return pl.pallas_call(
        body, name=name, in_specs=[ANY], out_specs=ANY, out_shape=jax.ShapeDtypeStruct((N_DEV,) + s.shape, s.dtype),
        scratch_shapes=[pltpu.SemaphoreType.DMA((N_DEV - 1,)), pltpu.SemaphoreType.DMA((N_DEV - 1,)), pltpu.SemaphoreType.DMA],
    )(s)


FFN_NAMES = ("ffn1_wg", "ffn1_wu", "ffn1_wd", "ffn2_wg", "ffn2_wu", "ffn2_wd")
XATTN_NAMES = ("x_wq", "x_wk", "x_wv", "x_wo")
BIG_NAMES = FFN_NAMES + XATTN_NAMES + ("mix_w_in", "mix_w_out", "pool_w")
SMALL_NAMES = ("norms", "mem_norm", "attn_sinks", "sgu_ln_g", "sgu_ln_b", "sgu_w", "sgu_b", "pool_scale")
WEIGHT_ORDER = ("norms", "mem_norm") + BIG_NAMES[:-1] + ("attn_sinks", "sgu_ln_g", "sgu_ln_b", "sgu_w", "sgu_b", "pool_w", "pool_scale")
COLUMN_CUT = ("x_wo", "mix_w_in")
N_POOL = len(POOL_WINDOWS)
BLOCK_ORDER = (("ffn1", 0), ("mix", 0), ("xattn", 0), ("ffn2", 0), ("ffn1", 1), ("mix", 1), ("xattn", 1), ("ffn2", 1))
PREFETCH_AT = ((1, 2, 3), (), (), (4,), (5, 6, 7), (), (), ())


def block_weight_names(kind, layer):
    if kind == "mix":
        return ("mix_w_in", "mix_w_out") if layer == 0 else ("pool_w",)
    return XATTN_NAMES if kind == "xattn" else tuple(f"{kind}_{part}" for part in ("wg", "wu", "wd"))


def _to_matmul_layout(name, g):
    n_j, n_rows, n_cols = g.shape
    if name in COLUMN_CUT:
        return g.transpose(1, 0, 2).reshape(n_rows, n_j * n_cols)
    if name == "pool_w":
        return g.reshape(n_j, N_POOL, n_rows // N_POOL, n_cols).transpose(1, 0, 2, 3).reshape(N_POOL, n_j * n_rows // N_POOL, n_cols)
    if name in ("x_wq", "x_wk", "x_wv", "mix_w_out"):
        return g.reshape(n_j * n_rows, n_cols)
    return g


def _from_matmul_layout(name, d):
    if name in COLUMN_CUT:
        n_rows, wide = d.shape
        return d.reshape(n_rows, N_CHIPS, wide // N_CHIPS).transpose(1, 0, 2)
    if name == "pool_w":
        n_g, n_in, n_cols = d.shape
        return d.reshape(n_g, N_CHIPS, n_in // N_CHIPS, n_cols).transpose(1, 0, 2, 3).reshape(N_CHIPS, n_g * n_in // N_CHIPS, n_cols)
    if name in ("x_wq", "x_wk", "x_wv", "mix_w_out"):
        return d.reshape(N_CHIPS, d.shape[0] // N_CHIPS, d.shape[1])
    return d


def _as3(w):
    return w.reshape(w.shape[0], -1, w.shape[-1])


def _pack(arrays, row_multiple):
    flat = jnp.concatenate([a.reshape(-1) for a in arrays])
    per = LANES * row_multiple
    total = -(-flat.shape[0] // per) * per
    return jnp.pad(flat, (0, total - flat.shape[0])).reshape(total // LANES, LANES)


def _unpack(packed, like):
    flat, out, at = packed.reshape(-1), [], 0
    for a in like:
        out.append(flat[at:at + a.size].reshape(a.shape))
        at += a.size
    return out


def kernel(x, mem, norms, mem_norm, ffn1_wg, ffn1_wu, ffn1_wd, ffn2_wg, ffn2_wu, ffn2_wd, x_wq, x_wk, x_wv, x_wo, mix_w_in, mix_w_out, attn_sinks, sgu_ln_g, sgu_ln_b, sgu_w, sgu_b, pool_w, pool_scale, loss_target, m_norms, m_mem_norm, m_ffn1_wg, m_ffn1_wu, m_ffn1_wd, m_ffn2_wg, m_ffn2_wu, m_ffn2_wd, m_x_wq, m_x_wk, m_x_wv, m_x_wo, m_mix_w_in, m_mix_w_out, m_attn_sinks, m_sgu_ln_g, m_sgu_ln_b, m_sgu_w, m_sgu_b, m_pool_w, m_pool_scale, v_norms, v_mem_norm, v_ffn1_wg, v_ffn1_wu, v_ffn1_wd, v_ffn2_wg, v_ffn2_wu, v_ffn2_wd, v_x_wq, v_x_wk, v_x_wv, v_x_wo, v_mix_w_in, v_mix_w_out, v_attn_sinks, v_sgu_ln_g, v_sgu_ln_b, v_sgu_w, v_sgu_b, v_pool_w, v_pool_scale):
    given = dict(locals())
    w = {n: given[n] for n in WEIGHT_ORDER}
    mom = {n: given["m_" + n] for n in WEIGHT_ORDER}
    var = {n: given["v_" + n] for n in WEIGHT_ORDER}
    chip_id = 2 * lax.axis_index("x") + lax.axis_index("y")
    chip = chip_id.astype(jnp.int32).reshape(1)
    core = lax.axis_index("c").astype(jnp.int32).reshape(1)
    n_shard = norms.shape[-1]

    keys = [(name, l) for name in BIG_NAMES for l in range(_as3(w[name]).shape[0])]
    first_keys = [(name, 0) for name in block_weight_names(*BLOCK_ORDER[0])]
    cast = lambda name, l: cast_into_slot(_as3(w[name]), l, chip, f"cast_{name}")
    slot_of = {key: cast(*key) for key in first_keys}
    small_rows = jnp.concatenate([norms.reshape(-1, n_shard), pool_scale, jnp.zeros((15, n_shard), F32)], axis=0)
    small_slot = lax.dynamic_update_slice_in_dim(jnp.zeros((N_CHIPS,) + small_rows.shape, F32), small_rows[None], chip_id, axis=0)
    pending = {}
    gather_start = lambda slots, after, name: copies_start(slots, [], _chip_copies, 3 * len(slots), after, name)
    pending["0a"] = gather_start([slot_of[first_keys[0]], small_slot], chip, "gather_start_0a")
    pending["0b"] = gather_start([slot_of[first_keys[1]]], pending["0a"][2][0], "gather_start_0b")
    pending["0c"] = gather_start([slot_of[first_keys[2]]], pending["0b"][2][0], "gather_start_0c")
    for key in keys:
        if key not in slot_of:
            slot_of[key] = cast(*key)


    def block_slots(k):
        kind, layer = BLOCK_ORDER[k]
        return [slot_of[name, layer if kind != "mix" else 0] for name in block_weight_names(kind, layer)]

    def start(k, after):
        pending[k] = gather_start(block_slots(k), after, f"gather_start_{k}")

    def finish(k, after):
        send_sems, recv_sems, bufs, _ = pending[k]
        return forward_to_sibling(copies_wait(send_sems, recv_sems, bufs, _chip_copies, [after], f"gather_wait_{k}"), f"gather_forward_{k}")

    def fetch(key, after):
        k = BLOCK_ORDER.index(key)
        names = block_weight_names(*key)
        if k == 0:
            late = lambda part, name: lambda after: _to_matmul_layout(name, finish(part, after)[0])
            bufs = [first_gate]
            weights = (_to_matmul_layout(names[0], first_gate), late("0b", names[1]), late("0c", names[2]))
        elif key[0] in ("ffn1", "ffn2"):
            send_sems, recv_sems, bufs, _ = pending[k]
            bufs = copies_wait(send_sems, recv_sems, bufs, _chip_copies, [after], f"gather_wait_{k}")
            gate, = forward_to_sibling(bufs[:1], f"gather_forward_{k}_gate")
            send_sems, recv_sems, rest, _ = copies_start(bufs[1:], [], _forward_copies, 6, gate, f"gather_forward_{k}_start")
            landed = []

            def late(i, name):
                def get(after):
                    if not landed:
                        landed.extend(copies_wait(send_sems, recv_sems, rest, _forward_copies, [after], f"gather_forward_{k}_wait"))
                    return _to_matmul_layout(name, landed[i])
                return get
            bufs = [gate]
            weights = (_to_matmul_layout(names[0], gate), late(0, names[1]), late(1, names[2]))
        else:
            bufs = finish(k, after)
            weights = tuple(_to_matmul_layout(name, b) for name, b in zip(names, bufs))
        token = 0.0
        for ahead in PREFETCH_AT[k]:
            start(ahead, bufs[0])
            token = token + pending[ahead][3][0, 0]
        return weights, token

    to_sibling, to_chips, halves, own_of, recv_of = [], [], [], {}, {}
    other_half = lambda c, src: _half_rows(1 - c, src.shape[1])

    def sibling_start(tag, names, full, after):
        lands = [jax.ShapeDtypeStruct((g.shape[0], g.shape[1] // 2, g.shape[2]), g.dtype) for g in full]
        send_sems, recv_sems, bufs, token = copies_start(full, lands, _to_sibling_copies(len(full), other_half), len(full), after,
                                                         f"grads_to_sibling_{tag}")
        to_sibling.append((tag, names, send_sems, recv_sems, bufs))
        return token

    def early(key, name, g):
        kind, layer = key
        return sibling_start(f"{name}{layer}", [(name, layer)], [_from_matmul_layout(name, g)], chip)

    def chips_start(tag, after):
        names, chip_sums = [], []
        for part, part_names, send_sems, recv_sems, bufs in to_sibling:
            n = len(part_names)
            bufs = copies_wait(send_sems, recv_sems, bufs, _to_sibling_copies(n, other_half), after, f"grads_from_sibling_{part}")
            chip_sums += [add_own_half(g, p, core, f"chip_sum_{name}") for (name, _), g, p in zip(part_names, bufs[:n], bufs[n:])]
            names += part_names
        to_sibling.clear()
        n = len(names)
        send_sems, recv_sems, bufs, token = copies_start(chip_sums, chip_sums, _scatter_copies(n), 3 * n, after[0], f"grads_start_{tag}")
        to_chips.append((tag, names, send_sems, recv_sems, bufs))
        return token[0, 0]

    def chips_wait(after):
        tag, names, send_sems, recv_sems, bufs = to_chips.pop()
        return tag, names, copies_wait(send_sems, recv_sems, bufs, _scatter_copies(len(names)), after, f"grads_wait_{tag}")

    def reduce_landed(tag, names, bufs):
        n = len(names)
        own = [sum_over_chips(t, q, chip, f"sum_{name}") for (name, _), t, q in zip(names, bufs[:n], bufs[n:])]
        send_sems, recv_sems, bufs, _ = copies_start(own, own, _to_sibling_copies(n, None), n, chip, f"grads_halves_{tag}")
        halves.append((tag, names, send_sems, recv_sems, bufs))

    def halves_land(after):
        for tag, names, send_sems, recv_sems, bufs in halves:
            n = len(names)
            bufs = copies_wait(send_sems, recv_sems, bufs, _to_sibling_copies(n, None), after, f"grads_halves_wait_{tag}")
            for key, o, r in zip(names, bufs[:n], bufs[n:]):
                own_of[key], recv_of[key] = o, r
        halves.clear()

    def emit(key, grads_of, after):
        kind, layer = key
        landed = chips_wait([after]) if to_chips else None
        begun = {name for _, part_names, _, _, _ in to_sibling for name, _ in part_names}
        rest = {name: g for name, g in grads_of.items() if name not in begun}
        if rest:
            sibling_start(f"{kind}{layer}", [(name, layer if kind != "mix" else 0) for name in rest],
                          [_from_matmul_layout(name, g) for name, g in rest.items()], after)
        token = chips_start(f"{kind}{layer}", [after])
        if landed:
            reduce_landed(*landed)
        return token

    first_gate, small_all = finish("0a", pending["0c"][2][0])
    n_norm_rows = norms.shape[0] * norms.shape[1]
    norms_all = jnp.concatenate([small_all[j, :n_norm_rows].reshape(norms.shape) for j in range(N_CHIPS)], axis=-1)
    pool_scale_all = jnp.concatenate([small_all[j, n_norm_rows:n_norm_rows + 1] for j in range(N_CHIPS)], axis=-1)
    small = dict(attn_sinks=attn_sinks[0], sgu_ln_g=sgu_ln_g, sgu_ln_b=sgu_ln_b, sgu_w=sgu_w[0], sgu_b=sgu_b[0], pool_scale=pool_scale_all)

    loss_part, dx, g_small, g_norms, g_mem_norm, last_token = device_step(
        x[0], mem[0], loss_target[0], norms_all, mem_norm, small, fetch, emit, early)
    loss = lax.psum(0.5 * jnp.sum(loss_part) / x.shape[-1], ("x", "y", "c"))
    last_names = {name for name, _ in to_chips[0][1]}
    order = last_token.reshape(1, 1)
    halves_land([dx])

    grads, delta, new_m, new_v = {}, {}, {}, {}

    def update(name):
        n_l = _as3(w[name]).shape[0]
        res = adamw_from_halves(_as3(w[name]), [own_of[name, l] for l in range(n_l)], [recv_of[name, l] for l in range(n_l)],
                                _as3(mom[name]), _as3(var[name]), core, order, f"adamw_{name}")
        grads[name], delta[name], new_m[name], new_v[name] = (t.reshape(w[name].shape) for t in res)

    for name in BIG_NAMES:
        if name not in last_names:
            update(name)
    done = [delta[name] for name in BIG_NAMES if name not in last_names]
    reduce_landed(*chips_wait(done))
    halves_land(done)
    for name in BIG_NAMES:
        if name in last_names:
            update(name)

    small_g = [g_norms, g_mem_norm, g_small["attn_sinks"], g_small["sgu_ln_g"], g_small["sgu_ln_b"], g_small["sgu_w"], g_small["sgu_b"],
               g_small["pool_scale"]]
    packed = _pack(small_g, 16)
    summed = sum_slots(gather_all_devices(packed, "small_grads_all"), "small_grads_sum")
    s_norms, s_mem, s_sinks, s_lg, s_lb, s_w, s_b, s_scale = _unpack(summed, small_g)
    grads["norms"] = lax.dynamic_slice_in_dim(s_norms, chip_id * n_shard, n_shard, axis=2)
    grads["pool_scale"] = lax.dynamic_slice_in_dim(s_scale, chip_id * n_shard, n_shard, axis=1)
    grads.update(mem_norm=s_mem, attn_sinks=s_sinks, sgu_ln_g=s_lg, sgu_ln_b=s_lb, sgu_w=s_w, sgu_b=s_b)
    like = [w[n] for n in SMALL_NAMES]
    packs = [_pack([src[n] for n in SMALL_NAMES], 16)[None] for src in (w, grads, mom, var)]
    for dst, t in zip((delta, new_m, new_v), adamw(*packs, "adamw_small")):
        for n, a in zip(SMALL_NAMES, _unpack(t[0], like)):
            dst[n] = a

    outs = [loss, dx[None]]
    for group in (grads, delta, new_m, new_v):
        outs += [group[n] for n in WEIGHT_ORDER]
    return tuple(outs)
```

```python
import functools

import jax
import jax.numpy as jnp
from jax import lax
from jax.experimental import pallas as pl
from jax.experimental.pallas import tpu as pltpu

F32 = jnp.float32
BF16 = jnp.bfloat16
MESH = pl.DeviceIdType.MESH

EPS = 1e-6
ROPE_THETA = 500000.0
ROPE_HALF = 8
HEAD_DIM = 64
N_Q_HEADS = 16
N_KV_HEADS = 2
Q_PER_KV = 8
BLOCK = 128
ATTN_WIDTH = 1024
KV_WIDTH = 128
QK_WIDTH = ATTN_WIDTH + KV_WIDTH
SGU_WIDTH = 1024
SGU_GROUPS = 8
POOL_WINDOWS = (2, 4, 8, 16)
POOL_HALO = 16
X_HEADS = 4
X_HEAD_DIM = 128
N_CHIPS = 4
N_DEV = 8

ADAM_LR = 0.001
ADAM_B1 = 0.9
ADAM_B2 = 0.999
ADAM_EPS = 1e-08
ADAM_WD = 0.01
ADAM_STEP = 10

VMEM_LIMIT_V7X = 52 * 1024 * 1024
MM_VMEM_BUDGET = 44 * 1024 * 1024
LANES = 128
ROW_TILE_BYTES = 6 * 1024 * 1024
ADAM_TILE_BYTES = 12 * 1024 * 1024
MXU_FLOPS_V7X = 1.0e15
HBM_BYTES_PER_S_V7X = 3.0e12
VMEM_STORE_BYTES_PER_S = 4.0e12
MXU_WEIGHT_LOAD_ROWS = 192
MXU_NARROW_COLS = 128
GRID_STEP_S = 0.35e-6


def _params(sem):
    return pltpu.CompilerParams(dimension_semantics=sem, vmem_limit_bytes=VMEM_LIMIT_V7X)


def _pick(dim, pref, align):
    cands = [t for t in range(align, dim + 1, align) if dim % t == 0]
    small = [t for t in cands if t <= pref]
    if small and small[-1] * 2 >= min(pref, dim):
        return small[-1]
    return dim


def _rows(n_rows, bytes_per_row, tile_bytes=ROW_TILE_BYTES):
    want = max(16, min(1024, tile_bytes // max(1, bytes_per_row)))
    cands = [t for t in range(16, n_rows + 1, 16) if n_rows % t == 0 and t <= want]
    return cands[-1] if cands else n_rows


def _divisors(dim, align, most):
    return [t for t in range(align, min(dim, most) + 1, align) if dim % t == 0] or [dim]


def _b_stays(reduce, n_tiles, k_tiles):
    return not reduce and n_tiles == 1 and k_tiles == 1


def _mm_tiles(M, N, K, J, m_align, k_align, a_bytes, b_bytes, o_bytes, reduce, ta, products=1):
    best = None
    for tm in _divisors(M, m_align, 2048):
        for tn in _divisors(N, LANES, 2048):
            for tk in _divisors(K, k_align, 4096):
                split = K // tk > 1 or reduce
                b_copies = 1 if _b_stays(reduce, N // tn, K // tk) else 2
                vmem = 2 * (tm * tk * a_bytes + tm * tn * o_bytes) + b_copies * tk * tn * b_bytes
                vmem += tm * tn * 4 * products * (2 if split else 1)
                if ta:
                    vmem += tm * tk * a_bytes
                if vmem > MM_VMEM_BUDGET:
                    continue
                steps = J * (M // tm) * (N // tn) * (K // tk)
                mxu = 2.0 * J * M * N * K / MXU_FLOPS_V7X * (tm + MXU_WEIGHT_LOAD_ROWS) / tm * (tn + MXU_NARROW_COLS) / tn
                acc = J * M * N * (K // tk) * 8 / VMEM_STORE_BYTES_PER_S if split else 0.0
                hbm = J * (M * K * a_bytes * (N // tn) + K * N * b_bytes * (M // tm) + M * N * o_bytes) / HBM_BYTES_PER_S_V7X
                cost = max(mxu + 0.5 * acc, hbm) + steps * GRID_STEP_S
                if best is None or cost < best[0]:
                    best = (cost, tm, tn, tk)
    return best[1:]


def _rsum8(v):
    r, c = v.shape
    return v.reshape(r // 8, 8, c).sum(axis=0)


def _mm(a, b, *, name, ta=False, tb=False, batch="none", groups=0, a_cb=False, b_cb=False, o_cb=False,
        out_dtype=F32, more=(), extras=(), epilogue=None, out_dtypes=None, order=None):
    J = groups or (a.shape[0] if a.ndim == 3 else (b.shape[0] if b.ndim == 3 else 1))
    a2, b2 = a.shape[-2:], b.shape[-2:]
    M, K = (a2[1], a2[0]) if ta else a2
    N, Kb = b2 if tb else (b2[1], b2[0])
    if a_cb:
        if ta:
            M //= J
        else:
            K //= J
    if b_cb:
        if tb:
            Kb //= J
        else:
            N //= J
    assert K == Kb, (name, a.shape, b.shape)
    reduce = batch == "reduce"
    out_dtypes = list(out_dtypes or [out_dtype])
    n_terms = 1 + len(more)
    o_bytes = sum(jnp.dtype(d).itemsize for d in out_dtypes) + sum(e.dtype.itemsize for e in extras)
    n_prod = n_terms if epilogue is not None else 1
    tm, tn, tk = _mm_tiles(M, N, K, J, LANES if ta else 16, LANES if (not ta or tb) else 16, a.dtype.itemsize * n_terms,
                           b.dtype.itemsize * n_terms, o_bytes, reduce, ta, n_prod)
    nm, nn, nk = M // tm, N // tn, K // tk
    if reduce:
        grid = (nm, nn, J, nk)
        unpack = lambda m, n, j, k: (j, m, n, k)
        sem = ("parallel", "parallel", "arbitrary", "arbitrary")
    else:
        grid = (J, nm, nn, nk)
        unpack = lambda j, m, n, k: (j, m, n, k)
        sem = ("parallel", "parallel", "parallel", "arbitrary")

    def a_map(*g):
        j, m, n, k = unpack(*g)
        r, c = (k, m) if ta else (m, k)
        if a_cb:
            c = c + j * (nm if ta else nk)
        return (j, r, c) if a.ndim == 3 else (r, c)

    def b_map(*g):
        j, m, n, k = unpack(*g)
        r, c = (n, k) if tb else (k, n)
        if b_cb:
            c = c + j * (nk if tb else nn)
        return (j, r, c) if b.ndim == 3 else (r, c)

    def o_map(*g):
        j, m, n, k = unpack(*g)
        if o_cb:
            return (m, n + j * nn)
        return (j, m, n) if batch == "map" else (m, n)

    a_blk = (tk, tm) if ta else (tm, tk)
    b_blk = (tn, tk) if tb else (tk, tn)
    a_spec = pl.BlockSpec(((None,) + a_blk) if a.ndim == 3 else a_blk, a_map)
    b_spec = pl.BlockSpec(((None,) + b_blk) if b.ndim == 3 else b_blk, b_map,
                          pipeline_mode=pl.Buffered(1) if _b_stays(reduce, nn, nk) else None)
    if o_cb:
        out_shape, o_blk = (M, N * J), (tm, tn)
    elif batch == "map":
        out_shape, o_blk = (J, M, N), (None, tm, tn)
    else:
        out_shape, o_blk = (M, N), (tm, tn)
    o_spec = pl.BlockSpec(o_blk, o_map)
    dims = (((0 if ta else 1,), (1 if tb else 0,)), ((), ()))
    red_axes = (2, 3) if reduce else (3,)
    split = reduce or nk > 1
    n_ex, n_out = len(extras), len(out_dtypes)
    n_ord = 0 if order is None else 1

    def body(*refs):
        refs = refs[n_ord:]
        ab_refs, ex_refs = refs[:2 * n_terms], refs[2 * n_terms:2 * n_terms + n_ex]
        o_refs, acc = refs[2 * n_terms + n_ex:2 * n_terms + n_ex + n_out], refs[2 * n_terms + n_ex + n_out:]
        prods = [lax.dot_general(ab_refs[2 * t][...], ab_refs[2 * t + 1][...], dims, preferred_element_type=F32) for t in range(n_terms)]
        if epilogue is None:
            prods = [functools.reduce(lambda p, q: p + q, prods)]

        def finish(vals):
            outs = epilogue(vals, [e[...] for e in ex_refs]) if epilogue is not None else vals
            for o_ref, val in zip(o_refs, outs):
                o_ref[...] = val.astype(o_ref.dtype)

        if not split:
            finish(prods)
            return
        first = functools.reduce(jnp.logical_and, [pl.program_id(ax) == 0 for ax in red_axes])
        last = functools.reduce(jnp.logical_and, [pl.program_id(ax) == grid[ax] - 1 for ax in red_axes])

        @pl.when(first)
        def _():
            for acc_ref, prod in zip(acc, prods):
                acc_ref[...] = prod

        @pl.when(jnp.logical_not(first))
        def _():
            for acc_ref, prod in zip(acc, prods):
                acc_ref[...] += prod

        @pl.when(last)
        def _():
            finish([acc_ref[...] for acc_ref in acc])

    operands = [order] * n_ord + [a, b] + [t for pair in more for t in pair] + list(extras)
    res = pl.pallas_call(
        body, name=name, grid=grid, out_specs=[o_spec] * n_out,
        in_specs=[pl.BlockSpec(memory_space=pl.ANY)] * n_ord + [a_spec, b_spec] * n_terms + [o_spec] * n_ex,
        out_shape=[jax.ShapeDtypeStruct(out_shape, d) for d in out_dtypes],
        scratch_shapes=[pltpu.VMEM((tm, tn), F32)] * (n_prod if split else 0), compiler_params=_params(sem),
    )(*operands)
    return res if epilogue is not None else res[0]


def _rowwise(fn, tiled, whole, outs, accs, *, name):
    n_rows = tiled[0].shape[0]
    row_bytes = sum(t.shape[1] * t.dtype.itemsize for t in tiled) + sum(c * jnp.dtype(d).itemsize for c, d in outs)
    tr = _rows(n_rows, row_bytes)
    n_t, n_w, n_o = len(tiled), len(whole), len(outs)

    def body(*refs):
        i = pl.program_id(0)
        t_refs, w_refs = refs[:n_t], refs[n_t:n_t + n_w]
        o_refs, a_refs = refs[n_t + n_w:n_t + n_w + n_o], refs[n_t + n_w + n_o:]
        o_vals, a_vals = fn(i, *[r[...] for r in t_refs], *[r[...] for r in w_refs])
        for r, v in zip(o_refs, o_vals):
            r[...] = v.astype(r.dtype)
        if a_refs:
            @pl.when(i == 0)
            def _():
                for r in a_refs:
                    r[...] = jnp.zeros_like(r)
            for r, v in zip(a_refs, a_vals):
                r[...] += v

    in_specs = [pl.BlockSpec((tr, t.shape[1]), lambda i: (i, 0)) for t in tiled]
    in_specs += [pl.BlockSpec(w.shape, lambda i, nd=w.ndim: (0,) * nd) for w in whole]
    out_specs = [pl.BlockSpec((tr, c), lambda i: (i, 0)) for c, _ in outs]
    out_specs += [pl.BlockSpec(s, lambda i, nd=len(s): (0,) * nd) for s, _ in accs]
    out_shape = [jax.ShapeDtypeStruct((n_rows, c), d) for c, d in outs]
    out_shape += [jax.ShapeDtypeStruct(s, d) for s, d in accs]
    res = pl.pallas_call(
        body, name=name, grid=(n_rows // tr,), in_specs=in_specs, out_specs=out_specs, out_shape=out_shape,
        compiler_params=_params(("arbitrary",) if accs else ("parallel",)),
    )(*tiled, *whole)
    return res


def _rms_stats(x):
    r = lax.rsqrt(jnp.mean(x * x, axis=-1, keepdims=True) + EPS)
    return x * r, r


def rms_fwd(x, g, out_dtype, name):
    def fn(i, x, g):
        xhat, _ = _rms_stats(x)
        return [xhat * g], []
    return _rowwise(fn, [x], [g], [(x.shape[1], out_dtype)], [], name=name)[0]


def postnorm_res(x, y, g, s, name):
    def fn(i, x, y, g):
        yhat, _ = _rms_stats(y)
        return [x + s * (yhat * g)], []
    return _rowwise(fn, [x, y], [g], [(x.shape[1], F32)], [], name=name)[0]


def rms_bwd(xin, g, douts, scale, add, out_dtype, name):
    n_d = len(douts)

    def fn(i, x, *rest):
        ds, rest = rest[:n_d], rest[n_d:]
        ad = rest[0] if add is not None else None
        g = rest[-1]
        xhat, r = _rms_stats(x)
        d = ds[0].astype(F32)
        for e in ds[1:]:
            d = d + e.astype(F32)
        if scale != 1.0:
            d = d * scale
        dg = _rsum8(d * xhat)
        dxhat = d * g
        dx = r * (dxhat - xhat * jnp.mean(dxhat * xhat, axis=-1, keepdims=True))
        if ad is not None:
            dx = dx + ad
        return [dx], [dg]

    tiled = [xin, *douts] + ([add] if add is not None else [])
    dx, dg = _rowwise(fn, tiled, [g], [(xin.shape[1], out_dtype)], [((8, xin.shape[1]), F32)], name=name)
    return dx, dg


def _silu_parts(g):
    sg = 1.0 / (1.0 + jnp.exp(-g))
    return g * sg, sg


def _swiglu_tiles(products, saved):
    u, = products
    return [u, _silu_parts(saved[0].astype(F32))[0] * u]


def _swiglu_bwd_tiles(products, saved):
    da, = products
    g, u = saved[0].astype(F32), saved[1].astype(F32)
    s, sg = _silu_parts(g)
    return [da * u * (sg * (1.0 + g * (1.0 - sg))), da * s]


def scale_cols(y, s, name):
    def fn(i, y, s):
        return [y * s], []
    return _rowwise(fn, [y], [s], [(y.shape[1], F32)], [], name=name)[0]


def scale_cols_bwd(dm, y, s, name):
    def fn(i, dm, y, s):
        return [dm * s], [_rsum8(dm * y)]
    return _rowwise(fn, [dm, y], [s], [(y.shape[1], BF16)], [((8, y.shape[1]), F32)], name=name)


def loss_and_grad(y, target, name):
    n_feat = y.shape[1]

    def fn(i, y, t):
        e = y - t
        return [e * (1.0 / n_feat)], [_rsum8(e * e)]
    dy, part = _rowwise(fn, [y, target], [], [(n_feat, F32)], [((8, n_feat), F32)], name=name)
    return part, dy


def cast_into_slot(w3, layer, chip, name):
    _, n_rows, n_cols = w3.shape
    tr = _rows(n_rows, n_cols * 6)

    def body(chip_ref, w_ref, o_ref):
        o_ref[...] = w_ref[...].astype(BF16)

    grid_spec = pltpu.PrefetchScalarGridSpec(
        num_scalar_prefetch=1, grid=(n_rows // tr,),
        in_specs=[pl.BlockSpec((None, tr, n_cols), lambda i, chip_ref: (layer, i, 0))],
        out_specs=pl.BlockSpec((None, tr, n_cols), lambda i, chip_ref: (chip_ref[0], i, 0)))
    return pl.pallas_call(
        body, name=name, grid_spec=grid_spec, out_shape=jax.ShapeDtypeStruct((N_CHIPS, n_rows, n_cols), BF16),
        compiler_params=_params(("parallel",)),
    )(chip, w3)


def _adam_update(w, g, m, v):
    c1 = 1.0 / (1.0 - ADAM_B1 ** ADAM_STEP)
    c2 = 1.0 / (1.0 - ADAM_B2 ** ADAM_STEP)
    m = ADAM_B1 * m + (1.0 - ADAM_B1) * g
    v = ADAM_B2 * v + (1.0 - ADAM_B2) * (g * g)
    return -ADAM_LR * ((m * c1) / (jnp.sqrt(v * c2) + ADAM_EPS) + ADAM_WD * w), m, v


def adamw(w, g, m, v, name):
    n_l, n_rows, n_cols = w.shape
    tr = _rows(n_rows, n_cols * 4 * 7)

    def body(w_ref, g_ref, m_ref, v_ref, d_ref, mo_ref, vo_ref):
        d_ref[...], mo_ref[...], vo_ref[...] = _adam_update(w_ref[...], g_ref[...], m_ref[...], v_ref[...])

    spec = pl.BlockSpec((None, tr, n_cols), lambda l, i: (l, i, 0))
    shp = jax.ShapeDtypeStruct(w.shape, F32)
    return pl.pallas_call(
        body, name=name, grid=(n_l, n_rows // tr), in_specs=[spec] * 4, out_specs=[spec] * 3, out_shape=[shp] * 3,
        compiler_params=_params(("parallel", "parallel")),
    )(w, g, m, v)


def adamw_from_halves(w, own, recv, m, v, core, order, name):
    n_l, n_rows, n_cols = w.shape
    half = n_rows // 2
    tr = _rows(half, n_cols * 4 * 9, ADAM_TILE_BYTES)
    per = half // tr

    def body(core_ref, *refs):
        w_ref, m_ref, v_ref, order_ref = refs[:4]
        own_refs, recv_refs = refs[4:4 + n_l], refs[4 + n_l:4 + 2 * n_l]
        g_ref, d_ref, mo_ref, vo_ref = refs[4 + 2 * n_l:]
        l, h = pl.program_id(0), pl.program_id(1)
        mine = h == core_ref[0]
        g = jnp.where(mine, own_refs[0][...], recv_refs[0][...])
        for k in range(1, n_l):
            g = jnp.where(l == k, jnp.where(mine, own_refs[k][...], recv_refs[k][...]), g)
        g = g + order_ref[...]
        g_ref[...] = g
        d_ref[...], mo_ref[...], vo_ref[...] = _adam_update(w_ref[...], g, m_ref[...], v_ref[...])

    full = pl.BlockSpec((None, tr, n_cols), lambda l, h, i, core_ref: (l, h * per + i, 0))

    def piece(layer, is_own):
        def index(l, h, i, core_ref):
            used = (l == layer) & ((h == core_ref[0]) == is_own)
            return (jnp.where(used, i, 0), 0)
        return pl.BlockSpec((tr, n_cols), index)

    grid_spec = pltpu.PrefetchScalarGridSpec(
        num_scalar_prefetch=1, grid=(n_l, 2, per),
        in_specs=[full] * 3 + [pl.BlockSpec((1, 1), lambda l, h, i, core_ref: (0, 0))]
        + [piece(k, True) for k in range(n_l)] + [piece(k, False) for k in range(n_l)],
        out_specs=[full] * 4)
    return pl.pallas_call(
        body, name=name, grid_spec=grid_spec, out_shape=[jax.ShapeDtypeStruct(w.shape, F32)] * 4,
        compiler_params=_params(("parallel", "parallel", "parallel")),
    )(core, w, m, v, order, *own, *recv)


def rope_tables(n_tok):
    inv = ROPE_THETA ** (-jnp.arange(ROPE_HALF, dtype=F32) * 2.0 / (2 * ROPE_HALF))
    ang = jnp.arange(n_tok, dtype=F32)[:, None] * inv[None, :]
    cos, sin = jnp.cos(ang), jnp.sin(ang)
    rest = HEAD_DIM - 2 * ROPE_HALF
    one, zero, z8 = jnp.ones((n_tok, rest), F32), jnp.zeros((n_tok, rest), F32), jnp.zeros((n_tok, ROPE_HALF), F32)
    c = jnp.concatenate([cos, cos, one], axis=1)
    s1 = jnp.concatenate([-sin, z8, zero], axis=1)
    s2 = jnp.concatenate([z8, sin, zero], axis=1)
    two = lambda t: jnp.concatenate([t, t], axis=1)
    return two(c), two(s1), two(s2)


def rope_apply(x, tabs, n_col_blocks, inverse, out_dtype, name):
    n_tok = x.shape[0]
    tr = _rows(n_tok, LANES * 4 * 6)

    def body(x_ref, c_ref, s1_ref, s2_ref, o_ref):
        x = x_ref[...].astype(F32)
        if inverse:
            out = x * c_ref[...] + pltpu.roll(x * s1_ref[...], ROPE_HALF, 1) + pltpu.roll(x * s2_ref[...], LANES - ROPE_HALF, 1)
        else:
            out = x * c_ref[...] + pltpu.roll(x, LANES - ROPE_HALF, 1) * s1_ref[...] + pltpu.roll(x, ROPE_HALF, 1) * s2_ref[...]
        o_ref[...] = out.astype(o_ref.dtype)

    tab_spec = pl.BlockSpec((tr, LANES), lambda i, c: (i, 0))
    blk = pl.BlockSpec((tr, LANES), lambda i, c: (i, c))
    return pl.pallas_call(
        body, name=name, grid=(n_tok // tr, n_col_blocks), in_specs=[blk, tab_spec, tab_spec, tab_spec], out_specs=blk,
        out_shape=jax.ShapeDtypeStruct((n_tok, n_col_blocks * LANES), out_dtype), compiler_params=_params(("parallel", "parallel")),
    )(x, *tabs)


def _swa_probs(q, k, sink, n):
    rows = Q_PER_KV * BLOCK
    s = lax.dot_general(q, k, (((1,), (1,)), ((), ())), preferred_element_type=F32) * (HEAD_DIM ** -0.5)
    qi = lax.broadcasted_iota(jnp.int32, (rows, 2 * BLOCK), 0) & (BLOCK - 1)
    kj = lax.broadcasted_iota(jnp.int32, (rows, 2 * BLOCK), 1)
    rel = qi + BLOCK - kj
    valid = (rel >= 0) & (rel < BLOCK) & ((n > 0) | (kj >= BLOCK))
    s = jnp.where(valid, s, -1e30)
    m = jnp.maximum(jnp.max(s, axis=-1, keepdims=True), sink)
    e = jnp.exp(s - m)
    es = jnp.exp(sink - m)
    inv = 1.0 / (jnp.sum(e, axis=-1, keepdims=True) + es)
    return e * inv, es * inv


def _swa_specs(n_blocks):
    q_spec = pl.BlockSpec((Q_PER_KV, BLOCK, HEAD_DIM), lambda h, n: (h, n, 0))
    prev = pl.BlockSpec((None, BLOCK, HEAD_DIM), lambda h, n: (h, jnp.maximum(n - 1, 0), 0))
    cur = pl.BlockSpec((None, BLOCK, HEAD_DIM), lambda h, n: (h, n, 0))
    sink = pl.BlockSpec((None, Q_PER_KV * BLOCK, 1), lambda h, n: (h, 0, 0))
    return q_spec, prev, cur, sink


def swa_fwd(q, k, v, sink_rows, name):
    n_tok = q.shape[1]
    q_spec, prev, cur, sink = _swa_specs(n_tok // BLOCK)

    def body(q_ref, kp_ref, kc_ref, vp_ref, vc_ref, s_ref, o_ref):
        n = pl.program_id(1)
        qq = q_ref[...].reshape(Q_PER_KV * BLOCK, HEAD_DIM)
        kk = jnp.concatenate([kp_ref[...], kc_ref[...]], axis=0)
        vv = jnp.concatenate([vp_ref[...], vc_ref[...]], axis=0)
        p, _ = _swa_probs(qq, kk, s_ref[...], n)
        o = jnp.dot(p.astype(BF16), vv, preferred_element_type=F32)
        o_ref[...] = o.reshape(Q_PER_KV, BLOCK, HEAD_DIM).astype(o_ref.dtype)

    return pl.pallas_call(
        body, name=name, grid=(N_KV_HEADS, n_tok // BLOCK), in_specs=[q_spec, prev, cur, prev, cur, sink], out_specs=q_spec,
        out_shape=jax.ShapeDtypeStruct(q.shape, BF16), compiler_params=_params(("parallel", "parallel")),
    )(q, k, k, v, v, sink_rows)


def swa_bwd(q, k, v, sink_rows, do, name):
    n_tok = q.shape[1]
    nb = n_tok // BLOCK
    q_spec, prev, cur, sink = _swa_specs(nb)
    rows = Q_PER_KV * BLOCK

    def body(q_ref, kp_ref, kc_ref, vp_ref, vc_ref, s_ref, do_ref, dq_ref, dkp_ref, dkc_ref, dvp_ref, dvc_ref, ds_ref):
        n = pl.program_id(1)
        qq = q_ref[...].reshape(rows, HEAD_DIM)
        dd = do_ref[...].reshape(rows, HEAD_DIM)
        kk = jnp.concatenate([kp_ref[...], kc_ref[...]], axis=0)
        vv = jnp.concatenate([vp_ref[...], vc_ref[...]], axis=0)
        p, ps = _swa_probs(qq, kk, s_ref[...], n)
        dp = lax.dot_general(dd, vv, (((1,), (1,)), ((), ())), preferred_element_type=F32)
        delta = jnp.sum(p * dp, axis=-1, keepdims=True)
        ds = (p * (dp - delta) * (HEAD_DIM ** -0.5)).astype(BF16)
        dq = jnp.dot(ds, kk, preferred_element_type=F32)
        dk = lax.dot_general(ds, qq, (((0,), (0,)), ((), ())), preferred_element_type=F32)
        dv = lax.dot_general(p.astype(BF16), dd, (((0,), (0,)), ((), ())), preferred_element_type=F32)
        dq_ref[...] = dq.reshape(Q_PER_KV, BLOCK, HEAD_DIM).astype(dq_ref.dtype)
        dkp_ref[...] = dk[:BLOCK]
        dkc_ref[...] = dk[BLOCK:]
        dvp_ref[...] = dv[:BLOCK]
        dvc_ref[...] = dv[BLOCK:]
        dsink = jnp.broadcast_to(-ps * delta, (rows, LANES)).reshape(Q_PER_KV, BLOCK, LANES)
        ds_ref[...] = jnp.sum(dsink, axis=1)

    part = pl.BlockSpec((None, None, BLOCK, HEAD_DIM), lambda h, n: (h, n, 0, 0))
    part_shape = jax.ShapeDtypeStruct((N_KV_HEADS, nb, BLOCK, HEAD_DIM), F32)
    return pl.pallas_call(
        body, name=name, grid=(N_KV_HEADS, nb), in_specs=[q_spec, prev, cur, prev, cur, sink, q_spec],
        out_specs=[q_spec, part, part, part, part, pl.BlockSpec((None, None, Q_PER_KV, LANES), lambda h, n: (h, n, 0, 0))],
        out_shape=[jax.ShapeDtypeStruct(q.shape, BF16), part_shape, part_shape, part_shape, part_shape,
                   jax.ShapeDtypeStruct((N_KV_HEADS, nb, Q_PER_KV, LANES), F32)],
        compiler_params=_params(("parallel", "parallel")),
    )(q, k, k, v, v, sink_rows, do)


def _to_heads(t, n_heads):
    return t.reshape(t.shape[0], n_heads, HEAD_DIM).transpose(1, 0, 2)


def _from_heads(t):
    return t.transpose(1, 0, 2).reshape(t.shape[1], -1)


def _fold_kv_grad(prev, cur):
    shifted = jnp.concatenate([prev[:, 1:], jnp.zeros_like(prev[:, :1])], axis=1)
    tot = (cur + shifted).reshape(N_KV_HEADS, -1, HEAD_DIM)
    return _from_heads(tot)


def _x_probs(qh, kh):
    s = lax.dot_general(qh, kh, (((1,), (1,)), ((), ())), preferred_element_type=F32) * (X_HEAD_DIM ** -0.5)
    e = jnp.exp(s - jnp.max(s, axis=-1, keepdims=True))
    return e * (1.0 / jnp.sum(e, axis=-1, keepdims=True))


def xattn_fwd(q, k, v, name):
    n_tok, width = q.shape
    n_mem = k.shape[0]
    tq = _pick(n_tok, 512, 16)

    def body(q_ref, k_ref, v_ref, o_ref):
        for h in range(X_HEADS):
            cols = slice(h * X_HEAD_DIM, (h + 1) * X_HEAD_DIM)
            p = _x_probs(q_ref[:, cols], k_ref[:, cols])
            o_ref[:, cols] = jnp.dot(p.astype(BF16), v_ref[:, cols], preferred_element_type=F32).astype(o_ref.dtype)

    row = pl.BlockSpec((tq, width), lambda i: (i, 0))
    mem = pl.BlockSpec((n_mem, width), lambda i: (0, 0))
    return pl.pallas_call(
        body, name=name, grid=(n_tok // tq,), in_specs=[row, mem, mem], out_specs=row,
        out_shape=jax.ShapeDtypeStruct(q.shape, BF16), compiler_params=_params(("parallel",)),
    )(q, k, v)


def xattn_bwd(q, k, v, do, name):
    n_tok, width = q.shape
    n_mem = k.shape[0]
    tq = _pick(n_tok, 512, 16)

    def body(q_ref, k_ref, v_ref, do_ref, dq_ref, dk_ref, dv_ref):
        @pl.when(pl.program_id(0) == 0)
        def _():
            dk_ref[...] = jnp.zeros_like(dk_ref)
            dv_ref[...] = jnp.zeros_like(dv_ref)

        for h in range(X_HEADS):
            cols = slice(h * X_HEAD_DIM, (h + 1) * X_HEAD_DIM)
            qh, kh, vh, dh = q_ref[:, cols], k_ref[:, cols], v_ref[:, cols], do_ref[:, cols]
            p = _x_probs(qh, kh)
            dp = lax.dot_general(dh, vh, (((1,), (1,)), ((), ())), preferred_element_type=F32)
            delta = jnp.sum(p * dp, axis=-1, keepdims=True)
            ds = (p * (dp - delta) * (X_HEAD_DIM ** -0.5)).astype(BF16)
            dq_ref[:, cols] = jnp.dot(ds, kh, preferred_element_type=F32).astype(dq_ref.dtype)
            dk_ref[:, cols] += lax.dot_general(ds, qh, (((0,), (0,)), ((), ())), preferred_element_type=F32)
            dv_ref[:, cols] += lax.dot_general(p.astype(BF16), dh, (((0,), (0,)), ((), ())), preferred_element_type=F32)

    row = pl.BlockSpec((tq, width), lambda i: (i, 0))
    mem = pl.BlockSpec((n_mem, width), lambda i: (0, 0))
    return pl.pallas_call(
        body, name=name, grid=(n_tok // tq,), in_specs=[row, mem, mem, row], out_specs=[row, mem, mem],
        out_shape=[jax.ShapeDtypeStruct(q.shape, BF16), jax.ShapeDtypeStruct(k.shape, F32), jax.ShapeDtypeStruct(k.shape, F32)],
        compiler_params=_params(("arbitrary",)),
    )(q, k, v, do)


GELU_C = 0.7978845608028654
GELU_A = 0.044715


def _gelu_parts(x):
    x2 = x * x
    t = jnp.tanh(GELU_C * x * (1.0 + GELU_A * x2))
    y = 0.5 * x * (1.0 + t)
    dy = 0.5 * (1.0 + t) + 0.5 * x * (1.0 - t * t) * GELU_C * (1.0 + 3.0 * GELU_A * x2)
    return y, dy


def _sgu_norm(v, ln_g, ln_b):
    mu = jnp.mean(v, axis=-1, keepdims=True)
    vc = v - mu
    r = lax.rsqrt(jnp.mean(vc * vc, axis=-1, keepdims=True) + EPS)
    xhat = vc * r
    return xhat * ln_g + ln_b, xhat, r


def _causal_weights(w_ref):
    i = lax.broadcasted_iota(jnp.int32, (BLOCK, BLOCK), 0)
    j = lax.broadcasted_iota(jnp.int32, (BLOCK, BLOCK), 1)
    return [jnp.where(i >= j, w_ref[g], 0.0).astype(BF16) for g in range(SGU_GROUPS)]


def sgu_fwd(u_pre, v_pre, ln_g, ln_b, w_s, bias_rows, name):
    n_tok = u_pre.shape[0]
    tm = _pick(n_tok, 512, BLOCK)

    def body(u_ref, v_ref, g_ref, b_ref, w_ref, bb_ref, o_ref):
        vn, _, _ = _sgu_norm(_gelu_parts(v_ref[...])[0], g_ref[...], b_ref[...])
        vn = vn.astype(BF16)
        wc = _causal_weights(w_ref)
        for c in range(tm // BLOCK):
            rows = slice(c * BLOCK, (c + 1) * BLOCK)
            for g in range(SGU_GROUPS):
                cols = slice(g * LANES, (g + 1) * LANES)
                mixed = jnp.dot(wc[g], vn[rows, cols], preferred_element_type=F32) + bb_ref[g]
                u = _gelu_parts(u_ref[rows, cols])[0]
                o_ref[rows, cols] = (u * mixed).astype(o_ref.dtype)

    row = pl.BlockSpec((tm, SGU_WIDTH), lambda i: (i, 0))
    vec = pl.BlockSpec((1, SGU_WIDTH), lambda i: (0, 0))
    mat = pl.BlockSpec((SGU_GROUPS, BLOCK, LANES), lambda i: (0, 0, 0))
    return pl.pallas_call(
        body, name=name, grid=(n_tok // tm,), in_specs=[row, row, vec, vec, mat, mat], out_specs=row,
        out_shape=jax.ShapeDtypeStruct((n_tok, SGU_WIDTH), BF16), compiler_params=_params(("parallel",)),
    )(u_pre, v_pre, ln_g, ln_b, w_s, bias_rows)


def sgu_bwd(u_pre, v_pre, ln_g, ln_b, w_s, bias_rows, dgate, name):
    n_tok = u_pre.shape[0]
    tm = _pick(n_tok, 512, BLOCK)

    def body(u_ref, v_ref, g_ref, b_ref, w_ref, bb_ref, dg_ref, du_ref, dv_ref, dw_ref, db_ref, dlg_ref, dlb_ref, dvn_ref):
        @pl.when(pl.program_id(0) == 0)
        def _():
            dw_ref[...] = jnp.zeros_like(dw_ref)
            db_ref[...] = jnp.zeros_like(db_ref)
            dlg_ref[...] = jnp.zeros_like(dlg_ref)
            dlb_ref[...] = jnp.zeros_like(dlb_ref)

        gv, dgv = _gelu_parts(v_ref[...])
        vn, xhat, r = _sgu_norm(gv, g_ref[...], b_ref[...])
        vn = vn.astype(BF16)
        wc = _causal_weights(w_ref)
        for c in range(tm // BLOCK):
            rows = slice(c * BLOCK, (c + 1) * BLOCK)
            for g in range(SGU_GROUPS):
                cols = slice(g * LANES, (g + 1) * LANES)
                vt = vn[rows, cols]
                mixed = jnp.dot(wc[g], vt, preferred_element_type=F32) + bb_ref[g]
                u, du_dpre = _gelu_parts(u_ref[rows, cols])
                dgate_t = dg_ref[rows, cols].astype(F32)
                du_ref[rows, cols] = (dgate_t * mixed * du_dpre).astype(du_ref.dtype)
                dmix = dgate_t * u
                dmix_b = dmix.astype(BF16)
                db_ref[g] += dmix
                dw_ref[g] += lax.dot_general(dmix_b, vt, (((1,), (1,)), ((), ())), preferred_element_type=F32)
                dvn_ref[rows, cols] = lax.dot_general(wc[g], dmix_b, (((0,), (0,)), ((), ())), preferred_element_type=F32)
        dvn = dvn_ref[...]
        dlg_ref[...] += _rsum8(dvn * xhat)
        dlb_ref[...] += _rsum8(dvn)
        dxhat = dvn * g_ref[...]
        dgv_in = r * (dxhat - jnp.mean(dxhat, axis=-1, keepdims=True) - xhat * jnp.mean(dxhat * xhat, axis=-1, keepdims=True))
        dv_ref[...] = (dgv_in * dgv).astype(dv_ref.dtype)

    row = pl.BlockSpec((tm, SGU_WIDTH), lambda i: (i, 0))
    vec = pl.BlockSpec((1, SGU_WIDTH), lambda i: (0, 0))
    mat = pl.BlockSpec((SGU_GROUPS, BLOCK, LANES), lambda i: (0, 0, 0))
    part = pl.BlockSpec((8, SGU_WIDTH), lambda i: (0, 0))
    mat_shape = jax.ShapeDtypeStruct((SGU_GROUPS, BLOCK, LANES), F32)
    part_shape = jax.ShapeDtypeStruct((8, SGU_WIDTH), F32)
    act_shape = jax.ShapeDtypeStruct((n_tok, SGU_WIDTH), BF16)
    return pl.pallas_call(
        body, name=name, grid=(n_tok // tm,), in_specs=[row, row, vec, vec, mat, mat, row],
        out_specs=[row, row, mat, mat, part, part], out_shape=[act_shape, act_shape, mat_shape, mat_shape, part_shape, part_shape],
        scratch_shapes=[pltpu.VMEM((tm, SGU_WIDTH), F32)], compiler_params=_params(("arbitrary",)),
    )(u_pre, v_pre, ln_g, ln_b, w_s, bias_rows, dgate)


def _pool_tile(n_tok):
    return _pick(n_tok, 256, POOL_HALO)


def pool_fwd(h, name):
    n_tok, width = h.shape
    gw = width // len(POOL_WINDOWS)
    tm = _pool_tile(n_tok)
    per = tm // POOL_HALO

    def body(cur_ref, halo_ref, o_ref, buf_ref):
        i = pl.program_id(0)
        buf_ref[0:POOL_HALO, :] = jnp.where(i > 0, halo_ref[...], 0.0)
        buf_ref[POOL_HALO:, :] = cur_ref[...]
        tok = i * tm + lax.broadcasted_iota(jnp.int32, (tm, 1), 0)
        for g, w in enumerate(POOL_WINDOWS):
            cols = slice(g * gw, (g + 1) * gw)
            acc = buf_ref[POOL_HALO:, cols]
            for j in range(1, w):
                acc = acc + buf_ref[POOL_HALO - j:POOL_HALO - j + tm, cols]
            cnt = jnp.minimum(tok + 1, w).astype(F32)
            o_ref[:, cols] = (acc / cnt - cur_ref[:, cols]).astype(o_ref.dtype)

    return pl.pallas_call(
        body, name=name, grid=(n_tok // tm,),
        in_specs=[pl.BlockSpec((tm, width), lambda i: (i, 0)),
                  pl.BlockSpec((POOL_HALO, width), lambda i: (jnp.maximum(i * per - 1, 0), 0))],
        out_specs=pl.BlockSpec((tm, width), lambda i: (i, 0)), out_shape=jax.ShapeDtypeStruct(h.shape, BF16),
        scratch_shapes=[pltpu.VMEM((tm + POOL_HALO, width), F32)], compiler_params=_params(("parallel",)),
    )(h, h)


def pool_bwd(dp, name):
    n_tok, width = dp.shape
    gw = width // len(POOL_WINDOWS)
    tm = _pool_tile(n_tok)
    per = tm // POOL_HALO
    n_steps = n_tok // tm

    def body(cur_ref, halo_ref, o_ref, buf_ref):
        i = pl.program_id(0)
        tok = i * tm + lax.broadcasted_iota(jnp.int32, (tm, 1), 0)
        for g, w in enumerate(POOL_WINDOWS):
            cols = slice(g * gw, (g + 1) * gw)
            cnt = jnp.minimum(tok + 1, w).astype(F32)
            buf_ref[0:tm, cols] = cur_ref[:, cols] / cnt
            buf_ref[tm:, cols] = jnp.where(i < n_steps - 1, halo_ref[:, cols] / float(w), 0.0)
        for g, w in enumerate(POOL_WINDOWS):
            cols = slice(g * gw, (g + 1) * gw)
            acc = buf_ref[0:tm, cols]
            for j in range(1, w):
                acc = acc + buf_ref[j:j + tm, cols]
            o_ref[:, cols] = acc - cur_ref[:, cols]

    return pl.pallas_call(
        body, name=name, grid=(n_steps,),
        in_specs=[pl.BlockSpec((tm, width), lambda i: (i, 0)),
                  pl.BlockSpec((POOL_HALO, width), lambda i: (jnp.minimum((i + 1) * per, n_tok // POOL_HALO - 1), 0))],
        out_specs=pl.BlockSpec((tm, width), lambda i: (i, 0)), out_shape=jax.ShapeDtypeStruct(dp.shape, F32),
        scratch_shapes=[pltpu.VMEM((tm + POOL_HALO, width), F32)], compiler_params=_params(("parallel",)),
    )(dp, dp)


def _ffn_fwd(x, ga, gb, wg, wu, wd, tag):
    h = rms_fwd(x, ga, BF16, f"{tag}_prenorm")
    G = _mm(h, wg, batch="map", out_dtype=BF16, name=f"{tag}_gate")
    wu = wu(G) if callable(wu) else wu
    U, A = _mm(h, wu, batch="map", extras=[G], epilogue=_swiglu_tiles, out_dtypes=[BF16] * 2, name=f"{tag}_up")
    wd = wd(A) if callable(wd) else wd
    y = _mm(A, wd, batch="reduce", name=f"{tag}_down")
    return postnorm_res(x, y, gb, 0.5, f"{tag}_postnorm"), (x, h, G, U, A, y, wg, wu, wd)


def _ffn_bwd(res, ga, gb, dx2, tag, early):
    x, h, G, U, A, y, wg, wu, wd = res
    dy, dgb = rms_bwd(y, gb, [dx2], 0.5, None, BF16, f"{tag}_postnorm_bwd")
    dG, dU = _mm(dy, wd, tb=True, batch="map", extras=[G, U], epilogue=_swiglu_bwd_tiles, out_dtypes=[BF16] * 2, name=f"{tag}_down_dx")
    dwd = _mm(A, dy, ta=True, batch="map", name=f"{tag}_down_dw", out_dtype=BF16)
    dwg = _mm(h, dG, ta=True, batch="map", name=f"{tag}_gate_dw", out_dtype=BF16, order=early("wd", dwd))
    dwu = _mm(h, dU, ta=True, batch="map", name=f"{tag}_up_dw", out_dtype=BF16, order=early("wg", dwg))
    dh = _mm(dG, wg, tb=True, batch="reduce", more=[(dU, wu)], name=f"{tag}_gate_up_dx", order=early("wu", dwu))
    dx, dga = rms_bwd(x, ga, [dh], 1.0, dx2, F32, f"{tag}_prenorm_bwd")
    return dx, dga, dgb, dwg, dwu, dwd


def _sink_rows(sinks):
    return jnp.repeat(sinks.reshape(N_KV_HEADS, Q_PER_KV), BLOCK, axis=1)[..., None]


def _attn_sgu_fwd(x, g_pre, g_post, w_in, w_out, sinks, ln_g, ln_b, sgu_w, bias_rows, tabs, tag):
    h = rms_fwd(x, g_pre, BF16, f"{tag}_prenorm")
    z = _mm(h, w_in, name=f"{tag}_in")
    qk = rope_apply(z, tabs, QK_WIDTH // LANES, False, BF16, f"{tag}_rope")
    q = _to_heads(qk[:, :ATTN_WIDTH], N_Q_HEADS)
    k = _to_heads(qk[:, ATTN_WIDTH:], N_KV_HEADS)
    v = _to_heads(z[:, QK_WIDTH:QK_WIDTH + KV_WIDTH].astype(BF16), N_KV_HEADS)
    o = swa_fwd(q, k, v, _sink_rows(sinks), f"{tag}_swa")
    u_pre = z[:, QK_WIDTH + KV_WIDTH:QK_WIDTH + KV_WIDTH + SGU_WIDTH]
    v_pre = z[:, QK_WIDTH + KV_WIDTH + SGU_WIDTH:]
    gate = sgu_fwd(u_pre, v_pre, ln_g, ln_b, sgu_w, bias_rows, f"{tag}_sgu")
    cat = jnp.concatenate([_from_heads(o), gate], axis=1)
    m = _mm(cat, w_out, name=f"{tag}_out")
    return postnorm_res(x, m, g_post, 1.0, f"{tag}_postnorm"), (x, h, q, k, v, u_pre, v_pre, cat, m)


def _attn_sgu_bwd(res, g_pre, g_post, w_in, w_out, sinks, ln_g, ln_b, sgu_w, bias_rows, tabs, dx2, tag):
    x, h, q, k, v, u_pre, v_pre, cat, m = res
    dm, dg_post = rms_bwd(m, g_post, [dx2], 1.0, None, BF16, f"{tag}_postnorm_bwd")
    dcat = _mm(dm, w_out, tb=True, out_dtype=BF16, name=f"{tag}_out_dx")
    dw_out = _mm(cat, dm, ta=True, name=f"{tag}_out_dw", out_dtype=BF16)
    do = _to_heads(dcat[:, :ATTN_WIDTH], N_Q_HEADS)
    dq, dkp, dkc, dvp, dvc, dsink = swa_bwd(q, k, v, _sink_rows(sinks), do, f"{tag}_swa_bwd")
    d_sinks = jnp.sum(dsink[..., 0], axis=1).reshape(1, N_Q_HEADS)
    dqk_rot = jnp.concatenate([_from_heads(dq).astype(F32), _fold_kv_grad(dkp, dkc)], axis=1)
    dqk = rope_apply(dqk_rot, tabs, QK_WIDTH // LANES, True, BF16, f"{tag}_rope_bwd")
    dv = _fold_kv_grad(dvp, dvc).astype(BF16)
    du_pre, dv_pre, dw_s, dbias, dlg, dlb = sgu_bwd(u_pre, v_pre, ln_g, ln_b, sgu_w, bias_rows, dcat[:, ATTN_WIDTH:], f"{tag}_sgu_bwd")
    dz = jnp.concatenate([dqk, dv, du_pre, dv_pre], axis=1)
    dw_in = _mm(h, dz, ta=True, name=f"{tag}_in_dw", out_dtype=BF16)
    dh = _mm(dz, w_in, tb=True, name=f"{tag}_in_dx")
    dx, dg_pre = rms_bwd(x, g_pre, [dh], 1.0, dx2, F32, f"{tag}_prenorm_bwd")
    causal = jnp.tril(jnp.ones((BLOCK, BLOCK), F32))
    small = dict(attn_sinks=d_sinks, sgu_ln_g=jnp.sum(dlg, axis=0, keepdims=True), sgu_ln_b=jnp.sum(dlb, axis=0, keepdims=True),
                 sgu_w=(dw_s * causal[None])[None], sgu_b=jnp.sum(dbias, axis=-1)[None])
    return dx, dg_pre, dg_post, dw_in, dw_out, small


def _pool_mix_fwd(x, g_pre, g_post, pool_w, pool_scale, tag):
    hf = rms_fwd(x, g_pre, F32, f"{tag}_prenorm")
    pooled = pool_fwd(hf, f"{tag}_pool")
    n_g = len(POOL_WINDOWS)
    ypre = _mm(pooled, pool_w, batch="map", groups=n_g, a_cb=True, o_cb=True, name=f"{tag}_proj")
    m = scale_cols(ypre, pool_scale, f"{tag}_scale")
    return postnorm_res(x, m, g_post, 1.0, f"{tag}_postnorm"), (x, pooled, ypre, m)


def _pool_mix_bwd(res, g_pre, g_post, pool_w, pool_scale, dx2, tag):
    x, pooled, ypre, m = res
    n_g = len(POOL_WINDOWS)
    dm, dg_post = rms_bwd(m, g_post, [dx2], 1.0, None, F32, f"{tag}_postnorm_bwd")
    dypre, dscale = scale_cols_bwd(dm, ypre, pool_scale, f"{tag}_scale_bwd")
    dpooled = _mm(dypre, pool_w, tb=True, batch="map", groups=n_g, a_cb=True, o_cb=True, name=f"{tag}_proj_dx")
    dpw = _mm(pooled, dypre, ta=True, batch="map", groups=n_g, a_cb=True, b_cb=True, name=f"{tag}_proj_dw", out_dtype=BF16)
    dhf = pool_bwd(dpooled, f"{tag}_pool_bwd")
    dx, dg_pre = rms_bwd(x, g_pre, [dhf], 1.0, dx2, F32, f"{tag}_prenorm_bwd")
    return dx, dg_pre, dg_post, dpw, jnp.sum(dscale, axis=0, keepdims=True)


def _xattn_fwd(x, mem, g_pre, g_post, g_mem, wq, wk, wv, wo, tag):
    h = rms_fwd(x, g_pre, BF16, f"{tag}_prenorm")
    mem_n = rms_fwd(mem, g_mem, BF16, f"{tag}_memnorm")
    q = _mm(h, wq, out_dtype=BF16, name=f"{tag}_q")
    k = _mm(mem_n, wk, out_dtype=BF16, name=f"{tag}_k")
    v = _mm(mem_n, wv, out_dtype=BF16, name=f"{tag}_v")
    o = xattn_fwd(q, k, v, f"{tag}_core")
    r = _mm(o, wo, name=f"{tag}_o")
    return postnorm_res(x, r, g_post, 1.0, f"{tag}_postnorm"), (x, h, mem_n, q, k, v, o, r)


def _xattn_bwd(res, mem, g_pre, g_post, g_mem, wq, wk, wv, wo, dx2, tag):
    x, h, mem_n, q, k, v, o, r = res
    dr, dg_post = rms_bwd(r, g_post, [dx2], 1.0, None, BF16, f"{tag}_postnorm_bwd")
    do = _mm(dr, wo, tb=True, out_dtype=BF16, name=f"{tag}_o_dx")
    dwo = _mm(o, dr, ta=True, name=f"{tag}_o_dw", out_dtype=BF16)
    dq, dk, dv = xattn_bwd(q, k, v, do, f"{tag}_core_bwd")
    dk, dv = dk.astype(BF16), dv.astype(BF16)
    dwq = _mm(h, dq, ta=True, name=f"{tag}_q_dw", out_dtype=BF16)
    dwk = _mm(mem_n, dk, ta=True, name=f"{tag}_k_dw", out_dtype=BF16)
    dwv = _mm(mem_n, dv, ta=True, name=f"{tag}_v_dw", out_dtype=BF16)
    dh = _mm(dq, wq, tb=True, name=f"{tag}_q_dx")
    dmem1 = _mm(dk, wk, tb=True, name=f"{tag}_k_dx")
    dmem2 = _mm(dv, wv, tb=True, name=f"{tag}_v_dx")
    _, dg_mem = rms_bwd(mem, g_mem, [dmem1, dmem2], 1.0, None, BF16, f"{tag}_memnorm_bwd")
    dx, dg_pre = rms_bwd(x, g_pre, [dh], 1.0, dx2, F32, f"{tag}_prenorm_bwd")
    return dx, dg_pre, dg_post, dg_mem, dwq, dwk, dwv, dwo


def _rowsum8(part):
    return jnp.sum(part, axis=0, keepdims=True)


def device_step(x, mem, target, norms, mem_norm, small, fetch, emit, early):
    n_tok = x.shape[0]
    tabs = rope_tables(n_tok)
    bias_rows = jnp.broadcast_to(small["sgu_b"][:, :, None], (SGU_GROUPS, BLOCK, LANES))
    gn = lambda l, i: norms[l, i][None, :]
    gm = lambda l: mem_norm[l][None, :]
    mix0 = (small["attn_sinks"], small["sgu_ln_g"], small["sgu_ln_b"], small["sgu_w"], bias_rows, tabs)

    wts, saved = {}, []
    for l in range(2):
        wts["ffn1", l], tok = fetch(("ffn1", l), x)
        x, r1 = _ffn_fwd(x, gn(l, 0) + tok, gn(l, 1), *wts["ffn1", l], f"l{l}_ffn1")
        wts["mix", l], tok = fetch(("mix", l), x)
        if l == 0:
            x, r2 = _attn_sgu_fwd(x, gn(l, 2) + tok, gn(l, 3), *wts["mix", l], *mix0, f"l{l}_mix")
        else:
            x, r2 = _pool_mix_fwd(x, gn(l, 2) + tok, gn(l, 3), *wts["mix", l], small["pool_scale"], f"l{l}_mix")
        wts["xattn", l], tok = fetch(("xattn", l), x)
        x, r3 = _xattn_fwd(x, mem, gn(l, 4) + tok, gn(l, 5), gm(l), *wts["xattn", l], f"l{l}_xattn")
        wts["ffn2", l], tok = fetch(("ffn2", l), x)
        x, r4 = _ffn_fwd(x, gn(l, 6) + tok, gn(l, 7), *wts["ffn2", l], f"l{l}_ffn2")
        saved.append((r1, r2, r3, r4))

    loss_part, dx = loss_and_grad(x, target, "loss")

    g_norm_rows = [[None] * 8, [None] * 8]
    g_mem_rows = [None, None]
    g_small = {}
    tok = 0.0
    for l in (1, 0):
        r1, r2, r3, r4 = saved[l]
        dx, g_norm_rows[l][6], g_norm_rows[l][7], dwg, dwu, dwd = _ffn_bwd(
            r4, gn(l, 6), gn(l, 7) + tok, dx, f"l{l}_ffn2", lambda part, g, l=l: early(("ffn2", l), f"ffn2_{part}", g))
        tok = emit(("ffn2", l), dict(ffn2_wg=dwg, ffn2_wu=dwu, ffn2_wd=dwd), dx)
        dx, g_norm_rows[l][4], g_norm_rows[l][5], g_mem_rows[l], dwq, dwk, dwv, dwo = _xattn_bwd(
            r3, mem, gn(l, 4), gn(l, 5) + tok, gm(l), *wts["xattn", l], dx, f"l{l}_xattn")
        tok = emit(("xattn", l), dict(x_wq=dwq, x_wk=dwk, x_wv=dwv, x_wo=dwo), dx)
        if l == 0:
            dx, g_norm_rows[l][2], g_norm_rows[l][3], dw_in, dw_out, sm = _attn_sgu_bwd(
                r2, gn(l, 2), gn(l, 3) + tok, *wts["mix", l], *mix0, dx, f"l{l}_mix")
            tok = emit(("mix", l), dict(mix_w_in=dw_in, mix_w_out=dw_out), dx)
            g_small.update(sm)
        else:
            dx, g_norm_rows[l][2], g_norm_rows[l][3], dpw, dscale = _pool_mix_bwd(
                r2, gn(l, 2), gn(l, 3) + tok, *wts["mix", l], small["pool_scale"], dx, f"l{l}_mix")
            tok = emit(("mix", l), dict(pool_w=dpw), dx)
            g_small["pool_scale"] = dscale
        dx, g_norm_rows[l][0], g_norm_rows[l][1], dwg, dwu, dwd = _ffn_bwd(
            r1, gn(l, 0), gn(l, 1) + tok, dx, f"l{l}_ffn1", lambda part, g, l=l: early(("ffn1", l), f"ffn1_{part}", g))
        tok = emit(("ffn1", l), dict(ffn1_wg=dwg, ffn1_wu=dwu, ffn1_wd=dwd), dx)
    g_norms = jnp.stack([jnp.concatenate([_rowsum8(p) for p in g_norm_rows[l]], axis=0) for l in range(2)])
    g_mem_norm = jnp.concatenate([_rowsum8(p) for p in g_mem_rows], axis=0)
    return loss_part, dx, g_small, g_norms, g_mem_norm, tok


ANY = pl.BlockSpec(memory_space=pl.ANY)


def _place():
    x, y, c = lax.axis_index("x"), lax.axis_index("y"), lax.axis_index("c")
    other_chips = [(1 - x, y), (x, 1 - y), (1 - x, 1 - y)]
    return x, y, c, other_chips


def _half_rows(core, n_rows):
    half = n_rows // 2
    return pl.ds(pl.multiple_of(core * half, 16), half)


def _remote(src, dst, send_sem, recv_sem, device):
    return pltpu.make_async_remote_copy(src_ref=src, dst_ref=dst, send_sem=send_sem, recv_sem=recv_sem,
                                        device_id=device, device_id_type=MESH)


HBM = pl.BlockSpec(memory_space=pltpu.HBM)
SEM = pl.BlockSpec(memory_space=pltpu.SEMAPHORE)
DATAFLOW = pltpu.SideEffectType.DATAFLOW_SIDE_EFFECTING


def _chip_copies(bufs, send_sems, recv_sems):
    x, y, c, chips = _place()
    me = 2 * x + y
    sends, arrivals = [], []
    for i, buf in enumerate(bufs):
        rows = _half_rows(c, buf.shape[1])
        for r, (px, py) in enumerate(chips):
            mine, theirs = buf.at[me, rows], buf.at[2 * px + py, rows]
            sends.append(_remote(mine, mine, send_sems.at[3 * i + r], recv_sems.at[3 * i + r], (px, py, c)))
            arrivals.append(_remote(theirs, theirs, send_sems.at[3 * i + r], recv_sems.at[3 * i + r], (px, py, c)))
    return sends, arrivals


def copies_start(arrays, fresh, copies, n_sems, after, name):
    operands = [pltpu.with_memory_space_constraint(a, pltpu.HBM) for a in arrays]
    operands += [pltpu.with_memory_space_constraint(lax.empty(f.shape, f.dtype), pltpu.HBM) for f in fresh]
    n = len(operands)

    def body(*refs):
        send_sems, recv_sems = refs[n + 1], refs[n + 2]
        bufs, token = refs[n + 3:2 * n + 3], refs[2 * n + 3]
        for cp in copies(bufs, send_sems, recv_sems)[0]:
            cp.start()
        token[...] = jnp.zeros_like(token)

    res = pl.pallas_call(
        body, name=name, in_specs=[HBM] * n + [ANY],
        out_specs=[SEM, SEM] + [HBM] * n + [pl.BlockSpec(memory_space=pltpu.VMEM)],
        out_shape=[pltpu.SemaphoreType.DMA((n_sems,)), pltpu.SemaphoreType.DMA((n_sems,))]
        + [pltpu.HBM(o.shape, o.dtype) for o in operands] + [jax.ShapeDtypeStruct((8, LANES), F32)],
        input_output_aliases={i: 2 + i for i in range(n)}, compiler_params=pltpu.CompilerParams(has_side_effects=DATAFLOW),
    )(*operands, after)
    return res[0], res[1], list(res[2:2 + n]), res[2 + n]


def copies_wait(send_sems, recv_sems, bufs, copies, after, name):
    n = len(bufs)

    def body(*refs):
        for cp in copies(refs[:n], refs[n], refs[n + 1])[1]:
            cp.wait_send()
            cp.wait_recv()

    return list(pl.pallas_call(
        body, name=name, in_specs=[HBM] * n + [SEM, SEM] + [ANY] * len(after), out_specs=[HBM] * n,
        out_shape=[pltpu.HBM(b.shape, b.dtype) for b in bufs], input_output_aliases={i: i for i in range(n)},
        compiler_params=pltpu.CompilerParams(has_side_effects=DATAFLOW),
    )(*bufs, send_sems, recv_sems, *after))


def _to_sibling_copies(n_src, src_rows):
    def copies(bufs, send_sems, recv_sems):
        x, y, c, _ = _place()
        sends, arrivals = [], []
        for i in range(n_src):
            src, land = bufs[i], bufs[n_src + i]
            src = src.at[:, src_rows(c, src)] if src_rows is not None else src
            sends.append(_remote(src, land, send_sems.at[i], recv_sems.at[i], (x, y, 1 - c)))
            arrivals.append(_remote(land, land, send_sems.at[i], recv_sems.at[i], (x, y, 1 - c)))
        return sends, arrivals
    return copies


def _forward_copies(bufs, send_sems, recv_sems):
    x, y, c, chips = _place()
    sends, arrivals = [], []
    for i, buf in enumerate(bufs):
        for r, (px, py) in enumerate(chips):
            mine = buf.at[2 * px + py, _half_rows(c, buf.shape[1])]
            theirs = buf.at[2 * px + py, _half_rows(1 - c, buf.shape[1])]
            sends.append(_remote(mine, mine, send_sems.at[3 * i + r], recv_sems.at[3 * i + r], (x, y, 1 - c)))
            arrivals.append(_remote(theirs, theirs, send_sems.at[3 * i + r], recv_sems.at[3 * i + r], (x, y, 1 - c)))
    return sends, arrivals


def forward_to_sibling(bufs, name):
    n = len(bufs)

    def body(*refs):
        sends, arrivals = _forward_copies(refs[n:2 * n], *refs[2 * n:])
        for cp in sends:
            cp.start()
        for cp in arrivals:
            cp.wait_recv()
        for cp in sends:
            cp.wait_send()

    return pl.pallas_call(
        body, name=name, in_specs=[ANY] * n, out_specs=[ANY] * n, input_output_aliases={i: i for i in range(n)},
        out_shape=[jax.ShapeDtypeStruct(b.shape, b.dtype) for b in bufs],
        scratch_shapes=[pltpu.SemaphoreType.DMA((3 * n,)), pltpu.SemaphoreType.DMA((3 * n,))],
    )(*bufs)


def add_own_half(g, p, core, name):
    n_j, n_rows, n_cols = g.shape
    half = n_rows // 2
    tr = _rows(half, n_cols * 10)

    def body(c_ref, g_ref, p_ref, o_ref):
        o_ref[...] = (g_ref[...] + p_ref[...]).astype(o_ref.dtype)

    grid_spec = pltpu.PrefetchScalarGridSpec(
        num_scalar_prefetch=1, grid=(n_j, half // tr),
        in_specs=[pl.BlockSpec((None, None, tr, n_cols), lambda j, i, c_ref: (j, c_ref[0], i, 0)),
                  pl.BlockSpec((None, tr, n_cols), lambda j, i, c_ref: (j, i, 0))],
        out_specs=pl.BlockSpec((None, tr, n_cols), lambda j, i, c_ref: (j, i, 0)))
    return pl.pallas_call(
        body, name=name, grid_spec=grid_spec, out_shape=jax.ShapeDtypeStruct((n_j, half, n_cols), BF16),
        compiler_params=_params(("parallel", "parallel")),
    )(core, g.reshape(n_j, 2, half, n_cols), p)


def _scatter_copies(n_parts):
    def copies(bufs, send_sems, recv_sems):
        x, y, c, chips = _place()
        me = 2 * x + y
        sends, arrivals = [], []
        for i in range(n_parts):
            part, land = bufs[i], bufs[n_parts + i]
            for r, (px, py) in enumerate(chips):
                theirs = land.at[2 * px + py]
                sends.append(_remote(part.at[2 * px + py], land.at[me], send_sems.at[3 * i + r], recv_sems.at[3 * i + r], (px, py, c)))
                arrivals.append(_remote(theirs, theirs, send_sems.at[3 * i + r], recv_sems.at[3 * i + r], (px, py, c)))
        return sends, arrivals
    return copies


def sum_over_chips(own, got, chip, name):
    n_s, n_rows, n_cols = got.shape
    tr = _rows(n_rows, n_cols * (got.dtype.itemsize * (n_s + 1) + 4))

    def body(chip_ref, own_ref, *refs):
        got_refs, o_ref = refs[:n_s], refs[n_s]
        me = chip_ref[0]
        acc = jnp.where(me == 0, own_ref[...], got_refs[0][...]).astype(F32)
        for k in range(1, n_s):
            acc = acc + jnp.where(me == k, own_ref[...], got_refs[k][...]).astype(F32)
        o_ref[...] = acc

    def slot(k):
        return pl.BlockSpec((None, tr, n_cols), lambda i, chip_ref: (jnp.where(chip_ref[0] == k, (k + 1) % n_s, k), i, 0))

    grid_spec = pltpu.PrefetchScalarGridSpec(
        num_scalar_prefetch=1, grid=(n_rows // tr,),
        in_specs=[pl.BlockSpec((None, tr, n_cols), lambda i, chip_ref: (chip_ref[0], i, 0))] + [slot(k) for k in range(n_s)],
        out_specs=pl.BlockSpec((tr, n_cols), lambda i, chip_ref: (i, 0)))
    return pl.pallas_call(
        body, name=name, grid_spec=grid_spec, out_shape=jax.ShapeDtypeStruct((n_rows, n_cols), F32),
        compiler_params=_params(("parallel",)),
    )(chip, own, *([got] * n_s))


def sum_slots(q, name):
    n_s, n_rows, n_cols = q.shape
    tr = _rows(n_rows, n_cols * (q.dtype.itemsize * n_s + 4))

    def body(q_ref, o_ref):
        acc = q_ref[0].astype(F32)
        for s in range(1, n_s):
            acc = acc + q_ref[s].astype(F32)
        o_ref[...] = acc

    return pl.pallas_call(
        body, name=name, grid=(n_rows // tr,), in_specs=[pl.BlockSpec((n_s, tr, n_cols), lambda i: (0, i, 0))],
        out_specs=pl.BlockSpec((tr, n_cols), lambda i: (i, 0)), out_shape=jax.ShapeDtypeStruct((n_rows, n_cols), F32),
        compiler_params=_params(("parallel",)),
    )(q)


def gather_all_devices(s, name):
    def body(s_ref, o_ref, send_sems, recv_sems, local_sem):
        x, y, c, _ = _place()
        me = 4 * x + 2 * y + c
        local = pltpu.make_async_copy(s_ref, o_ref.at[me], local_sem)
        local.start()
        sends = []
        for f in range(1, N_DEV):
            px, py, pc = x ^ (f >> 2), y ^ ((f >> 1) & 1), c ^ (f & 1)
            sends.append(_remote(s_ref, o_ref.at[me], send_sems.at[f - 1], recv_sems.at[f - 1], (px, py, pc)))
            sends[-1].start()
        for f in range(1, N_DEV):
            px, py, pc = x ^ (f >> 2), y ^ ((f >> 1) & 1), c ^ (f & 1)
            landed = o_ref.at[4 * px + 2 * py + pc]
            _remote(landed, landed, send_sems.at[f - 1], recv_sems.at[f - 1], (px, py, pc)).wait_recv()
        for cp in sends:
            cp.wait_send()
        local.wait()

    return pl.pallas_call(
        body, name=name, in_specs=[ANY], out_specs=ANY, out_shape=jax.ShapeDtypeStruct((N_DEV,) + s.shape, s.dtype),
        scratch_shapes=[pltpu.SemaphoreType.DMA((N_DEV - 1,)), pltpu.SemaphoreType.DMA((N_DEV - 1,)), pltpu.SemaphoreType.DMA],
    )(s)


FFN_NAMES = ("ffn1_wg", "ffn1_wu", "ffn1_wd", "ffn2_wg", "ffn2_wu", "ffn2_wd")
XATTN_NAMES = ("x_wq", "x_wk", "x_wv", "x_wo")
BIG_NAMES = FFN_NAMES + XATTN_NAMES + ("mix_w_in", "mix_w_out", "pool_w")
SMALL_NAMES = ("norms", "mem_norm", "attn_sinks", "sgu_ln_g", "sgu_ln_b", "sgu_w", "sgu_b", "pool_scale")
WEIGHT_ORDER = ("norms", "mem_norm") + BIG_NAMES[:-1] + ("attn_sinks", "sgu_ln_g", "sgu_ln_b", "sgu_w", "sgu_b", "pool_w", "pool_scale")
COLUMN_CUT = ("x_wo", "mix_w_in")
N_POOL = len(POOL_WINDOWS)
BLOCK_ORDER = (("ffn1", 0), ("mix", 0), ("xattn", 0), ("ffn2", 0), ("ffn1", 1), ("mix", 1), ("xattn", 1), ("ffn2", 1))
PREFETCH_AT = ((1, 2, 3), (), (), (4,), (5, 6, 7), (), (), ())


def block_weight_names(kind, layer):
    if kind == "mix":
        return ("mix_w_in", "mix_w_out") if layer == 0 else ("pool_w",)
    return XATTN_NAMES if kind == "xattn" else tuple(f"{kind}_{part}" for part in ("wg", "wu", "wd"))


def _to_matmul_layout(name, g):
    n_j, n_rows, n_cols = g.shape
    if name in COLUMN_CUT:
        return g.transpose(1, 0, 2).reshape(n_rows, n_j * n_cols)
    if name == "pool_w":
        return g.reshape(n_j, N_POOL, n_rows // N_POOL, n_cols).transpose(1, 0, 2, 3).reshape(N_POOL, n_j * n_rows // N_POOL, n_cols)
    if name in ("x_wq", "x_wk", "x_wv", "mix_w_out"):
        return g.reshape(n_j * n_rows, n_cols)
    return g


def _from_matmul_layout(name, d):
    if name in COLUMN_CUT:
        n_rows, wide = d.shape
        return d.reshape(n_rows, N_CHIPS, wide // N_CHIPS).transpose(1, 0, 2)
    if name == "pool_w":
        n_g, n_in, n_cols = d.shape
        return d.reshape(n_g, N_CHIPS, n_in // N_CHIPS, n_cols).transpose(1, 0, 2, 3).reshape(N_CHIPS, n_g * n_in // N_CHIPS, n_cols)
    if name in ("x_wq", "x_wk", "x_wv", "mix_w_out"):
        return d.reshape(N_CHIPS, d.shape[0] // N_CHIPS, d.shape[1])
    return d


def _as3(w):
    return w.reshape(w.shape[0], -1, w.shape[-1])


def _pack(arrays, row_multiple):
    flat = jnp.concatenate([a.reshape(-1) for a in arrays])
    per = LANES * row_multiple
    total = -(-flat.shape[0] // per) * per
    return jnp.pad(flat, (0, total - flat.shape[0])).reshape(total // LANES, LANES)


def _unpack(packed, like):
    flat, out, at = packed.reshape(-1), [], 0
    for a in like:
        out.append(flat[at:at + a.size].reshape(a.shape))
        at += a.size
    return out


def kernel(x, mem, norms, mem_norm, ffn1_wg, ffn1_wu, ffn1_wd, ffn2_wg, ffn2_wu, ffn2_wd, x_wq, x_wk, x_wv, x_wo, mix_w_in, mix_w_out, attn_sinks, sgu_ln_g, sgu_ln_b, sgu_w, sgu_b, pool_w, pool_scale, loss_target, m_norms, m_mem_norm, m_ffn1_wg, m_ffn1_wu, m_ffn1_wd, m_ffn2_wg, m_ffn2_wu, m_ffn2_wd, m_x_wq, m_x_wk, m_x_wv, m_x_wo, m_mix_w_in, m_mix_w_out, m_attn_sinks, m_sgu_ln_g, m_sgu_ln_b, m_sgu_w, m_sgu_b, m_pool_w, m_pool_scale, v_norms, v_mem_norm, v_ffn1_wg, v_ffn1_wu, v_ffn1_wd, v_ffn2_wg, v_ffn2_wu, v_ffn2_wd, v_x_wq, v_x_wk, v_x_wv, v_x_wo, v_mix_w_in, v_mix_w_out, v_attn_sinks, v_sgu_ln_g, v_sgu_ln_b, v_sgu_w, v_sgu_b, v_pool_w, v_pool_scale):
    given = dict(locals())
    w = {n: given[n] for n in WEIGHT_ORDER}
    mom = {n: given["m_" + n] for n in WEIGHT_ORDER}
    var = {n: given["v_" + n] for n in WEIGHT_ORDER}
    chip_id = 2 * lax.axis_index("x") + lax.axis_index("y")
    chip = chip_id.astype(jnp.int32).reshape(1)
    core = lax.axis_index("c").astype(jnp.int32).reshape(1)
    n_shard = norms.shape[-1]

    keys = [(name, l) for name in BIG_NAMES for l in range(_as3(w[name]).shape[0])]
    first_keys = [(name, 0) for name in block_weight_names(*BLOCK_ORDER[0])]
    cast = lambda name, l: cast_into_slot(_as3(w[name]), l, chip, f"cast_{name}")
    slot_of = {key: cast(*key) for key in first_keys}
    small_rows = jnp.concatenate([norms.reshape(-1, n_shard), pool_scale, jnp.zeros((15, n_shard), F32)], axis=0)
    small_slot = lax.dynamic_update_slice_in_dim(jnp.zeros((N_CHIPS,) + small_rows.shape, F32), small_rows[None], chip_id, axis=0)
    pending = {}
    gather_start = lambda slots, after, name: copies_start(slots, [], _chip_copies, 3 * len(slots), after, name)
    pending["0a"] = gather_start([slot_of[first_keys[0]], small_slot], chip, "gather_start_0a")
    pending["0b"] = gather_start([slot_of[first_keys[1]]], pending["0a"][2][0], "gather_start_0b")
    pending["0c"] = gather_start([slot_of[first_keys[2]]], pending["0b"][2][0], "gather_start_0c")
    for key in keys:
        if key not in slot_of:
            slot_of[key] = cast(*key)


    def block_slots(k):
        kind, layer = BLOCK_ORDER[k]
        return [slot_of[name, layer if kind != "mix" else 0] for name in block_weight_names(kind, layer)]

    def start(k, after):
        pending[k] = gather_start(block_slots(k), after, f"gather_start_{k}")

    def finish(k, after):
        send_sems, recv_sems, bufs, _ = pending[k]
        return forward_to_sibling(copies_wait(send_sems, recv_sems, bufs, _chip_copies, [after], f"gather_wait_{k}"), f"gather_forward_{k}")

    def fetch(key, after):
        k = BLOCK_ORDER.index(key)
        names = block_weight_names(*key)
        token = 0.0
        if k == 0:
            late = lambda part, name: lambda after: _to_matmul_layout(name, finish(part, after)[0])
            bufs = [first_gate]
            weights = (_to_matmul_layout(names[0], first_gate), late("0b", names[1]), late("0c", names[2]))
        elif key[0] in ("ffn1", "ffn2"):
            send_sems, recv_sems, bufs, _ = pending[k]
            bufs = copies_wait(send_sems, recv_sems, bufs, _chip_copies, [after], f"gather_wait_{k}")
            gate, = forward_to_sibling(bufs[:1], f"gather_forward_{k}_gate")
            send_sems, recv_sems, rest, begun = copies_start(bufs[1:], [], _forward_copies, 6, gate, f"gather_forward_{k}_start")
            landed = []

            def late(i, name):
                def get(after):
                    if not landed:
                        landed.extend(copies_wait(send_sems, recv_sems, rest, _forward_copies, [after], f"gather_forward_{k}_wait"))
                    return _to_matmul_layout(name, landed[i])
                return get
            bufs = [gate]
            weights = (_to_matmul_layout(names[0], gate), late(0, names[1]), late(1, names[2]))
            token = begun[0, 0]
        else:
            bufs = finish(k, after)
            weights = tuple(_to_matmul_layout(name, b) for name, b in zip(names, bufs))
        for ahead in PREFETCH_AT[k]:
            start(ahead, bufs[0])
            token = token + pending[ahead][3][0, 0]
        return weights, token

    to_sibling, to_chips, halves, own_of, recv_of = [], [], [], {}, {}
    other_half = lambda c, src: _half_rows(1 - c, src.shape[1])

    def sibling_start(tag, names, full, after):
        lands = [jax.ShapeDtypeStruct((g.shape[0], g.shape[1] // 2, g.shape[2]), g.dtype) for g in full]
        send_sems, recv_sems, bufs, token = copies_start(full, lands, _to_sibling_copies(len(full), other_half), len(full), after,
                                                         f"grads_to_sibling_{tag}")
        to_sibling.append((tag, names, send_sems, recv_sems, bufs))
        return token

    def early(key, name, g):
        kind, layer = key
        return sibling_start(f"{name}{layer}", [(name, layer)], [_from_matmul_layout(name, g)], chip)

    def chips_start(tag, after):
        names, chip_sums = [], []
        for part, part_names, send_sems, recv_sems, bufs in to_sibling:
            n = len(part_names)
            bufs = copies_wait(send_sems, recv_sems, bufs, _to_sibling_copies(n, other_half), after, f"grads_from_sibling_{part}")
            chip_sums += [add_own_half(g, p, core, f"chip_sum_{name}") for (name, _), g, p in zip(part_names, bufs[:n], bufs[n:])]
            names += part_names
        to_sibling.clear()
        n = len(names)
        send_sems, recv_sems, bufs, token = copies_start(chip_sums, chip_sums, _scatter_copies(n), 3 * n, after[0], f"grads_start_{tag}")
        to_chips.append((tag, names, send_sems, recv_sems, bufs))
        return token[0, 0]

    def chips_wait(after):
        tag, names, send_sems, recv_sems, bufs = to_chips.pop()
        return tag, names, copies_wait(send_sems, recv_sems, bufs, _scatter_copies(len(names)), after, f"grads_wait_{tag}")

    def reduce_landed(tag, names, bufs):
        n = len(names)
        own = [sum_over_chips(t, q, chip, f"sum_{name}") for (name, _), t, q in zip(names, bufs[:n], bufs[n:])]
        send_sems, recv_sems, bufs, _ = copies_start(own, own, _to_sibling_copies(n, None), n, chip, f"grads_halves_{tag}")
        halves.append((tag, names, send_sems, recv_sems, bufs))

    def halves_land(after):
        for tag, names, send_sems, recv_sems, bufs in halves:
            n = len(names)
            bufs = copies_wait(send_sems, recv_sems, bufs, _to_sibling_copies(n, None), after, f"grads_halves_wait_{tag}")
            for key, o, r in zip(names, bufs[:n], bufs[n:]):
                own_of[key], recv_of[key] = o, r
        halves.clear()

    def emit(key, grads_of, after):
        kind, layer = key
        landed = chips_wait([after]) if to_chips else None
        begun = {name for _, part_names, _, _, _ in to_sibling for name, _ in part_names}
        rest = {name: g for name, g in grads_of.items() if name not in begun}
        if rest:
            sibling_start(f"{kind}{layer}", [(name, layer if kind != "mix" else 0) for name in rest],
                          [_from_matmul_layout(name, g) for name, g in rest.items()], after)
        token = chips_start(f"{kind}{layer}", [after])
        if landed:
            reduce_landed(*landed)
        return token

    first_gate, small_all = finish("0a", pending["0c"][2][0])
    n_norm_rows = norms.shape[0] * norms.shape[1]
    norms_all = jnp.concatenate([small_all[j, :n_norm_rows].reshape(norms.shape) for j in range(N_CHIPS)], axis=-1)
    pool_scale_all = jnp.concatenate([small_all[j, n_norm_rows:n_norm_rows + 1] for j in range(N_CHIPS)], axis=-1)
    small = dict(attn_sinks=attn_sinks[0], sgu_ln_g=sgu_ln_g, sgu_ln_b=sgu_ln_b, sgu_w=sgu_w[0], sgu_b=sgu_b[0], pool_scale=pool_scale_all)

    loss_part, dx, g_small, g_norms, g_mem_norm, last_token = device_step(
        x[0], mem[0], loss_target[0], norms_all, mem_norm, small, fetch, emit, early)
    loss = lax.psum(0.5 * jnp.sum(loss_part) / x.shape[-1], ("x", "y", "c"))
    last_names = {name for name, _ in to_chips[0][1]}
    order = last_token.reshape(1, 1)
    halves_land([dx])

    grads, delta, new_m, new_v = {}, {}, {}, {}

    def update(name):
        n_l = _as3(w[name]).shape[0]
        res = adamw_from_halves(_as3(w[name]), [own_of[name, l] for l in range(n_l)], [recv_of[name, l] for l in range(n_l)],
                                _as3(mom[name]), _as3(var[name]), core, order, f"adamw_{name}")
        grads[name], delta[name], new_m[name], new_v[name] = (t.reshape(w[name].shape) for t in res)

    for name in BIG_NAMES:
        if name not in last_names:
            update(name)
    done = [delta[name] for name in BIG_NAMES if name not in last_names]
    reduce_landed(*chips_wait(done))
    halves_land(done)
    for name in BIG_NAMES:
        if name in last_names:
            update(name)

    small_g = [g_norms, g_mem_norm, g_small["attn_sinks"], g_small["sgu_ln_g"], g_small["sgu_ln_b"], g_small["sgu_w"], g_small["sgu_b"],
               g_small["pool_scale"]]
    packed = _pack(small_g, 16)
    summed = sum_slots(gather_all_devices(packed, "small_grads_all"), "small_grads_sum")
    s_norms, s_mem, s_sinks, s_lg, s_lb, s_w, s_b, s_scale = _unpack(summed, small_g)
    grads["norms"] = lax.dynamic_slice_in_dim(s_norms, chip_id * n_shard, n_shard, axis=2)
    grads["pool_scale"] = lax.dynamic_slice_in_dim(s_scale, chip_id * n_shard, n_shard, axis=1)
    grads.update(mem_norm=s_mem, attn_sinks=s_sinks, sgu_ln_g=s_lg, sgu_ln_b=s_lb, sgu_w=s_w, sgu_b=s_b)
    like = [w[n] for n in SMALL_NAMES]
    packs = [_pack([src[n] for n in SMALL_NAMES], 16)[None] for src in (w, grads, mom, var)]
    for dst, t in zip((delta, new_m, new_v), adamw(*packs, "adamw_small")):
        for n, a in zip(SMALL_NAMES, _unpack(t[0], like)):
            dst[n] = a

    outs = [loss, dx[None]]
    for group in (grads, delta, new_m, new_v):
        outs += [group[n] for n in WEIGHT_ORDER]
    return tuple(outs)
```

```python
import functools

import jax
import jax.numpy as jnp
from jax import lax
from jax.experimental import pallas as pl
from jax.experimental.pallas import tpu as pltpu

F32 = jnp.float32
BF16 = jnp.bfloat16
MESH = pl.DeviceIdType.MESH

EPS = 1e-6
ROPE_THETA = 500000.0
ROPE_HALF = 8
HEAD_DIM = 64
N_Q_HEADS = 16
N_KV_HEADS = 2
Q_PER_KV = 8
BLOCK = 128
ATTN_WIDTH = 1024
KV_WIDTH = 128
QK_WIDTH = ATTN_WIDTH + KV_WIDTH
SGU_WIDTH = 1024
SGU_GROUPS = 8
POOL_WINDOWS = (2, 4, 8, 16)
POOL_HALO = 16
X_HEADS = 4
X_HEAD_DIM = 128
N_CHIPS = 4
N_DEV = 8

ADAM_LR = 0.001
ADAM_B1 = 0.9
ADAM_B2 = 0.999
ADAM_EPS = 1e-08
ADAM_WD = 0.01
ADAM_STEP = 10

VMEM_LIMIT_V7X = 52 * 1024 * 1024
MM_VMEM_BUDGET = 44 * 1024 * 1024
LANES = 128
ROW_TILE_BYTES = 6 * 1024 * 1024
WEIGHT_TILE_BYTES = 12 * 1024 * 1024
MXU_FLOPS_V7X = 1.0e15
HBM_BYTES_PER_S_V7X = 3.0e12
VMEM_STORE_BYTES_PER_S = 4.0e12
MXU_WEIGHT_LOAD_ROWS = 192
MXU_NARROW_COLS = 128
GRID_STEP_S = 0.35e-6


def _params(sem):
    return pltpu.CompilerParams(dimension_semantics=sem, vmem_limit_bytes=VMEM_LIMIT_V7X)


def _pick(dim, pref, align):
    cands = [t for t in range(align, dim + 1, align) if dim % t == 0]
    small = [t for t in cands if t <= pref]
    if small and small[-1] * 2 >= min(pref, dim):
        return small[-1]
    return dim


def _rows(n_rows, bytes_per_row, tile_bytes=ROW_TILE_BYTES):
    want = max(16, min(1024, tile_bytes // max(1, bytes_per_row)))
    cands = [t for t in range(16, n_rows + 1, 16) if n_rows % t == 0 and t <= want]
    return cands[-1] if cands else n_rows


def _divisors(dim, align, most):
    return [t for t in range(align, min(dim, most) + 1, align) if dim % t == 0] or [dim]


def _mm_tiles(M, N, K, J, m_align, k_align, a_bytes, b_bytes, o_bytes, reduce, ta, products=1):
    best = None
    for tm in _divisors(M, m_align, 2048):
        for tn in _divisors(N, LANES, 2048):
            for tk in _divisors(K, k_align, 4096):
                split = K // tk > 1 or reduce
                vmem = 2 * (tm * tk * a_bytes + tk * tn * b_bytes + tm * tn * o_bytes) + tm * tn * 4 * products * (2 if split else 1)
                if ta:
                    vmem += tm * tk * a_bytes
                if vmem > MM_VMEM_BUDGET:
                    continue
                steps = J * (M // tm) * (N // tn) * (K // tk)
                mxu = 2.0 * J * M * N * K / MXU_FLOPS_V7X * (tm + MXU_WEIGHT_LOAD_ROWS) / tm * (tn + MXU_NARROW_COLS) / tn
                acc = J * M * N * (K // tk) * 8 / VMEM_STORE_BYTES_PER_S if split else 0.0
                hbm = J * (M * K * a_bytes * (N // tn) + K * N * b_bytes * (M // tm) + M * N * o_bytes) / HBM_BYTES_PER_S_V7X
                cost = max(mxu + 0.5 * acc, hbm) + steps * GRID_STEP_S
                if best is None or cost < best[0]:
                    best = (cost, tm, tn, tk)
    return best[1:]


def _rsum8(v):
    r, c = v.shape
    return v.reshape(r // 8, 8, c).sum(axis=0)


def _mm(a, b, *, name, ta=False, tb=False, batch="none", groups=0, a_cb=False, b_cb=False, o_cb=False,
        out_dtype=F32, more=(), extras=(), epilogue=None, out_dtypes=None, order=None):
    J = groups or (a.shape[0] if a.ndim == 3 else (b.shape[0] if b.ndim == 3 else 1))
    a2, b2 = a.shape[-2:], b.shape[-2:]
    M, K = (a2[1], a2[0]) if ta else a2
    N, Kb = b2 if tb else (b2[1], b2[0])
    if a_cb:
        if ta:
            M //= J
        else:
            K //= J
    if b_cb:
        if tb:
            Kb //= J
        else:
            N //= J
    assert K == Kb, (name, a.shape, b.shape)
    reduce = batch == "reduce"
    out_dtypes = list(out_dtypes or [out_dtype])
    n_terms = 1 + len(more)
    o_bytes = sum(jnp.dtype(d).itemsize for d in out_dtypes) + sum(e.dtype.itemsize for e in extras)
    n_prod = n_terms if epilogue is not None else 1
    tm, tn, tk = _mm_tiles(M, N, K, J, LANES if ta else 16, LANES if (not ta or tb) else 16, a.dtype.itemsize * n_terms,
                           b.dtype.itemsize * n_terms, o_bytes, reduce, ta, n_prod)
    nm, nn, nk = M // tm, N // tn, K // tk
    if reduce:
        grid = (nm, nn, J, nk)
        unpack = lambda m, n, j, k: (j, m, n, k)
        sem = ("parallel", "parallel", "arbitrary", "arbitrary")
    else:
        grid = (J, nm, nn, nk)
        unpack = lambda j, m, n, k: (j, m, n, k)
        sem = ("parallel", "parallel", "parallel", "arbitrary")

    def a_map(*g):
        j, m, n, k = unpack(*g)
        r, c = (k, m) if ta else (m, k)
        if a_cb:
            c = c + j * (nm if ta else nk)
        return (j, r, c) if a.ndim == 3 else (r, c)

    def b_map(*g):
        j, m, n, k = unpack(*g)
        r, c = (n, k) if tb else (k, n)
        if b_cb:
            c = c + j * (nk if tb else nn)
        return (j, r, c) if b.ndim == 3 else (r, c)

    def o_map(*g):
        j, m, n, k = unpack(*g)
        if o_cb:
            return (m, n + j * nn)
        return (j, m, n) if batch == "map" else (m, n)

    a_blk = (tk, tm) if ta else (tm, tk)
    b_blk = (tn, tk) if tb else (tk, tn)
    a_spec = pl.BlockSpec(((None,) + a_blk) if a.ndim == 3 else a_blk, a_map)
    b_spec = pl.BlockSpec(((None,) + b_blk) if b.ndim == 3 else b_blk, b_map)
    if o_cb:
        out_shape, o_blk = (M, N * J), (tm, tn)
    elif batch == "map":
        out_shape, o_blk = (J, M, N), (None, tm, tn)
    else:
        out_shape, o_blk = (M, N), (tm, tn)
    o_spec = pl.BlockSpec(o_blk, o_map)
    dims = (((0 if ta else 1,), (1 if tb else 0,)), ((), ()))
    red_axes = (2, 3) if reduce else (3,)
    split = reduce or nk > 1
    n_ex, n_out = len(extras), len(out_dtypes)
    n_ord = 0 if order is None else 1

    def body(*refs):
        refs = refs[n_ord:]
        ab_refs, ex_refs = refs[:2 * n_terms], refs[2 * n_terms:2 * n_terms + n_ex]
        o_refs, acc = refs[2 * n_terms + n_ex:2 * n_terms + n_ex + n_out], refs[2 * n_terms + n_ex + n_out:]
        prods = [lax.dot_general(ab_refs[2 * t][...], ab_refs[2 * t + 1][...], dims, preferred_element_type=F32) for t in range(n_terms)]
        if epilogue is None:
            prods = [functools.reduce(lambda p, q: p + q, prods)]

        def finish(vals):
            outs = epilogue(vals, [e[...] for e in ex_refs]) if epilogue is not None else vals
            for o_ref, val in zip(o_refs, outs):
                o_ref[...] = val.astype(o_ref.dtype)

        if not split:
            finish(prods)
            return
        first = functools.reduce(jnp.logical_and, [pl.program_id(ax) == 0 for ax in red_axes])
        last = functools.reduce(jnp.logical_and, [pl.program_id(ax) == grid[ax] - 1 for ax in red_axes])

        @pl.when(first)
        def _():
            for acc_ref, prod in zip(acc, prods):
                acc_ref[...] = prod

        @pl.when(jnp.logical_not(first))
        def _():
            for acc_ref, prod in zip(acc, prods):
                acc_ref[...] += prod

        @pl.when(last)
        def _():
            finish([acc_ref[...] for acc_ref in acc])

    operands = [order] * n_ord + [a, b] + [t for pair in more for t in pair] + list(extras)
    res = pl.pallas_call(
        body, name=name, grid=grid, out_specs=[o_spec] * n_out,
        in_specs=[pl.BlockSpec(memory_space=pl.ANY)] * n_ord + [a_spec, b_spec] * n_terms + [o_spec] * n_ex,
        out_shape=[jax.ShapeDtypeStruct(out_shape, d) for d in out_dtypes],
        scratch_shapes=[pltpu.VMEM((tm, tn), F32)] * (n_prod if split else 0), compiler_params=_params(sem),
    )(*operands)
    return res if epilogue is not None else res[0]


def _rowwise(fn, tiled, whole, outs, accs, *, name):
    n_rows = tiled[0].shape[0]
    row_bytes = sum(t.shape[1] * t.dtype.itemsize for t in tiled) + sum(c * jnp.dtype(d).itemsize for c, d in outs)
    tr = _rows(n_rows, row_bytes)
    n_t, n_w, n_o = len(tiled), len(whole), len(outs)

    def body(*refs):
        i = pl.program_id(0)
        t_refs, w_refs = refs[:n_t], refs[n_t:n_t + n_w]
        o_refs, a_refs = refs[n_t + n_w:n_t + n_w + n_o], refs[n_t + n_w + n_o:]
        o_vals, a_vals = fn(i, *[r[...] for r in t_refs], *[r[...] for r in w_refs])
        for r, v in zip(o_refs, o_vals):
            r[...] = v.astype(r.dtype)
        if a_refs:
            @pl.when(i == 0)
            def _():
                for r in a_refs:
                    r[...] = jnp.zeros_like(r)
            for r, v in zip(a_refs, a_vals):
                r[...] += v

    in_specs = [pl.BlockSpec((tr, t.shape[1]), lambda i: (i, 0)) for t in tiled]
    in_specs += [pl.BlockSpec(w.shape, lambda i, nd=w.ndim: (0,) * nd) for w in whole]
    out_specs = [pl.BlockSpec((tr, c), lambda i: (i, 0)) for c, _ in outs]
    out_specs += [pl.BlockSpec(s, lambda i, nd=len(s): (0,) * nd) for s, _ in accs]
    out_shape = [jax.ShapeDtypeStruct((n_rows, c), d) for c, d in outs]
    out_shape += [jax.ShapeDtypeStruct(s, d) for s, d in accs]
    res = pl.pallas_call(
        body, name=name, grid=(n_rows // tr,), in_specs=in_specs, out_specs=out_specs, out_shape=out_shape,
        compiler_params=_params(("arbitrary",) if accs else ("parallel",)),
    )(*tiled, *whole)
    return res


def _rms_stats(x):
    r = lax.rsqrt(jnp.mean(x * x, axis=-1, keepdims=True) + EPS)
    return x * r, r


def rms_fwd(x, g, out_dtype, name):
    def fn(i, x, g):
        xhat, _ = _rms_stats(x)
        return [xhat * g], []
    return _rowwise(fn, [x], [g], [(x.shape[1], out_dtype)], [], name=name)[0]


def postnorm_res(x, y, g, s, name):
    def fn(i, x, y, g):
        yhat, _ = _rms_stats(y)
        return [x + s * (yhat * g)], []
    return _rowwise(fn, [x, y], [g], [(x.shape[1], F32)], [], name=name)[0]


def rms_bwd(xin, g, douts, scale, add, out_dtype, name):
    n_d = len(douts)

    def fn(i, x, *rest):
        ds, rest = rest[:n_d], rest[n_d:]
        ad = rest[0] if add is not None else None
        g = rest[-1]
        xhat, r = _rms_stats(x)
        d = ds[0].astype(F32)
        for e in ds[1:]:
            d = d + e.astype(F32)
        if scale != 1.0:
            d = d * scale
        dg = _rsum8(d * xhat)
        dxhat = d * g
        dx = r * (dxhat - xhat * jnp.mean(dxhat * xhat, axis=-1, keepdims=True))
        if ad is not None:
            dx = dx + ad
        return [dx], [dg]

    tiled = [xin, *douts] + ([add] if add is not None else [])
    dx, dg = _rowwise(fn, tiled, [g], [(xin.shape[1], out_dtype)], [((8, xin.shape[1]), F32)], name=name)
    return dx, dg


def _silu_parts(g):
    sg = 1.0 / (1.0 + jnp.exp(-g))
    return g * sg, sg


def _swiglu_tiles(products, saved):
    u, = products
    return [u, _silu_parts(saved[0].astype(F32))[0] * u]


def _swiglu_bwd_tiles(products, saved):
    da, = products
    g, u = saved[0].astype(F32), saved[1].astype(F32)
    s, sg = _silu_parts(g)
    return [da * u * (sg * (1.0 + g * (1.0 - sg))), da * s]


def scale_cols(y, s, name):
    def fn(i, y, s):
        return [y * s], []
    return _rowwise(fn, [y], [s], [(y.shape[1], F32)], [], name=name)[0]


def scale_cols_bwd(dm, y, s, name):
    def fn(i, dm, y, s):
        return [dm * s], [_rsum8(dm * y)]
    return _rowwise(fn, [dm, y], [s], [(y.shape[1], BF16)], [((8, y.shape[1]), F32)], name=name)


def loss_and_grad(y, target, name):
    n_feat = y.shape[1]

    def fn(i, y, t):
        e = y - t
        return [e * (1.0 / n_feat)], [_rsum8(e * e)]
    dy, part = _rowwise(fn, [y, target], [], [(n_feat, F32)], [((8, n_feat), F32)], name=name)
    return part, dy


def cast_into_slot(w3, layer, chip, name):
    _, n_rows, n_cols = w3.shape
    tr = _rows(n_rows, n_cols * 6, WEIGHT_TILE_BYTES)

    def body(chip_ref, w_ref, o_ref):
        o_ref[...] = w_ref[...].astype(BF16)

    grid_spec = pltpu.PrefetchScalarGridSpec(
        num_scalar_prefetch=1, grid=(n_rows // tr,),
        in_specs=[pl.BlockSpec((None, tr, n_cols), lambda i, chip_ref: (layer, i, 0))],
        out_specs=pl.BlockSpec((None, tr, n_cols), lambda i, chip_ref: (chip_ref[0], i, 0)))
    return pl.pallas_call(
        body, name=name, grid_spec=grid_spec, out_shape=jax.ShapeDtypeStruct((N_CHIPS, n_rows, n_cols), BF16),
        compiler_params=_params(("parallel",)),
    )(chip, w3)


def _adam_update(w, g, m, v):
    c1 = 1.0 / (1.0 - ADAM_B1 ** ADAM_STEP)
    c2 = 1.0 / (1.0 - ADAM_B2 ** ADAM_STEP)
    m = ADAM_B1 * m + (1.0 - ADAM_B1) * g
    v = ADAM_B2 * v + (1.0 - ADAM_B2) * (g * g)
    return -ADAM_LR * ((m * c1) / (jnp.sqrt(v * c2) + ADAM_EPS) + ADAM_WD * w), m, v


def adamw(w, g, m, v, name):
    n_l, n_rows, n_cols = w.shape
    tr = _rows(n_rows, n_cols * 4 * 7)

    def body(w_ref, g_ref, m_ref, v_ref, d_ref, mo_ref, vo_ref):
        d_ref[...], mo_ref[...], vo_ref[...] = _adam_update(w_ref[...], g_ref[...], m_ref[...], v_ref[...])

    spec = pl.BlockSpec((None, tr, n_cols), lambda l, i: (l, i, 0))
    shp = jax.ShapeDtypeStruct(w.shape, F32)
    return pl.pallas_call(
        body, name=name, grid=(n_l, n_rows // tr), in_specs=[spec] * 4, out_specs=[spec] * 3, out_shape=[shp] * 3,
        compiler_params=_params(("parallel", "parallel")),
    )(w, g, m, v)


def adamw_from_halves(w, own, recv, m, v, core, order, name):
    n_l, n_rows, n_cols = w.shape
    half = n_rows // 2
    tr = _rows(half, n_cols * 4 * 9, WEIGHT_TILE_BYTES)
    per = half // tr

    def body(core_ref, *refs):
        w_ref, m_ref, v_ref, order_ref = refs[:4]
        own_refs, recv_refs = refs[4:4 + n_l], refs[4 + n_l:4 + 2 * n_l]
        g_ref, d_ref, mo_ref, vo_ref = refs[4 + 2 * n_l:]
        l, h = pl.program_id(0), pl.program_id(1)
        mine = h == core_ref[0]
        g = jnp.where(mine, own_refs[0][...], recv_refs[0][...])
        for k in range(1, n_l):
            g = jnp.where(l == k, jnp.where(mine, own_refs[k][...], recv_refs[k][...]), g)
        g = g + order_ref[...]
        g_ref[...] = g
        d_ref[...], mo_ref[...], vo_ref[...] = _adam_update(w_ref[...], g, m_ref[...], v_ref[...])

    full = pl.BlockSpec((None, tr, n_cols), lambda l, h, i, core_ref: (l, h * per + i, 0))

    def piece(layer, is_own):
        def index(l, h, i, core_ref):
            used = (l == layer) & ((h == core_ref[0]) == is_own)
            return (jnp.where(used, i, 0), 0)
        return pl.BlockSpec((tr, n_cols), index)

    grid_spec = pltpu.PrefetchScalarGridSpec(
        num_scalar_prefetch=1, grid=(n_l, 2, per),
        in_specs=[full] * 3 + [pl.BlockSpec((1, 1), lambda l, h, i, core_ref: (0, 0))]
        + [piece(k, True) for k in range(n_l)] + [piece(k, False) for k in range(n_l)],
        out_specs=[full] * 4)
    return pl.pallas_call(
        body, name=name, grid_spec=grid_spec, out_shape=[jax.ShapeDtypeStruct(w.shape, F32)] * 4,
        compiler_params=_params(("parallel", "parallel", "parallel")),
    )(core, w, m, v, order, *own, *recv)


def rope_tables(n_tok):
    inv = ROPE_THETA ** (-jnp.arange(ROPE_HALF, dtype=F32) * 2.0 / (2 * ROPE_HALF))
    ang = jnp.arange(n_tok, dtype=F32)[:, None] * inv[None, :]
    cos, sin = jnp.cos(ang), jnp.sin(ang)
    rest = HEAD_DIM - 2 * ROPE_HALF
    one, zero, z8 = jnp.ones((n_tok, rest), F32), jnp.zeros((n_tok, rest), F32), jnp.zeros((n_tok, ROPE_HALF), F32)
    c = jnp.concatenate([cos, cos, one], axis=1)
    s1 = jnp.concatenate([-sin, z8, zero], axis=1)
    s2 = jnp.concatenate([z8, sin, zero], axis=1)
    two = lambda t: jnp.concatenate([t, t], axis=1)
    return two(c), two(s1), two(s2)


def rope_apply(x, tabs, n_col_blocks, inverse, out_dtype, name):
    n_tok = x.shape[0]
    tr = _rows(n_tok, LANES * 4 * 6)

    def body(x_ref, c_ref, s1_ref, s2_ref, o_ref):
        x = x_ref[...].astype(F32)
        if inverse:
            out = x * c_ref[...] + pltpu.roll(x * s1_ref[...], ROPE_HALF, 1) + pltpu.roll(x * s2_ref[...], LANES - ROPE_HALF, 1)
        else:
            out = x * c_ref[...] + pltpu.roll(x, LANES - ROPE_HALF, 1) * s1_ref[...] + pltpu.roll(x, ROPE_HALF, 1) * s2_ref[...]
        o_ref[...] = out.astype(o_ref.dtype)

    tab_spec = pl.BlockSpec((tr, LANES), lambda i, c: (i, 0))
    blk = pl.BlockSpec((tr, LANES), lambda i, c: (i, c))
    return pl.pallas_call(
        body, name=name, grid=(n_tok // tr, n_col_blocks), in_specs=[blk, tab_spec, tab_spec, tab_spec], out_specs=blk,
        out_shape=jax.ShapeDtypeStruct((n_tok, n_col_blocks * LANES), out_dtype), compiler_params=_params(("parallel", "parallel")),
    )(x, *tabs)


def _swa_masks():
    rows = Q_PER_KV * BLOCK
    qi = lax.broadcasted_iota(jnp.int32, (rows, 2 * BLOCK), 0) & (BLOCK - 1)
    kj = lax.broadcasted_iota(jnp.int32, (rows, 2 * BLOCK), 1)
    rel = qi + BLOCK - kj
    band = (rel >= 0) & (rel < BLOCK)
    return jnp.where(jnp.stack([band & (kj >= BLOCK), band]), 0.0, -1e30).astype(F32)


def _swa_probs(q, k, sink, mask):
    s = lax.dot_general(q, k, (((1,), (1,)), ((), ())), preferred_element_type=F32) * (HEAD_DIM ** -0.5) + mask
    m = jnp.maximum(jnp.max(s, axis=-1, keepdims=True), sink)
    e = jnp.exp(s - m)
    es = jnp.exp(sink - m)
    inv = 1.0 / (jnp.sum(e, axis=-1, keepdims=True) + es)
    return e * inv, es * inv


def _swa_specs(n_blocks):
    q_spec = pl.BlockSpec((Q_PER_KV, BLOCK, HEAD_DIM), lambda h, n: (h, n, 0))
    prev = pl.BlockSpec((None, BLOCK, HEAD_DIM), lambda h, n: (h, jnp.maximum(n - 1, 0), 0))
    cur = pl.BlockSpec((None, BLOCK, HEAD_DIM), lambda h, n: (h, n, 0))
    sink = pl.BlockSpec((None, Q_PER_KV * BLOCK, 1), lambda h, n: (h, 0, 0))
    masks = pl.BlockSpec((2, Q_PER_KV * BLOCK, 2 * BLOCK), lambda h, n: (0, 0, 0))
    return q_spec, prev, cur, sink, masks


def swa_fwd(q, k, v, sink_rows, name):
    n_tok = q.shape[1]
    q_spec, prev, cur, sink, masks = _swa_specs(n_tok // BLOCK)

    def body(q_ref, kp_ref, kc_ref, vp_ref, vc_ref, s_ref, m_ref, o_ref):
        mask = m_ref[jnp.minimum(pl.program_id(1), 1)]
        qq = q_ref[...].reshape(Q_PER_KV * BLOCK, HEAD_DIM)
        kk = jnp.concatenate([kp_ref[...], kc_ref[...]], axis=0)
        vv = jnp.concatenate([vp_ref[...], vc_ref[...]], axis=0)
        p, _ = _swa_probs(qq, kk, s_ref[...], mask)
        o = jnp.dot(p.astype(BF16), vv, preferred_element_type=F32)
        o_ref[...] = o.reshape(Q_PER_KV, BLOCK, HEAD_DIM).astype(o_ref.dtype)

    return pl.pallas_call(
        body, name=name, grid=(N_KV_HEADS, n_tok // BLOCK), in_specs=[q_spec, prev, cur, prev, cur, sink, masks], out_specs=q_spec,
        out_shape=jax.ShapeDtypeStruct(q.shape, BF16), compiler_params=_params(("parallel", "parallel")),
    )(q, k, k, v, v, sink_rows, _swa_masks())


def swa_bwd(q, k, v, sink_rows, do, name):
    n_tok = q.shape[1]
    nb = n_tok // BLOCK
    q_spec, prev, cur, sink, masks = _swa_specs(nb)
    rows = Q_PER_KV * BLOCK

    def body(q_ref, kp_ref, kc_ref, vp_ref, vc_ref, s_ref, m_ref, do_ref, dq_ref, dkp_ref, dkc_ref, dvp_ref, dvc_ref, ds_ref):
        mask = m_ref[jnp.minimum(pl.program_id(1), 1)]
        qq = q_ref[...].reshape(rows, HEAD_DIM)
        dd = do_ref[...].reshape(rows, HEAD_DIM)
        kk = jnp.concatenate([kp_ref[...], kc_ref[...]], axis=0)
        vv = jnp.concatenate([vp_ref[...], vc_ref[...]], axis=0)
        p, ps = _swa_probs(qq, kk, s_ref[...], mask)
        dp = lax.dot_general(dd, vv, (((1,), (1,)), ((), ())), preferred_element_type=F32)
        delta = jnp.sum(p * dp, axis=-1, keepdims=True)
        ds = (p * (dp - delta) * (HEAD_DIM ** -0.5)).astype(BF16)
        dq = jnp.dot(ds, kk, preferred_element_type=F32)
        dk = lax.dot_general(ds, qq, (((0,), (0,)), ((), ())), preferred_element_type=F32)
        dv = lax.dot_general(p.astype(BF16), dd, (((0,), (0,)), ((), ())), preferred_element_type=F32)
        dq_ref[...] = dq.reshape(Q_PER_KV, BLOCK, HEAD_DIM).astype(dq_ref.dtype)
        dkp_ref[...] = dk[:BLOCK]
        dkc_ref[...] = dk[BLOCK:]
        dvp_ref[...] = dv[:BLOCK]
        dvc_ref[...] = dv[BLOCK:]
        dsink = jnp.broadcast_to(-ps * delta, (rows, LANES)).reshape(Q_PER_KV, BLOCK, LANES)
        ds_ref[...] = jnp.sum(dsink, axis=1)

    part = pl.BlockSpec((None, None, BLOCK, HEAD_DIM), lambda h, n: (h, n, 0, 0))
    part_shape = jax.ShapeDtypeStruct((N_KV_HEADS, nb, BLOCK, HEAD_DIM), F32)
    return pl.pallas_call(
        body, name=name, grid=(N_KV_HEADS, nb), in_specs=[q_spec, prev, cur, prev, cur, sink, masks, q_spec],
        out_specs=[q_spec, part, part, part, part, pl.BlockSpec((None, None, Q_PER_KV, LANES), lambda h, n: (h, n, 0, 0))],
        out_shape=[jax.ShapeDtypeStruct(q.shape, BF16), part_shape, part_shape, part_shape, part_shape,
                   jax.ShapeDtypeStruct((N_KV_HEADS, nb, Q_PER_KV, LANES), F32)],
        compiler_params=_params(("parallel", "parallel")),
    )(q, k, k, v, v, sink_rows, _swa_masks(), do)


def _to_heads(t, n_heads):
    return t.reshape(t.shape[0], n_heads, HEAD_DIM).transpose(1, 0, 2)


def _from_heads(t):
    return t.transpose(1, 0, 2).reshape(t.shape[1], -1)


def _fold_kv_grad(prev, cur):
    shifted = jnp.concatenate([prev[:, 1:], jnp.zeros_like(prev[:, :1])], axis=1)
    tot = (cur + shifted).reshape(N_KV_HEADS, -1, HEAD_DIM)
    return _from_heads(tot)


def _x_probs(qh, kh):
    s = lax.dot_general(qh, kh, (((1,), (1,)), ((), ())), preferred_element_type=F32) * (X_HEAD_DIM ** -0.5)
    e = jnp.exp(s - jnp.max(s, axis=-1, keepdims=True))
    return e * (1.0 / jnp.sum(e, axis=-1, keepdims=True))


def xattn_fwd(q, k, v, name):
    n_tok, width = q.shape
    n_mem = k.shape[0]
    tq = _pick(n_tok, 512, 16)

    def body(q_ref, k_ref, v_ref, o_ref):
        for h in range(X_HEADS):
            cols = slice(h * X_HEAD_DIM, (h + 1) * X_HEAD_DIM)
            p = _x_probs(q_ref[:, cols], k_ref[:, cols])
            o_ref[:, cols] = jnp.dot(p.astype(BF16), v_ref[:, cols], preferred_element_type=F32).astype(o_ref.dtype)

    row = pl.BlockSpec((tq, width), lambda i: (i, 0))
    mem = pl.BlockSpec((n_mem, width), lambda i: (0, 0))
    return pl.pallas_call(
        body, name=name, grid=(n_tok // tq,), in_specs=[row, mem, mem], out_specs=row,
        out_shape=jax.ShapeDtypeStruct(q.shape, BF16), compiler_params=_params(("parallel",)),
    )(q, k, v)


def xattn_bwd(q, k, v, do, name):
    n_tok, width = q.shape
    n_mem = k.shape[0]
    tq = _pick(n_tok, 512, 16)

    def body(q_ref, k_ref, v_ref, do_ref, dq_ref, dk_ref, dv_ref):
        @pl.when(pl.program_id(0) == 0)
        def _():
            dk_ref[...] = jnp.zeros_like(dk_ref)
            dv_ref[...] = jnp.zeros_like(dv_ref)

        for h in range(X_HEADS):
            cols = slice(h * X_HEAD_DIM, (h + 1) * X_HEAD_DIM)
            qh, kh, vh, dh = q_ref[:, cols], k_ref[:, cols], v_ref[:, cols], do_ref[:, cols]
            p = _x_probs(qh, kh)
            dp = lax.dot_general(dh, vh, (((1,), (1,)), ((), ())), preferred_element_type=F32)
            delta = jnp.sum(p * dp, axis=-1, keepdims=True)
            ds = (p * (dp - delta) * (X_HEAD_DIM ** -0.5)).astype(BF16)
            dq_ref[:, cols] = jnp.dot(ds, kh, preferred_element_type=F32).astype(dq_ref.dtype)
            dk_ref[:, cols] += lax.dot_general(ds, qh, (((0,), (0,)), ((), ())), preferred_element_type=F32)
            dv_ref[:, cols] += lax.dot_general(p.astype(BF16), dh, (((0,), (0,)), ((), ())), preferred_element_type=F32)

    row = pl.BlockSpec((tq, width), lambda i: (i, 0))
    mem = pl.BlockSpec((n_mem, width), lambda i: (0, 0))
    return pl.pallas_call(
        body, name=name, grid=(n_tok // tq,), in_specs=[row, mem, mem, row], out_specs=[row, mem, mem],
        out_shape=[jax.ShapeDtypeStruct(q.shape, BF16), jax.ShapeDtypeStruct(k.shape, F32), jax.ShapeDtypeStruct(k.shape, F32)],
        compiler_params=_params(("arbitrary",)),
    )(q, k, v, do)


GELU_C = 0.7978845608028654
GELU_A = 0.044715


def _gelu_parts(x):
    x2 = x * x
    t = jnp.tanh(GELU_C * x * (1.0 + GELU_A * x2))
    y = 0.5 * x * (1.0 + t)
    dy = 0.5 * (1.0 + t) + 0.5 * x * (1.0 - t * t) * GELU_C * (1.0 + 3.0 * GELU_A * x2)
    return y, dy


def _sgu_norm(v, ln_g, ln_b):
    mu = jnp.mean(v, axis=-1, keepdims=True)
    vc = v - mu
    r = lax.rsqrt(jnp.mean(vc * vc, axis=-1, keepdims=True) + EPS)
    xhat = vc * r
    return xhat * ln_g + ln_b, xhat, r


def _causal_weights(w_ref):
    i = lax.broadcasted_iota(jnp.int32, (BLOCK, BLOCK), 0)
    j = lax.broadcasted_iota(jnp.int32, (BLOCK, BLOCK), 1)
    return [jnp.where(i >= j, w_ref[g], 0.0).astype(BF16) for g in range(SGU_GROUPS)]


def sgu_fwd(u_pre, v_pre, ln_g, ln_b, w_s, bias_rows, name):
    n_tok = u_pre.shape[0]
    tm = _pick(n_tok, 512, BLOCK)

    def body(u_ref, v_ref, g_ref, b_ref, w_ref, bb_ref, o_ref):
        vn, _, _ = _sgu_norm(_gelu_parts(v_ref[...])[0], g_ref[...], b_ref[...])
        vn = vn.astype(BF16)
        wc = _causal_weights(w_ref)
        for c in range(tm // BLOCK):
            rows = slice(c * BLOCK, (c + 1) * BLOCK)
            for g in range(SGU_GROUPS):
                cols = slice(g * LANES, (g + 1) * LANES)
                mixed = jnp.dot(wc[g], vn[rows, cols], preferred_element_type=F32) + bb_ref[g]
                u = _gelu_parts(u_ref[rows, cols])[0]
                o_ref[rows, cols] = (u * mixed).astype(o_ref.dtype)

    row = pl.BlockSpec((tm, SGU_WIDTH), lambda i: (i, 0))
    vec = pl.BlockSpec((1, SGU_WIDTH), lambda i: (0, 0))
    mat = pl.BlockSpec((SGU_GROUPS, BLOCK, LANES), lambda i: (0, 0, 0))
    return pl.pallas_call(
        body, name=name, grid=(n_tok // tm,), in_specs=[row, row, vec, vec, mat, mat], out_specs=row,
        out_shape=jax.ShapeDtypeStruct((n_tok, SGU_WIDTH), BF16), compiler_params=_params(("parallel",)),
    )(u_pre, v_pre, ln_g, ln_b, w_s, bias_rows)


def sgu_bwd(u_pre, v_pre, ln_g, ln_b, w_s, bias_rows, dgate, name):
    n_tok = u_pre.shape[0]
    tm = _pick(n_tok, 512, BLOCK)

    def body(u_ref, v_ref, g_ref, b_ref, w_ref, bb_ref, dg_ref, du_ref, dv_ref, dw_ref, db_ref, dlg_ref, dlb_ref, dvn_ref):
        @pl.when(pl.program_id(0) == 0)
        def _():
            dw_ref[...] = jnp.zeros_like(dw_ref)
            db_ref[...] = jnp.zeros_like(db_ref)
            dlg_ref[...] = jnp.zeros_like(dlg_ref)
            dlb_ref[...] = jnp.zeros_like(dlb_ref)

        gv, dgv = _gelu_parts(v_ref[...])
        vn, xhat, r = _sgu_norm(gv, g_ref[...], b_ref[...])
        vn = vn.astype(BF16)
        wc = _causal_weights(w_ref)
        for c in range(tm // BLOCK):
            rows = slice(c * BLOCK, (c + 1) * BLOCK)
            for g in range(SGU_GROUPS):
                cols = slice(g * LANES, (g + 1) * LANES)
                vt = vn[rows, cols]
                mixed = jnp.dot(wc[g], vt, preferred_element_type=F32) + bb_ref[g]
                u, du_dpre = _gelu_parts(u_ref[rows, cols])
                dgate_t = dg_ref[rows, cols].astype(F32)
                du_ref[rows, cols] = (dgate_t * mixed * du_dpre).astype(du_ref.dtype)
                dmix = dgate_t * u
                dmix_b = dmix.astype(BF16)
                db_ref[g] += dmix
                dw_ref[g] += lax.dot_general(dmix_b, vt, (((1,), (1,)), ((), ())), preferred_element_type=F32)
                dvn_ref[rows, cols] = lax.dot_general(wc[g], dmix_b, (((0,), (0,)), ((), ())), preferred_element_type=F32)
        dvn = dvn_ref[...]
        dlg_ref[...] += _rsum8(dvn * xhat)
        dlb_ref[...] += _rsum8(dvn)
        dxhat = dvn * g_ref[...]
        dgv_in = r * (dxhat - jnp.mean(dxhat, axis=-1, keepdims=True) - xhat * jnp.mean(dxhat * xhat, axis=-1, keepdims=True))
        dv_ref[...] = (dgv_in * dgv).astype(dv_ref.dtype)

    row = pl.BlockSpec((tm, SGU_WIDTH), lambda i: (i, 0))
    vec = pl.BlockSpec((1, SGU_WIDTH), lambda i: (0, 0))
    mat = pl.BlockSpec((SGU_GROUPS, BLOCK, LANES), lambda i: (0, 0, 0))
    part = pl.BlockSpec((8, SGU_WIDTH), lambda i: (0, 0))
    mat_shape = jax.ShapeDtypeStruct((SGU_GROUPS, BLOCK, LANES), F32)
    part_shape = jax.ShapeDtypeStruct((8, SGU_WIDTH), F32)
    act_shape = jax.ShapeDtypeStruct((n_tok, SGU_WIDTH), BF16)
    return pl.pallas_call(
        body, name=name, grid=(n_tok // tm,), in_specs=[row, row, vec, vec, mat, mat, row],
        out_specs=[row, row, mat, mat, part, part], out_shape=[act_shape, act_shape, mat_shape, mat_shape, part_shape, part_shape],
        scratch_shapes=[pltpu.VMEM((tm, SGU_WIDTH), F32)], compiler_params=_params(("arbitrary",)),
    )(u_pre, v_pre, ln_g, ln_b, w_s, bias_rows, dgate)


def _pool_tile(n_tok):
    return _pick(n_tok, 256, POOL_HALO)


def pool_fwd(h, name):
    n_tok, width = h.shape
    gw = width // len(POOL_WINDOWS)
    tm = _pool_tile(n_tok)
    per = tm // POOL_HALO

    def body(cur_ref, halo_ref, o_ref, buf_ref):
        i = pl.program_id(0)
        buf_ref[0:POOL_HALO, :] = jnp.where(i > 0, halo_ref[...], 0.0)
        buf_ref[POOL_HALO:, :] = cur_ref[...]
        tok = i * tm + lax.broadcasted_iota(jnp.int32, (tm, 1), 0)
        for g, w in enumerate(POOL_WINDOWS):
            cols = slice(g * gw, (g + 1) * gw)
            acc = buf_ref[POOL_HALO:, cols]
            for j in range(1, w):
                acc = acc + buf_ref[POOL_HALO - j:POOL_HALO - j + tm, cols]
            cnt = jnp.minimum(tok + 1, w).astype(F32)
            o_ref[:, cols] = (acc / cnt - cur_ref[:, cols]).astype(o_ref.dtype)

    return pl.pallas_call(
        body, name=name, grid=(n_tok // tm,),
        in_specs=[pl.BlockSpec((tm, width), lambda i: (i, 0)),
                  pl.BlockSpec((POOL_HALO, width), lambda i: (jnp.maximum(i * per - 1, 0), 0))],
        out_specs=pl.BlockSpec((tm, width), lambda i: (i, 0)), out_shape=jax.ShapeDtypeStruct(h.shape, BF16),
        scratch_shapes=[pltpu.VMEM((tm + POOL_HALO, width), F32)], compiler_params=_params(("parallel",)),
    )(h, h)


def pool_bwd(dp, name):
    n_tok, width = dp.shape
    gw = width // len(POOL_WINDOWS)
    tm = _pool_tile(n_tok)
    per = tm // POOL_HALO
    n_steps = n_tok // tm

    def body(cur_ref, halo_ref, o_ref, buf_ref):
        i = pl.program_id(0)
        tok = i * tm + lax.broadcasted_iota(jnp.int32, (tm, 1), 0)
        for g, w in enumerate(POOL_WINDOWS):
            cols = slice(g * gw, (g + 1) * gw)
            cnt = jnp.minimum(tok + 1, w).astype(F32)
            buf_ref[0:tm, cols] = cur_ref[:, cols] / cnt
            buf_ref[tm:, cols] = jnp.where(i < n_steps - 1, halo_ref[:, cols] / float(w), 0.0)
        for g, w in enumerate(POOL_WINDOWS):
            cols = slice(g * gw, (g + 1) * gw)
            acc = buf_ref[0:tm, cols]
            for j in range(1, w):
                acc = acc + buf_ref[j:j + tm, cols]
            o_ref[:, cols] = acc - cur_ref[:, cols]

    return pl.pallas_call(
        body, name=name, grid=(n_steps,),
        in_specs=[pl.BlockSpec((tm, width), lambda i: (i, 0)),
                  pl.BlockSpec((POOL_HALO, width), lambda i: (jnp.minimum((i + 1) * per, n_tok // POOL_HALO - 1), 0))],
        out_specs=pl.BlockSpec((tm, width), lambda i: (i, 0)), out_shape=jax.ShapeDtypeStruct(dp.shape, F32),
        scratch_shapes=[pltpu.VMEM((tm + POOL_HALO, width), F32)], compiler_params=_params(("parallel",)),
    )(dp, dp)


def _ffn_fwd(x, ga, gb, wg, wu, wd, tag):
    h = rms_fwd(x, ga, BF16, f"{tag}_prenorm")
    G = _mm(h, wg, batch="map", out_dtype=BF16, name=f"{tag}_gate")
    wu = wu(G) if callable(wu) else wu
    U, A = _mm(h, wu, batch="map", extras=[G], epilogue=_swiglu_tiles, out_dtypes=[BF16] * 2, name=f"{tag}_up")
    wd = wd(A) if callable(wd) else wd
    y = _mm(A, wd, batch="reduce", name=f"{tag}_down")
    return postnorm_res(x, y, gb, 0.5, f"{tag}_postnorm"), (x, h, G, U, A, y, wg, wu, wd)


def _ffn_bwd(res, ga, gb, dx2, tag, early):
    x, h, G, U, A, y, wg, wu, wd = res
    dy, dgb = rms_bwd(y, gb, [dx2], 0.5, None, BF16, f"{tag}_postnorm_bwd")
    dG, dU = _mm(dy, wd, tb=True, batch="map", extras=[G, U], epilogue=_swiglu_bwd_tiles, out_dtypes=[BF16] * 2, name=f"{tag}_down_dx")
    dwd = _mm(A, dy, ta=True, batch="map", name=f"{tag}_down_dw", out_dtype=BF16)
    dwg = _mm(h, dG, ta=True, batch="map", name=f"{tag}_gate_dw", out_dtype=BF16, order=early("wd", dwd))
    dwu = _mm(h, dU, ta=True, batch="map", name=f"{tag}_up_dw", out_dtype=BF16, order=early("wg", dwg))
    dh = _mm(dG, wg, tb=True, batch="reduce", more=[(dU, wu)], name=f"{tag}_gate_up_dx", order=early("wu", dwu))
    dx, dga = rms_bwd(x, ga, [dh], 1.0, dx2, F32, f"{tag}_prenorm_bwd")
    return dx, dga, dgb, dwg, dwu, dwd


def _sink_rows(sinks):
    return jnp.repeat(sinks.reshape(N_KV_HEADS, Q_PER_KV), BLOCK, axis=1)[..., None]


def _attn_sgu_fwd(x, g_pre, g_post, w_in, w_out, sinks, ln_g, ln_b, sgu_w, bias_rows, tabs, tag):
    h = rms_fwd(x, g_pre, BF16, f"{tag}_prenorm")
    z = _mm(h, w_in, name=f"{tag}_in")
    qk = rope_apply(z, tabs, QK_WIDTH // LANES, False, BF16, f"{tag}_rope")
    q = _to_heads(qk[:, :ATTN_WIDTH], N_Q_HEADS)
    k = _to_heads(qk[:, ATTN_WIDTH:], N_KV_HEADS)
    v = _to_heads(z[:, QK_WIDTH:QK_WIDTH + KV_WIDTH].astype(BF16), N_KV_HEADS)
    o = swa_fwd(q, k, v, _sink_rows(sinks), f"{tag}_swa")
    u_pre = z[:, QK_WIDTH + KV_WIDTH:QK_WIDTH + KV_WIDTH + SGU_WIDTH]
    v_pre = z[:, QK_WIDTH + KV_WIDTH + SGU_WIDTH:]
    gate = sgu_fwd(u_pre, v_pre, ln_g, ln_b, sgu_w, bias_rows, f"{tag}_sgu")
    cat = jnp.concatenate([_from_heads(o), gate], axis=1)
    m = _mm(cat, w_out, name=f"{tag}_out")
    return postnorm_res(x, m, g_post, 1.0, f"{tag}_postnorm"), (x, h, q, k, v, u_pre, v_pre, cat, m)


def _attn_sgu_bwd(res, g_pre, g_post, w_in, w_out, sinks, ln_g, ln_b, sgu_w, bias_rows, tabs, dx2, tag):
    x, h, q, k, v, u_pre, v_pre, cat, m = res
    dm, dg_post = rms_bwd(m, g_post, [dx2], 1.0, None, BF16, f"{tag}_postnorm_bwd")
    dcat = _mm(dm, w_out, tb=True, out_dtype=BF16, name=f"{tag}_out_dx")
    dw_out = _mm(cat, dm, ta=True, name=f"{tag}_out_dw", out_dtype=BF16)
    do = _to_heads(dcat[:, :ATTN_WIDTH], N_Q_HEADS)
    dq, dkp, dkc, dvp, dvc, dsink = swa_bwd(q, k, v, _sink_rows(sinks), do, f"{tag}_swa_bwd")
    d_sinks = jnp.sum(dsink[..., 0], axis=1).reshape(1, N_Q_HEADS)
    dqk_rot = jnp.concatenate([_from_heads(dq).astype(F32), _fold_kv_grad(dkp, dkc)], axis=1)
    dqk = rope_apply(dqk_rot, tabs, QK_WIDTH // LANES, True, BF16, f"{tag}_rope_bwd")
    dv = _fold_kv_grad(dvp, dvc).astype(BF16)
    du_pre, dv_pre, dw_s, dbias, dlg, dlb = sgu_bwd(u_pre, v_pre, ln_g, ln_b, sgu_w, bias_rows, dcat[:, ATTN_WIDTH:], f"{tag}_sgu_bwd")
    dz = jnp.concatenate([dqk, dv, du_pre, dv_pre], axis=1)
    dw_in = _mm(h, dz, ta=True, name=f"{tag}_in_dw", out_dtype=BF16)
    dh = _mm(dz, w_in, tb=True, name=f"{tag}_in_dx")
    dx, dg_pre = rms_bwd(x, g_pre, [dh], 1.0, dx2, F32, f"{tag}_prenorm_bwd")
    causal = jnp.tril(jnp.ones((BLOCK, BLOCK), F32))
    small = dict(attn_sinks=d_sinks, sgu_ln_g=jnp.sum(dlg, axis=0, keepdims=True), sgu_ln_b=jnp.sum(dlb, axis=0, keepdims=True),
                 sgu_w=(dw_s * causal[None])[None], sgu_b=jnp.sum(dbias, axis=-1)[None])
    return dx, dg_pre, dg_post, dw_in, dw_out, small


def _pool_mix_fwd(x, g_pre, g_post, pool_w, pool_scale, tag):
    hf = rms_fwd(x, g_pre, F32, f"{tag}_prenorm")
    pooled = pool_fwd(hf, f"{tag}_pool")
    n_g = len(POOL_WINDOWS)
    ypre = _mm(pooled, pool_w, batch="map", groups=n_g, a_cb=True, o_cb=True, name=f"{tag}_proj")
    m = scale_cols(ypre, pool_scale, f"{tag}_scale")
    return postnorm_res(x, m, g_post, 1.0, f"{tag}_postnorm"), (x, pooled, ypre, m)


def _pool_mix_bwd(res, g_pre, g_post, pool_w, pool_scale, dx2, tag):
    x, pooled, ypre, m = res
    n_g = len(POOL_WINDOWS)
    dm, dg_post = rms_bwd(m, g_post, [dx2], 1.0, None, F32, f"{tag}_postnorm_bwd")
    dypre, dscale = scale_cols_bwd(dm, ypre, pool_scale, f"{tag}_scale_bwd")
    dpooled = _mm(dypre, pool_w, tb=True, batch="map", groups=n_g, a_cb=True, o_cb=True, name=f"{tag}_proj_dx")
    dpw = _mm(pooled, dypre, ta=True, batch="map", groups=n_g, a_cb=True, b_cb=True, name=f"{tag}_proj_dw", out_dtype=BF16)
    dhf = pool_bwd(dpooled, f"{tag}_pool_bwd")
    dx, dg_pre = rms_bwd(x, g_pre, [dhf], 1.0, dx2, F32, f"{tag}_prenorm_bwd")
    return dx, dg_pre, dg_post, dpw, jnp.sum(dscale, axis=0, keepdims=True)


def _xattn_fwd(x, mem, g_pre, g_post, g_mem, wq, wk, wv, wo, tag):
    h = rms_fwd(x, g_pre, BF16, f"{tag}_prenorm")
    mem_n = rms_fwd(mem, g_mem, BF16, f"{tag}_memnorm")
    q = _mm(h, wq, out_dtype=BF16, name=f"{tag}_q")
    k = _mm(mem_n, wk, out_dtype=BF16, name=f"{tag}_k")
    v = _mm(mem_n, wv, out_dtype=BF16, name=f"{tag}_v")
    o = xattn_fwd(q, k, v, f"{tag}_core")
    r = _mm(o, wo, name=f"{tag}_o")
    return postnorm_res(x, r, g_post, 1.0, f"{tag}_postnorm"), (x, h, mem_n, q, k, v, o, r)


def _xattn_bwd(res, mem, g_pre, g_post, g_mem, wq, wk, wv, wo, dx2, tag):
    x, h, mem_n, q, k, v, o, r = res
    dr, dg_post = rms_bwd(r, g_post, [dx2], 1.0, None, BF16, f"{tag}_postnorm_bwd")
    do = _mm(dr, wo, tb=True, out_dtype=BF16, name=f"{tag}_o_dx")
    dwo = _mm(o, dr, ta=True, name=f"{tag}_o_dw", out_dtype=BF16)
    dq, dk, dv = xattn_bwd(q, k, v, do, f"{tag}_core_bwd")
    dk, dv = dk.astype(BF16), dv.astype(BF16)
    dwq = _mm(h, dq, ta=True, name=f"{tag}_q_dw", out_dtype=BF16)
    dwk = _mm(mem_n, dk, ta=True, name=f"{tag}_k_dw", out_dtype=BF16)
    dwv = _mm(mem_n, dv, ta=True, name=f"{tag}_v_dw", out_dtype=BF16)
    dh = _mm(dq, wq, tb=True, name=f"{tag}_q_dx")
    dmem1 = _mm(dk, wk, tb=True, name=f"{tag}_k_dx")
    dmem2 = _mm(dv, wv, tb=True, name=f"{tag}_v_dx")
    _, dg_mem = rms_bwd(mem, g_mem, [dmem1, dmem2], 1.0, None, BF16, f"{tag}_memnorm_bwd")
    dx, dg_pre = rms_bwd(x, g_pre, [dh], 1.0, dx2, F32, f"{tag}_prenorm_bwd")
    return dx, dg_pre, dg_post, dg_mem, dwq, dwk, dwv, dwo


def _rowsum8(part):
    return jnp.sum(part, axis=0, keepdims=True)


def device_step(x, mem, target, norms, mem_norm, small, fetch, emit, early):
    n_tok = x.shape[0]
    tabs = rope_tables(n_tok)
    bias_rows = jnp.broadcast_to(small["sgu_b"][:, :, None], (SGU_GROUPS, BLOCK, LANES))
    gn = lambda l, i: norms[l, i][None, :]
    gm = lambda l: mem_norm[l][None, :]
    mix0 = (small["attn_sinks"], small["sgu_ln_g"], small["sgu_ln_b"], small["sgu_w"], bias_rows, tabs)

    wts, saved = {}, []
    for l in range(2):
        wts["ffn1", l], tok = fetch(("ffn1", l), x)
        x, r1 = _ffn_fwd(x, gn(l, 0) + tok, gn(l, 1), *wts["ffn1", l], f"l{l}_ffn1")
        wts["mix", l], tok = fetch(("mix", l), x)
        if l == 0:
            x, r2 = _attn_sgu_fwd(x, gn(l, 2) + tok, gn(l, 3), *wts["mix", l], *mix0, f"l{l}_mix")
        else:
            x, r2 = _pool_mix_fwd(x, gn(l, 2) + tok, gn(l, 3), *wts["mix", l], small["pool_scale"], f"l{l}_mix")
        wts["xattn", l], tok = fetch(("xattn", l), x)
        x, r3 = _xattn_fwd(x, mem, gn(l, 4) + tok, gn(l, 5), gm(l), *wts["xattn", l], f"l{l}_xattn")
        wts["ffn2", l], tok = fetch(("ffn2", l), x)
        x, r4 = _ffn_fwd(x, gn(l, 6) + tok, gn(l, 7), *wts["ffn2", l], f"l{l}_ffn2")
        saved.append((r1, r2, r3, r4))

    loss_part, dx = loss_and_grad(x, target, "loss")

    g_norm_rows = [[None] * 8, [None] * 8]
    g_mem_rows = [None, None]
    g_small = {}
    tok = 0.0
    for l in (1, 0):
        r1, r2, r3, r4 = saved[l]
        dx, g_norm_rows[l][6], g_norm_rows[l][7], dwg, dwu, dwd = _ffn_bwd(
            r4, gn(l, 6), gn(l, 7) + tok, dx, f"l{l}_ffn2", lambda part, g, l=l: early(("ffn2", l), f"ffn2_{part}", g))
        tok = emit(("ffn2", l), dict(ffn2_wg=dwg, ffn2_wu=dwu, ffn2_wd=dwd), dx)
        dx, g_norm_rows[l][4], g_norm_rows[l][5], g_mem_rows[l], dwq, dwk, dwv, dwo = _xattn_bwd(
            r3, mem, gn(l, 4), gn(l, 5) + tok, gm(l), *wts["xattn", l], dx, f"l{l}_xattn")
        tok = emit(("xattn", l), dict(x_wq=dwq, x_wk=dwk, x_wv=dwv, x_wo=dwo), dx)
        if l == 0:
            dx, g_norm_rows[l][2], g_norm_rows[l][3], dw_in, dw_out, sm = _attn_sgu_bwd(
                r2, gn(l, 2), gn(l, 3) + tok, *wts["mix", l], *mix0, dx, f"l{l}_mix")
            tok = emit(("mix", l), dict(mix_w_in=dw_in, mix_w_out=dw_out), dx)
            g_small.update(sm)
        else:
            dx, g_norm_rows[l][2], g_norm_rows[l][3], dpw, dscale = _pool_mix_bwd(
                r2, gn(l, 2), gn(l, 3) + tok, *wts["mix", l], small["pool_scale"], dx, f"l{l}_mix")
            tok = emit(("mix", l), dict(pool_w=dpw), dx)
            g_small["pool_scale"] = dscale
        dx, g_norm_rows[l][0], g_norm_rows[l][1], dwg, dwu, dwd = _ffn_bwd(
            r1, gn(l, 0), gn(l, 1) + tok, dx, f"l{l}_ffn1", lambda part, g, l=l: early(("ffn1", l), f"ffn1_{part}", g))
        tok = emit(("ffn1", l), dict(ffn1_wg=dwg, ffn1_wu=dwu, ffn1_wd=dwd), dx)
    g_norms = jnp.stack([jnp.concatenate([_rowsum8(p) for p in g_norm_rows[l]], axis=0) for l in range(2)])
    g_mem_norm = jnp.concatenate([_rowsum8(p) for p in g_mem_rows], axis=0)
    return loss_part, dx, g_small, g_norms, g_mem_norm, tok


ANY = pl.BlockSpec(memory_space=pl.ANY)


def _place():
    x, y, c = lax.axis_index("x"), lax.axis_index("y"), lax.axis_index("c")
    other_chips = [(1 - x, y), (x, 1 - y), (1 - x, 1 - y)]
    return x, y, c, other_chips


def _half_rows(core, n_rows):
    half = n_rows // 2
    return pl.ds(pl.multiple_of(core * half, 16), half)


def _remote(src, dst, send_sem, recv_sem, device):
    return pltpu.make_async_remote_copy(src_ref=src, dst_ref=dst, send_sem=send_sem, recv_sem=recv_sem,
                                        device_id=device, device_id_type=MESH)


HBM = pl.BlockSpec(memory_space=pltpu.HBM)
SEM = pl.BlockSpec(memory_space=pltpu.SEMAPHORE)
DATAFLOW = pltpu.SideEffectType.DATAFLOW_SIDE_EFFECTING


def _chip_copies(bufs, send_sems, recv_sems):
    x, y, c, chips = _place()
    me = 2 * x + y
    sends, arrivals = [], []
    for i, buf in enumerate(bufs):
        rows = _half_rows(c, buf.shape[1])
        for r, (px, py) in enumerate(chips):
            mine, theirs = buf.at[me, rows], buf.at[2 * px + py, rows]
            sends.append(_remote(mine, mine, send_sems.at[3 * i + r], recv_sems.at[3 * i + r], (px, py, c)))
            arrivals.append(_remote(theirs, theirs, send_sems.at[3 * i + r], recv_sems.at[3 * i + r], (px, py, c)))
    return sends, arrivals


def copies_start(arrays, fresh, copies, n_sems, after, name):
    operands = [pltpu.with_memory_space_constraint(a, pltpu.HBM) for a in arrays]
    operands += [pltpu.with_memory_space_constraint(lax.empty(f.shape, f.dtype), pltpu.HBM) for f in fresh]
    n = len(operands)

    def body(*refs):
        send_sems, recv_sems = refs[n + 1], refs[n + 2]
        bufs, token = refs[n + 3:2 * n + 3], refs[2 * n + 3]
        for cp in copies(bufs, send_sems, recv_sems)[0]:
            cp.start()
        token[...] = jnp.zeros_like(token)

    res = pl.pallas_call(
        body, name=name, in_specs=[HBM] * n + [ANY],
        out_specs=[SEM, SEM] + [HBM] * n + [pl.BlockSpec(memory_space=pltpu.VMEM)],
        out_shape=[pltpu.SemaphoreType.DMA((n_sems,)), pltpu.SemaphoreType.DMA((n_sems,))]
        + [pltpu.HBM(o.shape, o.dtype) for o in operands] + [jax.ShapeDtypeStruct((8, LANES), F32)],
        input_output_aliases={i: 2 + i for i in range(n)}, compiler_params=pltpu.CompilerParams(has_side_effects=DATAFLOW),
    )(*operands, after)
    return res[0], res[1], list(res[2:2 + n]), res[2 + n]


def copies_wait(send_sems, recv_sems, bufs, copies, after, name):
    n = len(bufs)

    def body(*refs):
        for cp in copies(refs[:n], refs[n], refs[n + 1])[1]:
            cp.wait_send()
            cp.wait_recv()

    return list(pl.pallas_call(
        body, name=name, in_specs=[HBM] * n + [SEM, SEM] + [ANY] * len(after), out_specs=[HBM] * n,
        out_shape=[pltpu.HBM(b.shape, b.dtype) for b in bufs], input_output_aliases={i: i for i in range(n)},
        compiler_params=pltpu.CompilerParams(has_side_effects=DATAFLOW),
    )(*bufs, send_sems, recv_sems, *after))


def _to_sibling_copies(n_src, src_rows):
    def copies(bufs, send_sems, recv_sems):
        x, y, c, _ = _place()
        sends, arrivals = [], []
        for i in range(n_src):
            src, land = bufs[i], bufs[n_src + i]
            src = src.at[:, src_rows(c, src)] if src_rows is not None else src
            sends.append(_remote(src, land, send_sems.at[i], recv_sems.at[i], (x, y, 1 - c)))
            arrivals.append(_remote(land, land, send_sems.at[i], recv_sems.at[i], (x, y, 1 - c)))
        return sends, arrivals
    return copies


def _forward_copies(bufs, send_sems, recv_sems):
    x, y, c, chips = _place()
    sends, arrivals = [], []
    for i, buf in enumerate(bufs):
        for r, (px, py) in enumerate(chips):
            mine = buf.at[2 * px + py, _half_rows(c, buf.shape[1])]
            theirs = buf.at[2 * px + py, _half_rows(1 - c, buf.shape[1])]
            sends.append(_remote(mine, mine, send_sems.at[3 * i + r], recv_sems.at[3 * i + r], (x, y, 1 - c)))
            arrivals.append(_remote(theirs, theirs, send_sems.at[3 * i + r], recv_sems.at[3 * i + r], (x, y, 1 - c)))
    return sends, arrivals


def forward_to_sibling(bufs, name):
    n = len(bufs)

    def body(*refs):
        sends, arrivals = _forward_copies(refs[n:2 * n], *refs[2 * n:])
        for cp in sends:
            cp.start()
        for cp in arrivals:
            cp.wait_recv()
        for cp in sends:
            cp.wait_send()

    return pl.pallas_call(
        body, name=name, in_specs=[ANY] * n, out_specs=[ANY] * n, input_output_aliases={i: i for i in range(n)},
        out_shape=[jax.ShapeDtypeStruct(b.shape, b.dtype) for b in bufs],
        scratch_shapes=[pltpu.SemaphoreType.DMA((3 * n,)), pltpu.SemaphoreType.DMA((3 * n,))],
    )(*bufs)


def add_own_half(g, p, core, name):
    n_j, n_rows, n_cols = g.shape
    half = n_rows // 2
    tr = _rows(half, n_cols * (2 * g.dtype.itemsize + 2), WEIGHT_TILE_BYTES)

    def body(c_ref, g_ref, p_ref, o_ref):
        o_ref[...] = (g_ref[...] + p_ref[...]).astype(o_ref.dtype)

    grid_spec = pltpu.PrefetchScalarGridSpec(
        num_scalar_prefetch=1, grid=(n_j, half // tr),
        in_specs=[pl.BlockSpec((None, None, tr, n_cols), lambda j, i, c_ref: (j, c_ref[0], i, 0)),
                  pl.BlockSpec((None, tr, n_cols), lambda j, i, c_ref: (j, i, 0))],
        out_specs=pl.BlockSpec((None, tr, n_cols), lambda j, i, c_ref: (j, i, 0)))
    return pl.pallas_call(
        body, name=name, grid_spec=grid_spec, out_shape=jax.ShapeDtypeStruct((n_j, half, n_cols), BF16),
        compiler_params=_params(("parallel", "parallel")),
    )(core, g.reshape(n_j, 2, half, n_cols), p)


def _scatter_copies(n_parts):
    def copies(bufs, send_sems, recv_sems):
        x, y, c, chips = _place()
        me = 2 * x + y
        sends, arrivals = [], []
        for i in range(n_parts):
            part, land = bufs[i], bufs[n_parts + i]
            for r, (px, py) in enumerate(chips):
                theirs = land.at[2 * px + py]
                sends.append(_remote(part.at[2 * px + py], land.at[me], send_sems.at[3 * i + r], recv_sems.at[3 * i + r], (px, py, c)))
                arrivals.append(_remote(theirs, theirs, send_sems.at[3 * i + r], recv_sems.at[3 * i + r], (px, py, c)))
        return sends, arrivals
    return copies


def sum_over_chips(own, got, chip, name):
    n_s, n_rows, n_cols = got.shape
    tr = _rows(n_rows, n_cols * (got.dtype.itemsize * (n_s + 1) + 4), WEIGHT_TILE_BYTES)

    def body(chip_ref, own_ref, *refs):
        got_refs, o_ref = refs[:n_s], refs[n_s]
        me = chip_ref[0]
        acc = jnp.where(me == 0, own_ref[...], got_refs[0][...]).astype(F32)
        for k in range(1, n_s):
            acc = acc + jnp.where(me == k, own_ref[...], got_refs[k][...]).astype(F32)
        o_ref[...] = acc

    def slot(k):
        return pl.BlockSpec((None, tr, n_cols), lambda i, chip_ref: (jnp.where(chip_ref[0] == k, (k + 1) % n_s, k), i, 0))

    grid_spec = pltpu.PrefetchScalarGridSpec(
        num_scalar_prefetch=1, grid=(n_rows // tr,),
        in_specs=[pl.BlockSpec((None, tr, n_cols), lambda i, chip_ref: (chip_ref[0], i, 0))] + [slot(k) for k in range(n_s)],
        out_specs=pl.BlockSpec((tr, n_cols), lambda i, chip_ref: (i, 0)))
    return pl.pallas_call(
        body, name=name, grid_spec=grid_spec, out_shape=jax.ShapeDtypeStruct((n_rows, n_cols), F32),
        compiler_params=_params(("parallel",)),
    )(chip, own, *([got] * n_s))


def sum_slots(q, name):
    n_s, n_rows, n_cols = q.shape
    tr = _rows(n_rows, n_cols * (q.dtype.itemsize * n_s + 4))

    def body(q_ref, o_ref):
        acc = q_ref[0].astype(F32)
        for s in range(1, n_s):
            acc = acc + q_ref[s].astype(F32)
        o_ref[...] = acc

    return pl.pallas_call(
        body, name=name, grid=(n_rows // tr,), in_specs=[pl.BlockSpec((n_s, tr, n_cols), lambda i: (0, i, 0))],
        out_specs=pl.BlockSpec((tr, n_cols), lambda i: (i, 0)), out_shape=jax.ShapeDtypeStruct((n_rows, n_cols), F32),
        compiler_params=_params(("parallel",)),
    )(q)


def gather_all_devices(s, name):
    def body(s_ref, o_ref, send_sems, recv_sems, local_sem):
        x, y, c, _ = _place()
        me = 4 * x + 2 * y + c
        local = pltpu.make_async_copy(s_ref, o_ref.at[me], local_sem)
        local.start()
        sends = []
        for f in range(1, N_DEV):
            px, py, pc = x ^ (f >> 2), y ^ ((f >> 1) & 1), c ^ (f & 1)
            sends.append(_remote(s_ref, o_ref.at[me], send_sems.at[f - 1], recv_sems.at[f - 1], (px, py, pc)))
            sends[-1].start()
        for f in range(1, N_DEV):
            px, py, pc = x ^ (f >> 2), y ^ ((f >> 1) & 1), c ^ (f & 1)
            landed = o_ref.at[4 * px + 2 * py + pc]
            _remote(landed, landed, send_sems.at[f - 1], recv_sems.at[f - 1], (px, py, pc)).wait_recv()
        for cp in sends:
            cp.wait_send()
        local.wait()

    return pl.pallas_call(
        body, name=name, in_specs=[ANY], out_specs=ANY, out_shape=jax.ShapeDtypeStruct((N_DEV,) + s.shape, s.dtype),
        scratch_shapes=[pltpu.SemaphoreType.DMA((N_DEV - 1,)), pltpu.SemaphoreType.DMA((N_DEV - 1,)), pltpu.SemaphoreType.DMA],
    )(s)


FFN_NAMES = ("ffn1_wg", "ffn1_wu", "ffn1_wd", "ffn2_wg", "ffn2_wu", "ffn2_wd")
XATTN_NAMES = ("x_wq", "x_wk", "x_wv", "x_wo")
BIG_NAMES = FFN_NAMES + XATTN_NAMES + ("mix_w_in", "mix_w_out", "pool_w")
SMALL_NAMES = ("norms", "mem_norm", "attn_sinks", "sgu_ln_g", "sgu_ln_b", "sgu_w", "sgu_b", "pool_scale")
WEIGHT_ORDER = ("norms", "mem_norm") + BIG_NAMES[:-1] + ("attn_sinks", "sgu_ln_g", "sgu_ln_b", "sgu_w", "sgu_b", "pool_w", "pool_scale")
COLUMN_CUT = ("x_wo", "mix_w_in")
N_POOL = len(POOL_WINDOWS)
BLOCK_ORDER = (("ffn1", 0), ("mix", 0), ("xattn", 0), ("ffn2", 0), ("ffn1", 1), ("mix", 1), ("xattn", 1), ("ffn2", 1))
PREFETCH_AT = ((1, 2, 3), (), (), (4,), (5, 6, 7), (), (), ())


def block_weight_names(kind, layer):
    if kind == "mix":
        return ("mix_w_in", "mix_w_out") if layer == 0 else ("pool_w",)
    return XATTN_NAMES if kind == "xattn" else tuple(f"{kind}_{part}" for part in ("wg", "wu", "wd"))


def _to_matmul_layout(name, g):
    n_j, n_rows, n_cols = g.shape
    if name in COLUMN_CUT:
        return g.transpose(1, 0, 2).reshape(n_rows, n_j * n_cols)
    if name == "pool_w":
        return g.reshape(n_j, N_POOL, n_rows // N_POOL, n_cols).transpose(1, 0, 2, 3).reshape(N_POOL, n_j * n_rows // N_POOL, n_cols)
    if name in ("x_wq", "x_wk", "x_wv", "mix_w_out"):
        return g.reshape(n_j * n_rows, n_cols)
    return g


def _from_matmul_layout(name, d):
    if name in COLUMN_CUT:
        n_rows, wide = d.shape
        return d.reshape(n_rows, N_CHIPS, wide // N_CHIPS).transpose(1, 0, 2)
    if name == "pool_w":
        n_g, n_in, n_cols = d.shape
        return d.reshape(n_g, N_CHIPS, n_in // N_CHIPS, n_cols).transpose(1, 0, 2, 3).reshape(N_CHIPS, n_g * n_in // N_CHIPS, n_cols)
    if name in ("x_wq", "x_wk", "x_wv", "mix_w_out"):
        return d.reshape(N_CHIPS, d.shape[0] // N_CHIPS, d.shape[1])
    return d


def _as3(w):
    return w.reshape(w.shape[0], -1, w.shape[-1])


def _pack(arrays, row_multiple):
    flat = jnp.concatenate([a.reshape(-1) for a in arrays])
    per = LANES * row_multiple
    total = -(-flat.shape[0] // per) * per
    return jnp.pad(flat, (0, total - flat.shape[0])).reshape(total // LANES, LANES)


def _unpack(packed, like):
    flat, out, at = packed.reshape(-1), [], 0
    for a in like:
        out.append(flat[at:at + a.size].reshape(a.shape))
        at += a.size
    return out


def kernel(x, mem, norms, mem_norm, ffn1_wg, ffn1_wu, ffn1_wd, ffn2_wg, ffn2_wu, ffn2_wd, x_wq, x_wk, x_wv, x_wo, mix_w_in, mix_w_out, attn_sinks, sgu_ln_g, sgu_ln_b, sgu_w, sgu_b, pool_w, pool_scale, loss_target, m_norms, m_mem_norm, m_ffn1_wg, m_ffn1_wu, m_ffn1_wd, m_ffn2_wg, m_ffn2_wu, m_ffn2_wd, m_x_wq, m_x_wk, m_x_wv, m_x_wo, m_mix_w_in, m_mix_w_out, m_attn_sinks, m_sgu_ln_g, m_sgu_ln_b, m_sgu_w, m_sgu_b, m_pool_w, m_pool_scale, v_norms, v_mem_norm, v_ffn1_wg, v_ffn1_wu, v_ffn1_wd, v_ffn2_wg, v_ffn2_wu, v_ffn2_wd, v_x_wq, v_x_wk, v_x_wv, v_x_wo, v_mix_w_in, v_mix_w_out, v_attn_sinks, v_sgu_ln_g, v_sgu_ln_b, v_sgu_w, v_sgu_b, v_pool_w, v_pool_scale):
    given = dict(locals())
    w = {n: given[n] for n in WEIGHT_ORDER}
    mom = {n: given["m_" + n] for n in WEIGHT_ORDER}
    var = {n: given["v_" + n] for n in WEIGHT_ORDER}
    chip_id = 2 * lax.axis_index("x") + lax.axis_index("y")
    chip = chip_id.astype(jnp.int32).reshape(1)
    core = lax.axis_index("c").astype(jnp.int32).reshape(1)
    n_shard = norms.shape[-1]

    keys = [(name, l) for name in BIG_NAMES for l in range(_as3(w[name]).shape[0])]
    first_keys = [(name, 0) for name in block_weight_names(*BLOCK_ORDER[0])]
    cast = lambda name, l: cast_into_slot(_as3(w[name]), l, chip, f"cast_{name}")
    slot_of = {key: cast(*key) for key in first_keys}
    small_rows = jnp.concatenate([norms.reshape(-1, n_shard), pool_scale, jnp.zeros((15, n_shard), F32)], axis=0)
    small_slot = lax.dynamic_update_slice_in_dim(jnp.zeros((N_CHIPS,) + small_rows.shape, F32), small_rows[None], chip_id, axis=0)
    pending = {}
    gather_start = lambda slots, after, name: copies_start(slots, [], _chip_copies, 3 * len(slots), after, name)
    pending["0a"] = gather_start([slot_of[first_keys[0]], small_slot], chip, "gather_start_0a")
    pending["0b"] = gather_start([slot_of[first_keys[1]]], pending["0a"][2][0], "gather_start_0b")
    pending["0c"] = gather_start([slot_of[first_keys[2]]], pending["0b"][2][0], "gather_start_0c")
    for key in keys:
        if key not in slot_of:
            slot_of[key] = cast(*key)


    def block_slots(k):
        kind, layer = BLOCK_ORDER[k]
        return [slot_of[name, layer if kind != "mix" else 0] for name in block_weight_names(kind, layer)]

    def start(k, after):
        pending[k] = gather_start(block_slots(k), after, f"gather_start_{k}")

    def finish(k, after):
        send_sems, recv_sems, bufs, _ = pending[k]
        return forward_to_sibling(copies_wait(send_sems, recv_sems, bufs, _chip_copies, [after], f"gather_wait_{k}"), f"gather_forward_{k}")

    def fetch(key, after):
        k = BLOCK_ORDER.index(key)
        names = block_weight_names(*key)
        token = 0.0
        if k == 0:
            late = lambda part, name: lambda after: _to_matmul_layout(name, finish(part, after)[0])
            bufs = [first_gate]
            weights = (_to_matmul_layout(names[0], first_gate), late("0b", names[1]), late("0c", names[2]))
        elif key[0] in ("ffn1", "ffn2"):
            send_sems, recv_sems, bufs, _ = pending[k]
            bufs = copies_wait(send_sems, recv_sems, bufs, _chip_copies, [after], f"gather_wait_{k}")
            gate, = forward_to_sibling(bufs[:1], f"gather_forward_{k}_gate")
            send_sems, recv_sems, rest, begun = copies_start(bufs[1:], [], _forward_copies, 6, gate, f"gather_forward_{k}_start")
            landed = []

            def late(i, name):
                def get(after):
                    if not landed:
                        landed.extend(copies_wait(send_sems, recv_sems, rest, _forward_copies, [after], f"gather_forward_{k}_wait"))
                    return _to_matmul_layout(name, landed[i])
                return get
            bufs = [gate]
            weights = (_to_matmul_layout(names[0], gate), late(0, names[1]), late(1, names[2]))
            token = begun[0, 0]
        else:
            bufs = finish(k, after)
            weights = tuple(_to_matmul_layout(name, b) for name, b in zip(names, bufs))
        for ahead in PREFETCH_AT[k]:
            start(ahead, bufs[0])
            token = token + pending[ahead][3][0, 0]
        return weights, token

    to_sibling, to_chips, halves, own_of, recv_of = [], [], [], {}, {}
    other_half = lambda c, src: _half_rows(1 - c, src.shape[1])

    def sibling_start(tag, names, full, after):
        lands = [jax.ShapeDtypeStruct((g.shape[0], g.shape[1] // 2, g.shape[2]), g.dtype) for g in full]
        send_sems, recv_sems, bufs, token = copies_start(full, lands, _to_sibling_copies(len(full), other_half), len(full), after,
                                                         f"grads_to_sibling_{tag}")
        to_sibling.append((tag, names, send_sems, recv_sems, bufs))
        return token

    def early(key, name, g):
        kind, layer = key
        return sibling_start(f"{name}{layer}", [(name, layer)], [_from_matmul_layout(name, g)], chip)

    def chips_start(tag, after):
        names, chip_sums = [], []
        for part, part_names, send_sems, recv_sems, bufs in to_sibling:
            n = len(part_names)
            bufs = copies_wait(send_sems, recv_sems, bufs, _to_sibling_copies(n, other_half), after, f"grads_from_sibling_{part}")
            chip_sums += [add_own_half(g, p, core, f"chip_sum_{name}") for (name, _), g, p in zip(part_names, bufs[:n], bufs[n:])]
            names += part_names
        to_sibling.clear()
        n = len(names)
        send_sems, recv_sems, bufs, token = copies_start(chip_sums, chip_sums, _scatter_copies(n), 3 * n, after[0], f"grads_start_{tag}")
        to_chips.append((tag, names, send_sems, recv_sems, bufs))
        return token[0, 0]

    def chips_wait(after):
        tag, names, send_sems, recv_sems, bufs = to_chips.pop()
        return tag, names, copies_wait(send_sems, recv_sems, bufs, _scatter_copies(len(names)), after, f"grads_wait_{tag}")

    def reduce_landed(tag, names, bufs):
        n = len(names)
        own = [sum_over_chips(t, q, chip, f"sum_{name}") for (name, _), t, q in zip(names, bufs[:n], bufs[n:])]
        send_sems, recv_sems, bufs, _ = copies_start(own, own, _to_sibling_copies(n, None), n, chip, f"grads_halves_{tag}")
        halves.append((tag, names, send_sems, recv_sems, bufs))

    def halves_land(after):
        for tag, names, send_sems, recv_sems, bufs in halves:
            n = len(names)
            bufs = copies_wait(send_sems, recv_sems, bufs, _to_sibling_copies(n, None), after, f"grads_halves_wait_{tag}")
            for key, o, r in zip(names, bufs[:n], bufs[n:]):
                own_of[key], recv_of[key] = o, r
        halves.clear()

    def emit(key, grads_of, after):
        kind, layer = key
        landed = chips_wait([after]) if to_chips else None
        begun = {name for _, part_names, _, _, _ in to_sibling for name, _ in part_names}
        rest = {name: g for name, g in grads_of.items() if name not in begun}
        if rest:
            sibling_start(f"{kind}{layer}", [(name, layer if kind != "mix" else 0) for name in rest],
                          [_from_matmul_layout(name, g) for name, g in rest.items()], after)
        token = chips_start(f"{kind}{layer}", [after])
        if landed:
            reduce_landed(*landed)
        return token

    first_gate, small_all = finish("0a", pending["0c"][2][0])
    n_norm_rows = norms.shape[0] * norms.shape[1]
    norms_all = jnp.concatenate([small_all[j, :n_norm_rows].reshape(norms.shape) for j in range(N_CHIPS)], axis=-1)
    pool_scale_all = jnp.concatenate([small_all[j, n_norm_rows:n_norm_rows + 1] for j in range(N_CHIPS)], axis=-1)
    small = dict(attn_sinks=attn_sinks[0], sgu_ln_g=sgu_ln_g, sgu_ln_b=sgu_ln_b, sgu_w=sgu_w[0], sgu_b=sgu_b[0], pool_scale=pool_scale_all)

    loss_part, dx, g_small, g_norms, g_mem_norm, last_token = device_step(
        x[0], mem[0], loss_target[0], norms_all, mem_norm, small, fetch, emit, early)
    loss = lax.psum(0.5 * jnp.sum(loss_part) / x.shape[-1], ("x", "y", "c"))
    last_names = {name for name, _ in to_chips[0][1]}
    order = last_token.reshape(1, 1)
    halves_land([dx])

    grads, delta, new_m, new_v = {}, {}, {}, {}

    def update(name):
        n_l = _as3(w[name]).shape[0]
        res = adamw_from_halves(_as3(w[name]), [own_of[name, l] for l in range(n_l)], [recv_of[name, l] for l in range(n_l)],
                                _as3(mom[name]), _as3(var[name]), core, order, f"adamw_{name}")
        grads[name], delta[name], new_m[name], new_v[name] = (t.reshape(w[name].shape) for t in res)

    for name in BIG_NAMES:
        if name not in last_names:
            update(name)
    done = [delta[name] for name in BIG_NAMES if name not in last_names]
    reduce_landed(*chips_wait(done))
    halves_land(done)
    for name in BIG_NAMES:
        if name in last_names:
            update(name)

    small_g = [g_norms, g_mem_norm, g_small["attn_sinks"], g_small["sgu_ln_g"], g_small["sgu_ln_b"], g_small["sgu_w"], g_small["sgu_b"],
               g_small["pool_scale"]]
    packed = _pack(small_g, 16)
    summed = sum_slots(gather_all_devices(packed, "small_grads_all"), "small_grads_sum")
    s_norms, s_mem, s_sinks, s_lg, s_lb, s_w, s_b, s_scale = _unpack(summed, small_g)
    grads["norms"] = lax.dynamic_slice_in_dim(s_norms, chip_id * n_shard, n_shard, axis=2)
    grads["pool_scale"] = lax.dynamic_slice_in_dim(s_scale, chip_id * n_shard, n_shard, axis=1)
    grads.update(mem_norm=s_mem, attn_sinks=s_sinks, sgu_ln_g=s_lg, sgu_ln_b=s_lb, sgu_w=s_w, sgu_b=s_b)
    like = [w[n] for n in SMALL_NAMES]
    packs = [_pack([src[n] for n in SMALL_NAMES], 16)[None] for src in (w, grads, mom, var)]
    for dst, t in zip((delta, new_m, new_v), adamw(*packs, "adamw_small")):
        for n, a in zip(SMALL_NAMES, _unpack(t[0], like)):
            dst[n] = a

    outs = [loss, dx[None]]
    for group in (grads, delta, new_m, new_v):
        outs += [group[n] for n in WEIGHT_ORDER]
    return tuple(outs)
```

```python
import functools

import jax
import jax.numpy as jnp
from jax import lax
from jax.experimental import pallas as pl
from jax.experimental.pallas import tpu as pltpu

F32 = jnp.float32
BF16 = jnp.bfloat16
MESH = pl.DeviceIdType.MESH

EPS = 1e-6
ROPE_THETA = 500000.0
ROPE_HALF = 8
HEAD_DIM = 64
N_Q_HEADS = 16
N_KV_HEADS = 2
Q_PER_KV = 8
BLOCK = 128
ATTN_WIDTH = 1024
KV_WIDTH = 128
QK_WIDTH = ATTN_WIDTH + KV_WIDTH
SGU_WIDTH = 1024
SGU_GROUPS = 8
POOL_WINDOWS = (2, 4, 8, 16)
POOL_HALO = 16
X_HEADS = 4
X_HEAD_DIM = 128
N_CHIPS = 4
N_DEV = 8

ADAM_LR = 0.001
ADAM_B1 = 0.9
ADAM_B2 = 0.999
ADAM_EPS = 1e-08
ADAM_WD = 0.01
ADAM_STEP = 10

VMEM_LIMIT_V7X = 52 * 1024 * 1024
MM_VMEM_BUDGET = 44 * 1024 * 1024
LANES = 128
ROW_TILE_BYTES = 6 * 1024 * 1024
WEIGHT_TILE_BYTES = 12 * 1024 * 1024
MXU_FLOPS_V7X = 1.0e15
HBM_BYTES_PER_S_V7X = 3.0e12
VMEM_STORE_BYTES_PER_S = 4.0e12
MXU_WEIGHT_LOAD_ROWS = 192
MXU_NARROW_COLS = 128
GRID_STEP_S = 0.35e-6


def _params(sem):
    return pltpu.CompilerParams(dimension_semantics=sem, vmem_limit_bytes=VMEM_LIMIT_V7X)


def _pick(dim, pref, align):
    cands = [t for t in range(align, dim + 1, align) if dim % t == 0]
    small = [t for t in cands if t <= pref]
    if small and small[-1] * 2 >= min(pref, dim):
        return small[-1]
    return dim


def _rows(n_rows, bytes_per_row, tile_bytes=ROW_TILE_BYTES):
    want = max(16, min(1024, tile_bytes // max(1, bytes_per_row)))
    cands = [t for t in range(16, n_rows + 1, 16) if n_rows % t == 0 and t <= want]
    return cands[-1] if cands else n_rows


def _divisors(dim, align, most):
    return [t for t in range(align, min(dim, most) + 1, align) if dim % t == 0] or [dim]


def _mm_tiles(M, N, K, J, m_align, k_align, a_bytes, b_bytes, o_bytes, reduce, ta, products=1):
    best = None
    for tm in _divisors(M, m_align, 2048):
        for tn in _divisors(N, LANES, 2048):
            for tk in _divisors(K, k_align, 4096):
                split = K // tk > 1 or reduce
                vmem = 2 * (tm * tk * a_bytes + tk * tn * b_bytes + tm * tn * o_bytes) + tm * tn * 4 * products * (2 if split else 1)
                if ta:
                    vmem += tm * tk * a_bytes
                if vmem > MM_VMEM_BUDGET:
                    continue
                steps = J * (M // tm) * (N // tn) * (K // tk)
                mxu = 2.0 * J * M * N * K / MXU_FLOPS_V7X * (tm + MXU_WEIGHT_LOAD_ROWS) / tm * (tn + MXU_NARROW_COLS) / tn
                acc = J * M * N * (K // tk) * 8 / VMEM_STORE_BYTES_PER_S if split else 0.0
                hbm = J * (M * K * a_bytes * (N // tn) + K * N * b_bytes * (M // tm) + M * N * o_bytes) / HBM_BYTES_PER_S_V7X
                cost = max(mxu + 0.5 * acc, hbm) + steps * GRID_STEP_S
                if best is None or cost < best[0]:
                    best = (cost, tm, tn, tk)
    return best[1:]


def _rsum8(v):
    r, c = v.shape
    return v.reshape(r // 8, 8, c).sum(axis=0)


def _mm(a, b, *, name, ta=False, tb=False, batch="none", groups=0, a_cb=False, b_cb=False, o_cb=False,
        out_dtype=F32, more=(), extras=(), epilogue=None, out_dtypes=None, order=None):
    J = groups or (a.shape[0] if a.ndim == 3 else (b.shape[0] if b.ndim == 3 else 1))
    a2, b2 = a.shape[-2:], b.shape[-2:]
    M, K = (a2[1], a2[0]) if ta else a2
    N, Kb = b2 if tb else (b2[1], b2[0])
    if a_cb:
        if ta:
            M //= J
        else:
            K //= J
    if b_cb:
        if tb:
            Kb //= J
        else:
            N //= J
    assert K == Kb, (name, a.shape, b.shape)
    reduce = batch == "reduce"
    out_dtypes = list(out_dtypes or [out_dtype])
    n_terms = 1 + len(more)
    o_bytes = sum(jnp.dtype(d).itemsize for d in out_dtypes) + sum(e.dtype.itemsize for e in extras)
    n_prod = n_terms if epilogue is not None else 1
    tm, tn, tk = _mm_tiles(M, N, K, J, LANES if ta else 16, LANES if (not ta or tb) else 16, a.dtype.itemsize * n_terms,
                           b.dtype.itemsize * n_terms, o_bytes, reduce, ta, n_prod)
    nm, nn, nk = M // tm, N // tn, K // tk
    if reduce:
        grid = (nm, nn, J, nk)
        unpack = lambda m, n, j, k: (j, m, n, k)
        sem = ("parallel", "parallel", "arbitrary", "arbitrary")
    else:
        grid = (J, nm, nn, nk)
        unpack = lambda j, m, n, k: (j, m, n, k)
        sem = ("parallel", "parallel", "parallel", "arbitrary")

    def a_map(*g):
        j, m, n, k = unpack(*g)
        r, c = (k, m) if ta else (m, k)
        if a_cb:
            c = c + j * (nm if ta else nk)
        return (j, r, c) if a.ndim == 3 else (r, c)

    def b_map(*g):
        j, m, n, k = unpack(*g)
        r, c = (n, k) if tb else (k, n)
        if b_cb:
            c = c + j * (nk if tb else nn)
        return (j, r, c) if b.ndim == 3 else (r, c)

    def o_map(*g):
        j, m, n, k = unpack(*g)
        if o_cb:
            return (m, n + j * nn)
        return (j, m, n) if batch == "map" else (m, n)

    a_blk = (tk, tm) if ta else (tm, tk)
    b_blk = (tn, tk) if tb else (tk, tn)
    a_spec = pl.BlockSpec(((None,) + a_blk) if a.ndim == 3 else a_blk, a_map)
    b_spec = pl.BlockSpec(((None,) + b_blk) if b.ndim == 3 else b_blk, b_map)
    if o_cb:
        out_shape, o_blk = (M, N * J), (tm, tn)
    elif batch == "map":
        out_shape, o_blk = (J, M, N), (None, tm, tn)
    else:
        out_shape, o_blk = (M, N), (tm, tn)
    o_spec = pl.BlockSpec(o_blk, o_map)
    dims = (((0 if ta else 1,), (1 if tb else 0,)), ((), ()))
    red_axes = (2, 3) if reduce else (3,)
    split = reduce or nk > 1
    n_ex, n_out = len(extras), len(out_dtypes)
    n_ord = 0 if order is None else 1

    def body(*refs):
        refs = refs[n_ord:]
        ab_refs, ex_refs = refs[:2 * n_terms], refs[2 * n_terms:2 * n_terms + n_ex]
        o_refs, acc = refs[2 * n_terms + n_ex:2 * n_terms + n_ex + n_out], refs[2 * n_terms + n_ex + n_out:]
        prods = [lax.dot_general(ab_refs[2 * t][...], ab_refs[2 * t + 1][...], dims, preferred_element_type=F32) for t in range(n_terms)]
        if epilogue is None:
            prods = [functools.reduce(lambda p, q: p + q, prods)]

        def finish(vals):
            outs = epilogue(vals, [e[...] for e in ex_refs]) if epilogue is not None else vals
            for o_ref, val in zip(o_refs, outs):
                o_ref[...] = val.astype(o_ref.dtype)

        if not split:
            finish(prods)
            return
        first = functools.reduce(jnp.logical_and, [pl.program_id(ax) == 0 for ax in red_axes])
        last = functools.reduce(jnp.logical_and, [pl.program_id(ax) == grid[ax] - 1 for ax in red_axes])

        @pl.when(first)
        def _():
            for acc_ref, prod in zip(acc, prods):
                acc_ref[...] = prod

        @pl.when(jnp.logical_not(first))
        def _():
            for acc_ref, prod in zip(acc, prods):
                acc_ref[...] += prod

        @pl.when(last)
        def _():
            finish([acc_ref[...] for acc_ref in acc])

    operands = [order] * n_ord + [a, b] + [t for pair in more for t in pair] + list(extras)
    res = pl.pallas_call(
        body, name=name, grid=grid, out_specs=[o_spec] * n_out,
        in_specs=[pl.BlockSpec(memory_space=pl.ANY)] * n_ord + [a_spec, b_spec] * n_terms + [o_spec] * n_ex,
        out_shape=[jax.ShapeDtypeStruct(out_shape, d) for d in out_dtypes],
        scratch_shapes=[pltpu.VMEM((tm, tn), F32)] * (n_prod if split else 0), compiler_params=_params(sem),
    )(*operands)
    return res if epilogue is not None else res[0]


def _rowwise(fn, tiled, whole, outs, accs, *, name):
    n_rows = tiled[0].shape[0]
    row_bytes = sum(t.shape[1] * t.dtype.itemsize for t in tiled) + sum(c * jnp.dtype(d).itemsize for c, d in outs)
    tr = _rows(n_rows, row_bytes)
    n_t, n_w, n_o = len(tiled), len(whole), len(outs)

    def body(*refs):
        i = pl.program_id(0)
        t_refs, w_refs = refs[:n_t], refs[n_t:n_t + n_w]
        o_refs, a_refs = refs[n_t + n_w:n_t + n_w + n_o], refs[n_t + n_w + n_o:]
        o_vals, a_vals = fn(i, *[r[...] for r in t_refs], *[r[...] for r in w_refs])
        for r, v in zip(o_refs, o_vals):
            r[...] = v.astype(r.dtype)
        if a_refs:
            @pl.when(i == 0)
            def _():
                for r in a_refs:
                    r[...] = jnp.zeros_like(r)
            for r, v in zip(a_refs, a_vals):
                r[...] += v

    in_specs = [pl.BlockSpec((tr, t.shape[1]), lambda i: (i, 0)) for t in tiled]
    in_specs += [pl.BlockSpec(w.shape, lambda i, nd=w.ndim: (0,) * nd) for w in whole]
    out_specs = [pl.BlockSpec((tr, c), lambda i: (i, 0)) for c, _ in outs]
    out_specs += [pl.BlockSpec(s, lambda i, nd=len(s): (0,) * nd) for s, _ in accs]
    out_shape = [jax.ShapeDtypeStruct((n_rows, c), d) for c, d in outs]
    out_shape += [jax.ShapeDtypeStruct(s, d) for s, d in accs]
    res = pl.pallas_call(
        body, name=name, grid=(n_rows // tr,), in_specs=in_specs, out_specs=out_specs, out_shape=out_shape,
        compiler_params=_params(("arbitrary",) if accs else ("parallel",)),
    )(*tiled, *whole)
    return res


def _rms_stats(x):
    x = x.astype(F32)
    r = lax.rsqrt(jnp.mean(x * x, axis=-1, keepdims=True) + EPS)
    return x * r, r


def rms_fwd(x, g, out_dtype, name):
    def fn(i, x, g):
        xhat, _ = _rms_stats(x)
        return [xhat * g], []
    return _rowwise(fn, [x], [g], [(x.shape[1], out_dtype)], [], name=name)[0]


def postnorm_res(x, y, g, s, name):
    def fn(i, x, y, g):
        yhat, _ = _rms_stats(y)
        return [x + s * (yhat * g)], []
    return _rowwise(fn, [x, y], [g], [(x.shape[1], F32)], [], name=name)[0]


def rms_bwd(xin, g, douts, scale, add, out_dtype, name):
    n_d = len(douts)

    def fn(i, x, *rest):
        ds, rest = rest[:n_d], rest[n_d:]
        ad = rest[0] if add is not None else None
        g = rest[-1]
        xhat, r = _rms_stats(x)
        d = ds[0].astype(F32)
        for e in ds[1:]:
            d = d + e.astype(F32)
        if scale != 1.0:
            d = d * scale
        dg = _rsum8(d * xhat)
        dxhat = d * g
        dx = r * (dxhat - xhat * jnp.mean(dxhat * xhat, axis=-1, keepdims=True))
        if ad is not None:
            dx = dx + ad
        return [dx], [dg]

    tiled = [xin, *douts] + ([add] if add is not None else [])
    dx, dg = _rowwise(fn, tiled, [g], [(xin.shape[1], out_dtype)], [((8, xin.shape[1]), F32)], name=name)
    return dx, dg


def _silu_parts(g):
    sg = 1.0 / (1.0 + jnp.exp(-g))
    return g * sg, sg


def _swiglu_tiles(products, saved):
    u, = products
    return [u, _silu_parts(saved[0].astype(F32))[0] * u]


def _swiglu_bwd_tiles(products, saved):
    da, = products
    g, u = saved[0].astype(F32), saved[1].astype(F32)
    s, sg = _silu_parts(g)
    return [da * u * (sg * (1.0 + g * (1.0 - sg))), da * s]


def scale_cols(y, s, name):
    def fn(i, y, s):
        return [y * s], []
    return _rowwise(fn, [y], [s], [(y.shape[1], F32)], [], name=name)[0]


def scale_cols_bwd(dm, y, s, name):
    def fn(i, dm, y, s):
        return [dm * s], [_rsum8(dm * y)]
    return _rowwise(fn, [dm, y], [s], [(y.shape[1], BF16)], [((8, y.shape[1]), F32)], name=name)


def loss_and_grad(y, target, name):
    n_feat = y.shape[1]

    def fn(i, y, t):
        e = y - t
        return [e * (1.0 / n_feat)], [_rsum8(e * e)]
    dy, part = _rowwise(fn, [y, target], [], [(n_feat, F32)], [((8, n_feat), F32)], name=name)
    return part, dy


def cast_into_slot(w3, layer, chip, name):
    _, n_rows, n_cols = w3.shape
    tr = _rows(n_rows, n_cols * 6, WEIGHT_TILE_BYTES)

    def body(chip_ref, w_ref, o_ref):
        o_ref[...] = w_ref[...].astype(BF16)

    grid_spec = pltpu.PrefetchScalarGridSpec(
        num_scalar_prefetch=1, grid=(n_rows // tr,),
        in_specs=[pl.BlockSpec((None, tr, n_cols), lambda i, chip_ref: (layer, i, 0))],
        out_specs=pl.BlockSpec((None, tr, n_cols), lambda i, chip_ref: (chip_ref[0], i, 0)))
    return pl.pallas_call(
        body, name=name, grid_spec=grid_spec, out_shape=jax.ShapeDtypeStruct((N_CHIPS, n_rows, n_cols), BF16),
        compiler_params=_params(("parallel",)),
    )(chip, w3)


def _adam_update(w, g, m, v):
    c1 = 1.0 / (1.0 - ADAM_B1 ** ADAM_STEP)
    c2 = 1.0 / (1.0 - ADAM_B2 ** ADAM_STEP)
    m = ADAM_B1 * m + (1.0 - ADAM_B1) * g
    v = ADAM_B2 * v + (1.0 - ADAM_B2) * (g * g)
    return -ADAM_LR * ((m * c1) / (jnp.sqrt(v * c2) + ADAM_EPS) + ADAM_WD * w), m, v


def adamw(w, g, m, v, name):
    n_l, n_rows, n_cols = w.shape
    tr = _rows(n_rows, n_cols * 4 * 7)

    def body(w_ref, g_ref, m_ref, v_ref, d_ref, mo_ref, vo_ref):
        d_ref[...], mo_ref[...], vo_ref[...] = _adam_update(w_ref[...], g_ref[...], m_ref[...], v_ref[...])

    spec = pl.BlockSpec((None, tr, n_cols), lambda l, i: (l, i, 0))
    shp = jax.ShapeDtypeStruct(w.shape, F32)
    return pl.pallas_call(
        body, name=name, grid=(n_l, n_rows // tr), in_specs=[spec] * 4, out_specs=[spec] * 3, out_shape=[shp] * 3,
        compiler_params=_params(("parallel", "parallel")),
    )(w, g, m, v)


def adamw_from_halves(w, own, recv, m, v, core, order, name):
    n_l, n_rows, n_cols = w.shape
    half = n_rows // 2
    tr = _rows(half, n_cols * 4 * 9, WEIGHT_TILE_BYTES)
    per = half // tr

    def body(core_ref, *refs):
        w_ref, m_ref, v_ref, order_ref = refs[:4]
        own_refs, recv_refs = refs[4:4 + n_l], refs[4 + n_l:4 + 2 * n_l]
        g_ref, d_ref, mo_ref, vo_ref = refs[4 + 2 * n_l:]
        l, h = pl.program_id(0), pl.program_id(1)
        mine = h == core_ref[0]
        g = jnp.where(mine, own_refs[0][...], recv_refs[0][...])
        for k in range(1, n_l):
            g = jnp.where(l == k, jnp.where(mine, own_refs[k][...], recv_refs[k][...]), g)
        g = g + order_ref[...]
        g_ref[...] = g
        d_ref[...], mo_ref[...], vo_ref[...] = _adam_update(w_ref[...], g, m_ref[...], v_ref[...])

    full = pl.BlockSpec((None, tr, n_cols), lambda l, h, i, core_ref: (l, h * per + i, 0))

    def piece(layer, is_own):
        def index(l, h, i, core_ref):
            used = (l == layer) & ((h == core_ref[0]) == is_own)
            return (jnp.where(used, i, 0), 0)
        return pl.BlockSpec((tr, n_cols), index)

    grid_spec = pltpu.PrefetchScalarGridSpec(
        num_scalar_prefetch=1, grid=(n_l, 2, per),
        in_specs=[full] * 3 + [pl.BlockSpec((1, 1), lambda l, h, i, core_ref: (0, 0))]
        + [piece(k, True) for k in range(n_l)] + [piece(k, False) for k in range(n_l)],
        out_specs=[full] * 4)
    return pl.pallas_call(
        body, name=name, grid_spec=grid_spec, out_shape=[jax.ShapeDtypeStruct(w.shape, F32)] * 4,
        compiler_params=_params(("parallel", "parallel", "parallel")),
    )(core, w, m, v, order, *own, *recv)


def rope_tables(n_tok):
    inv = ROPE_THETA ** (-jnp.arange(ROPE_HALF, dtype=F32) * 2.0 / (2 * ROPE_HALF))
    ang = jnp.arange(n_tok, dtype=F32)[:, None] * inv[None, :]
    cos, sin = jnp.cos(ang), jnp.sin(ang)
    rest = HEAD_DIM - 2 * ROPE_HALF
    one, zero, z8 = jnp.ones((n_tok, rest), F32), jnp.zeros((n_tok, rest), F32), jnp.zeros((n_tok, ROPE_HALF), F32)
    c = jnp.concatenate([cos, cos, one], axis=1)
    s1 = jnp.concatenate([-sin, z8, zero], axis=1)
    s2 = jnp.concatenate([z8, sin, zero], axis=1)
    two = lambda t: jnp.concatenate([t, t], axis=1)
    return two(c), two(s1), two(s2)


def rope_apply(x, tabs, n_col_blocks, inverse, out_dtype, name):
    n_tok = x.shape[0]
    tr = _rows(n_tok, LANES * 4 * 6)

    def body(x_ref, c_ref, s1_ref, s2_ref, o_ref):
        x = x_ref[...].astype(F32)
        if inverse:
            out = x * c_ref[...] + pltpu.roll(x * s1_ref[...], ROPE_HALF, 1) + pltpu.roll(x * s2_ref[...], LANES - ROPE_HALF, 1)
        else:
            out = x * c_ref[...] + pltpu.roll(x, LANES - ROPE_HALF, 1) * s1_ref[...] + pltpu.roll(x, ROPE_HALF, 1) * s2_ref[...]
        o_ref[...] = out.astype(o_ref.dtype)

    tab_spec = pl.BlockSpec((tr, LANES), lambda i, c: (i, 0))
    blk = pl.BlockSpec((tr, LANES), lambda i, c: (i, c))
    return pl.pallas_call(
        body, name=name, grid=(n_tok // tr, n_col_blocks), in_specs=[blk, tab_spec, tab_spec, tab_spec], out_specs=blk,
        out_shape=jax.ShapeDtypeStruct((n_tok, n_col_blocks * LANES), out_dtype), compiler_params=_params(("parallel", "parallel")),
    )(x, *tabs)


def _swa_masks():
    rows = Q_PER_KV * BLOCK
    qi = lax.broadcasted_iota(jnp.int32, (rows, 2 * BLOCK), 0) & (BLOCK - 1)
    kj = lax.broadcasted_iota(jnp.int32, (rows, 2 * BLOCK), 1)
    rel = qi + BLOCK - kj
    band = (rel >= 0) & (rel < BLOCK)
    return jnp.where(jnp.stack([band & (kj >= BLOCK), band]), 0.0, -1e30).astype(F32)


def _swa_probs(q, k, sink, mask):
    s = lax.dot_general(q, k, (((1,), (1,)), ((), ())), preferred_element_type=F32) * (HEAD_DIM ** -0.5) + mask
    m = jnp.maximum(jnp.max(s, axis=-1, keepdims=True), sink)
    e = jnp.exp(s - m)
    es = jnp.exp(sink - m)
    inv = 1.0 / (jnp.sum(e, axis=-1, keepdims=True) + es)
    return e * inv, es * inv


def _swa_specs(n_blocks):
    q_spec = pl.BlockSpec((Q_PER_KV, BLOCK, HEAD_DIM), lambda h, n: (h, n, 0))
    prev = pl.BlockSpec((None, BLOCK, HEAD_DIM), lambda h, n: (h, jnp.maximum(n - 1, 0), 0))
    cur = pl.BlockSpec((None, BLOCK, HEAD_DIM), lambda h, n: (h, n, 0))
    sink = pl.BlockSpec((None, Q_PER_KV * BLOCK, 1), lambda h, n: (h, 0, 0))
    masks = pl.BlockSpec((2, Q_PER_KV * BLOCK, 2 * BLOCK), lambda h, n: (0, 0, 0))
    return q_spec, prev, cur, sink, masks


def swa_fwd(q, k, v, sink_rows, name):
    n_tok = q.shape[1]
    q_spec, prev, cur, sink, masks = _swa_specs(n_tok // BLOCK)

    def body(q_ref, kp_ref, kc_ref, vp_ref, vc_ref, s_ref, m_ref, o_ref):
        mask = m_ref[jnp.minimum(pl.program_id(1), 1)]
        qq = q_ref[...].reshape(Q_PER_KV * BLOCK, HEAD_DIM)
        kk = jnp.concatenate([kp_ref[...], kc_ref[...]], axis=0)
        vv = jnp.concatenate([vp_ref[...], vc_ref[...]], axis=0)
        p, _ = _swa_probs(qq, kk, s_ref[...], mask)
        o = jnp.dot(p.astype(BF16), vv, preferred_element_type=F32)
        o_ref[...] = o.reshape(Q_PER_KV, BLOCK, HEAD_DIM).astype(o_ref.dtype)

    return pl.pallas_call(
        body, name=name, grid=(N_KV_HEADS, n_tok // BLOCK), in_specs=[q_spec, prev, cur, prev, cur, sink, masks], out_specs=q_spec,
        out_shape=jax.ShapeDtypeStruct(q.shape, BF16), compiler_params=_params(("parallel", "parallel")),
    )(q, k, k, v, v, sink_rows, _swa_masks())


def swa_bwd(q, k, v, sink_rows, do, name):
    n_tok = q.shape[1]
    nb = n_tok // BLOCK
    q_spec, prev, cur, sink, masks = _swa_specs(nb)
    rows = Q_PER_KV * BLOCK

    def body(q_ref, kp_ref, kc_ref, vp_ref, vc_ref, s_ref, m_ref, do_ref, dq_ref, dkp_ref, dkc_ref, dvp_ref, dvc_ref, ds_ref):
        mask = m_ref[jnp.minimum(pl.program_id(1), 1)]
        qq = q_ref[...].reshape(rows, HEAD_DIM)
        dd = do_ref[...].reshape(rows, HEAD_DIM)
        kk = jnp.concatenate([kp_ref[...], kc_ref[...]], axis=0)
        vv = jnp.concatenate([vp_ref[...], vc_ref[...]], axis=0)
        p, ps = _swa_probs(qq, kk, s_ref[...], mask)
        dp = lax.dot_general(dd, vv, (((1,), (1,)), ((), ())), preferred_element_type=F32)
        delta = jnp.sum(p * dp, axis=-1, keepdims=True)
        ds = (p * (dp - delta) * (HEAD_DIM ** -0.5)).astype(BF16)
        dq = jnp.dot(ds, kk, preferred_element_type=F32)
        dk = lax.dot_general(ds, qq, (((0,), (0,)), ((), ())), preferred_element_type=F32)
        dv = lax.dot_general(p.astype(BF16), dd, (((0,), (0,)), ((), ())), preferred_element_type=F32)
        dq_ref[...] = dq.reshape(Q_PER_KV, BLOCK, HEAD_DIM).astype(dq_ref.dtype)
        dkp_ref[...] = dk[:BLOCK]
        dkc_ref[...] = dk[BLOCK:]
        dvp_ref[...] = dv[:BLOCK]
        dvc_ref[...] = dv[BLOCK:]
        dsink = jnp.broadcast_to(-ps * delta, (rows, LANES)).reshape(Q_PER_KV, BLOCK, LANES)
        ds_ref[...] = jnp.sum(dsink, axis=1)

    part = pl.BlockSpec((None, None, BLOCK, HEAD_DIM), lambda h, n: (h, n, 0, 0))
    part_shape = jax.ShapeDtypeStruct((N_KV_HEADS, nb, BLOCK, HEAD_DIM), F32)
    return pl.pallas_call(
        body, name=name, grid=(N_KV_HEADS, nb), in_specs=[q_spec, prev, cur, prev, cur, sink, masks, q_spec],
        out_specs=[q_spec, part, part, part, part, pl.BlockSpec((None, None, Q_PER_KV, LANES), lambda h, n: (h, n, 0, 0))],
        out_shape=[jax.ShapeDtypeStruct(q.shape, BF16), part_shape, part_shape, part_shape, part_shape,
                   jax.ShapeDtypeStruct((N_KV_HEADS, nb, Q_PER_KV, LANES), F32)],
        compiler_params=_params(("parallel", "parallel")),
    )(q, k, k, v, v, sink_rows, _swa_masks(), do)


def _to_heads(t, n_heads):
    return t.reshape(t.shape[0], n_heads, HEAD_DIM).transpose(1, 0, 2)


def _from_heads(t):
    return t.transpose(1, 0, 2).reshape(t.shape[1], -1)


def _fold_kv_grad(prev, cur):
    shifted = jnp.concatenate([prev[:, 1:], jnp.zeros_like(prev[:, :1])], axis=1)
    tot = (cur + shifted).reshape(N_KV_HEADS, -1, HEAD_DIM)
    return _from_heads(tot)


def _x_probs(qh, kh):
    s = lax.dot_general(qh, kh, (((1,), (1,)), ((), ())), preferred_element_type=F32) * (X_HEAD_DIM ** -0.5)
    e = jnp.exp(s - jnp.max(s, axis=-1, keepdims=True))
    return e * (1.0 / jnp.sum(e, axis=-1, keepdims=True))


def xattn_fwd(q, k, v, name):
    n_tok, width = q.shape
    n_mem = k.shape[0]
    tq = _pick(n_tok, 512, 16)

    def body(q_ref, k_ref, v_ref, o_ref):
        for h in range(X_HEADS):
            cols = slice(h * X_HEAD_DIM, (h + 1) * X_HEAD_DIM)
            p = _x_probs(q_ref[:, cols], k_ref[:, cols])
            o_ref[:, cols] = jnp.dot(p.astype(BF16), v_ref[:, cols], preferred_element_type=F32).astype(o_ref.dtype)

    row = pl.BlockSpec((tq, width), lambda i: (i, 0))
    mem = pl.BlockSpec((n_mem, width), lambda i: (0, 0))
    return pl.pallas_call(
        body, name=name, grid=(n_tok // tq,), in_specs=[row, mem, mem], out_specs=row,
        out_shape=jax.ShapeDtypeStruct(q.shape, BF16), compiler_params=_params(("parallel",)),
    )(q, k, v)


def xattn_bwd(q, k, v, do, name):
    n_tok, width = q.shape
    n_mem = k.shape[0]
    tq = _pick(n_tok, 512, 16)

    def body(q_ref, k_ref, v_ref, do_ref, dq_ref, dk_ref, dv_ref):
        @pl.when(pl.program_id(0) == 0)
        def _():
            dk_ref[...] = jnp.zeros_like(dk_ref)
            dv_ref[...] = jnp.zeros_like(dv_ref)

        for h in range(X_HEADS):
            cols = slice(h * X_HEAD_DIM, (h + 1) * X_HEAD_DIM)
            qh, kh, vh, dh = q_ref[:, cols], k_ref[:, cols], v_ref[:, cols], do_ref[:, cols]
            p = _x_probs(qh, kh)
            dp = lax.dot_general(dh, vh, (((1,), (1,)), ((), ())), preferred_element_type=F32)
            delta = jnp.sum(p * dp, axis=-1, keepdims=True)
            ds = (p * (dp - delta) * (X_HEAD_DIM ** -0.5)).astype(BF16)
            dq_ref[:, cols] = jnp.dot(ds, kh, preferred_element_type=F32).astype(dq_ref.dtype)
            dk_ref[:, cols] += lax.dot_general(ds, qh, (((0,), (0,)), ((), ())), preferred_element_type=F32)
            dv_ref[:, cols] += lax.dot_general(p.astype(BF16), dh, (((0,), (0,)), ((), ())), preferred_element_type=F32)

    row = pl.BlockSpec((tq, width), lambda i: (i, 0))
    mem = pl.BlockSpec((n_mem, width), lambda i: (0, 0))
    return pl.pallas_call(
        body, name=name, grid=(n_tok // tq,), in_specs=[row, mem, mem, row], out_specs=[row, mem, mem],
        out_shape=[jax.ShapeDtypeStruct(q.shape, BF16), jax.ShapeDtypeStruct(k.shape, F32), jax.ShapeDtypeStruct(k.shape, F32)],
        compiler_params=_params(("arbitrary",)),
    )(q, k, v, do)


GELU_C = 0.7978845608028654
GELU_A = 0.044715


def _gelu_parts(x):
    x2 = x * x
    t = jnp.tanh(GELU_C * x * (1.0 + GELU_A * x2))
    y = 0.5 * x * (1.0 + t)
    dy = 0.5 * (1.0 + t) + 0.5 * x * (1.0 - t * t) * GELU_C * (1.0 + 3.0 * GELU_A * x2)
    return y, dy


def _sgu_norm(v, ln_g, ln_b):
    mu = jnp.mean(v, axis=-1, keepdims=True)
    vc = v - mu
    r = lax.rsqrt(jnp.mean(vc * vc, axis=-1, keepdims=True) + EPS)
    xhat = vc * r
    return xhat * ln_g + ln_b, xhat, r


def _causal_weights(w_ref):
    i = lax.broadcasted_iota(jnp.int32, (BLOCK, BLOCK), 0)
    j = lax.broadcasted_iota(jnp.int32, (BLOCK, BLOCK), 1)
    return [jnp.where(i >= j, w_ref[g], 0.0).astype(BF16) for g in range(SGU_GROUPS)]


def sgu_fwd(u_pre, v_pre, ln_g, ln_b, w_s, bias_rows, name):
    n_tok = u_pre.shape[0]
    tm = _pick(n_tok, 512, BLOCK)

    def body(u_ref, v_ref, g_ref, b_ref, w_ref, bb_ref, o_ref):
        vn, _, _ = _sgu_norm(_gelu_parts(v_ref[...])[0], g_ref[...], b_ref[...])
        vn = vn.astype(BF16)
        wc = _causal_weights(w_ref)
        for c in range(tm // BLOCK):
            rows = slice(c * BLOCK, (c + 1) * BLOCK)
            for g in range(SGU_GROUPS):
                cols = slice(g * LANES, (g + 1) * LANES)
                mixed = jnp.dot(wc[g], vn[rows, cols], preferred_element_type=F32) + bb_ref[g]
                u = _gelu_parts(u_ref[rows, cols])[0]
                o_ref[rows, cols] = (u * mixed).astype(o_ref.dtype)

    row = pl.BlockSpec((tm, SGU_WIDTH), lambda i: (i, 0))
    vec = pl.BlockSpec((1, SGU_WIDTH), lambda i: (0, 0))
    mat = pl.BlockSpec((SGU_GROUPS, BLOCK, LANES), lambda i: (0, 0, 0))
    return pl.pallas_call(
        body, name=name, grid=(n_tok // tm,), in_specs=[row, row, vec, vec, mat, mat], out_specs=row,
        out_shape=jax.ShapeDtypeStruct((n_tok, SGU_WIDTH), BF16), compiler_params=_params(("parallel",)),
    )(u_pre, v_pre, ln_g, ln_b, w_s, bias_rows)


def sgu_bwd(u_pre, v_pre, ln_g, ln_b, w_s, bias_rows, dgate, name):
    n_tok = u_pre.shape[0]
    tm = _pick(n_tok, 512, BLOCK)

    def body(u_ref, v_ref, g_ref, b_ref, w_ref, bb_ref, dg_ref, du_ref, dv_ref, dw_ref, db_ref, dlg_ref, dlb_ref, dvn_ref):
        @pl.when(pl.program_id(0) == 0)
        def _():
            dw_ref[...] = jnp.zeros_like(dw_ref)
            db_ref[...] = jnp.zeros_like(db_ref)
            dlg_ref[...] = jnp.zeros_like(dlg_ref)
            dlb_ref[...] = jnp.zeros_like(dlb_ref)

        gv, dgv = _gelu_parts(v_ref[...])
        vn, xhat, r = _sgu_norm(gv, g_ref[...], b_ref[...])
        vn = vn.astype(BF16)
        wc = _causal_weights(w_ref)
        for c in range(tm // BLOCK):
            rows = slice(c * BLOCK, (c + 1) * BLOCK)
            for g in range(SGU_GROUPS):
                cols = slice(g * LANES, (g + 1) * LANES)
                vt = vn[rows, cols]
                mixed = jnp.dot(wc[g], vt, preferred_element_type=F32) + bb_ref[g]
                u, du_dpre = _gelu_parts(u_ref[rows, cols])
                dgate_t = dg_ref[rows, cols].astype(F32)
                du_ref[rows, cols] = (dgate_t * mixed * du_dpre).astype(du_ref.dtype)
                dmix = dgate_t * u
                dmix_b = dmix.astype(BF16)
                db_ref[g] += dmix
                dw_ref[g] += lax.dot_general(dmix_b, vt, (((1,), (1,)), ((), ())), preferred_element_type=F32)
                dvn_ref[rows, cols] = lax.dot_general(wc[g], dmix_b, (((0,), (0,)), ((), ())), preferred_element_type=F32)
        dvn = dvn_ref[...]
        dlg_ref[...] += _rsum8(dvn * xhat)
        dlb_ref[...] += _rsum8(dvn)
        dxhat = dvn * g_ref[...]
        dgv_in = r * (dxhat - jnp.mean(dxhat, axis=-1, keepdims=True) - xhat * jnp.mean(dxhat * xhat, axis=-1, keepdims=True))
        dv_ref[...] = (dgv_in * dgv).astype(dv_ref.dtype)

    row = pl.BlockSpec((tm, SGU_WIDTH), lambda i: (i, 0))
    vec = pl.BlockSpec((1, SGU_WIDTH), lambda i: (0, 0))
    mat = pl.BlockSpec((SGU_GROUPS, BLOCK, LANES), lambda i: (0, 0, 0))
    part = pl.BlockSpec((8, SGU_WIDTH), lambda i: (0, 0))
    mat_shape = jax.ShapeDtypeStruct((SGU_GROUPS, BLOCK, LANES), F32)
    part_shape = jax.ShapeDtypeStruct((8, SGU_WIDTH), F32)
    act_shape = jax.ShapeDtypeStruct((n_tok, SGU_WIDTH), BF16)
    return pl.pallas_call(
        body, name=name, grid=(n_tok // tm,), in_specs=[row, row, vec, vec, mat, mat, row],
        out_specs=[row, row, mat, mat, part, part], out_shape=[act_shape, act_shape, mat_shape, mat_shape, part_shape, part_shape],
        scratch_shapes=[pltpu.VMEM((tm, SGU_WIDTH), F32)], compiler_params=_params(("arbitrary",)),
    )(u_pre, v_pre, ln_g, ln_b, w_s, bias_rows, dgate)


def _pool_tile(n_tok):
    return _pick(n_tok, 256, POOL_HALO)


def pool_fwd(h, name):
    n_tok, width = h.shape
    gw = width // len(POOL_WINDOWS)
    tm = _pool_tile(n_tok)
    per = tm // POOL_HALO

    def body(cur_ref, halo_ref, o_ref, buf_ref):
        i = pl.program_id(0)
        buf_ref[0:POOL_HALO, :] = jnp.where(i > 0, halo_ref[...], 0.0)
        buf_ref[POOL_HALO:, :] = cur_ref[...]
        tok = i * tm + lax.broadcasted_iota(jnp.int32, (tm, 1), 0)
        for g, w in enumerate(POOL_WINDOWS):
            cols = slice(g * gw, (g + 1) * gw)
            acc = buf_ref[POOL_HALO:, cols]
            for j in range(1, w):
                acc = acc + buf_ref[POOL_HALO - j:POOL_HALO - j + tm, cols]
            cnt = jnp.minimum(tok + 1, w).astype(F32)
            o_ref[:, cols] = (acc / cnt - cur_ref[:, cols]).astype(o_ref.dtype)

    return pl.pallas_call(
        body, name=name, grid=(n_tok // tm,),
        in_specs=[pl.BlockSpec((tm, width), lambda i: (i, 0)),
                  pl.BlockSpec((POOL_HALO, width), lambda i: (jnp.maximum(i * per - 1, 0), 0))],
        out_specs=pl.BlockSpec((tm, width), lambda i: (i, 0)), out_shape=jax.ShapeDtypeStruct(h.shape, BF16),
        scratch_shapes=[pltpu.VMEM((tm + POOL_HALO, width), F32)], compiler_params=_params(("parallel",)),
    )(h, h)


def pool_bwd(dp, name):
    n_tok, width = dp.shape
    gw = width // len(POOL_WINDOWS)
    tm = _pool_tile(n_tok)
    per = tm // POOL_HALO
    n_steps = n_tok // tm

    def body(cur_ref, halo_ref, o_ref, buf_ref):
        i = pl.program_id(0)
        tok = i * tm + lax.broadcasted_iota(jnp.int32, (tm, 1), 0)
        for g, w in enumerate(POOL_WINDOWS):
            cols = slice(g * gw, (g + 1) * gw)
            cnt = jnp.minimum(tok + 1, w).astype(F32)
            buf_ref[0:tm, cols] = cur_ref[:, cols] / cnt
            buf_ref[tm:, cols] = jnp.where(i < n_steps - 1, halo_ref[:, cols] / float(w), 0.0)
        for g, w in enumerate(POOL_WINDOWS):
            cols = slice(g * gw, (g + 1) * gw)
            acc = buf_ref[0:tm, cols]
            for j in range(1, w):
                acc = acc + buf_ref[j:j + tm, cols]
            o_ref[:, cols] = acc - cur_ref[:, cols]

    return pl.pallas_call(
        body, name=name, grid=(n_steps,),
        in_specs=[pl.BlockSpec((tm, width), lambda i: (i, 0)),
                  pl.BlockSpec((POOL_HALO, width), lambda i: (jnp.minimum((i + 1) * per, n_tok // POOL_HALO - 1), 0))],
        out_specs=pl.BlockSpec((tm, width), lambda i: (i, 0)), out_shape=jax.ShapeDtypeStruct(dp.shape, F32),
        scratch_shapes=[pltpu.VMEM((tm + POOL_HALO, width), F32)], compiler_params=_params(("parallel",)),
    )(dp, dp)


def _ffn_fwd(x, ga, gb, wg, wu, wd, tag):
    h = rms_fwd(x, ga, BF16, f"{tag}_prenorm")
    G = _mm(h, wg, batch="map", out_dtype=BF16, name=f"{tag}_gate")
    wu = wu(G) if callable(wu) else wu
    U, A = _mm(h, wu, batch="map", extras=[G], epilogue=_swiglu_tiles, out_dtypes=[BF16] * 2, name=f"{tag}_up")
    wd = wd(A) if callable(wd) else wd
    y = _mm(A, wd, batch="reduce", out_dtype=BF16, name=f"{tag}_down")
    return postnorm_res(x, y, gb, 0.5, f"{tag}_postnorm"), (x, h, G, U, A, y, wg, wu, wd)


def _ffn_bwd(res, ga, gb, dx2, tag, early):
    x, h, G, U, A, y, wg, wu, wd = res
    dy, dgb = rms_bwd(y, gb, [dx2], 0.5, None, BF16, f"{tag}_postnorm_bwd")
    dG, dU = _mm(dy, wd, tb=True, batch="map", extras=[G, U], epilogue=_swiglu_bwd_tiles, out_dtypes=[BF16] * 2, name=f"{tag}_down_dx")
    dwd = _mm(A, dy, ta=True, batch="map", name=f"{tag}_down_dw", out_dtype=BF16)
    dwg = _mm(h, dG, ta=True, batch="map", name=f"{tag}_gate_dw", out_dtype=BF16, order=early("wd", dwd))
    dwu = _mm(h, dU, ta=True, batch="map", name=f"{tag}_up_dw", out_dtype=BF16, order=early("wg", dwg))
    dh = _mm(dG, wg, tb=True, batch="reduce", more=[(dU, wu)], out_dtype=BF16, name=f"{tag}_gate_up_dx", order=early("wu", dwu))
    dx, dga = rms_bwd(x, ga, [dh], 1.0, dx2, F32, f"{tag}_prenorm_bwd")
    return dx, dga, dgb, dwg, dwu, dwd


def _sink_rows(sinks):
    return jnp.repeat(sinks.reshape(N_KV_HEADS, Q_PER_KV), BLOCK, axis=1)[..., None]


def _attn_sgu_fwd(x, g_pre, g_post, w_in, w_out, sinks, ln_g, ln_b, sgu_w, bias_rows, tabs, tag):
    h = rms_fwd(x, g_pre, BF16, f"{tag}_prenorm")
    z = _mm(h, w_in, name=f"{tag}_in")
    qk = rope_apply(z, tabs, QK_WIDTH // LANES, False, BF16, f"{tag}_rope")
    q = _to_heads(qk[:, :ATTN_WIDTH], N_Q_HEADS)
    k = _to_heads(qk[:, ATTN_WIDTH:], N_KV_HEADS)
    v = _to_heads(z[:, QK_WIDTH:QK_WIDTH + KV_WIDTH].astype(BF16), N_KV_HEADS)
    o = swa_fwd(q, k, v, _sink_rows(sinks), f"{tag}_swa")
    u_pre = z[:, QK_WIDTH + KV_WIDTH:QK_WIDTH + KV_WIDTH + SGU_WIDTH]
    v_pre = z[:, QK_WIDTH + KV_WIDTH + SGU_WIDTH:]
    gate = sgu_fwd(u_pre, v_pre, ln_g, ln_b, sgu_w, bias_rows, f"{tag}_sgu")
    cat = jnp.concatenate([_from_heads(o), gate], axis=1)
    m = _mm(cat, w_out, out_dtype=BF16, name=f"{tag}_out")
    return postnorm_res(x, m, g_post, 1.0, f"{tag}_postnorm"), (x, h, q, k, v, u_pre, v_pre, cat, m)


def _attn_sgu_bwd(res, g_pre, g_post, w_in, w_out, sinks, ln_g, ln_b, sgu_w, bias_rows, tabs, dx2, tag):
    x, h, q, k, v, u_pre, v_pre, cat, m = res
    dm, dg_post = rms_bwd(m, g_post, [dx2], 1.0, None, BF16, f"{tag}_postnorm_bwd")
    dcat = _mm(dm, w_out, tb=True, out_dtype=BF16, name=f"{tag}_out_dx")
    dw_out = _mm(cat, dm, ta=True, name=f"{tag}_out_dw", out_dtype=BF16)
    do = _to_heads(dcat[:, :ATTN_WIDTH], N_Q_HEADS)
    dq, dkp, dkc, dvp, dvc, dsink = swa_bwd(q, k, v, _sink_rows(sinks), do, f"{tag}_swa_bwd")
    d_sinks = jnp.sum(dsink[..., 0], axis=1).reshape(1, N_Q_HEADS)
    dqk_rot = jnp.concatenate([_from_heads(dq).astype(F32), _fold_kv_grad(dkp, dkc)], axis=1)
    dqk = rope_apply(dqk_rot, tabs, QK_WIDTH // LANES, True, BF16, f"{tag}_rope_bwd")
    dv = _fold_kv_grad(dvp, dvc).astype(BF16)
    du_pre, dv_pre, dw_s, dbias, dlg, dlb = sgu_bwd(u_pre, v_pre, ln_g, ln_b, sgu_w, bias_rows, dcat[:, ATTN_WIDTH:], f"{tag}_sgu_bwd")
    dz = jnp.concatenate([dqk, dv, du_pre, dv_pre], axis=1)
    dw_in = _mm(h, dz, ta=True, name=f"{tag}_in_dw", out_dtype=BF16)
    dh = _mm(dz, w_in, tb=True, out_dtype=BF16, name=f"{tag}_in_dx")
    dx, dg_pre = rms_bwd(x, g_pre, [dh], 1.0, dx2, F32, f"{tag}_prenorm_bwd")
    causal = jnp.tril(jnp.ones((BLOCK, BLOCK), F32))
    small = dict(attn_sinks=d_sinks, sgu_ln_g=jnp.sum(dlg, axis=0, keepdims=True), sgu_ln_b=jnp.sum(dlb, axis=0, keepdims=True),
                 sgu_w=(dw_s * causal[None])[None], sgu_b=jnp.sum(dbias, axis=-1)[None])
    return dx, dg_pre, dg_post, dw_in, dw_out, small


def _pool_mix_fwd(x, g_pre, g_post, pool_w, pool_scale, tag):
    hf = rms_fwd(x, g_pre, F32, f"{tag}_prenorm")
    pooled = pool_fwd(hf, f"{tag}_pool")
    n_g = len(POOL_WINDOWS)
    ypre = _mm(pooled, pool_w, batch="map", groups=n_g, a_cb=True, o_cb=True, name=f"{tag}_proj")
    m = scale_cols(ypre, pool_scale, f"{tag}_scale")
    return postnorm_res(x, m, g_post, 1.0, f"{tag}_postnorm"), (x, pooled, ypre, m)


def _pool_mix_bwd(res, g_pre, g_post, pool_w, pool_scale, dx2, tag):
    x, pooled, ypre, m = res
    n_g = len(POOL_WINDOWS)
    dm, dg_post = rms_bwd(m, g_post, [dx2], 1.0, None, F32, f"{tag}_postnorm_bwd")
    dypre, dscale = scale_cols_bwd(dm, ypre, pool_scale, f"{tag}_scale_bwd")
    dpooled = _mm(dypre, pool_w, tb=True, batch="map", groups=n_g, a_cb=True, o_cb=True, name=f"{tag}_proj_dx")
    dpw = _mm(pooled, dypre, ta=True, batch="map", groups=n_g, a_cb=True, b_cb=True, name=f"{tag}_proj_dw", out_dtype=BF16)
    dhf = pool_bwd(dpooled, f"{tag}_pool_bwd")
    dx, dg_pre = rms_bwd(x, g_pre, [dhf], 1.0, dx2, F32, f"{tag}_prenorm_bwd")
    return dx, dg_pre, dg_post, dpw, jnp.sum(dscale, axis=0, keepdims=True)


def _xattn_fwd(x, mem, g_pre, g_post, g_mem, wq, wk, wv, wo, tag):
    h = rms_fwd(x, g_pre, BF16, f"{tag}_prenorm")
    mem_n = rms_fwd(mem, g_mem, BF16, f"{tag}_memnorm")
    q = _mm(h, wq, out_dtype=BF16, name=f"{tag}_q")
    k = _mm(mem_n, wk, out_dtype=BF16, name=f"{tag}_k")
    v = _mm(mem_n, wv, out_dtype=BF16, name=f"{tag}_v")
    o = xattn_fwd(q, k, v, f"{tag}_core")
    r = _mm(o, wo, out_dtype=BF16, name=f"{tag}_o")
    return postnorm_res(x, r, g_post, 1.0, f"{tag}_postnorm"), (x, h, mem_n, q, k, v, o, r)


def _xattn_bwd(res, mem, g_pre, g_post, g_mem, wq, wk, wv, wo, dx2, tag):
    x, h, mem_n, q, k, v, o, r = res
    dr, dg_post = rms_bwd(r, g_post, [dx2], 1.0, None, BF16, f"{tag}_postnorm_bwd")
    do = _mm(dr, wo, tb=True, out_dtype=BF16, name=f"{tag}_o_dx")
    dwo = _mm(o, dr, ta=True, name=f"{tag}_o_dw", out_dtype=BF16)
    dq, dk, dv = xattn_bwd(q, k, v, do, f"{tag}_core_bwd")
    dk, dv = dk.astype(BF16), dv.astype(BF16)
    dwq = _mm(h, dq, ta=True, name=f"{tag}_q_dw", out_dtype=BF16)
    dwk = _mm(mem_n, dk, ta=True, name=f"{tag}_k_dw", out_dtype=BF16)
    dwv = _mm(mem_n, dv, ta=True, name=f"{tag}_v_dw", out_dtype=BF16)
    dh = _mm(dq, wq, tb=True, out_dtype=BF16, name=f"{tag}_q_dx")
    dmem1 = _mm(dk, wk, tb=True, name=f"{tag}_k_dx")
    dmem2 = _mm(dv, wv, tb=True, name=f"{tag}_v_dx")
    _, dg_mem = rms_bwd(mem, g_mem, [dmem1, dmem2], 1.0, None, BF16, f"{tag}_memnorm_bwd")
    dx, dg_pre = rms_bwd(x, g_pre, [dh], 1.0, dx2, F32, f"{tag}_prenorm_bwd")
    return dx, dg_pre, dg_post, dg_mem, dwq, dwk, dwv, dwo


def _rowsum8(part):
    return jnp.sum(part, axis=0, keepdims=True)


def device_step(x, mem, target, norms, mem_norm, small, fetch, emit, early):
    n_tok = x.shape[0]
    tabs = rope_tables(n_tok)
    bias_rows = jnp.broadcast_to(small["sgu_b"][:, :, None], (SGU_GROUPS, BLOCK, LANES))
    gn = lambda l, i: norms[l, i][None, :]
    gm = lambda l: mem_norm[l][None, :]
    mix0 = (small["attn_sinks"], small["sgu_ln_g"], small["sgu_ln_b"], small["sgu_w"], bias_rows, tabs)

    wts, saved = {}, []
    for l in range(2):
        wts["ffn1", l], tok = fetch(("ffn1", l), x)
        x, r1 = _ffn_fwd(x, gn(l, 0) + tok, gn(l, 1), *wts["ffn1", l], f"l{l}_ffn1")
        wts["mix", l], tok = fetch(("mix", l), x)
        if l == 0:
            x, r2 = _attn_sgu_fwd(x, gn(l, 2) + tok, gn(l, 3), *wts["mix", l], *mix0, f"l{l}_mix")
        else:
            x, r2 = _pool_mix_fwd(x, gn(l, 2) + tok, gn(l, 3), *wts["mix", l], small["pool_scale"], f"l{l}_mix")
        wts["xattn", l], tok = fetch(("xattn", l), x)
        x, r3 = _xattn_fwd(x, mem, gn(l, 4) + tok, gn(l, 5), gm(l), *wts["xattn", l], f"l{l}_xattn")
        wts["ffn2", l], tok = fetch(("ffn2", l), x)
        x, r4 = _ffn_fwd(x, gn(l, 6) + tok, gn(l, 7), *wts["ffn2", l], f"l{l}_ffn2")
        saved.append((r1, r2, r3, r4))

    loss_part, dx = loss_and_grad(x, target, "loss")

    g_norm_rows = [[None] * 8, [None] * 8]
    g_mem_rows = [None, None]
    g_small = {}
    tok = 0.0
    for l in (1, 0):
        r1, r2, r3, r4 = saved[l]
        dx, g_norm_rows[l][6], g_norm_rows[l][7], dwg, dwu, dwd = _ffn_bwd(
            r4, gn(l, 6), gn(l, 7) + tok, dx, f"l{l}_ffn2", lambda part, g, l=l: early(("ffn2", l), f"ffn2_{part}", g))
        tok = emit(("ffn2", l), dict(ffn2_wg=dwg, ffn2_wu=dwu, ffn2_wd=dwd), dx)
        dx, g_norm_rows[l][4], g_norm_rows[l][5], g_mem_rows[l], dwq, dwk, dwv, dwo = _xattn_bwd(
            r3, mem, gn(l, 4), gn(l, 5) + tok, gm(l), *wts["xattn", l], dx, f"l{l}_xattn")
        tok = emit(("xattn", l), dict(x_wq=dwq, x_wk=dwk, x_wv=dwv, x_wo=dwo), dx)
        if l == 0:
            dx, g_norm_rows[l][2], g_norm_rows[l][3], dw_in, dw_out, sm = _attn_sgu_bwd(
                r2, gn(l, 2), gn(l, 3) + tok, *wts["mix", l], *mix0, dx, f"l{l}_mix")
            tok = emit(("mix", l), dict(mix_w_in=dw_in, mix_w_out=dw_out), dx)
            g_small.update(sm)
        else:
            dx, g_norm_rows[l][2], g_norm_rows[l][3], dpw, dscale = _pool_mix_bwd(
                r2, gn(l, 2), gn(l, 3) + tok, *wts["mix", l], small["pool_scale"], dx, f"l{l}_mix")
            tok = emit(("mix", l), dict(pool_w=dpw), dx)
            g_small["pool_scale"] = dscale
        dx, g_norm_rows[l][0], g_norm_rows[l][1], dwg, dwu, dwd = _ffn_bwd(
            r1, gn(l, 0), gn(l, 1) + tok, dx, f"l{l}_ffn1", lambda part, g, l=l: early(("ffn1", l), f"ffn1_{part}", g))
        tok = emit(("ffn1", l), dict(ffn1_wg=dwg, ffn1_wu=dwu, ffn1_wd=dwd), dx)
    g_norms = jnp.stack([jnp.concatenate([_rowsum8(p) for p in g_norm_rows[l]], axis=0) for l in range(2)])
    g_mem_norm = jnp.concatenate([_rowsum8(p) for p in g_mem_rows], axis=0)
    return loss_part, dx, g_small, g_norms, g_mem_norm, tok


ANY = pl.BlockSpec(memory_space=pl.ANY)


def _place():
    x, y, c = lax.axis_index("x"), lax.axis_index("y"), lax.axis_index("c")
    other_chips = [(1 - x, y), (x, 1 - y), (1 - x, 1 - y)]
    return x, y, c, other_chips


def _half_rows(core, n_rows):
    half = n_rows // 2
    return pl.ds(pl.multiple_of(core * half, 16), half)


def _remote(src, dst, send_sem, recv_sem, device):
    return pltpu.make_async_remote_copy(src_ref=src, dst_ref=dst, send_sem=send_sem, recv_sem=recv_sem,
                                        device_id=device, device_id_type=MESH)


HBM = pl.BlockSpec(memory_space=pltpu.HBM)
SEM = pl.BlockSpec(memory_space=pltpu.SEMAPHORE)
DATAFLOW = pltpu.SideEffectType.DATAFLOW_SIDE_EFFECTING


def _chip_copies(bufs, send_sems, recv_sems):
    x, y, c, chips = _place()
    me = 2 * x + y
    sends, arrivals = [], []
    for i, buf in enumerate(bufs):
        rows = _half_rows(c, buf.shape[1])
        for r, (px, py) in enumerate(chips):
            mine, theirs = buf.at[me, rows], buf.at[2 * px + py, rows]
            sends.append(_remote(mine, mine, send_sems.at[3 * i + r], recv_sems.at[3 * i + r], (px, py, c)))
            arrivals.append(_remote(theirs, theirs, send_sems.at[3 * i + r], recv_sems.at[3 * i + r], (px, py, c)))
    return sends, arrivals


def copies_start(arrays, fresh, copies, n_sems, after, name):
    operands = [pltpu.with_memory_space_constraint(a, pltpu.HBM) for a in arrays]
    operands += [pltpu.with_memory_space_constraint(lax.empty(f.shape, f.dtype), pltpu.HBM) for f in fresh]
    n = len(operands)

    def body(*refs):
        send_sems, recv_sems = refs[n + 1], refs[n + 2]
        bufs, token = refs[n + 3:2 * n + 3], refs[2 * n + 3]
        for cp in copies(bufs, send_sems, recv_sems)[0]:
            cp.start()
        token[...] = jnp.zeros_like(token)

    res = pl.pallas_call(
        body, name=name, in_specs=[HBM] * n + [ANY],
        out_specs=[SEM, SEM] + [HBM] * n + [pl.BlockSpec(memory_space=pltpu.VMEM)],
        out_shape=[pltpu.SemaphoreType.DMA((n_sems,)), pltpu.SemaphoreType.DMA((n_sems,))]
        + [pltpu.HBM(o.shape, o.dtype) for o in operands] + [jax.ShapeDtypeStruct((8, LANES), F32)],
        input_output_aliases={i: 2 + i for i in range(n)}, compiler_params=pltpu.CompilerParams(has_side_effects=DATAFLOW),
    )(*operands, after)
    return res[0], res[1], list(res[2:2 + n]), res[2 + n]


def copies_wait(send_sems, recv_sems, bufs, copies, after, name):
    n = len(bufs)

    def body(*refs):
        for cp in copies(refs[:n], refs[n], refs[n + 1])[1]:
            cp.wait_send()
            cp.wait_recv()

    return list(pl.pallas_call(
        body, name=name, in_specs=[HBM] * n + [SEM, SEM] + [ANY] * len(after), out_specs=[HBM] * n,
        out_shape=[pltpu.HBM(b.shape, b.dtype) for b in bufs], input_output_aliases={i: i for i in range(n)},
        compiler_params=pltpu.CompilerParams(has_side_effects=DATAFLOW),
    )(*bufs, send_sems, recv_sems, *after))


def _to_sibling_copies(n_src, src_rows):
    def copies(bufs, send_sems, recv_sems):
        x, y, c, _ = _place()
        sends, arrivals = [], []
        for i in range(n_src):
            src, land = bufs[i], bufs[n_src + i]
            src = src.at[:, src_rows(c, src)] if src_rows is not None else src
            sends.append(_remote(src, land, send_sems.at[i], recv_sems.at[i], (x, y, 1 - c)))
            arrivals.append(_remote(land, land, send_sems.at[i], recv_sems.at[i], (x, y, 1 - c)))
        return sends, arrivals
    return copies


def _forward_copies(bufs, send_sems, recv_sems):
    x, y, c, chips = _place()
    sends, arrivals = [], []
    for i, buf in enumerate(bufs):
        for r, (px, py) in enumerate(chips):
            mine = buf.at[2 * px + py, _half_rows(c, buf.shape[1])]
            theirs = buf.at[2 * px + py, _half_rows(1 - c, buf.shape[1])]
            sends.append(_remote(mine, mine, send_sems.at[3 * i + r], recv_sems.at[3 * i + r], (x, y, 1 - c)))
            arrivals.append(_remote(theirs, theirs, send_sems.at[3 * i + r], recv_sems.at[3 * i + r], (x, y, 1 - c)))
    return sends, arrivals


def forward_to_sibling(bufs, name):
    n = len(bufs)

    def body(*refs):
        sends, arrivals = _forward_copies(refs[n:2 * n], *refs[2 * n:])
        for cp in sends:
            cp.start()
        for cp in arrivals:
            cp.wait_recv()
        for cp in sends:
            cp.wait_send()

    return pl.pallas_call(
        body, name=name, in_specs=[ANY] * n, out_specs=[ANY] * n, input_output_aliases={i: i for i in range(n)},
        out_shape=[jax.ShapeDtypeStruct(b.shape, b.dtype) for b in bufs],
        scratch_shapes=[pltpu.SemaphoreType.DMA((3 * n,)), pltpu.SemaphoreType.DMA((3 * n,))],
    )(*bufs)


def add_own_half(g, p, core, name):
    n_j, n_rows, n_cols = g.shape
    half = n_rows // 2
    tr = _rows(half, n_cols * (2 * g.dtype.itemsize + 2), WEIGHT_TILE_BYTES)

    def body(c_ref, g_ref, p_ref, o_ref):
        o_ref[...] = (g_ref[...] + p_ref[...]).astype(o_ref.dtype)

    grid_spec = pltpu.PrefetchScalarGridSpec(
        num_scalar_prefetch=1, grid=(n_j, half // tr),
        in_specs=[pl.BlockSpec((None, None, tr, n_cols), lambda j, i, c_ref: (j, c_ref[0], i, 0)),
                  pl.BlockSpec((None, tr, n_cols), lambda j, i, c_ref: (j, i, 0))],
        out_specs=pl.BlockSpec((None, tr, n_cols), lambda j, i, c_ref: (j, i, 0)))
    return pl.pallas_call(
        body, name=name, grid_spec=grid_spec, out_shape=jax.ShapeDtypeStruct((n_j, half, n_cols), BF16),
        compiler_params=_params(("parallel", "parallel")),
    )(core, g.reshape(n_j, 2, half, n_cols), p)


def _scatter_copies(n_parts):
    def copies(bufs, send_sems, recv_sems):
        x, y, c, chips = _place()
        me = 2 * x + y
        sends, arrivals = [], []
        for i in range(n_parts):
            part, land = bufs[i], bufs[n_parts + i]
            for r, (px, py) in enumerate(chips):
                theirs = land.at[2 * px + py]
                sends.append(_remote(part.at[2 * px + py], land.at[me], send_sems.at[3 * i + r], recv_sems.at[3 * i + r], (px, py, c)))
                arrivals.append(_remote(theirs, theirs, send_sems.at[3 * i + r], recv_sems.at[3 * i + r], (px, py, c)))
        return sends, arrivals
    return copies


def sum_over_chips(own, got, chip, name):
    n_s, n_rows, n_cols = got.shape
    tr = _rows(n_rows, n_cols * (got.dtype.itemsize * (n_s + 1) + 4), WEIGHT_TILE_BYTES)

    def body(chip_ref, own_ref, *refs):
        got_refs, o_ref = refs[:n_s], refs[n_s]
        me = chip_ref[0]
        acc = jnp.where(me == 0, own_ref[...], got_refs[0][...]).astype(F32)
        for k in range(1, n_s):
            acc = acc + jnp.where(me == k, own_ref[...], got_refs[k][...]).astype(F32)
        o_ref[...] = acc

    def slot(k):
        return pl.BlockSpec((None, tr, n_cols), lambda i, chip_ref: (jnp.where(chip_ref[0] == k, (k + 1) % n_s, k), i, 0))

    grid_spec = pltpu.PrefetchScalarGridSpec(
        num_scalar_prefetch=1, grid=(n_rows // tr,),
        in_specs=[pl.BlockSpec((None, tr, n_cols), lambda i, chip_ref: (chip_ref[0], i, 0))] + [slot(k) for k in range(n_s)],
        out_specs=pl.BlockSpec((tr, n_cols), lambda i, chip_ref: (i, 0)))
    return pl.pallas_call(
        body, name=name, grid_spec=grid_spec, out_shape=jax.ShapeDtypeStruct((n_rows, n_cols), F32),
        compiler_params=_params(("parallel",)),
    )(chip, own, *([got] * n_s))


def sum_slots(q, name):
    n_s, n_rows, n_cols = q.shape
    tr = _rows(n_rows, n_cols * (q.dtype.itemsize * n_s + 4))

    def body(q_ref, o_ref):
        acc = q_ref[0].astype(F32)
        for s in range(1, n_s):
            acc = acc + q_ref[s].astype(F32)
        o_ref[...] = acc

    return pl.pallas_call(
        body, name=name, grid=(n_rows // tr,), in_specs=[pl.BlockSpec((n_s, tr, n_cols), lambda i: (0, i, 0))],
        out_specs=pl.BlockSpec((tr, n_cols), lambda i: (i, 0)), out_shape=jax.ShapeDtypeStruct((n_rows, n_cols), F32),
        compiler_params=_params(("parallel",)),
    )(q)


def gather_all_devices(s, name):
    def body(s_ref, o_ref, send_sems, recv_sems, local_sem):
        x, y, c, _ = _place()
        me = 4 * x + 2 * y + c
        local = pltpu.make_async_copy(s_ref, o_ref.at[me], local_sem)
        local.start()
        sends = []
        for f in range(1, N_DEV):
            px, py, pc = x ^ (f >> 2), y ^ ((f >> 1) & 1), c ^ (f & 1)
            sends.append(_remote(s_ref, o_ref.at[me], send_sems.at[f - 1], recv_sems.at[f - 1], (px, py, pc)))
            sends[-1].start()
        for f in range(1, N_DEV):
            px, py, pc = x ^ (f >> 2), y ^ ((f >> 1) & 1), c ^ (f & 1)
            landed = o_ref.at[4 * px + 2 * py + pc]
            _remote(landed, landed, send_sems.at[f - 1], recv_sems.at[f - 1], (px, py, pc)).wait_recv()
        for cp in sends:
            cp.wait_send()
        local.wait()

    return pl.pallas_call(
        body, name=name, in_specs=[ANY], out_specs=ANY, out_shape=jax.ShapeDtypeStruct((N_DEV,) + s.shape, s.dtype),
        scratch_shapes=[pltpu.SemaphoreType.DMA((N_DEV - 1,)), pltpu.SemaphoreType.DMA((N_DEV - 1,)), pltpu.SemaphoreType.DMA],
    )(s)


FFN_NAMES = ("ffn1_wg", "ffn1_wu", "ffn1_wd", "ffn2_wg", "ffn2_wu", "ffn2_wd")
XATTN_NAMES = ("x_wq", "x_wk", "x_wv", "x_wo")
BIG_NAMES = FFN_NAMES + XATTN_NAMES + ("mix_w_in", "mix_w_out", "pool_w")
SMALL_NAMES = ("norms", "mem_norm", "attn_sinks", "sgu_ln_g", "sgu_ln_b", "sgu_w", "sgu_b", "pool_scale")
WEIGHT_ORDER = ("norms", "mem_norm") + BIG_NAMES[:-1] + ("attn_sinks", "sgu_ln_g", "sgu_ln_b", "sgu_w", "sgu_b", "pool_w", "pool_scale")
COLUMN_CUT = ("x_wo", "mix_w_in")
N_POOL = len(POOL_WINDOWS)
BLOCK_ORDER = (("ffn1", 0), ("mix", 0), ("xattn", 0), ("ffn2", 0), ("ffn1", 1), ("mix", 1), ("xattn", 1), ("ffn2", 1))
PREFETCH_AT = ((1, 2, 3), (), (), (4,), (5, 6, 7), (), (), ())


def block_weight_names(kind, layer):
    if kind == "mix":
        return ("mix_w_in", "mix_w_out") if layer == 0 else ("pool_w",)
    return XATTN_NAMES if kind == "xattn" else tuple(f"{kind}_{part}" for part in ("wg", "wu", "wd"))


def _to_matmul_layout(name, g):
    n_j, n_rows, n_cols = g.shape
    if name in COLUMN_CUT:
        return g.transpose(1, 0, 2).reshape(n_rows, n_j * n_cols)
    if name == "pool_w":
        return g.reshape(n_j, N_POOL, n_rows // N_POOL, n_cols).transpose(1, 0, 2, 3).reshape(N_POOL, n_j * n_rows // N_POOL, n_cols)
    if name in ("x_wq", "x_wk", "x_wv", "mix_w_out"):
        return g.reshape(n_j * n_rows, n_cols)
    return g


def _from_matmul_layout(name, d):
    if name in COLUMN_CUT:
        n_rows, wide = d.shape
        return d.reshape(n_rows, N_CHIPS, wide // N_CHIPS).transpose(1, 0, 2)
    if name == "pool_w":
        n_g, n_in, n_cols = d.shape
        return d.reshape(n_g, N_CHIPS, n_in // N_CHIPS, n_cols).transpose(1, 0, 2, 3).reshape(N_CHIPS, n_g * n_in // N_CHIPS, n_cols)
    if name in ("x_wq", "x_wk", "x_wv", "mix_w_out"):
        return d.reshape(N_CHIPS, d.shape[0] // N_CHIPS, d.shape[1])
    return d


def _as3(w):
    return w.reshape(w.shape[0], -1, w.shape[-1])


def _pack(arrays, row_multiple):
    flat = jnp.concatenate([a.reshape(-1) for a in arrays])
    per = LANES * row_multiple
    total = -(-flat.shape[0] // per) * per
    return jnp.pad(flat, (0, total - flat.shape[0])).reshape(total // LANES, LANES)


def _unpack(packed, like):
    flat, out, at = packed.reshape(-1), [], 0
    for a in like:
        out.append(flat[at:at + a.size].reshape(a.shape))
        at += a.size
    return out


def kernel(x, mem, norms, mem_norm, ffn1_wg, ffn1_wu, ffn1_wd, ffn2_wg, ffn2_wu, ffn2_wd, x_wq, x_wk, x_wv, x_wo, mix_w_in, mix_w_out, attn_sinks, sgu_ln_g, sgu_ln_b, sgu_w, sgu_b, pool_w, pool_scale, loss_target, m_norms, m_mem_norm, m_ffn1_wg, m_ffn1_wu, m_ffn1_wd, m_ffn2_wg, m_ffn2_wu, m_ffn2_wd, m_x_wq, m_x_wk, m_x_wv, m_x_wo, m_mix_w_in, m_mix_w_out, m_attn_sinks, m_sgu_ln_g, m_sgu_ln_b, m_sgu_w, m_sgu_b, m_pool_w, m_pool_scale, v_norms, v_mem_norm, v_ffn1_wg, v_ffn1_wu, v_ffn1_wd, v_ffn2_wg, v_ffn2_wu, v_ffn2_wd, v_x_wq, v_x_wk, v_x_wv, v_x_wo, v_mix_w_in, v_mix_w_out, v_attn_sinks, v_sgu_ln_g, v_sgu_ln_b, v_sgu_w, v_sgu_b, v_pool_w, v_pool_scale):
    given = dict(locals())
    w = {n: given[n] for n in WEIGHT_ORDER}
    mom = {n: given["m_" + n] for n in WEIGHT_ORDER}
    var = {n: given["v_" + n] for n in WEIGHT_ORDER}
    chip_id = 2 * lax.axis_index("x") + lax.axis_index("y")
    chip = chip_id.astype(jnp.int32).reshape(1)
    core = lax.axis_index("c").astype(jnp.int32).reshape(1)
    n_shard = norms.shape[-1]

    keys = [(name, l) for name in BIG_NAMES for l in range(_as3(w[name]).shape[0])]
    first_keys = [(name, 0) for name in block_weight_names(*BLOCK_ORDER[0])]
    cast = lambda name, l: cast_into_slot(_as3(w[name]), l, chip, f"cast_{name}")
    slot_of = {key: cast(*key) for key in first_keys}
    small_rows = jnp.concatenate([norms.reshape(-1, n_shard), pool_scale, jnp.zeros((15, n_shard), F32)], axis=0)
    small_slot = lax.dynamic_update_slice_in_dim(jnp.zeros((N_CHIPS,) + small_rows.shape, F32), small_rows[None], chip_id, axis=0)
    pending = {}
    gather_start = lambda slots, after, name: copies_start(slots, [], _chip_copies, 3 * len(slots), after, name)
    pending["0a"] = gather_start([slot_of[first_keys[0]], small_slot], chip, "gather_start_0a")
    pending["0b"] = gather_start([slot_of[first_keys[1]]], pending["0a"][2][0], "gather_start_0b")
    pending["0c"] = gather_start([slot_of[first_keys[2]]], pending["0b"][2][0], "gather_start_0c")
    for key in keys:
        if key not in slot_of:
            slot_of[key] = cast(*key)


    def block_slots(k):
        kind, layer = BLOCK_ORDER[k]
        return [slot_of[name, layer if kind != "mix" else 0] for name in block_weight_names(kind, layer)]

    def start(k, after):
        pending[k] = gather_start(block_slots(k), after, f"gather_start_{k}")

    def finish(k, after):
        send_sems, recv_sems, bufs, _ = pending[k]
        return forward_to_sibling(copies_wait(send_sems, recv_sems, bufs, _chip_copies, [after], f"gather_wait_{k}"), f"gather_forward_{k}")

    def fetch(key, after):
        k = BLOCK_ORDER.index(key)
        names = block_weight_names(*key)
        token = 0.0
        if k == 0:
            late = lambda part, name: lambda after: _to_matmul_layout(name, finish(part, after)[0])
            bufs = [first_gate]
            weights = (_to_matmul_layout(names[0], first_gate), late("0b", names[1]), late("0c", names[2]))
        elif key[0] in ("ffn1", "ffn2"):
            send_sems, recv_sems, bufs, _ = pending[k]
            bufs = copies_wait(send_sems, recv_sems, bufs, _chip_copies, [after], f"gather_wait_{k}")
            gate, = forward_to_sibling(bufs[:1], f"gather_forward_{k}_gate")
            send_sems, recv_sems, rest, begun = copies_start(bufs[1:], [], _forward_copies, 6, gate, f"gather_forward_{k}_start")
            landed = []

            def late(i, name):
                def get(after):
                    if not landed:
                        landed.extend(copies_wait(send_sems, recv_sems, rest, _forward_copies, [after], f"gather_forward_{k}_wait"))
                    return _to_matmul_layout(name, landed[i])
                return get
            bufs = [gate]
            weights = (_to_matmul_layout(names[0], gate), late(0, names[1]), late(1, names[2]))
            token = begun[0, 0]
        else:
            bufs = finish(k, after)
            weights = tuple(_to_matmul_layout(name, b) for name, b in zip(names, bufs))
        for ahead in PREFETCH_AT[k]:
            start(ahead, bufs[0])
            token = token + pending[ahead][3][0, 0]
        return weights, token

    to_sibling, to_chips, halves, own_of, recv_of = [], [], [], {}, {}
    other_half = lambda c, src: _half_rows(1 - c, src.shape[1])

    def sibling_start(tag, names, full, after):
        lands = [jax.ShapeDtypeStruct((g.shape[0], g.shape[1] // 2, g.shape[2]), g.dtype) for g in full]
        send_sems, recv_sems, bufs, token = copies_start(full, lands, _to_sibling_copies(len(full), other_half), len(full), after,
                                                         f"grads_to_sibling_{tag}")
        to_sibling.append((tag, names, send_sems, recv_sems, bufs))
        return token

    def early(key, name, g):
        kind, layer = key
        return sibling_start(f"{name}{layer}", [(name, layer)], [_from_matmul_layout(name, g)], chip)

    def chips_start(tag, after):
        names, chip_sums = [], []
        for part, part_names, send_sems, recv_sems, bufs in to_sibling:
            n = len(part_names)
            bufs = copies_wait(send_sems, recv_sems, bufs, _to_sibling_copies(n, other_half), after, f"grads_from_sibling_{part}")
            chip_sums += [add_own_half(g, p, core, f"chip_sum_{name}") for (name, _), g, p in zip(part_names, bufs[:n], bufs[n:])]
            names += part_names
        to_sibling.clear()
        n = len(names)
        send_sems, recv_sems, bufs, token = copies_start(chip_sums, chip_sums, _scatter_copies(n), 3 * n, after[0], f"grads_start_{tag}")
        to_chips.append((tag, names, send_sems, recv_sems, bufs))
        return token[0, 0]

    def chips_wait(after):
        tag, names, send_sems, recv_sems, bufs = to_chips.pop()
        return tag, names, copies_wait(send_sems, recv_sems, bufs, _scatter_copies(len(names)), after, f"grads_wait_{tag}")

    def reduce_landed(tag, names, bufs):
        n = len(names)
        own = [sum_over_chips(t, q, chip, f"sum_{name}") for (name, _), t, q in zip(names, bufs[:n], bufs[n:])]
        send_sems, recv_sems, bufs, _ = copies_start(own, own, _to_sibling_copies(n, None), n, chip, f"grads_halves_{tag}")
        halves.append((tag, names, send_sems, recv_sems, bufs))

    def halves_land(after):
        for tag, names, send_sems, recv_sems, bufs in halves:
            n = len(names)
            bufs = copies_wait(send_sems, recv_sems, bufs, _to_sibling_copies(n, None), after, f"grads_halves_wait_{tag}")
            for key, o, r in zip(names, bufs[:n], bufs[n:]):
                own_of[key], recv_of[key] = o, r
        halves.clear()

    def emit(key, grads_of, after):
        kind, layer = key
        landed = chips_wait([after]) if to_chips else None
        begun = {name for _, part_names, _, _, _ in to_sibling for name, _ in part_names}
        rest = {name: g for name, g in grads_of.items() if name not in begun}
        if rest:
            sibling_start(f"{kind}{layer}", [(name, layer if kind != "mix" else 0) for name in rest],
                          [_from_matmul_layout(name, g) for name, g in rest.items()], after)
        token = chips_start(f"{kind}{layer}", [after])
        if landed:
            reduce_landed(*landed)
        return token

    first_gate, small_all = finish("0a", pending["0c"][2][0])
    n_norm_rows = norms.shape[0] * norms.shape[1]
    norms_all = jnp.concatenate([small_all[j, :n_norm_rows].reshape(norms.shape) for j in range(N_CHIPS)], axis=-1)
    pool_scale_all = jnp.concatenate([small_all[j, n_norm_rows:n_norm_rows + 1] for j in range(N_CHIPS)], axis=-1)
    small = dict(attn_sinks=attn_sinks[0], sgu_ln_g=sgu_ln_g, sgu_ln_b=sgu_ln_b, sgu_w=sgu_w[0], sgu_b=sgu_b[0], pool_scale=pool_scale_all)

    loss_part, dx, g_small, g_norms, g_mem_norm, last_token = device_step(
        x[0], mem[0], loss_target[0], norms_all, mem_norm, small, fetch, emit, early)
    loss = lax.psum(0.5 * jnp.sum(loss_part) / x.shape[-1], ("x", "y", "c"))
    last_names = {name for name, _ in to_chips[0][1]}
    order = last_token.reshape(1, 1)
    halves_land([dx])

    grads, delta, new_m, new_v = {}, {}, {}, {}

    def update(name):
        n_l = _as3(w[name]).shape[0]
        res = adamw_from_halves(_as3(w[name]), [own_of[name, l] for l in range(n_l)], [recv_of[name, l] for l in range(n_l)],
                                _as3(mom[name]), _as3(var[name]), core, order, f"adamw_{name}")
        grads[name], delta[name], new_m[name], new_v[name] = (t.reshape(w[name].shape) for t in res)

    for name in BIG_NAMES:
        if name not in last_names:
            update(name)
    done = [delta[name] for name in BIG_NAMES if name not in last_names]
    reduce_landed(*chips_wait(done))
    halves_land(done)
    for name in BIG_NAMES:
        if name in last_names:
            update(name)

    small_g = [g_norms, g_mem_norm, g_small["attn_sinks"], g_small["sgu_ln_g"], g_small["sgu_ln_b"], g_small["sgu_w"], g_small["sgu_b"],
               g_small["pool_scale"]]
    packed = _pack(small_g, 16)
    summed = sum_slots(gather_all_devices(packed, "small_grads_all"), "small_grads_sum")
    s_norms, s_mem, s_sinks, s_lg, s_lb, s_w, s_b, s_scale = _unpack(summed, small_g)
    grads["norms"] = lax.dynamic_slice_in_dim(s_norms, chip_id * n_shard, n_shard, axis=2)
    grads["pool_scale"] = lax.dynamic_slice_in_dim(s_scale, chip_id * n_shard, n_shard, axis=1)
    grads.update(mem_norm=s_mem, attn_sinks=s_sinks, sgu_ln_g=s_lg, sgu_ln_b=s_lb, sgu_w=s_w, sgu_b=s_b)
    like = [w[n] for n in SMALL_NAMES]
    packs = [_pack([src[n] for n in SMALL_NAMES], 16)[None] for src in (w, grads, mom, var)]
    for dst, t in zip((delta, new_m, new_v), adamw(*packs, "adamw_small")):
        for n, a in zip(SMALL_NAMES, _unpack(t[0], like)):
            dst[n] = a

    outs = [loss, dx[None]]
    for group in (grads, delta, new_m, new_v):
        outs += [group[n] for n in WEIGHT_ORDER]
    return tuple(outs)
```

```python
import functools

import jax
import jax.numpy as jnp
from jax import lax
from jax.experimental import pallas as pl
from jax.experimental.pallas import tpu as pltpu

F32 = jnp.float32
BF16 = jnp.bfloat16
MESH = pl.DeviceIdType.MESH

EPS = 1e-6
ROPE_THETA = 500000.0
ROPE_HALF = 8
HEAD_DIM = 64
N_Q_HEADS = 16
N_KV_HEADS = 2
Q_PER_KV = 8
BLOCK = 128
ATTN_WIDTH = 1024
KV_WIDTH = 128
QK_WIDTH = ATTN_WIDTH + KV_WIDTH
SGU_WIDTH = 1024
SGU_GROUPS = 8
POOL_WINDOWS = (2, 4, 8, 16)
POOL_HALO = 16
X_HEADS = 4
X_HEAD_DIM = 128
N_CHIPS = 4
N_DEV = 8

ADAM_LR = 0.001
ADAM_B1 = 0.9
ADAM_B2 = 0.999
ADAM_EPS = 1e-08
ADAM_WD = 0.01
ADAM_STEP = 10

VMEM_LIMIT_V7X = 52 * 1024 * 1024
MM_VMEM_BUDGET = 44 * 1024 * 1024
LANES = 128
ROW_TILE_BYTES = 6 * 1024 * 1024
WEIGHT_TILE_BYTES = 12 * 1024 * 1024
MXU_FLOPS_V7X = 1.0e15
HBM_BYTES_PER_S_V7X = 3.0e12
VMEM_STORE_BYTES_PER_S = 4.0e12
MXU_WEIGHT_LOAD_ROWS = 192
MXU_NARROW_COLS = 128
GRID_STEP_S = 0.35e-6


def _params(sem):
    return pltpu.CompilerParams(dimension_semantics=sem, vmem_limit_bytes=VMEM_LIMIT_V7X)


def _pick(dim, pref, align):
    cands = [t for t in range(align, dim + 1, align) if dim % t == 0]
    small = [t for t in cands if t <= pref]
    if small and small[-1] * 2 >= min(pref, dim):
        return small[-1]
    return dim


def _rows(n_rows, bytes_per_row, tile_bytes=ROW_TILE_BYTES):
    want = max(16, min(1024, tile_bytes // max(1, bytes_per_row)))
    cands = [t for t in range(16, n_rows + 1, 16) if n_rows % t == 0 and t <= want]
    return cands[-1] if cands else n_rows


def _divisors(dim, align, most):
    return [t for t in range(align, min(dim, most) + 1, align) if dim % t == 0] or [dim]


def _mm_tiles(M, N, K, J, m_align, k_align, a_bytes, b_bytes, o_bytes, reduce, ta, products=1):
    best = None
    for tm in _divisors(M, m_align, 2048):
        for tn in _divisors(N, LANES, 2048):
            for tk in _divisors(K, k_align, 4096):
                split = K // tk > 1 or reduce
                vmem = 2 * (tm * tk * a_bytes + tk * tn * b_bytes + tm * tn * o_bytes) + tm * tn * 4 * products * (2 if split else 1)
                if ta:
                    vmem += tm * tk * a_bytes
                if vmem > MM_VMEM_BUDGET:
                    continue
                steps = J * (M // tm) * (N // tn) * (K // tk)
                mxu = 2.0 * J * M * N * K / MXU_FLOPS_V7X * (tm + MXU_WEIGHT_LOAD_ROWS) / tm * (tn + MXU_NARROW_COLS) / tn
                acc = J * M * N * (K // tk) * 8 / VMEM_STORE_BYTES_PER_S if split else 0.0
                hbm = J * (M * K * a_bytes * (N // tn) + K * N * b_bytes * (M // tm) + M * N * o_bytes) / HBM_BYTES_PER_S_V7X
                cost = max(mxu + 0.5 * acc, hbm) + steps * GRID_STEP_S
                if best is None or cost < best[0]:
                    best = (cost, tm, tn, tk)
    return best[1:]


def _rsum8(v):
    r, c = v.shape
    return v.reshape(r // 8, 8, c).sum(axis=0)


def _mm(a, b, *, name, ta=False, tb=False, batch="none", groups=0, a_cb=False, b_cb=False, o_cb=False,
        out_dtype=F32, more=(), extras=(), epilogue=None, out_dtypes=None, order=None):
    J = groups or (a.shape[0] if a.ndim == 3 else (b.shape[0] if b.ndim == 3 else 1))
    a2, b2 = a.shape[-2:], b.shape[-2:]
    M, K = (a2[1], a2[0]) if ta else a2
    N, Kb = b2 if tb else (b2[1], b2[0])
    if a_cb:
        if ta:
            M //= J
        else:
            K //= J
    if b_cb:
        if tb:
            Kb //= J
        else:
            N //= J
    assert K == Kb, (name, a.shape, b.shape)
    reduce = batch == "reduce"
    out_dtypes = list(out_dtypes or [out_dtype])
    n_terms = 1 + len(more)
    o_bytes = sum(jnp.dtype(d).itemsize for d in out_dtypes) + sum(e.dtype.itemsize for e in extras)
    n_prod = n_terms if epilogue is not None else 1
    tm, tn, tk = _mm_tiles(M, N, K, J, LANES if ta else 16, LANES if (not ta or tb) else 16, a.dtype.itemsize * n_terms,
                           b.dtype.itemsize * n_terms, o_bytes, reduce, ta, n_prod)
    nm, nn, nk = M // tm, N // tn, K // tk
    if reduce:
        grid = (nm, nn, J, nk)
        unpack = lambda m, n, j, k: (j, m, n, k)
        sem = ("parallel", "parallel", "arbitrary", "arbitrary")
    else:
        grid = (J, nm, nn, nk)
        unpack = lambda j, m, n, k: (j, m, n, k)
        sem = ("parallel", "parallel", "parallel", "arbitrary")

    def a_map(*g):
        j, m, n, k = unpack(*g)
        r, c = (k, m) if ta else (m, k)
        if a_cb:
            c = c + j * (nm if ta else nk)
        return (j, r, c) if a.ndim == 3 else (r, c)

    def b_map(*g):
        j, m, n, k = unpack(*g)
        r, c = (n, k) if tb else (k, n)
        if b_cb:
            c = c + j * (nk if tb else nn)
        return (j, r, c) if b.ndim == 3 else (r, c)

    def o_map(*g):
        j, m, n, k = unpack(*g)
        if o_cb:
            return (m, n + j * nn)
        return (j, m, n) if batch == "map" else (m, n)

    a_blk = (tk, tm) if ta else (tm, tk)
    b_blk = (tn, tk) if tb else (tk, tn)
    a_spec = pl.BlockSpec(((None,) + a_blk) if a.ndim == 3 else a_blk, a_map)
    b_spec = pl.BlockSpec(((None,) + b_blk) if b.ndim == 3 else b_blk, b_map)
    if o_cb:
        out_shape, o_blk = (M, N * J), (tm, tn)
    elif batch == "map":
        out_shape, o_blk = (J, M, N), (None, tm, tn)
    else:
        out_shape, o_blk = (M, N), (tm, tn)
    o_spec = pl.BlockSpec(o_blk, o_map)
    dims = (((0 if ta else 1,), (1 if tb else 0,)), ((), ()))
    red_axes = (2, 3) if reduce else (3,)
    split = reduce or nk > 1
    n_ex, n_out = len(extras), len(out_dtypes)
    n_ord = 0 if order is None else 1

    def body(*refs):
        refs = refs[n_ord:]
        ab_refs, ex_refs = refs[:2 * n_terms], refs[2 * n_terms:2 * n_terms + n_ex]
        o_refs, acc = refs[2 * n_terms + n_ex:2 * n_terms + n_ex + n_out], refs[2 * n_terms + n_ex + n_out:]
        prods = [lax.dot_general(ab_refs[2 * t][...], ab_refs[2 * t + 1][...], dims, preferred_element_type=F32) for t in range(n_terms)]
        if epilogue is None:
            prods = [functools.reduce(lambda p, q: p + q, prods)]

        def finish(vals):
            outs = epilogue(vals, [e[...] for e in ex_refs]) if epilogue is not None else vals
            for o_ref, val in zip(o_refs, outs):
                o_ref[...] = val.astype(o_ref.dtype)

        if not split:
            finish(prods)
            return
        first = functools.reduce(jnp.logical_and, [pl.program_id(ax) == 0 for ax in red_axes])
        last = functools.reduce(jnp.logical_and, [pl.program_id(ax) == grid[ax] - 1 for ax in red_axes])

        @pl.when(first)
        def _():
            for acc_ref, prod in zip(acc, prods):
                acc_ref[...] = prod

        @pl.when(jnp.logical_not(first))
        def _():
            for acc_ref, prod in zip(acc, prods):
                acc_ref[...] += prod

        @pl.when(last)
        def _():
            finish([acc_ref[...] for acc_ref in acc])

    operands = [order] * n_ord + [a, b] + [t for pair in more for t in pair] + list(extras)
    res = pl.pallas_call(
        body, name=name, grid=grid, out_specs=[o_spec] * n_out,
        in_specs=[pl.BlockSpec(memory_space=pl.ANY)] * n_ord + [a_spec, b_spec] * n_terms + [o_spec] * n_ex,
        out_shape=[jax.ShapeDtypeStruct(out_shape, d) for d in out_dtypes],
        scratch_shapes=[pltpu.VMEM((tm, tn), F32)] * (n_prod if split else 0), compiler_params=_params(sem),
    )(*operands)
    return res if epilogue is not None else res[0]


def _rowwise(fn, tiled, whole, outs, accs, *, name):
    n_rows = tiled[0].shape[0]
    row_bytes = sum(t.shape[1] * t.dtype.itemsize for t in tiled) + sum(c * jnp.dtype(d).itemsize for c, d in outs)
    tr = _rows(n_rows, row_bytes)
    n_t, n_w, n_o = len(tiled), len(whole), len(outs)

    def body(*refs):
        i = pl.program_id(0)
        t_refs, w_refs = refs[:n_t], refs[n_t:n_t + n_w]
        o_refs, a_refs = refs[n_t + n_w:n_t + n_w + n_o], refs[n_t + n_w + n_o:]
        o_vals, a_vals = fn(i, *[r[...] for r in t_refs], *[r[...] for r in w_refs])
        for r, v in zip(o_refs, o_vals):
            r[...] = v.astype(r.dtype)
        if a_refs:
            @pl.when(i == 0)
            def _():
                for r in a_refs:
                    r[...] = jnp.zeros_like(r)
            for r, v in zip(a_refs, a_vals):
                r[...] += v

    in_specs = [pl.BlockSpec((tr, t.shape[1]), lambda i: (i, 0)) for t in tiled]
    in_specs += [pl.BlockSpec(w.shape, lambda i, nd=w.ndim: (0,) * nd) for w in whole]
    out_specs = [pl.BlockSpec((tr, c), lambda i: (i, 0)) for c, _ in outs]
    out_specs += [pl.BlockSpec(s, lambda i, nd=len(s): (0,) * nd) for s, _ in accs]
    out_shape = [jax.ShapeDtypeStruct((n_rows, c), d) for c, d in outs]
    out_shape += [jax.ShapeDtypeStruct(s, d) for s, d in accs]
    res = pl.pallas_call(
        body, name=name, grid=(n_rows // tr,), in_specs=in_specs, out_specs=out_specs, out_shape=out_shape,
        compiler_params=_params(("arbitrary",) if accs else ("parallel",)),
    )(*tiled, *whole)
    return res


def _rms_stats(x):
    x = x.astype(F32)
    r = lax.rsqrt(jnp.mean(x * x, axis=-1, keepdims=True) + EPS)
    return x * r, r


def rms_fwd(x, g, out_dtype, name):
    def fn(i, x, g):
        xhat, _ = _rms_stats(x)
        return [xhat * g], []
    return _rowwise(fn, [x], [g], [(x.shape[1], out_dtype)], [], name=name)[0]


def postnorm_res(x, y, g, s, name):
    def fn(i, x, y, g):
        yhat, _ = _rms_stats(y)
        return [x + s * (yhat * g)], []
    return _rowwise(fn, [x, y], [g], [(x.shape[1], F32)], [], name=name)[0]


def close_block(x, y, g, s, nxt, tag):
    if nxt is None:
        return postnorm_res(x, y, g, s, f"{tag}_postnorm"), None
    g_next, next_dtype = nxt(y)

    def fn(i, x, y, g, g_next):
        yhat, _ = _rms_stats(y)
        x2 = x + s * (yhat * g)
        return [x2, _rms_stats(x2)[0] * g_next], []
    return _rowwise(fn, [x, y], [g, g_next], [(x.shape[1], F32), (x.shape[1], next_dtype)], [], name=f"{tag}_postnorm_next")


def rms_bwd(xin, g, douts, scale, add, out_dtype, name):
    n_d = len(douts)

    def fn(i, x, *rest):
        ds, rest = rest[:n_d], rest[n_d:]
        ad = rest[0] if add is not None else None
        g = rest[-1]
        xhat, r = _rms_stats(x)
        d = ds[0].astype(F32)
        for e in ds[1:]:
            d = d + e.astype(F32)
        if scale != 1.0:
            d = d * scale
        dg = _rsum8(d * xhat)
        dxhat = d * g
        dx = r * (dxhat - xhat * jnp.mean(dxhat * xhat, axis=-1, keepdims=True))
        if ad is not None:
            dx = dx + ad
        return [dx], [dg]

    tiled = [xin, *douts] + ([add] if add is not None else [])
    dx, dg = _rowwise(fn, tiled, [g], [(xin.shape[1], out_dtype)], [((8, xin.shape[1]), F32)], name=name)
    return dx, dg


def _silu_parts(g):
    sg = 1.0 / (1.0 + jnp.exp(-g))
    return g * sg, sg


def _swiglu_tiles(products, saved):
    u, = products
    return [u, _silu_parts(saved[0].astype(F32))[0] * u]


def _swiglu_bwd_tiles(products, saved):
    da, = products
    g, u = saved[0].astype(F32), saved[1].astype(F32)
    s, sg = _silu_parts(g)
    return [da * u * (sg * (1.0 + g * (1.0 - sg))), da * s]


def scale_cols(y, s, name):
    def fn(i, y, s):
        return [y * s], []
    return _rowwise(fn, [y], [s], [(y.shape[1], F32)], [], name=name)[0]


def scale_cols_bwd(dm, y, s, name):
    def fn(i, dm, y, s):
        return [dm * s], [_rsum8(dm * y)]
    return _rowwise(fn, [dm, y], [s], [(y.shape[1], BF16)], [((8, y.shape[1]), F32)], name=name)


def loss_and_grad(y, target, name):
    n_feat = y.shape[1]

    def fn(i, y, t):
        e = y - t
        return [e * (1.0 / n_feat)], [_rsum8(e * e)]
    dy, part = _rowwise(fn, [y, target], [], [(n_feat, F32)], [((8, n_feat), F32)], name=name)
    return part, dy


def cast_into_slot(w3, layer, chip, name):
    _, n_rows, n_cols = w3.shape
    tr = _rows(n_rows, n_cols * 6, WEIGHT_TILE_BYTES)

    def body(chip_ref, w_ref, o_ref):
        o_ref[...] = w_ref[...].astype(BF16)

    grid_spec = pltpu.PrefetchScalarGridSpec(
        num_scalar_prefetch=1, grid=(n_rows // tr,),
        in_specs=[pl.BlockSpec((None, tr, n_cols), lambda i, chip_ref: (layer, i, 0))],
        out_specs=pl.BlockSpec((None, tr, n_cols), lambda i, chip_ref: (chip_ref[0], i, 0)))
    return pl.pallas_call(
        body, name=name, grid_spec=grid_spec, out_shape=jax.ShapeDtypeStruct((N_CHIPS, n_rows, n_cols), BF16),
        compiler_params=_params(("parallel",)),
    )(chip, w3)


def _adam_update(w, g, m, v):
    c1 = 1.0 / (1.0 - ADAM_B1 ** ADAM_STEP)
    c2 = 1.0 / (1.0 - ADAM_B2 ** ADAM_STEP)
    m = ADAM_B1 * m + (1.0 - ADAM_B1) * g
    v = ADAM_B2 * v + (1.0 - ADAM_B2) * (g * g)
    return -ADAM_LR * ((m * c1) / (jnp.sqrt(v * c2) + ADAM_EPS) + ADAM_WD * w), m, v


def adamw(w, g, m, v, name):
    n_l, n_rows, n_cols = w.shape
    tr = _rows(n_rows, n_cols * 4 * 7)

    def body(w_ref, g_ref, m_ref, v_ref, d_ref, mo_ref, vo_ref):
        d_ref[...], mo_ref[...], vo_ref[...] = _adam_update(w_ref[...], g_ref[...], m_ref[...], v_ref[...])

    spec = pl.BlockSpec((None, tr, n_cols), lambda l, i: (l, i, 0))
    shp = jax.ShapeDtypeStruct(w.shape, F32)
    return pl.pallas_call(
        body, name=name, grid=(n_l, n_rows // tr), in_specs=[spec] * 4, out_specs=[spec] * 3, out_shape=[shp] * 3,
        compiler_params=_params(("parallel", "parallel")),
    )(w, g, m, v)


def adamw_from_halves(w, own, recv, m, v, core, order, name):
    n_l, n_rows, n_cols = w.shape
    half = n_rows // 2
    tr = _rows(half, n_cols * 4 * 9, WEIGHT_TILE_BYTES)
    per = half // tr

    def body(core_ref, *refs):
        w_ref, m_ref, v_ref, order_ref = refs[:4]
        own_refs, recv_refs = refs[4:4 + n_l], refs[4 + n_l:4 + 2 * n_l]
        g_ref, d_ref, mo_ref, vo_ref = refs[4 + 2 * n_l:]
        l, h = pl.program_id(0), pl.program_id(1)
        mine = h == core_ref[0]
        g = jnp.where(mine, own_refs[0][...], recv_refs[0][...])
        for k in range(1, n_l):
            g = jnp.where(l == k, jnp.where(mine, own_refs[k][...], recv_refs[k][...]), g)
        g = g + order_ref[...]
        g_ref[...] = g
        d_ref[...], mo_ref[...], vo_ref[...] = _adam_update(w_ref[...], g, m_ref[...], v_ref[...])

    full = pl.BlockSpec((None, tr, n_cols), lambda l, h, i, core_ref: (l, h * per + i, 0))

    def piece(layer, is_own):
        def index(l, h, i, core_ref):
            used = (l == layer) & ((h == core_ref[0]) == is_own)
            return (jnp.where(used, i, 0), 0)
        return pl.BlockSpec((tr, n_cols), index)

    grid_spec = pltpu.PrefetchScalarGridSpec(
        num_scalar_prefetch=1, grid=(n_l, 2, per),
        in_specs=[full] * 3 + [pl.BlockSpec((1, 1), lambda l, h, i, core_ref: (0, 0))]
        + [piece(k, True) for k in range(n_l)] + [piece(k, False) for k in range(n_l)],
        out_specs=[full] * 4)
    return pl.pallas_call(
        body, name=name, grid_spec=grid_spec, out_shape=[jax.ShapeDtypeStruct(w.shape, F32)] * 4,
        compiler_params=_params(("parallel", "parallel", "parallel")),
    )(core, w, m, v, order, *own, *recv)


def rope_tables(n_tok):
    inv = ROPE_THETA ** (-jnp.arange(ROPE_HALF, dtype=F32) * 2.0 / (2 * ROPE_HALF))
    ang = jnp.arange(n_tok, dtype=F32)[:, None] * inv[None, :]
    cos, sin = jnp.cos(ang), jnp.sin(ang)
    rest = HEAD_DIM - 2 * ROPE_HALF
    one, zero, z8 = jnp.ones((n_tok, rest), F32), jnp.zeros((n_tok, rest), F32), jnp.zeros((n_tok, ROPE_HALF), F32)
    c = jnp.concatenate([cos, cos, one], axis=1)
    s1 = jnp.concatenate([-sin, z8, zero], axis=1)
    s2 = jnp.concatenate([z8, sin, zero], axis=1)
    two = lambda t: jnp.concatenate([t, t], axis=1)
    return two(c), two(s1), two(s2)


def rope_apply(x, tabs, n_col_blocks, inverse, out_dtype, name):
    n_tok = x.shape[0]
    tr = _rows(n_tok, LANES * 4 * 6)

    def body(x_ref, c_ref, s1_ref, s2_ref, o_ref):
        x = x_ref[...].astype(F32)
        if inverse:
            out = x * c_ref[...] + pltpu.roll(x * s1_ref[...], ROPE_HALF, 1) + pltpu.roll(x * s2_ref[...], LANES - ROPE_HALF, 1)
        else:
            out = x * c_ref[...] + pltpu.roll(x, LANES - ROPE_HALF, 1) * s1_ref[...] + pltpu.roll(x, ROPE_HALF, 1) * s2_ref[...]
        o_ref[...] = out.astype(o_ref.dtype)

    tab_spec = pl.BlockSpec((tr, LANES), lambda i, c: (i, 0))
    blk = pl.BlockSpec((tr, LANES), lambda i, c: (i, c))
    return pl.pallas_call(
        body, name=name, grid=(n_tok // tr, n_col_blocks), in_specs=[blk, tab_spec, tab_spec, tab_spec], out_specs=blk,
        out_shape=jax.ShapeDtypeStruct((n_tok, n_col_blocks * LANES), out_dtype), compiler_params=_params(("parallel", "parallel")),
    )(x, *tabs)


def _swa_masks():
    rows = Q_PER_KV * BLOCK
    qi = lax.broadcasted_iota(jnp.int32, (rows, 2 * BLOCK), 0) & (BLOCK - 1)
    kj = lax.broadcasted_iota(jnp.int32, (rows, 2 * BLOCK), 1)
    rel = qi + BLOCK - kj
    band = (rel >= 0) & (rel < BLOCK)
    return jnp.where(jnp.stack([band & (kj >= BLOCK), band]), 0.0, -1e30).astype(F32)


def _swa_probs(q, k, sink, mask):
    s = lax.dot_general(q, k, (((1,), (1,)), ((), ())), preferred_element_type=F32) * (HEAD_DIM ** -0.5) + mask
    m = jnp.maximum(jnp.max(s, axis=-1, keepdims=True), sink)
    e = jnp.exp(s - m)
    es = jnp.exp(sink - m)
    inv = 1.0 / (jnp.sum(e, axis=-1, keepdims=True) + es)
    return e * inv, es * inv


def _swa_specs(n_blocks):
    q_spec = pl.BlockSpec((Q_PER_KV, BLOCK, HEAD_DIM), lambda h, n: (h, n, 0))
    prev = pl.BlockSpec((None, BLOCK, HEAD_DIM), lambda h, n: (h, jnp.maximum(n - 1, 0), 0))
    cur = pl.BlockSpec((None, BLOCK, HEAD_DIM), lambda h, n: (h, n, 0))
    sink = pl.BlockSpec((None, Q_PER_KV * BLOCK, 1), lambda h, n: (h, 0, 0))
    masks = pl.BlockSpec((2, Q_PER_KV * BLOCK, 2 * BLOCK), lambda h, n: (0, 0, 0))
    return q_spec, prev, cur, sink, masks


def swa_fwd(q, k, v, sink_rows, name):
    n_tok = q.shape[1]
    q_spec, prev, cur, sink, masks = _swa_specs(n_tok // BLOCK)

    def body(q_ref, kp_ref, kc_ref, vp_ref, vc_ref, s_ref, m_ref, o_ref):
        mask = m_ref[jnp.minimum(pl.program_id(1), 1)]
        qq = q_ref[...].reshape(Q_PER_KV * BLOCK, HEAD_DIM)
        kk = jnp.concatenate([kp_ref[...], kc_ref[...]], axis=0)
        vv = jnp.concatenate([vp_ref[...], vc_ref[...]], axis=0)
        p, _ = _swa_probs(qq, kk, s_ref[...], mask)
        o = jnp.dot(p.astype(BF16), vv, preferred_element_type=F32)
        o_ref[...] = o.reshape(Q_PER_KV, BLOCK, HEAD_DIM).astype(o_ref.dtype)

    return pl.pallas_call(
        body, name=name, grid=(N_KV_HEADS, n_tok // BLOCK), in_specs=[q_spec, prev, cur, prev, cur, sink, masks], out_specs=q_spec,
        out_shape=jax.ShapeDtypeStruct(q.shape, BF16), compiler_params=_params(("parallel", "parallel")),
    )(q, k, k, v, v, sink_rows, _swa_masks())


def swa_bwd(q, k, v, sink_rows, do, name):
    n_tok = q.shape[1]
    nb = n_tok // BLOCK
    q_spec, prev, cur, sink, masks = _swa_specs(nb)
    rows = Q_PER_KV * BLOCK

    def body(q_ref, kp_ref, kc_ref, vp_ref, vc_ref, s_ref, m_ref, do_ref, dq_ref, dkp_ref, dkc_ref, dvp_ref, dvc_ref, ds_ref):
        mask = m_ref[jnp.minimum(pl.program_id(1), 1)]
        qq = q_ref[...].reshape(rows, HEAD_DIM)
        dd = do_ref[...].reshape(rows, HEAD_DIM)
        kk = jnp.concatenate([kp_ref[...], kc_ref[...]], axis=0)
        vv = jnp.concatenate([vp_ref[...], vc_ref[...]], axis=0)
        p, ps = _swa_probs(qq, kk, s_ref[...], mask)
        dp = lax.dot_general(dd, vv, (((1,), (1,)), ((), ())), preferred_element_type=F32)
        delta = jnp.sum(p * dp, axis=-1, keepdims=True)
        ds = (p * (dp - delta) * (HEAD_DIM ** -0.5)).astype(BF16)
        dq = jnp.dot(ds, kk, preferred_element_type=F32)
        dk = lax.dot_general(ds, qq, (((0,), (0,)), ((), ())), preferred_element_type=F32)
        dv = lax.dot_general(p.astype(BF16), dd, (((0,), (0,)), ((), ())), preferred_element_type=F32)
        dq_ref[...] = dq.reshape(Q_PER_KV, BLOCK, HEAD_DIM).astype(dq_ref.dtype)
        dkp_ref[...] = dk[:BLOCK]
        dkc_ref[...] = dk[BLOCK:]
        dvp_ref[...] = dv[:BLOCK]
        dvc_ref[...] = dv[BLOCK:]
        dsink = jnp.broadcast_to(-ps * delta, (rows, LANES)).reshape(Q_PER_KV, BLOCK, LANES)
        ds_ref[...] = jnp.sum(dsink, axis=1)

    part = pl.BlockSpec((None, None, BLOCK, HEAD_DIM), lambda h, n: (h, n, 0, 0))
    part_shape = jax.ShapeDtypeStruct((N_KV_HEADS, nb, BLOCK, HEAD_DIM), F32)
    return pl.pallas_call(
        body, name=name, grid=(N_KV_HEADS, nb), in_specs=[q_spec, prev, cur, prev, cur, sink, masks, q_spec],
        out_specs=[q_spec, part, part, part, part, pl.BlockSpec((None, None, Q_PER_KV, LANES), lambda h, n: (h, n, 0, 0))],
        out_shape=[jax.ShapeDtypeStruct(q.shape, BF16), part_shape, part_shape, part_shape, part_shape,
                   jax.ShapeDtypeStruct((N_KV_HEADS, nb, Q_PER_KV, LANES), F32)],
        compiler_params=_params(("parallel", "parallel")),
    )(q, k, k, v, v, sink_rows, _swa_masks(), do)


def _to_heads(t, n_heads):
    return t.reshape(t.shape[0], n_heads, HEAD_DIM).transpose(1, 0, 2)


def _from_heads(t):
    return t.transpose(1, 0, 2).reshape(t.shape[1], -1)


def _fold_kv_grad(prev, cur):
    shifted = jnp.concatenate([prev[:, 1:], jnp.zeros_like(prev[:, :1])], axis=1)
    tot = (cur + shifted).reshape(N_KV_HEADS, -1, HEAD_DIM)
    return _from_heads(tot)


def _x_probs(qh, kh):
    s = lax.dot_general(qh, kh, (((1,), (1,)), ((), ())), preferred_element_type=F32) * (X_HEAD_DIM ** -0.5)
    e = jnp.exp(s - jnp.max(s, axis=-1, keepdims=True))
    return e * (1.0 / jnp.sum(e, axis=-1, keepdims=True))


def xattn_fwd(q, k, v, name):
    n_tok, width = q.shape
    n_mem = k.shape[0]
    tq = _pick(n_tok, 512, 16)

    def body(q_ref, k_ref, v_ref, o_ref):
        for h in range(X_HEADS):
            cols = slice(h * X_HEAD_DIM, (h + 1) * X_HEAD_DIM)
            p = _x_probs(q_ref[:, cols], k_ref[:, cols])
            o_ref[:, cols] = jnp.dot(p.astype(BF16), v_ref[:, cols], preferred_element_type=F32).astype(o_ref.dtype)

    row = pl.BlockSpec((tq, width), lambda i: (i, 0))
    mem = pl.BlockSpec((n_mem, width), lambda i: (0, 0))
    return pl.pallas_call(
        body, name=name, grid=(n_tok // tq,), in_specs=[row, mem, mem], out_specs=row,
        out_shape=jax.ShapeDtypeStruct(q.shape, BF16), compiler_params=_params(("parallel",)),
    )(q, k, v)


def xattn_bwd(q, k, v, do, name):
    n_tok, width = q.shape
    n_mem = k.shape[0]
    tq = _pick(n_tok, 512, 16)

    def body(q_ref, k_ref, v_ref, do_ref, dq_ref, dk_ref, dv_ref):
        @pl.when(pl.program_id(0) == 0)
        def _():
            dk_ref[...] = jnp.zeros_like(dk_ref)
            dv_ref[...] = jnp.zeros_like(dv_ref)

        for h in range(X_HEADS):
            cols = slice(h * X_HEAD_DIM, (h + 1) * X_HEAD_DIM)
            qh, kh, vh, dh = q_ref[:, cols], k_ref[:, cols], v_ref[:, cols], do_ref[:, cols]
            p = _x_probs(qh, kh)
            dp = lax.dot_general(dh, vh, (((1,), (1,)), ((), ())), preferred_element_type=F32)
            delta = jnp.sum(p * dp, axis=-1, keepdims=True)
            ds = (p * (dp - delta) * (X_HEAD_DIM ** -0.5)).astype(BF16)
            dq_ref[:, cols] = jnp.dot(ds, kh, preferred_element_type=F32).astype(dq_ref.dtype)
            dk_ref[:, cols] += lax.dot_general(ds, qh, (((0,), (0,)), ((), ())), preferred_element_type=F32)
            dv_ref[:, cols] += lax.dot_general(p.astype(BF16), dh, (((0,), (0,)), ((), ())), preferred_element_type=F32)

    row = pl.BlockSpec((tq, width), lambda i: (i, 0))
    mem = pl.BlockSpec((n_mem, width), lambda i: (0, 0))
    return pl.pallas_call(
        body, name=name, grid=(n_tok // tq,), in_specs=[row, mem, mem, row], out_specs=[row, mem, mem],
        out_shape=[jax.ShapeDtypeStruct(q.shape, BF16), jax.ShapeDtypeStruct(k.shape, F32), jax.ShapeDtypeStruct(k.shape, F32)],
        compiler_params=_params(("arbitrary",)),
    )(q, k, v, do)


GELU_C = 0.7978845608028654
GELU_A = 0.044715


def _gelu_parts(x):
    x2 = x * x
    t = jnp.tanh(GELU_C * x * (1.0 + GELU_A * x2))
    y = 0.5 * x * (1.0 + t)
    dy = 0.5 * (1.0 + t) + 0.5 * x * (1.0 - t * t) * GELU_C * (1.0 + 3.0 * GELU_A * x2)
    return y, dy


def _sgu_norm(v, ln_g, ln_b):
    mu = jnp.mean(v, axis=-1, keepdims=True)
    vc = v - mu
    r = lax.rsqrt(jnp.mean(vc * vc, axis=-1, keepdims=True) + EPS)
    xhat = vc * r
    return xhat * ln_g + ln_b, xhat, r


def _causal_weights(w_ref):
    i = lax.broadcasted_iota(jnp.int32, (BLOCK, BLOCK), 0)
    j = lax.broadcasted_iota(jnp.int32, (BLOCK, BLOCK), 1)
    return [jnp.where(i >= j, w_ref[g], 0.0).astype(BF16) for g in range(SGU_GROUPS)]


def sgu_fwd(u_pre, v_pre, ln_g, ln_b, w_s, bias_rows, name):
    n_tok = u_pre.shape[0]
    tm = _pick(n_tok, 512, BLOCK)

    def body(u_ref, v_ref, g_ref, b_ref, w_ref, bb_ref, o_ref):
        vn, _, _ = _sgu_norm(_gelu_parts(v_ref[...])[0], g_ref[...], b_ref[...])
        vn = vn.astype(BF16)
        wc = _causal_weights(w_ref)
        for c in range(tm // BLOCK):
            rows = slice(c * BLOCK, (c + 1) * BLOCK)
            for g in range(SGU_GROUPS):
                cols = slice(g * LANES, (g + 1) * LANES)
                mixed = jnp.dot(wc[g], vn[rows, cols], preferred_element_type=F32) + bb_ref[g]
                u = _gelu_parts(u_ref[rows, cols])[0]
                o_ref[rows, cols] = (u * mixed).astype(o_ref.dtype)

    row = pl.BlockSpec((tm, SGU_WIDTH), lambda i: (i, 0))
    vec = pl.BlockSpec((1, SGU_WIDTH), lambda i: (0, 0))
    mat = pl.BlockSpec((SGU_GROUPS, BLOCK, LANES), lambda i: (0, 0, 0))
    return pl.pallas_call(
        body, name=name, grid=(n_tok // tm,), in_specs=[row, row, vec, vec, mat, mat], out_specs=row,
        out_shape=jax.ShapeDtypeStruct((n_tok, SGU_WIDTH), BF16), compiler_params=_params(("parallel",)),
    )(u_pre, v_pre, ln_g, ln_b, w_s, bias_rows)


def sgu_bwd(u_pre, v_pre, ln_g, ln_b, w_s, bias_rows, dgate, name):
    n_tok = u_pre.shape[0]
    tm = _pick(n_tok, 512, BLOCK)

    def body(u_ref, v_ref, g_ref, b_ref, w_ref, bb_ref, dg_ref, du_ref, dv_ref, dw_ref, db_ref, dlg_ref, dlb_ref, dvn_ref):
        @pl.when(pl.program_id(0) == 0)
        def _():
            dw_ref[...] = jnp.zeros_like(dw_ref)
            db_ref[...] = jnp.zeros_like(db_ref)
            dlg_ref[...] = jnp.zeros_like(dlg_ref)
            dlb_ref[...] = jnp.zeros_like(dlb_ref)

        gv, dgv = _gelu_parts(v_ref[...])
        vn, xhat, r = _sgu_norm(gv, g_ref[...], b_ref[...])
        vn = vn.astype(BF16)
        wc = _causal_weights(w_ref)
        for c in range(tm // BLOCK):
            rows = slice(c * BLOCK, (c + 1) * BLOCK)
            for g in range(SGU_GROUPS):
                cols = slice(g * LANES, (g + 1) * LANES)
                vt = vn[rows, cols]
                mixed = jnp.dot(wc[g], vt, preferred_element_type=F32) + bb_ref[g]
                u, du_dpre = _gelu_parts(u_ref[rows, cols])
                dgate_t = dg_ref[rows, cols].astype(F32)
                du_ref[rows, cols] = (dgate_t * mixed * du_dpre).astype(du_ref.dtype)
                dmix = dgate_t * u
                dmix_b = dmix.astype(BF16)
                db_ref[g] += dmix
                dw_ref[g] += lax.dot_general(dmix_b, vt, (((1,), (1,)), ((), ())), preferred_element_type=F32)
                dvn_ref[rows, cols] = lax.dot_general(wc[g], dmix_b, (((0,), (0,)), ((), ())), preferred_element_type=F32)
        dvn = dvn_ref[...]
        dlg_ref[...] += _rsum8(dvn * xhat)
        dlb_ref[...] += _rsum8(dvn)
        dxhat = dvn * g_ref[...]
        dgv_in = r * (dxhat - jnp.mean(dxhat, axis=-1, keepdims=True) - xhat * jnp.mean(dxhat * xhat, axis=-1, keepdims=True))
        dv_ref[...] = (dgv_in * dgv).astype(dv_ref.dtype)

    row = pl.BlockSpec((tm, SGU_WIDTH), lambda i: (i, 0))
    vec = pl.BlockSpec((1, SGU_WIDTH), lambda i: (0, 0))
    mat = pl.BlockSpec((SGU_GROUPS, BLOCK, LANES), lambda i: (0, 0, 0))
    part = pl.BlockSpec((8, SGU_WIDTH), lambda i: (0, 0))
    mat_shape = jax.ShapeDtypeStruct((SGU_GROUPS, BLOCK, LANES), F32)
    part_shape = jax.ShapeDtypeStruct((8, SGU_WIDTH), F32)
    act_shape = jax.ShapeDtypeStruct((n_tok, SGU_WIDTH), BF16)
    return pl.pallas_call(
        body, name=name, grid=(n_tok // tm,), in_specs=[row, row, vec, vec, mat, mat, row],
        out_specs=[row, row, mat, mat, part, part], out_shape=[act_shape, act_shape, mat_shape, mat_shape, part_shape, part_shape],
        scratch_shapes=[pltpu.VMEM((tm, SGU_WIDTH), F32)], compiler_params=_params(("arbitrary",)),
    )(u_pre, v_pre, ln_g, ln_b, w_s, bias_rows, dgate)


def _pool_tile(n_tok):
    return _pick(n_tok, 256, POOL_HALO)


def pool_fwd(h, name):
    n_tok, width = h.shape
    gw = width // len(POOL_WINDOWS)
    tm = _pool_tile(n_tok)
    per = tm // POOL_HALO

    def body(cur_ref, halo_ref, o_ref, buf_ref):
        i = pl.program_id(0)
        buf_ref[0:POOL_HALO, :] = jnp.where(i > 0, halo_ref[...], 0.0)
        buf_ref[POOL_HALO:, :] = cur_ref[...]
        tok = i * tm + lax.broadcasted_iota(jnp.int32, (tm, 1), 0)
        for g, w in enumerate(POOL_WINDOWS):
            cols = slice(g * gw, (g + 1) * gw)
            acc = buf_ref[POOL_HALO:, cols]
            for j in range(1, w):
                acc = acc + buf_ref[POOL_HALO - j:POOL_HALO - j + tm, cols]
            cnt = jnp.minimum(tok + 1, w).astype(F32)
            o_ref[:, cols] = (acc / cnt - cur_ref[:, cols]).astype(o_ref.dtype)

    return pl.pallas_call(
        body, name=name, grid=(n_tok // tm,),
        in_specs=[pl.BlockSpec((tm, width), lambda i: (i, 0)),
                  pl.BlockSpec((POOL_HALO, width), lambda i: (jnp.maximum(i * per - 1, 0), 0))],
        out_specs=pl.BlockSpec((tm, width), lambda i: (i, 0)), out_shape=jax.ShapeDtypeStruct(h.shape, BF16),
        scratch_shapes=[pltpu.VMEM((tm + POOL_HALO, width), F32)], compiler_params=_params(("parallel",)),
    )(h, h)


def pool_bwd(dp, name):
    n_tok, width = dp.shape
    gw = width // len(POOL_WINDOWS)
    tm = _pool_tile(n_tok)
    per = tm // POOL_HALO
    n_steps = n_tok // tm

    def body(cur_ref, halo_ref, o_ref, buf_ref):
        i = pl.program_id(0)
        tok = i * tm + lax.broadcasted_iota(jnp.int32, (tm, 1), 0)
        for g, w in enumerate(POOL_WINDOWS):
            cols = slice(g * gw, (g + 1) * gw)
            cnt = jnp.minimum(tok + 1, w).astype(F32)
            buf_ref[0:tm, cols] = cur_ref[:, cols] / cnt
            buf_ref[tm:, cols] = jnp.where(i < n_steps - 1, halo_ref[:, cols] / float(w), 0.0)
        for g, w in enumerate(POOL_WINDOWS):
            cols = slice(g * gw, (g + 1) * gw)
            acc = buf_ref[0:tm, cols]
            for j in range(1, w):
                acc = acc + buf_ref[j:j + tm, cols]
            o_ref[:, cols] = acc - cur_ref[:, cols]

    return pl.pallas_call(
        body, name=name, grid=(n_steps,),
        in_specs=[pl.BlockSpec((tm, width), lambda i: (i, 0)),
                  pl.BlockSpec((POOL_HALO, width), lambda i: (jnp.minimum((i + 1) * per, n_tok // POOL_HALO - 1), 0))],
        out_specs=pl.BlockSpec((tm, width), lambda i: (i, 0)), out_shape=jax.ShapeDtypeStruct(dp.shape, F32),
        scratch_shapes=[pltpu.VMEM((tm + POOL_HALO, width), F32)], compiler_params=_params(("parallel",)),
    )(dp, dp)


def _ffn_fwd(x, h, gb, wg, wu, wd, tag, nxt):
    G = _mm(h, wg, batch="map", out_dtype=BF16, name=f"{tag}_gate")
    wu = wu(G) if callable(wu) else wu
    U, A = _mm(h, wu, batch="map", extras=[G], epilogue=_swiglu_tiles, out_dtypes=[BF16] * 2, name=f"{tag}_up")
    wd = wd(A) if callable(wd) else wd
    y = _mm(A, wd, batch="reduce", out_dtype=BF16, name=f"{tag}_down")
    return (*close_block(x, y, gb, 0.5, nxt, tag), (x, h, G, U, A, y, wg, wu, wd))


def _ffn_bwd(res, ga, gb, dx2, tag, early):
    x, h, G, U, A, y, wg, wu, wd = res
    dy, dgb = rms_bwd(y, gb, [dx2], 0.5, None, BF16, f"{tag}_postnorm_bwd")
    dG, dU = _mm(dy, wd, tb=True, batch="map", extras=[G, U], epilogue=_swiglu_bwd_tiles, out_dtypes=[BF16] * 2, name=f"{tag}_down_dx")
    dwd = _mm(A, dy, ta=True, batch="map", name=f"{tag}_down_dw", out_dtype=BF16)
    dwg = _mm(h, dG, ta=True, batch="map", name=f"{tag}_gate_dw", out_dtype=BF16, order=early("wd", dwd))
    dwu = _mm(h, dU, ta=True, batch="map", name=f"{tag}_up_dw", out_dtype=BF16, order=early("wg", dwg))
    dh = _mm(dG, wg, tb=True, batch="reduce", more=[(dU, wu)], out_dtype=BF16, name=f"{tag}_gate_up_dx", order=early("wu", dwu))
    dx, dga = rms_bwd(x, ga, [dh], 1.0, dx2, F32, f"{tag}_prenorm_bwd")
    return dx, dga, dgb, dwg, dwu, dwd


def _sink_rows(sinks):
    return jnp.repeat(sinks.reshape(N_KV_HEADS, Q_PER_KV), BLOCK, axis=1)[..., None]


def _attn_sgu_fwd(x, h, g_post, w_in, w_out, sinks, ln_g, ln_b, sgu_w, bias_rows, tabs, tag, nxt):
    z = _mm(h, w_in, name=f"{tag}_in")
    qk = rope_apply(z, tabs, QK_WIDTH // LANES, False, BF16, f"{tag}_rope")
    q = _to_heads(qk[:, :ATTN_WIDTH], N_Q_HEADS)
    k = _to_heads(qk[:, ATTN_WIDTH:], N_KV_HEADS)
    v = _to_heads(z[:, QK_WIDTH:QK_WIDTH + KV_WIDTH].astype(BF16), N_KV_HEADS)
    o = swa_fwd(q, k, v, _sink_rows(sinks), f"{tag}_swa")
    u_pre = z[:, QK_WIDTH + KV_WIDTH:QK_WIDTH + KV_WIDTH + SGU_WIDTH]
    v_pre = z[:, QK_WIDTH + KV_WIDTH + SGU_WIDTH:]
    gate = sgu_fwd(u_pre, v_pre, ln_g, ln_b, sgu_w, bias_rows, f"{tag}_sgu")
    cat = jnp.concatenate([_from_heads(o), gate], axis=1)
    m = _mm(cat, w_out, out_dtype=BF16, name=f"{tag}_out")
    return (*close_block(x, m, g_post, 1.0, nxt, tag), (x, h, q, k, v, u_pre, v_pre, cat, m))


def _attn_sgu_bwd(res, g_pre, g_post, w_in, w_out, sinks, ln_g, ln_b, sgu_w, bias_rows, tabs, dx2, tag):
    x, h, q, k, v, u_pre, v_pre, cat, m = res
    dm, dg_post = rms_bwd(m, g_post, [dx2], 1.0, None, BF16, f"{tag}_postnorm_bwd")
    dcat = _mm(dm, w_out, tb=True, out_dtype=BF16, name=f"{tag}_out_dx")
    dw_out = _mm(cat, dm, ta=True, name=f"{tag}_out_dw", out_dtype=BF16)
    do = _to_heads(dcat[:, :ATTN_WIDTH], N_Q_HEADS)
    dq, dkp, dkc, dvp, dvc, dsink = swa_bwd(q, k, v, _sink_rows(sinks), do, f"{tag}_swa_bwd")
    d_sinks = jnp.sum(dsink[..., 0], axis=1).reshape(1, N_Q_HEADS)
    dqk_rot = jnp.concatenate([_from_heads(dq).astype(F32), _fold_kv_grad(dkp, dkc)], axis=1)
    dqk = rope_apply(dqk_rot, tabs, QK_WIDTH // LANES, True, BF16, f"{tag}_rope_bwd")
    dv = _fold_kv_grad(dvp, dvc).astype(BF16)
    du_pre, dv_pre, dw_s, dbias, dlg, dlb = sgu_bwd(u_pre, v_pre, ln_g, ln_b, sgu_w, bias_rows, dcat[:, ATTN_WIDTH:], f"{tag}_sgu_bwd")
    dz = jnp.concatenate([dqk, dv, du_pre, dv_pre], axis=1)
    dw_in = _mm(h, dz, ta=True, name=f"{tag}_in_dw", out_dtype=BF16)
    dh = _mm(dz, w_in, tb=True, out_dtype=BF16, name=f"{tag}_in_dx")
    dx, dg_pre = rms_bwd(x, g_pre, [dh], 1.0, dx2, F32, f"{tag}_prenorm_bwd")
    causal = jnp.tril(jnp.ones((BLOCK, BLOCK), F32))
    small = dict(attn_sinks=d_sinks, sgu_ln_g=jnp.sum(dlg, axis=0, keepdims=True), sgu_ln_b=jnp.sum(dlb, axis=0, keepdims=True),
                 sgu_w=(dw_s * causal[None])[None], sgu_b=jnp.sum(dbias, axis=-1)[None])
    return dx, dg_pre, dg_post, dw_in, dw_out, small


def _pool_mix_fwd(x, hf, g_post, pool_w, pool_scale, tag, nxt):
    pooled = pool_fwd(hf, f"{tag}_pool")
    n_g = len(POOL_WINDOWS)
    ypre = _mm(pooled, pool_w, batch="map", groups=n_g, a_cb=True, o_cb=True, name=f"{tag}_proj")
    m = scale_cols(ypre, pool_scale, f"{tag}_scale")
    return (*close_block(x, m, g_post, 1.0, nxt, tag), (x, pooled, ypre, m))


def _pool_mix_bwd(res, g_pre, g_post, pool_w, pool_scale, dx2, tag):
    x, pooled, ypre, m = res
    n_g = len(POOL_WINDOWS)
    dm, dg_post = rms_bwd(m, g_post, [dx2], 1.0, None, F32, f"{tag}_postnorm_bwd")
    dypre, dscale = scale_cols_bwd(dm, ypre, pool_scale, f"{tag}_scale_bwd")
    dpooled = _mm(dypre, pool_w, tb=True, batch="map", groups=n_g, a_cb=True, o_cb=True, name=f"{tag}_proj_dx")
    dpw = _mm(pooled, dypre, ta=True, batch="map", groups=n_g, a_cb=True, b_cb=True, name=f"{tag}_proj_dw", out_dtype=BF16)
    dhf = pool_bwd(dpooled, f"{tag}_pool_bwd")
    dx, dg_pre = rms_bwd(x, g_pre, [dhf], 1.0, dx2, F32, f"{tag}_prenorm_bwd")
    return dx, dg_pre, dg_post, dpw, jnp.sum(dscale, axis=0, keepdims=True)


def _xattn_fwd(x, h, mem, g_post, g_mem, wq, wk, wv, wo, tag, nxt):
    mem_n = rms_fwd(mem, g_mem, BF16, f"{tag}_memnorm")
    q = _mm(h, wq, out_dtype=BF16, name=f"{tag}_q")
    k = _mm(mem_n, wk, out_dtype=BF16, name=f"{tag}_k")
    v = _mm(mem_n, wv, out_dtype=BF16, name=f"{tag}_v")
    o = xattn_fwd(q, k, v, f"{tag}_core")
    r = _mm(o, wo, out_dtype=BF16, name=f"{tag}_o")
    return (*close_block(x, r, g_post, 1.0, nxt, tag), (x, h, mem_n, q, k, v, o, r))


def _xattn_bwd(res, mem, g_pre, g_post, g_mem, wq, wk, wv, wo, dx2, tag):
    x, h, mem_n, q, k, v, o, r = res
    dr, dg_post = rms_bwd(r, g_post, [dx2], 1.0, None, BF16, f"{tag}_postnorm_bwd")
    do = _mm(dr, wo, tb=True, out_dtype=BF16, name=f"{tag}_o_dx")
    dwo = _mm(o, dr, ta=True, name=f"{tag}_o_dw", out_dtype=BF16)
    dq, dk, dv = xattn_bwd(q, k, v, do, f"{tag}_core_bwd")
    dk, dv = dk.astype(BF16), dv.astype(BF16)
    dwq = _mm(h, dq, ta=True, name=f"{tag}_q_dw", out_dtype=BF16)
    dwk = _mm(mem_n, dk, ta=True, name=f"{tag}_k_dw", out_dtype=BF16)
    dwv = _mm(mem_n, dv, ta=True, name=f"{tag}_v_dw", out_dtype=BF16)
    dh = _mm(dq, wq, tb=True, out_dtype=BF16, name=f"{tag}_q_dx")
    dmem1 = _mm(dk, wk, tb=True, name=f"{tag}_k_dx")
    dmem2 = _mm(dv, wv, tb=True, name=f"{tag}_v_dx")
    _, dg_mem = rms_bwd(mem, g_mem, [dmem1, dmem2], 1.0, None, BF16, f"{tag}_memnorm_bwd")
    dx, dg_pre = rms_bwd(x, g_pre, [dh], 1.0, dx2, F32, f"{tag}_prenorm_bwd")
    return dx, dg_pre, dg_post, dg_mem, dwq, dwk, dwv, dwo


def _rowsum8(part):
    return jnp.sum(part, axis=0, keepdims=True)


def device_step(x, mem, target, norms, mem_norm, small, fetch, emit, early):
    n_tok = x.shape[0]
    tabs = rope_tables(n_tok)
    bias_rows = jnp.broadcast_to(small["sgu_b"][:, :, None], (SGU_GROUPS, BLOCK, LANES))
    gn = lambda l, i: norms[l, i][None, :]
    gm = lambda l: mem_norm[l][None, :]
    mix0 = (small["attn_sinks"], small["sgu_ln_g"], small["sgu_ln_b"], small["sgu_w"], bias_rows, tabs)

    wts, saved = {}, []
    pre_gain = {"ffn1": 0, "mix": 2, "xattn": 4, "ffn2": 6}
    in_dtype = lambda kind, l: F32 if (kind, l) == ("mix", 1) else BF16

    def then(kind, l):
        def nxt(after):
            wts[kind, l], tok = fetch((kind, l), after)
            return gn(l, pre_gain[kind]) + tok, in_dtype(kind, l)
        return nxt

    g_first, _ = then("ffn1", 0)(x)
    h = rms_fwd(x, g_first, BF16, "l0_ffn1_prenorm")
    for l in range(2):
        x, h, r1 = _ffn_fwd(x, h, gn(l, 1), *wts["ffn1", l], f"l{l}_ffn1", then("mix", l))
        if l == 0:
            x, h, r2 = _attn_sgu_fwd(x, h, gn(l, 3), *wts["mix", l], *mix0, f"l{l}_mix", then("xattn", l))
        else:
            x, h, r2 = _pool_mix_fwd(x, h, gn(l, 3), *wts["mix", l], small["pool_scale"], f"l{l}_mix", then("xattn", l))
        x, h, r3 = _xattn_fwd(x, h, mem, gn(l, 5), gm(l), *wts["xattn", l], f"l{l}_xattn", then("ffn2", l))
        x, h, r4 = _ffn_fwd(x, h, gn(l, 7), *wts["ffn2", l], f"l{l}_ffn2", then("ffn1", l + 1) if l == 0 else None)
        saved.append((r1, r2, r3, r4))

    loss_part, dx = loss_and_grad(x, target, "loss")

    g_norm_rows = [[None] * 8, [None] * 8]
    g_mem_rows = [None, None]
    g_small = {}
    tok = 0.0
    for l in (1, 0):
        r1, r2, r3, r4 = saved[l]
        dx, g_norm_rows[l][6], g_norm_rows[l][7], dwg, dwu, dwd = _ffn_bwd(
            r4, gn(l, 6), gn(l, 7) + tok, dx, f"l{l}_ffn2", lambda part, g, l=l: early(("ffn2", l), f"ffn2_{part}", g))
        tok = emit(("ffn2", l), dict(ffn2_wg=dwg, ffn2_wu=dwu, ffn2_wd=dwd), dx)
        dx, g_norm_rows[l][4], g_norm_rows[l][5], g_mem_rows[l], dwq, dwk, dwv, dwo = _xattn_bwd(
            r3, mem, gn(l, 4), gn(l, 5) + tok, gm(l), *wts["xattn", l], dx, f"l{l}_xattn")
        tok = emit(("xattn", l), dict(x_wq=dwq, x_wk=dwk, x_wv=dwv, x_wo=dwo), dx)
        if l == 0:
            dx, g_norm_rows[l][2], g_norm_rows[l][3], dw_in, dw_out, sm = _attn_sgu_bwd(
                r2, gn(l, 2), gn(l, 3) + tok, *wts["mix", l], *mix0, dx, f"l{l}_mix")
            tok = emit(("mix", l), dict(mix_w_in=dw_in, mix_w_out=dw_out), dx)
            g_small.update(sm)
        else:
            dx, g_norm_rows[l][2], g_norm_rows[l][3], dpw, dscale = _pool_mix_bwd(
                r2, gn(l, 2), gn(l, 3) + tok, *wts["mix", l], small["pool_scale"], dx, f"l{l}_mix")
            tok = emit(("mix", l), dict(pool_w=dpw), dx)
            g_small["pool_scale"] = dscale
        dx, g_norm_rows[l][0], g_norm_rows[l][1], dwg, dwu, dwd = _ffn_bwd(
            r1, gn(l, 0), gn(l, 1) + tok, dx, f"l{l}_ffn1", lambda part, g, l=l: early(("ffn1", l), f"ffn1_{part}", g))
        tok = emit(("ffn1", l), dict(ffn1_wg=dwg, ffn1_wu=dwu, ffn1_wd=dwd), dx)
    g_norms = jnp.stack([jnp.concatenate([_rowsum8(p) for p in g_norm_rows[l]], axis=0) for l in range(2)])
    g_mem_norm = jnp.concatenate([_rowsum8(p) for p in g_mem_rows], axis=0)
    return loss_part, dx, g_small, g_norms, g_mem_norm, tok


ANY = pl.BlockSpec(memory_space=pl.ANY)


def _place():
    x, y, c = lax.axis_index("x"), lax.axis_index("y"), lax.axis_index("c")
    other_chips = [(1 - x, y), (x, 1 - y), (1 - x, 1 - y)]
    return x, y, c, other_chips


def _half_rows(core, n_rows):
    half = n_rows // 2
    return pl.ds(pl.multiple_of(core * half, 16), half)


def _remote(src, dst, send_sem, recv_sem, device):
    return pltpu.make_async_remote_copy(src_ref=src, dst_ref=dst, send_sem=send_sem, recv_sem=recv_sem,
                                        device_id=device, device_id_type=MESH)


HBM = pl.BlockSpec(memory_space=pltpu.HBM)
SEM = pl.BlockSpec(memory_space=pltpu.SEMAPHORE)
DATAFLOW = pltpu.SideEffectType.DATAFLOW_SIDE_EFFECTING


def _chip_copies(bufs, send_sems, recv_sems):
    x, y, c, chips = _place()
    me = 2 * x + y
    sends, arrivals = [], []
    for i, buf in enumerate(bufs):
        rows = _half_rows(c, buf.shape[1])
        for r, (px, py) in enumerate(chips):
            mine, theirs = buf.at[me, rows], buf.at[2 * px + py, rows]
            sends.append(_remote(mine, mine, send_sems.at[3 * i + r], recv_sems.at[3 * i + r], (px, py, c)))
            arrivals.append(_remote(theirs, theirs, send_sems.at[3 * i + r], recv_sems.at[3 * i + r], (px, py, c)))
    return sends, arrivals


def copies_start(arrays, fresh, copies, n_sems, after, name):
    operands = [pltpu.with_memory_space_constraint(a, pltpu.HBM) for a in arrays]
    operands += [pltpu.with_memory_space_constraint(lax.empty(f.shape, f.dtype), pltpu.HBM) for f in fresh]
    n = len(operands)

    def body(*refs):
        send_sems, recv_sems = refs[n + 1], refs[n + 2]
        bufs, token = refs[n + 3:2 * n + 3], refs[2 * n + 3]
        for cp in copies(bufs, send_sems, recv_sems)[0]:
            cp.start()
        token[...] = jnp.zeros_like(token)

    res = pl.pallas_call(
        body, name=name, in_specs=[HBM] * n + [ANY],
        out_specs=[SEM, SEM] + [HBM] * n + [pl.BlockSpec(memory_space=pltpu.VMEM)],
        out_shape=[pltpu.SemaphoreType.DMA((n_sems,)), pltpu.SemaphoreType.DMA((n_sems,))]
        + [pltpu.HBM(o.shape, o.dtype) for o in operands] + [jax.ShapeDtypeStruct((8, LANES), F32)],
        input_output_aliases={i: 2 + i for i in range(n)}, compiler_params=pltpu.CompilerParams(has_side_effects=DATAFLOW),
    )(*operands, after)
    return res[0], res[1], list(res[2:2 + n]), res[2 + n]


def copies_wait(send_sems, recv_sems, bufs, copies, after, name):
    n = len(bufs)

    def body(*refs):
        for cp in copies(refs[:n], refs[n], refs[n + 1])[1]:
            cp.wait_send()
            cp.wait_recv()

    return list(pl.pallas_call(
        body, name=name, in_specs=[HBM] * n + [SEM, SEM] + [ANY] * len(after), out_specs=[HBM] * n,
        out_shape=[pltpu.HBM(b.shape, b.dtype) for b in bufs], input_output_aliases={i: i for i in range(n)},
        compiler_params=pltpu.CompilerParams(has_side_effects=DATAFLOW),
    )(*bufs, send_sems, recv_sems, *after))


def _to_sibling_copies(n_src, src_rows):
    def copies(bufs, send_sems, recv_sems):
        x, y, c, _ = _place()
        sends, arrivals = [], []
        for i in range(n_src):
            src, land = bufs[i], bufs[n_src + i]
            src = src.at[:, src_rows(c, src)] if src_rows is not None else src
            sends.append(_remote(src, land, send_sems.at[i], recv_sems.at[i], (x, y, 1 - c)))
            arrivals.append(_remote(land, land, send_sems.at[i], recv_sems.at[i], (x, y, 1 - c)))
        return sends, arrivals
    return copies


def _forward_copies(bufs, send_sems, recv_sems):
    x, y, c, chips = _place()
    sends, arrivals = [], []
    for i, buf in enumerate(bufs):
        for r, (px, py) in enumerate(chips):
            mine = buf.at[2 * px + py, _half_rows(c, buf.shape[1])]
            theirs = buf.at[2 * px + py, _half_rows(1 - c, buf.shape[1])]
            sends.append(_remote(mine, mine, send_sems.at[3 * i + r], recv_sems.at[3 * i + r], (x, y, 1 - c)))
            arrivals.append(_remote(theirs, theirs, send_sems.at[3 * i + r], recv_sems.at[3 * i + r], (x, y, 1 - c)))
    return sends, arrivals


def forward_to_sibling(bufs, name):
    n = len(bufs)

    def body(*refs):
        sends, arrivals = _forward_copies(refs[n:2 * n], *refs[2 * n:])
        for cp in sends:
            cp.start()
        for cp in arrivals:
            cp.wait_recv()
        for cp in sends:
            cp.wait_send()

    return pl.pallas_call(
        body, name=name, in_specs=[ANY] * n, out_specs=[ANY] * n, input_output_aliases={i: i for i in range(n)},
        out_shape=[jax.ShapeDtypeStruct(b.shape, b.dtype) for b in bufs],
        scratch_shapes=[pltpu.SemaphoreType.DMA((3 * n,)), pltpu.SemaphoreType.DMA((3 * n,))],
    )(*bufs)


def add_own_half(g, p, core, name):
    n_j, n_rows, n_cols = g.shape
    half = n_rows // 2
    tr = _rows(half, n_cols * (2 * g.dtype.itemsize + 2), WEIGHT_TILE_BYTES)

    def body(c_ref, g_ref, p_ref, o_ref):
        o_ref[...] = (g_ref[...] + p_ref[...]).astype(o_ref.dtype)

    grid_spec = pltpu.PrefetchScalarGridSpec(
        num_scalar_prefetch=1, grid=(n_j, half // tr),
        in_specs=[pl.BlockSpec((None, None, tr, n_cols), lambda j, i, c_ref: (j, c_ref[0], i, 0)),
                  pl.BlockSpec((None, tr, n_cols), lambda j, i, c_ref: (j, i, 0))],
        out_specs=pl.BlockSpec((None, tr, n_cols), lambda j, i, c_ref: (j, i, 0)))
    return pl.pallas_call(
        body, name=name, grid_spec=grid_spec, out_shape=jax.ShapeDtypeStruct((n_j, half, n_cols), BF16),
        compiler_params=_params(("parallel", "parallel")),
    )(core, g.reshape(n_j, 2, half, n_cols), p)


def _scatter_copies(n_parts):
    def copies(bufs, send_sems, recv_sems):
        x, y, c, chips = _place()
        me = 2 * x + y
        sends, arrivals = [], []
        for i in range(n_parts):
            part, land = bufs[i], bufs[n_parts + i]
            for r, (px, py) in enumerate(chips):
                theirs = land.at[2 * px + py]
                sends.append(_remote(part.at[2 * px + py], land.at[me], send_sems.at[3 * i + r], recv_sems.at[3 * i + r], (px, py, c)))
                arrivals.append(_remote(theirs, theirs, send_sems.at[3 * i + r], recv_sems.at[3 * i + r], (px, py, c)))
        return sends, arrivals
    return copies


def sum_over_chips(own, got, chip, name):
    n_s, n_rows, n_cols = got.shape
    tr = _rows(n_rows, n_cols * (got.dtype.itemsize * (n_s + 1) + 4), WEIGHT_TILE_BYTES)

    def body(chip_ref, own_ref, *refs):
        got_refs, o_ref = refs[:n_s], refs[n_s]
        me = chip_ref[0]
        acc = jnp.where(me == 0, own_ref[...], got_refs[0][...]).astype(F32)
        for k in range(1, n_s):
            acc = acc + jnp.where(me == k, own_ref[...], got_refs[k][...]).astype(F32)
        o_ref[...] = acc

    def slot(k):
        return pl.BlockSpec((None, tr, n_cols), lambda i, chip_ref: (jnp.where(chip_ref[0] == k, (k + 1) % n_s, k), i, 0))

    grid_spec = pltpu.PrefetchScalarGridSpec(
        num_scalar_prefetch=1, grid=(n_rows // tr,),
        in_specs=[pl.BlockSpec((None, tr, n_cols), lambda i, chip_ref: (chip_ref[0], i, 0))] + [slot(k) for k in range(n_s)],
        out_specs=pl.BlockSpec((tr, n_cols), lambda i, chip_ref: (i, 0)))
    return pl.pallas_call(
        body, name=name, grid_spec=grid_spec, out_shape=jax.ShapeDtypeStruct((n_rows, n_cols), F32),
        compiler_params=_params(("parallel",)),
    )(chip, own, *([got] * n_s))


def sum_slots(q, name):
    n_s, n_rows, n_cols = q.shape
    tr = _rows(n_rows, n_cols * (q.dtype.itemsize * n_s + 4))

    def body(q_ref, o_ref):
        acc = q_ref[0].astype(F32)
        for s in range(1, n_s):
            acc = acc + q_ref[s].astype(F32)
        o_ref[...] = acc

    return pl.pallas_call(
        body, name=name, grid=(n_rows // tr,), in_specs=[pl.BlockSpec((n_s, tr, n_cols), lambda i: (0, i, 0))],
        out_specs=pl.BlockSpec((tr, n_cols), lambda i: (i, 0)), out_shape=jax.ShapeDtypeStruct((n_rows, n_cols), F32),
        compiler_params=_params(("parallel",)),
    )(q)


def gather_all_devices(s, name):
    def body(s_ref, o_ref, send_sems, recv_sems, local_sem):
        x, y, c, _ = _place()
        me = 4 * x + 2 * y + c
        local = pltpu.make_async_copy(s_ref, o_ref.at[me], local_sem)
        local.start()
        sends = []
        for f in range(1, N_DEV):
            px, py, pc = x ^ (f >> 2), y ^ ((f >> 1) & 1), c ^ (f & 1)
            sends.append(_remote(s_ref, o_ref.at[me], send_sems.at[f - 1], recv_sems.at[f - 1], (px, py, pc)))
            sends[-1].start()
        for f in range(1, N_DEV):
            px, py, pc = x ^ (f >> 2), y ^ ((f >> 1) & 1), c ^ (f & 1)
            landed = o_ref.at[4 * px + 2 * py + pc]
            _remote(landed, landed, send_sems.at[f - 1], recv_sems.at[f - 1], (px, py, pc)).wait_recv()
        for cp in sends:
            cp.wait_send()
        local.wait()

    return pl.pallas_call(
        body, name=name, in_specs=[ANY], out_specs=ANY, out_shape=jax.ShapeDtypeStruct((N_DEV,) + s.shape, s.dtype),
        scratch_shapes=[pltpu.SemaphoreType.DMA((N_DEV - 1,)), pltpu.SemaphoreType.DMA((N_DEV - 1,)), pltpu.SemaphoreType.DMA],
    )(s)


FFN_NAMES = ("ffn1_wg", "ffn1_wu", "ffn1_wd", "ffn2_wg", "ffn2_wu", "ffn2_wd")
XATTN_NAMES = ("x_wq", "x_wk", "x_wv", "x_wo")
BIG_NAMES = FFN_NAMES + XATTN_NAMES + ("mix_w_in", "mix_w_out", "pool_w")
SMALL_NAMES = ("norms", "mem_norm", "attn_sinks", "sgu_ln_g", "sgu_ln_b", "sgu_w", "sgu_b", "pool_scale")
WEIGHT_ORDER = ("norms", "mem_norm") + BIG_NAMES[:-1] + ("attn_sinks", "sgu_ln_g", "sgu_ln_b", "sgu_w", "sgu_b", "pool_w", "pool_scale")
COLUMN_CUT = ("x_wo", "mix_w_in")
N_POOL = len(POOL_WINDOWS)
BLOCK_ORDER = (("ffn1", 0), ("mix", 0), ("xattn", 0), ("ffn2", 0), ("ffn1", 1), ("mix", 1), ("xattn", 1), ("ffn2", 1))
PREFETCH_AT = ((1, 2, 3), (), (), (4,), (5, 6, 7), (), (), ())


def block_weight_names(kind, layer):
    if kind == "mix":
        return ("mix_w_in", "mix_w_out") if layer == 0 else ("pool_w",)
    return XATTN_NAMES if kind == "xattn" else tuple(f"{kind}_{part}" for part in ("wg", "wu", "wd"))


def _to_matmul_layout(name, g):
    n_j, n_rows, n_cols = g.shape
    if name in COLUMN_CUT:
        return g.transpose(1, 0, 2).reshape(n_rows, n_j * n_cols)
    if name == "pool_w":
        return g.reshape(n_j, N_POOL, n_rows // N_POOL, n_cols).transpose(1, 0, 2, 3).reshape(N_POOL, n_j * n_rows // N_POOL, n_cols)
    if name in ("x_wq", "x_wk", "x_wv", "mix_w_out"):
        return g.reshape(n_j * n_rows, n_cols)
    return g


def _from_matmul_layout(name, d):
    if name in COLUMN_CUT:
        n_rows, wide = d.shape
        return d.reshape(n_rows, N_CHIPS, wide // N_CHIPS).transpose(1, 0, 2)
    if name == "pool_w":
        n_g, n_in, n_cols = d.shape
        return d.reshape(n_g, N_CHIPS, n_in // N_CHIPS, n_cols).transpose(1, 0, 2, 3).reshape(N_CHIPS, n_g * n_in // N_CHIPS, n_cols)
    if name in ("x_wq", "x_wk", "x_wv", "mix_w_out"):
        return d.reshape(N_CHIPS, d.shape[0] // N_CHIPS, d.shape[1])
    return d


def _as3(w):
    return w.reshape(w.shape[0], -1, w.shape[-1])


def _pack(arrays, row_multiple):
    flat = jnp.concatenate([a.reshape(-1) for a in arrays])
    per = LANES * row_multiple
    total = -(-flat.shape[0] // per) * per
    return jnp.pad(flat, (0, total - flat.shape[0])).reshape(total // LANES, LANES)


def _unpack(packed, like):
    flat, out, at = packed.reshape(-1), [], 0
    for a in like:
        out.append(flat[at:at + a.size].reshape(a.shape))
        at += a.size
    return out


def kernel(x, mem, norms, mem_norm, ffn1_wg, ffn1_wu, ffn1_wd, ffn2_wg, ffn2_wu, ffn2_wd, x_wq, x_wk, x_wv, x_wo, mix_w_in, mix_w_out, attn_sinks, sgu_ln_g, sgu_ln_b, sgu_w, sgu_b, pool_w, pool_scale, loss_target, m_norms, m_mem_norm, m_ffn1_wg, m_ffn1_wu, m_ffn1_wd, m_ffn2_wg, m_ffn2_wu, m_ffn2_wd, m_x_wq, m_x_wk, m_x_wv, m_x_wo, m_mix_w_in, m_mix_w_out, m_attn_sinks, m_sgu_ln_g, m_sgu_ln_b, m_sgu_w, m_sgu_b, m_pool_w, m_pool_scale, v_norms, v_mem_norm, v_ffn1_wg, v_ffn1_wu, v_ffn1_wd, v_ffn2_wg, v_ffn2_wu, v_ffn2_wd, v_x_wq, v_x_wk, v_x_wv, v_x_wo, v_mix_w_in, v_mix_w_out, v_attn_sinks, v_sgu_ln_g, v_sgu_ln_b, v_sgu_w, v_sgu_b, v_pool_w, v_pool_scale):
    given = dict(locals())
    w = {n: given[n] for n in WEIGHT_ORDER}
    mom = {n: given["m_" + n] for n in WEIGHT_ORDER}
    var = {n: given["v_" + n] for n in WEIGHT_ORDER}
    chip_id = 2 * lax.axis_index("x") + lax.axis_index("y")
    chip = chip_id.astype(jnp.int32).reshape(1)
    core = lax.axis_index("c").astype(jnp.int32).reshape(1)
    n_shard = norms.shape[-1]

    keys = [(name, l) for name in BIG_NAMES for l in range(_as3(w[name]).shape[0])]
    first_keys = [(name, 0) for name in block_weight_names(*BLOCK_ORDER[0])]
    cast = lambda name, l: cast_into_slot(_as3(w[name]), l, chip, f"cast_{name}")
    slot_of = {key: cast(*key) for key in first_keys}
    small_rows = jnp.concatenate([norms.reshape(-1, n_shard), pool_scale, jnp.zeros((15, n_shard), F32)], axis=0)
    small_slot = lax.dynamic_update_slice_in_dim(jnp.zeros((N_CHIPS,) + small_rows.shape, F32), small_rows[None], chip_id, axis=0)
    pending = {}
    gather_start = lambda slots, after, name: copies_start(slots, [], _chip_copies, 3 * len(slots), after, name)
    pending["0a"] = gather_start([slot_of[first_keys[0]], small_slot], chip, "gather_start_0a")
    pending["0b"] = gather_start([slot_of[first_keys[1]]], pending["0a"][2][0], "gather_start_0b")
    pending["0c"] = gather_start([slot_of[first_keys[2]]], pending["0b"][2][0], "gather_start_0c")
    for key in keys:
        if key not in slot_of:
            slot_of[key] = cast(*key)


    def block_slots(k):
        kind, layer = BLOCK_ORDER[k]
        return [slot_of[name, layer if kind != "mix" else 0] for name in block_weight_names(kind, layer)]

    def start(k, after):
        pending[k] = gather_start(block_slots(k), after, f"gather_start_{k}")

    def finish(k, after):
        send_sems, recv_sems, bufs, _ = pending[k]
        return forward_to_sibling(copies_wait(send_sems, recv_sems, bufs, _chip_copies, [after], f"gather_wait_{k}"), f"gather_forward_{k}")

    def fetch(key, after):
        k = BLOCK_ORDER.index(key)
        names = block_weight_names(*key)
        token = 0.0
        if k == 0:
            late = lambda part, name: lambda after: _to_matmul_layout(name, finish(part, after)[0])
            bufs = [first_gate]
            weights = (_to_matmul_layout(names[0], first_gate), late("0b", names[1]), late("0c", names[2]))
        elif key[0] in ("ffn1", "ffn2"):
            send_sems, recv_sems, bufs, _ = pending[k]
            bufs = copies_wait(send_sems, recv_sems, bufs, _chip_copies, [after], f"gather_wait_{k}")
            gate, = forward_to_sibling(bufs[:1], f"gather_forward_{k}_gate")
            send_sems, recv_sems, rest, begun = copies_start(bufs[1:], [], _forward_copies, 6, gate, f"gather_forward_{k}_start")
            landed = []

            def late(i, name):
                def get(after):
                    if not landed:
                        landed.extend(copies_wait(send_sems, recv_sems, rest, _forward_copies, [after], f"gather_forward_{k}_wait"))
                    return _to_matmul_layout(name, landed[i])
                return get
            bufs = [gate]
            weights = (_to_matmul_layout(names[0], gate), late(0, names[1]), late(1, names[2]))
            token = begun[0, 0]
        else:
            bufs = finish(k, after)
            weights = tuple(_to_matmul_layout(name, b) for name, b in zip(names, bufs))
        for ahead in PREFETCH_AT[k]:
            start(ahead, bufs[0])
            token = token + pending[ahead][3][0, 0]
        return weights, token

    to_sibling, to_chips, halves, own_of, recv_of = [], [], [], {}, {}
    other_half = lambda c, src: _half_rows(1 - c, src.shape[1])

    def sibling_start(tag, names, full, after):
        lands = [jax.ShapeDtypeStruct((g.shape[0], g.shape[1] // 2, g.shape[2]), g.dtype) for g in full]
        send_sems, recv_sems, bufs, token = copies_start(full, lands, _to_sibling_copies(len(full), other_half), len(full), after,
                                                         f"grads_to_sibling_{tag}")
        to_sibling.append((tag, names, send_sems, recv_sems, bufs))
        return token

    def early(key, name, g):
        kind, layer = key
        return sibling_start(f"{name}{layer}", [(name, layer)], [_from_matmul_layout(name, g)], chip)

    def chips_start(tag, after):
        names, chip_sums = [], []
        for part, part_names, send_sems, recv_sems, bufs in to_sibling:
            n = len(part_names)
            bufs = copies_wait(send_sems, recv_sems, bufs, _to_sibling_copies(n, other_half), after, f"grads_from_sibling_{part}")
            chip_sums += [add_own_half(g, p, core, f"chip_sum_{name}") for (name, _), g, p in zip(part_names, bufs[:n], bufs[n:])]
            names += part_names
        to_sibling.clear()
        n = len(names)
        send_sems, recv_sems, bufs, token = copies_start(chip_sums, chip_sums, _scatter_copies(n), 3 * n, after[0], f"grads_start_{tag}")
        to_chips.append((tag, names, send_sems, recv_sems, bufs))
        return token[0, 0]

    def chips_wait(after):
        tag, names, send_sems, recv_sems, bufs = to_chips.pop()
        return tag, names, copies_wait(send_sems, recv_sems, bufs, _scatter_copies(len(names)), after, f"grads_wait_{tag}")

    def reduce_landed(tag, names, bufs):
        n = len(names)
        own = [sum_over_chips(t, q, chip, f"sum_{name}") for (name, _), t, q in zip(names, bufs[:n], bufs[n:])]
        send_sems, recv_sems, bufs, _ = copies_start(own, own, _to_sibling_copies(n, None), n, chip, f"grads_halves_{tag}")
        halves.append((tag, names, send_sems, recv_sems, bufs))

    def halves_land(after):
        for tag, names, send_sems, recv_sems, bufs in halves:
            n = len(names)
            bufs = copies_wait(send_sems, recv_sems, bufs, _to_sibling_copies(n, None), after, f"grads_halves_wait_{tag}")
            for key, o, r in zip(names, bufs[:n], bufs[n:]):
                own_of[key], recv_of[key] = o, r
        halves.clear()

    def emit(key, grads_of, after):
        kind, layer = key
        landed = chips_wait([after]) if to_chips else None
        begun = {name for _, part_names, _, _, _ in to_sibling for name, _ in part_names}
        rest = {name: g for name, g in grads_of.items() if name not in begun}
        if rest:
            sibling_start(f"{kind}{layer}", [(name, layer if kind != "mix" else 0) for name in rest],
                          [_from_matmul_layout(name, g) for name, g in rest.items()], after)
        token = chips_start(f"{kind}{layer}", [after])
        if landed:
            reduce_landed(*landed)
        return token

    first_gate, small_all = finish("0a", pending["0c"][2][0])
    n_norm_rows = norms.shape[0] * norms.shape[1]
    norms_all = jnp.concatenate([small_all[j, :n_norm_rows].reshape(norms.shape) for j in range(N_CHIPS)], axis=-1)
    pool_scale_all = jnp.concatenate([small_all[j, n_norm_rows:n_norm_rows + 1] for j in range(N_CHIPS)], axis=-1)
    small = dict(attn_sinks=attn_sinks[0], sgu_ln_g=sgu_ln_g, sgu_ln_b=sgu_ln_b, sgu_w=sgu_w[0], sgu_b=sgu_b[0], pool_scale=pool_scale_all)

    loss_part, dx, g_small, g_norms, g_mem_norm, last_token = device_step(
        x[0], mem[0], loss_target[0], norms_all, mem_norm, small, fetch, emit, early)
    loss = lax.psum(0.5 * jnp.sum(loss_part) / x.shape[-1], ("x", "y", "c"))
    last_names = {name for name, _ in to_chips[0][1]}
    order = last_token.reshape(1, 1)
    halves_land([dx])

    grads, delta, new_m, new_v = {}, {}, {}, {}

    def update(name):
        n_l = _as3(w[name]).shape[0]
        res = adamw_from_halves(_as3(w[name]), [own_of[name, l] for l in range(n_l)], [recv_of[name, l] for l in range(n_l)],
                                _as3(mom[name]), _as3(var[name]), core, order, f"adamw_{name}")
        grads[name], delta[name], new_m[name], new_v[name] = (t.reshape(w[name].shape) for t in res)

    for name in BIG_NAMES:
        if name not in last_names:
            update(name)
    done = [delta[name] for name in BIG_NAMES if name not in last_names]
    reduce_landed(*chips_wait(done))
    halves_land(done)
    for name in BIG_NAMES:
        if name in last_names:
            update(name)

    small_g = [g_norms, g_mem_norm, g_small["attn_sinks"], g_small["sgu_ln_g"], g_small["sgu_ln_b"], g_small["sgu_w"], g_small["sgu_b"],
               g_small["pool_scale"]]
    packed = _pack(small_g, 16)
    summed = sum_slots(gather_all_devices(packed, "small_grads_all"), "small_grads_sum")
    s_norms, s_mem, s_sinks, s_lg, s_lb, s_w, s_b, s_scale = _unpack(summed, small_g)
    grads["norms"] = lax.dynamic_slice_in_dim(s_norms, chip_id * n_shard, n_shard, axis=2)
    grads["pool_scale"] = lax.dynamic_slice_in_dim(s_scale, chip_id * n_shard, n_shard, axis=1)
    grads.update(mem_norm=s_mem, attn_sinks=s_sinks, sgu_ln_g=s_lg, sgu_ln_b=s_lb, sgu_w=s_w, sgu_b=s_b)
    like = [w[n] for n in SMALL_NAMES]
    packs = [_pack([src[n] for n in SMALL_NAMES], 16)[None] for src in (w, grads, mom, var)]
    for dst, t in zip((delta, new_m, new_v), adamw(*packs, "adamw_small")):
        for n, a in zip(SMALL_NAMES, _unpack(t[0], like)):
            dst[n] = a

    outs = [loss, dx[None]]
    for group in (grads, delta, new_m, new_v):
        outs += [group[n] for n in WEIGHT_ORDER]
    return tuple(outs)
```

```python
import functools

import jax
import jax.numpy as jnp
from jax import lax
from jax.experimental import pallas as pl
from jax.experimental.pallas import tpu as pltpu

F32 = jnp.float32
BF16 = jnp.bfloat16
MESH = pl.DeviceIdType.MESH

EPS = 1e-6
ROPE_THETA = 500000.0
ROPE_HALF = 8
HEAD_DIM = 64
N_Q_HEADS = 16
N_KV_HEADS = 2
Q_PER_KV = 8
BLOCK = 128
ATTN_WIDTH = 1024
KV_WIDTH = 128
QK_WIDTH = ATTN_WIDTH + KV_WIDTH
SGU_WIDTH = 1024
SGU_GROUPS = 8
POOL_WINDOWS = (2, 4, 8, 16)
POOL_HALO = 16
X_HEADS = 4
X_HEAD_DIM = 128
N_CHIPS = 4
N_DEV = 8

ADAM_LR = 0.001
ADAM_B1 = 0.9
ADAM_B2 = 0.999
ADAM_EPS = 1e-08
ADAM_WD = 0.01
ADAM_STEP = 10

VMEM_LIMIT_V7X = 52 * 1024 * 1024
MM_VMEM_BUDGET = 44 * 1024 * 1024
LANES = 128
ROW_TILE_BYTES = 10 * 1024 * 1024
WEIGHT_TILE_BYTES = 12 * 1024 * 1024
MXU_FLOPS_V7X = 1.0e15
HBM_BYTES_PER_S_V7X = 3.0e12
VMEM_STORE_BYTES_PER_S = 4.0e12
MXU_WEIGHT_LOAD_ROWS = 192
MXU_NARROW_COLS = 128
GRID_STEP_S = 0.35e-6


def _params(sem):
    return pltpu.CompilerParams(dimension_semantics=sem, vmem_limit_bytes=VMEM_LIMIT_V7X)


def _pick(dim, pref, align):
    cands = [t for t in range(align, dim + 1, align) if dim % t == 0]
    small = [t for t in cands if t <= pref]
    if small and small[-1] * 2 >= min(pref, dim):
        return small[-1]
    return dim


def _rows(n_rows, bytes_per_row, tile_bytes=ROW_TILE_BYTES):
    want = max(16, min(1024, tile_bytes // max(1, bytes_per_row)))
    cands = [t for t in range(16, n_rows + 1, 16) if n_rows % t == 0 and t <= want]
    return cands[-1] if cands else n_rows


def _divisors(dim, align, most):
    return [t for t in range(align, min(dim, most) + 1, align) if dim % t == 0] or [dim]


def _mm_tiles(M, N, K, J, m_align, k_align, a_bytes, b_bytes, o_bytes, reduce, ta, products=1):
    best = None
    for tm in _divisors(M, m_align, 2048):
        for tn in _divisors(N, LANES, 2048):
            for tk in _divisors(K, k_align, 4096):
                split = K // tk > 1 or reduce
                vmem = 2 * (tm * tk * a_bytes + tk * tn * b_bytes + tm * tn * o_bytes) + tm * tn * 4 * products * (2 if split else 1)
                if ta:
                    vmem += tm * tk * a_bytes
                if vmem > MM_VMEM_BUDGET:
                    continue
                steps = J * (M // tm) * (N // tn) * (K // tk)
                mxu = 2.0 * J * M * N * K / MXU_FLOPS_V7X * (tm + MXU_WEIGHT_LOAD_ROWS) / tm * (tn + MXU_NARROW_COLS) / tn
                acc = J * M * N * (K // tk) * 8 / VMEM_STORE_BYTES_PER_S if split else 0.0
                hbm = J * (M * K * a_bytes * (N // tn) + K * N * b_bytes * (M // tm) + M * N * o_bytes) / HBM_BYTES_PER_S_V7X
                cost = max(mxu + 0.5 * acc, hbm) + steps * GRID_STEP_S
                if best is None or cost < best[0]:
                    best = (cost, tm, tn, tk)
    return best[1:]


def _rsum8(v):
    r, c = v.shape
    return v.reshape(r // 8, 8, c).sum(axis=0)


def _mm(a, b, *, name, ta=False, tb=False, batch="none", groups=0, a_cb=False, b_cb=False, o_cb=False,
        out_dtype=F32, more=(), extras=(), epilogue=None, out_dtypes=None, order=None):
    J = groups or (a.shape[0] if a.ndim == 3 else (b.shape[0] if b.ndim == 3 else 1))
    a2, b2 = a.shape[-2:], b.shape[-2:]
    M, K = (a2[1], a2[0]) if ta else a2
    N, Kb = b2 if tb else (b2[1], b2[0])
    if a_cb:
        if ta:
            M //= J
        else:
            K //= J
    if b_cb:
        if tb:
            Kb //= J
        else:
            N //= J
    assert K == Kb, (name, a.shape, b.shape)
    reduce = batch == "reduce"
    out_dtypes = list(out_dtypes or [out_dtype])
    n_terms = 1 + len(more)
    o_bytes = sum(jnp.dtype(d).itemsize for d in out_dtypes) + sum(e.dtype.itemsize for e in extras)
    n_prod = n_terms if epilogue is not None else 1
    tm, tn, tk = _mm_tiles(M, N, K, J, LANES if ta else 16, LANES if (not ta or tb) else 16, a.dtype.itemsize * n_terms,
                           b.dtype.itemsize * n_terms, o_bytes, reduce, ta, n_prod)
    nm, nn, nk = M // tm, N // tn, K // tk
    if reduce:
        grid = (nm, nn, J, nk)
        unpack = lambda m, n, j, k: (j, m, n, k)
        sem = ("parallel", "parallel", "arbitrary", "arbitrary")
    else:
        grid = (J, nm, nn, nk)
        unpack = lambda j, m, n, k: (j, m, n, k)
        sem = ("parallel", "parallel", "parallel", "arbitrary")

    def a_map(*g):
        j, m, n, k = unpack(*g)
        r, c = (k, m) if ta else (m, k)
        if a_cb:
            c = c + j * (nm if ta else nk)
        return (j, r, c) if a.ndim == 3 else (r, c)

    def b_map(*g):
        j, m, n, k = unpack(*g)
        r, c = (n, k) if tb else (k, n)
        if b_cb:
            c = c + j * (nk if tb else nn)
        return (j, r, c) if b.ndim == 3 else (r, c)

    def o_map(*g):
        j, m, n, k = unpack(*g)
        if o_cb:
            return (m, n + j * nn)
        return (j, m, n) if batch == "map" else (m, n)

    a_blk = (tk, tm) if ta else (tm, tk)
    b_blk = (tn, tk) if tb else (tk, tn)
    a_spec = pl.BlockSpec(((None,) + a_blk) if a.ndim == 3 else a_blk, a_map)
    b_spec = pl.BlockSpec(((None,) + b_blk) if b.ndim == 3 else b_blk, b_map)
    if o_cb:
        out_shape, o_blk = (M, N * J), (tm, tn)
    elif batch == "map":
        out_shape, o_blk = (J, M, N), (None, tm, tn)
    else:
        out_shape, o_blk = (M, N), (tm, tn)
    o_spec = pl.BlockSpec(o_blk, o_map)
    dims = (((0 if ta else 1,), (1 if tb else 0,)), ((), ()))
    red_axes = (2, 3) if reduce else (3,)
    split = reduce or nk > 1
    n_ex, n_out = len(extras), len(out_dtypes)
    n_ord = 0 if order is None else 1

    def body(*refs):
        refs = refs[n_ord:]
        ab_refs, ex_refs = refs[:2 * n_terms], refs[2 * n_terms:2 * n_terms + n_ex]
        o_refs, acc = refs[2 * n_terms + n_ex:2 * n_terms + n_ex + n_out], refs[2 * n_terms + n_ex + n_out:]
        prods = [lax.dot_general(ab_refs[2 * t][...], ab_refs[2 * t + 1][...], dims, preferred_element_type=F32) for t in range(n_terms)]
        if epilogue is None:
            prods = [functools.reduce(lambda p, q: p + q, prods)]

        def finish(vals):
            outs = epilogue(vals, [e[...] for e in ex_refs]) if epilogue is not None else vals
            for o_ref, val in zip(o_refs, outs):
                o_ref[...] = val.astype(o_ref.dtype)

        if not split:
            finish(prods)
            return
        first = functools.reduce(jnp.logical_and, [pl.program_id(ax) == 0 for ax in red_axes])
        last = functools.reduce(jnp.logical_and, [pl.program_id(ax) == grid[ax] - 1 for ax in red_axes])

        @pl.when(first)
        def _():
            for acc_ref, prod in zip(acc, prods):
                acc_ref[...] = prod

        @pl.when(jnp.logical_not(first))
        def _():
            for acc_ref, prod in zip(acc, prods):
                acc_ref[...] += prod

        @pl.when(last)
        def _():
            finish([acc_ref[...] for acc_ref in acc])

    operands = [order] * n_ord + [a, b] + [t for pair in more for t in pair] + list(extras)
    res = pl.pallas_call(
        body, name=name, grid=grid, out_specs=[o_spec] * n_out,
        in_specs=[pl.BlockSpec(memory_space=pl.ANY)] * n_ord + [a_spec, b_spec] * n_terms + [o_spec] * n_ex,
        out_shape=[jax.ShapeDtypeStruct(out_shape, d) for d in out_dtypes],
        scratch_shapes=[pltpu.VMEM((tm, tn), F32)] * (n_prod if split else 0), compiler_params=_params(sem),
    )(*operands)
    return res if epilogue is not None else res[0]


def _rowwise(fn, tiled, whole, outs, accs, *, name):
    n_rows = tiled[0].shape[0]
    row_bytes = sum(t.shape[1] * t.dtype.itemsize for t in tiled) + sum(c * jnp.dtype(d).itemsize for c, d in outs)
    tr = _rows(n_rows, row_bytes)
    n_t, n_w, n_o = len(tiled), len(whole), len(outs)

    def body(*refs):
        i = pl.program_id(0)
        t_refs, w_refs = refs[:n_t], refs[n_t:n_t + n_w]
        o_refs, a_refs = refs[n_t + n_w:n_t + n_w + n_o], refs[n_t + n_w + n_o:]
        o_vals, a_vals = fn(i, *[r[...] for r in t_refs], *[r[...] for r in w_refs])
        for r, v in zip(o_refs, o_vals):
            r[...] = v.astype(r.dtype)
        if a_refs:
            @pl.when(i == 0)
            def _():
                for r in a_refs:
                    r[...] = jnp.zeros_like(r)
            for r, v in zip(a_refs, a_vals):
                r[...] += v

    in_specs = [pl.BlockSpec((tr, t.shape[1]), lambda i: (i, 0)) for t in tiled]
    in_specs += [pl.BlockSpec(w.shape, lambda i, nd=w.ndim: (0,) * nd) for w in whole]
    out_specs = [pl.BlockSpec((tr, c), lambda i: (i, 0)) for c, _ in outs]
    out_specs += [pl.BlockSpec(s, lambda i, nd=len(s): (0,) * nd) for s, _ in accs]
    out_shape = [jax.ShapeDtypeStruct((n_rows, c), d) for c, d in outs]
    out_shape += [jax.ShapeDtypeStruct(s, d) for s, d in accs]
    res = pl.pallas_call(
        body, name=name, grid=(n_rows // tr,), in_specs=in_specs, out_specs=out_specs, out_shape=out_shape,
        compiler_params=_params(("arbitrary",) if accs else ("parallel",)),
    )(*tiled, *whole)
    return res


def _rms_stats(x):
    x = x.astype(F32)
    r = lax.rsqrt(jnp.mean(x * x, axis=-1, keepdims=True) + EPS)
    return x * r, r


def rms_fwd(x, g, out_dtype, name):
    def fn(i, x, g):
        xhat, _ = _rms_stats(x)
        return [xhat * g], []
    return _rowwise(fn, [x], [g], [(x.shape[1], out_dtype)], [], name=name)[0]


def postnorm_res(x, y, g, s, name):
    def fn(i, x, y, g):
        yhat, _ = _rms_stats(y)
        return [x + s * (yhat * g)], []
    return _rowwise(fn, [x, y], [g], [(x.shape[1], F32)], [], name=name)[0]


def close_block(x, y, g, s, nxt, tag):
    if nxt is None:
        return postnorm_res(x, y, g, s, f"{tag}_postnorm"), None
    g_next, next_dtype = nxt(y)

    def fn(i, x, y, g, g_next):
        yhat, _ = _rms_stats(y)
        x2 = x + s * (yhat * g)
        return [x2, _rms_stats(x2)[0] * g_next], []
    return _rowwise(fn, [x, y], [g, g_next], [(x.shape[1], F32), (x.shape[1], next_dtype)], [], name=f"{tag}_postnorm_next")


def rms_bwd(xin, g, douts, scale, add, out_dtype, name):
    n_d = len(douts)

    def fn(i, x, *rest):
        ds, rest = rest[:n_d], rest[n_d:]
        ad = rest[0] if add is not None else None
        g = rest[-1]
        xhat, r = _rms_stats(x)
        d = ds[0].astype(F32)
        for e in ds[1:]:
            d = d + e.astype(F32)
        if scale != 1.0:
            d = d * scale
        dg = _rsum8(d * xhat)
        dxhat = d * g
        dx = r * (dxhat - xhat * jnp.mean(dxhat * xhat, axis=-1, keepdims=True))
        if ad is not None:
            dx = dx + ad
        return [dx], [dg]

    tiled = [xin, *douts] + ([add] if add is not None else [])
    dx, dg = _rowwise(fn, tiled, [g], [(xin.shape[1], out_dtype)], [((8, xin.shape[1]), F32)], name=name)
    return dx, dg


def _silu_parts(g):
    sg = 1.0 / (1.0 + jnp.exp(-g))
    return g * sg, sg


def _swiglu_tiles(products, saved):
    u, = products
    return [u, _silu_parts(saved[0].astype(F32))[0] * u]


def _swiglu_bwd_tiles(products, saved):
    da, = products
    g, u = saved[0].astype(F32), saved[1].astype(F32)
    s, sg = _silu_parts(g)
    return [da * u * (sg * (1.0 + g * (1.0 - sg))), da * s]


def scale_cols(y, s, name):
    def fn(i, y, s):
        return [y * s], []
    return _rowwise(fn, [y], [s], [(y.shape[1], F32)], [], name=name)[0]


def scale_cols_bwd(dm, y, s, name):
    def fn(i, dm, y, s):
        return [dm * s], [_rsum8(dm * y)]
    return _rowwise(fn, [dm, y], [s], [(y.shape[1], BF16)], [((8, y.shape[1]), F32)], name=name)


def loss_and_grad(y, target, name):
    n_feat = y.shape[1]

    def fn(i, y, t):
        e = y - t
        return [e * (1.0 / n_feat)], [_rsum8(e * e)]
    dy, part = _rowwise(fn, [y, target], [], [(n_feat, F32)], [((8, n_feat), F32)], name=name)
    return part, dy


def cast_into_slot(w3, layer, chip, name):
    _, n_rows, n_cols = w3.shape
    tr = _rows(n_rows, n_cols * 6, WEIGHT_TILE_BYTES)

    def body(chip_ref, w_ref, o_ref):
        o_ref[...] = w_ref[...].astype(BF16)

    grid_spec = pltpu.PrefetchScalarGridSpec(
        num_scalar_prefetch=1, grid=(n_rows // tr,),
        in_specs=[pl.BlockSpec((None, tr, n_cols), lambda i, chip_ref: (layer, i, 0))],
        out_specs=pl.BlockSpec((None, tr, n_cols), lambda i, chip_ref: (chip_ref[0], i, 0)))
    return pl.pallas_call(
        body, name=name, grid_spec=grid_spec, out_shape=jax.ShapeDtypeStruct((N_CHIPS, n_rows, n_cols), BF16),
        compiler_params=_params(("parallel",)),
    )(chip, w3)


def _adam_update(w, g, m, v):
    c1 = 1.0 / (1.0 - ADAM_B1 ** ADAM_STEP)
    c2 = 1.0 / (1.0 - ADAM_B2 ** ADAM_STEP)
    m = ADAM_B1 * m + (1.0 - ADAM_B1) * g
    v = ADAM_B2 * v + (1.0 - ADAM_B2) * (g * g)
    return -ADAM_LR * ((m * c1) / (jnp.sqrt(v * c2) + ADAM_EPS) + ADAM_WD * w), m, v


def adamw(w, g, m, v, name):
    n_l, n_rows, n_cols = w.shape
    tr = _rows(n_rows, n_cols * 4 * 7)

    def body(w_ref, g_ref, m_ref, v_ref, d_ref, mo_ref, vo_ref):
        d_ref[...], mo_ref[...], vo_ref[...] = _adam_update(w_ref[...], g_ref[...], m_ref[...], v_ref[...])

    spec = pl.BlockSpec((None, tr, n_cols), lambda l, i: (l, i, 0))
    shp = jax.ShapeDtypeStruct(w.shape, F32)
    return pl.pallas_call(
        body, name=name, grid=(n_l, n_rows // tr), in_specs=[spec] * 4, out_specs=[spec] * 3, out_shape=[shp] * 3,
        compiler_params=_params(("parallel", "parallel")),
    )(w, g, m, v)


def adamw_from_halves(w, own, recv, m, v, core, order, name):
    n_l, n_rows, n_cols = w.shape
    half = n_rows // 2
    tr = _rows(half, n_cols * 4 * 9, WEIGHT_TILE_BYTES)
    per = half // tr

    def body(core_ref, *refs):
        w_ref, m_ref, v_ref, order_ref = refs[:4]
        own_refs, recv_refs = refs[4:4 + n_l], refs[4 + n_l:4 + 2 * n_l]
        g_ref, d_ref, mo_ref, vo_ref = refs[4 + 2 * n_l:]
        l, h = pl.program_id(0), pl.program_id(1)
        mine = h == core_ref[0]
        g = jnp.where(mine, own_refs[0][...], recv_refs[0][...])
        for k in range(1, n_l):
            g = jnp.where(l == k, jnp.where(mine, own_refs[k][...], recv_refs[k][...]), g)
        g = g + order_ref[...]
        g_ref[...] = g
        d_ref[...], mo_ref[...], vo_ref[...] = _adam_update(w_ref[...], g, m_ref[...], v_ref[...])

    full = pl.BlockSpec((None, tr, n_cols), lambda l, h, i, core_ref: (l, h * per + i, 0))

    def piece(layer, is_own):
        def index(l, h, i, core_ref):
            used = (l == layer) & ((h == core_ref[0]) == is_own)
            return (jnp.where(used, i, 0), 0)
        return pl.BlockSpec((tr, n_cols), index)

    grid_spec = pltpu.PrefetchScalarGridSpec(
        num_scalar_prefetch=1, grid=(n_l, 2, per),
        in_specs=[full] * 3 + [pl.BlockSpec((1, 1), lambda l, h, i, core_ref: (0, 0))]
        + [piece(k, True) for k in range(n_l)] + [piece(k, False) for k in range(n_l)],
        out_specs=[full] * 4)
    return pl.pallas_call(
        body, name=name, grid_spec=grid_spec, out_shape=[jax.ShapeDtypeStruct(w.shape, F32)] * 4,
        compiler_params=_params(("parallel", "parallel", "parallel")),
    )(core, w, m, v, order, *own, *recv)


def rope_tables(n_tok):
    inv = ROPE_THETA ** (-jnp.arange(ROPE_HALF, dtype=F32) * 2.0 / (2 * ROPE_HALF))
    ang = jnp.arange(n_tok, dtype=F32)[:, None] * inv[None, :]
    cos, sin = jnp.cos(ang), jnp.sin(ang)
    rest = HEAD_DIM - 2 * ROPE_HALF
    one, zero, z8 = jnp.ones((n_tok, rest), F32), jnp.zeros((n_tok, rest), F32), jnp.zeros((n_tok, ROPE_HALF), F32)
    c = jnp.concatenate([cos, cos, one], axis=1)
    s1 = jnp.concatenate([-sin, z8, zero], axis=1)
    s2 = jnp.concatenate([z8, sin, zero], axis=1)
    two = lambda t: jnp.concatenate([t, t], axis=1)
    return two(c), two(s1), two(s2)


def rope_apply(x, tabs, n_col_blocks, inverse, out_dtype, name):
    n_tok = x.shape[0]
    tr = _rows(n_tok, LANES * 4 * 6)

    def body(x_ref, c_ref, s1_ref, s2_ref, o_ref):
        x = x_ref[...].astype(F32)
        if inverse:
            out = x * c_ref[...] + pltpu.roll(x * s1_ref[...], ROPE_HALF, 1) + pltpu.roll(x * s2_ref[...], LANES - ROPE_HALF, 1)
        else:
            out = x * c_ref[...] + pltpu.roll(x, LANES - ROPE_HALF, 1) * s1_ref[...] + pltpu.roll(x, ROPE_HALF, 1) * s2_ref[...]
        o_ref[...] = out.astype(o_ref.dtype)

    tab_spec = pl.BlockSpec((tr, LANES), lambda i, c: (i, 0))
    blk = pl.BlockSpec((tr, LANES), lambda i, c: (i, c))
    return pl.pallas_call(
        body, name=name, grid=(n_tok // tr, n_col_blocks), in_specs=[blk, tab_spec, tab_spec, tab_spec], out_specs=blk,
        out_shape=jax.ShapeDtypeStruct((n_tok, n_col_blocks * LANES), out_dtype), compiler_params=_params(("parallel", "parallel")),
    )(x, *tabs)


def _swa_masks():
    rows = Q_PER_KV * BLOCK
    qi = lax.broadcasted_iota(jnp.int32, (rows, 2 * BLOCK), 0) & (BLOCK - 1)
    kj = lax.broadcasted_iota(jnp.int32, (rows, 2 * BLOCK), 1)
    rel = qi + BLOCK - kj
    band = (rel >= 0) & (rel < BLOCK)
    return jnp.where(jnp.stack([band & (kj >= BLOCK), band]), 0.0, -1e30).astype(F32)


def _swa_probs(q, k, sink, mask):
    s = lax.dot_general(q, k, (((1,), (1,)), ((), ())), preferred_element_type=F32) * (HEAD_DIM ** -0.5) + mask
    m = jnp.maximum(jnp.max(s, axis=-1, keepdims=True), sink)
    e = jnp.exp(s - m)
    es = jnp.exp(sink - m)
    inv = 1.0 / (jnp.sum(e, axis=-1, keepdims=True) + es)
    return e * inv, es * inv


def _swa_specs(n_blocks):
    q_spec = pl.BlockSpec((Q_PER_KV, BLOCK, HEAD_DIM), lambda h, n: (h, n, 0))
    prev = pl.BlockSpec((None, BLOCK, HEAD_DIM), lambda h, n: (h, jnp.maximum(n - 1, 0), 0))
    cur = pl.BlockSpec((None, BLOCK, HEAD_DIM), lambda h, n: (h, n, 0))
    sink = pl.BlockSpec((None, Q_PER_KV * BLOCK, 1), lambda h, n: (h, 0, 0))
    masks = pl.BlockSpec((2, Q_PER_KV * BLOCK, 2 * BLOCK), lambda h, n: (0, 0, 0))
    return q_spec, prev, cur, sink, masks


def swa_fwd(q, k, v, sink_rows, name):
    n_tok = q.shape[1]
    q_spec, prev, cur, sink, masks = _swa_specs(n_tok // BLOCK)

    def body(q_ref, kp_ref, kc_ref, vp_ref, vc_ref, s_ref, m_ref, o_ref):
        mask = m_ref[jnp.minimum(pl.program_id(1), 1)]
        qq = q_ref[...].reshape(Q_PER_KV * BLOCK, HEAD_DIM)
        kk = jnp.concatenate([kp_ref[...], kc_ref[...]], axis=0)
        vv = jnp.concatenate([vp_ref[...], vc_ref[...]], axis=0)
        p, _ = _swa_probs(qq, kk, s_ref[...], mask)
        o = jnp.dot(p.astype(BF16), vv, preferred_element_type=F32)
        o_ref[...] = o.reshape(Q_PER_KV, BLOCK, HEAD_DIM).astype(o_ref.dtype)

    return pl.pallas_call(
        body, name=name, grid=(N_KV_HEADS, n_tok // BLOCK), in_specs=[q_spec, prev, cur, prev, cur, sink, masks], out_specs=q_spec,
        out_shape=jax.ShapeDtypeStruct(q.shape, BF16), compiler_params=_params(("parallel", "parallel")),
    )(q, k, k, v, v, sink_rows, _swa_masks())


def swa_bwd(q, k, v, sink_rows, do, name):
    n_tok = q.shape[1]
    nb = n_tok // BLOCK
    q_spec, prev, cur, sink, masks = _swa_specs(nb)
    rows = Q_PER_KV * BLOCK

    def body(q_ref, kp_ref, kc_ref, vp_ref, vc_ref, s_ref, m_ref, do_ref, dq_ref, dkp_ref, dkc_ref, dvp_ref, dvc_ref, ds_ref):
        mask = m_ref[jnp.minimum(pl.program_id(1), 1)]
        qq = q_ref[...].reshape(rows, HEAD_DIM)
        dd = do_ref[...].reshape(rows, HEAD_DIM)
        kk = jnp.concatenate([kp_ref[...], kc_ref[...]], axis=0)
        vv = jnp.concatenate([vp_ref[...], vc_ref[...]], axis=0)
        p, ps = _swa_probs(qq, kk, s_ref[...], mask)
        dp = lax.dot_general(dd, vv, (((1,), (1,)), ((), ())), preferred_element_type=F32)
        delta = jnp.sum(p * dp, axis=-1, keepdims=True)
        ds = (p * (dp - delta) * (HEAD_DIM ** -0.5)).astype(BF16)
        dq = jnp.dot(ds, kk, preferred_element_type=F32)
        dk = lax.dot_general(ds, qq, (((0,), (0,)), ((), ())), preferred_element_type=F32)
        dv = lax.dot_general(p.astype(BF16), dd, (((0,), (0,)), ((), ())), preferred_element_type=F32)
        dq_ref[...] = dq.reshape(Q_PER_KV, BLOCK, HEAD_DIM).astype(dq_ref.dtype)
        dkp_ref[...] = dk[:BLOCK]
        dkc_ref[...] = dk[BLOCK:]
        dvp_ref[...] = dv[:BLOCK]
        dvc_ref[...] = dv[BLOCK:]
        dsink = jnp.broadcast_to(-ps * delta, (rows, LANES)).reshape(Q_PER_KV, BLOCK, LANES)
        ds_ref[...] = jnp.sum(dsink, axis=1)

    part = pl.BlockSpec((None, None, BLOCK, HEAD_DIM), lambda h, n: (h, n, 0, 0))
    part_shape = jax.ShapeDtypeStruct((N_KV_HEADS, nb, BLOCK, HEAD_DIM), F32)
    return pl.pallas_call(
        body, name=name, grid=(N_KV_HEADS, nb), in_specs=[q_spec, prev, cur, prev, cur, sink, masks, q_spec],
        out_specs=[q_spec, part, part, part, part, pl.BlockSpec((None, None, Q_PER_KV, LANES), lambda h, n: (h, n, 0, 0))],
        out_shape=[jax.ShapeDtypeStruct(q.shape, BF16), part_shape, part_shape, part_shape, part_shape,
                   jax.ShapeDtypeStruct((N_KV_HEADS, nb, Q_PER_KV, LANES), F32)],
        compiler_params=_params(("parallel", "parallel")),
    )(q, k, k, v, v, sink_rows, _swa_masks(), do)


def _to_heads(t, n_heads):
    return t.reshape(t.shape[0], n_heads, HEAD_DIM).transpose(1, 0, 2)


def _from_heads(t):
    return t.transpose(1, 0, 2).reshape(t.shape[1], -1)


def _fold_kv_grad(prev, cur):
    shifted = jnp.concatenate([prev[:, 1:], jnp.zeros_like(prev[:, :1])], axis=1)
    tot = (cur + shifted).reshape(N_KV_HEADS, -1, HEAD_DIM)
    return _from_heads(tot)


def _x_probs(qh, kh):
    s = lax.dot_general(qh, kh, (((1,), (1,)), ((), ())), preferred_element_type=F32) * (X_HEAD_DIM ** -0.5)
    e = jnp.exp(s - jnp.max(s, axis=-1, keepdims=True))
    return e * (1.0 / jnp.sum(e, axis=-1, keepdims=True))


def xattn_fwd(q, k, v, name):
    n_tok, width = q.shape
    n_mem = k.shape[0]
    tq = _pick(n_tok, 512, 16)

    def body(q_ref, k_ref, v_ref, o_ref):
        for h in range(X_HEADS):
            cols = slice(h * X_HEAD_DIM, (h + 1) * X_HEAD_DIM)
            p = _x_probs(q_ref[:, cols], k_ref[:, cols])
            o_ref[:, cols] = jnp.dot(p.astype(BF16), v_ref[:, cols], preferred_element_type=F32).astype(o_ref.dtype)

    row = pl.BlockSpec((tq, width), lambda i: (i, 0))
    mem = pl.BlockSpec((n_mem, width), lambda i: (0, 0))
    return pl.pallas_call(
        body, name=name, grid=(n_tok // tq,), in_specs=[row, mem, mem], out_specs=row,
        out_shape=jax.ShapeDtypeStruct(q.shape, BF16), compiler_params=_params(("parallel",)),
    )(q, k, v)


def xattn_bwd(q, k, v, do, name):
    n_tok, width = q.shape
    n_mem = k.shape[0]
    tq = _pick(n_tok, 512, 16)

    def body(q_ref, k_ref, v_ref, do_ref, dq_ref, dk_ref, dv_ref):
        @pl.when(pl.program_id(0) == 0)
        def _():
            dk_ref[...] = jnp.zeros_like(dk_ref)
            dv_ref[...] = jnp.zeros_like(dv_ref)

        for h in range(X_HEADS):
            cols = slice(h * X_HEAD_DIM, (h + 1) * X_HEAD_DIM)
            qh, kh, vh, dh = q_ref[:, cols], k_ref[:, cols], v_ref[:, cols], do_ref[:, cols]
            p = _x_probs(qh, kh)
            dp = lax.dot_general(dh, vh, (((1,), (1,)), ((), ())), preferred_element_type=F32)
            delta = jnp.sum(p * dp, axis=-1, keepdims=True)
            ds = (p * (dp - delta) * (X_HEAD_DIM ** -0.5)).astype(BF16)
            dq_ref[:, cols] = jnp.dot(ds, kh, preferred_element_type=F32).astype(dq_ref.dtype)
            dk_ref[:, cols] += lax.dot_general(ds, qh, (((0,), (0,)), ((), ())), preferred_element_type=F32)
            dv_ref[:, cols] += lax.dot_general(p.astype(BF16), dh, (((0,), (0,)), ((), ())), preferred_element_type=F32)

    row = pl.BlockSpec((tq, width), lambda i: (i, 0))
    mem = pl.BlockSpec((n_mem, width), lambda i: (0, 0))
    return pl.pallas_call(
        body, name=name, grid=(n_tok // tq,), in_specs=[row, mem, mem, row], out_specs=[row, mem, mem],
        out_shape=[jax.ShapeDtypeStruct(q.shape, BF16), jax.ShapeDtypeStruct(k.shape, F32), jax.ShapeDtypeStruct(k.shape, F32)],
        compiler_params=_params(("arbitrary",)),
    )(q, k, v, do)


GELU_C = 0.7978845608028654
GELU_A = 0.044715


def _gelu_parts(x):
    x2 = x * x
    t = jnp.tanh(GELU_C * x * (1.0 + GELU_A * x2))
    y = 0.5 * x * (1.0 + t)
    dy = 0.5 * (1.0 + t) + 0.5 * x * (1.0 - t * t) * GELU_C * (1.0 + 3.0 * GELU_A * x2)
    return y, dy


def _sgu_norm(v, ln_g, ln_b):
    mu = jnp.mean(v, axis=-1, keepdims=True)
    vc = v - mu
    r = lax.rsqrt(jnp.mean(vc * vc, axis=-1, keepdims=True) + EPS)
    xhat = vc * r
    return xhat * ln_g + ln_b, xhat, r


def _causal_weights(w_ref):
    i = lax.broadcasted_iota(jnp.int32, (BLOCK, BLOCK), 0)
    j = lax.broadcasted_iota(jnp.int32, (BLOCK, BLOCK), 1)
    return [jnp.where(i >= j, w_ref[g], 0.0).astype(BF16) for g in range(SGU_GROUPS)]


def sgu_fwd(u_pre, v_pre, ln_g, ln_b, w_s, bias_rows, name):
    n_tok = u_pre.shape[0]
    tm = _pick(n_tok, 512, BLOCK)

    def body(u_ref, v_ref, g_ref, b_ref, w_ref, bb_ref, o_ref):
        vn, _, _ = _sgu_norm(_gelu_parts(v_ref[...])[0], g_ref[...], b_ref[...])
        vn = vn.astype(BF16)
        wc = _causal_weights(w_ref)
        for c in range(tm // BLOCK):
            rows = slice(c * BLOCK, (c + 1) * BLOCK)
            for g in range(SGU_GROUPS):
                cols = slice(g * LANES, (g + 1) * LANES)
                mixed = jnp.dot(wc[g], vn[rows, cols], preferred_element_type=F32) + bb_ref[g]
                u = _gelu_parts(u_ref[rows, cols])[0]
                o_ref[rows, cols] = (u * mixed).astype(o_ref.dtype)

    row = pl.BlockSpec((tm, SGU_WIDTH), lambda i: (i, 0))
    vec = pl.BlockSpec((1, SGU_WIDTH), lambda i: (0, 0))
    mat = pl.BlockSpec((SGU_GROUPS, BLOCK, LANES), lambda i: (0, 0, 0))
    return pl.pallas_call(
        body, name=name, grid=(n_tok // tm,), in_specs=[row, row, vec, vec, mat, mat], out_specs=row,
        out_shape=jax.ShapeDtypeStruct((n_tok, SGU_WIDTH), BF16), compiler_params=_params(("parallel",)),
    )(u_pre, v_pre, ln_g, ln_b, w_s, bias_rows)


def sgu_bwd(u_pre, v_pre, ln_g, ln_b, w_s, bias_rows, dgate, name):
    n_tok = u_pre.shape[0]
    tm = _pick(n_tok, 512, BLOCK)

    def body(u_ref, v_ref, g_ref, b_ref, w_ref, bb_ref, dg_ref, du_ref, dv_ref, dw_ref, db_ref, dlg_ref, dlb_ref, dvn_ref):
        @pl.when(pl.program_id(0) == 0)
        def _():
            dw_ref[...] = jnp.zeros_like(dw_ref)
            db_ref[...] = jnp.zeros_like(db_ref)
            dlg_ref[...] = jnp.zeros_like(dlg_ref)
            dlb_ref[...] = jnp.zeros_like(dlb_ref)

        gv, dgv = _gelu_parts(v_ref[...])
        vn, xhat, r = _sgu_norm(gv, g_ref[...], b_ref[...])
        vn = vn.astype(BF16)
        wc = _causal_weights(w_ref)
        for c in range(tm // BLOCK):
            rows = slice(c * BLOCK, (c + 1) * BLOCK)
            for g in range(SGU_GROUPS):
                cols = slice(g * LANES, (g + 1) * LANES)
                vt = vn[rows, cols]
                mixed = jnp.dot(wc[g], vt, preferred_element_type=F32) + bb_ref[g]
                u, du_dpre = _gelu_parts(u_ref[rows, cols])
                dgate_t = dg_ref[rows, cols].astype(F32)
                du_ref[rows, cols] = (dgate_t * mixed * du_dpre).astype(du_ref.dtype)
                dmix = dgate_t * u
                dmix_b = dmix.astype(BF16)
                db_ref[g] += dmix
                dw_ref[g] += lax.dot_general(dmix_b, vt, (((1,), (1,)), ((), ())), preferred_element_type=F32)
                dvn_ref[rows, cols] = lax.dot_general(wc[g], dmix_b, (((0,), (0,)), ((), ())), preferred_element_type=F32)
        dvn = dvn_ref[...]
        dlg_ref[...] += _rsum8(dvn * xhat)
        dlb_ref[...] += _rsum8(dvn)
        dxhat = dvn * g_ref[...]
        dgv_in = r * (dxhat - jnp.mean(dxhat, axis=-1, keepdims=True) - xhat * jnp.mean(dxhat * xhat, axis=-1, keepdims=True))
        dv_ref[...] = (dgv_in * dgv).astype(dv_ref.dtype)

    row = pl.BlockSpec((tm, SGU_WIDTH), lambda i: (i, 0))
    vec = pl.BlockSpec((1, SGU_WIDTH), lambda i: (0, 0))
    mat = pl.BlockSpec((SGU_GROUPS, BLOCK, LANES), lambda i: (0, 0, 0))
    part = pl.BlockSpec((8, SGU_WIDTH), lambda i: (0, 0))
    mat_shape = jax.ShapeDtypeStruct((SGU_GROUPS, BLOCK, LANES), F32)
    part_shape = jax.ShapeDtypeStruct((8, SGU_WIDTH), F32)
    act_shape = jax.ShapeDtypeStruct((n_tok, SGU_WIDTH), BF16)
    return pl.pallas_call(
        body, name=name, grid=(n_tok // tm,), in_specs=[row, row, vec, vec, mat, mat, row],
        out_specs=[row, row, mat, mat, part, part], out_shape=[act_shape, act_shape, mat_shape, mat_shape, part_shape, part_shape],
        scratch_shapes=[pltpu.VMEM((tm, SGU_WIDTH), F32)], compiler_params=_params(("arbitrary",)),
    )(u_pre, v_pre, ln_g, ln_b, w_s, bias_rows, dgate)


def _pool_tile(n_tok):
    return _pick(n_tok, 256, POOL_HALO)


def pool_fwd(h, name):
    n_tok, width = h.shape
    gw = width // len(POOL_WINDOWS)
    tm = _pool_tile(n_tok)
    per = tm // POOL_HALO

    def body(cur_ref, halo_ref, o_ref, buf_ref):
        i = pl.program_id(0)
        buf_ref[0:POOL_HALO, :] = jnp.where(i > 0, halo_ref[...], 0.0)
        buf_ref[POOL_HALO:, :] = cur_ref[...]
        tok = i * tm + lax.broadcasted_iota(jnp.int32, (tm, 1), 0)
        for g, w in enumerate(POOL_WINDOWS):
            cols = slice(g * gw, (g + 1) * gw)
            acc = buf_ref[POOL_HALO:, cols]
            for j in range(1, w):
                acc = acc + buf_ref[POOL_HALO - j:POOL_HALO - j + tm, cols]
            cnt = jnp.minimum(tok + 1, w).astype(F32)
            o_ref[:, cols] = (acc / cnt - cur_ref[:, cols]).astype(o_ref.dtype)

    return pl.pallas_call(
        body, name=name, grid=(n_tok // tm,),
        in_specs=[pl.BlockSpec((tm, width), lambda i: (i, 0)),
                  pl.BlockSpec((POOL_HALO, width), lambda i: (jnp.maximum(i * per - 1, 0), 0))],
        out_specs=pl.BlockSpec((tm, width), lambda i: (i, 0)), out_shape=jax.ShapeDtypeStruct(h.shape, BF16),
        scratch_shapes=[pltpu.VMEM((tm + POOL_HALO, width), F32)], compiler_params=_params(("parallel",)),
    )(h, h)


def pool_bwd(dp, name):
    n_tok, width = dp.shape
    gw = width // len(POOL_WINDOWS)
    tm = _pool_tile(n_tok)
    per = tm // POOL_HALO
    n_steps = n_tok // tm

    def body(cur_ref, halo_ref, o_ref, buf_ref):
        i = pl.program_id(0)
        tok = i * tm + lax.broadcasted_iota(jnp.int32, (tm, 1), 0)
        for g, w in enumerate(POOL_WINDOWS):
            cols = slice(g * gw, (g + 1) * gw)
            cnt = jnp.minimum(tok + 1, w).astype(F32)
            buf_ref[0:tm, cols] = cur_ref[:, cols] / cnt
            buf_ref[tm:, cols] = jnp.where(i < n_steps - 1, halo_ref[:, cols] / float(w), 0.0)
        for g, w in enumerate(POOL_WINDOWS):
            cols = slice(g * gw, (g + 1) * gw)
            acc = buf_ref[0:tm, cols]
            for j in range(1, w):
                acc = acc + buf_ref[j:j + tm, cols]
            o_ref[:, cols] = acc - cur_ref[:, cols]

    return pl.pallas_call(
        body, name=name, grid=(n_steps,),
        in_specs=[pl.BlockSpec((tm, width), lambda i: (i, 0)),
                  pl.BlockSpec((POOL_HALO, width), lambda i: (jnp.minimum((i + 1) * per, n_tok // POOL_HALO - 1), 0))],
        out_specs=pl.BlockSpec((tm, width), lambda i: (i, 0)), out_shape=jax.ShapeDtypeStruct(dp.shape, F32),
        scratch_shapes=[pltpu.VMEM((tm + POOL_HALO, width), F32)], compiler_params=_params(("parallel",)),
    )(dp, dp)


def _ffn_fwd(x, h, gb, wg, wu, wd, tag, nxt):
    G = _mm(h, wg, batch="map", out_dtype=BF16, name=f"{tag}_gate")
    wu = wu(G) if callable(wu) else wu
    U, A = _mm(h, wu, batch="map", extras=[G], epilogue=_swiglu_tiles, out_dtypes=[BF16] * 2, name=f"{tag}_up")
    wd = wd(A) if callable(wd) else wd
    y = _mm(A, wd, batch="reduce", out_dtype=BF16, name=f"{tag}_down")
    return (*close_block(x, y, gb, 0.5, nxt, tag), (x, h, G, U, A, y, wg, wu, wd))


def _ffn_bwd(res, ga, gb, dx2, tag, early):
    x, h, G, U, A, y, wg, wu, wd = res
    dy, dgb = rms_bwd(y, gb, [dx2], 0.5, None, BF16, f"{tag}_postnorm_bwd")
    dG, dU = _mm(dy, wd, tb=True, batch="map", extras=[G, U], epilogue=_swiglu_bwd_tiles, out_dtypes=[BF16] * 2, name=f"{tag}_down_dx")
    dwd = _mm(A, dy, ta=True, batch="map", name=f"{tag}_down_dw", out_dtype=BF16)
    dwg = _mm(h, dG, ta=True, batch="map", name=f"{tag}_gate_dw", out_dtype=BF16, order=early("wd", dwd))
    dwu = _mm(h, dU, ta=True, batch="map", name=f"{tag}_up_dw", out_dtype=BF16, order=early("wg", dwg))
    dh = _mm(dG, wg, tb=True, batch="reduce", more=[(dU, wu)], out_dtype=BF16, name=f"{tag}_gate_up_dx", order=early("wu", dwu))
    dx, dga = rms_bwd(x, ga, [dh], 1.0, dx2, F32, f"{tag}_prenorm_bwd")
    return dx, dga, dgb, dwg, dwu, dwd


def _sink_rows(sinks):
    return jnp.repeat(sinks.reshape(N_KV_HEADS, Q_PER_KV), BLOCK, axis=1)[..., None]


def _attn_sgu_fwd(x, h, g_post, w_in, w_out, sinks, ln_g, ln_b, sgu_w, bias_rows, tabs, tag, nxt):
    z = _mm(h, w_in, name=f"{tag}_in")
    qk = rope_apply(z, tabs, QK_WIDTH // LANES, False, BF16, f"{tag}_rope")
    q = _to_heads(qk[:, :ATTN_WIDTH], N_Q_HEADS)
    k = _to_heads(qk[:, ATTN_WIDTH:], N_KV_HEADS)
    v = _to_heads(z[:, QK_WIDTH:QK_WIDTH + KV_WIDTH].astype(BF16), N_KV_HEADS)
    o = swa_fwd(q, k, v, _sink_rows(sinks), f"{tag}_swa")
    u_pre = z[:, QK_WIDTH + KV_WIDTH:QK_WIDTH + KV_WIDTH + SGU_WIDTH]
    v_pre = z[:, QK_WIDTH + KV_WIDTH + SGU_WIDTH:]
    gate = sgu_fwd(u_pre, v_pre, ln_g, ln_b, sgu_w, bias_rows, f"{tag}_sgu")
    cat = jnp.concatenate([_from_heads(o), gate], axis=1)
    m = _mm(cat, w_out, out_dtype=BF16, name=f"{tag}_out")
    return (*close_block(x, m, g_post, 1.0, nxt, tag), (x, h, q, k, v, u_pre, v_pre, cat, m))


def _attn_sgu_bwd(res, g_pre, g_post, w_in, w_out, sinks, ln_g, ln_b, sgu_w, bias_rows, tabs, dx2, tag):
    x, h, q, k, v, u_pre, v_pre, cat, m = res
    dm, dg_post = rms_bwd(m, g_post, [dx2], 1.0, None, BF16, f"{tag}_postnorm_bwd")
    dcat = _mm(dm, w_out, tb=True, out_dtype=BF16, name=f"{tag}_out_dx")
    dw_out = _mm(cat, dm, ta=True, name=f"{tag}_out_dw", out_dtype=BF16)
    do = _to_heads(dcat[:, :ATTN_WIDTH], N_Q_HEADS)
    dq, dkp, dkc, dvp, dvc, dsink = swa_bwd(q, k, v, _sink_rows(sinks), do, f"{tag}_swa_bwd")
    d_sinks = jnp.sum(dsink[..., 0], axis=1).reshape(1, N_Q_HEADS)
    dqk_rot = jnp.concatenate([_from_heads(dq).astype(F32), _fold_kv_grad(dkp, dkc)], axis=1)
    dqk = rope_apply(dqk_rot, tabs, QK_WIDTH // LANES, True, BF16, f"{tag}_rope_bwd")
    dv = _fold_kv_grad(dvp, dvc).astype(BF16)
    du_pre, dv_pre, dw_s, dbias, dlg, dlb = sgu_bwd(u_pre, v_pre, ln_g, ln_b, sgu_w, bias_rows, dcat[:, ATTN_WIDTH:], f"{tag}_sgu_bwd")
    dz = jnp.concatenate([dqk, dv, du_pre, dv_pre], axis=1)
    dw_in = _mm(h, dz, ta=True, name=f"{tag}_in_dw", out_dtype=BF16)
    dh = _mm(dz, w_in, tb=True, out_dtype=BF16, name=f"{tag}_in_dx")
    dx, dg_pre = rms_bwd(x, g_pre, [dh], 1.0, dx2, F32, f"{tag}_prenorm_bwd")
    causal = jnp.tril(jnp.ones((BLOCK, BLOCK), F32))
    small = dict(attn_sinks=d_sinks, sgu_ln_g=jnp.sum(dlg, axis=0, keepdims=True), sgu_ln_b=jnp.sum(dlb, axis=0, keepdims=True),
                 sgu_w=(dw_s * causal[None])[None], sgu_b=jnp.sum(dbias, axis=-1)[None])
    return dx, dg_pre, dg_post, dw_in, dw_out, small


def _pool_mix_fwd(x, hf, g_post, pool_w, pool_scale, tag, nxt):
    pooled = pool_fwd(hf, f"{tag}_pool")
    n_g = len(POOL_WINDOWS)
    ypre = _mm(pooled, pool_w, batch="map", groups=n_g, a_cb=True, o_cb=True, name=f"{tag}_proj")
    m = scale_cols(ypre, pool_scale, f"{tag}_scale")
    return (*close_block(x, m, g_post, 1.0, nxt, tag), (x, pooled, ypre, m))


def _pool_mix_bwd(res, g_pre, g_post, pool_w, pool_scale, dx2, tag):
    x, pooled, ypre, m = res
    n_g = len(POOL_WINDOWS)
    dm, dg_post = rms_bwd(m, g_post, [dx2], 1.0, None, F32, f"{tag}_postnorm_bwd")
    dypre, dscale = scale_cols_bwd(dm, ypre, pool_scale, f"{tag}_scale_bwd")
    dpooled = _mm(dypre, pool_w, tb=True, batch="map", groups=n_g, a_cb=True, o_cb=True, name=f"{tag}_proj_dx")
    dpw = _mm(pooled, dypre, ta=True, batch="map", groups=n_g, a_cb=True, b_cb=True, name=f"{tag}_proj_dw", out_dtype=BF16)
    dhf = pool_bwd(dpooled, f"{tag}_pool_bwd")
    dx, dg_pre = rms_bwd(x, g_pre, [dhf], 1.0, dx2, F32, f"{tag}_prenorm_bwd")
    return dx, dg_pre, dg_post, dpw, jnp.sum(dscale, axis=0, keepdims=True)


def _xattn_fwd(x, h, mem, g_post, g_mem, wq, wk, wv, wo, tag, nxt):
    mem_n = rms_fwd(mem, g_mem, BF16, f"{tag}_memnorm")
    q = _mm(h, wq, out_dtype=BF16, name=f"{tag}_q")
    k = _mm(mem_n, wk, out_dtype=BF16, name=f"{tag}_k")
    v = _mm(mem_n, wv, out_dtype=BF16, name=f"{tag}_v")
    o = xattn_fwd(q, k, v, f"{tag}_core")
    r = _mm(o, wo, out_dtype=BF16, name=f"{tag}_o")
    return (*close_block(x, r, g_post, 1.0, nxt, tag), (x, h, mem_n, q, k, v, o, r))


def _xattn_bwd(res, mem, g_pre, g_post, g_mem, wq, wk, wv, wo, dx2, tag):
    x, h, mem_n, q, k, v, o, r = res
    dr, dg_post = rms_bwd(r, g_post, [dx2], 1.0, None, BF16, f"{tag}_postnorm_bwd")
    do = _mm(dr, wo, tb=True, out_dtype=BF16, name=f"{tag}_o_dx")
    dwo = _mm(o, dr, ta=True, name=f"{tag}_o_dw", out_dtype=BF16)
    dq, dk, dv = xattn_bwd(q, k, v, do, f"{tag}_core_bwd")
    dk, dv = dk.astype(BF16), dv.astype(BF16)
    dwq = _mm(h, dq, ta=True, name=f"{tag}_q_dw", out_dtype=BF16)
    dwk = _mm(mem_n, dk, ta=True, name=f"{tag}_k_dw", out_dtype=BF16)
    dwv = _mm(mem_n, dv, ta=True, name=f"{tag}_v_dw", out_dtype=BF16)
    dh = _mm(dq, wq, tb=True, out_dtype=BF16, name=f"{tag}_q_dx")
    dmem1 = _mm(dk, wk, tb=True, name=f"{tag}_k_dx")
    dmem2 = _mm(dv, wv, tb=True, name=f"{tag}_v_dx")
    _, dg_mem = rms_bwd(mem, g_mem, [dmem1, dmem2], 1.0, None, BF16, f"{tag}_memnorm_bwd")
    dx, dg_pre = rms_bwd(x, g_pre, [dh], 1.0, dx2, F32, f"{tag}_prenorm_bwd")
    return dx, dg_pre, dg_post, dg_mem, dwq, dwk, dwv, dwo


def _rowsum8(part):
    return jnp.sum(part, axis=0, keepdims=True)


def device_step(x, mem, target, norms, mem_norm, small, fetch, emit, early):
    n_tok = x.shape[0]
    tabs = rope_tables(n_tok)
    bias_rows = jnp.broadcast_to(small["sgu_b"][:, :, None], (SGU_GROUPS, BLOCK, LANES))
    gn = lambda l, i: norms[l, i][None, :]
    gm = lambda l: mem_norm[l][None, :]
    mix0 = (small["attn_sinks"], small["sgu_ln_g"], small["sgu_ln_b"], small["sgu_w"], bias_rows, tabs)

    wts, saved = {}, []
    pre_gain = {"ffn1": 0, "mix": 2, "xattn": 4, "ffn2": 6}
    in_dtype = lambda kind, l: F32 if (kind, l) == ("mix", 1) else BF16

    def then(kind, l):
        def nxt(after):
            wts[kind, l], tok = fetch((kind, l), after)
            return gn(l, pre_gain[kind]) + tok, in_dtype(kind, l)
        return nxt

    g_first, _ = then("ffn1", 0)(x)
    h = rms_fwd(x, g_first, BF16, "l0_ffn1_prenorm")
    for l in range(2):
        x, h, r1 = _ffn_fwd(x, h, gn(l, 1), *wts["ffn1", l], f"l{l}_ffn1", then("mix", l))
        if l == 0:
            x, h, r2 = _attn_sgu_fwd(x, h, gn(l, 3), *wts["mix", l], *mix0, f"l{l}_mix", then("xattn", l))
        else:
            x, h, r2 = _pool_mix_fwd(x, h, gn(l, 3), *wts["mix", l], small["pool_scale"], f"l{l}_mix", then("xattn", l))
        x, h, r3 = _xattn_fwd(x, h, mem, gn(l, 5), gm(l), *wts["xattn", l], f"l{l}_xattn", then("ffn2", l))
        x, h, r4 = _ffn_fwd(x, h, gn(l, 7), *wts["ffn2", l], f"l{l}_ffn2", then("ffn1", l + 1) if l == 0 else None)
        saved.append((r1, r2, r3, r4))

    loss_part, dx = loss_and_grad(x, target, "loss")

    g_norm_rows = [[None] * 8, [None] * 8]
    g_mem_rows = [None, None]
    g_small = {}
    tok = 0.0
    for l in (1, 0):
        r1, r2, r3, r4 = saved[l]
        dx, g_norm_rows[l][6], g_norm_rows[l][7], dwg, dwu, dwd = _ffn_bwd(
            r4, gn(l, 6), gn(l, 7) + tok, dx, f"l{l}_ffn2", lambda part, g, l=l: early(("ffn2", l), f"ffn2_{part}", g))
        tok = emit(("ffn2", l), dict(ffn2_wg=dwg, ffn2_wu=dwu, ffn2_wd=dwd), dx)
        dx, g_norm_rows[l][4], g_norm_rows[l][5], g_mem_rows[l], dwq, dwk, dwv, dwo = _xattn_bwd(
            r3, mem, gn(l, 4), gn(l, 5) + tok, gm(l), *wts["xattn", l], dx, f"l{l}_xattn")
        tok = emit(("xattn", l), dict(x_wq=dwq, x_wk=dwk, x_wv=dwv, x_wo=dwo), dx)
        if l == 0:
            dx, g_norm_rows[l][2], g_norm_rows[l][3], dw_in, dw_out, sm = _attn_sgu_bwd(
                r2, gn(l, 2), gn(l, 3) + tok, *wts["mix", l], *mix0, dx, f"l{l}_mix")
            tok = emit(("mix", l), dict(mix_w_in=dw_in, mix_w_out=dw_out), dx)
            g_small.update(sm)
        else:
            dx, g_norm_rows[l][2], g_norm_rows[l][3], dpw, dscale = _pool_mix_bwd(
                r2, gn(l, 2), gn(l, 3) + tok, *wts["mix", l], small["pool_scale"], dx, f"l{l}_mix")
            tok = emit(("mix", l), dict(pool_w=dpw), dx)
            g_small["pool_scale"] = dscale
        dx, g_norm_rows[l][0], g_norm_rows[l][1], dwg, dwu, dwd = _ffn_bwd(
            r1, gn(l, 0), gn(l, 1) + tok, dx, f"l{l}_ffn1", lambda part, g, l=l: early(("ffn1", l), f"ffn1_{part}", g))
        tok = emit(("ffn1", l), dict(ffn1_wg=dwg, ffn1_wu=dwu, ffn1_wd=dwd), dx)
    g_norms = jnp.stack([jnp.concatenate([_rowsum8(p) for p in g_norm_rows[l]], axis=0) for l in range(2)])
    g_mem_norm = jnp.concatenate([_rowsum8(p) for p in g_mem_rows], axis=0)
    return loss_part, dx, g_small, g_norms, g_mem_norm, tok


ANY = pl.BlockSpec(memory_space=pl.ANY)


def _place():
    x, y, c = lax.axis_index("x"), lax.axis_index("y"), lax.axis_index("c")
    other_chips = [(1 - x, y), (x, 1 - y), (1 - x, 1 - y)]
    return x, y, c, other_chips


def _half_rows(core, n_rows):
    half = n_rows // 2
    return pl.ds(pl.multiple_of(core * half, 16), half)


def _remote(src, dst, send_sem, recv_sem, device):
    return pltpu.make_async_remote_copy(src_ref=src, dst_ref=dst, send_sem=send_sem, recv_sem=recv_sem,
                                        device_id=device, device_id_type=MESH)


HBM = pl.BlockSpec(memory_space=pltpu.HBM)
SEM = pl.BlockSpec(memory_space=pltpu.SEMAPHORE)
DATAFLOW = pltpu.SideEffectType.DATAFLOW_SIDE_EFFECTING


def _chip_copies(bufs, send_sems, recv_sems):
    x, y, c, chips = _place()
    me = 2 * x + y
    sends, arrivals = [], []
    for i, buf in enumerate(bufs):
        rows = _half_rows(c, buf.shape[1])
        for r, (px, py) in enumerate(chips):
            mine, theirs = buf.at[me, rows], buf.at[2 * px + py, rows]
            sends.append(_remote(mine, mine, send_sems.at[3 * i + r], recv_sems.at[3 * i + r], (px, py, c)))
            arrivals.append(_remote(theirs, theirs, send_sems.at[3 * i + r], recv_sems.at[3 * i + r], (px, py, c)))
    return sends, arrivals


def copies_start(arrays, fresh, copies, n_sems, after, name):
    operands = [pltpu.with_memory_space_constraint(a, pltpu.HBM) for a in arrays]
    operands += [pltpu.with_memory_space_constraint(lax.empty(f.shape, f.dtype), pltpu.HBM) for f in fresh]
    n = len(operands)

    def body(*refs):
        send_sems, recv_sems = refs[n + 1], refs[n + 2]
        bufs, token = refs[n + 3:2 * n + 3], refs[2 * n + 3]
        for cp in copies(bufs, send_sems, recv_sems)[0]:
            cp.start()
        token[...] = jnp.zeros_like(token)

    res = pl.pallas_call(
        body, name=name, in_specs=[HBM] * n + [ANY],
        out_specs=[SEM, SEM] + [HBM] * n + [pl.BlockSpec(memory_space=pltpu.VMEM)],
        out_shape=[pltpu.SemaphoreType.DMA((n_sems,)), pltpu.SemaphoreType.DMA((n_sems,))]
        + [pltpu.HBM(o.shape, o.dtype) for o in operands] + [jax.ShapeDtypeStruct((8, LANES), F32)],
        input_output_aliases={i: 2 + i for i in range(n)}, compiler_params=pltpu.CompilerParams(has_side_effects=DATAFLOW),
    )(*operands, after)
    return res[0], res[1], list(res[2:2 + n]), res[2 + n]


def copies_wait(send_sems, recv_sems, bufs, copies, after, name):
    n = len(bufs)

    def body(*refs):
        for cp in copies(refs[:n], refs[n], refs[n + 1])[1]:
            cp.wait_send()
            cp.wait_recv()

    return list(pl.pallas_call(
        body, name=name, in_specs=[HBM] * n + [SEM, SEM] + [ANY] * len(after), out_specs=[HBM] * n,
        out_shape=[pltpu.HBM(b.shape, b.dtype) for b in bufs], input_output_aliases={i: i for i in range(n)},
        compiler_params=pltpu.CompilerParams(has_side_effects=DATAFLOW),
    )(*bufs, send_sems, recv_sems, *after))


def _to_sibling_copies(n_src, src_rows):
    def copies(bufs, send_sems, recv_sems):
        x, y, c, _ = _place()
        sends, arrivals = [], []
        for i in range(n_src):
            src, land = bufs[i], bufs[n_src + i]
            src = src.at[:, src_rows(c, src)] if src_rows is not None else src
            sends.append(_remote(src, land, send_sems.at[i], recv_sems.at[i], (x, y, 1 - c)))
            arrivals.append(_remote(land, land, send_sems.at[i], recv_sems.at[i], (x, y, 1 - c)))
        return sends, arrivals
    return copies


def _forward_copies(bufs, send_sems, recv_sems):
    x, y, c, chips = _place()
    sends, arrivals = [], []
    for i, buf in enumerate(bufs):
        for r, (px, py) in enumerate(chips):
            mine = buf.at[2 * px + py, _half_rows(c, buf.shape[1])]
            theirs = buf.at[2 * px + py, _half_rows(1 - c, buf.shape[1])]
            sends.append(_remote(mine, mine, send_sems.at[3 * i + r], recv_sems.at[3 * i + r], (x, y, 1 - c)))
            arrivals.append(_remote(theirs, theirs, send_sems.at[3 * i + r], recv_sems.at[3 * i + r], (x, y, 1 - c)))
    return sends, arrivals


def forward_to_sibling(bufs, name):
    n = len(bufs)

    def body(*refs):
        sends, arrivals = _forward_copies(refs[n:2 * n], *refs[2 * n:])
        for cp in sends:
            cp.start()
        for cp in arrivals:
            cp.wait_recv()
        for cp in sends:
            cp.wait_send()

    return pl.pallas_call(
        body, name=name, in_specs=[ANY] * n, out_specs=[ANY] * n, input_output_aliases={i: i for i in range(n)},
        out_shape=[jax.ShapeDtypeStruct(b.shape, b.dtype) for b in bufs],
        scratch_shapes=[pltpu.SemaphoreType.DMA((3 * n,)), pltpu.SemaphoreType.DMA((3 * n,))],
    )(*bufs)


def add_own_half(g, p, core, name):
    n_j, n_rows, n_cols = g.shape
    half = n_rows // 2
    tr = _rows(half, n_cols * (2 * g.dtype.itemsize + 2), WEIGHT_TILE_BYTES)

    def body(c_ref, g_ref, p_ref, o_ref):
        o_ref[...] = (g_ref[...] + p_ref[...]).astype(o_ref.dtype)

    grid_spec = pltpu.PrefetchScalarGridSpec(
        num_scalar_prefetch=1, grid=(n_j, half // tr),
        in_specs=[pl.BlockSpec((None, None, tr, n_cols), lambda j, i, c_ref: (j, c_ref[0], i, 0)),
                  pl.BlockSpec((None, tr, n_cols), lambda j, i, c_ref: (j, i, 0))],
        out_specs=pl.BlockSpec((None, tr, n_cols), lambda j, i, c_ref: (j, i, 0)))
    return pl.pallas_call(
        body, name=name, grid_spec=grid_spec, out_shape=jax.ShapeDtypeStruct((n_j, half, n_cols), BF16),
        compiler_params=_params(("parallel", "parallel")),
    )(core, g.reshape(n_j, 2, half, n_cols), p)


def _scatter_copies(n_parts):
    def copies(bufs, send_sems, recv_sems):
        x, y, c, chips = _place()
        me = 2 * x + y
        sends, arrivals = [], []
        for i in range(n_parts):
            part, land = bufs[i], bufs[n_parts + i]
            for r, (px, py) in enumerate(chips):
                theirs = land.at[2 * px + py]
                sends.append(_remote(part.at[2 * px + py], land.at[me], send_sems.at[3 * i + r], recv_sems.at[3 * i + r], (px, py, c)))
                arrivals.append(_remote(theirs, theirs, send_sems.at[3 * i + r], recv_sems.at[3 * i + r], (px, py, c)))
        return sends, arrivals
    return copies


def sum_over_chips(own, got, chip, name):
    n_s, n_rows, n_cols = got.shape
    tr = _rows(n_rows, n_cols * (got.dtype.itemsize * (n_s + 1) + 4), WEIGHT_TILE_BYTES)

    def body(chip_ref, own_ref, *refs):
        got_refs, o_ref = refs[:n_s], refs[n_s]
        me = chip_ref[0]
        acc = jnp.where(me == 0, own_ref[...], got_refs[0][...]).astype(F32)
        for k in range(1, n_s):
            acc = acc + jnp.where(me == k, own_ref[...], got_refs[k][...]).astype(F32)
        o_ref[...] = acc

    def slot(k):
        return pl.BlockSpec((None, tr, n_cols), lambda i, chip_ref: (jnp.where(chip_ref[0] == k, (k + 1) % n_s, k), i, 0))

    grid_spec = pltpu.PrefetchScalarGridSpec(
        num_scalar_prefetch=1, grid=(n_rows // tr,),
        in_specs=[pl.BlockSpec((None, tr, n_cols), lambda i, chip_ref: (chip_ref[0], i, 0))] + [slot(k) for k in range(n_s)],
        out_specs=pl.BlockSpec((tr, n_cols), lambda i, chip_ref: (i, 0)))
    return pl.pallas_call(
        body, name=name, grid_spec=grid_spec, out_shape=jax.ShapeDtypeStruct((n_rows, n_cols), F32),
        compiler_params=_params(("parallel",)),
    )(chip, own, *([got] * n_s))


def sum_slots(q, name):
    n_s, n_rows, n_cols = q.shape
    tr = _rows(n_rows, n_cols * (q.dtype.itemsize * n_s + 4))

    def body(q_ref, o_ref):
        acc = q_ref[0].astype(F32)
        for s in range(1, n_s):
            acc = acc + q_ref[s].astype(F32)
        o_ref[...] = acc

    return pl.pallas_call(
        body, name=name, grid=(n_rows // tr,), in_specs=[pl.BlockSpec((n_s, tr, n_cols), lambda i: (0, i, 0))],
        out_specs=pl.BlockSpec((tr, n_cols), lambda i: (i, 0)), out_shape=jax.ShapeDtypeStruct((n_rows, n_cols), F32),
        compiler_params=_params(("parallel",)),
    )(q)


def gather_all_devices(s, name):
    def body(s_ref, o_ref, send_sems, recv_sems, local_sem):
        x, y, c, _ = _place()
        me = 4 * x + 2 * y + c
        local = pltpu.make_async_copy(s_ref, o_ref.at[me], local_sem)
        local.start()
        sends = []
        for f in range(1, N_DEV):
            px, py, pc = x ^ (f >> 2), y ^ ((f >> 1) & 1), c ^ (f & 1)
            sends.append(_remote(s_ref, o_ref.at[me], send_sems.at[f - 1], recv_sems.at[f - 1], (px, py, pc)))
            sends[-1].start()
        for f in range(1, N_DEV):
            px, py, pc = x ^ (f >> 2), y ^ ((f >> 1) & 1), c ^ (f & 1)
            landed = o_ref.at[4 * px + 2 * py + pc]
            _remote(landed, landed, send_sems.at[f - 1], recv_sems.at[f - 1], (px, py, pc)).wait_recv()
        for cp in sends:
            cp.wait_send()
        local.wait()

    return pl.pallas_call(
        body, name=name, in_specs=[ANY], out_specs=ANY, out_shape=jax.ShapeDtypeStruct((N_DEV,) + s.shape, s.dtype),
        scratch_shapes=[pltpu.SemaphoreType.DMA((N_DEV - 1,)), pltpu.SemaphoreType.DMA((N_DEV - 1,)), pltpu.SemaphoreType.DMA],
    )(s)


FFN_NAMES = ("ffn1_wg", "ffn1_wu", "ffn1_wd", "ffn2_wg", "ffn2_wu", "ffn2_wd")
XATTN_NAMES = ("x_wq", "x_wk", "x_wv", "x_wo")
BIG_NAMES = FFN_NAMES + XATTN_NAMES + ("mix_w_in", "mix_w_out", "pool_w")
SMALL_NAMES = ("norms", "mem_norm", "attn_sinks", "sgu_ln_g", "sgu_ln_b", "sgu_w", "sgu_b", "pool_scale")
WEIGHT_ORDER = ("norms", "mem_norm") + BIG_NAMES[:-1] + ("attn_sinks", "sgu_ln_g", "sgu_ln_b", "sgu_w", "sgu_b", "pool_w", "pool_scale")
COLUMN_CUT = ("x_wo", "mix_w_in")
N_POOL = len(POOL_WINDOWS)
BLOCK_ORDER = (("ffn1", 0), ("mix", 0), ("xattn", 0), ("ffn2", 0), ("ffn1", 1), ("mix", 1), ("xattn", 1), ("ffn2", 1))
PREFETCH_AT = ((1, 2, 3), (), (), (4,), (5, 6, 7), (), (), ())


def block_weight_names(kind, layer):
    if kind == "mix":
        return ("mix_w_in", "mix_w_out") if layer == 0 else ("pool_w",)
    return XATTN_NAMES if kind == "xattn" else tuple(f"{kind}_{part}" for part in ("wg", "wu", "wd"))


def _to_matmul_layout(name, g):
    n_j, n_rows, n_cols = g.shape
    if name in COLUMN_CUT:
        return g.transpose(1, 0, 2).reshape(n_rows, n_j * n_cols)
    if name == "pool_w":
        return g.reshape(n_j, N_POOL, n_rows // N_POOL, n_cols).transpose(1, 0, 2, 3).reshape(N_POOL, n_j * n_rows // N_POOL, n_cols)
    if name in ("x_wq", "x_wk", "x_wv", "mix_w_out"):
        return g.reshape(n_j * n_rows, n_cols)
    return g


def _from_matmul_layout(name, d):
    if name in COLUMN_CUT:
        n_rows, wide = d.shape
        return d.reshape(n_rows, N_CHIPS, wide // N_CHIPS).transpose(1, 0, 2)
    if name == "pool_w":
        n_g, n_in, n_cols = d.shape
        return d.reshape(n_g, N_CHIPS, n_in // N_CHIPS, n_cols).transpose(1, 0, 2, 3).reshape(N_CHIPS, n_g * n_in // N_CHIPS, n_cols)
    if name in ("x_wq", "x_wk", "x_wv", "mix_w_out"):
        return d.reshape(N_CHIPS, d.shape[0] // N_CHIPS, d.shape[1])
    return d


def _as3(w):
    return w.reshape(w.shape[0], -1, w.shape[-1])


def _pack(arrays, row_multiple):
    flat = jnp.concatenate([a.reshape(-1) for a in arrays])
    per = LANES * row_multiple
    total = -(-flat.shape[0] // per) * per
    return jnp.pad(flat, (0, total - flat.shape[0])).reshape(total // LANES, LANES)


def _unpack(packed, like):
    flat, out, at = packed.reshape(-1), [], 0
    for a in like:
        out.append(flat[at:at + a.size].reshape(a.shape))
        at += a.size
    return out


def kernel(x, mem, norms, mem_norm, ffn1_wg, ffn1_wu, ffn1_wd, ffn2_wg, ffn2_wu, ffn2_wd, x_wq, x_wk, x_wv, x_wo, mix_w_in, mix_w_out, attn_sinks, sgu_ln_g, sgu_ln_b, sgu_w, sgu_b, pool_w, pool_scale, loss_target, m_norms, m_mem_norm, m_ffn1_wg, m_ffn1_wu, m_ffn1_wd, m_ffn2_wg, m_ffn2_wu, m_ffn2_wd, m_x_wq, m_x_wk, m_x_wv, m_x_wo, m_mix_w_in, m_mix_w_out, m_attn_sinks, m_sgu_ln_g, m_sgu_ln_b, m_sgu_w, m_sgu_b, m_pool_w, m_pool_scale, v_norms, v_mem_norm, v_ffn1_wg, v_ffn1_wu, v_ffn1_wd, v_ffn2_wg, v_ffn2_wu, v_ffn2_wd, v_x_wq, v_x_wk, v_x_wv, v_x_wo, v_mix_w_in, v_mix_w_out, v_attn_sinks, v_sgu_ln_g, v_sgu_ln_b, v_sgu_w, v_sgu_b, v_pool_w, v_pool_scale):
    given = dict(locals())
    w = {n: given[n] for n in WEIGHT_ORDER}
    mom = {n: given["m_" + n] for n in WEIGHT_ORDER}
    var = {n: given["v_" + n] for n in WEIGHT_ORDER}
    chip_id = 2 * lax.axis_index("x") + lax.axis_index("y")
    chip = chip_id.astype(jnp.int32).reshape(1)
    core = lax.axis_index("c").astype(jnp.int32).reshape(1)
    n_shard = norms.shape[-1]

    keys = [(name, l) for name in BIG_NAMES for l in range(_as3(w[name]).shape[0])]
    first_keys = [(name, 0) for name in block_weight_names(*BLOCK_ORDER[0])]
    cast = lambda name, l: cast_into_slot(_as3(w[name]), l, chip, f"cast_{name}")
    slot_of = {key: cast(*key) for key in first_keys}
    small_rows = jnp.concatenate([norms.reshape(-1, n_shard), pool_scale, jnp.zeros((15, n_shard), F32)], axis=0)
    small_slot = lax.dynamic_update_slice_in_dim(jnp.zeros((N_CHIPS,) + small_rows.shape, F32), small_rows[None], chip_id, axis=0)
    pending = {}
    gather_start = lambda slots, after, name: copies_start(slots, [], _chip_copies, 3 * len(slots), after, name)
    pending["0a"] = gather_start([slot_of[first_keys[0]], small_slot], chip, "gather_start_0a")
    pending["0b"] = gather_start([slot_of[first_keys[1]]], pending["0a"][2][0], "gather_start_0b")
    pending["0c"] = gather_start([slot_of[first_keys[2]]], pending["0b"][2][0], "gather_start_0c")
    for key in keys:
        if key not in slot_of:
            slot_of[key] = cast(*key)


    def block_slots(k):
        kind, layer = BLOCK_ORDER[k]
        return [slot_of[name, layer if kind != "mix" else 0] for name in block_weight_names(kind, layer)]

    def start(k, after):
        pending[k] = gather_start(block_slots(k), after, f"gather_start_{k}")

    def finish(k, after):
        send_sems, recv_sems, bufs, _ = pending[k]
        return forward_to_sibling(copies_wait(send_sems, recv_sems, bufs, _chip_copies, [after], f"gather_wait_{k}"), f"gather_forward_{k}")

    def fetch(key, after):
        k = BLOCK_ORDER.index(key)
        names = block_weight_names(*key)
        token = 0.0
        if k == 0:
            late = lambda part, name: lambda after: _to_matmul_layout(name, finish(part, after)[0])
            bufs = [first_gate]
            weights = (_to_matmul_layout(names[0], first_gate), late("0b", names[1]), late("0c", names[2]))
        elif key[0] in ("ffn1", "ffn2"):
            send_sems, recv_sems, bufs, _ = pending[k]
            bufs = copies_wait(send_sems, recv_sems, bufs, _chip_copies, [after], f"gather_wait_{k}")
            gate, = forward_to_sibling(bufs[:1], f"gather_forward_{k}_gate")
            send_sems, recv_sems, rest, begun = copies_start(bufs[1:], [], _forward_copies, 6, gate, f"gather_forward_{k}_start")
            landed = []

            def late(i, name):
                def get(after):
                    if not landed:
                        landed.extend(copies_wait(send_sems, recv_sems, rest, _forward_copies, [after], f"gather_forward_{k}_wait"))
                    return _to_matmul_layout(name, landed[i])
                return get
            bufs = [gate]
            weights = (_to_matmul_layout(names[0], gate), late(0, names[1]), late(1, names[2]))
            token = begun[0, 0]
        else:
            bufs = finish(k, after)
            weights = tuple(_to_matmul_layout(name, b) for name, b in zip(names, bufs))
        for ahead in PREFETCH_AT[k]:
            start(ahead, bufs[0])
            token = token + pending[ahead][3][0, 0]
        return weights, token

    to_sibling, to_chips, halves, own_of, recv_of = [], [], [], {}, {}
    other_half = lambda c, src: _half_rows(1 - c, src.shape[1])

    def sibling_start(tag, names, full, after):
        lands = [jax.ShapeDtypeStruct((g.shape[0], g.shape[1] // 2, g.shape[2]), g.dtype) for g in full]
        send_sems, recv_sems, bufs, token = copies_start(full, lands, _to_sibling_copies(len(full), other_half), len(full), after,
                                                         f"grads_to_sibling_{tag}")
        to_sibling.append((tag, names, send_sems, recv_sems, bufs))
        return token

    def early(key, name, g):
        kind, layer = key
        return sibling_start(f"{name}{layer}", [(name, layer)], [_from_matmul_layout(name, g)], chip)

    def chips_start(tag, after):
        names, chip_sums = [], []
        for part, part_names, send_sems, recv_sems, bufs in to_sibling:
            n = len(part_names)
            bufs = copies_wait(send_sems, recv_sems, bufs, _to_sibling_copies(n, other_half), after, f"grads_from_sibling_{part}")
            chip_sums += [add_own_half(g, p, core, f"chip_sum_{name}") for (name, _), g, p in zip(part_names, bufs[:n], bufs[n:])]
            names += part_names
        to_sibling.clear()
        n = len(names)
        send_sems, recv_sems, bufs, token = copies_start(chip_sums, chip_sums, _scatter_copies(n), 3 * n, after[0], f"grads_start_{tag}")
        to_chips.append((tag, names, send_sems, recv_sems, bufs))
        return token[0, 0]

    def chips_wait(after):
        tag, names, send_sems, recv_sems, bufs = to_chips.pop()
        return tag, names, copies_wait(send_sems, recv_sems, bufs, _scatter_copies(len(names)), after, f"grads_wait_{tag}")

    def reduce_landed(tag, names, bufs):
        n = len(names)
        own = [sum_over_chips(t, q, chip, f"sum_{name}") for (name, _), t, q in zip(names, bufs[:n], bufs[n:])]
        send_sems, recv_sems, bufs, _ = copies_start(own, own, _to_sibling_copies(n, None), n, chip, f"grads_halves_{tag}")
        halves.append((tag, names, send_sems, recv_sems, bufs))

    def halves_land(after):
        for tag, names, send_sems, recv_sems, bufs in halves:
            n = len(names)
            bufs = copies_wait(send_sems, recv_sems, bufs, _to_sibling_copies(n, None), after, f"grads_halves_wait_{tag}")
            for key, o, r in zip(names, bufs[:n], bufs[n:]):
                own_of[key], recv_of[key] = o, r
        halves.clear()

    def emit(key, grads_of, after):
        kind, layer = key
        landed = chips_wait([after]) if to_chips else None
        begun = {name for _, part_names, _, _, _ in to_sibling for name, _ in part_names}
        rest = {name: g for name, g in grads_of.items() if name not in begun}
        if rest:
            sibling_start(f"{kind}{layer}", [(name, layer if kind != "mix" else 0) for name in rest],
                          [_from_matmul_layout(name, g) for name, g in rest.items()], after)
        token = chips_start(f"{kind}{layer}", [after])
        if landed:
            reduce_landed(*landed)
        return token

    first_gate, small_all = finish("0a", pending["0c"][2][0])
    n_norm_rows = norms.shape[0] * norms.shape[1]
    norms_all = jnp.concatenate([small_all[j, :n_norm_rows].reshape(norms.shape) for j in range(N_CHIPS)], axis=-1)
    pool_scale_all = jnp.concatenate([small_all[j, n_norm_rows:n_norm_rows + 1] for j in range(N_CHIPS)], axis=-1)
    small = dict(attn_sinks=attn_sinks[0], sgu_ln_g=sgu_ln_g, sgu_ln_b=sgu_ln_b, sgu_w=sgu_w[0], sgu_b=sgu_b[0], pool_scale=pool_scale_all)

    loss_part, dx, g_small, g_norms, g_mem_norm, last_token = device_step(
        x[0], mem[0], loss_target[0], norms_all, mem_norm, small, fetch, emit, early)
    loss = lax.psum(0.5 * jnp.sum(loss_part) / x.shape[-1], ("x", "y", "c"))
    last_names = {name for name, _ in to_chips[0][1]}
    order = last_token.reshape(1, 1)
    halves_land([dx])

    grads, delta, new_m, new_v = {}, {}, {}, {}

    def update(name):
        n_l = _as3(w[name]).shape[0]
        res = adamw_from_halves(_as3(w[name]), [own_of[name, l] for l in range(n_l)], [recv_of[name, l] for l in range(n_l)],
                                _as3(mom[name]), _as3(var[name]), core, order, f"adamw_{name}")
        grads[name], delta[name], new_m[name], new_v[name] = (t.reshape(w[name].shape) for t in res)

    for name in BIG_NAMES:
        if name not in last_names:
            update(name)
    done = [delta[name] for name in BIG_NAMES if name not in last_names]
    reduce_landed(*chips_wait(done))
    halves_land(done)
    for name in BIG_NAMES:
        if name in last_names:
            update(name)

    small_g = [g_norms, g_mem_norm, g_small["attn_sinks"], g_small["sgu_ln_g"], g_small["sgu_ln_b"], g_small["sgu_w"], g_small["sgu_b"],
               g_small["pool_scale"]]
    packed = _pack(small_g, 16)
    summed = sum_slots(gather_all_devices(packed, "small_grads_all"), "small_grads_sum")
    s_norms, s_mem, s_sinks, s_lg, s_lb, s_w, s_b, s_scale = _unpack(summed, small_g)
    grads["norms"] = lax.dynamic_slice_in_dim(s_norms, chip_id * n_shard, n_shard, axis=2)
    grads["pool_scale"] = lax.dynamic_slice_in_dim(s_scale, chip_id * n_shard, n_shard, axis=1)
    grads.update(mem_norm=s_mem, attn_sinks=s_sinks, sgu_ln_g=s_lg, sgu_ln_b=s_lb, sgu_w=s_w, sgu_b=s_b)
    like = [w[n] for n in SMALL_NAMES]
    packs = [_pack([src[n] for n in SMALL_NAMES], 16)[None] for src in (w, grads, mom, var)]
    for dst, t in zip((delta, new_m, new_v), adamw(*packs, "adamw_small")):
        for n, a in zip(SMALL_NAMES, _unpack(t[0], like)):
            dst[n] = a

    outs = [loss, dx[None]]
    for group in (grads, delta, new_m, new_v):
        outs += [group[n] for n in WEIGHT_ORDER]
    return tuple(outs)
```

```python
import functools

import jax
import jax.numpy as jnp
from jax import lax
from jax.experimental import pallas as pl
from jax.experimental.pallas import tpu as pltpu

F32 = jnp.float32
BF16 = jnp.bfloat16
MESH = pl.DeviceIdType.MESH

EPS = 1e-6
ROPE_THETA = 500000.0
ROPE_HALF = 8
HEAD_DIM = 64
N_Q_HEADS = 16
N_KV_HEADS = 2
Q_PER_KV = 8
BLOCK = 128
ATTN_WIDTH = 1024
KV_WIDTH = 128
QK_WIDTH = ATTN_WIDTH + KV_WIDTH
SGU_WIDTH = 1024
SGU_GROUPS = 8
POOL_WINDOWS = (2, 4, 8, 16)
POOL_HALO = 16
X_HEADS = 4
X_HEAD_DIM = 128
N_CHIPS = 4
N_DEV = 8

ADAM_LR = 0.001
ADAM_B1 = 0.9
ADAM_B2 = 0.999
ADAM_EPS = 1e-08
ADAM_WD = 0.01
ADAM_STEP = 10

VMEM_LIMIT_V7X = 52 * 1024 * 1024
MM_VMEM_BUDGET = 44 * 1024 * 1024
LANES = 128
ROW_TILE_BYTES = 14 * 1024 * 1024
WEIGHT_TILE_BYTES = 12 * 1024 * 1024
MXU_FLOPS_V7X = 1.0e15
HBM_BYTES_PER_S_V7X = 3.0e12
VMEM_STORE_BYTES_PER_S = 4.0e12
MXU_WEIGHT_LOAD_ROWS = 192
MXU_NARROW_COLS = 128
GRID_STEP_S = 0.35e-6


def _params(sem):
    return pltpu.CompilerParams(dimension_semantics=sem, vmem_limit_bytes=VMEM_LIMIT_V7X)


def _pick(dim, pref, align):
    cands = [t for t in range(align, dim + 1, align) if dim % t == 0]
    small = [t for t in cands if t <= pref]
    if small and small[-1] * 2 >= min(pref, dim):
        return small[-1]
    return dim


def _rows(n_rows, bytes_per_row, tile_bytes=ROW_TILE_BYTES):
    want = max(16, min(1024, tile_bytes // max(1, bytes_per_row)))
    cands = [t for t in range(16, n_rows + 1, 16) if n_rows % t == 0 and t <= want]
    return cands[-1] if cands else n_rows


def _divisors(dim, align, most):
    return [t for t in range(align, min(dim, most) + 1, align) if dim % t == 0] or [dim]


def _mm_tiles(M, N, K, J, m_align, k_align, a_bytes, b_bytes, o_bytes, reduce, ta, products=1):
    best = None
    for tm in _divisors(M, m_align, 2048):
        for tn in _divisors(N, LANES, 2048):
            for tk in _divisors(K, k_align, 4096):
                split = K // tk > 1 or reduce
                vmem = 2 * (tm * tk * a_bytes + tk * tn * b_bytes + tm * tn * o_bytes) + tm * tn * 4 * products * (2 if split else 1)
                if ta:
                    vmem += tm * tk * a_bytes
                if vmem > MM_VMEM_BUDGET:
                    continue
                steps = J * (M // tm) * (N // tn) * (K // tk)
                mxu = 2.0 * J * M * N * K / MXU_FLOPS_V7X * (tm + MXU_WEIGHT_LOAD_ROWS) / tm * (tn + MXU_NARROW_COLS) / tn
                acc = J * M * N * (K // tk) * 8 / VMEM_STORE_BYTES_PER_S if split else 0.0
                hbm = J * (M * K * a_bytes * (N // tn) + K * N * b_bytes * (M // tm) + M * N * o_bytes) / HBM_BYTES_PER_S_V7X
                cost = max(mxu + 0.5 * acc, hbm) + steps * GRID_STEP_S
                if best is None or cost < best[0]:
                    best = (cost, tm, tn, tk)
    return best[1:]


def _rsum8(v):
    r, c = v.shape
    return v.reshape(r // 8, 8, c).sum(axis=0)


def _mm(a, b, *, name, ta=False, tb=False, batch="none", groups=0, a_cb=False, b_cb=False, o_cb=False,
        out_dtype=F32, more=(), extras=(), epilogue=None, out_dtypes=None, order=None):
    J = groups or (a.shape[0] if a.ndim == 3 else (b.shape[0] if b.ndim == 3 else 1))
    a2, b2 = a.shape[-2:], b.shape[-2:]
    M, K = (a2[1], a2[0]) if ta else a2
    N, Kb = b2 if tb else (b2[1], b2[0])
    if a_cb:
        if ta:
            M //= J
        else:
            K //= J
    if b_cb:
        if tb:
            Kb //= J
        else:
            N //= J
    assert K == Kb, (name, a.shape, b.shape)
    reduce = batch == "reduce"
    out_dtypes = list(out_dtypes or [out_dtype])
    n_terms = 1 + len(more)
    o_bytes = sum(jnp.dtype(d).itemsize for d in out_dtypes) + sum(e.dtype.itemsize for e in extras)
    n_prod = n_terms if epilogue is not None else 1
    tm, tn, tk = _mm_tiles(M, N, K, J, LANES if ta else 16, LANES if (not ta or tb) else 16, a.dtype.itemsize * n_terms,
                           b.dtype.itemsize * n_terms, o_bytes, reduce, ta, n_prod)
    nm, nn, nk = M // tm, N // tn, K // tk
    if reduce:
        grid = (nm, nn, J, nk)
        unpack = lambda m, n, j, k: (j, m, n, k)
        sem = ("parallel", "parallel", "arbitrary", "arbitrary")
    else:
        grid = (J, nm, nn, nk)
        unpack = lambda j, m, n, k: (j, m, n, k)
        sem = ("parallel", "parallel", "parallel", "arbitrary")

    def a_map(*g):
        j, m, n, k = unpack(*g)
        r, c = (k, m) if ta else (m, k)
        if a_cb:
            c = c + j * (nm if ta else nk)
        return (j, r, c) if a.ndim == 3 else (r, c)

    def b_map(*g):
        j, m, n, k = unpack(*g)
        r, c = (n, k) if tb else (k, n)
        if b_cb:
            c = c + j * (nk if tb else nn)
        return (j, r, c) if b.ndim == 3 else (r, c)

    def o_map(*g):
        j, m, n, k = unpack(*g)
        if o_cb:
            return (m, n + j * nn)
        return (j, m, n) if batch == "map" else (m, n)

    a_blk = (tk, tm) if ta else (tm, tk)
    b_blk = (tn, tk) if tb else (tk, tn)
    a_spec = pl.BlockSpec(((None,) + a_blk) if a.ndim == 3 else a_blk, a_map)
    b_spec = pl.BlockSpec(((None,) + b_blk) if b.ndim == 3 else b_blk, b_map)
    if o_cb:
        out_shape, o_blk = (M, N * J), (tm, tn)
    elif batch == "map":
        out_shape, o_blk = (J, M, N), (None, tm, tn)
    else:
        out_shape, o_blk = (M, N), (tm, tn)
    o_spec = pl.BlockSpec(o_blk, o_map)
    dims = (((0 if ta else 1,), (1 if tb else 0,)), ((), ()))
    red_axes = (2, 3) if reduce else (3,)
    split = reduce or nk > 1
    n_ex, n_out = len(extras), len(out_dtypes)
    n_ord = 0 if order is None else 1

    def body(*refs):
        refs = refs[n_ord:]
        ab_refs, ex_refs = refs[:2 * n_terms], refs[2 * n_terms:2 * n_terms + n_ex]
        o_refs, acc = refs[2 * n_terms + n_ex:2 * n_terms + n_ex + n_out], refs[2 * n_terms + n_ex + n_out:]
        prods = [lax.dot_general(ab_refs[2 * t][...], ab_refs[2 * t + 1][...], dims, preferred_element_type=F32) for t in range(n_terms)]
        if epilogue is None:
            prods = [functools.reduce(lambda p, q: p + q, prods)]

        def finish(vals):
            outs = epilogue(vals, [e[...] for e in ex_refs]) if epilogue is not None else vals
            for o_ref, val in zip(o_refs, outs):
                o_ref[...] = val.astype(o_ref.dtype)

        if not split:
            finish(prods)
            return
        first = functools.reduce(jnp.logical_and, [pl.program_id(ax) == 0 for ax in red_axes])
        last = functools.reduce(jnp.logical_and, [pl.program_id(ax) == grid[ax] - 1 for ax in red_axes])

        @pl.when(first)
        def _():
            for acc_ref, prod in zip(acc, prods):
                acc_ref[...] = prod

        @pl.when(jnp.logical_not(first))
        def _():
            for acc_ref, prod in zip(acc, prods):
                acc_ref[...] += prod

        @pl.when(last)
        def _():
            finish([acc_ref[...] for acc_ref in acc])

    operands = [order] * n_ord + [a, b] + [t for pair in more for t in pair] + list(extras)
    res = pl.pallas_call(
        body, name=name, grid=grid, out_specs=[o_spec] * n_out,
        in_specs=[pl.BlockSpec(memory_space=pl.ANY)] * n_ord + [a_spec, b_spec] * n_terms + [o_spec] * n_ex,
        out_shape=[jax.ShapeDtypeStruct(out_shape, d) for d in out_dtypes],
        scratch_shapes=[pltpu.VMEM((tm, tn), F32)] * (n_prod if split else 0), compiler_params=_params(sem),
    )(*operands)
    return res if epilogue is not None else res[0]


def _rowwise(fn, tiled, whole, outs, accs, *, name):
    n_rows = tiled[0].shape[0]
    row_bytes = sum(t.shape[1] * t.dtype.itemsize for t in tiled) + sum(c * jnp.dtype(d).itemsize for c, d in outs)
    tr = _rows(n_rows, row_bytes)
    n_t, n_w, n_o = len(tiled), len(whole), len(outs)

    def body(*refs):
        i = pl.program_id(0)
        t_refs, w_refs = refs[:n_t], refs[n_t:n_t + n_w]
        o_refs, a_refs = refs[n_t + n_w:n_t + n_w + n_o], refs[n_t + n_w + n_o:]
        o_vals, a_vals = fn(i, *[r[...] for r in t_refs], *[r[...] for r in w_refs])
        for r, v in zip(o_refs, o_vals):
            r[...] = v.astype(r.dtype)
        if a_refs:
            @pl.when(i == 0)
            def _():
                for r in a_refs:
                    r[...] = jnp.zeros_like(r)
            for r, v in zip(a_refs, a_vals):
                r[...] += v

    in_specs = [pl.BlockSpec((tr, t.shape[1]), lambda i: (i, 0)) for t in tiled]
    in_specs += [pl.BlockSpec(w.shape, lambda i, nd=w.ndim: (0,) * nd) for w in whole]
    out_specs = [pl.BlockSpec((tr, c), lambda i: (i, 0)) for c, _ in outs]
    out_specs += [pl.BlockSpec(s, lambda i, nd=len(s): (0,) * nd) for s, _ in accs]
    out_shape = [jax.ShapeDtypeStruct((n_rows, c), d) for c, d in outs]
    out_shape += [jax.ShapeDtypeStruct(s, d) for s, d in accs]
    res = pl.pallas_call(
        body, name=name, grid=(n_rows // tr,), in_specs=in_specs, out_specs=out_specs, out_shape=out_shape,
        compiler_params=_params(("arbitrary",) if accs else ("parallel",)),
    )(*tiled, *whole)
    return res


def _rms_stats(x):
    x = x.astype(F32)
    r = lax.rsqrt(jnp.mean(x * x, axis=-1, keepdims=True) + EPS)
    return x * r, r


def rms_fwd(x, g, out_dtype, name):
    def fn(i, x, g):
        xhat, _ = _rms_stats(x)
        return [xhat * g], []
    return _rowwise(fn, [x], [g], [(x.shape[1], out_dtype)], [], name=name)[0]


def postnorm_res(x, y, g, s, name):
    def fn(i, x, y, g):
        yhat, _ = _rms_stats(y)
        return [x + s * (yhat * g)], []
    return _rowwise(fn, [x, y], [g], [(x.shape[1], F32)], [], name=name)[0]


def close_block(x, y, g, s, nxt, tag):
    if nxt is None:
        return postnorm_res(x, y, g, s, f"{tag}_postnorm"), None
    g_next, next_dtype = nxt(y)

    def fn(i, x, y, g, g_next):
        yhat, _ = _rms_stats(y)
        x2 = x + s * (yhat * g)
        return [x2, _rms_stats(x2)[0] * g_next], []
    return _rowwise(fn, [x, y], [g, g_next], [(x.shape[1], F32), (x.shape[1], next_dtype)], [], name=f"{tag}_postnorm_next")


def rms_bwd(xin, g, douts, scale, add, out_dtype, name):
    n_d = len(douts)

    def fn(i, x, *rest):
        ds, rest = rest[:n_d], rest[n_d:]
        ad = rest[0] if add is not None else None
        g = rest[-1]
        xhat, r = _rms_stats(x)
        d = ds[0].astype(F32)
        for e in ds[1:]:
            d = d + e.astype(F32)
        if scale != 1.0:
            d = d * scale
        dg = _rsum8(d * xhat)
        dxhat = d * g
        dx = r * (dxhat - xhat * jnp.mean(dxhat * xhat, axis=-1, keepdims=True))
        if ad is not None:
            dx = dx + ad
        return [dx], [dg]

    tiled = [xin, *douts] + ([add] if add is not None else [])
    dx, dg = _rowwise(fn, tiled, [g], [(xin.shape[1], out_dtype)], [((8, xin.shape[1]), F32)], name=name)
    return dx, dg


def _silu_parts(g):
    sg = 1.0 / (1.0 + jnp.exp(-g))
    return g * sg, sg


def _swiglu_tiles(products, saved):
    u, = products
    return [u, _silu_parts(saved[0].astype(F32))[0] * u]


def _swiglu_bwd_tiles(products, saved):
    da, = products
    g, u = saved[0].astype(F32), saved[1].astype(F32)
    s, sg = _silu_parts(g)
    return [da * u * (sg * (1.0 + g * (1.0 - sg))), da * s]


def scale_cols(y, s, name):
    def fn(i, y, s):
        return [y * s], []
    return _rowwise(fn, [y], [s], [(y.shape[1], F32)], [], name=name)[0]


def scale_cols_bwd(dm, y, s, name):
    def fn(i, dm, y, s):
        return [dm * s], [_rsum8(dm * y)]
    return _rowwise(fn, [dm, y], [s], [(y.shape[1], BF16)], [((8, y.shape[1]), F32)], name=name)


def loss_and_grad(y, target, name):
    n_feat = y.shape[1]

    def fn(i, y, t):
        e = y - t
        return [e * (1.0 / n_feat)], [_rsum8(e * e)]
    dy, part = _rowwise(fn, [y, target], [], [(n_feat, F32)], [((8, n_feat), F32)], name=name)
    return part, dy


def cast_into_slot(w3, layer, chip, name):
    _, n_rows, n_cols = w3.shape
    tr = _rows(n_rows, n_cols * 6, WEIGHT_TILE_BYTES)

    def body(chip_ref, w_ref, o_ref):
        o_ref[...] = w_ref[...].astype(BF16)

    grid_spec = pltpu.PrefetchScalarGridSpec(
        num_scalar_prefetch=1, grid=(n_rows // tr,),
        in_specs=[pl.BlockSpec((None, tr, n_cols), lambda i, chip_ref: (layer, i, 0))],
        out_specs=pl.BlockSpec((None, tr, n_cols), lambda i, chip_ref: (chip_ref[0], i, 0)))
    return pl.pallas_call(
        body, name=name, grid_spec=grid_spec, out_shape=jax.ShapeDtypeStruct((N_CHIPS, n_rows, n_cols), BF16),
        compiler_params=_params(("parallel",)),
    )(chip, w3)


def _adam_update(w, g, m, v):
    c1 = 1.0 / (1.0 - ADAM_B1 ** ADAM_STEP)
    c2 = 1.0 / (1.0 - ADAM_B2 ** ADAM_STEP)
    m = ADAM_B1 * m + (1.0 - ADAM_B1) * g
    v = ADAM_B2 * v + (1.0 - ADAM_B2) * (g * g)
    return -ADAM_LR * ((m * c1) / (jnp.sqrt(v * c2) + ADAM_EPS) + ADAM_WD * w), m, v


def adamw(w, g, m, v, name):
    n_l, n_rows, n_cols = w.shape
    tr = _rows(n_rows, n_cols * 4 * 7)

    def body(w_ref, g_ref, m_ref, v_ref, d_ref, mo_ref, vo_ref):
        d_ref[...], mo_ref[...], vo_ref[...] = _adam_update(w_ref[...], g_ref[...], m_ref[...], v_ref[...])

    spec = pl.BlockSpec((None, tr, n_cols), lambda l, i: (l, i, 0))
    shp = jax.ShapeDtypeStruct(w.shape, F32)
    return pl.pallas_call(
        body, name=name, grid=(n_l, n_rows // tr), in_specs=[spec] * 4, out_specs=[spec] * 3, out_shape=[shp] * 3,
        compiler_params=_params(("parallel", "parallel")),
    )(w, g, m, v)


def adamw_from_halves(w, own, recv, m, v, core, order, name):
    n_l, n_rows, n_cols = w.shape
    half = n_rows // 2
    tr = _rows(half, n_cols * 4 * 9, WEIGHT_TILE_BYTES)
    per = half // tr

    def body(core_ref, *refs):
        w_ref, m_ref, v_ref, order_ref = refs[:4]
        own_refs, recv_refs = refs[4:4 + n_l], refs[4 + n_l:4 + 2 * n_l]
        g_ref, d_ref, mo_ref, vo_ref = refs[4 + 2 * n_l:]
        l, h = pl.program_id(0), pl.program_id(1)
        mine = h == core_ref[0]
        g = jnp.where(mine, own_refs[0][...], recv_refs[0][...])
        for k in range(1, n_l):
            g = jnp.where(l == k, jnp.where(mine, own_refs[k][...], recv_refs[k][...]), g)
        g = g + order_ref[...]
        g_ref[...] = g
        d_ref[...], mo_ref[...], vo_ref[...] = _adam_update(w_ref[...], g, m_ref[...], v_ref[...])

    full = pl.BlockSpec((None, tr, n_cols), lambda l, h, i, core_ref: (l, h * per + i, 0))

    def piece(layer, is_own):
        def index(l, h, i, core_ref):
            used = (l == layer) & ((h == core_ref[0]) == is_own)
            return (jnp.where(used, i, 0), 0)
        return pl.BlockSpec((tr, n_cols), index)

    grid_spec = pltpu.PrefetchScalarGridSpec(
        num_scalar_prefetch=1, grid=(n_l, 2, per),
        in_specs=[full] * 3 + [pl.BlockSpec((1, 1), lambda l, h, i, core_ref: (0, 0))]
        + [piece(k, True) for k in range(n_l)] + [piece(k, False) for k in range(n_l)],
        out_specs=[full] * 4)
    return pl.pallas_call(
        body, name=name, grid_spec=grid_spec, out_shape=[jax.ShapeDtypeStruct(w.shape, F32)] * 4,
        compiler_params=_params(("parallel", "parallel", "parallel")),
    )(core, w, m, v, order, *own, *recv)


def rope_tables(n_tok):
    inv = ROPE_THETA ** (-jnp.arange(ROPE_HALF, dtype=F32) * 2.0 / (2 * ROPE_HALF))
    ang = jnp.arange(n_tok, dtype=F32)[:, None] * inv[None, :]
    cos, sin = jnp.cos(ang), jnp.sin(ang)
    rest = HEAD_DIM - 2 * ROPE_HALF
    one, zero, z8 = jnp.ones((n_tok, rest), F32), jnp.zeros((n_tok, rest), F32), jnp.zeros((n_tok, ROPE_HALF), F32)
    c = jnp.concatenate([cos, cos, one], axis=1)
    s1 = jnp.concatenate([-sin, z8, zero], axis=1)
    s2 = jnp.concatenate([z8, sin, zero], axis=1)
    two = lambda t: jnp.concatenate([t, t], axis=1)
    return two(c), two(s1), two(s2)


def rope_apply(x, tabs, n_col_blocks, inverse, out_dtype, name):
    n_tok = x.shape[0]
    tr = _rows(n_tok, LANES * 4 * 6)

    def body(x_ref, c_ref, s1_ref, s2_ref, o_ref):
        x = x_ref[...].astype(F32)
        if inverse:
            out = x * c_ref[...] + pltpu.roll(x * s1_ref[...], ROPE_HALF, 1) + pltpu.roll(x * s2_ref[...], LANES - ROPE_HALF, 1)
        else:
            out = x * c_ref[...] + pltpu.roll(x, LANES - ROPE_HALF, 1) * s1_ref[...] + pltpu.roll(x, ROPE_HALF, 1) * s2_ref[...]
        o_ref[...] = out.astype(o_ref.dtype)

    tab_spec = pl.BlockSpec((tr, LANES), lambda i, c: (i, 0))
    blk = pl.BlockSpec((tr, LANES), lambda i, c: (i, c))
    return pl.pallas_call(
        body, name=name, grid=(n_tok // tr, n_col_blocks), in_specs=[blk, tab_spec, tab_spec, tab_spec], out_specs=blk,
        out_shape=jax.ShapeDtypeStruct((n_tok, n_col_blocks * LANES), out_dtype), compiler_params=_params(("parallel", "parallel")),
    )(x, *tabs)


def _swa_masks():
    rows = Q_PER_KV * BLOCK
    qi = lax.broadcasted_iota(jnp.int32, (rows, 2 * BLOCK), 0) & (BLOCK - 1)
    kj = lax.broadcasted_iota(jnp.int32, (rows, 2 * BLOCK), 1)
    rel = qi + BLOCK - kj
    band = (rel >= 0) & (rel < BLOCK)
    return jnp.where(jnp.stack([band & (kj >= BLOCK), band]), 0.0, -1e30).astype(F32)


def _swa_probs(q, k, sink, mask):
    s = lax.dot_general(q, k, (((1,), (1,)), ((), ())), preferred_element_type=F32) * (HEAD_DIM ** -0.5) + mask
    m = jnp.maximum(jnp.max(s, axis=-1, keepdims=True), sink)
    e = jnp.exp(s - m)
    es = jnp.exp(sink - m)
    inv = 1.0 / (jnp.sum(e, axis=-1, keepdims=True) + es)
    return e * inv, es * inv


def _swa_specs(n_blocks):
    q_spec = pl.BlockSpec((Q_PER_KV, BLOCK, HEAD_DIM), lambda h, n: (h, n, 0))
    prev = pl.BlockSpec((None, BLOCK, HEAD_DIM), lambda h, n: (h, jnp.maximum(n - 1, 0), 0))
    cur = pl.BlockSpec((None, BLOCK, HEAD_DIM), lambda h, n: (h, n, 0))
    sink = pl.BlockSpec((None, Q_PER_KV * BLOCK, 1), lambda h, n: (h, 0, 0))
    masks = pl.BlockSpec((2, Q_PER_KV * BLOCK, 2 * BLOCK), lambda h, n: (0, 0, 0))
    return q_spec, prev, cur, sink, masks


def swa_fwd(q, k, v, sink_rows, name):
    n_tok = q.shape[1]
    q_spec, prev, cur, sink, masks = _swa_specs(n_tok // BLOCK)

    def body(q_ref, kp_ref, kc_ref, vp_ref, vc_ref, s_ref, m_ref, o_ref):
        mask = m_ref[jnp.minimum(pl.program_id(1), 1)]
        qq = q_ref[...].reshape(Q_PER_KV * BLOCK, HEAD_DIM)
        kk = jnp.concatenate([kp_ref[...], kc_ref[...]], axis=0)
        vv = jnp.concatenate([vp_ref[...], vc_ref[...]], axis=0)
        p, _ = _swa_probs(qq, kk, s_ref[...], mask)
        o = jnp.dot(p.astype(BF16), vv, preferred_element_type=F32)
        o_ref[...] = o.reshape(Q_PER_KV, BLOCK, HEAD_DIM).astype(o_ref.dtype)

    return pl.pallas_call(
        body, name=name, grid=(N_KV_HEADS, n_tok // BLOCK), in_specs=[q_spec, prev, cur, prev, cur, sink, masks], out_specs=q_spec,
        out_shape=jax.ShapeDtypeStruct(q.shape, BF16), compiler_params=_params(("parallel", "parallel")),
    )(q, k, k, v, v, sink_rows, _swa_masks())


def swa_bwd(q, k, v, sink_rows, do, name):
    n_tok = q.shape[1]
    nb = n_tok // BLOCK
    q_spec, prev, cur, sink, masks = _swa_specs(nb)
    rows = Q_PER_KV * BLOCK

    def body(q_ref, kp_ref, kc_ref, vp_ref, vc_ref, s_ref, m_ref, do_ref, dq_ref, dkp_ref, dkc_ref, dvp_ref, dvc_ref, ds_ref):
        mask = m_ref[jnp.minimum(pl.program_id(1), 1)]
        qq = q_ref[...].reshape(rows, HEAD_DIM)
        dd = do_ref[...].reshape(rows, HEAD_DIM)
        kk = jnp.concatenate([kp_ref[...], kc_ref[...]], axis=0)
        vv = jnp.concatenate([vp_ref[...], vc_ref[...]], axis=0)
        p, ps = _swa_probs(qq, kk, s_ref[...], mask)
        dp = lax.dot_general(dd, vv, (((1,), (1,)), ((), ())), preferred_element_type=F32)
        delta = jnp.sum(p * dp, axis=-1, keepdims=True)
        ds = (p * (dp - delta) * (HEAD_DIM ** -0.5)).astype(BF16)
        dq = jnp.dot(ds, kk, preferred_element_type=F32)
        dk = lax.dot_general(ds, qq, (((0,), (0,)), ((), ())), preferred_element_type=F32)
        dv = lax.dot_general(p.astype(BF16), dd, (((0,), (0,)), ((), ())), preferred_element_type=F32)
        dq_ref[...] = dq.reshape(Q_PER_KV, BLOCK, HEAD_DIM).astype(dq_ref.dtype)
        dkp_ref[...] = dk[:BLOCK]
        dkc_ref[...] = dk[BLOCK:]
        dvp_ref[...] = dv[:BLOCK]
        dvc_ref[...] = dv[BLOCK:]
        dsink = jnp.broadcast_to(-ps * delta, (rows, LANES)).reshape(Q_PER_KV, BLOCK, LANES)
        ds_ref[...] = jnp.sum(dsink, axis=1)

    part = pl.BlockSpec((None, None, BLOCK, HEAD_DIM), lambda h, n: (h, n, 0, 0))
    part_shape = jax.ShapeDtypeStruct((N_KV_HEADS, nb, BLOCK, HEAD_DIM), F32)
    return pl.pallas_call(
        body, name=name, grid=(N_KV_HEADS, nb), in_specs=[q_spec, prev, cur, prev, cur, sink, masks, q_spec],
        out_specs=[q_spec, part, part, part, part, pl.BlockSpec((None, None, Q_PER_KV, LANES), lambda h, n: (h, n, 0, 0))],
        out_shape=[jax.ShapeDtypeStruct(q.shape, BF16), part_shape, part_shape, part_shape, part_shape,
                   jax.ShapeDtypeStruct((N_KV_HEADS, nb, Q_PER_KV, LANES), F32)],
        compiler_params=_params(("parallel", "parallel")),
    )(q, k, k, v, v, sink_rows, _swa_masks(), do)


def _to_heads(t, n_heads):
    return t.reshape(t.shape[0], n_heads, HEAD_DIM).transpose(1, 0, 2)


def _from_heads(t):
    return t.transpose(1, 0, 2).reshape(t.shape[1], -1)


def _fold_kv_grad(prev, cur):
    shifted = jnp.concatenate([prev[:, 1:], jnp.zeros_like(prev[:, :1])], axis=1)
    tot = (cur + shifted).reshape(N_KV_HEADS, -1, HEAD_DIM)
    return _from_heads(tot)


def _x_probs(qh, kh):
    s = lax.dot_general(qh, kh, (((1,), (1,)), ((), ())), preferred_element_type=F32) * (X_HEAD_DIM ** -0.5)
    e = jnp.exp(s - jnp.max(s, axis=-1, keepdims=True))
    return e * (1.0 / jnp.sum(e, axis=-1, keepdims=True))


def xattn_fwd(q, k, v, name):
    n_tok, width = q.shape
    n_mem = k.shape[0]
    tq = _pick(n_tok, 512, 16)

    def body(q_ref, k_ref, v_ref, o_ref):
        for h in range(X_HEADS):
            cols = slice(h * X_HEAD_DIM, (h + 1) * X_HEAD_DIM)
            p = _x_probs(q_ref[:, cols], k_ref[:, cols])
            o_ref[:, cols] = jnp.dot(p.astype(BF16), v_ref[:, cols], preferred_element_type=F32).astype(o_ref.dtype)

    row = pl.BlockSpec((tq, width), lambda i: (i, 0))
    mem = pl.BlockSpec((n_mem, width), lambda i: (0, 0))
    return pl.pallas_call(
        body, name=name, grid=(n_tok // tq,), in_specs=[row, mem, mem], out_specs=row,
        out_shape=jax.ShapeDtypeStruct(q.shape, BF16), compiler_params=_params(("parallel",)),
    )(q, k, v)


def xattn_bwd(q, k, v, do, name):
    n_tok, width = q.shape
    n_mem = k.shape[0]
    tq = _pick(n_tok, 512, 16)

    def body(q_ref, k_ref, v_ref, do_ref, dq_ref, dk_ref, dv_ref):
        @pl.when(pl.program_id(0) == 0)
        def _():
            dk_ref[...] = jnp.zeros_like(dk_ref)
            dv_ref[...] = jnp.zeros_like(dv_ref)

        for h in range(X_HEADS):
            cols = slice(h * X_HEAD_DIM, (h + 1) * X_HEAD_DIM)
            qh, kh, vh, dh = q_ref[:, cols], k_ref[:, cols], v_ref[:, cols], do_ref[:, cols]
            p = _x_probs(qh, kh)
            dp = lax.dot_general(dh, vh, (((1,), (1,)), ((), ())), preferred_element_type=F32)
            delta = jnp.sum(p * dp, axis=-1, keepdims=True)
            ds = (p * (dp - delta) * (X_HEAD_DIM ** -0.5)).astype(BF16)
            dq_ref[:, cols] = jnp.dot(ds, kh, preferred_element_type=F32).astype(dq_ref.dtype)
            dk_ref[:, cols] += lax.dot_general(ds, qh, (((0,), (0,)), ((), ())), preferred_element_type=F32)
            dv_ref[:, cols] += lax.dot_general(p.astype(BF16), dh, (((0,), (0,)), ((), ())), preferred_element_type=F32)

    row = pl.BlockSpec((tq, width), lambda i: (i, 0))
    mem = pl.BlockSpec((n_mem, width), lambda i: (0, 0))
    return pl.pallas_call(
        body, name=name, grid=(n_tok // tq,), in_specs=[row, mem, mem, row], out_specs=[row, mem, mem],
        out_shape=[jax.ShapeDtypeStruct(q.shape, BF16), jax.ShapeDtypeStruct(k.shape, F32), jax.ShapeDtypeStruct(k.shape, F32)],
        compiler_params=_params(("arbitrary",)),
    )(q, k, v, do)


GELU_C = 0.7978845608028654
GELU_A = 0.044715


def _gelu_parts(x):
    x2 = x * x
    t = jnp.tanh(GELU_C * x * (1.0 + GELU_A * x2))
    y = 0.5 * x * (1.0 + t)
    dy = 0.5 * (1.0 + t) + 0.5 * x * (1.0 - t * t) * GELU_C * (1.0 + 3.0 * GELU_A * x2)
    return y, dy


def _sgu_norm(v, ln_g, ln_b):
    mu = jnp.mean(v, axis=-1, keepdims=True)
    vc = v - mu
    r = lax.rsqrt(jnp.mean(vc * vc, axis=-1, keepdims=True) + EPS)
    xhat = vc * r
    return xhat * ln_g + ln_b, xhat, r


def _causal_weights(w_ref):
    i = lax.broadcasted_iota(jnp.int32, (BLOCK, BLOCK), 0)
    j = lax.broadcasted_iota(jnp.int32, (BLOCK, BLOCK), 1)
    return [jnp.where(i >= j, w_ref[g], 0.0).astype(BF16) for g in range(SGU_GROUPS)]


def sgu_fwd(u_pre, v_pre, ln_g, ln_b, w_s, bias_rows, name):
    n_tok = u_pre.shape[0]
    tm = _pick(n_tok, 512, BLOCK)

    def body(u_ref, v_ref, g_ref, b_ref, w_ref, bb_ref, o_ref):
        vn, _, _ = _sgu_norm(_gelu_parts(v_ref[...])[0], g_ref[...], b_ref[...])
        vn = vn.astype(BF16)
        wc = _causal_weights(w_ref)
        for c in range(tm // BLOCK):
            rows = slice(c * BLOCK, (c + 1) * BLOCK)
            for g in range(SGU_GROUPS):
                cols = slice(g * LANES, (g + 1) * LANES)
                mixed = jnp.dot(wc[g], vn[rows, cols], preferred_element_type=F32) + bb_ref[g]
                u = _gelu_parts(u_ref[rows, cols])[0]
                o_ref[rows, cols] = (u * mixed).astype(o_ref.dtype)

    row = pl.BlockSpec((tm, SGU_WIDTH), lambda i: (i, 0))
    vec = pl.BlockSpec((1, SGU_WIDTH), lambda i: (0, 0))
    mat = pl.BlockSpec((SGU_GROUPS, BLOCK, LANES), lambda i: (0, 0, 0))
    return pl.pallas_call(
        body, name=name, grid=(n_tok // tm,), in_specs=[row, row, vec, vec, mat, mat], out_specs=row,
        out_shape=jax.ShapeDtypeStruct((n_tok, SGU_WIDTH), BF16), compiler_params=_params(("parallel",)),
    )(u_pre, v_pre, ln_g, ln_b, w_s, bias_rows)


def sgu_bwd(u_pre, v_pre, ln_g, ln_b, w_s, bias_rows, dgate, name):
    n_tok = u_pre.shape[0]
    tm = _pick(n_tok, 512, BLOCK)

    def body(u_ref, v_ref, g_ref, b_ref, w_ref, bb_ref, dg_ref, du_ref, dv_ref, dw_ref, db_ref, dlg_ref, dlb_ref, dvn_ref):
        @pl.when(pl.program_id(0) == 0)
        def _():
            dw_ref[...] = jnp.zeros_like(dw_ref)
            db_ref[...] = jnp.zeros_like(db_ref)
            dlg_ref[...] = jnp.zeros_like(dlg_ref)
            dlb_ref[...] = jnp.zeros_like(dlb_ref)

        gv, dgv = _gelu_parts(v_ref[...])
        vn, xhat, r = _sgu_norm(gv, g_ref[...], b_ref[...])
        vn = vn.astype(BF16)
        wc = _causal_weights(w_ref)
        for c in range(tm // BLOCK):
            rows = slice(c * BLOCK, (c + 1) * BLOCK)
            for g in range(SGU_GROUPS):
                cols = slice(g * LANES, (g + 1) * LANES)
                vt = vn[rows, cols]
                mixed = jnp.dot(wc[g], vt, preferred_element_type=F32) + bb_ref[g]
                u, du_dpre = _gelu_parts(u_ref[rows, cols])
                dgate_t = dg_ref[rows, cols].astype(F32)
                du_ref[rows, cols] = (dgate_t * mixed * du_dpre).astype(du_ref.dtype)
                dmix = dgate_t * u
                dmix_b = dmix.astype(BF16)
                db_ref[g] += dmix
                dw_ref[g] += lax.dot_general(dmix_b, vt, (((1,), (1,)), ((), ())), preferred_element_type=F32)
                dvn_ref[rows, cols] = lax.dot_general(wc[g], dmix_b, (((0,), (0,)), ((), ())), preferred_element_type=F32)
        dvn = dvn_ref[...]
        dlg_ref[...] += _rsum8(dvn * xhat)
        dlb_ref[...] += _rsum8(dvn)
        dxhat = dvn * g_ref[...]
        dgv_in = r * (dxhat - jnp.mean(dxhat, axis=-1, keepdims=True) - xhat * jnp.mean(dxhat * xhat, axis=-1, keepdims=True))
        dv_ref[...] = (dgv_in * dgv).astype(dv_ref.dtype)

    row = pl.BlockSpec((tm, SGU_WIDTH), lambda i: (i, 0))
    vec = pl.BlockSpec((1, SGU_WIDTH), lambda i: (0, 0))
    mat = pl.BlockSpec((SGU_GROUPS, BLOCK, LANES), lambda i: (0, 0, 0))
    part = pl.BlockSpec((8, SGU_WIDTH), lambda i: (0, 0))
    mat_shape = jax.ShapeDtypeStruct((SGU_GROUPS, BLOCK, LANES), F32)
    part_shape = jax.ShapeDtypeStruct((8, SGU_WIDTH), F32)
    act_shape = jax.ShapeDtypeStruct((n_tok, SGU_WIDTH), BF16)
    return pl.pallas_call(
        body, name=name, grid=(n_tok // tm,), in_specs=[row, row, vec, vec, mat, mat, row],
        out_specs=[row, row, mat, mat, part, part], out_shape=[act_shape, act_shape, mat_shape, mat_shape, part_shape, part_shape],
        scratch_shapes=[pltpu.VMEM((tm, SGU_WIDTH), F32)], compiler_params=_params(("arbitrary",)),
    )(u_pre, v_pre, ln_g, ln_b, w_s, bias_rows, dgate)


def _pool_tile(n_tok):
    return _pick(n_tok, 256, POOL_HALO)


def pool_fwd(h, name):
    n_tok, width = h.shape
    gw = width // len(POOL_WINDOWS)
    tm = _pool_tile(n_tok)
    per = tm // POOL_HALO

    def body(cur_ref, halo_ref, o_ref, buf_ref):
        i = pl.program_id(0)
        buf_ref[0:POOL_HALO, :] = jnp.where(i > 0, halo_ref[...], 0.0)
        buf_ref[POOL_HALO:, :] = cur_ref[...]
        tok = i * tm + lax.broadcasted_iota(jnp.int32, (tm, 1), 0)
        for g, w in enumerate(POOL_WINDOWS):
            cols = slice(g * gw, (g + 1) * gw)
            acc = buf_ref[POOL_HALO:, cols]
            for j in range(1, w):
                acc = acc + buf_ref[POOL_HALO - j:POOL_HALO - j + tm, cols]
            cnt = jnp.minimum(tok + 1, w).astype(F32)
            o_ref[:, cols] = (acc / cnt - cur_ref[:, cols]).astype(o_ref.dtype)

    return pl.pallas_call(
        body, name=name, grid=(n_tok // tm,),
        in_specs=[pl.BlockSpec((tm, width), lambda i: (i, 0)),
                  pl.BlockSpec((POOL_HALO, width), lambda i: (jnp.maximum(i * per - 1, 0), 0))],
        out_specs=pl.BlockSpec((tm, width), lambda i: (i, 0)), out_shape=jax.ShapeDtypeStruct(h.shape, BF16),
        scratch_shapes=[pltpu.VMEM((tm + POOL_HALO, width), F32)], compiler_params=_params(("parallel",)),
    )(h, h)


def pool_bwd(dp, name):
    n_tok, width = dp.shape
    gw = width // len(POOL_WINDOWS)
    tm = _pool_tile(n_tok)
    per = tm // POOL_HALO
    n_steps = n_tok // tm

    def body(cur_ref, halo_ref, o_ref, buf_ref):
        i = pl.program_id(0)
        tok = i * tm + lax.broadcasted_iota(jnp.int32, (tm, 1), 0)
        for g, w in enumerate(POOL_WINDOWS):
            cols = slice(g * gw, (g + 1) * gw)
            cnt = jnp.minimum(tok + 1, w).astype(F32)
            buf_ref[0:tm, cols] = cur_ref[:, cols] / cnt
            buf_ref[tm:, cols] = jnp.where(i < n_steps - 1, halo_ref[:, cols] / float(w), 0.0)
        for g, w in enumerate(POOL_WINDOWS):
            cols = slice(g * gw, (g + 1) * gw)
            acc = buf_ref[0:tm, cols]
            for j in range(1, w):
                acc = acc + buf_ref[j:j + tm, cols]
            o_ref[:, cols] = acc - cur_ref[:, cols]

    return pl.pallas_call(
        body, name=name, grid=(n_steps,),
        in_specs=[pl.BlockSpec((tm, width), lambda i: (i, 0)),
                  pl.BlockSpec((POOL_HALO, width), lambda i: (jnp.minimum((i + 1) * per, n_tok // POOL_HALO - 1), 0))],
        out_specs=pl.BlockSpec((tm, width), lambda i: (i, 0)), out_shape=jax.ShapeDtypeStruct(dp.shape, F32),
        scratch_shapes=[pltpu.VMEM((tm + POOL_HALO, width), F32)], compiler_params=_params(("parallel",)),
    )(dp, dp)


def _ffn_fwd(x, h, gb, wg, wu, wd, tag, nxt):
    G = _mm(h, wg, batch="map", out_dtype=BF16, name=f"{tag}_gate")
    wu = wu(G) if callable(wu) else wu
    U, A = _mm(h, wu, batch="map", extras=[G], epilogue=_swiglu_tiles, out_dtypes=[BF16] * 2, name=f"{tag}_up")
    wd = wd(A) if callable(wd) else wd
    y = _mm(A, wd, batch="reduce", out_dtype=BF16, name=f"{tag}_down")
    return (*close_block(x, y, gb, 0.5, nxt, tag), (x, h, G, U, A, y, wg, wu, wd))


def _ffn_bwd(res, ga, gb, dx2, tag, early):
    x, h, G, U, A, y, wg, wu, wd = res
    dy, dgb = rms_bwd(y, gb, [dx2], 0.5, None, BF16, f"{tag}_postnorm_bwd")
    dG, dU = _mm(dy, wd, tb=True, batch="map", extras=[G, U], epilogue=_swiglu_bwd_tiles, out_dtypes=[BF16] * 2, name=f"{tag}_down_dx")
    dwd = _mm(A, dy, ta=True, batch="map", name=f"{tag}_down_dw", out_dtype=BF16)
    dwg = _mm(h, dG, ta=True, batch="map", name=f"{tag}_gate_dw", out_dtype=BF16, order=early("wd", dwd))
    dwu = _mm(h, dU, ta=True, batch="map", name=f"{tag}_up_dw", out_dtype=BF16, order=early("wg", dwg))
    dh = _mm(dG, wg, tb=True, batch="reduce", more=[(dU, wu)], out_dtype=BF16, name=f"{tag}_gate_up_dx", order=early("wu", dwu))
    dx, dga = rms_bwd(x, ga, [dh], 1.0, dx2, F32, f"{tag}_prenorm_bwd")
    return dx, dga, dgb, dwg, dwu, dwd


def _sink_rows(sinks):
    return jnp.repeat(sinks.reshape(N_KV_HEADS, Q_PER_KV), BLOCK, axis=1)[..., None]


def _attn_sgu_fwd(x, h, g_post, w_in, w_out, sinks, ln_g, ln_b, sgu_w, bias_rows, tabs, tag, nxt):
    z = _mm(h, w_in, name=f"{tag}_in")
    qk = rope_apply(z, tabs, QK_WIDTH // LANES, False, BF16, f"{tag}_rope")
    q = _to_heads(qk[:, :ATTN_WIDTH], N_Q_HEADS)
    k = _to_heads(qk[:, ATTN_WIDTH:], N_KV_HEADS)
    v = _to_heads(z[:, QK_WIDTH:QK_WIDTH + KV_WIDTH].astype(BF16), N_KV_HEADS)
    o = swa_fwd(q, k, v, _sink_rows(sinks), f"{tag}_swa")
    u_pre = z[:, QK_WIDTH + KV_WIDTH:QK_WIDTH + KV_WIDTH + SGU_WIDTH]
    v_pre = z[:, QK_WIDTH + KV_WIDTH + SGU_WIDTH:]
    gate = sgu_fwd(u_pre, v_pre, ln_g, ln_b, sgu_w, bias_rows, f"{tag}_sgu")
    cat = jnp.concatenate([_from_heads(o), gate], axis=1)
    m = _mm(cat, w_out, out_dtype=BF16, name=f"{tag}_out")
    return (*close_block(x, m, g_post, 1.0, nxt, tag), (x, h, q, k, v, u_pre, v_pre, cat, m))


def _attn_sgu_bwd(res, g_pre, g_post, w_in, w_out, sinks, ln_g, ln_b, sgu_w, bias_rows, tabs, dx2, tag):
    x, h, q, k, v, u_pre, v_pre, cat, m = res
    dm, dg_post = rms_bwd(m, g_post, [dx2], 1.0, None, BF16, f"{tag}_postnorm_bwd")
    dcat = _mm(dm, w_out, tb=True, out_dtype=BF16, name=f"{tag}_out_dx")
    dw_out = _mm(cat, dm, ta=True, name=f"{tag}_out_dw", out_dtype=BF16)
    do = _to_heads(dcat[:, :ATTN_WIDTH], N_Q_HEADS)
    dq, dkp, dkc, dvp, dvc, dsink = swa_bwd(q, k, v, _sink_rows(sinks), do, f"{tag}_swa_bwd")
    d_sinks = jnp.sum(dsink[..., 0], axis=1).reshape(1, N_Q_HEADS)
    dqk_rot = jnp.concatenate([_from_heads(dq).astype(F32), _fold_kv_grad(dkp, dkc)], axis=1)
    dqk = rope_apply(dqk_rot, tabs, QK_WIDTH // LANES, True, BF16, f"{tag}_rope_bwd")
    dv = _fold_kv_grad(dvp, dvc).astype(BF16)
    du_pre, dv_pre, dw_s, dbias, dlg, dlb = sgu_bwd(u_pre, v_pre, ln_g, ln_b, sgu_w, bias_rows, dcat[:, ATTN_WIDTH:], f"{tag}_sgu_bwd")
    dz = jnp.concatenate([dqk, dv, du_pre, dv_pre], axis=1)
    dw_in = _mm(h, dz, ta=True, name=f"{tag}_in_dw", out_dtype=BF16)
    dh = _mm(dz, w_in, tb=True, out_dtype=BF16, name=f"{tag}_in_dx")
    dx, dg_pre = rms_bwd(x, g_pre, [dh], 1.0, dx2, F32, f"{tag}_prenorm_bwd")
    causal = jnp.tril(jnp.ones((BLOCK, BLOCK), F32))
    small = dict(attn_sinks=d_sinks, sgu_ln_g=jnp.sum(dlg, axis=0, keepdims=True), sgu_ln_b=jnp.sum(dlb, axis=0, keepdims=True),
                 sgu_w=(dw_s * causal[None])[None], sgu_b=jnp.sum(dbias, axis=-1)[None])
    return dx, dg_pre, dg_post, dw_in, dw_out, small


def _pool_mix_fwd(x, hf, g_post, pool_w, pool_scale, tag, nxt):
    pooled = pool_fwd(hf, f"{tag}_pool")
    n_g = len(POOL_WINDOWS)
    ypre = _mm(pooled, pool_w, batch="map", groups=n_g, a_cb=True, o_cb=True, name=f"{tag}_proj")
    m = scale_cols(ypre, pool_scale, f"{tag}_scale")
    return (*close_block(x, m, g_post, 1.0, nxt, tag), (x, pooled, ypre, m))


def _pool_mix_bwd(res, g_pre, g_post, pool_w, pool_scale, dx2, tag):
    x, pooled, ypre, m = res
    n_g = len(POOL_WINDOWS)
    dm, dg_post = rms_bwd(m, g_post, [dx2], 1.0, None, F32, f"{tag}_postnorm_bwd")
    dypre, dscale = scale_cols_bwd(dm, ypre, pool_scale, f"{tag}_scale_bwd")
    dpooled = _mm(dypre, pool_w, tb=True, batch="map", groups=n_g, a_cb=True, o_cb=True, name=f"{tag}_proj_dx")
    dpw = _mm(pooled, dypre, ta=True, batch="map", groups=n_g, a_cb=True, b_cb=True, name=f"{tag}_proj_dw", out_dtype=BF16)
    dhf = pool_bwd(dpooled, f"{tag}_pool_bwd")
    dx, dg_pre = rms_bwd(x, g_pre, [dhf], 1.0, dx2, F32, f"{tag}_prenorm_bwd")
    return dx, dg_pre, dg_post, dpw, jnp.sum(dscale, axis=0, keepdims=True)


def _xattn_fwd(x, h, mem, g_post, g_mem, wq, wk, wv, wo, tag, nxt):
    mem_n = rms_fwd(mem, g_mem, BF16, f"{tag}_memnorm")
    q = _mm(h, wq, out_dtype=BF16, name=f"{tag}_q")
    k = _mm(mem_n, wk, out_dtype=BF16, name=f"{tag}_k")
    v = _mm(mem_n, wv, out_dtype=BF16, name=f"{tag}_v")
    o = xattn_fwd(q, k, v, f"{tag}_core")
    r = _mm(o, wo, out_dtype=BF16, name=f"{tag}_o")
    return (*close_block(x, r, g_post, 1.0, nxt, tag), (x, h, mem_n, q, k, v, o, r))


def _xattn_bwd(res, mem, g_pre, g_post, g_mem, wq, wk, wv, wo, dx2, tag):
    x, h, mem_n, q, k, v, o, r = res
    dr, dg_post = rms_bwd(r, g_post, [dx2], 1.0, None, BF16, f"{tag}_postnorm_bwd")
    do = _mm(dr, wo, tb=True, out_dtype=BF16, name=f"{tag}_o_dx")
    dwo = _mm(o, dr, ta=True, name=f"{tag}_o_dw", out_dtype=BF16)
    dq, dk, dv = xattn_bwd(q, k, v, do, f"{tag}_core_bwd")
    dk, dv = dk.astype(BF16), dv.astype(BF16)
    dwq = _mm(h, dq, ta=True, name=f"{tag}_q_dw", out_dtype=BF16)
    dwk = _mm(mem_n, dk, ta=True, name=f"{tag}_k_dw", out_dtype=BF16)
    dwv = _mm(mem_n, dv, ta=True, name=f"{tag}_v_dw", out_dtype=BF16)
    dh = _mm(dq, wq, tb=True, out_dtype=BF16, name=f"{tag}_q_dx")
    dmem1 = _mm(dk, wk, tb=True, name=f"{tag}_k_dx")
    dmem2 = _mm(dv, wv, tb=True, name=f"{tag}_v_dx")
    _, dg_mem = rms_bwd(mem, g_mem, [dmem1, dmem2], 1.0, None, BF16, f"{tag}_memnorm_bwd")
    dx, dg_pre = rms_bwd(x, g_pre, [dh], 1.0, dx2, F32, f"{tag}_prenorm_bwd")
    return dx, dg_pre, dg_post, dg_mem, dwq, dwk, dwv, dwo


def _rowsum8(part):
    return jnp.sum(part, axis=0, keepdims=True)


def device_step(x, mem, target, norms, mem_norm, small, fetch, emit, early):
    n_tok = x.shape[0]
    tabs = rope_tables(n_tok)
    bias_rows = jnp.broadcast_to(small["sgu_b"][:, :, None], (SGU_GROUPS, BLOCK, LANES))
    gn = lambda l, i: norms[l, i][None, :]
    gm = lambda l: mem_norm[l][None, :]
    mix0 = (small["attn_sinks"], small["sgu_ln_g"], small["sgu_ln_b"], small["sgu_w"], bias_rows, tabs)

    wts, saved = {}, []
    pre_gain = {"ffn1": 0, "mix": 2, "xattn": 4, "ffn2": 6}
    in_dtype = lambda kind, l: F32 if (kind, l) == ("mix", 1) else BF16

    def then(kind, l):
        def nxt(after):
            wts[kind, l], tok = fetch((kind, l), after)
            return gn(l, pre_gain[kind]) + tok, in_dtype(kind, l)
        return nxt

    g_first, _ = then("ffn1", 0)(x)
    h = rms_fwd(x, g_first, BF16, "l0_ffn1_prenorm")
    for l in range(2):
        x, h, r1 = _ffn_fwd(x, h, gn(l, 1), *wts["ffn1", l], f"l{l}_ffn1", then("mix", l))
        if l == 0:
            x, h, r2 = _attn_sgu_fwd(x, h, gn(l, 3), *wts["mix", l], *mix0, f"l{l}_mix", then("xattn", l))
        else:
            x, h, r2 = _pool_mix_fwd(x, h, gn(l, 3), *wts["mix", l], small["pool_scale"], f"l{l}_mix", then("xattn", l))
        x, h, r3 = _xattn_fwd(x, h, mem, gn(l, 5), gm(l), *wts["xattn", l], f"l{l}_xattn", then("ffn2", l))
        x, h, r4 = _ffn_fwd(x, h, gn(l, 7), *wts["ffn2", l], f"l{l}_ffn2", then("ffn1", l + 1) if l == 0 else None)
        saved.append((r1, r2, r3, r4))

    loss_part, dx = loss_and_grad(x, target, "loss")

    g_norm_rows = [[None] * 8, [None] * 8]
    g_mem_rows = [None, None]
    g_small = {}
    tok = 0.0
    for l in (1, 0):
        r1, r2, r3, r4 = saved[l]
        dx, g_norm_rows[l][6], g_norm_rows[l][7], dwg, dwu, dwd = _ffn_bwd(
            r4, gn(l, 6), gn(l, 7) + tok, dx, f"l{l}_ffn2", lambda part, g, l=l: early(("ffn2", l), f"ffn2_{part}", g))
        tok = emit(("ffn2", l), dict(ffn2_wg=dwg, ffn2_wu=dwu, ffn2_wd=dwd), dx)
        dx, g_norm_rows[l][4], g_norm_rows[l][5], g_mem_rows[l], dwq, dwk, dwv, dwo = _xattn_bwd(
            r3, mem, gn(l, 4), gn(l, 5) + tok, gm(l), *wts["xattn", l], dx, f"l{l}_xattn")
        tok = emit(("xattn", l), dict(x_wq=dwq, x_wk=dwk, x_wv=dwv, x_wo=dwo), dx)
        if l == 0:
            dx, g_norm_rows[l][2], g_norm_rows[l][3], dw_in, dw_out, sm = _attn_sgu_bwd(
                r2, gn(l, 2), gn(l, 3) + tok, *wts["mix", l], *mix0, dx, f"l{l}_mix")
            tok = emit(("mix", l), dict(mix_w_in=dw_in, mix_w_out=dw_out), dx)
            g_small.update(sm)
        else:
            dx, g_norm_rows[l][2], g_norm_rows[l][3], dpw, dscale = _pool_mix_bwd(
                r2, gn(l, 2), gn(l, 3) + tok, *wts["mix", l], small["pool_scale"], dx, f"l{l}_mix")
            tok = emit(("mix", l), dict(pool_w=dpw), dx)
            g_small["pool_scale"] = dscale
        dx, g_norm_rows[l][0], g_norm_rows[l][1], dwg, dwu, dwd = _ffn_bwd(
            r1, gn(l, 0), gn(l, 1) + tok, dx, f"l{l}_ffn1", lambda part, g, l=l: early(("ffn1", l), f"ffn1_{part}", g))
        tok = emit(("ffn1", l), dict(ffn1_wg=dwg, ffn1_wu=dwu, ffn1_wd=dwd), dx)
    g_norms = jnp.stack([jnp.concatenate([_rowsum8(p) for p in g_norm_rows[l]], axis=0) for l in range(2)])
    g_mem_norm = jnp.concatenate([_rowsum8(p) for p in g_mem_rows], axis=0)
    return loss_part, dx, g_small, g_norms, g_mem_norm, tok


ANY = pl.BlockSpec(memory_space=pl.ANY)


def _place():
    x, y, c = lax.axis_index("x"), lax.axis_index("y"), lax.axis_index("c")
    other_chips = [(1 - x, y), (x, 1 - y), (1 - x, 1 - y)]
    return x, y, c, other_chips


def _half_rows(core, n_rows):
    half = n_rows // 2
    return pl.ds(pl.multiple_of(core * half, 16), half)


def _remote(src, dst, send_sem, recv_sem, device):
    return pltpu.make_async_remote_copy(src_ref=src, dst_ref=dst, send_sem=send_sem, recv_sem=recv_sem,
                                        device_id=device, device_id_type=MESH)


HBM = pl.BlockSpec(memory_space=pltpu.HBM)
SEM = pl.BlockSpec(memory_space=pltpu.SEMAPHORE)
DATAFLOW = pltpu.SideEffectType.DATAFLOW_SIDE_EFFECTING


def _chip_copies(bufs, send_sems, recv_sems):
    x, y, c, chips = _place()
    me = 2 * x + y
    sends, arrivals = [], []
    for i, buf in enumerate(bufs):
        rows = _half_rows(c, buf.shape[1])
        for r, (px, py) in enumerate(chips):
            mine, theirs = buf.at[me, rows], buf.at[2 * px + py, rows]
            sends.append(_remote(mine, mine, send_sems.at[3 * i + r], recv_sems.at[3 * i + r], (px, py, c)))
            arrivals.append(_remote(theirs, theirs, send_sems.at[3 * i + r], recv_sems.at[3 * i + r], (px, py, c)))
    return sends, arrivals


def copies_start(arrays, fresh, copies, n_sems, after, name):
    operands = [pltpu.with_memory_space_constraint(a, pltpu.HBM) for a in arrays]
    operands += [pltpu.with_memory_space_constraint(lax.empty(f.shape, f.dtype), pltpu.HBM) for f in fresh]
    n = len(operands)

    def body(*refs):
        send_sems, recv_sems = refs[n + 1], refs[n + 2]
        bufs, token = refs[n + 3:2 * n + 3], refs[2 * n + 3]
        for cp in copies(bufs, send_sems, recv_sems)[0]:
            cp.start()
        token[...] = jnp.zeros_like(token)

    res = pl.pallas_call(
        body, name=name, in_specs=[HBM] * n + [ANY],
        out_specs=[SEM, SEM] + [HBM] * n + [pl.BlockSpec(memory_space=pltpu.VMEM)],
        out_shape=[pltpu.SemaphoreType.DMA((n_sems,)), pltpu.SemaphoreType.DMA((n_sems,))]
        + [pltpu.HBM(o.shape, o.dtype) for o in operands] + [jax.ShapeDtypeStruct((8, LANES), F32)],
        input_output_aliases={i: 2 + i for i in range(n)}, compiler_params=pltpu.CompilerParams(has_side_effects=DATAFLOW),
    )(*operands, after)
    return res[0], res[1], list(res[2:2 + n]), res[2 + n]


def copies_wait(send_sems, recv_sems, bufs, copies, after, name):
    n = len(bufs)

    def body(*refs):
        for cp in copies(refs[:n], refs[n], refs[n + 1])[1]:
            cp.wait_send()
            cp.wait_recv()

    return list(pl.pallas_call(
        body, name=name, in_specs=[HBM] * n + [SEM, SEM] + [ANY] * len(after), out_specs=[HBM] * n,
        out_shape=[pltpu.HBM(b.shape, b.dtype) for b in bufs], input_output_aliases={i: i for i in range(n)},
        compiler_params=pltpu.CompilerParams(has_side_effects=DATAFLOW),
    )(*bufs, send_sems, recv_sems, *after))


def _to_sibling_copies(n_src, src_rows):
    def copies(bufs, send_sems, recv_sems):
        x, y, c, _ = _place()
        sends, arrivals = [], []
        for i in range(n_src):
            src, land = bufs[i], bufs[n_src + i]
            src = src.at[:, src_rows(c, src)] if src_rows is not None else src
            sends.append(_remote(src, land, send_sems.at[i], recv_sems.at[i], (x, y, 1 - c)))
            arrivals.append(_remote(land, land, send_sems.at[i], recv_sems.at[i], (x, y, 1 - c)))
        return sends, arrivals
    return copies


def _forward_copies(bufs, send_sems, recv_sems):
    x, y, c, chips = _place()
    sends, arrivals = [], []
    for i, buf in enumerate(bufs):
        for r, (px, py) in enumerate(chips):
            mine = buf.at[2 * px + py, _half_rows(c, buf.shape[1])]
            theirs = buf.at[2 * px + py, _half_rows(1 - c, buf.shape[1])]
            sends.append(_remote(mine, mine, send_sems.at[3 * i + r], recv_sems.at[3 * i + r], (x, y, 1 - c)))
            arrivals.append(_remote(theirs, theirs, send_sems.at[3 * i + r], recv_sems.at[3 * i + r], (x, y, 1 - c)))
    return sends, arrivals


def forward_to_sibling(bufs, name):
    n = len(bufs)

    def body(*refs):
        sends, arrivals = _forward_copies(refs[n:2 * n], *refs[2 * n:])
        for cp in sends:
            cp.start()
        for cp in arrivals:
            cp.wait_recv()
        for cp in sends:
            cp.wait_send()

    return pl.pallas_call(
        body, name=name, in_specs=[ANY] * n, out_specs=[ANY] * n, input_output_aliases={i: i for i in range(n)},
        out_shape=[jax.ShapeDtypeStruct(b.shape, b.dtype) for b in bufs],
        scratch_shapes=[pltpu.SemaphoreType.DMA((3 * n,)), pltpu.SemaphoreType.DMA((3 * n,))],
    )(*bufs)


def add_own_half(g, p, core, name):
    n_j, n_rows, n_cols = g.shape
    half = n_rows // 2
    tr = _rows(half, n_cols * (2 * g.dtype.itemsize + 2), WEIGHT_TILE_BYTES)

    def body(c_ref, g_ref, p_ref, o_ref):
        o_ref[...] = (g_ref[...] + p_ref[...]).astype(o_ref.dtype)

    grid_spec = pltpu.PrefetchScalarGridSpec(
        num_scalar_prefetch=1, grid=(n_j, half // tr),
        in_specs=[pl.BlockSpec((None, None, tr, n_cols), lambda j, i, c_ref: (j, c_ref[0], i, 0)),
                  pl.BlockSpec((None, tr, n_cols), lambda j, i, c_ref: (j, i, 0))],
        out_specs=pl.BlockSpec((None, tr, n_cols), lambda j, i, c_ref: (j, i, 0)))
    return pl.pallas_call(
        body, name=name, grid_spec=grid_spec, out_shape=jax.ShapeDtypeStruct((n_j, half, n_cols), BF16),
        compiler_params=_params(("parallel", "parallel")),
    )(core, g.reshape(n_j, 2, half, n_cols), p)


def _scatter_copies(n_parts):
    def copies(bufs, send_sems, recv_sems):
        x, y, c, chips = _place()
        me = 2 * x + y
        sends, arrivals = [], []
        for i in range(n_parts):
            part, land = bufs[i], bufs[n_parts + i]
            for r, (px, py) in enumerate(chips):
                theirs = land.at[2 * px + py]
                sends.append(_remote(part.at[2 * px + py], land.at[me], send_sems.at[3 * i + r], recv_sems.at[3 * i + r], (px, py, c)))
                arrivals.append(_remote(theirs, theirs, send_sems.at[3 * i + r], recv_sems.at[3 * i + r], (px, py, c)))
        return sends, arrivals
    return copies


def sum_over_chips(own, got, chip, name):
    n_s, n_rows, n_cols = got.shape
    tr = _rows(n_rows, n_cols * (got.dtype.itemsize * (n_s + 1) + 4), WEIGHT_TILE_BYTES)

    def body(chip_ref, own_ref, *refs):
        got_refs, o_ref = refs[:n_s], refs[n_s]
        me = chip_ref[0]
        acc = jnp.where(me == 0, own_ref[...], got_refs[0][...]).astype(F32)
        for k in range(1, n_s):
            acc = acc + jnp.where(me == k, own_ref[...], got_refs[k][...]).astype(F32)
        o_ref[...] = acc

    def slot(k):
        return pl.BlockSpec((None, tr, n_cols), lambda i, chip_ref: (jnp.where(chip_ref[0] == k, (k + 1) % n_s, k), i, 0))

    grid_spec = pltpu.PrefetchScalarGridSpec(
        num_scalar_prefetch=1, grid=(n_rows // tr,),
        in_specs=[pl.BlockSpec((None, tr, n_cols), lambda i, chip_ref: (chip_ref[0], i, 0))] + [slot(k) for k in range(n_s)],
        out_specs=pl.BlockSpec((tr, n_cols), lambda i, chip_ref: (i, 0)))
    return pl.pallas_call(
        body, name=name, grid_spec=grid_spec, out_shape=jax.ShapeDtypeStruct((n_rows, n_cols), F32),
        compiler_params=_params(("parallel",)),
    )(chip, own, *([got] * n_s))


def sum_slots(q, name):
    n_s, n_rows, n_cols = q.shape
    tr = _rows(n_rows, n_cols * (q.dtype.itemsize * n_s + 4))

    def body(q_ref, o_ref):
        acc = q_ref[0].astype(F32)
        for s in range(1, n_s):
            acc = acc + q_ref[s].astype(F32)
        o_ref[...] = acc

    return pl.pallas_call(
        body, name=name, grid=(n_rows // tr,), in_specs=[pl.BlockSpec((n_s, tr, n_cols), lambda i: (0, i, 0))],
        out_specs=pl.BlockSpec((tr, n_cols), lambda i: (i, 0)), out_shape=jax.ShapeDtypeStruct((n_rows, n_cols), F32),
        compiler_params=_params(("parallel",)),
    )(q)


def gather_all_devices(s, name):
    def body(s_ref, o_ref, send_sems, recv_sems, local_sem):
        x, y, c, _ = _place()
        me = 4 * x + 2 * y + c
        local = pltpu.make_async_copy(s_ref, o_ref.at[me], local_sem)
        local.start()
        sends = []
        for f in range(1, N_DEV):
            px, py, pc = x ^ (f >> 2), y ^ ((f >> 1) & 1), c ^ (f & 1)
            sends.append(_remote(s_ref, o_ref.at[me], send_sems.at[f - 1], recv_sems.at[f - 1], (px, py, pc)))
            sends[-1].start()
        for f in range(1, N_DEV):
            px, py, pc = x ^ (f >> 2), y ^ ((f >> 1) & 1), c ^ (f & 1)
            landed = o_ref.at[4 * px + 2 * py + pc]
            _remote(landed, landed, send_sems.at[f - 1], recv_sems.at[f - 1], (px, py, pc)).wait_recv()
        for cp in sends:
            cp.wait_send()
        local.wait()

    return pl.pallas_call(
        body, name=name, in_specs=[ANY], out_specs=ANY, out_shape=jax.ShapeDtypeStruct((N_DEV,) + s.shape, s.dtype),
        scratch_shapes=[pltpu.SemaphoreType.DMA((N_DEV - 1,)), pltpu.SemaphoreType.DMA((N_DEV - 1,)), pltpu.SemaphoreType.DMA],
    )(s)


FFN_NAMES = ("ffn1_wg", "ffn1_wu", "ffn1_wd", "ffn2_wg", "ffn2_wu", "ffn2_wd")
XATTN_NAMES = ("x_wq", "x_wk", "x_wv", "x_wo")
BIG_NAMES = FFN_NAMES + XATTN_NAMES + ("mix_w_in", "mix_w_out", "pool_w")
SMALL_NAMES = ("norms", "mem_norm", "attn_sinks", "sgu_ln_g", "sgu_ln_b", "sgu_w", "sgu_b", "pool_scale")
WEIGHT_ORDER = ("norms", "mem_norm") + BIG_NAMES[:-1] + ("attn_sinks", "sgu_ln_g", "sgu_ln_b", "sgu_w", "sgu_b", "pool_w", "pool_scale")
COLUMN_CUT = ("x_wo", "mix_w_in")
N_POOL = len(POOL_WINDOWS)
BLOCK_ORDER = (("ffn1", 0), ("mix", 0), ("xattn", 0), ("ffn2", 0), ("ffn1", 1), ("mix", 1), ("xattn", 1), ("ffn2", 1))
PREFETCH_AT = ((1, 2, 3), (), (), (4,), (5, 6, 7), (), (), ())


def block_weight_names(kind, layer):
    if kind == "mix":
        return ("mix_w_in", "mix_w_out") if layer == 0 else ("pool_w",)
    return XATTN_NAMES if kind == "xattn" else tuple(f"{kind}_{part}" for part in ("wg", "wu", "wd"))


def _to_matmul_layout(name, g):
    n_j, n_rows, n_cols = g.shape
    if name in COLUMN_CUT:
        return g.transpose(1, 0, 2).reshape(n_rows, n_j * n_cols)
    if name == "pool_w":
        return g.reshape(n_j, N_POOL, n_rows // N_POOL, n_cols).transpose(1, 0, 2, 3).reshape(N_POOL, n_j * n_rows // N_POOL, n_cols)
    if name in ("x_wq", "x_wk", "x_wv", "mix_w_out"):
        return g.reshape(n_j * n_rows, n_cols)
    return g


def _from_matmul_layout(name, d):
    if name in COLUMN_CUT:
        n_rows, wide = d.shape
        return d.reshape(n_rows, N_CHIPS, wide // N_CHIPS).transpose(1, 0, 2)
    if name == "pool_w":
        n_g, n_in, n_cols = d.shape
        return d.reshape(n_g, N_CHIPS, n_in // N_CHIPS, n_cols).transpose(1, 0, 2, 3).reshape(N_CHIPS, n_g * n_in // N_CHIPS, n_cols)
    if name in ("x_wq", "x_wk", "x_wv", "mix_w_out"):
        return d.reshape(N_CHIPS, d.shape[0] // N_CHIPS, d.shape[1])
    return d


def _as3(w):
    return w.reshape(w.shape[0], -1, w.shape[-1])


def _pack(arrays, row_multiple):
    flat = jnp.concatenate([a.reshape(-1) for a in arrays])
    per = LANES * row_multiple
    total = -(-flat.shape[0] // per) * per
    return jnp.pad(flat, (0, total - flat.shape[0])).reshape(total // LANES, LANES)


def _unpack(packed, like):
    flat, out, at = packed.reshape(-1), [], 0
    for a in like:
        out.append(flat[at:at + a.size].reshape(a.shape))
        at += a.size
    return out


def kernel(x, mem, norms, mem_norm, ffn1_wg, ffn1_wu, ffn1_wd, ffn2_wg, ffn2_wu, ffn2_wd, x_wq, x_wk, x_wv, x_wo, mix_w_in, mix_w_out, attn_sinks, sgu_ln_g, sgu_ln_b, sgu_w, sgu_b, pool_w, pool_scale, loss_target, m_norms, m_mem_norm, m_ffn1_wg, m_ffn1_wu, m_ffn1_wd, m_ffn2_wg, m_ffn2_wu, m_ffn2_wd, m_x_wq, m_x_wk, m_x_wv, m_x_wo, m_mix_w_in, m_mix_w_out, m_attn_sinks, m_sgu_ln_g, m_sgu_ln_b, m_sgu_w, m_sgu_b, m_pool_w, m_pool_scale, v_norms, v_mem_norm, v_ffn1_wg, v_ffn1_wu, v_ffn1_wd, v_ffn2_wg, v_ffn2_wu, v_ffn2_wd, v_x_wq, v_x_wk, v_x_wv, v_x_wo, v_mix_w_in, v_mix_w_out, v_attn_sinks, v_sgu_ln_g, v_sgu_ln_b, v_sgu_w, v_sgu_b, v_pool_w, v_pool_scale):
    given = dict(locals())
    w = {n: given[n] for n in WEIGHT_ORDER}
    mom = {n: given["m_" + n] for n in WEIGHT_ORDER}
    var = {n: given["v_" + n] for n in WEIGHT_ORDER}
    chip_id = 2 * lax.axis_index("x") + lax.axis_index("y")
    chip = chip_id.astype(jnp.int32).reshape(1)
    core = lax.axis_index("c").astype(jnp.int32).reshape(1)
    n_shard = norms.shape[-1]

    keys = [(name, l) for name in BIG_NAMES for l in range(_as3(w[name]).shape[0])]
    first_keys = [(name, 0) for name in block_weight_names(*BLOCK_ORDER[0])]
    cast = lambda name, l: cast_into_slot(_as3(w[name]), l, chip, f"cast_{name}")
    slot_of = {key: cast(*key) for key in first_keys}
    small_rows = jnp.concatenate([norms.reshape(-1, n_shard), pool_scale, jnp.zeros((15, n_shard), F32)], axis=0)
    small_slot = lax.dynamic_update_slice_in_dim(jnp.zeros((N_CHIPS,) + small_rows.shape, F32), small_rows[None], chip_id, axis=0)
    pending = {}
    gather_start = lambda slots, after, name: copies_start(slots, [], _chip_copies, 3 * len(slots), after, name)
    pending["0a"] = gather_start([slot_of[first_keys[0]], small_slot], chip, "gather_start_0a")
    pending["0b"] = gather_start([slot_of[first_keys[1]]], pending["0a"][2][0], "gather_start_0b")
    pending["0c"] = gather_start([slot_of[first_keys[2]]], pending["0b"][2][0], "gather_start_0c")
    for key in keys:
        if key not in slot_of:
            slot_of[key] = cast(*key)


    def block_slots(k):
        kind, layer = BLOCK_ORDER[k]
        return [slot_of[name, layer if kind != "mix" else 0] for name in block_weight_names(kind, layer)]

    def start(k, after):
        pending[k] = gather_start(block_slots(k), after, f"gather_start_{k}")

    def finish(k, after):
        send_sems, recv_sems, bufs, _ = pending[k]
        return forward_to_sibling(copies_wait(send_sems, recv_sems, bufs, _chip_copies, [after], f"gather_wait_{k}"), f"gather_forward_{k}")

    def fetch(key, after):
        k = BLOCK_ORDER.index(key)
        names = block_weight_names(*key)
        token = 0.0
        if k == 0:
            late = lambda part, name: lambda after: _to_matmul_layout(name, finish(part, after)[0])
            bufs = [first_gate]
            weights = (_to_matmul_layout(names[0], first_gate), late("0b", names[1]), late("0c", names[2]))
        elif key[0] in ("ffn1", "ffn2"):
            send_sems, recv_sems, bufs, _ = pending[k]
            bufs = copies_wait(send_sems, recv_sems, bufs, _chip_copies, [after], f"gather_wait_{k}")
            gate, = forward_to_sibling(bufs[:1], f"gather_forward_{k}_gate")
            send_sems, recv_sems, rest, begun = copies_start(bufs[1:], [], _forward_copies, 6, gate, f"gather_forward_{k}_start")
            landed = []

            def late(i, name):
                def get(after):
                    if not landed:
                        landed.extend(copies_wait(send_sems, recv_sems, rest, _forward_copies, [after], f"gather_forward_{k}_wait"))
                    return _to_matmul_layout(name, landed[i])
                return get
            bufs = [gate]
            weights = (_to_matmul_layout(names[0], gate), late(0, names[1]), late(1, names[2]))
            token = begun[0, 0]
        else:
            bufs = finish(k, after)
            weights = tuple(_to_matmul_layout(name, b) for name, b in zip(names, bufs))
        for ahead in PREFETCH_AT[k]:
            start(ahead, bufs[0])
            token = token + pending[ahead][3][0, 0]
        return weights, token

    to_sibling, to_chips, halves, own_of, recv_of = [], [], [], {}, {}
    other_half = lambda c, src: _half_rows(1 - c, src.shape[1])

    def sibling_start(tag, names, full, after):
        lands = [jax.ShapeDtypeStruct((g.shape[0], g.shape[1] // 2, g.shape[2]), g.dtype) for g in full]
        send_sems, recv_sems, bufs, token = copies_start(full, lands, _to_sibling_copies(len(full), other_half), len(full), after,
                                                         f"grads_to_sibling_{tag}")
        to_sibling.append((tag, names, send_sems, recv_sems, bufs))
        return token

    def early(key, name, g):
        kind, layer = key
        return sibling_start(f"{name}{layer}", [(name, layer)], [_from_matmul_layout(name, g)], chip)

    def chips_start(tag, after):
        names, chip_sums = [], []
        for part, part_names, send_sems, recv_sems, bufs in to_sibling:
            n = len(part_names)
            bufs = copies_wait(send_sems, recv_sems, bufs, _to_sibling_copies(n, other_half), after, f"grads_from_sibling_{part}")
            chip_sums += [add_own_half(g, p, core, f"chip_sum_{name}") for (name, _), g, p in zip(part_names, bufs[:n], bufs[n:])]
            names += part_names
        to_sibling.clear()
        n = len(names)
        send_sems, recv_sems, bufs, token = copies_start(chip_sums, chip_sums, _scatter_copies(n), 3 * n, after[0], f"grads_start_{tag}")
        to_chips.append((tag, names, send_sems, recv_sems, bufs))
        return token[0, 0]

    def chips_wait(after):
        tag, names, send_sems, recv_sems, bufs = to_chips.pop()
        return tag, names, copies_wait(send_sems, recv_sems, bufs, _scatter_copies(len(names)), after, f"grads_wait_{tag}")

    def reduce_landed(tag, names, bufs):
        n = len(names)
        own = [sum_over_chips(t, q, chip, f"sum_{name}") for (name, _), t, q in zip(names, bufs[:n], bufs[n:])]
        send_sems, recv_sems, bufs, _ = copies_start(own, own, _to_sibling_copies(n, None), n, chip, f"grads_halves_{tag}")
        halves.append((tag, names, send_sems, recv_sems, bufs))

    def halves_land(after):
        for tag, names, send_sems, recv_sems, bufs in halves:
            n = len(names)
            bufs = copies_wait(send_sems, recv_sems, bufs, _to_sibling_copies(n, None), after, f"grads_halves_wait_{tag}")
            for key, o, r in zip(names, bufs[:n], bufs[n:]):
                own_of[key], recv_of[key] = o, r
        halves.clear()

    def emit(key, grads_of, after):
        kind, layer = key
        landed = chips_wait([after]) if to_chips else None
        begun = {name for _, part_names, _, _, _ in to_sibling for name, _ in part_names}
        rest = {name: g for name, g in grads_of.items() if name not in begun}
        if rest:
            sibling_start(f"{kind}{layer}", [(name, layer if kind != "mix" else 0) for name in rest],
                          [_from_matmul_layout(name, g) for name, g in rest.items()], after)
        token = chips_start(f"{kind}{layer}", [after])
        if landed:
            reduce_landed(*landed)
        return token

    first_gate, small_all = finish("0a", pending["0c"][2][0])
    n_norm_rows = norms.shape[0] * norms.shape[1]
    norms_all = jnp.concatenate([small_all[j, :n_norm_rows].reshape(norms.shape) for j in range(N_CHIPS)], axis=-1)
    pool_scale_all = jnp.concatenate([small_all[j, n_norm_rows:n_norm_rows + 1] for j in range(N_CHIPS)], axis=-1)
    small = dict(attn_sinks=attn_sinks[0], sgu_ln_g=sgu_ln_g, sgu_ln_b=sgu_ln_b, sgu_w=sgu_w[0], sgu_b=sgu_b[0], pool_scale=pool_scale_all)

    loss_part, dx, g_small, g_norms, g_mem_norm, last_token = device_step(
        x[0], mem[0], loss_target[0], norms_all, mem_norm, small, fetch, emit, early)
    loss = lax.psum(0.5 * jnp.sum(loss_part) / x.shape[-1], ("x", "y", "c"))
    last_names = {name for name, _ in to_chips[0][1]}
    order = last_token.reshape(1, 1)
    halves_land([dx])

    grads, delta, new_m, new_v = {}, {}, {}, {}

    def update(name):
        n_l = _as3(w[name]).shape[0]
        res = adamw_from_halves(_as3(w[name]), [own_of[name, l] for l in range(n_l)], [recv_of[name, l] for l in range(n_l)],
                                _as3(mom[name]), _as3(var[name]), core, order, f"adamw_{name}")
        grads[name], delta[name], new_m[name], new_v[name] = (t.reshape(w[name].shape) for t in res)

    for name in BIG_NAMES:
        if name not in last_names:
            update(name)
    done = [delta[name] for name in BIG_NAMES if name not in last_names]
    reduce_landed(*chips_wait(done))
    halves_land(done)
    for name in BIG_NAMES:
        if name in last_names:
            update(name)

    small_g = [g_norms, g_mem_norm, g_small["attn_sinks"], g_small["sgu_ln_g"], g_small["sgu_ln_b"], g_small["sgu_w"], g_small["sgu_b"],
               g_small["pool_scale"]]
    packed = _pack(small_g, 16)
    summed = sum_slots(gather_all_devices(packed, "small_grads_all"), "small_grads_sum")
    s_norms, s_mem, s_sinks, s_lg, s_lb, s_w, s_b, s_scale = _unpack(summed, small_g)
    grads["norms"] = lax.dynamic_slice_in_dim(s_norms, chip_id * n_shard, n_shard, axis=2)
    grads["pool_scale"] = lax.dynamic_slice_in_dim(s_scale, chip_id * n_shard, n_shard, axis=1)
    grads.update(mem_norm=s_mem, attn_sinks=s_sinks, sgu_ln_g=s_lg, sgu_ln_b=s_lb, sgu_w=s_w, sgu_b=s_b)
    like = [w[n] for n in SMALL_NAMES]
    packs = [_pack([src[n] for n in SMALL_NAMES], 16)[None] for src in (w, grads, mom, var)]
    for dst, t in zip((delta, new_m, new_v), adamw(*packs, "adamw_small")):
        for n, a in zip(SMALL_NAMES, _unpack(t[0], like)):
            dst[n] = a

    outs = [loss, dx[None]]
    for group in (grads, delta, new_m, new_v):
        outs += [group[n] for n in WEIGHT_ORDER]
    return tuple(outs)
```
